```python
import math
import jax, jax.numpy as jnp
from jax import lax
import numpy as np

D_MODEL = 1024
BATCH = 8
SEQ = 16384
DEPTH = 4

N_MIXERS = 3
LAYER_MIXER = tuple(i % N_MIXERS for i in range(DEPTH))
N_S5 = LAYER_MIXER.count(0)
N_HG = LAYER_MIXER.count(1)
N_MLA = LAYER_MIXER.count(2)
MIX_WIDTH = D_MODEL
S5_GROUP = 16
S5_GROUPS = MIX_WIDTH // S5_GROUP
S5_STATE = 64
S5_DT_MIN = 0.001
S5_DT_MAX = 0.1
S5_BLOCK = 1024
HG_HEAD_DIM = 128
HG_HEADS = MIX_WIDTH // HG_HEAD_DIM
HG_CHUNK = 64
MLA_HEADS = 8
MLA_NOPE = 128
MLA_ROPE = 64
MLA_V = 128
MLA_Q_LORA = 384
MLA_KV_LORA = 256
ROPE_THETA = 10000.0
ATTN_BLOCK = 128
MAX_POS_OFFSET = 4096
FFN_HIDDEN = -(-8 * D_MODEL // (3 * 256)) * 256
DEEPNORM_ALPHA = (2 * DEPTH) ** 0.25
DEEPNORM_BETA = (8 * DEPTH) ** -0.25
LN_EPS = 1e-5
RMS_EPS = 1e-6

kernel_name = 'hybrid_s5_hgrn2_mla_deepnorm_encoder'


def layer_norm(x, g, b):
    xf = x.astype(jnp.float32)
    mu = jnp.mean(xf, axis=-1, keepdims=True)
    var = jnp.mean(jnp.square(xf - mu), axis=-1, keepdims=True)
    return ((xf - mu) * lax.rsqrt(var + LN_EPS) * g + b).astype(x.dtype)


def rms_norm(x, g):
    xf = x.astype(jnp.float32)
    return (xf * lax.rsqrt(jnp.mean(xf * xf, axis=-1, keepdims=True) + RMS_EPS) * g).astype(x.dtype)


def swiglu(x, w_in, w_out):
    gate, up = jnp.split(x @ w_in, 2, axis=-1)
    return (jax.nn.silu(gate) * up) @ w_out


def s5_direction(u, lam_re, lam_im, log_step, b_re, b_im, c_re, c_im):
    seq, bsz = u.shape[0], u.shape[1]
    blk = math.gcd(seq, S5_BLOCK)
    f32 = jnp.float32
    lam = lax.complex(lam_re.astype(f32), lam_im.astype(f32))
    step = jnp.exp(log_step.astype(f32))[:, None]
    lam_dt = lam * step
    lam_bar = jnp.exp(lam_dt)
    b = lax.complex(b_re.astype(f32), b_im.astype(f32))
    b_bar = ((lam_bar - 1.0) / lam)[..., None] * b
    c = lax.complex(c_re.astype(f32), c_im.astype(f32))
    powers = jnp.exp(lam_dt[None] * jnp.arange(1, blk + 1, dtype=f32)[:, None, None])
    a_blk = jnp.broadcast_to(lam_bar, (blk, 1) + lam_bar.shape)
    u_blocks = u.astype(f32).reshape(seq // blk, blk, bsz, S5_GROUPS, S5_GROUP)

    def combine(left, right):
        a_l, b_l = left
        a_r, b_r = right
        return a_r * a_l, a_r * b_l + b_r

    def step_fn(h, u_blk):
        bu = jnp.einsum('lbgh,gph->lbgp', u_blk.astype(jnp.complex64), b_bar)
        _, h_loc = lax.associative_scan(combine, (a_blk, bu), axis=0)
        h_all = h_loc + powers[:, None] * h[None]
        y = jnp.einsum('ghp,lbgp->lbgh', c, h_all).real
        return h_all[-1], y

    h0 = jnp.zeros((bsz, S5_GROUPS, S5_STATE), jnp.complex64)
    _, y = lax.scan(step_fn, h0, u_blocks)
    return y.reshape(seq, bsz, S5_GROUPS, S5_GROUP)


def s5_mixer(x, w_in, lam_re, lam_im, log_step, b_re, b_im, c_re, c_im, d_skip, w_glu, b_glu, w_out):
    bsz, seq, _ = x.shape
    u = x @ w_in
    u_s = jnp.swapaxes(u, 0, 1).reshape(seq, bsz, S5_GROUPS, S5_GROUP)
    y_fw = s5_direction(u_s, lam_re[0], lam_im[0], log_step[0], b_re[0], b_im[0], c_re[0], c_im[0])
    y_bw = jnp.flip(s5_direction(jnp.flip(u_s, 0), lam_re[1], lam_im[1], log_step[1],
                                 b_re[1], b_im[1], c_re[1], c_im[1]), 0)
    y = jnp.swapaxes((y_fw + y_bw).reshape(seq, bsz, MIX_WIDTH), 0, 1).astype(x.dtype) + d_skip * u
    y = jax.nn.gelu(y)
    y = y * jax.nn.sigmoid(y @ w_glu + b_glu)
    return y @ w_out


def chunked_gated_recurrence(q, k, v, log_f):
    n, h, seq, dk = q.shape
    dv = v.shape[-1]
    nc = seq // HG_CHUNK

    def to_chunks(t):
        return jnp.moveaxis(t.reshape(n, h, nc, HG_CHUNK, t.shape[-1]), 2, 0)

    lower = jnp.tril(jnp.ones((HG_CHUNK, HG_CHUNK), bool))[:, :, None]

    def step(state, inp):
        qc, kc, vc, lfc = inp
        b = jnp.cumsum(lfc, axis=2)
        b_last = b[:, :, -1:, :]
        o_inter = jnp.einsum('nhtk,nhkv->nhtv', qc * jnp.exp(b), state)
        diff = b[:, :, :, None, :] - b[:, :, None, :, :]
        decay = jnp.where(lower, jnp.exp(jnp.where(lower, diff, 0.0)), 0.0)
        scores = jnp.einsum('nhtk,nhsk,nhtsk->nhts', qc, kc, decay)
        o_intra = jnp.einsum('nhts,nhsv->nhtv', scores, vc)
        new_state = (jnp.exp(b_last[:, :, 0, :])[..., None] * state
                     + jnp.einsum('nhsk,nhsv->nhkv', kc * jnp.exp(b_last - b), vc))
        return new_state, o_inter + o_intra

    s0 = jnp.zeros((n, h, dk, dv), jnp.float32)
    _, o = lax.scan(step, s0, (to_chunks(q), to_chunks(k), to_chunks(v), to_chunks(log_f)))
    return jnp.moveaxis(o, 0, 2).reshape(n, h, seq, dv)


def hgrn2_mixer(x, w_in, lower_bound, norm_g, w_out):
    bsz, seq, _ = x.shape
    q, f_fw, f_bw, i_in, g = jnp.split(x @ w_in, 5, axis=-1)

    def heads(t):
        return t.reshape(bsz, seq, HG_HEADS, HG_HEAD_DIM).transpose(0, 2, 1, 3).astype(jnp.float32)

    q = jax.nn.silu(heads(q))
    v = heads(i_in)
    lb = lower_bound.reshape(HG_HEADS, 1, HG_HEAD_DIM).astype(jnp.float32)

    def gates(f):
        z = heads(f)
        log_f = jnp.logaddexp(jnp.log(lb), jnp.log1p(-lb) + jax.nn.log_sigmoid(z))
        k = (1.0 - lb) * jax.nn.sigmoid(-z)
        return log_f, k

    lf_fw, k_fw = gates(f_fw)
    lf_bw, k_bw = gates(f_bw)
    flip = lambda t: jnp.flip(t, axis=2)
    q2 = jnp.concatenate([q, flip(q)], axis=0)
    k2 = jnp.concatenate([k_fw, flip(k_bw)], axis=0)
    v2 = jnp.concatenate([v, flip(v)], axis=0)
    lf2 = jnp.concatenate([lf_fw, flip(lf_bw)], axis=0)
    o = chunked_gated_recurrence(q2, k2, v2, lf2)
    o = o[:bsz] + flip(o[bsz:])
    o = rms_norm(o, norm_g)
    o = o.transpose(0, 2, 1, 3).reshape(bsz, seq, MIX_WIDTH).astype(x.dtype)
    o = o * jax.nn.silu(g)
    return o @ w_out


def rope_cos_sin(positions):
    half = MLA_ROPE // 2
    inv_freq = 1.0 / (ROPE_THETA ** (jnp.arange(half, dtype=jnp.float32) * (2.0 / MLA_ROPE)))
    ang = positions.astype(jnp.float32)[..., None] * inv_freq
    return jnp.cos(ang), jnp.sin(ang)


def apply_rope(t, cos, sin):
    half = t.shape[-1] // 2
    t1, t2 = t[..., :half], t[..., half:]
    return jnp.concatenate([t1 * cos - t2 * sin, t1 * sin + t2 * cos], axis=-1).astype(t.dtype)


def dense_attention(q_nope, q_pe, k_nope, k_pe, v):
    bsz, seq, h, _ = q_nope.shape
    nb = seq // ATTN_BLOCK
    qn_b = jnp.moveaxis(q_nope.reshape(bsz, nb, ATTN_BLOCK, h, MLA_NOPE), 1, 0)
    qp_b = jnp.moveaxis(q_pe.reshape(bsz, nb, ATTN_BLOCK, h, MLA_ROPE), 1, 0)

    def one_block(blk):
        qn, qp = blk
        s = (jnp.einsum('bqhd,bkhd->bhqk', qn, k_nope, preferred_element_type=jnp.float32)
             + jnp.einsum('bqhr,bkr->bhqk', qp, k_pe, preferred_element_type=jnp.float32))
        p = jax.nn.softmax(s, axis=-1).astype(v.dtype)
        return jnp.einsum('bhqk,bkhd->bqhd', p, v)

    o = lax.map(one_block, (qn_b, qp_b))
    return jnp.moveaxis(o, 0, 1).reshape(bsz, seq, h, MLA_V)


def mla_mixer(x, positions, w_in, q_norm_g, w_q_b, kv_norm_g, w_kv_b, w_out):
    bsz, seq, _ = x.shape
    q_lat, kv_lat, k_rope = jnp.split(x @ w_in, [MLA_Q_LORA, MLA_Q_LORA + MLA_KV_LORA], axis=-1)
    q = (rms_norm(q_lat, q_norm_g) @ w_q_b).reshape(bsz, seq, MLA_HEADS, MLA_NOPE + MLA_ROPE)
    kv = (rms_norm(kv_lat, kv_norm_g) @ w_kv_b).reshape(bsz, seq, MLA_HEADS, MLA_NOPE + MLA_V)
    cos, sin = rope_cos_sin(positions)
    scale = (MLA_NOPE + MLA_ROPE) ** -0.5
    q_nope = q[..., :MLA_NOPE] * scale
    q_pe = apply_rope(q[..., MLA_NOPE:], cos[:, :, None, :], sin[:, :, None, :]) * scale
    k_pe = apply_rope(k_rope, cos, sin)
    k_nope, v = kv[..., :MLA_NOPE], kv[..., MLA_NOPE:]
    o = dense_attention(q_nope, q_pe, k_nope, k_pe, v)
    return o.reshape(bsz, seq, MLA_HEADS * MLA_V) @ w_out


def _fwd_setup_inputs(seed: int = 0) -> dict:
    key = jax.random.key(seed)
    ks = iter(jax.random.split(key, 40))
    f32 = jnp.float32
    d, w = D_MODEL, MIX_WIDTH
    g, p, hs = S5_GROUPS, S5_STATE, S5_GROUP

    def nrm(shape, scale):
        return jax.random.normal(next(ks), shape, f32) * scale

    inp = {}
    inp['x'] = nrm((BATCH, SEQ, d), 1.0)
    inp['positions'] = (jnp.arange(SEQ, dtype=jnp.int32)[None, :]
                        + jax.random.randint(next(ks), (BATCH, 1), 0, MAX_POS_OFFSET, dtype=jnp.int32))
    inp['ln_mix_g'] = 1.0 + nrm((DEPTH, d), 0.02)
    inp['ln_mix_b'] = nrm((DEPTH, d), 0.01)
    inp['ln_ffn_g'] = 1.0 + nrm((DEPTH, d), 0.02)
    inp['ln_ffn_b'] = nrm((DEPTH, d), 0.01)
    inp['ffn_w_in'] = nrm((DEPTH, d, 2 * FFN_HIDDEN), d ** -0.5)
    inp['ffn_w_out'] = nrm((DEPTH, FFN_HIDDEN, d), DEEPNORM_BETA * FFN_HIDDEN ** -0.5)
    inp['s5_w_in'] = nrm((N_S5, d, w), d ** -0.5)
    inp['s5_lam_re'] = -0.5 + nrm((N_S5, 2, g, p), 0.01)
    inp['s5_lam_im'] = jnp.pi * jnp.arange(p, dtype=f32) + nrm((N_S5, 2, g, p), 0.01)
    inp['s5_log_step'] = jax.random.uniform(next(ks), (N_S5, 2, g), f32,
                                            math.log(S5_DT_MIN), math.log(S5_DT_MAX))
    inp['s5_b_re'] = nrm((N_S5, 2, g, p, hs), (2 * hs) ** -0.5)
    inp['s5_b_im'] = nrm((N_S5, 2, g, p, hs), (2 * hs) ** -0.5)
    inp['s5_c_re'] = nrm((N_S5, 2, g, hs, p), (2 * p) ** -0.5)
    inp['s5_c_im'] = nrm((N_S5, 2, g, hs, p), (2 * p) ** -0.5)
    inp['s5_d'] = nrm((N_S5, w), 1.0)
    inp['s5_w_glu'] = nrm((N_S5, w, w), w ** -0.5)
    inp['s5_b_glu'] = nrm((N_S5, w), 0.01)
    inp['s5_w_out'] = nrm((N_S5, w, d), DEEPNORM_BETA * w ** -0.5)
    inp['hg_w_in'] = nrm((N_HG, d, 5 * w), d ** -0.5)
    inp['hg_lower_bound'] = nrm((DEPTH, w), 0.1)
    inp['hg_norm_g'] = 1.0 + nrm((N_HG, HG_HEAD_DIM), 0.02)
    inp['hg_w_out'] = nrm((N_HG, w, d), DEEPNORM_BETA * w ** -0.5)
    inp['mla_w_in'] = nrm((N_MLA, d, MLA_Q_LORA + MLA_KV_LORA + MLA_ROPE), d ** -0.5)
    inp['mla_q_norm_g'] = 1.0 + nrm((N_MLA, MLA_Q_LORA), 0.02)
    inp['mla_w_q_b'] = nrm((N_MLA, MLA_Q_LORA, MLA_HEADS * (MLA_NOPE + MLA_ROPE)), MLA_Q_LORA ** -0.5)
    inp['mla_kv_norm_g'] = 1.0 + nrm((N_MLA, MLA_KV_LORA), 0.02)
    inp['mla_w_kv_b'] = nrm((N_MLA, MLA_KV_LORA, MLA_HEADS * (MLA_NOPE + MLA_V)), MLA_KV_LORA ** -0.5)
    inp['mla_w_out'] = nrm((N_MLA, MLA_HEADS * MLA_V, d), DEEPNORM_BETA * (MLA_HEADS * MLA_V) ** -0.5)
    return inp


def _fwd_reference(x, positions, ln_mix_g, ln_mix_b, ln_ffn_g, ln_ffn_b, ffn_w_in, ffn_w_out,
              s5_w_in, s5_lam_re, s5_lam_im, s5_log_step, s5_b_re, s5_b_im, s5_c_re, s5_c_im,
              s5_d, s5_w_glu, s5_b_glu, s5_w_out,
              hg_w_in, hg_lower_bound, hg_norm_g, hg_w_out,
              mla_w_in, mla_q_norm_g, mla_w_q_b, mla_kv_norm_g, mla_w_kv_b, mla_w_out):
    lbs = jax.nn.softmax(hg_lower_bound.astype(jnp.float32), axis=0)
    lbs = jnp.cumsum(lbs, axis=0) - lbs[0]
    h = x
    for layer in range(DEPTH):
        kind = LAYER_MIXER[layer]
        slot = layer // N_MIXERS
        if kind == 0:
            m = s5_mixer(h, s5_w_in[slot], s5_lam_re[slot], s5_lam_im[slot], s5_log_step[slot],
                         s5_b_re[slot], s5_b_im[slot], s5_c_re[slot], s5_c_im[slot],
                         s5_d[slot], s5_w_glu[slot], s5_b_glu[slot], s5_w_out[slot])
        elif kind == 1:
            m = hgrn2_mixer(h, hg_w_in[slot], lbs[layer], hg_norm_g[slot], hg_w_out[slot])
        else:
            m = mla_mixer(h, positions, mla_w_in[slot], mla_q_norm_g[slot], mla_w_q_b[slot],
                          mla_kv_norm_g[slot], mla_w_kv_b[slot], mla_w_out[slot])
        h = layer_norm(DEEPNORM_ALPHA * h + m, ln_mix_g[layer], ln_mix_b[layer])
        f = swiglu(h, ffn_w_in[layer], ffn_w_out[layer])
        h = layer_norm(DEEPNORM_ALPHA * h + f, ln_ffn_g[layer], ln_ffn_b[layer])
    return h


import jax as _jax
import jax.numpy as _jnp

TWIN_FORMAT = 'train_step'
FWD_PARAMS = ['x', 'positions', 'ln_mix_g', 'ln_mix_b', 'ln_ffn_g', 'ln_ffn_b', 'ffn_w_in', 'ffn_w_out', 's5_w_in', 's5_lam_re', 's5_lam_im', 's5_log_step', 's5_b_re', 's5_b_im', 's5_c_re', 's5_c_im', 's5_d', 's5_w_glu', 's5_b_glu', 's5_w_out', 'hg_w_in', 'hg_lower_bound', 'hg_norm_g', 'hg_w_out', 'mla_w_in', 'mla_q_norm_g', 'mla_w_q_b', 'mla_kv_norm_g', 'mla_w_kv_b', 'mla_w_out']
TWIN_WEIGHTS = ['ln_mix_g', 'ln_mix_b', 'ln_ffn_g', 'ln_ffn_b', 'ffn_w_in', 'ffn_w_out', 's5_w_in', 's5_lam_re', 's5_lam_im', 's5_log_step', 's5_b_re', 's5_b_im', 's5_c_re', 's5_c_im', 's5_d', 's5_w_glu', 's5_b_glu', 's5_w_out', 'hg_w_in', 'hg_lower_bound', 'hg_norm_g', 'hg_w_out', 'mla_w_in', 'mla_q_norm_g', 'mla_w_q_b', 'mla_kv_norm_g', 'mla_w_kv_b', 'mla_w_out']
TWIN_DIFF_INPUT = 'x'
TWIN_INPUTS = ['x', 'positions', 'ln_mix_g', 'ln_mix_b', 'ln_ffn_g', 'ln_ffn_b', 'ffn_w_in', 'ffn_w_out', 's5_w_in', 's5_lam_re', 's5_lam_im', 's5_log_step', 's5_b_re', 's5_b_im', 's5_c_re', 's5_c_im', 's5_d', 's5_w_glu', 's5_b_glu', 's5_w_out', 'hg_w_in', 'hg_lower_bound', 'hg_norm_g', 'hg_w_out', 'mla_w_in', 'mla_q_norm_g', 'mla_w_q_b', 'mla_kv_norm_g', 'mla_w_kv_b', 'mla_w_out', 'loss_target', 'm_ln_mix_g', 'm_ln_mix_b', 'm_ln_ffn_g', 'm_ln_ffn_b', 'm_ffn_w_in', 'm_ffn_w_out', 'm_s5_w_in', 'm_s5_lam_re', 'm_s5_lam_im', 'm_s5_log_step', 'm_s5_b_re', 'm_s5_b_im', 'm_s5_c_re', 'm_s5_c_im', 'm_s5_d', 'm_s5_w_glu', 'm_s5_b_glu', 'm_s5_w_out', 'm_hg_w_in', 'm_hg_lower_bound', 'm_hg_norm_g', 'm_hg_w_out', 'm_mla_w_in', 'm_mla_q_norm_g', 'm_mla_w_q_b', 'm_mla_kv_norm_g', 'm_mla_w_kv_b', 'm_mla_w_out', 'v_ln_mix_g', 'v_ln_mix_b', 'v_ln_ffn_g', 'v_ln_ffn_b', 'v_ffn_w_in', 'v_ffn_w_out', 'v_s5_w_in', 'v_s5_lam_re', 'v_s5_lam_im', 'v_s5_log_step', 'v_s5_b_re', 'v_s5_b_im', 'v_s5_c_re', 'v_s5_c_im', 'v_s5_d', 'v_s5_w_glu', 'v_s5_b_glu', 'v_s5_w_out', 'v_hg_w_in', 'v_hg_lower_bound', 'v_hg_norm_g', 'v_hg_w_out', 'v_mla_w_in', 'v_mla_q_norm_g', 'v_mla_w_q_b', 'v_mla_kv_norm_g', 'v_mla_w_kv_b', 'v_mla_w_out']
TWIN_OUTPUTS = ['loss', 'grad_x', 'grad_ln_mix_g', 'grad_ln_mix_b', 'grad_ln_ffn_g', 'grad_ln_ffn_b', 'grad_ffn_w_in', 'grad_ffn_w_out', 'grad_s5_w_in', 'grad_s5_lam_re', 'grad_s5_lam_im', 'grad_s5_log_step', 'grad_s5_b_re', 'grad_s5_b_im', 'grad_s5_c_re', 'grad_s5_c_im', 'grad_s5_d', 'grad_s5_w_glu', 'grad_s5_b_glu', 'grad_s5_w_out', 'grad_hg_w_in', 'grad_hg_lower_bound', 'grad_hg_norm_g', 'grad_hg_w_out', 'grad_mla_w_in', 'grad_mla_q_norm_g', 'grad_mla_w_q_b', 'grad_mla_kv_norm_g', 'grad_mla_w_kv_b', 'grad_mla_w_out', 'delta_ln_mix_g', 'delta_ln_mix_b', 'delta_ln_ffn_g', 'delta_ln_ffn_b', 'delta_ffn_w_in', 'delta_ffn_w_out', 'delta_s5_w_in', 'delta_s5_lam_re', 'delta_s5_lam_im', 'delta_s5_log_step', 'delta_s5_b_re', 'delta_s5_b_im', 'delta_s5_c_re', 'delta_s5_c_im', 'delta_s5_d', 'delta_s5_w_glu', 'delta_s5_b_glu', 'delta_s5_w_out', 'delta_hg_w_in', 'delta_hg_lower_bound', 'delta_hg_norm_g', 'delta_hg_w_out', 'delta_mla_w_in', 'delta_mla_q_norm_g', 'delta_mla_w_q_b', 'delta_mla_kv_norm_g', 'delta_mla_w_kv_b', 'delta_mla_w_out', 'new_m_ln_mix_g', 'new_m_ln_mix_b', 'new_m_ln_ffn_g', 'new_m_ln_ffn_b', 'new_m_ffn_w_in', 'new_m_ffn_w_out', 'new_m_s5_w_in', 'new_m_s5_lam_re', 'new_m_s5_lam_im', 'new_m_s5_log_step', 'new_m_s5_b_re', 'new_m_s5_b_im', 'new_m_s5_c_re', 'new_m_s5_c_im', 'new_m_s5_d', 'new_m_s5_w_glu', 'new_m_s5_b_glu', 'new_m_s5_w_out', 'new_m_hg_w_in', 'new_m_hg_lower_bound', 'new_m_hg_norm_g', 'new_m_hg_w_out', 'new_m_mla_w_in', 'new_m_mla_q_norm_g', 'new_m_mla_w_q_b', 'new_m_mla_kv_norm_g', 'new_m_mla_w_kv_b', 'new_m_mla_w_out', 'new_v_ln_mix_g', 'new_v_ln_mix_b', 'new_v_ln_ffn_g', 'new_v_ln_ffn_b', 'new_v_ffn_w_in', 'new_v_ffn_w_out', 'new_v_s5_w_in', 'new_v_s5_lam_re', 'new_v_s5_lam_im', 'new_v_s5_log_step', 'new_v_s5_b_re', 'new_v_s5_b_im', 'new_v_s5_c_re', 'new_v_s5_c_im', 'new_v_s5_d', 'new_v_s5_w_glu', 'new_v_s5_b_glu', 'new_v_s5_w_out', 'new_v_hg_w_in', 'new_v_hg_lower_bound', 'new_v_hg_norm_g', 'new_v_hg_w_out', 'new_v_mla_w_in', 'new_v_mla_q_norm_g', 'new_v_mla_w_q_b', 'new_v_mla_kv_norm_g', 'new_v_mla_w_kv_b', 'new_v_mla_w_out']
TWIN_LEAF_KINDS = {'loss': 'loss', 'grad_x': 'grad_x', 'grad_ln_mix_g': 'grad_w', 'grad_ln_mix_b': 'grad_w', 'grad_ln_ffn_g': 'grad_w', 'grad_ln_ffn_b': 'grad_w', 'grad_ffn_w_in': 'grad_w', 'grad_ffn_w_out': 'grad_w', 'grad_s5_w_in': 'grad_w', 'grad_s5_lam_re': 'grad_w', 'grad_s5_lam_im': 'grad_w', 'grad_s5_log_step': 'grad_w', 'grad_s5_b_re': 'grad_w', 'grad_s5_b_im': 'grad_w', 'grad_s5_c_re': 'grad_w', 'grad_s5_c_im': 'grad_w', 'grad_s5_d': 'grad_w', 'grad_s5_w_glu': 'grad_w', 'grad_s5_b_glu': 'grad_w', 'grad_s5_w_out': 'grad_w', 'grad_hg_w_in': 'grad_w', 'grad_hg_lower_bound': 'grad_w', 'grad_hg_norm_g': 'grad_w', 'grad_hg_w_out': 'grad_w', 'grad_mla_w_in': 'grad_w', 'grad_mla_q_norm_g': 'grad_w', 'grad_mla_w_q_b': 'grad_w', 'grad_mla_kv_norm_g': 'grad_w', 'grad_mla_w_kv_b': 'grad_w', 'grad_mla_w_out': 'grad_w', 'delta_ln_mix_g': 'delta_w', 'delta_ln_mix_b': 'delta_w', 'delta_ln_ffn_g': 'delta_w', 'delta_ln_ffn_b': 'delta_w', 'delta_ffn_w_in': 'delta_w', 'delta_ffn_w_out': 'delta_w', 'delta_s5_w_in': 'delta_w', 'delta_s5_lam_re': 'delta_w', 'delta_s5_lam_im': 'delta_w', 'delta_s5_log_step': 'delta_w', 'delta_s5_b_re': 'delta_w', 'delta_s5_b_im': 'delta_w', 'delta_s5_c_re': 'delta_w', 'delta_s5_c_im': 'delta_w', 'delta_s5_d': 'delta_w', 'delta_s5_w_glu': 'delta_w', 'delta_s5_b_glu': 'delta_w', 'delta_s5_w_out': 'delta_w', 'delta_hg_w_in': 'delta_w', 'delta_hg_lower_bound': 'delta_w', 'delta_hg_norm_g': 'delta_w', 'delta_hg_w_out': 'delta_w', 'delta_mla_w_in': 'delta_w', 'delta_mla_q_norm_g': 'delta_w', 'delta_mla_w_q_b': 'delta_w', 'delta_mla_kv_norm_g': 'delta_w', 'delta_mla_w_kv_b': 'delta_w', 'delta_mla_w_out': 'delta_w', 'new_m_ln_mix_g': 'new_m', 'new_m_ln_mix_b': 'new_m', 'new_m_ln_ffn_g': 'new_m', 'new_m_ln_ffn_b': 'new_m', 'new_m_ffn_w_in': 'new_m', 'new_m_ffn_w_out': 'new_m', 'new_m_s5_w_in': 'new_m', 'new_m_s5_lam_re': 'new_m', 'new_m_s5_lam_im': 'new_m', 'new_m_s5_log_step': 'new_m', 'new_m_s5_b_re': 'new_m', 'new_m_s5_b_im': 'new_m', 'new_m_s5_c_re': 'new_m', 'new_m_s5_c_im': 'new_m', 'new_m_s5_d': 'new_m', 'new_m_s5_w_glu': 'new_m', 'new_m_s5_b_glu': 'new_m', 'new_m_s5_w_out': 'new_m', 'new_m_hg_w_in': 'new_m', 'new_m_hg_lower_bound': 'new_m', 'new_m_hg_norm_g': 'new_m', 'new_m_hg_w_out': 'new_m', 'new_m_mla_w_in': 'new_m', 'new_m_mla_q_norm_g': 'new_m', 'new_m_mla_w_q_b': 'new_m', 'new_m_mla_kv_norm_g': 'new_m', 'new_m_mla_w_kv_b': 'new_m', 'new_m_mla_w_out': 'new_m', 'new_v_ln_mix_g': 'new_v', 'new_v_ln_mix_b': 'new_v', 'new_v_ln_ffn_g': 'new_v', 'new_v_ln_ffn_b': 'new_v', 'new_v_ffn_w_in': 'new_v', 'new_v_ffn_w_out': 'new_v', 'new_v_s5_w_in': 'new_v', 'new_v_s5_lam_re': 'new_v', 'new_v_s5_lam_im': 'new_v', 'new_v_s5_log_step': 'new_v', 'new_v_s5_b_re': 'new_v', 'new_v_s5_b_im': 'new_v', 'new_v_s5_c_re': 'new_v', 'new_v_s5_c_im': 'new_v', 'new_v_s5_d': 'new_v', 'new_v_s5_w_glu': 'new_v', 'new_v_s5_b_glu': 'new_v', 'new_v_s5_w_out': 'new_v', 'new_v_hg_w_in': 'new_v', 'new_v_hg_lower_bound': 'new_v', 'new_v_hg_norm_g': 'new_v', 'new_v_hg_w_out': 'new_v', 'new_v_mla_w_in': 'new_v', 'new_v_mla_q_norm_g': 'new_v', 'new_v_mla_w_q_b': 'new_v', 'new_v_mla_kv_norm_g': 'new_v', 'new_v_mla_w_kv_b': 'new_v', 'new_v_mla_w_out': 'new_v'}


def _forward(args):
    return _fwd_reference(*[args[k] for k in FWD_PARAMS])


def _output_shape():
    def fwd():
        inp = _fwd_setup_inputs(0)
        return _fwd_reference(*[inp[k] for k in FWD_PARAMS])
    out = _jax.eval_shape(fwd)
    return out.shape, out.dtype

N_MICROBATCH = 1
ADAM_LR = 0.001
ADAM_B1 = 0.9
ADAM_B2 = 0.999
ADAM_EPS = 1e-08
ADAM_WD = 0.01
ADAM_STEP = 10
PER_EXAMPLE_BATCH_AXIS = {'x': 0, 'positions': 0, 'loss_target': 0}
SHARED_INPUTS = []
_WEIGHT_DTYPES = {'ln_mix_g': _jnp.float32, 'ln_mix_b': _jnp.float32, 'ln_ffn_g': _jnp.float32, 'ln_ffn_b': _jnp.float32, 'ffn_w_in': _jnp.float32, 'ffn_w_out': _jnp.float32, 's5_w_in': _jnp.float32, 's5_lam_re': _jnp.float32, 's5_lam_im': _jnp.float32, 's5_log_step': _jnp.float32, 's5_b_re': _jnp.float32, 's5_b_im': _jnp.float32, 's5_c_re': _jnp.float32, 's5_c_im': _jnp.float32, 's5_d': _jnp.float32, 's5_w_glu': _jnp.float32, 's5_b_glu': _jnp.float32, 's5_w_out': _jnp.float32, 'hg_w_in': _jnp.float32, 'hg_lower_bound': _jnp.float32, 'hg_norm_g': _jnp.float32, 'hg_w_out': _jnp.float32, 'mla_w_in': _jnp.float32, 'mla_q_norm_g': _jnp.float32, 'mla_w_q_b': _jnp.float32, 'mla_kv_norm_g': _jnp.float32, 'mla_w_kv_b': _jnp.float32, 'mla_w_out': _jnp.float32}
MOMENT_SCALE = {'ln_mix_g': 4.220654e+00, 'ln_mix_b': 8.794493e-01, 'ln_ffn_g': 6.440216e+01, 'ln_ffn_b': 3.258989e+00, 'ffn_w_in': 3.272813e-02, 'ffn_w_out': 1.269803e-01, 's5_w_in': 3.446852e-02, 's5_lam_re': 2.126097e-03, 's5_lam_im': 2.296676e-03, 's5_log_step': 2.095197e+00, 's5_b_re': 1.211565e-03, 's5_b_im': 1.220044e-03, 's5_c_re': 2.408675e-03, 's5_c_im': 2.415584e-03, 's5_d': 4.645161e-02, 's5_w_glu': 1.043866e-02, 's5_b_glu': 2.077861e-02, 's5_w_out': 9.860582e-02, 'hg_w_in': 3.485500e-02, 'hg_lower_bound': 1.376998e-03, 'hg_norm_g': 1.583768e-01, 'hg_w_out': 1.288739e-01, 'mla_w_in': 2.098183e-02, 'mla_q_norm_g': 1.685809e-02, 'mla_w_q_b': 8.379790e-03, 'mla_kv_norm_g': 3.182236e-02, 'mla_w_kv_b': 1.019080e-02, 'mla_w_out': 2.704244e-02}


def _to_microbatches(a, axis):
    t = _jnp.moveaxis(a, axis, 0)
    t = t.reshape((N_MICROBATCH, t.shape[0] // N_MICROBATCH) + t.shape[1:])
    return _jnp.moveaxis(t, 1, axis + 1)


def setup_inputs(seed: int = 0) -> dict:
    inp = _fwd_setup_inputs(seed)
    key = _jax.random.fold_in(_jax.random.key(seed), 7919)
    shape, _ = _output_shape()
    out = dict(inp)
    out["loss_target"] = _jax.random.normal(_jax.random.fold_in(key, 0), shape, _jnp.float32)
    for i, name in enumerate(TWIN_WEIGHTS):
        w = inp[name].astype(_jnp.float32)
        if MOMENT_SCALE is None:
            s = _jnp.sqrt(_jnp.mean(_jnp.square(w)) + 1e-30)
        else:
            s = MOMENT_SCALE[name]
        km, kv = _jax.random.split(_jax.random.fold_in(key, i + 1))
        out[name] = w
        out["m_" + name] = s * _jax.random.normal(km, w.shape, _jnp.float32)
        out["v_" + name] = (s * s) * _jax.random.uniform(kv, w.shape, _jnp.float32, 0.5, 1.5)
    if N_MICROBATCH > 1:
        for name, axis in PER_EXAMPLE_BATCH_AXIS.items():
            out[name] = _to_microbatches(out[name], axis)
    return {'x': out['x'], 'positions': out['positions'], 'ln_mix_g': out['ln_mix_g'], 'ln_mix_b': out['ln_mix_b'], 'ln_ffn_g': out['ln_ffn_g'], 'ln_ffn_b': out['ln_ffn_b'], 'ffn_w_in': out['ffn_w_in'], 'ffn_w_out': out['ffn_w_out'], 's5_w_in': out['s5_w_in'], 's5_lam_re': out['s5_lam_re'], 's5_lam_im': out['s5_lam_im'], 's5_log_step': out['s5_log_step'], 's5_b_re': out['s5_b_re'], 's5_b_im': out['s5_b_im'], 's5_c_re': out['s5_c_re'], 's5_c_im': out['s5_c_im'], 's5_d': out['s5_d'], 's5_w_glu': out['s5_w_glu'], 's5_b_glu': out['s5_b_glu'], 's5_w_out': out['s5_w_out'], 'hg_w_in': out['hg_w_in'], 'hg_lower_bound': out['hg_lower_bound'], 'hg_norm_g': out['hg_norm_g'], 'hg_w_out': out['hg_w_out'], 'mla_w_in': out['mla_w_in'], 'mla_q_norm_g': out['mla_q_norm_g'], 'mla_w_q_b': out['mla_w_q_b'], 'mla_kv_norm_g': out['mla_kv_norm_g'], 'mla_w_kv_b': out['mla_w_kv_b'], 'mla_w_out': out['mla_w_out'], 'loss_target': out['loss_target'], 'm_ln_mix_g': out['m_ln_mix_g'], 'm_ln_mix_b': out['m_ln_mix_b'], 'm_ln_ffn_g': out['m_ln_ffn_g'], 'm_ln_ffn_b': out['m_ln_ffn_b'], 'm_ffn_w_in': out['m_ffn_w_in'], 'm_ffn_w_out': out['m_ffn_w_out'], 'm_s5_w_in': out['m_s5_w_in'], 'm_s5_lam_re': out['m_s5_lam_re'], 'm_s5_lam_im': out['m_s5_lam_im'], 'm_s5_log_step': out['m_s5_log_step'], 'm_s5_b_re': out['m_s5_b_re'], 'm_s5_b_im': out['m_s5_b_im'], 'm_s5_c_re': out['m_s5_c_re'], 'm_s5_c_im': out['m_s5_c_im'], 'm_s5_d': out['m_s5_d'], 'm_s5_w_glu': out['m_s5_w_glu'], 'm_s5_b_glu': out['m_s5_b_glu'], 'm_s5_w_out': out['m_s5_w_out'], 'm_hg_w_in': out['m_hg_w_in'], 'm_hg_lower_bound': out['m_hg_lower_bound'], 'm_hg_norm_g': out['m_hg_norm_g'], 'm_hg_w_out': out['m_hg_w_out'], 'm_mla_w_in': out['m_mla_w_in'], 'm_mla_q_norm_g': out['m_mla_q_norm_g'], 'm_mla_w_q_b': out['m_mla_w_q_b'], 'm_mla_kv_norm_g': out['m_mla_kv_norm_g'], 'm_mla_w_kv_b': out['m_mla_w_kv_b'], 'm_mla_w_out': out['m_mla_w_out'], 'v_ln_mix_g': out['v_ln_mix_g'], 'v_ln_mix_b': out['v_ln_mix_b'], 'v_ln_ffn_g': out['v_ln_ffn_g'], 'v_ln_ffn_b': out['v_ln_ffn_b'], 'v_ffn_w_in': out['v_ffn_w_in'], 'v_ffn_w_out': out['v_ffn_w_out'], 'v_s5_w_in': out['v_s5_w_in'], 'v_s5_lam_re': out['v_s5_lam_re'], 'v_s5_lam_im': out['v_s5_lam_im'], 'v_s5_log_step': out['v_s5_log_step'], 'v_s5_b_re': out['v_s5_b_re'], 'v_s5_b_im': out['v_s5_b_im'], 'v_s5_c_re': out['v_s5_c_re'], 'v_s5_c_im': out['v_s5_c_im'], 'v_s5_d': out['v_s5_d'], 'v_s5_w_glu': out['v_s5_w_glu'], 'v_s5_b_glu': out['v_s5_b_glu'], 'v_s5_w_out': out['v_s5_w_out'], 'v_hg_w_in': out['v_hg_w_in'], 'v_hg_lower_bound': out['v_hg_lower_bound'], 'v_hg_norm_g': out['v_hg_norm_g'], 'v_hg_w_out': out['v_hg_w_out'], 'v_mla_w_in': out['v_mla_w_in'], 'v_mla_q_norm_g': out['v_mla_q_norm_g'], 'v_mla_w_q_b': out['v_mla_w_q_b'], 'v_mla_kv_norm_g': out['v_mla_kv_norm_g'], 'v_mla_w_kv_b': out['v_mla_w_kv_b'], 'v_mla_w_out': out['v_mla_w_out']}


def _loss(weights, diff, rest, loss_target):
    with _jax.named_scope("forward"):
        args = {**rest, TWIN_DIFF_INPUT: diff, **{k: w.astype(_WEIGHT_DTYPES[k]) for k, w in weights.items()}}
        y = _forward(args)
    with _jax.named_scope("loss_head"):
        err = _jnp.square(y.astype(_jnp.float32) - loss_target)
        return 0.5 * _jnp.sum(_jnp.mean(err, axis=-1)) if err.ndim else 0.5 * err


def _adamw(w, g, m, v):
    m = ADAM_B1 * m + (1.0 - ADAM_B1) * g
    v = ADAM_B2 * v + (1.0 - ADAM_B2) * _jnp.square(g)
    m_hat = m / (1.0 - ADAM_B1 ** ADAM_STEP)
    v_hat = v / (1.0 - ADAM_B2 ** ADAM_STEP)
    delta = -ADAM_LR * (m_hat / (_jnp.sqrt(v_hat) + ADAM_EPS) + ADAM_WD * w)
    return delta, m, v


def reference(x, positions, ln_mix_g, ln_mix_b, ln_ffn_g, ln_ffn_b, ffn_w_in, ffn_w_out, s5_w_in, s5_lam_re, s5_lam_im, s5_log_step, s5_b_re, s5_b_im, s5_c_re, s5_c_im, s5_d, s5_w_glu, s5_b_glu, s5_w_out, hg_w_in, hg_lower_bound, hg_norm_g, hg_w_out, mla_w_in, mla_q_norm_g, mla_w_q_b, mla_kv_norm_g, mla_w_kv_b, mla_w_out, loss_target, m_ln_mix_g, m_ln_mix_b, m_ln_ffn_g, m_ln_ffn_b, m_ffn_w_in, m_ffn_w_out, m_s5_w_in, m_s5_lam_re, m_s5_lam_im, m_s5_log_step, m_s5_b_re, m_s5_b_im, m_s5_c_re, m_s5_c_im, m_s5_d, m_s5_w_glu, m_s5_b_glu, m_s5_w_out, m_hg_w_in, m_hg_lower_bound, m_hg_norm_g, m_hg_w_out, m_mla_w_in, m_mla_q_norm_g, m_mla_w_q_b, m_mla_kv_norm_g, m_mla_w_kv_b, m_mla_w_out, v_ln_mix_g, v_ln_mix_b, v_ln_ffn_g, v_ln_ffn_b, v_ffn_w_in, v_ffn_w_out, v_s5_w_in, v_s5_lam_re, v_s5_lam_im, v_s5_log_step, v_s5_b_re, v_s5_b_im, v_s5_c_re, v_s5_c_im, v_s5_d, v_s5_w_glu, v_s5_b_glu, v_s5_w_out, v_hg_w_in, v_hg_lower_bound, v_hg_norm_g, v_hg_w_out, v_mla_w_in, v_mla_q_norm_g, v_mla_w_q_b, v_mla_kv_norm_g, v_mla_w_kv_b, v_mla_w_out):
    given = dict(x=x, positions=positions, ln_mix_g=ln_mix_g, ln_mix_b=ln_mix_b, ln_ffn_g=ln_ffn_g, ln_ffn_b=ln_ffn_b, ffn_w_in=ffn_w_in, ffn_w_out=ffn_w_out, s5_w_in=s5_w_in, s5_lam_re=s5_lam_re, s5_lam_im=s5_lam_im, s5_log_step=s5_log_step, s5_b_re=s5_b_re, s5_b_im=s5_b_im, s5_c_re=s5_c_re, s5_c_im=s5_c_im, s5_d=s5_d, s5_w_glu=s5_w_glu, s5_b_glu=s5_b_glu, s5_w_out=s5_w_out, hg_w_in=hg_w_in, hg_lower_bound=hg_lower_bound, hg_norm_g=hg_norm_g, hg_w_out=hg_w_out, mla_w_in=mla_w_in, mla_q_norm_g=mla_q_norm_g, mla_w_q_b=mla_w_q_b, mla_kv_norm_g=mla_kv_norm_g, mla_w_kv_b=mla_w_kv_b, mla_w_out=mla_w_out, loss_target=loss_target, m_ln_mix_g=m_ln_mix_g, m_ln_mix_b=m_ln_mix_b, m_ln_ffn_g=m_ln_ffn_g, m_ln_ffn_b=m_ln_ffn_b, m_ffn_w_in=m_ffn_w_in, m_ffn_w_out=m_ffn_w_out, m_s5_w_in=m_s5_w_in, m_s5_lam_re=m_s5_lam_re, m_s5_lam_im=m_s5_lam_im, m_s5_log_step=m_s5_log_step, m_s5_b_re=m_s5_b_re, m_s5_b_im=m_s5_b_im, m_s5_c_re=m_s5_c_re, m_s5_c_im=m_s5_c_im, m_s5_d=m_s5_d, m_s5_w_glu=m_s5_w_glu, m_s5_b_glu=m_s5_b_glu, m_s5_w_out=m_s5_w_out, m_hg_w_in=m_hg_w_in, m_hg_lower_bound=m_hg_lower_bound, m_hg_norm_g=m_hg_norm_g, m_hg_w_out=m_hg_w_out, m_mla_w_in=m_mla_w_in, m_mla_q_norm_g=m_mla_q_norm_g, m_mla_w_q_b=m_mla_w_q_b, m_mla_kv_norm_g=m_mla_kv_norm_g, m_mla_w_kv_b=m_mla_w_kv_b, m_mla_w_out=m_mla_w_out, v_ln_mix_g=v_ln_mix_g, v_ln_mix_b=v_ln_mix_b, v_ln_ffn_g=v_ln_ffn_g, v_ln_ffn_b=v_ln_ffn_b, v_ffn_w_in=v_ffn_w_in, v_ffn_w_out=v_ffn_w_out, v_s5_w_in=v_s5_w_in, v_s5_lam_re=v_s5_lam_re, v_s5_lam_im=v_s5_lam_im, v_s5_log_step=v_s5_log_step, v_s5_b_re=v_s5_b_re, v_s5_b_im=v_s5_b_im, v_s5_c_re=v_s5_c_re, v_s5_c_im=v_s5_c_im, v_s5_d=v_s5_d, v_s5_w_glu=v_s5_w_glu, v_s5_b_glu=v_s5_b_glu, v_s5_w_out=v_s5_w_out, v_hg_w_in=v_hg_w_in, v_hg_lower_bound=v_hg_lower_bound, v_hg_norm_g=v_hg_norm_g, v_hg_w_out=v_hg_w_out, v_mla_w_in=v_mla_w_in, v_mla_q_norm_g=v_mla_q_norm_g, v_mla_w_q_b=v_mla_w_q_b, v_mla_kv_norm_g=v_mla_kv_norm_g, v_mla_w_kv_b=v_mla_w_kv_b, v_mla_w_out=v_mla_w_out)
    weights = {n: given[n] for n in TWIN_WEIGHTS}
    shared = {n: given[n] for n in SHARED_INPUTS}
    per_example = {n: given[n] for n in ['x', 'positions']}
    grad_fn = _jax.value_and_grad(_loss, argnums=(0, 1))

    def one_microbatch(ex, loss_target):
        ex = dict(ex)
        diff = ex.pop(TWIN_DIFF_INPUT)
        return grad_fn(weights, diff, {**shared, **ex}, loss_target)

    if N_MICROBATCH == 1:
        loss, (grad_w, grad_x) = one_microbatch(per_example, given["loss_target"])
    else:
        def body(carry, xs):
            loss_sum, grad_sum = carry
            l_k, (gw_k, gx_k) = one_microbatch(xs[0], xs[1])
            with _jax.named_scope("update"):
                return (loss_sum + l_k, _jax.tree.map(_jnp.add, grad_sum, gw_k)), gx_k

        init = (_jnp.zeros((), _jnp.float32), _jax.tree.map(_jnp.zeros_like, weights))
        (loss, grad_w), grad_x = _jax.lax.scan(body, init, (per_example, given["loss_target"]))
    with _jax.named_scope("update"):
        delta_w, new_m, new_v = {}, {}, {}
        for n in TWIN_WEIGHTS:
            delta_w[n], new_m[n], new_v[n] = _adamw(weights[n], grad_w[n], given["m_" + n], given["v_" + n])
    return (loss, grad_x, *[grad_w[n] for n in TWIN_WEIGHTS], *[delta_w[n] for n in TWIN_WEIGHTS],
            *[new_m[n] for n in TWIN_WEIGHTS], *[new_v[n] for n in TWIN_WEIGHTS])
```

```python
import functools
import math

import numpy as np
import jax
import jax.numpy as jnp
from jax import lax
from jax.experimental import pallas as pl
from jax.experimental.pallas import tpu as pltpu

F32 = jnp.float32
BF16 = jnp.bfloat16

D_MODEL = 1024
DEPTH = 4
LAYER_MIXER = (0, 1, 2, 0)
S5_GROUP = 16
S5_GROUPS = 64
S5_STATE = 64
S5_CHUNK = 16
HG_HEADS = 8
HG_DIM = 128
HG_CHUNK = 128
MLA_HEADS = 8
MLA_NOPE = 128
MLA_ROPE = 64
MLA_V = 128
MLA_Q_LORA = 384
MLA_KV_LORA = 256
ROPE_THETA = 10000.0
FFN_HIDDEN = 2816
ALPHA = (2 * DEPTH) ** 0.25
LN_EPS = 1e-5
RMS_EPS = 1e-6
ADAM_LR, ADAM_B1, ADAM_B2, ADAM_EPS, ADAM_WD, ADAM_STEP = 0.001, 0.9, 0.999, 1e-08, 0.01, 10
VMEM_LIMIT_BYTES = 56 * 1024 * 1024
MESH = pl.DeviceIdType.MESH
N_CHIPS = 4
N_DEV = 8


def _cparams():
    return pltpu.CompilerParams(vmem_limit_bytes=VMEM_LIMIT_BYTES)


def _tile(n, want, mult):
    t = min(want, n)
    t -= t % mult
    while t >= mult:
        if n % t == 0:
            return t
        t -= mult
    return n


def _sigmoid(x):
    return 1.0 / (1.0 + jnp.exp(-x))


def _silu(x):
    return x * _sigmoid(x)


def _dsilu(x):
    s = _sigmoid(x)
    return s * (1.0 + x * (1.0 - s))


_GELU_C = math.sqrt(2.0 / math.pi)


def _gelu(x):
    return 0.5 * x * (1.0 + jnp.tanh(_GELU_C * (x + 0.044715 * x * x * x)))


def _dgelu(x):
    t = jnp.tanh(_GELU_C * (x + 0.044715 * x * x * x))
    return 0.5 * (1.0 + t) + 0.5 * x * (1.0 - t * t) * _GELU_C * (1.0 + 3 * 0.044715 * x * x)


def _layer_norm(z, g, b):
    mu = jnp.mean(z, axis=-1, keepdims=True)
    zc = z - mu
    var = jnp.mean(zc * zc, axis=-1, keepdims=True)
    return zc * lax.rsqrt(var + LN_EPS) * g + b


def _bcast8(row):
    return jnp.broadcast_to(row, (8, row.shape[-1]))


def _stack_rows8(rows):
    n = rows[0].shape[-1]
    idx = lax.broadcasted_iota(jnp.int32, (8, n), 0)
    out = jnp.zeros((8, n), F32)
    for i, r in enumerate(rows):
        out = jnp.where(idx == i, _bcast8(r), out)
    return out


def _psum_rows(a):
    return a.reshape(-1, 8, a.shape[-1])[:, 0, :].sum(axis=0)


_DN = {"nn": ((1,), (0,)), "nt": ((1,), (1,)), "tn": ((0,), (0,))}


def _mm(name, grid, a_defs, b_defs, pairs, n_acc, acc_shape, extra_defs, out_defs, epilogue):
    nk = grid[2]
    na, nb, ne, no = len(a_defs), len(b_defs), len(extra_defs), len(out_defs)

    def body(*refs):
        a_refs = refs[:na]
        b_refs = refs[na:na + nb]
        e_refs = refs[na + nb:na + nb + ne]
        o_refs = refs[na + nb + ne:na + nb + ne + no]
        acc = refs[-1]
        k = pl.program_id(2)

        @pl.when(k == 0)
        def _():
            acc[...] = jnp.zeros(acc.shape, F32)

        for (ai, bi, ci, mode) in pairs:
            a = a_refs[ai][...].astype(BF16)
            b = b_refs[bi][...].astype(BF16)
            acc[ci] += lax.dot_general(a, b, (_DN[mode], ((), ())), preferred_element_type=F32)

        @pl.when(k == nk - 1)
        def _():
            outs = epilogue([acc[c] for c in range(n_acc)], *[e[...] for e in e_refs])
            for o_ref, o in zip(o_refs, outs):
                o_ref[...] = o.astype(o_ref.dtype)

    in_specs = [pl.BlockSpec(d[1], d[2]) for d in list(a_defs) + list(b_defs) + list(extra_defs)]
    out_specs = [pl.BlockSpec(d[2], d[3]) for d in out_defs]
    out_shape = [jax.ShapeDtypeStruct(d[0], d[1]) for d in out_defs]
    res = pl.pallas_call(
        body, name=name, grid=grid, in_specs=in_specs, out_specs=out_specs, out_shape=out_shape,
        scratch_shapes=[pltpu.VMEM((n_acc,) + tuple(acc_shape), F32)], compiler_params=_cparams(),
    )(*[d[0] for d in list(a_defs) + list(b_defs) + list(extra_defs)])
    return res


def _mm_simple(name, a, b, mode, out_dtype, tm=512, tn=1024, tk=1024, extras=(), epilogue=None, n_out=1,
               out_dtypes=None, psum_outs=0):
    if mode == "nn":
        (M, K), (K2, N) = a.shape, b.shape
    elif mode == "nt":
        (M, K), (N, K2) = a.shape, b.shape
    else:
        (K, M), (K2, N) = a.shape, b.shape
    assert K == K2, (name, a.shape, b.shape, mode)
    tm, tn, tk = _tile(M, tm, 8), _tile(N, tn, 128), _tile(K, tk, 128)
    grid = (M // tm, N // tn, K // tk)
    if mode == "nn":
        a_def = (a, (tm, tk), lambda i, j, k: (i, k))
        b_def = (b, (tk, tn), lambda i, j, k: (k, j))
    elif mode == "nt":
        a_def = (a, (tm, tk), lambda i, j, k: (i, k))
        b_def = (b, (tn, tk), lambda i, j, k: (j, k))
    else:
        a_def = (a, (tk, tm), lambda i, j, k: (k, i))
        b_def = (b, (tk, tn), lambda i, j, k: (k, j))
    extra_defs = []
    for e in extras:
        if e.shape[0] == 1:
            extra_defs.append((e, (1, tn), lambda i, j, k: (0, j)))
        else:
            extra_defs.append((e, (tm, tn), lambda i, j, k: (i, j)))
    out_dtypes = out_dtypes or [out_dtype] * n_out
    out_defs = [((M, N), dt, (tm, tn), lambda i, j, k: (i, j)) for dt in out_dtypes]
    out_defs += [((M // tm * 8, N), F32, (8, tn), lambda i, j, k: (i, j)) for _ in range(psum_outs)]
    if epilogue is None:
        epilogue = lambda accs: [accs[0]]
    res = _mm(name, grid, [a_def], [b_def], [(0, 0, 0, mode)], 1, (tm, tn), extra_defs, out_defs, epilogue)
    return res[0] if len(res) == 1 else res


def _rows(name, T, tm, in_defs, out_defs, fn):
    ni = len(in_defs)

    def body(*refs):
        outs = fn(*[r[...] for r in refs[:ni]])
        for o_ref, o in zip(refs[ni:], outs):
            o_ref[...] = o.astype(o_ref.dtype)

    res = pl.pallas_call(
        body, name=name, grid=(T // tm,),
        in_specs=[pl.BlockSpec(d[1], d[2]) for d in in_defs],
        out_specs=[pl.BlockSpec(d[2], d[3]) for d in out_defs],
        out_shape=[jax.ShapeDtypeStruct(d[0], d[1]) for d in out_defs],
        compiler_params=_cparams(),
    )(*[d[0] for d in in_defs])
    return res


def _grid_call(name, grid, in_defs, out_defs, fn, acc_outs=()):
    ni = len(in_defs)

    def body(*refs):
        outs = fn(*[r[...] for r in refs[:ni]])
        first = pl.program_id(len(grid) - 1) == 0
        for idx, (o_ref, o) in enumerate(zip(refs[ni:], outs)):
            if idx in acc_outs:
                @pl.when(first)
                def _(o_ref=o_ref, o=o):
                    o_ref[...] = o.astype(o_ref.dtype)

                @pl.when(jnp.logical_not(first))
                def _(o_ref=o_ref, o=o):
                    o_ref[...] += o.astype(o_ref.dtype)
            else:
                o_ref[...] = o.astype(o_ref.dtype)

    return pl.pallas_call(
        body, name=name, grid=grid,
        in_specs=[pl.BlockSpec(d[1], d[2]) for d in in_defs],
        out_specs=[pl.BlockSpec(d[2], d[3]) for d in out_defs],
        out_shape=[jax.ShapeDtypeStruct(d[0], d[1]) for d in out_defs],
        compiler_params=_cparams(),
    )(*[d[0] for d in in_defs])


def _rt(a, tm):
    return (a, (tm, a.shape[1]), lambda i: (i, 0))


def _full(a):
    return (a, a.shape, lambda i: (0,) * a.ndim)


def _rt_out(T, n, dt, tm):
    return ((T, n), dt, (tm, n), lambda i: (i, 0))


def _ps_out(T, n, tm):
    return ((T // tm * 8, n), F32, (8, n), lambda i: (i, 0))


def _out_proj_ln(name, a, w, h_in, g, b):
    def epi(accs, h_t, g_t, b_t):
        z = ALPHA * h_t + accs[0]
        return [z, _layer_norm(z, g_t, b_t)]
    return _mm_simple(name, a, w, "nn", F32, tm=256, tn=D_MODEL, tk=1024, extras=(h_in, g, b), epilogue=epi, n_out=2)


def _ln_bwd(name, dh, z, g):
    T = dh.shape[0]
    tm = _tile(T, 512, 8)

    def fn(dh_t, z_t, g_t):
        mu = jnp.mean(z_t, axis=-1, keepdims=True)
        zc = z_t - mu
        var = jnp.mean(zc * zc, axis=-1, keepdims=True)
        rstd = lax.rsqrt(var + LN_EPS)
        xhat = zc * rstd
        dxh = dh_t * g_t
        m1 = jnp.mean(dxh, axis=-1, keepdims=True)
        m2 = jnp.mean(dxh * xhat, axis=-1, keepdims=True)
        dz = rstd * (dxh - m1 - xhat * m2)
        return [dz, _bcast8(jnp.sum(dh_t * xhat, axis=0, keepdims=True)), _bcast8(jnp.sum(dh_t, axis=0, keepdims=True))]

    dz, pg, pb = _rows(name, T, tm, [_rt(dh, tm), _rt(z, tm), _full(g)],
                       [_rt_out(T, D_MODEL, F32, tm), _ps_out(T, D_MODEL, tm), _ps_out(T, D_MODEL, tm)], fn)
    return dz, _psum_rows(pg), _psum_rows(pb)


def _ffn_fwd(l, h, w_in, w_out, g, b):
    T = h.shape[0]
    tm, tn = _tile(T, 512, 8), 1408
    nj = FFN_HIDDEN // tn
    grid = (T // tm, nj, 1)

    def epi(accs):
        gg, uu = accs
        return [gg, uu, _silu(gg) * uu]

    G, U, A = _mm(
        f"ffn{l}_in", grid, [(h, (tm, D_MODEL), lambda i, j, k: (i, 0))],
        [(w_in, (D_MODEL, tn), lambda i, j, k: (0, j)), (w_in, (D_MODEL, tn), lambda i, j, k: (0, j + nj))],
        [(0, 0, 0, "nn"), (0, 1, 1, "nn")], 2, (tm, tn), [],
        [((T, FFN_HIDDEN), F32, (tm, tn), lambda i, j, k: (i, j)),
         ((T, FFN_HIDDEN), F32, (tm, tn), lambda i, j, k: (i, j)),
         ((T, FFN_HIDDEN), BF16, (tm, tn), lambda i, j, k: (i, j))], epi)
    z, h_out = _out_proj_ln(f"ffn{l}_out", A, w_out, h, g, b)
    return h_out, (h, G, U, A, z)


def _ffn_bwd(l, dh_out, saved, w_in, w_out, g):
    h, G, U, A, z = saved
    T = h.shape[0]
    dz, dg, db = _ln_bwd(f"ffn{l}_lnb", dh_out, z, g)

    def epi(accs, g_t, u_t):
        da = accs[0]
        return [da * u_t * _dsilu(g_t), da * _silu(g_t)]

    dG, dU = _mm_simple(f"ffn{l}_dA", dz, w_out, "nt", BF16, tm=512, tn=1408, tk=1024, extras=(G, U), epilogue=epi, n_out=2)
    dw_out = _mm_simple(f"ffn{l}_dWout", A, dz, "tn", BF16, tm=1408, tn=1024, tk=512)
    dw_g = _mm_simple(f"ffn{l}_dWg", h, dG, "tn", BF16, tm=1024, tn=1408, tk=512)
    dw_u = _mm_simple(f"ffn{l}_dWu", h, dU, "tn", BF16, tm=1024, tn=1408, tk=512)
    tm, tk = _tile(T, 512, 8), 1408
    nkk = FFN_HIDDEN // tk
    dh = _mm(
        f"ffn{l}_dX", (T // tm, 1, nkk),
        [(dG, (tm, tk), lambda i, j, k: (i, k)), (dU, (tm, tk), lambda i, j, k: (i, k))],
        [(w_in, (D_MODEL, tk), lambda i, j, k: (0, k)), (w_in, (D_MODEL, tk), lambda i, j, k: (0, k + nkk))],
        [(0, 0, 0, "nt"), (1, 1, 0, "nt")], 1, (tm, D_MODEL),
        [(dz, (tm, D_MODEL), lambda i, j, k: (i, 0))],
        [((T, D_MODEL), F32, (tm, D_MODEL), lambda i, j, k: (i, 0))],
        lambda accs, dz_t: [accs[0] + ALPHA * dz_t])[0]
    return dh, dict(w_in=jnp.concatenate([dw_g, dw_u], axis=1), w_out=dw_out, ln_g=dg, ln_b=db)


def _s5_matrices(lam_re, lam_im, log_step, b_re, b_im, c_re, c_im):
    L, hp = S5_CHUNK, lax.Precision.HIGHEST
    out = {}
    mt_total = 0.0
    for d in range(2):
        lam = lax.complex(lam_re[d], lam_im[d])
        step = jnp.exp(log_step[d])[:, None]
        lam_dt = lam * step
        lam_bar = jnp.exp(lam_dt)
        b_bar = ((lam_bar - 1.0) / lam)[..., None] * lax.complex(b_re[d], b_im[d])
        c = lax.complex(c_re[d], c_im[d])
        pw = jnp.exp(lam_dt[None] * jnp.arange(L + 1, dtype=F32)[:, None, None])
        kj = jnp.einsum("ghp,jgp,gpk->gjhk", c, pw[:L], b_bar, precision=hp).real
        lag = np.arange(L)[None, :] - np.arange(L)[:, None]
        lag = lag if d == 0 else -lag
        sel = np.stack([(lag == j) for j in range(L)]).astype(np.float32)
        mt = jnp.einsum("jst,gjab->gsbta", sel, kj, precision=hp).reshape(S5_GROUPS, 16 * L, 16 * L)
        mt_total = mt_total + mt
        dist = (L - 1 - np.arange(L)) if d == 0 else np.arange(L)
        pc = pw[dist].transpose(1, 0, 2)[:, :, None, :] * b_bar.transpose(0, 2, 1)[:, None, :, :]
        pm = jnp.concatenate([pc.real, pc.imag], axis=-1).reshape(S5_GROUPS, 16 * L, 2 * S5_STATE)
        dq = (np.arange(L) + 1) if d == 0 else (L - np.arange(L))
        qc = c[:, None, :, :] * pw[dq].transpose(1, 0, 2)[:, :, None, :]
        qm = jnp.concatenate([qc.real, -qc.imag], axis=-1).reshape(S5_GROUPS, 16 * L, 2 * S5_STATE).transpose(0, 2, 1)
        a = pw[L]
        out[f"p{d}"], out[f"q{d}"] = pm, qm
        out[f"a{d}"] = jnp.concatenate([a.real, a.imag], axis=-1)
    out["mt"] = mt_total
    return out


def _s5_apow(lam_re, lam_im, log_step, n_steps, conj):
    lam_dt = lax.complex(lam_re, lam_im) * jnp.exp(log_step)[..., None]
    k = (S5_CHUNK * 2.0 ** jnp.arange(n_steps, dtype=F32))[None, None, :, None]
    a = jnp.exp(lam_dt[:, :, None, :] * k)
    re, im = a.real, (-a.imag if conj else a.imag)
    return jnp.stack([jnp.concatenate([re, re], -1), jnp.concatenate([-im, im], -1)], axis=3)


def _shift_rows(x, s, down):
    n = x.shape[0]
    if s >= n:
        return jnp.zeros_like(x)
    if s % 8 == 0:
        z = jnp.zeros((s, x.shape[1]), x.dtype)
        return jnp.concatenate([z, x[:n - s]], axis=0) if down else jnp.concatenate([x[s:], z], axis=0)
    row = lax.broadcasted_iota(jnp.int32, x.shape, 0)
    if down:
        return jnp.where(row >= s, pltpu.roll(x, s, 0), 0.0)
    return jnp.where(row < n - s, pltpu.roll(x, n - s, 0), 0.0)


def _cmul(x, a1, a2):
    return x * a1 + pltpu.roll(x, S5_STATE, 1) * a2


def _chunk_scan(s, apow_ref, down):
    n = s.shape[0]
    k, sh = 0, 1
    while sh < n:
        s = s + _cmul(_shift_rows(s, sh, down), apow_ref[k, 0:1, :], apow_ref[k, 1:2, :])
        k, sh = k + 1, sh * 2
    return s


def _s5_scan_fwd(name, ug, mats, apow):
    G, C, W = ug.shape
    n_steps = apow.shape[2]

    def body(u_ref, mt_ref, p0_ref, p1_ref, q0_ref, q1_ref, ap0_ref, ap1_ref, y_ref, h0_ref, h1_ref):
        u = u_ref[...]
        s0 = jnp.dot(u, p0_ref[...], preferred_element_type=F32)
        s1 = jnp.dot(u, p1_ref[...], preferred_element_type=F32)
        h0 = _shift_rows(_chunk_scan(s0, ap0_ref, True), 1, True)
        h1 = _shift_rows(_chunk_scan(s1, ap1_ref, False), 1, False)
        y = jnp.dot(u, mt_ref[...], preferred_element_type=F32)
        y += jnp.dot(h0.astype(BF16), q0_ref[...], preferred_element_type=F32)
        y += jnp.dot(h1.astype(BF16), q1_ref[...], preferred_element_type=F32)
        y_ref[...] = y
        h0_ref[...] = h0
        h1_ref[...] = h1

    def gspec(shape):
        return pl.BlockSpec((None,) + shape, lambda g: (g,) + (0,) * len(shape))

    ap0, ap1 = apow[0], apow[1]
    return pl.pallas_call(
        body, name=name, grid=(G,),
        in_specs=[gspec((C, W)), gspec((W, W)), gspec((W, 128)), gspec((W, 128)), gspec((128, W)), gspec((128, W)),
                  gspec((n_steps, 2, 128)), gspec((n_steps, 2, 128))],
        out_specs=[gspec((C, W)), gspec((C, 128)), gspec((C, 128))],
        out_shape=[jax.ShapeDtypeStruct((G, C, W), F32), jax.ShapeDtypeStruct((G, C, 128), F32),
                   jax.ShapeDtypeStruct((G, C, 128), F32)],
        compiler_params=_cparams(),
    )(ug, mats["mt"].astype(BF16), mats["p0"].astype(BF16), mats["p1"].astype(BF16),
      mats["q0"].astype(BF16), mats["q1"].astype(BF16), ap0, ap1)


def _s5_scan_bwd(name, dyg, ug, h0, h1, mats, apow_conj):
    G, C, W = ug.shape
    n_steps = apow_conj.shape[2]

    def body(dy_ref, u_ref, h0_ref, h1_ref, mt_ref, p0_ref, p1_ref, q0_ref, q1_ref, ap0_ref, ap1_ref,
             du_ref, dmt_ref, dp0_ref, dp1_ref, dq0_ref, dq1_ref, da_ref):
        dy, u = dy_ref[...], u_ref[...]
        nt, tn = (_DN["nt"], ((), ())), (_DN["tn"], ((), ()))
        dh0 = lax.dot_general(dy, q0_ref[...], nt, preferred_element_type=F32)
        dh1 = lax.dot_general(dy, q1_ref[...], nt, preferred_element_type=F32)
        ds0 = _chunk_scan(_shift_rows(dh0, 1, False), ap0_ref, False)
        ds1 = _chunk_scan(_shift_rows(dh1, 1, True), ap1_ref, True)
        ds0b, ds1b = ds0.astype(BF16), ds1.astype(BF16)
        du = lax.dot_general(dy, mt_ref[...], nt, preferred_element_type=F32)
        du += lax.dot_general(ds0b, p0_ref[...], nt, preferred_element_type=F32)
        du += lax.dot_general(ds1b, p1_ref[...], nt, preferred_element_type=F32)
        du_ref[...] = du
        dmt_ref[...] = lax.dot_general(u, dy, tn, preferred_element_type=F32)
        dp0_ref[...] = lax.dot_general(u, ds0b, tn, preferred_element_type=F32)
        dp1_ref[...] = lax.dot_general(u, ds1b, tn, preferred_element_type=F32)
        h0v, h1v = h0_ref[...], h1_ref[...]
        dq0_ref[...] = lax.dot_general(h0v.astype(BF16), dy, tn, preferred_element_type=F32)
        dq1_ref[...] = lax.dot_general(h1v.astype(BF16), dy, tn, preferred_element_type=F32)
        rows = [jnp.sum(ds0 * h0v, axis=0, keepdims=True), jnp.sum(ds0 * pltpu.roll(h0v, S5_STATE, 1), axis=0, keepdims=True),
                jnp.sum(ds1 * h1v, axis=0, keepdims=True), jnp.sum(ds1 * pltpu.roll(h1v, S5_STATE, 1), axis=0, keepdims=True)]
        da_ref[...] = _stack_rows8(rows)

    def gspec(shape):
        return pl.BlockSpec((None,) + shape, lambda g: (g,) + (0,) * len(shape))

    f32s = lambda *s: jax.ShapeDtypeStruct((G,) + s, F32)
    return pl.pallas_call(
        body, name=name, grid=(G,),
        in_specs=[gspec((C, W)), gspec((C, W)), gspec((C, 128)), gspec((C, 128)), gspec((W, W)), gspec((W, 128)),
                  gspec((W, 128)), gspec((128, W)), gspec((128, W)), gspec((n_steps, 2, 128)), gspec((n_steps, 2, 128))],
        out_specs=[gspec((C, W)), gspec((W, W)), gspec((W, 128)), gspec((W, 128)), gspec((128, W)), gspec((128, W)),
                   gspec((8, 128))],
        out_shape=[f32s(C, W), f32s(W, W), f32s(W, 128), f32s(W, 128), f32s(128, W), f32s(128, W), f32s(8, 128)],
        compiler_params=_cparams(),
    )(dyg, ug, h0, h1, mats["mt"].astype(BF16), mats["p0"].astype(BF16), mats["p1"].astype(BF16),
      mats["q0"].astype(BF16), mats["q1"].astype(BF16), apow_conj[0], apow_conj[1])


def _to_groups(a):
    T = a.shape[0]
    return a.reshape(T // S5_CHUNK, S5_CHUNK, S5_GROUPS, S5_GROUP).transpose(2, 0, 1, 3).reshape(
        S5_GROUPS, T // S5_CHUNK, S5_CHUNK * S5_GROUP)


def _from_groups(a):
    G, C, W = a.shape
    return a.reshape(G, C, S5_CHUNK, S5_GROUP).transpose(1, 2, 0, 3).reshape(C * S5_CHUNK, G * S5_GROUP)


def _s5_fwd(l, h, w, small, g, b):
    T = h.shape[0]
    tm = _tile(T, 512, 8)
    u = _mm_simple(f"s5{l}_in", h, w["w_in"], "nn", F32)
    ug = _to_groups(u).astype(BF16)
    prep = lambda *p: _s5_matrices(*p)
    sp = (small["lam_re"], small["lam_im"], small["log_step"], small["b_re"], small["b_im"], small["c_re"], small["c_im"])
    mats, mats_vjp = jax.vjp(prep, *sp)
    n_steps = max(1, int(math.log2(T // S5_CHUNK)))
    apow = _s5_apow(small["lam_re"], small["lam_im"], small["log_step"], n_steps, False)
    yg, h0, h1 = _s5_scan_fwd(f"s5{l}_scan", ug, mats, apow)
    yssm = _from_groups(yg)
    d_row, bglu_row = small["d"].reshape(1, -1), small["b_glu"].reshape(1, -1)

    def mid(y_t, u_t, d_t):
        return [_gelu(y_t + d_t * u_t)]

    y1 = _rows(f"s5{l}_mid", T, tm, [_rt(yssm, tm), _rt(u, tm), _full(d_row)], [_rt_out(T, D_MODEL, F32, tm)], mid)[0]

    def epi(accs, y1_t, bg_t):
        gate = _sigmoid(accs[0] + bg_t)
        return [y1_t * gate, gate]

    y2, gate = _mm_simple(f"s5{l}_glu", y1, w["w_glu"], "nn", None, extras=(y1, bglu_row), epilogue=epi, n_out=2,
                          out_dtypes=[BF16, F32])
    z, h_out = _out_proj_ln(f"s5{l}_out", y2, w["w_out"], h, g, b)
    return h_out, (h, u, ug, yssm, y1, y2, gate, z, h0, h1, mats, mats_vjp, d_row)


def _s5_bwd(l, dh_out, saved, w, small, g):
    h, u, ug, yssm, y1, y2, gate, z, h0, h1, mats, mats_vjp, d_row = saved
    T = h.shape[0]
    dz, dg, db = _ln_bwd(f"s5{l}_lnb", dh_out, z, g)

    def epi1(accs, y1_t, gate_t):
        dy2 = accs[0]
        dpre = dy2 * y1_t * gate_t * (1.0 - gate_t)
        return [dpre, dy2 * gate_t, _bcast8(jnp.sum(dpre, axis=0, keepdims=True))]

    dpre, dy1a, pbg = _mm_simple(f"s5{l}_dy2", dz, w["w_out"], "nt", None, extras=(y1, gate), epilogue=epi1, n_out=2,
                                 out_dtypes=[BF16, F32], psum_outs=1)
    dw_out = _mm_simple(f"s5{l}_dWout", y2, dz, "tn", BF16, tm=1024, tn=1024, tk=512)

    def epi2(accs, dy1a_t, yssm_t, u_t, d_t):
        dy1 = accs[0] + dy1a_t
        dy = dy1 * _dgelu(yssm_t + d_t * u_t)
        return [dy, dy * d_t, _bcast8(jnp.sum(dy * u_t, axis=0, keepdims=True))]

    dy, du_skip, pdd = _mm_simple(f"s5{l}_dy1", dpre, w["w_glu"], "nt", None, extras=(dy1a, yssm, u, d_row), epilogue=epi2,
                                  n_out=2, out_dtypes=[BF16, F32], psum_outs=1)
    dw_glu = _mm_simple(f"s5{l}_dWglu", y1, dpre, "tn", BF16, tm=1024, tn=1024, tk=512)
    n_steps = max(1, int(math.log2(T // S5_CHUNK)))
    apow_c = _s5_apow(small["lam_re"], small["lam_im"], small["log_step"], n_steps, True)
    dug, dmt, dp0, dp1, dq0, dq1, da = _s5_scan_bwd(f"s5{l}_scanb", _to_groups(dy), ug, h0, h1, mats, apow_c)
    du = _from_groups(dug) + du_skip

    def a_grad(p, q):
        return jnp.concatenate([p[:, :S5_STATE] + p[:, S5_STATE:], q[:, S5_STATE:] - q[:, :S5_STATE]], axis=-1)

    dmats = dict(mt=dmt, p0=dp0, p1=dp1, q0=dq0, q1=dq1, a0=a_grad(da[:, 0], da[:, 1]), a1=a_grad(da[:, 2], da[:, 3]))
    dsp = mats_vjp(dmats)
    dh = _mm_simple(f"s5{l}_dX", du, w["w_in"], "nt", F32, extras=(dz,), epilogue=lambda accs, dz_t: [accs[0] + ALPHA * dz_t])
    dw_in = _mm_simple(f"s5{l}_dWin", h, du, "tn", BF16, tm=1024, tn=1024, tk=512)
    grads = dict(w_in=dw_in, w_glu=dw_glu, w_out=dw_out, d=_psum_rows(pdd), b_glu=_psum_rows(pbg), ln_g=dg, ln_b=db,
                 lam_re=dsp[0], lam_im=dsp[1], log_step=dsp[2], b_re=dsp[3], b_im=dsp[4], c_re=dsp[5], c_im=dsp[6])
    return dh, grads


def _gla_levels(lc):
    ms, m = [], lc // 2
    while m >= 1:
        ms.append(m)
        m //= 2
    return ms


def _gla_scan_matrix(lc):
    r = np.arange(lc)[:, None]
    t = np.arange(lc)[None, :]
    blocks = []
    for m in _gla_levels(lc):
        same = (r // m) == (t // m)
        upper = ((r // m) % 2) == 1
        blocks.append(same & np.where(upper, t >= r, t < r))
    blocks.append(t >= r)
    blocks.append(t < r)
    return np.concatenate(blocks, axis=1).astype(np.float32)


def _gla_gates(z, lb):
    sig = _sigmoid(z)
    ls = jnp.minimum(z, 0.0) - jnp.log(1.0 + jnp.exp(-jnp.abs(z)))
    a = jnp.log(lb)
    bb = jnp.log(1.0 - lb) + ls
    lf = jnp.maximum(a, bb) + jnp.log(1.0 + jnp.exp(-jnp.abs(a - bb)))
    return lf, (1.0 - lb) * (1.0 - sig), sig


def _gla_cumsum(lf):
    b, sh = lf, 1
    while sh < lf.shape[0]:
        b = b + _shift_rows(b, sh, True)
        sh *= 2
    return b


def _gla_bref(b, m):
    lc, n = b.shape
    if 2 * m >= 8:
        nb = lc // (2 * m)
        b3 = b.reshape(nb, 2 * m, n)
        return jnp.broadcast_to(b3[:, m - 1:m, :], (nb, 2 * m, n)).reshape(lc, n)
    row = lax.broadcasted_iota(jnp.int32, b.shape, 0)
    dn1 = pltpu.roll(b, 1, 0)
    if m == 1:
        return jnp.where((row & 1) == 0, b, dn1)
    j = row & 3
    return jnp.where(j == 0, pltpu.roll(b, lc - 1, 0), jnp.where(j == 1, b, jnp.where(j == 2, dn1, pltpu.roll(b, 2, 0))))


def _gla_chunk(q, k, lf):
    lc = q.shape[0]
    nt = (_DN["nt"], ((), ()))
    b = _gla_cumsum(lf)
    row = lax.broadcasted_iota(jnp.int32, (lc, HG_DIM), 0)
    ri = lax.broadcasted_iota(jnp.int32, (lc, lc), 0)
    ci = lax.broadcasted_iota(jnp.int32, (lc, lc), 1)
    qb, kb = q.astype(BF16), k.astype(BF16)
    sc = jnp.where(ri == ci, lax.dot_general(qb, kb, nt, preferred_element_type=F32), 0.0)
    levels = []
    for m in _gla_levels(lc):
        lg = int(math.log2(m))
        upper = ((row >> lg) & 1) == 1
        bref = _gla_bref(b, m)
        w = jnp.exp(jnp.where(upper, b - bref, bref - b))
        xf = jnp.where(upper, q * w, 0.0)
        yf = jnp.where(upper, 0.0, k * w)
        bmask = (ri >> (lg + 1)) == (ci >> (lg + 1))
        xb, yb = xf.astype(BF16), yf.astype(BF16)
        sc = sc + jnp.where(bmask, lax.dot_general(xb, yb, nt, preferred_element_type=F32), 0.0)
        levels.append((upper, w, xf, yf, xb, yb, bmask))
    return b, sc, levels, (ri == ci)


def _hg_specs(T, lc, rev):
    nc = T // lc
    cc = (lambda c: nc - 1 - c) if rev else (lambda c: c)
    q_spec = pl.BlockSpec((None, lc, HG_DIM), lambda d, h, c: (d, cc(c), h))
    z_spec = pl.BlockSpec((None, lc, HG_DIM), lambda d, h, c: (d, cc(c), HG_HEADS * (1 + d) + h))
    v_spec = pl.BlockSpec((None, lc, HG_DIM), lambda d, h, c: (d, cc(c), 3 * HG_HEADS + h))
    lb_spec = pl.BlockSpec((1, HG_DIM), lambda d, h, c: (0, h))
    o_spec = pl.BlockSpec((None, lc, HG_DIM), lambda d, h, c: (d, cc(c), h))
    st_spec = pl.BlockSpec((None, None, None, HG_DIM, HG_DIM), lambda d, h, c: (d, h, cc(c), 0, 0))
    return nc, q_spec, z_spec, v_spec, lb_spec, o_spec, st_spec


def _gla_fwd(name, proj2, lb_row):
    T = proj2.shape[1]
    lc = _tile(T, HG_CHUNK, 8)
    nc, q_spec, z_spec, v_spec, lb_spec, o_spec, st_spec = _hg_specs(T, lc, False)

    def body(q_ref, z_ref, v_ref, lb_ref, o_ref, st_ref, st_s):
        @pl.when(pl.program_id(2) == 0)
        def _():
            st_s[...] = jnp.zeros(st_s.shape, F32)

        q = _silu(q_ref[...])
        lf, k, _ = _gla_gates(z_ref[...], lb_ref[...])
        vb = v_ref[...].astype(BF16)
        b, sc, _, _ = _gla_chunk(q, k, lf)
        st0 = st_s[...]
        st_ref[...] = st0
        bl = b[lc - 1:lc, :]
        o = jnp.dot(sc.astype(BF16), vb, preferred_element_type=F32)
        o += lax.dot_general((q * jnp.exp(b)).astype(BF16), st0.astype(BF16), (_DN["nt"], ((), ())), preferred_element_type=F32)
        o_ref[...] = o
        kd = (k * jnp.exp(bl - b)).astype(BF16)
        st_s[...] = st0 * jnp.exp(bl) + lax.dot_general(vb, kd, (_DN["tn"], ((), ())), preferred_element_type=F32)

    return pl.pallas_call(
        body, name=name, grid=(2, HG_HEADS, nc),
        in_specs=[q_spec, z_spec, v_spec, lb_spec], out_specs=[o_spec, st_spec],
        out_shape=[jax.ShapeDtypeStruct((2, T, D_MODEL), F32), jax.ShapeDtypeStruct((2, HG_HEADS, nc, HG_DIM, HG_DIM), F32)],
        scratch_shapes=[pltpu.VMEM((HG_DIM, HG_DIM), F32)], compiler_params=_cparams(),
    )(proj2, proj2, proj2, lb_row)


def _gla_bwd(name, proj2, lb_row, do2, st):
    T = proj2.shape[1]
    lc = _tile(T, HG_CHUNK, 8)
    nc, q_spec, z_spec, v_spec, lb_spec, o_spec, st_spec = _hg_specs(T, lc, True)
    wall = jnp.asarray(_gla_scan_matrix(lc), BF16)

    def body(q_ref, z_ref, v_ref, lb_ref, do_ref, st_ref, wall_ref, dq_ref, dz_ref, dv_ref, dlb_ref, dst_s):
        first = pl.program_id(2) == 0

        @pl.when(first)
        def _():
            dst_s[...] = jnp.zeros(dst_s.shape, F32)
            dlb_ref[...] = jnp.zeros(dlb_ref.shape, F32)

        nn, nt, tn = (_DN["nn"], ((), ())), (_DN["nt"], ((), ())), (_DN["tn"], ((), ()))
        dot = functools.partial(lax.dot_general, preferred_element_type=F32)
        qr, z, lb = q_ref[...], z_ref[...], lb_ref[...]
        q = _silu(qr)
        lf, k, sig = _gla_gates(z, lb)
        vb = v_ref[...].astype(BF16)
        b, sc, levels, eye = _gla_chunk(q, k, lf)
        st0, dst = st_ref[...], dst_s[...]
        st0b, dstb = st0.astype(BF16), dst.astype(BF16)
        dob = do_ref[...].astype(BF16)
        bl = b[lc - 1:lc, :]
        eb, ebl, ekd = jnp.exp(b), jnp.exp(bl), jnp.exp(bl - b)
        qe, kd = q * eb, k * ekd
        kdb = kd.astype(BF16)
        dsc = dot(dob, vb, nt)
        dv_ref[...] = dot(sc.astype(BF16), dob, tn) + dot(kdb, dstb, nt)
        dqe = dot(dob, st0b, nn)
        dkd = dot(vb, dstb, nn)
        dq = dqe * eb
        dk = dkd * ekd
        zs = []
        dsd = jnp.where(eye, dsc, 0.0).astype(BF16)
        dq += dot(dsd, k.astype(BF16), nn)
        dk += dot(dsd, q.astype(BF16), tn)
        for (upper, w, xf, yf, xb, yb, bmask) in levels:
            dsl = jnp.where(bmask, dsc, 0.0).astype(BF16)
            dx = dot(dsl, yb, nn)
            dy = dot(dsl, xb, tn)
            dq += jnp.where(upper, dx * w, 0.0)
            dk += jnp.where(upper, 0.0, dy * w)
            zs.append(jnp.where(upper, dx * xf, dy * yf).astype(BF16))
        zs.append((dqe * qe).astype(BF16))
        zs.append((dkd * kd).astype(BF16))
        zl = jnp.sum(dst * st0, axis=0, keepdims=True) * ebl
        dlf = dot(wall_ref[...], jnp.concatenate(zs, axis=0), nn) + zl
        dst_s[...] = dst * ebl + dot(dob, qe.astype(BF16), tn)
        inv_f = jnp.exp(-lf)
        one_sig = 1.0 - sig
        dz_ref[...] = (dlf * inv_f - dk) * (1.0 - lb) * sig * one_sig
        dq_ref[...] = dq * _dsilu(qr)
        dlb_ref[...] += _bcast8(jnp.sum((dlf * inv_f - dk) * one_sig, axis=0, keepdims=True))

    big = jax.ShapeDtypeStruct((2, T, D_MODEL), F32)
    return pl.pallas_call(
        body, name=name, grid=(2, HG_HEADS, nc),
        in_specs=[q_spec, z_spec, v_spec, lb_spec, o_spec, st_spec, pl.BlockSpec(wall.shape, lambda d, h, c: (0, 0))],
        out_specs=[o_spec, o_spec, o_spec, pl.BlockSpec((None, None, 8, HG_DIM), lambda d, h, c: (d, h, 0, 0))],
        out_shape=[big, big, big, jax.ShapeDtypeStruct((2, HG_HEADS, 8, HG_DIM), F32)],
        scratch_shapes=[pltpu.VMEM((HG_DIM, HG_DIM), F32)], compiler_params=_cparams(),
    )(proj2, proj2, proj2, lb_row, do2, st, wall)


def _hg_lower_bounds(hg_lower_bound, layer):
    lbs = jax.nn.softmax(hg_lower_bound, axis=0)
    lbs = jnp.cumsum(lbs, axis=0) - lbs[0]
    return lbs[layer].reshape(1, -1)


def _hg_post(o_fw, o_bw, g_raw, ng):
    outs, ons, os_, rs = [], [], [], []
    for hd in range(o_fw.shape[1] // HG_DIM):
        sl = slice(hd * HG_DIM, (hd + 1) * HG_DIM)
        o = o_fw[:, sl] + o_bw[:, sl]
        r = lax.rsqrt(jnp.mean(o * o, axis=-1, keepdims=True) + RMS_EPS)
        on = o * r * ng
        outs.append(on * _silu(g_raw[:, sl]))
        ons.append(on)
        os_.append(o)
        rs.append(r)
    return outs, ons, os_, rs


def _hg_fwd(l, h, w, small, g, b):
    T = h.shape[0]
    tm = _tile(T, 512, 8)
    proj = _mm_simple(f"hg{l}_in", h, w["w_in"], "nn", F32, tn=1280)
    proj2 = jnp.stack([proj, jnp.flip(proj, axis=0)])
    lb_fn = lambda p: _hg_lower_bounds(p, l)
    lb_row, lb_vjp = jax.vjp(lb_fn, small["lower_bound"])
    o2, st = _gla_fwd(f"hg{l}_gla", proj2, lb_row)
    o_fw, o_bw = o2[0], jnp.flip(o2[1], axis=0)
    ng = small["norm_g"].reshape(1, HG_DIM)

    def post(of_t, ob_t, g_t, ng_t):
        return [jnp.concatenate(_hg_post(of_t, ob_t, g_t, ng_t)[0], axis=1)]

    og = _rows(f"hg{l}_post", T, tm,
               [_rt(o_fw, tm), _rt(o_bw, tm), (proj, (tm, D_MODEL), lambda i: (i, 4)), _full(ng)],
               [_rt_out(T, D_MODEL, BF16, tm)], post)[0]
    z, h_out = _out_proj_ln(f"hg{l}_out", og, w["w_out"], h, g, b)
    return h_out, (h, proj, proj2, lb_row, lb_vjp, st, o_fw, o_bw, ng, og, z)


def _hg_bwd(l, dh_out, saved, w, small, g):
    h, proj, proj2, lb_row, lb_vjp, st, o_fw, o_bw, ng, og, z = saved
    T = h.shape[0]
    dz, dg, db = _ln_bwd(f"hg{l}_lnb", dh_out, z, g)
    tm = _tile(T, 512, 8)
    nn_tiles = D_MODEL // HG_DIM

    def epi(accs, of_t, ob_t, g_t, ng_t):
        dog = accs[0]
        _, ons, os_, rs = _hg_post(of_t, ob_t, g_t, ng_t)
        dos, dgs, dngs = [], [], []
        for hd in range(nn_tiles):
            sl = slice(hd * HG_DIM, (hd + 1) * HG_DIM)
            d, o, r = dog[:, sl], os_[hd], rs[hd]
            dgs.append(d * ons[hd] * _dsilu(g_t[:, sl]))
            don = d * _silu(g_t[:, sl])
            dngs.append(jnp.sum(don * o * r, axis=0, keepdims=True))
            dxn = don * ng_t
            dos.append(r * dxn - o * (r * r * r) * jnp.mean(dxn * o, axis=-1, keepdims=True))
        return [jnp.concatenate(dos, axis=1), jnp.concatenate(dgs, axis=1), _bcast8(jnp.concatenate(dngs, axis=1))]

    grid = (T // tm, 1, 1)
    row_map = lambda i, j, k: (i, 0)
    do, dg_raw, png = _mm(
        f"hg{l}_dog", grid, [(dz, (tm, D_MODEL), row_map)], [(w["w_out"], (D_MODEL, D_MODEL), lambda i, j, k: (0, 0))],
        [(0, 0, 0, "nt")], 1, (tm, D_MODEL),
        [(o_fw, (tm, D_MODEL), row_map), (o_bw, (tm, D_MODEL), row_map), (proj, (tm, D_MODEL), lambda i, j, k: (i, 4)),
         (ng, (1, HG_DIM), lambda i, j, k: (0, 0))],
        [((T, D_MODEL), F32, (tm, D_MODEL), row_map), ((T, D_MODEL), F32, (tm, D_MODEL), row_map),
         ((T // tm * 8, D_MODEL), F32, (8, D_MODEL), row_map)], epi)
    dw_out = _mm_simple(f"hg{l}_dWout", og, dz, "tn", BF16, tm=1024, tn=1024, tk=512)
    do2 = jnp.stack([do, jnp.flip(do, axis=0)])
    dq2, dz2, dv2, dlb = _gla_bwd(f"hg{l}_glab", proj2, lb_row, do2, st)
    dproj = jnp.concatenate([dq2[0] + jnp.flip(dq2[1], axis=0), dz2[0], jnp.flip(dz2[1], axis=0),
                             dv2[0] + jnp.flip(dv2[1], axis=0), dg_raw], axis=1).astype(BF16)
    dh = _mm_simple(f"hg{l}_dX", dproj, w["w_in"], "nt", F32, tk=1280, extras=(dz,),
                    epilogue=lambda accs, dz_t: [accs[0] + ALPHA * dz_t])
    dw_in = _mm_simple(f"hg{l}_dWin", h, dproj, "tn", BF16, tm=1024, tn=1280, tk=512)
    dlb_row = (dlb[0, :, 0, :] + dlb[1, :, 0, :]).reshape(1, D_MODEL)
    grads = dict(w_in=dw_in, w_out=dw_out, lower_bound=lb_vjp(dlb_row)[0],
                 norm_g=_psum_rows(png).reshape(nn_tiles, HG_DIM).sum(axis=0), ln_g=dg, ln_b=db)
    return dh, grads


MLA_W = 256
MLA_SCALE = (MLA_NOPE + MLA_ROPE) ** -0.5


def _swap_halves(a):
    n = a.shape[-1] // 2
    return jnp.concatenate([a[..., n:], a[..., :n]], axis=-1)


def _mla_ext_weights(w_in, w_q_b):
    w_in_ext = jnp.concatenate([w_in, _swap_halves(w_in[:, MLA_Q_LORA + MLA_KV_LORA:])], axis=1)
    wq = w_q_b.reshape(MLA_Q_LORA, MLA_HEADS, MLA_NOPE + MLA_ROPE)
    wq_ext = jnp.concatenate([wq, _swap_halves(wq[:, :, MLA_NOPE:])], axis=2).reshape(MLA_Q_LORA, MLA_HEADS * MLA_W)
    return w_in_ext, wq_ext


def _mla_ext_grads(dw_in_ext, dwq_ext):
    n_lat = MLA_Q_LORA + MLA_KV_LORA
    dw_in = jnp.concatenate([dw_in_ext[:, :n_lat], dw_in_ext[:, n_lat:n_lat + MLA_ROPE]
                             + _swap_halves(dw_in_ext[:, n_lat + MLA_ROPE:])], axis=1)
    dq = dwq_ext.reshape(MLA_Q_LORA, MLA_HEADS, MLA_W)
    dwq = jnp.concatenate([dq[:, :, :MLA_NOPE], dq[:, :, MLA_NOPE:MLA_NOPE + MLA_ROPE]
                           + _swap_halves(dq[:, :, MLA_NOPE + MLA_ROPE:])], axis=2)
    return dw_in, dwq.reshape(MLA_Q_LORA, MLA_HEADS * (MLA_NOPE + MLA_ROPE))


def _rope_table(positions):
    half = MLA_ROPE // 2
    inv_freq = 1.0 / (ROPE_THETA ** (jnp.arange(half, dtype=F32) * (2.0 / MLA_ROPE)))
    ang = positions.astype(F32)[:, None] * inv_freq
    cos, sin = jnp.cos(ang), jnp.sin(ang)
    return jnp.concatenate([cos, cos, -sin, sin], axis=1)


def _rope_sum(prod):
    return prod + pltpu.roll(prod, MLA_ROPE, 1)


def _low_half(a):
    lane = lax.broadcasted_iota(jnp.int32, a.shape, 1)
    return jnp.where(lane < MLA_ROPE, a, 0.0)


def _rms(x, g):
    r = lax.rsqrt(jnp.mean(x * x, axis=-1, keepdims=True) + RMS_EPS)
    return x * r * g


def _rms_bwd(x, g, dy):
    r = lax.rsqrt(jnp.mean(x * x, axis=-1, keepdims=True) + RMS_EPS)
    dxn = dy * g
    dx = r * dxn - x * (r * r * r) * jnp.mean(dxn * x, axis=-1, keepdims=True)
    return dx, jnp.sum(dy * x * r, axis=0, keepdims=True)


def _flash_fwd(name, q, k, v):
    H, T, _ = q.shape
    tq, tk = _tile(T, 512, 8), _tile(T, 512, 128)
    nk = T // tk

    def body(q_ref, k_ref, v_ref, o_ref, lse_ref, m_s, l_s, acc_s):
        ki = pl.program_id(2)

        @pl.when(ki == 0)
        def _():
            m_s[...] = jnp.full(m_s.shape, -jnp.inf, F32)
            l_s[...] = jnp.zeros(l_s.shape, F32)
            acc_s[...] = jnp.zeros(acc_s.shape, F32)

        s = lax.dot_general(q_ref[...], k_ref[...], (_DN["nt"], ((), ())), preferred_element_type=F32)
        m_old = m_s[...]
        m_new = jnp.maximum(m_old, jnp.max(s, axis=-1, keepdims=True))
        a = jnp.exp(m_old - m_new)
        p = jnp.exp(s - m_new)
        l_s[...] = a * l_s[...] + jnp.sum(p, axis=-1, keepdims=True)
        acc_s[...] = a * acc_s[...] + jnp.dot(p.astype(BF16), v_ref[...], preferred_element_type=F32)
        m_s[...] = m_new

        @pl.when(ki == nk - 1)
        def _():
            o_ref[...] = acc_s[...] / l_s[...]
            lse_ref[...] = jnp.broadcast_to(m_s[...] + jnp.log(l_s[...]), lse_ref.shape)

    return pl.pallas_call(
        body, name=name, grid=(H, T // tq, nk),
        in_specs=[pl.BlockSpec((None, tq, MLA_W), lambda h, i, j: (h, i, 0)),
                  pl.BlockSpec((None, tk, MLA_W), lambda h, i, j: (h, j, 0)),
                  pl.BlockSpec((None, tk, MLA_V), lambda h, i, j: (h, j, 0))],
        out_specs=[pl.BlockSpec((tq, MLA_V), lambda h, i, j: (i, h)), pl.BlockSpec((tq, MLA_V), lambda h, i, j: (i, h))],
        out_shape=[jax.ShapeDtypeStruct((T, H * MLA_V), F32), jax.ShapeDtypeStruct((T, H * MLA_V), F32)],
        scratch_shapes=[pltpu.VMEM((tq, 1), F32), pltpu.VMEM((tq, 1), F32), pltpu.VMEM((tq, MLA_V), F32)],
        compiler_params=_cparams(),
    )(q, k, v)


def _flash_probs(q_ref, k_ref, v_ref, do_ref, o_ref, lse_ref):
    nt = (_DN["nt"], ((), ()))
    s = lax.dot_general(q_ref[...], k_ref[...], nt, preferred_element_type=F32)
    p = jnp.exp(s - lse_ref[:, 0:1])
    dob = do_ref[...]
    dp = lax.dot_general(dob, v_ref[...], nt, preferred_element_type=F32)
    delta = jnp.sum(dob.astype(F32) * o_ref[...], axis=-1, keepdims=True)
    return p, (p * (dp - delta)).astype(BF16), dob


def _flash_bwd(name, q, k, v, do, o, lse):
    H, T, _ = q.shape
    tq, tk = _tile(T, 512, 8), _tile(T, 512, 128)
    nq, nk = T // tq, T // tk

    def dq_body(q_ref, k_ref, v_ref, do_ref, o_ref, lse_ref, dq_ref, acc_s):
        ki = pl.program_id(2)

        @pl.when(ki == 0)
        def _():
            acc_s[...] = jnp.zeros(acc_s.shape, F32)

        _, ds, _ = _flash_probs(q_ref, k_ref, v_ref, do_ref, o_ref, lse_ref)
        acc_s[...] += jnp.dot(ds, k_ref[...], preferred_element_type=F32)

        @pl.when(ki == nk - 1)
        def _():
            dq_ref[...] = acc_s[...]

    def qspecs(qi, ki):
        return [pl.BlockSpec((None, tq, MLA_W), lambda h, a, b: (h, qi(a, b), 0)),
                pl.BlockSpec((None, tk, MLA_W), lambda h, a, b: (h, ki(a, b), 0)),
                pl.BlockSpec((None, tk, MLA_V), lambda h, a, b: (h, ki(a, b), 0)),
                pl.BlockSpec((tq, MLA_V), lambda h, a, b: (qi(a, b), h)),
                pl.BlockSpec((tq, MLA_V), lambda h, a, b: (qi(a, b), h)),
                pl.BlockSpec((tq, MLA_V), lambda h, a, b: (qi(a, b), h))]

    dq = pl.pallas_call(
        dq_body, name=name + "_dq", grid=(H, nq, nk), in_specs=qspecs(lambda a, b: a, lambda a, b: b),
        out_specs=pl.BlockSpec((None, tq, MLA_W), lambda h, a, b: (h, a, 0)),
        out_shape=jax.ShapeDtypeStruct((H, T, MLA_W), F32),
        scratch_shapes=[pltpu.VMEM((tq, MLA_W), F32)], compiler_params=_cparams(),
    )(q, k, v, do, o, lse)

    def dkv_body(q_ref, k_ref, v_ref, do_ref, o_ref, lse_ref, dk_ref, dv_ref, dk_s, dv_s):
        qi = pl.program_id(2)

        @pl.when(qi == 0)
        def _():
            dk_s[...] = jnp.zeros(dk_s.shape, F32)
            dv_s[...] = jnp.zeros(dv_s.shape, F32)

        tn = (_DN["tn"], ((), ()))
        p, ds, dob = _flash_probs(q_ref, k_ref, v_ref, do_ref, o_ref, lse_ref)
        dv_s[...] += lax.dot_general(p.astype(BF16), dob, tn, preferred_element_type=F32)
        dk_s[...] += lax.dot_general(ds, q_ref[...], tn, preferred_element_type=F32)

        @pl.when(qi == nq - 1)
        def _():
            dk_ref[...] = dk_s[...]
            dv_ref[...] = dv_s[...]

    dk, dv = pl.pallas_call(
        dkv_body, name=name + "_dkv", grid=(H, nk, nq), in_specs=qspecs(lambda a, b: b, lambda a, b: a),
        out_specs=[pl.BlockSpec((None, tk, MLA_W), lambda h, a, b: (h, a, 0)),
                   pl.BlockSpec((None, tk, MLA_V), lambda h, a, b: (h, a, 0))],
        out_shape=[jax.ShapeDtypeStruct((H, T, MLA_W), F32), jax.ShapeDtypeStruct((H, T, MLA_V), F32)],
        scratch_shapes=[pltpu.VMEM((tk, MLA_W), F32), pltpu.VMEM((tk, MLA_V), F32)], compiler_params=_cparams(),
    )(q, k, v, do, o, lse)
    return dq, dk, dv


def _mla_fwd(l, h, w, small, g, b, cs):
    T = h.shape[0]
    tm = _tile(T, 512, 8)
    H = MLA_HEADS
    gq, gkv = small["q_norm_g"].reshape(1, -1), small["kv_norm_g"].reshape(1, -1)
    n_ext = MLA_Q_LORA + MLA_KV_LORA + 2 * MLA_ROPE
    row = lambda i, j, k: (i, 0)
    fix = lambda i, j, k: (0, 0)

    def epi_lat(accs, gq_t, gkv_t):
        a = accs[0]
        ql, kvl = a[:, :MLA_Q_LORA], a[:, MLA_Q_LORA:MLA_Q_LORA + MLA_KV_LORA]
        return [ql, kvl, a[:, MLA_Q_LORA + MLA_KV_LORA:], _rms(ql, gq_t), _rms(kvl, gkv_t)]

    ql, kvl, kr, xq, xkv = _mm(
        f"mla{l}_in", (T // tm, 1, 1), [(h, (tm, D_MODEL), row)], [(w["w_in_ext"], (D_MODEL, n_ext), fix)],
        [(0, 0, 0, "nn")], 1, (tm, n_ext), [(gq, gq.shape, fix), (gkv, gkv.shape, fix)],
        [((T, MLA_Q_LORA), F32, (tm, MLA_Q_LORA), row), ((T, MLA_KV_LORA), F32, (tm, MLA_KV_LORA), row),
         ((T, 2 * MLA_ROPE), F32, (tm, 2 * MLA_ROPE), row), ((T, MLA_Q_LORA), BF16, (tm, MLA_Q_LORA), row),
         ((T, MLA_KV_LORA), BF16, (tm, MLA_KV_LORA), row)], epi_lat)

    def epi_q(accs, cs_t):
        a = accs[0]
        return [jnp.concatenate([a[:, :MLA_NOPE], _rope_sum(a[:, MLA_NOPE:] * cs_t)], axis=1) * MLA_SCALE]

    head_out = lambda i, j, k: (j, i, 0)
    q = _mm(f"mla{l}_q", (T // tm, H, 1), [(xq, (tm, MLA_Q_LORA), row)],
            [(w["wq_ext"], (MLA_Q_LORA, MLA_W), lambda i, j, k: (0, j))], [(0, 0, 0, "nn")], 1, (tm, MLA_W),
            [(cs, (tm, 2 * MLA_ROPE), row)], [((H, T, MLA_W), BF16, (None, tm, MLA_W), head_out)], epi_q)[0]

    def epi_kv(accs, kr_t, cs_t):
        a = accs[0]
        return [jnp.concatenate([a[:, :MLA_NOPE], _low_half(_rope_sum(kr_t * cs_t))], axis=1), a[:, MLA_NOPE:]]

    k, v = _mm(f"mla{l}_kv", (T // tm, H, 1), [(xkv, (tm, MLA_KV_LORA), row)],
               [(w["w_kv_b"], (MLA_KV_LORA, MLA_W), lambda i, j, k: (0, j))], [(0, 0, 0, "nn")], 1, (tm, MLA_W),
               [(kr, (tm, 2 * MLA_ROPE), row), (cs, (tm, 2 * MLA_ROPE), row)],
               [((H, T, MLA_W), BF16, (None, tm, MLA_W), head_out), ((H, T, MLA_V), BF16, (None, tm, MLA_V), head_out)],
               epi_kv)
    o, lse = _flash_fwd(f"mla{l}_attn", q, k, v)
    z, h_out = _out_proj_ln(f"mla{l}_out", o, w["w_out"], h, g, b)
    return h_out, (h, ql, kvl, xq, xkv, q, k, v, o, lse, z, gq, gkv, cs)


def _mla_bwd(l, dh_out, saved, w, g):
    h, ql, kvl, xq, xkv, q, k, v, o, lse, z, gq, gkv, cs = saved
    T = h.shape[0]
    tm = _tile(T, 512, 8)
    H = MLA_HEADS
    dz, dg, db = _ln_bwd(f"mla{l}_lnb", dh_out, z, g)
    do = _mm_simple(f"mla{l}_dO", dz, w["w_out"], "nt", BF16)
    dw_out = _mm_simple(f"mla{l}_dWout", o, dz, "tn", BF16, tm=1024, tn=1024, tk=512)
    dq, dk, dv = _flash_bwd(f"mla{l}_attnb", q, k, v, do, o, lse)
    head_in = lambda i, hh: (hh, i, 0)
    row2 = lambda i, hh: (i, 0)

    def fn_q(dq_t, cs_t):
        d = dq_t[:, MLA_NOPE:]
        return [jnp.concatenate([dq_t[:, :MLA_NOPE], _rope_sum(d) * cs_t], axis=1) * MLA_SCALE]

    dq_eff = _grid_call(f"mla{l}_dqeff", (T // tm, H), [(dq, (None, tm, MLA_W), head_in), (cs, (tm, 2 * MLA_ROPE), row2)],
                        [((T, H * MLA_W), BF16, (tm, MLA_W), lambda i, hh: (i, hh))], fn_q)[0]

    def fn_kv(dk_t, dv_t, cs_t):
        return [jnp.concatenate([dk_t[:, :MLA_NOPE], dv_t], axis=1), _rope_sum(_low_half(dk_t[:, MLA_NOPE:])) * cs_t]

    dkv_eff, dkr = _grid_call(
        f"mla{l}_dkveff", (T // tm, H),
        [(dk, (None, tm, MLA_W), head_in), (dv, (None, tm, MLA_V), head_in), (cs, (tm, 2 * MLA_ROPE), row2)],
        [((T, H * MLA_W), BF16, (tm, MLA_W), lambda i, hh: (i, hh)), ((T, 2 * MLA_ROPE), F32, (tm, 2 * MLA_ROPE), row2)],
        fn_kv, acc_outs=(1,))
    dxq = _mm_simple(f"mla{l}_dxq", dq_eff, w["wq_ext"], "nt", F32, tn=MLA_Q_LORA)
    dwq_ext = _mm_simple(f"mla{l}_dWq", xq, dq_eff, "tn", BF16, tm=MLA_Q_LORA, tn=1024, tk=512)
    dxkv = _mm_simple(f"mla{l}_dxkv", dkv_eff, w["w_kv_b"], "nt", F32, tn=MLA_KV_LORA)
    dwkv = _mm_simple(f"mla{l}_dWkv", xkv, dkv_eff, "tn", BF16, tm=MLA_KV_LORA, tn=1024, tk=512)

    def fn_lat(ql_t, dxq_t, gq_t, kvl_t, dxkv_t, gkv_t, dkr_t):
        dql, dgq = _rms_bwd(ql_t, gq_t, dxq_t)
        dkvl, dgkv = _rms_bwd(kvl_t, gkv_t, dxkv_t)
        return [jnp.concatenate([dql, dkvl, dkr_t], axis=1), _bcast8(dgq), _bcast8(dgkv)]

    n_ext = MLA_Q_LORA + MLA_KV_LORA + 2 * MLA_ROPE
    dlat, pgq, pgkv = _rows(f"mla{l}_dlat", T, tm,
                            [_rt(ql, tm), _rt(dxq, tm), _full(gq), _rt(kvl, tm), _rt(dxkv, tm), _full(gkv), _rt(dkr, tm)],
                            [_rt_out(T, n_ext, BF16, tm), _ps_out(T, MLA_Q_LORA, tm), _ps_out(T, MLA_KV_LORA, tm)], fn_lat)
    dh = _mm_simple(f"mla{l}_dX", dlat, w["w_in_ext"], "nt", F32, tk=n_ext, extras=(dz,),
                    epilogue=lambda accs, dz_t: [accs[0] + ALPHA * dz_t])
    dw_in_ext = _mm_simple(f"mla{l}_dWin", h, dlat, "tn", BF16, tm=1024, tn=n_ext, tk=512)
    dw_in, dwq = _mla_ext_grads(dw_in_ext.astype(F32), dwq_ext.astype(F32))
    grads = dict(w_in=dw_in, w_q_b=dwq, w_kv_b=dwkv, w_out=dw_out, q_norm_g=_psum_rows(pgq), kv_norm_g=_psum_rows(pgkv),
                 ln_g=dg, ln_b=db)
    return dh, grads


def _loss_head(y, target):
    T = y.shape[0]
    tm = _tile(T, 512, 8)

    def fn(y_t, t_t):
        d = y_t - t_t
        part = 0.5 * jnp.sum(jnp.mean(d * d, axis=-1, keepdims=True), axis=0, keepdims=True)
        return [d * (1.0 / D_MODEL), jnp.broadcast_to(part, (8, 128))]

    dy, part = _rows("loss_head", T, tm, [_rt(y, tm), _rt(target, tm)],
                     [_rt_out(T, D_MODEL, F32, tm), ((T // tm * 8, 128), F32, (8, 128), lambda i: (i, 0))], fn)
    return jnp.sum(part.reshape(-1, 8, 128)[:, 0, 0]), dy


_S5_SMALL = ("lam_re", "lam_im", "log_step", "b_re", "b_im", "c_re", "c_im", "d", "b_glu")


def _local_step(x, positions, target, W):
    row = lambda a, i: a[i].reshape(1, -1)
    w_in_ext, wq_ext = _mla_ext_weights(W["mla_w_in"][0], W["mla_w_q_b"][0])
    cs = _rope_table(positions)
    h, saves = x, []
    for l in range(DEPTH):
        kind, slot = LAYER_MIXER[l], l // 3
        g, b = row(W["ln_mix_g"], l), row(W["ln_mix_b"], l)
        if kind == 0:
            w = {k: W["s5_" + k][slot] for k in ("w_in", "w_glu", "w_out")}
            small = {k: W["s5_" + k][slot] for k in _S5_SMALL}
            h, sv = _s5_fwd(l, h, w, small, g, b)
        elif kind == 1:
            w = dict(w_in=W["hg_w_in"][slot], w_out=W["hg_w_out"][slot])
            small = dict(lower_bound=W["hg_lower_bound"], norm_g=W["hg_norm_g"][slot])
            h, sv = _hg_fwd(l, h, w, small, g, b)
        else:
            w = dict(w_in_ext=w_in_ext, wq_ext=wq_ext, w_kv_b=W["mla_w_kv_b"][slot], w_out=W["mla_w_out"][slot])
            small = dict(q_norm_g=W["mla_q_norm_g"][slot], kv_norm_g=W["mla_kv_norm_g"][slot])
            h, sv = _mla_fwd(l, h, w, small, g, b, cs)
        h, fsv = _ffn_fwd(l, h, W["ffn_w_in"][l], W["ffn_w_out"][l], row(W["ln_ffn_g"], l), row(W["ln_ffn_b"], l))
        saves.append((w, small, sv, fsv))
    loss, dh = _loss_head(h, target)
    per_layer = [None] * DEPTH
    for l in reversed(range(DEPTH)):
        kind = LAYER_MIXER[l]
        w, small, sv, fsv = saves[l]
        dh, gf = _ffn_bwd(l, dh, fsv, W["ffn_w_in"][l], W["ffn_w_out"][l], row(W["ln_ffn_g"], l))
        g = row(W["ln_mix_g"], l)
        if kind == 0:
            dh, gm = _s5_bwd(l, dh, sv, w, small, g)
        elif kind == 1:
            dh, gm = _hg_bwd(l, dh, sv, w, small, g)
        else:
            dh, gm = _mla_bwd(l, dh, sv, w, g)
        per_layer[l] = (gm, gf)
    grads = {}
    stack = lambda xs: jnp.stack([a.astype(F32) if a.dtype != BF16 else a for a in xs])
    grads["ln_mix_g"] = stack([per_layer[l][0]["ln_g"] for l in range(DEPTH)])
    grads["ln_mix_b"] = stack([per_layer[l][0]["ln_b"] for l in range(DEPTH)])
    grads["ln_ffn_g"] = stack([per_layer[l][1]["ln_g"] for l in range(DEPTH)])
    grads["ln_ffn_b"] = stack([per_layer[l][1]["ln_b"] for l in range(DEPTH)])
    grads["ffn_w_in"] = stack([per_layer[l][1]["w_in"] for l in range(DEPTH)])
    grads["ffn_w_out"] = stack([per_layer[l][1]["w_out"] for l in range(DEPTH)])
    s5_layers = [l for l in range(DEPTH) if LAYER_MIXER[l] == 0]
    for k in ("w_in", "w_glu", "w_out") + _S5_SMALL:
        grads["s5_" + k] = stack([per_layer[l][0][k] for l in s5_layers])
    hg = per_layer[1][0]
    grads["hg_w_in"], grads["hg_w_out"] = hg["w_in"][None], hg["w_out"][None]
    grads["hg_lower_bound"], grads["hg_norm_g"] = hg["lower_bound"], hg["norm_g"][None]
    for k, v in per_layer[2][0].items():
        if not k.startswith("ln_"):
            grads["mla_" + k] = v[None]
    return loss, dh, grads


def _here():
    return lax.axis_index("x"), lax.axis_index("y"), lax.axis_index("c")


def _any_spec():
    return pl.BlockSpec(memory_space=pl.ANY)


def _chip_exchange(name, xs, scatter):
    n = len(xs)

    def body(*refs):
        ins, outs = refs[:n], refs[n:2 * n]
        send_sems, recv_sems, loc_sems = refs[2 * n:]
        x, y, c = _here()
        me = 2 * x + y
        peers = [(1 - x, y), (x, 1 - y), (1 - x, 1 - y)]
        copies = []
        for t in range(n):
            src_of = (lambda p, t=t: ins[t].at[p]) if scatter else (lambda p, t=t: ins[t])
            loc = pltpu.make_async_copy(src_of(me), outs[t].at[me], loc_sems.at[t])
            loc.start()
            copies.append(loc)
            for j, (px, py) in enumerate(peers):
                cp = pltpu.make_async_remote_copy(
                    src_ref=src_of(2 * px + py), dst_ref=outs[t].at[me], send_sem=send_sems.at[t, j],
                    recv_sem=recv_sems.at[t, j], device_id=(px, py, c), device_id_type=MESH)
                cp.start()
                copies.append(cp)
        for cp in copies:
            cp.wait()

    out_shape = [jax.ShapeDtypeStruct(a.shape if scatter else (N_CHIPS,) + a.shape, a.dtype) for a in xs]
    return pl.pallas_call(
        body, name=name, in_specs=[_any_spec()] * n, out_specs=[_any_spec()] * n, out_shape=out_shape,
        scratch_shapes=[pltpu.SemaphoreType.DMA((n, 3)), pltpu.SemaphoreType.DMA((n, 3)), pltpu.SemaphoreType.DMA((n,))],
    )(*xs)


def _core_exchange(name, a):
    def body(a_ref, o_ref, send_sem, recv_sem, loc_sem):
        x, y, c = _here()
        loc = pltpu.make_async_copy(a_ref, o_ref.at[c], loc_sem)
        loc.start()
        cp = pltpu.make_async_remote_copy(src_ref=a_ref, dst_ref=o_ref.at[c], send_sem=send_sem, recv_sem=recv_sem,
                                          device_id=(x, y, 1 - c), device_id_type=MESH)
        cp.start()
        cp.wait()
        loc.wait()

    return pl.pallas_call(
        body, name=name, in_specs=[_any_spec()], out_specs=_any_spec(),
        out_shape=jax.ShapeDtypeStruct((2,) + a.shape, a.dtype),
        scratch_shapes=[pltpu.SemaphoreType.DMA, pltpu.SemaphoreType.DMA, pltpu.SemaphoreType.DMA],
    )(a)


def _all_exchange(name, a):
    def body(a_ref, o_ref, send_sems, recv_sems, loc_sem):
        x, y, c = _here()
        me = 4 * x + 2 * y + c
        loc = pltpu.make_async_copy(a_ref, o_ref.at[me], loc_sem)
        loc.start()
        copies = [loc]
        for mask in range(1, N_DEV):
            fx, fy, fc = (mask >> 2) & 1, (mask >> 1) & 1, mask & 1
            peer = (1 - x if fx else x, 1 - y if fy else y, 1 - c if fc else c)
            cp = pltpu.make_async_remote_copy(src_ref=a_ref, dst_ref=o_ref.at[me], send_sem=send_sems.at[mask - 1],
                                              recv_sem=recv_sems.at[mask - 1], device_id=peer, device_id_type=MESH)
            cp.start()
            copies.append(cp)
        for cp in copies:
            cp.wait()

    return pl.pallas_call(
        body, name=name, in_specs=[_any_spec()], out_specs=_any_spec(),
        out_shape=jax.ShapeDtypeStruct((N_DEV,) + a.shape, a.dtype),
        scratch_shapes=[pltpu.SemaphoreType.DMA((N_DEV - 1,)), pltpu.SemaphoreType.DMA((N_DEV - 1,)), pltpu.SemaphoreType.DMA],
    )(a)


def _sum_leading(name, a, out_dtype=F32):
    n, R, C = a.shape
    tr = _tile(R, 512, 16)

    def fn(a_t):
        s = a_t[0].astype(F32)
        for i in range(1, n):
            s = s + a_t[i].astype(F32)
        return [s]

    return _grid_call(name, (R // tr,), [(a, (n, tr, C), lambda i: (0, i, 0))],
                      [((R, C), out_dtype, (tr, C), lambda i: (i, 0))], fn)[0]


def _adamw(name, g_parts, w, m, v):
    R, C = w.shape
    tr = _tile(R, 256, 8)
    ng = len(g_parts)
    c1 = 1.0 / (1.0 - ADAM_B1 ** ADAM_STEP)
    c2 = 1.0 / (1.0 - ADAM_B2 ** ADAM_STEP)

    def fn(*tiles):
        g = tiles[0]
        for t in tiles[1:ng]:
            g = g + t
        w_t, m_t, v_t = tiles[ng:]
        m_n = ADAM_B1 * m_t + (1.0 - ADAM_B1) * g
        v_n = ADAM_B2 * v_t + (1.0 - ADAM_B2) * (g * g)
        delta = -ADAM_LR * ((m_n * c1) / (jnp.sqrt(v_n * c2) + ADAM_EPS) + ADAM_WD * w_t)
        return [g, delta, m_n, v_n]

    spec = lambda a: (a, (tr, C), lambda i: (i, 0))
    return _grid_call(name, (R // tr,), [spec(a) for a in list(g_parts) + [w, m, v]],
                      [((R, C), F32, (tr, C), lambda i: (i, 0))] * 4, fn)


_WEIGHTS = ("ln_mix_g", "ln_mix_b", "ln_ffn_g", "ln_ffn_b", "ffn_w_in", "ffn_w_out", "s5_w_in", "s5_lam_re", "s5_lam_im",
            "s5_log_step", "s5_b_re", "s5_b_im", "s5_c_re", "s5_c_im", "s5_d", "s5_w_glu", "s5_b_glu", "s5_w_out", "hg_w_in",
            "hg_lower_bound", "hg_norm_g", "hg_w_out", "mla_w_in", "mla_q_norm_g", "mla_w_q_b", "mla_kv_norm_g", "mla_w_kv_b",
            "mla_w_out")
_BIG = {"ffn_w_in": 2, "ffn_w_out": 1, "s5_w_in": 1, "s5_w_glu": 1, "s5_w_out": 1, "hg_w_in": 2, "hg_w_out": 1,
        "mla_w_in": 1, "mla_w_q_b": 2, "mla_w_kv_b": 2, "mla_w_out": 1}
_SMALL_SHARDED = {"s5_d": 1, "s5_b_glu": 1, "mla_q_norm_g": 1, "mla_kv_norm_g": 1}
_REPLICATED = tuple(n for n in _WEIGHTS if n not in _BIG and n not in _SMALL_SHARDED)
LANES = 1024


def _pack(arrs, dtype, row_mult, lead=0):
    rows, segs, r = [], [], 0
    for a in arrs:
        lead_shape = a.shape[:lead]
        flat = a.astype(dtype).reshape(lead_shape + (-1,))
        n = -(-flat.shape[-1] // LANES)
        flat = jnp.pad(flat, [(0, 0)] * lead + [(0, n * LANES - flat.shape[-1])])
        rows.append(flat.reshape(lead_shape + (n, LANES)))
        segs.append((r, n))
        r += n
    pad = -r % row_mult
    if pad:
        rows.append(jnp.zeros(rows[0].shape[:lead] + (pad, LANES), dtype))
    return jnp.concatenate(rows, axis=lead), segs


def _unpack(packed, segs, shapes):
    out = []
    for (r0, n), shp in zip(segs, shapes):
        size = int(np.prod(shp))
        out.append(packed[..., r0:r0 + n, :].reshape(packed.shape[:-2] + (n * LANES,))[..., :size].reshape(packed.shape[:-2] + tuple(shp)))
    return out


def _unshard(stacked, axis):
    moved = jnp.moveaxis(stacked, 0, axis)
    shp = list(moved.shape)
    return moved.reshape(shp[:axis] + [shp[axis] * shp[axis + 1]] + shp[axis + 2:])


def _shard_split(full, axis):
    shp = list(full.shape)
    a = full.reshape(shp[:axis] + [N_CHIPS, shp[axis] // N_CHIPS] + shp[axis + 1:])
    return jnp.moveaxis(a, axis, 0)


def _train_step(x, positions, target, w, m, v):
    big, small_sh = list(_BIG), list(_SMALL_SHARDED)
    chip = 2 * lax.axis_index("x") + lax.axis_index("y")

    big_pack, big_segs = _pack([w[n] for n in big], BF16, 16)
    sm_pack, sm_segs = _pack([w[n] for n in small_sh], F32, 8)
    big_all, sm_all = _chip_exchange("gather_weights", [big_pack, sm_pack], scatter=False)
    W = {n: w[n] for n in _REPLICATED}
    for n, s in zip(big, _unpack(big_all, big_segs, [w[n].shape for n in big])):
        W[n] = _unshard(s, _BIG[n])
    for n, s in zip(small_sh, _unpack(sm_all, sm_segs, [w[n].shape for n in small_sh])):
        W[n] = _unshard(s, _SMALL_SHARDED[n])

    loss_local, grad_x, G = _local_step(x, positions, target, W)
    loss = lax.psum(loss_local, ("x", "y", "c"))
    out = {}

    g_pack, _ = _pack([_shard_split(G[n].astype(BF16), _BIG[n]) for n in big], BF16, 16, lead=1)
    recv = _chip_exchange("scatter_grads", [g_pack], scatter=True)[0]
    pair = _core_exchange("swap_core_sums", _sum_leading("sum_chips", recv))
    for n, (r0, nr) in zip(big, big_segs):
        as_rows = lambda a: a.reshape(nr, LANES)
        res = _adamw("adamw_" + n, [pair[0, r0:r0 + nr], pair[1, r0:r0 + nr]], as_rows(w[n]), as_rows(m[n]), as_rows(v[n]))
        out[n] = tuple(r.reshape(w[n].shape) for r in res)

    small = list(_REPLICATED) + small_sh
    s_pack, s_segs = _pack([G[n] for n in small], F32, 16)
    total = _sum_leading("sum_small", _all_exchange("gather_small_grads", s_pack))
    g_small = dict(zip(small, _unpack(total, s_segs, [G[n].shape for n in small])))
    for n in small_sh:
        width = w[n].shape[1]
        g_small[n] = lax.dynamic_slice_in_dim(g_small[n], chip * width, width, axis=1)
    packs = [_pack([d[n] for n in small], F32, 8)[0] for d in (g_small, w, m, v)]
    _, a_segs = _pack([w[n] for n in small], F32, 8)
    res = _adamw("adamw_small", [packs[0]], packs[1], packs[2], packs[3])
    unpacked = [_unpack(r, a_segs, [w[n].shape for n in small]) for r in res]
    for i, n in enumerate(small):
        out[n] = tuple(u[i] for u in unpacked)
    return loss, grad_x, out


def kernel(x, positions, ln_mix_g, ln_mix_b, ln_ffn_g, ln_ffn_b, ffn_w_in, ffn_w_out, s5_w_in, s5_lam_re, s5_lam_im,
           s5_log_step, s5_b_re, s5_b_im, s5_c_re, s5_c_im, s5_d, s5_w_glu, s5_b_glu, s5_w_out, hg_w_in,
           hg_lower_bound, hg_norm_g, hg_w_out, mla_w_in, mla_q_norm_g, mla_w_q_b, mla_kv_norm_g, mla_w_kv_b,
           mla_w_out, loss_target, m_ln_mix_g, m_ln_mix_b, m_ln_ffn_g, m_ln_ffn_b, m_ffn_w_in, m_ffn_w_out, m_s5_w_in,
           m_s5_lam_re, m_s5_lam_im, m_s5_log_step, m_s5_b_re, m_s5_b_im, m_s5_c_re, m_s5_c_im, m_s5_d, m_s5_w_glu,
           m_s5_b_glu, m_s5_w_out, m_hg_w_in, m_hg_lower_bound, m_hg_norm_g, m_hg_w_out, m_mla_w_in, m_mla_q_norm_g,
           m_mla_w_q_b, m_mla_kv_norm_g, m_mla_w_kv_b, m_mla_w_out, v_ln_mix_g, v_ln_mix_b, v_ln_ffn_g, v_ln_ffn_b,
           v_ffn_w_in, v_ffn_w_out, v_s5_w_in, v_s5_lam_re, v_s5_lam_im, v_s5_log_step, v_s5_b_re, v_s5_b_im,
           v_s5_c_re, v_s5_c_im, v_s5_d, v_s5_w_glu, v_s5_b_glu, v_s5_w_out, v_hg_w_in, v_hg_lower_bound, v_hg_norm_g,
           v_hg_w_out, v_mla_w_in, v_mla_q_norm_g, v_mla_w_q_b, v_mla_kv_norm_g, v_mla_w_kv_b, v_mla_w_out):
    args = (ln_mix_g, ln_mix_b, ln_ffn_g, ln_ffn_b, ffn_w_in, ffn_w_out, s5_w_in, s5_lam_re, s5_lam_im,
            s5_log_step, s5_b_re, s5_b_im, s5_c_re, s5_c_im, s5_d, s5_w_glu, s5_b_glu, s5_w_out, hg_w_in,
            hg_lower_bound, hg_norm_g, hg_w_out, mla_w_in, mla_q_norm_g, mla_w_q_b, mla_kv_norm_g, mla_w_kv_b,
            mla_w_out, m_ln_mix_g, m_ln_mix_b, m_ln_ffn_g, m_ln_ffn_b, m_ffn_w_in, m_ffn_w_out,
            m_s5_w_in, m_s5_lam_re, m_s5_lam_im, m_s5_log_step, m_s5_b_re, m_s5_b_im, m_s5_c_re, m_s5_c_im, m_s5_d,
            m_s5_w_glu, m_s5_b_glu, m_s5_w_out, m_hg_w_in, m_hg_lower_bound, m_hg_norm_g, m_hg_w_out, m_mla_w_in,
            m_mla_q_norm_g, m_mla_w_q_b, m_mla_kv_norm_g, m_mla_w_kv_b, m_mla_w_out, v_ln_mix_g, v_ln_mix_b,
            v_ln_ffn_g, v_ln_ffn_b, v_ffn_w_in, v_ffn_w_out, v_s5_w_in, v_s5_lam_re, v_s5_lam_im, v_s5_log_step,
            v_s5_b_re, v_s5_b_im, v_s5_c_re, v_s5_c_im, v_s5_d, v_s5_w_glu, v_s5_b_glu, v_s5_w_out, v_hg_w_in,
            v_hg_lower_bound, v_hg_norm_g, v_hg_w_out, v_mla_w_in, v_mla_q_norm_g, v_mla_w_q_b, v_mla_kv_norm_g,
            v_mla_w_kv_b, v_mla_w_out)
    nw = len(_WEIGHTS)
    w = dict(zip(_WEIGHTS, args[:nw]))
    m = dict(zip(_WEIGHTS, args[nw:2 * nw]))
    v = dict(zip(_WEIGHTS, args[2 * nw:]))
    loss, grad_x, out = _train_step(x[0], positions[0], loss_target[0], w, m, v)
    res = [loss, grad_x[None]]
    for i in range(4):
        res += [out[n][i] for n in _WEIGHTS]
    return tuple(res)
```

```python
import functools
import math

import numpy as np
import jax
import jax.numpy as jnp
from jax import lax
from jax.experimental import pallas as pl
from jax.experimental.pallas import tpu as pltpu

F32 = jnp.float32
BF16 = jnp.bfloat16

D_MODEL = 1024
DEPTH = 4
LAYER_MIXER = (0, 1, 2, 0)
S5_GROUP = 16
S5_GROUPS = 64
S5_STATE = 64
S5_CHUNK = 16
HG_HEADS = 8
HG_DIM = 128
HG_CHUNK = 128
MLA_HEADS = 8
MLA_NOPE = 128
MLA_ROPE = 64
MLA_V = 128
MLA_Q_LORA = 384
MLA_KV_LORA = 256
ROPE_THETA = 10000.0
FFN_HIDDEN = 2816
ALPHA = (2 * DEPTH) ** 0.25
LN_EPS = 1e-5
RMS_EPS = 1e-6
ADAM_LR, ADAM_B1, ADAM_B2, ADAM_EPS, ADAM_WD, ADAM_STEP = 0.001, 0.9, 0.999, 1e-08, 0.01, 10
VMEM_LIMIT_BYTES = 56 * 1024 * 1024
MESH = pl.DeviceIdType.MESH
N_CHIPS = 4
N_DEV = 8


def _cparams():
    return pltpu.CompilerParams(vmem_limit_bytes=VMEM_LIMIT_BYTES)


def _tile(n, want, mult):
    t = min(want, n)
    t -= t % mult
    while t >= mult:
        if n % t == 0:
            return t
        t -= mult
    return n


def _sigmoid(x):
    return 1.0 / (1.0 + jnp.exp(-x))


def _silu(x):
    return x * _sigmoid(x)


def _dsilu(x):
    s = _sigmoid(x)
    return s * (1.0 + x * (1.0 - s))


_GELU_C = math.sqrt(2.0 / math.pi)


def _gelu(x):
    return 0.5 * x * (1.0 + jnp.tanh(_GELU_C * (x + 0.044715 * x * x * x)))


def _dgelu(x):
    t = jnp.tanh(_GELU_C * (x + 0.044715 * x * x * x))
    return 0.5 * (1.0 + t) + 0.5 * x * (1.0 - t * t) * _GELU_C * (1.0 + 3 * 0.044715 * x * x)


def _layer_norm(z, g, b):
    mu = jnp.mean(z, axis=-1, keepdims=True)
    zc = z - mu
    var = jnp.mean(zc * zc, axis=-1, keepdims=True)
    return zc * lax.rsqrt(var + LN_EPS) * g + b


def _bcast8(row):
    return jnp.broadcast_to(row, (8, row.shape[-1]))


def _stack_rows8(rows):
    n = rows[0].shape[-1]
    idx = lax.broadcasted_iota(jnp.int32, (8, n), 0)
    out = jnp.zeros((8, n), F32)
    for i, r in enumerate(rows):
        out = jnp.where(idx == i, _bcast8(r), out)
    return out


def _psum_rows(a):
    return a.reshape(-1, 8, a.shape[-1])[:, 0, :].sum(axis=0)


_DN = {"nn": ((1,), (0,)), "nt": ((1,), (1,)), "tn": ((0,), (0,))}


def _mm(name, grid, a_defs, b_defs, pairs, n_acc, acc_shape, extra_defs, out_defs, epilogue):
    nk = grid[2]
    na, nb, ne, no = len(a_defs), len(b_defs), len(extra_defs), len(out_defs)

    def body(*refs):
        a_refs = refs[:na]
        b_refs = refs[na:na + nb]
        e_refs = refs[na + nb:na + nb + ne]
        o_refs = refs[na + nb + ne:na + nb + ne + no]
        acc = refs[-1]
        k = pl.program_id(2)

        @pl.when(k == 0)
        def _():
            acc[...] = jnp.zeros(acc.shape, F32)

        for (ai, bi, ci, mode) in pairs:
            a = a_refs[ai][...].astype(BF16)
            b = b_refs[bi][...].astype(BF16)
            acc[ci] += lax.dot_general(a, b, (_DN[mode], ((), ())), preferred_element_type=F32)

        @pl.when(k == nk - 1)
        def _():
            outs = epilogue([acc[c] for c in range(n_acc)], *[e[...] for e in e_refs])
            for o_ref, o in zip(o_refs, outs):
                o_ref[...] = o.astype(o_ref.dtype)

    in_specs = [pl.BlockSpec(d[1], d[2]) for d in list(a_defs) + list(b_defs) + list(extra_defs)]
    out_specs = [pl.BlockSpec(d[2], d[3]) for d in out_defs]
    out_shape = [jax.ShapeDtypeStruct(d[0], d[1]) for d in out_defs]
    res = pl.pallas_call(
        body, name=name, grid=grid, in_specs=in_specs, out_specs=out_specs, out_shape=out_shape,
        scratch_shapes=[pltpu.VMEM((n_acc,) + tuple(acc_shape), F32)], compiler_params=_cparams(),
    )(*[d[0] for d in list(a_defs) + list(b_defs) + list(extra_defs)])
    return res


def _mm_simple(name, a, b, mode, out_dtype, tm=512, tn=1024, tk=1024, extras=(), epilogue=None, n_out=1,
               out_dtypes=None, psum_outs=0):
    if mode == "nn":
        (M, K), (K2, N) = a.shape, b.shape
    elif mode == "nt":
        (M, K), (N, K2) = a.shape, b.shape
    else:
        (K, M), (K2, N) = a.shape, b.shape
    assert K == K2, (name, a.shape, b.shape, mode)
    tm, tn, tk = _tile(M, tm, 8), _tile(N, tn, 128), _tile(K, tk, 128)
    grid = (M // tm, N // tn, K // tk)
    if mode == "nn":
        a_def = (a, (tm, tk), lambda i, j, k: (i, k))
        b_def = (b, (tk, tn), lambda i, j, k: (k, j))
    elif mode == "nt":
        a_def = (a, (tm, tk), lambda i, j, k: (i, k))
        b_def = (b, (tn, tk), lambda i, j, k: (j, k))
    else:
        a_def = (a, (tk, tm), lambda i, j, k: (k, i))
        b_def = (b, (tk, tn), lambda i, j, k: (k, j))
    extra_defs = []
    for e in extras:
        if e.shape[0] == 1:
            extra_defs.append((e, (1, tn), lambda i, j, k: (0, j)))
        else:
            extra_defs.append((e, (tm, tn), lambda i, j, k: (i, j)))
    out_dtypes = out_dtypes or [out_dtype] * n_out
    out_defs = [((M, N), dt, (tm, tn), lambda i, j, k: (i, j)) for dt in out_dtypes]
    out_defs += [((M // tm * 8, N), F32, (8, tn), lambda i, j, k: (i, j)) for _ in range(psum_outs)]
    if epilogue is None:
        epilogue = lambda accs: [accs[0]]
    res = _mm(name, grid, [a_def], [b_def], [(0, 0, 0, mode)], 1, (tm, tn), extra_defs, out_defs, epilogue)
    return res[0] if len(res) == 1 else res


def _rows(name, T, tm, in_defs, out_defs, fn):
    ni = len(in_defs)

    def body(*refs):
        outs = fn(*[r[...] for r in refs[:ni]])
        for o_ref, o in zip(refs[ni:], outs):
            o_ref[...] = o.astype(o_ref.dtype)

    res = pl.pallas_call(
        body, name=name, grid=(T // tm,),
        in_specs=[pl.BlockSpec(d[1], d[2]) for d in in_defs],
        out_specs=[pl.BlockSpec(d[2], d[3]) for d in out_defs],
        out_shape=[jax.ShapeDtypeStruct(d[0], d[1]) for d in out_defs],
        compiler_params=_cparams(),
    )(*[d[0] for d in in_defs])
    return res


def _grid_call(name, grid, in_defs, out_defs, fn, acc_outs=()):
    ni = len(in_defs)

    def body(*refs):
        outs = fn(*[r[...] for r in refs[:ni]])
        first = pl.program_id(len(grid) - 1) == 0
        for idx, (o_ref, o) in enumerate(zip(refs[ni:], outs)):
            if idx in acc_outs:
                @pl.when(first)
                def _(o_ref=o_ref, o=o):
                    o_ref[...] = o.astype(o_ref.dtype)

                @pl.when(jnp.logical_not(first))
                def _(o_ref=o_ref, o=o):
                    o_ref[...] += o.astype(o_ref.dtype)
            else:
                o_ref[...] = o.astype(o_ref.dtype)

    return pl.pallas_call(
        body, name=name, grid=grid,
        in_specs=[pl.BlockSpec(d[1], d[2]) for d in in_defs],
        out_specs=[pl.BlockSpec(d[2], d[3]) for d in out_defs],
        out_shape=[jax.ShapeDtypeStruct(d[0], d[1]) for d in out_defs],
        compiler_params=_cparams(),
    )(*[d[0] for d in in_defs])


def _rt(a, tm):
    return (a, (tm, a.shape[1]), lambda i: (i, 0))


def _full(a):
    return (a, a.shape, lambda i: (0,) * a.ndim)


def _rt_out(T, n, dt, tm):
    return ((T, n), dt, (tm, n), lambda i: (i, 0))


def _ps_out(T, n, tm):
    return ((T // tm * 8, n), F32, (8, n), lambda i: (i, 0))


def _out_proj_ln(name, a, w, h_in, g, b):
    def epi(accs, h_t, g_t, b_t):
        z = ALPHA * h_t + accs[0]
        return [z, _layer_norm(z, g_t, b_t)]
    return _mm_simple(name, a, w, "nn", F32, tm=256, tn=D_MODEL, tk=1024, extras=(h_in, g, b), epilogue=epi, n_out=2)


def _ln_bwd(name, dh, z, g):
    T = dh.shape[0]
    tm = _tile(T, 512, 8)

    def fn(dh_t, z_t, g_t):
        mu = jnp.mean(z_t, axis=-1, keepdims=True)
        zc = z_t - mu
        var = jnp.mean(zc * zc, axis=-1, keepdims=True)
        rstd = lax.rsqrt(var + LN_EPS)
        xhat = zc * rstd
        dxh = dh_t * g_t
        m1 = jnp.mean(dxh, axis=-1, keepdims=True)
        m2 = jnp.mean(dxh * xhat, axis=-1, keepdims=True)
        dz = rstd * (dxh - m1 - xhat * m2)
        return [dz, _bcast8(jnp.sum(dh_t * xhat, axis=0, keepdims=True)), _bcast8(jnp.sum(dh_t, axis=0, keepdims=True))]

    dz, pg, pb = _rows(name, T, tm, [_rt(dh, tm), _rt(z, tm), _full(g)],
                       [_rt_out(T, D_MODEL, F32, tm), _ps_out(T, D_MODEL, tm), _ps_out(T, D_MODEL, tm)], fn)
    return dz, _psum_rows(pg), _psum_rows(pb)


def _ffn_fwd(l, h, w_in, w_out, g, b):
    T = h.shape[0]
    tm, tn = _tile(T, 512, 8), 1408
    nj = FFN_HIDDEN // tn
    grid = (T // tm, nj, 1)

    def epi(accs):
        gg, uu = accs
        return [gg, uu, _silu(gg) * uu]

    G, U, A = _mm(
        f"ffn{l}_in", grid, [(h, (tm, D_MODEL), lambda i, j, k: (i, 0))],
        [(w_in, (D_MODEL, tn), lambda i, j, k: (0, j)), (w_in, (D_MODEL, tn), lambda i, j, k: (0, j + nj))],
        [(0, 0, 0, "nn"), (0, 1, 1, "nn")], 2, (tm, tn), [],
        [((T, FFN_HIDDEN), F32, (tm, tn), lambda i, j, k: (i, j)),
         ((T, FFN_HIDDEN), F32, (tm, tn), lambda i, j, k: (i, j)),
         ((T, FFN_HIDDEN), BF16, (tm, tn), lambda i, j, k: (i, j))], epi)
    z, h_out = _out_proj_ln(f"ffn{l}_out", A, w_out, h, g, b)
    return h_out, (h, G, U, A, z)


def _ffn_bwd(l, dh_out, saved, w_in, w_out, g):
    h, G, U, A, z = saved
    T = h.shape[0]
    dz, dg, db = _ln_bwd(f"ffn{l}_lnb", dh_out, z, g)

    def epi(accs, g_t, u_t):
        da = accs[0]
        return [da * u_t * _dsilu(g_t), da * _silu(g_t)]

    dG, dU = _mm_simple(f"ffn{l}_dA", dz, w_out, "nt", BF16, tm=512, tn=1408, tk=1024, extras=(G, U), epilogue=epi, n_out=2)
    dw_out = _mm_simple(f"ffn{l}_dWout", A, dz, "tn", BF16, tm=1408, tn=1024, tk=512)
    dw_g = _mm_simple(f"ffn{l}_dWg", h, dG, "tn", BF16, tm=1024, tn=1408, tk=512)
    dw_u = _mm_simple(f"ffn{l}_dWu", h, dU, "tn", BF16, tm=1024, tn=1408, tk=512)
    tm, tk = _tile(T, 512, 8), 1408
    nkk = FFN_HIDDEN // tk
    dh = _mm(
        f"ffn{l}_dX", (T // tm, 1, nkk),
        [(dG, (tm, tk), lambda i, j, k: (i, k)), (dU, (tm, tk), lambda i, j, k: (i, k))],
        [(w_in, (D_MODEL, tk), lambda i, j, k: (0, k)), (w_in, (D_MODEL, tk), lambda i, j, k: (0, k + nkk))],
        [(0, 0, 0, "nt"), (1, 1, 0, "nt")], 1, (tm, D_MODEL),
        [(dz, (tm, D_MODEL), lambda i, j, k: (i, 0))],
        [((T, D_MODEL), F32, (tm, D_MODEL), lambda i, j, k: (i, 0))],
        lambda accs, dz_t: [accs[0] + ALPHA * dz_t])[0]
    return dh, dict(w_in=jnp.concatenate([dw_g, dw_u], axis=1), w_out=dw_out, ln_g=dg, ln_b=db)


def _s5_matrices(lam_re, lam_im, log_step, b_re, b_im, c_re, c_im):
    L, hp = S5_CHUNK, lax.Precision.HIGHEST
    out = {}
    mt_total = 0.0
    for d in range(2):
        lam = lax.complex(lam_re[d], lam_im[d])
        step = jnp.exp(log_step[d])[:, None]
        lam_dt = lam * step
        lam_bar = jnp.exp(lam_dt)
        b_bar = ((lam_bar - 1.0) / lam)[..., None] * lax.complex(b_re[d], b_im[d])
        c = lax.complex(c_re[d], c_im[d])
        pw = jnp.exp(lam_dt[None] * jnp.arange(L + 1, dtype=F32)[:, None, None])
        kj = jnp.einsum("ghp,jgp,gpk->gjhk", c, pw[:L], b_bar, precision=hp).real
        lag = np.arange(L)[None, :] - np.arange(L)[:, None]
        lag = lag if d == 0 else -lag
        sel = np.stack([(lag == j) for j in range(L)]).astype(np.float32)
        mt = jnp.einsum("jst,gjab->gsbta", sel, kj, precision=hp).reshape(S5_GROUPS, 16 * L, 16 * L)
        mt_total = mt_total + mt
        pw_dist = jnp.flip(pw[:L], 0) if d == 0 else pw[:L]
        pc = pw_dist.transpose(1, 0, 2)[:, :, None, :] * b_bar.transpose(0, 2, 1)[:, None, :, :]
        pm = jnp.concatenate([pc.real, pc.imag], axis=-1).reshape(S5_GROUPS, 16 * L, 2 * S5_STATE)
        pw_read = pw[1:] if d == 0 else jnp.flip(pw[1:], 0)
        qc = c[:, None, :, :] * pw_read.transpose(1, 0, 2)[:, :, None, :]
        qm = jnp.concatenate([qc.real, -qc.imag], axis=-1).reshape(S5_GROUPS, 16 * L, 2 * S5_STATE).transpose(0, 2, 1)
        a = pw[L]
        out[f"p{d}"], out[f"q{d}"] = pm, qm
        out[f"a{d}"] = jnp.concatenate([a.real, a.imag], axis=-1)
    out["mt"] = mt_total
    return out


def _s5_apow(lam_re, lam_im, log_step, n_steps, conj):
    lam_dt = lax.complex(lam_re, lam_im) * jnp.exp(log_step)[..., None]
    k = (S5_CHUNK * 2.0 ** jnp.arange(n_steps, dtype=F32))[None, None, :, None]
    a = jnp.exp(lam_dt[:, :, None, :] * k)
    re, im = a.real, (-a.imag if conj else a.imag)
    return jnp.stack([jnp.concatenate([re, re], -1), jnp.concatenate([-im, im], -1)], axis=3)


def _shift_rows(x, s, down):
    n = x.shape[0]
    if s >= n:
        return jnp.zeros_like(x)
    if s % 8 == 0:
        z = jnp.zeros((s, x.shape[1]), x.dtype)
        return jnp.concatenate([z, x[:n - s]], axis=0) if down else jnp.concatenate([x[s:], z], axis=0)
    row = lax.broadcasted_iota(jnp.int32, x.shape, 0)
    if down:
        return jnp.where(row >= s, pltpu.roll(x, s, 0), 0.0)
    return jnp.where(row < n - s, pltpu.roll(x, n - s, 0), 0.0)


def _cmul(x, a1, a2):
    return x * a1 + pltpu.roll(x, S5_STATE, 1) * a2


def _chunk_scan(s, apow_ref, down):
    n = s.shape[0]
    k, sh = 0, 1
    while sh < n:
        s = s + _cmul(_shift_rows(s, sh, down), apow_ref[k, 0:1, :], apow_ref[k, 1:2, :])
        k, sh = k + 1, sh * 2
    return s


def _s5_scan_fwd(name, ug, mats, apow):
    G, C, W = ug.shape
    n_steps = apow.shape[2]

    def body(u_ref, mt_ref, p0_ref, p1_ref, q0_ref, q1_ref, ap0_ref, ap1_ref, y_ref, h0_ref, h1_ref):
        u = u_ref[...]
        s0 = jnp.dot(u, p0_ref[...], preferred_element_type=F32)
        s1 = jnp.dot(u, p1_ref[...], preferred_element_type=F32)
        h0 = _shift_rows(_chunk_scan(s0, ap0_ref, True), 1, True)
        h1 = _shift_rows(_chunk_scan(s1, ap1_ref, False), 1, False)
        y = jnp.dot(u, mt_ref[...], preferred_element_type=F32)
        y += jnp.dot(h0.astype(BF16), q0_ref[...], preferred_element_type=F32)
        y += jnp.dot(h1.astype(BF16), q1_ref[...], preferred_element_type=F32)
        y_ref[...] = y
        h0_ref[...] = h0
        h1_ref[...] = h1

    def gspec(shape):
        return pl.BlockSpec((None,) + shape, lambda g: (g,) + (0,) * len(shape))

    ap0, ap1 = apow[0], apow[1]
    return pl.pallas_call(
        body, name=name, grid=(G,),
        in_specs=[gspec((C, W)), gspec((W, W)), gspec((W, 128)), gspec((W, 128)), gspec((128, W)), gspec((128, W)),
                  gspec((n_steps, 2, 128)), gspec((n_steps, 2, 128))],
        out_specs=[gspec((C, W)), gspec((C, 128)), gspec((C, 128))],
        out_shape=[jax.ShapeDtypeStruct((G, C, W), F32), jax.ShapeDtypeStruct((G, C, 128), F32),
                   jax.ShapeDtypeStruct((G, C, 128), F32)],
        compiler_params=_cparams(),
    )(ug, mats["mt"].astype(BF16), mats["p0"].astype(BF16), mats["p1"].astype(BF16),
      mats["q0"].astype(BF16), mats["q1"].astype(BF16), ap0, ap1)


def _s5_scan_bwd(name, dyg, ug, h0, h1, mats, apow_conj):
    G, C, W = ug.shape
    n_steps = apow_conj.shape[2]

    def body(dy_ref, u_ref, h0_ref, h1_ref, mt_ref, p0_ref, p1_ref, q0_ref, q1_ref, ap0_ref, ap1_ref,
             du_ref, dmt_ref, dp0_ref, dp1_ref, dq0_ref, dq1_ref, da_ref):
        dy, u = dy_ref[...], u_ref[...]
        nt, tn = (_DN["nt"], ((), ())), (_DN["tn"], ((), ()))
        dh0 = lax.dot_general(dy, q0_ref[...], nt, preferred_element_type=F32)
        dh1 = lax.dot_general(dy, q1_ref[...], nt, preferred_element_type=F32)
        ds0 = _chunk_scan(_shift_rows(dh0, 1, False), ap0_ref, False)
        ds1 = _chunk_scan(_shift_rows(dh1, 1, True), ap1_ref, True)
        ds0b, ds1b = ds0.astype(BF16), ds1.astype(BF16)
        du = lax.dot_general(dy, mt_ref[...], nt, preferred_element_type=F32)
        du += lax.dot_general(ds0b, p0_ref[...], nt, preferred_element_type=F32)
        du += lax.dot_general(ds1b, p1_ref[...], nt, preferred_element_type=F32)
        du_ref[...] = du
        dmt_ref[...] = lax.dot_general(u, dy, tn, preferred_element_type=F32)
        dp0_ref[...] = lax.dot_general(u, ds0b, tn, preferred_element_type=F32)
        dp1_ref[...] = lax.dot_general(u, ds1b, tn, preferred_element_type=F32)
        h0v, h1v = h0_ref[...], h1_ref[...]
        dq0_ref[...] = lax.dot_general(h0v.astype(BF16), dy, tn, preferred_element_type=F32)
        dq1_ref[...] = lax.dot_general(h1v.astype(BF16), dy, tn, preferred_element_type=F32)
        rows = [jnp.sum(ds0 * h0v, axis=0, keepdims=True), jnp.sum(ds0 * pltpu.roll(h0v, S5_STATE, 1), axis=0, keepdims=True),
                jnp.sum(ds1 * h1v, axis=0, keepdims=True), jnp.sum(ds1 * pltpu.roll(h1v, S5_STATE, 1), axis=0, keepdims=True)]
        da_ref[...] = _stack_rows8(rows)

    def gspec(shape):
        return pl.BlockSpec((None,) + shape, lambda g: (g,) + (0,) * len(shape))

    f32s = lambda *s: jax.ShapeDtypeStruct((G,) + s, F32)
    return pl.pallas_call(
        body, name=name, grid=(G,),
        in_specs=[gspec((C, W)), gspec((C, W)), gspec((C, 128)), gspec((C, 128)), gspec((W, W)), gspec((W, 128)),
                  gspec((W, 128)), gspec((128, W)), gspec((128, W)), gspec((n_steps, 2, 128)), gspec((n_steps, 2, 128))],
        out_specs=[gspec((C, W)), gspec((W, W)), gspec((W, 128)), gspec((W, 128)), gspec((128, W)), gspec((128, W)),
                   gspec((8, 128))],
        out_shape=[f32s(C, W), f32s(W, W), f32s(W, 128), f32s(W, 128), f32s(128, W), f32s(128, W), f32s(8, 128)],
        compiler_params=_cparams(),
    )(dyg, ug, h0, h1, mats["mt"].astype(BF16), mats["p0"].astype(BF16), mats["p1"].astype(BF16),
      mats["q0"].astype(BF16), mats["q1"].astype(BF16), apow_conj[0], apow_conj[1])


def _to_groups(a):
    T = a.shape[0]
    return a.reshape(T // S5_CHUNK, S5_CHUNK, S5_GROUPS, S5_GROUP).transpose(2, 0, 1, 3).reshape(
        S5_GROUPS, T // S5_CHUNK, S5_CHUNK * S5_GROUP)


def _from_groups(a):
    G, C, W = a.shape
    return a.reshape(G, C, S5_CHUNK, S5_GROUP).transpose(1, 2, 0, 3).reshape(C * S5_CHUNK, G * S5_GROUP)


def _s5_fwd(l, h, w, small, g, b):
    T = h.shape[0]
    tm = _tile(T, 512, 8)
    u = _mm_simple(f"s5{l}_in", h, w["w_in"], "nn", F32)
    ug = _to_groups(u).astype(BF16)
    prep = lambda *p: _s5_matrices(*p)
    sp = (small["lam_re"], small["lam_im"], small["log_step"], small["b_re"], small["b_im"], small["c_re"], small["c_im"])
    mats, mats_vjp = jax.vjp(prep, *sp)
    n_steps = max(1, int(math.log2(T // S5_CHUNK)))
    apow = _s5_apow(small["lam_re"], small["lam_im"], small["log_step"], n_steps, False)
    yg, h0, h1 = _s5_scan_fwd(f"s5{l}_scan", ug, mats, apow)
    yssm = _from_groups(yg)
    d_row, bglu_row = small["d"].reshape(1, -1), small["b_glu"].reshape(1, -1)

    def mid(y_t, u_t, d_t):
        return [_gelu(y_t + d_t * u_t)]

    y1 = _rows(f"s5{l}_mid", T, tm, [_rt(yssm, tm), _rt(u, tm), _full(d_row)], [_rt_out(T, D_MODEL, F32, tm)], mid)[0]

    def epi(accs, y1_t, bg_t):
        gate = _sigmoid(accs[0] + bg_t)
        return [y1_t * gate, gate]

    y2, gate = _mm_simple(f"s5{l}_glu", y1, w["w_glu"], "nn", None, extras=(y1, bglu_row), epilogue=epi, n_out=2,
                          out_dtypes=[BF16, F32])
    z, h_out = _out_proj_ln(f"s5{l}_out", y2, w["w_out"], h, g, b)
    return h_out, (h, u, ug, yssm, y1, y2, gate, z, h0, h1, mats, mats_vjp, d_row)


def _s5_bwd(l, dh_out, saved, w, small, g):
    h, u, ug, yssm, y1, y2, gate, z, h0, h1, mats, mats_vjp, d_row = saved
    T = h.shape[0]
    dz, dg, db = _ln_bwd(f"s5{l}_lnb", dh_out, z, g)

    def epi1(accs, y1_t, gate_t):
        dy2 = accs[0]
        dpre = dy2 * y1_t * gate_t * (1.0 - gate_t)
        return [dpre, dy2 * gate_t, _bcast8(jnp.sum(dpre, axis=0, keepdims=True))]

    dpre, dy1a, pbg = _mm_simple(f"s5{l}_dy2", dz, w["w_out"], "nt", None, extras=(y1, gate), epilogue=epi1, n_out=2,
                                 out_dtypes=[BF16, F32], psum_outs=1)
    dw_out = _mm_simple(f"s5{l}_dWout", y2, dz, "tn", BF16, tm=1024, tn=1024, tk=512)

    def epi2(accs, dy1a_t, yssm_t, u_t, d_t):
        dy1 = accs[0] + dy1a_t
        dy = dy1 * _dgelu(yssm_t + d_t * u_t)
        return [dy, dy * d_t, _bcast8(jnp.sum(dy * u_t, axis=0, keepdims=True))]

    dy, du_skip, pdd = _mm_simple(f"s5{l}_dy1", dpre, w["w_glu"], "nt", None, extras=(dy1a, yssm, u, d_row), epilogue=epi2,
                                  n_out=2, out_dtypes=[BF16, F32], psum_outs=1)
    dw_glu = _mm_simple(f"s5{l}_dWglu", y1, dpre, "tn", BF16, tm=1024, tn=1024, tk=512)
    n_steps = max(1, int(math.log2(T // S5_CHUNK)))
    apow_c = _s5_apow(small["lam_re"], small["lam_im"], small["log_step"], n_steps, True)
    dug, dmt, dp0, dp1, dq0, dq1, da = _s5_scan_bwd(f"s5{l}_scanb", _to_groups(dy), ug, h0, h1, mats, apow_c)
    du = _from_groups(dug) + du_skip

    def a_grad(p, q):
        return jnp.concatenate([p[:, :S5_STATE] + p[:, S5_STATE:], q[:, S5_STATE:] - q[:, :S5_STATE]], axis=-1)

    dmats = dict(mt=dmt, p0=dp0, p1=dp1, q0=dq0, q1=dq1, a0=a_grad(da[:, 0], da[:, 1]), a1=a_grad(da[:, 2], da[:, 3]))
    dsp = mats_vjp(dmats)
    dh = _mm_simple(f"s5{l}_dX", du, w["w_in"], "nt", F32, extras=(dz,), epilogue=lambda accs, dz_t: [accs[0] + ALPHA * dz_t])
    dw_in = _mm_simple(f"s5{l}_dWin", h, du, "tn", BF16, tm=1024, tn=1024, tk=512)
    grads = dict(w_in=dw_in, w_glu=dw_glu, w_out=dw_out, d=_psum_rows(pdd), b_glu=_psum_rows(pbg), ln_g=dg, ln_b=db,
                 lam_re=dsp[0], lam_im=dsp[1], log_step=dsp[2], b_re=dsp[3], b_im=dsp[4], c_re=dsp[5], c_im=dsp[6])
    return dh, grads


def _gla_levels(lc):
    ms, m = [], lc // 2
    while m >= 1:
        ms.append(m)
        m //= 2
    return ms


def _gla_scan_matrix(lc, rev):
    r = np.arange(lc)[:, None]
    t = np.arange(lc)[None, :]
    blocks = []
    for m in _gla_levels(lc):
        same = (r // m) == (t // m)
        upper = ((r // m) % 2) == 1
        blocks.append(same & np.where(upper, t >= r, t < r))
    blocks.append(t >= r)
    blocks.append(t < r)
    if rev:
        blocks = [blk[::-1, ::-1] for blk in blocks]
    return np.concatenate(blocks, axis=1).astype(np.float32)


def _gla_gates(z, lb):
    sig = _sigmoid(z)
    ls = jnp.minimum(z, 0.0) - jnp.log(1.0 + jnp.exp(-jnp.abs(z)))
    a = jnp.log(lb)
    bb = jnp.log(1.0 - lb) + ls
    lf = jnp.maximum(a, bb) + jnp.log(1.0 + jnp.exp(-jnp.abs(a - bb)))
    return lf, (1.0 - lb) * (1.0 - sig), sig


def _gla_cumsum(lf, rev):
    b, sh = lf, 1
    while sh < lf.shape[0]:
        b = b + _shift_rows(b, sh, not rev)
        sh *= 2
    return b


def _gla_bref(b, m, rev):
    lc, n = b.shape
    idx = m if rev else m - 1
    if 2 * m >= 8:
        nb = lc // (2 * m)
        b3 = b.reshape(nb, 2 * m, n)
        return jnp.broadcast_to(b3[:, idx:idx + 1, :], (nb, 2 * m, n)).reshape(lc, n)
    row = lax.broadcasted_iota(jnp.int32, b.shape, 0)
    j = row & (2 * m - 1)
    out = b
    for jj in range(2 * m):
        if jj != idx:
            out = jnp.where(j == jj, pltpu.roll(b, (jj - idx) % lc, 0), out)
    return out


def _gla_chunk(q, k, lf, rev):
    lc = q.shape[0]
    nt = (_DN["nt"], ((), ()))
    b = _gla_cumsum(lf, rev)
    row = lax.broadcasted_iota(jnp.int32, (lc, HG_DIM), 0)
    ri = lax.broadcasted_iota(jnp.int32, (lc, lc), 0)
    ci = lax.broadcasted_iota(jnp.int32, (lc, lc), 1)
    qb, kb = q.astype(BF16), k.astype(BF16)
    sc = jnp.where(ri == ci, lax.dot_general(qb, kb, nt, preferred_element_type=F32), 0.0)
    levels = []
    for m in _gla_levels(lc):
        lg = int(math.log2(m))
        isq = ((row >> lg) & 1) == (0 if rev else 1)
        bref = _gla_bref(b, m, rev)
        w = jnp.exp(jnp.where(isq, b - bref, bref - b))
        xf = jnp.where(isq, q * w, 0.0)
        yf = jnp.where(isq, 0.0, k * w)
        bmask = (ri >> (lg + 1)) == (ci >> (lg + 1))
        xb, yb = xf.astype(BF16), yf.astype(BF16)
        sc = sc + jnp.where(bmask, lax.dot_general(xb, yb, nt, preferred_element_type=F32), 0.0)
        levels.append((isq, w, xf, yf, xb, yb, bmask))
    return b, sc, levels, (ri == ci)


def _hg_specs(T, lc, rev, backward):
    nc = T // lc
    cc = (lambda c: nc - 1 - c) if rev != backward else (lambda c: c)
    zcol = HG_HEADS * (2 if rev else 1)
    q_spec = pl.BlockSpec((lc, HG_DIM), lambda h, c: (cc(c), h))
    z_spec = pl.BlockSpec((lc, HG_DIM), lambda h, c: (cc(c), zcol + h))
    v_spec = pl.BlockSpec((lc, HG_DIM), lambda h, c: (cc(c), 3 * HG_HEADS + h))
    lb_spec = pl.BlockSpec((1, HG_DIM), lambda h, c: (0, h))
    st_spec = pl.BlockSpec((None, None, HG_DIM, HG_DIM), lambda h, c: (h, cc(c), 0, 0))
    return nc, q_spec, z_spec, v_spec, lb_spec, st_spec


def _gla_fwd(name, proj, lb_row, rev):
    T = proj.shape[0]
    lc = _tile(T, HG_CHUNK, 8)
    nc, q_spec, z_spec, v_spec, lb_spec, st_spec = _hg_specs(T, lc, rev, False)
    last = 0 if rev else lc - 1

    def body(q_ref, z_ref, v_ref, lb_ref, o_ref, st_ref, st_s):
        @pl.when(pl.program_id(1) == 0)
        def _():
            st_s[...] = jnp.zeros(st_s.shape, F32)

        q = _silu(q_ref[...])
        lf, k, _ = _gla_gates(z_ref[...], lb_ref[...])
        vb = v_ref[...].astype(BF16)
        b, sc, _, _ = _gla_chunk(q, k, lf, rev)
        st0 = st_s[...]
        st_ref[...] = st0
        bl = b[last:last + 1, :]
        o = jnp.dot(sc.astype(BF16), vb, preferred_element_type=F32)
        o += lax.dot_general((q * jnp.exp(b)).astype(BF16), st0.astype(BF16), (_DN["nt"], ((), ())), preferred_element_type=F32)
        o_ref[...] = o
        kd = (k * jnp.exp(bl - b)).astype(BF16)
        st_s[...] = st0 * jnp.exp(bl) + lax.dot_general(vb, kd, (_DN["tn"], ((), ())), preferred_element_type=F32)

    return pl.pallas_call(
        body, name=name, grid=(HG_HEADS, nc),
        in_specs=[q_spec, z_spec, v_spec, lb_spec], out_specs=[q_spec, st_spec],
        out_shape=[jax.ShapeDtypeStruct((T, D_MODEL), F32), jax.ShapeDtypeStruct((HG_HEADS, nc, HG_DIM, HG_DIM), F32)],
        scratch_shapes=[pltpu.VMEM((HG_DIM, HG_DIM), F32)], compiler_params=_cparams(),
    )(proj, proj, proj, lb_row)


def _gla_bwd(name, proj, lb_row, do, st, rev):
    T = proj.shape[0]
    lc = _tile(T, HG_CHUNK, 8)
    nc, q_spec, z_spec, v_spec, lb_spec, st_spec = _hg_specs(T, lc, rev, True)
    wall = jnp.asarray(_gla_scan_matrix(lc, rev), BF16)
    last = 0 if rev else lc - 1

    def body(q_ref, z_ref, v_ref, lb_ref, do_ref, st_ref, wall_ref, dq_ref, dz_ref, dv_ref, dlb_ref, dst_s):
        first = pl.program_id(1) == 0

        @pl.when(first)
        def _():
            dst_s[...] = jnp.zeros(dst_s.shape, F32)
            dlb_ref[...] = jnp.zeros(dlb_ref.shape, F32)

        nn, nt, tn = (_DN["nn"], ((), ())), (_DN["nt"], ((), ())), (_DN["tn"], ((), ()))
        dot = functools.partial(lax.dot_general, preferred_element_type=F32)
        qr, z, lb = q_ref[...], z_ref[...], lb_ref[...]
        q = _silu(qr)
        lf, k, sig = _gla_gates(z, lb)
        vb = v_ref[...].astype(BF16)
        b, sc, levels, eye = _gla_chunk(q, k, lf, rev)
        st0, dst = st_ref[...], dst_s[...]
        st0b, dstb = st0.astype(BF16), dst.astype(BF16)
        dob = do_ref[...].astype(BF16)
        bl = b[last:last + 1, :]
        eb, ebl, ekd = jnp.exp(b), jnp.exp(bl), jnp.exp(bl - b)
        qe, kd = q * eb, k * ekd
        kdb = kd.astype(BF16)
        dsc = dot(dob, vb, nt)
        dv_ref[...] = dot(sc.astype(BF16), dob, tn) + dot(kdb, dstb, nt)
        dqe = dot(dob, st0b, nn)
        dkd = dot(vb, dstb, nn)
        dq = dqe * eb
        dk = dkd * ekd
        zs = []
        dsd = jnp.where(eye, dsc, 0.0).astype(BF16)
        dq += dot(dsd, k.astype(BF16), nn)
        dk += dot(dsd, q.astype(BF16), tn)
        for (upper, w, xf, yf, xb, yb, bmask) in levels:
            dsl = jnp.where(bmask, dsc, 0.0).astype(BF16)
            dx = dot(dsl, yb, nn)
            dy = dot(dsl, xb, tn)
            dq += jnp.where(upper, dx * w, 0.0)
            dk += jnp.where(upper, 0.0, dy * w)
            zs.append(jnp.where(upper, dx * xf, dy * yf).astype(BF16))
        zs.append((dqe * qe).astype(BF16))
        zs.append((dkd * kd).astype(BF16))
        zl = jnp.sum(dst * st0, axis=0, keepdims=True) * ebl
        dlf = dot(wall_ref[...], jnp.concatenate(zs, axis=0), nn) + zl
        dst_s[...] = dst * ebl + dot(dob, qe.astype(BF16), tn)
        inv_f = jnp.exp(-lf)
        one_sig = 1.0 - sig
        dz_ref[...] = (dlf * inv_f - dk) * (1.0 - lb) * sig * one_sig
        dq_ref[...] = dq * _dsilu(qr)
        dlb_ref[...] += _bcast8(jnp.sum((dlf * inv_f - dk) * one_sig, axis=0, keepdims=True))

    big = jax.ShapeDtypeStruct((T, D_MODEL), F32)
    return pl.pallas_call(
        body, name=name, grid=(HG_HEADS, nc),
        in_specs=[q_spec, z_spec, v_spec, lb_spec, q_spec, st_spec, pl.BlockSpec(wall.shape, lambda h, c: (0, 0))],
        out_specs=[q_spec, q_spec, q_spec, pl.BlockSpec((None, 8, HG_DIM), lambda h, c: (h, 0, 0))],
        out_shape=[big, big, big, jax.ShapeDtypeStruct((HG_HEADS, 8, HG_DIM), F32)],
        scratch_shapes=[pltpu.VMEM((HG_DIM, HG_DIM), F32)], compiler_params=_cparams(),
    )(proj, proj, proj, lb_row, do, st, wall)


def _hg_lower_bounds(hg_lower_bound, layer):
    lbs = jax.nn.softmax(hg_lower_bound, axis=0)
    lbs = jnp.cumsum(lbs, axis=0) - lbs[0]
    return lbs[layer].reshape(1, -1)


def _hg_post(o_fw, o_bw, g_raw, ng):
    outs, ons, os_, rs = [], [], [], []
    for hd in range(o_fw.shape[1] // HG_DIM):
        sl = slice(hd * HG_DIM, (hd + 1) * HG_DIM)
        o = o_fw[:, sl] + o_bw[:, sl]
        r = lax.rsqrt(jnp.mean(o * o, axis=-1, keepdims=True) + RMS_EPS)
        on = o * r * ng
        outs.append(on * _silu(g_raw[:, sl]))
        ons.append(on)
        os_.append(o)
        rs.append(r)
    return outs, ons, os_, rs


def _hg_fwd(l, h, w, small, g, b):
    T = h.shape[0]
    tm = _tile(T, 512, 8)
    proj = _mm_simple(f"hg{l}_in", h, w["w_in"], "nn", F32, tn=1280)
    lb_fn = lambda p: _hg_lower_bounds(p, l)
    lb_row, lb_vjp = jax.vjp(lb_fn, small["lower_bound"])
    o_fw, st_fw = _gla_fwd(f"hg{l}_gla_fw", proj, lb_row, False)
    o_bw, st_bw = _gla_fwd(f"hg{l}_gla_bw", proj, lb_row, True)
    ng = small["norm_g"].reshape(1, HG_DIM)

    def post(of_t, ob_t, g_t, ng_t):
        return [jnp.concatenate(_hg_post(of_t, ob_t, g_t, ng_t)[0], axis=1)]

    og = _rows(f"hg{l}_post", T, tm,
               [_rt(o_fw, tm), _rt(o_bw, tm), (proj, (tm, D_MODEL), lambda i: (i, 4)), _full(ng)],
               [_rt_out(T, D_MODEL, BF16, tm)], post)[0]
    z, h_out = _out_proj_ln(f"hg{l}_out", og, w["w_out"], h, g, b)
    return h_out, (h, proj, lb_row, lb_vjp, st_fw, st_bw, o_fw, o_bw, ng, og, z)


def _hg_bwd(l, dh_out, saved, w, small, g):
    h, proj, lb_row, lb_vjp, st_fw, st_bw, o_fw, o_bw, ng, og, z = saved
    T = h.shape[0]
    dz, dg, db = _ln_bwd(f"hg{l}_lnb", dh_out, z, g)
    tm = _tile(T, 512, 8)
    nn_tiles = D_MODEL // HG_DIM

    def epi(accs, of_t, ob_t, g_t, ng_t):
        dog = accs[0]
        _, ons, os_, rs = _hg_post(of_t, ob_t, g_t, ng_t)
        dos, dgs, dngs = [], [], []
        for hd in range(nn_tiles):
            sl = slice(hd * HG_DIM, (hd + 1) * HG_DIM)
            d, o, r = dog[:, sl], os_[hd], rs[hd]
            dgs.append(d * ons[hd] * _dsilu(g_t[:, sl]))
            don = d * _silu(g_t[:, sl])
            dngs.append(jnp.sum(don * o * r, axis=0, keepdims=True))
            dxn = don * ng_t
            dos.append(r * dxn - o * (r * r * r) * jnp.mean(dxn * o, axis=-1, keepdims=True))
        return [jnp.concatenate(dos, axis=1), jnp.concatenate(dgs, axis=1), _bcast8(jnp.concatenate(dngs, axis=1))]

    grid = (T // tm, 1, 1)
    row_map = lambda i, j, k: (i, 0)
    do, dg_raw, png = _mm(
        f"hg{l}_dog", grid, [(dz, (tm, D_MODEL), row_map)], [(w["w_out"], (D_MODEL, D_MODEL), lambda i, j, k: (0, 0))],
        [(0, 0, 0, "nt")], 1, (tm, D_MODEL),
        [(o_fw, (tm, D_MODEL), row_map), (o_bw, (tm, D_MODEL), row_map), (proj, (tm, D_MODEL), lambda i, j, k: (i, 4)),
         (ng, (1, HG_DIM), lambda i, j, k: (0, 0))],
        [((T, D_MODEL), F32, (tm, D_MODEL), row_map), ((T, D_MODEL), F32, (tm, D_MODEL), row_map),
         ((T // tm * 8, D_MODEL), F32, (8, D_MODEL), row_map)], epi)
    dw_out = _mm_simple(f"hg{l}_dWout", og, dz, "tn", BF16, tm=1024, tn=1024, tk=512)
    dq_f, dz_f, dv_f, dlb_f = _gla_bwd(f"hg{l}_glab_fw", proj, lb_row, do, st_fw, False)
    dq_b, dz_b, dv_b, dlb_b = _gla_bwd(f"hg{l}_glab_bw", proj, lb_row, do, st_bw, True)
    dproj = jnp.concatenate([dq_f + dq_b, dz_f, dz_b, dv_f + dv_b, dg_raw], axis=1).astype(BF16)
    dh = _mm_simple(f"hg{l}_dX", dproj, w["w_in"], "nt", F32, tk=1280, extras=(dz,),
                    epilogue=lambda accs, dz_t: [accs[0] + ALPHA * dz_t])
    dw_in = _mm_simple(f"hg{l}_dWin", h, dproj, "tn", BF16, tm=1024, tn=1280, tk=512)
    dlb_row = (dlb_f[:, 0, :] + dlb_b[:, 0, :]).reshape(1, D_MODEL)
    grads = dict(w_in=dw_in, w_out=dw_out, lower_bound=lb_vjp(dlb_row)[0],
                 norm_g=_psum_rows(png).reshape(nn_tiles, HG_DIM).sum(axis=0), ln_g=dg, ln_b=db)
    return dh, grads


MLA_W = 256
MLA_SCALE = (MLA_NOPE + MLA_ROPE) ** -0.5


def _swap_halves(a):
    n = a.shape[-1] // 2
    return jnp.concatenate([a[..., n:], a[..., :n]], axis=-1)


def _mla_ext_weights(w_in, w_q_b):
    w_in_ext = jnp.concatenate([w_in, _swap_halves(w_in[:, MLA_Q_LORA + MLA_KV_LORA:])], axis=1)
    wq = w_q_b.reshape(MLA_Q_LORA, MLA_HEADS, MLA_NOPE + MLA_ROPE)
    wq_ext = jnp.concatenate([wq, _swap_halves(wq[:, :, MLA_NOPE:])], axis=2).reshape(MLA_Q_LORA, MLA_HEADS * MLA_W)
    return w_in_ext, wq_ext


def _mla_ext_grads(dw_in_ext, dwq_ext):
    n_lat = MLA_Q_LORA + MLA_KV_LORA
    dw_in = jnp.concatenate([dw_in_ext[:, :n_lat], dw_in_ext[:, n_lat:n_lat + MLA_ROPE]
                             + _swap_halves(dw_in_ext[:, n_lat + MLA_ROPE:])], axis=1)
    dq = dwq_ext.reshape(MLA_Q_LORA, MLA_HEADS, MLA_W)
    dwq = jnp.concatenate([dq[:, :, :MLA_NOPE], dq[:, :, MLA_NOPE:MLA_NOPE + MLA_ROPE]
                           + _swap_halves(dq[:, :, MLA_NOPE + MLA_ROPE:])], axis=2)
    return dw_in, dwq.reshape(MLA_Q_LORA, MLA_HEADS * (MLA_NOPE + MLA_ROPE))


def _rope_table(positions):
    half = MLA_ROPE // 2
    inv_freq = 1.0 / (ROPE_THETA ** (jnp.arange(half, dtype=F32) * (2.0 / MLA_ROPE)))
    ang = positions.astype(F32)[:, None] * inv_freq
    cos, sin = jnp.cos(ang), jnp.sin(ang)
    return jnp.concatenate([cos, cos, -sin, sin], axis=1)


def _rope_sum(prod):
    return prod + pltpu.roll(prod, MLA_ROPE, 1)


def _low_half(a):
    lane = lax.broadcasted_iota(jnp.int32, a.shape, 1)
    return jnp.where(lane < MLA_ROPE, a, 0.0)


def _rms(x, g):
    r = lax.rsqrt(jnp.mean(x * x, axis=-1, keepdims=True) + RMS_EPS)
    return x * r * g


def _rms_bwd(x, g, dy):
    r = lax.rsqrt(jnp.mean(x * x, axis=-1, keepdims=True) + RMS_EPS)
    dxn = dy * g
    dx = r * dxn - x * (r * r * r) * jnp.mean(dxn * x, axis=-1, keepdims=True)
    return dx, jnp.sum(dy * x * r, axis=0, keepdims=True)


ATT_TILE = 512
ATT_BLOCK = 2048


def _lanes(col, n):
    return jnp.tile(col, (1, n // 128))


def _flash_fwd(name, q, k, v):
    H, T, _ = q.shape
    tq, tkb = _tile(T, ATT_TILE, 8), _tile(T, ATT_BLOCK, 128)
    ts = _tile(tkb, ATT_TILE, 128)
    nk, nsub = T // tkb, tkb // ts

    def body(q_ref, k_ref, v_ref, o_ref, lse_ref, lser_ref, m_s, l_s, acc_s):
        ki = pl.program_id(2)

        @pl.when(ki == 0)
        def _():
            m_s[...] = jnp.full(m_s.shape, -jnp.inf, F32)
            l_s[...] = jnp.zeros(l_s.shape, F32)
            acc_s[...] = jnp.zeros(acc_s.shape, F32)

        qv = q_ref[...]
        m, l, acc = m_s[...], l_s[...], acc_s[...]
        for j in range(nsub):
            kj, vj = k_ref[j * ts:(j + 1) * ts, :], v_ref[j * ts:(j + 1) * ts, :]
            s = lax.dot_general(qv, kj, (_DN["nt"], ((), ())), preferred_element_type=F32)
            m_new = jnp.maximum(m, jnp.max(s, axis=-1, keepdims=True))
            a = jnp.exp(m - m_new)
            p = jnp.exp(s - _lanes(m_new, ts))
            l = a * l + jnp.sum(p, axis=-1, keepdims=True)
            acc = a * acc + jnp.dot(p.astype(BF16), vj, preferred_element_type=F32)
            m = m_new
        m_s[...], l_s[...], acc_s[...] = m, l, acc

        @pl.when(ki == nk - 1)
        def _():
            o_ref[...] = acc / l
            lse = m + jnp.log(l)
            lse_ref[...] = lse
            lser_ref[...] = lse.T[:8, :]

    return pl.pallas_call(
        body, name=name, grid=(H, T // tq, nk),
        in_specs=[pl.BlockSpec((None, tq, MLA_W), lambda h, i, j: (h, i, 0)),
                  pl.BlockSpec((None, tkb, MLA_W), lambda h, i, j: (h, j, 0)),
                  pl.BlockSpec((None, tkb, MLA_V), lambda h, i, j: (h, j, 0))],
        out_specs=[pl.BlockSpec((tq, MLA_V), lambda h, i, j: (i, h)), pl.BlockSpec((tq, MLA_V), lambda h, i, j: (i, h)),
                   pl.BlockSpec((None, 8, tq), lambda h, i, j: (h, 0, i))],
        out_shape=[jax.ShapeDtypeStruct((T, H * MLA_V), F32), jax.ShapeDtypeStruct((T, H * MLA_V), F32),
                   jax.ShapeDtypeStruct((H, 8, T), F32)],
        scratch_shapes=[pltpu.VMEM((tq, MLA_V), F32), pltpu.VMEM((tq, MLA_V), F32), pltpu.VMEM((tq, MLA_V), F32)],
        compiler_params=_cparams(),
    )(q, k, v)


def _flash_bwd(name, q, k, v, do, o, lse, lse_rows):
    H, T, _ = q.shape
    nt = (_DN["nt"], ((), ()))
    tq, tkb = _tile(T, ATT_TILE, 8), _tile(T, ATT_BLOCK, 128)
    ts = _tile(tkb, ATT_TILE, 128)
    nkb, nsub = T // tkb, tkb // ts

    def dq_body(q_ref, k_ref, v_ref, do_ref, o_ref, lse_ref, dq_ref, dlr_ref, acc_s, dl_s):
        ki = pl.program_id(2)

        @pl.when(ki == 0)
        def _():
            acc_s[...] = jnp.zeros(acc_s.shape, F32)
            delta = jnp.sum(do_ref[...].astype(F32) * o_ref[...], axis=-1, keepdims=True)
            dl = jnp.broadcast_to(delta, dl_s.shape)
            dl_s[...] = dl
            dlr_ref[...] = dl.T[:8, :]

        qv, dob = q_ref[...], do_ref[...]
        lse_t, dl_t = _lanes(lse_ref[...], ts), _lanes(dl_s[...], ts)
        acc = acc_s[...]
        for j in range(nsub):
            kj, vj = k_ref[j * ts:(j + 1) * ts, :], v_ref[j * ts:(j + 1) * ts, :]
            s = lax.dot_general(qv, kj, nt, preferred_element_type=F32)
            dp = lax.dot_general(dob, vj, nt, preferred_element_type=F32)
            ds = jnp.exp(s - lse_t) * (dp - dl_t)
            acc = acc + jnp.dot(ds.astype(BF16), kj, preferred_element_type=F32)
        acc_s[...] = acc

        @pl.when(ki == nkb - 1)
        def _():
            dq_ref[...] = acc

    dq, delta_rows = pl.pallas_call(
        dq_body, name=name + "_dq", grid=(H, T // tq, nkb),
        in_specs=[pl.BlockSpec((None, tq, MLA_W), lambda h, i, j: (h, i, 0)),
                  pl.BlockSpec((None, tkb, MLA_W), lambda h, i, j: (h, j, 0)),
                  pl.BlockSpec((None, tkb, MLA_V), lambda h, i, j: (h, j, 0)),
                  pl.BlockSpec((tq, MLA_V), lambda h, i, j: (i, h)),
                  pl.BlockSpec((tq, MLA_V), lambda h, i, j: (i, h)),
                  pl.BlockSpec((tq, MLA_V), lambda h, i, j: (i, h))],
        out_specs=[pl.BlockSpec((None, tq, MLA_W), lambda h, i, j: (h, i, 0)),
                   pl.BlockSpec((None, 8, tq), lambda h, i, j: (h, 0, i))],
        out_shape=[jax.ShapeDtypeStruct((H, T, MLA_W), F32), jax.ShapeDtypeStruct((H, 8, T), F32)],
        scratch_shapes=[pltpu.VMEM((tq, MLA_W), F32), pltpu.VMEM((tq, MLA_V), F32)], compiler_params=_cparams(),
    )(q, k, v, do, o, lse)

    tk, tqb = _tile(T, ATT_TILE, 128), _tile(T, ATT_BLOCK, 128)
    tqs = _tile(tqb, ATT_TILE, 128)
    nqb, nqsub = T // tqb, tqb // tqs

    def dkv_body(q_ref, k_ref, v_ref, do_ref, lser_ref, dlr_ref, dk_ref, dv_ref, dk_s, dv_s):
        qi = pl.program_id(2)

        @pl.when(qi == 0)
        def _():
            dk_s[...] = jnp.zeros(dk_s.shape, F32)
            dv_s[...] = jnp.zeros(dv_s.shape, F32)

        kv, vv = k_ref[...], v_ref[...]
        dk, dv = dk_s[...], dv_s[...]
        for j in range(nqsub):
            sl = slice(j * tqs, (j + 1) * tqs)
            qj, doj = q_ref[sl, :], do_ref[sl, :]
            st = lax.dot_general(kv, qj, nt, preferred_element_type=F32)
            dpt = lax.dot_general(vv, doj, nt, preferred_element_type=F32)
            pt = jnp.exp(st - lser_ref[0:1, sl])
            dst = pt * (dpt - dlr_ref[0:1, sl])
            dv = dv + jnp.dot(pt.astype(BF16), doj, preferred_element_type=F32)
            dk = dk + jnp.dot(dst.astype(BF16), qj, preferred_element_type=F32)
        dk_s[...], dv_s[...] = dk, dv

        @pl.when(qi == nqb - 1)
        def _():
            dk_ref[...] = dk
            dv_ref[...] = dv

    dk, dv = pl.pallas_call(
        dkv_body, name=name + "_dkv", grid=(H, T // tk, nqb),
        in_specs=[pl.BlockSpec((None, tqb, MLA_W), lambda h, i, j: (h, j, 0)),
                  pl.BlockSpec((None, tk, MLA_W), lambda h, i, j: (h, i, 0)),
                  pl.BlockSpec((None, tk, MLA_V), lambda h, i, j: (h, i, 0)),
                  pl.BlockSpec((tqb, MLA_V), lambda h, i, j: (j, h)),
                  pl.BlockSpec((None, 8, tqb), lambda h, i, j: (h, 0, j)),
                  pl.BlockSpec((None, 8, tqb), lambda h, i, j: (h, 0, j))],
        out_specs=[pl.BlockSpec((None, tk, MLA_W), lambda h, i, j: (h, i, 0)),
                   pl.BlockSpec((None, tk, MLA_V), lambda h, i, j: (h, i, 0))],
        out_shape=[jax.ShapeDtypeStruct((H, T, MLA_W), F32), jax.ShapeDtypeStruct((H, T, MLA_V), F32)],
        scratch_shapes=[pltpu.VMEM((tk, MLA_W), F32), pltpu.VMEM((tk, MLA_V), F32)], compiler_params=_cparams(),
    )(q, k, v, do, lse_rows, delta_rows)
    return dq, dk, dv


def _mla_fwd(l, h, w, small, g, b, cs):
    T = h.shape[0]
    tm = _tile(T, 512, 8)
    H = MLA_HEADS
    gq, gkv = small["q_norm_g"].reshape(1, -1), small["kv_norm_g"].reshape(1, -1)
    n_ext = MLA_Q_LORA + MLA_KV_LORA + 2 * MLA_ROPE
    row = lambda i, j, k: (i, 0)
    fix = lambda i, j, k: (0, 0)

    def epi_lat(accs, gq_t, gkv_t):
        a = accs[0]
        ql, kvl = a[:, :MLA_Q_LORA], a[:, MLA_Q_LORA:MLA_Q_LORA + MLA_KV_LORA]
        return [ql, kvl, a[:, MLA_Q_LORA + MLA_KV_LORA:], _rms(ql, gq_t), _rms(kvl, gkv_t)]

    ql, kvl, kr, xq, xkv = _mm(
        f"mla{l}_in", (T // tm, 1, 1), [(h, (tm, D_MODEL), row)], [(w["w_in_ext"], (D_MODEL, n_ext), fix)],
        [(0, 0, 0, "nn")], 1, (tm, n_ext), [(gq, gq.shape, fix), (gkv, gkv.shape, fix)],
        [((T, MLA_Q_LORA), F32, (tm, MLA_Q_LORA), row), ((T, MLA_KV_LORA), F32, (tm, MLA_KV_LORA), row),
         ((T, 2 * MLA_ROPE), F32, (tm, 2 * MLA_ROPE), row), ((T, MLA_Q_LORA), BF16, (tm, MLA_Q_LORA), row),
         ((T, MLA_KV_LORA), BF16, (tm, MLA_KV_LORA), row)], epi_lat)

    def epi_q(accs, cs_t):
        a = accs[0]
        return [jnp.concatenate([a[:, :MLA_NOPE], _rope_sum(a[:, MLA_NOPE:] * cs_t)], axis=1) * MLA_SCALE]

    head_out = lambda i, j, k: (j, i, 0)
    q = _mm(f"mla{l}_q", (T // tm, H, 1), [(xq, (tm, MLA_Q_LORA), row)],
            [(w["wq_ext"], (MLA_Q_LORA, MLA_W), lambda i, j, k: (0, j))], [(0, 0, 0, "nn")], 1, (tm, MLA_W),
            [(cs, (tm, 2 * MLA_ROPE), row)], [((H, T, MLA_W), BF16, (None, tm, MLA_W), head_out)], epi_q)[0]

    def epi_kv(accs, kr_t, cs_t):
        a = accs[0]
        return [jnp.concatenate([a[:, :MLA_NOPE], _low_half(_rope_sum(kr_t * cs_t))], axis=1), a[:, MLA_NOPE:]]

    k, v = _mm(f"mla{l}_kv", (T // tm, H, 1), [(xkv, (tm, MLA_KV_LORA), row)],
               [(w["w_kv_b"], (MLA_KV_LORA, MLA_W), lambda i, j, k: (0, j))], [(0, 0, 0, "nn")], 1, (tm, MLA_W),
               [(kr, (tm, 2 * MLA_ROPE), row), (cs, (tm, 2 * MLA_ROPE), row)],
               [((H, T, MLA_W), BF16, (None, tm, MLA_W), head_out), ((H, T, MLA_V), BF16, (None, tm, MLA_V), head_out)],
               epi_kv)
    o, lse, lse_rows = _flash_fwd(f"mla{l}_attn", q, k, v)
    z, h_out = _out_proj_ln(f"mla{l}_out", o, w["w_out"], h, g, b)
    return h_out, (h, ql, kvl, xq, xkv, q, k, v, o, (lse, lse_rows), z, gq, gkv, cs)


def _mla_bwd(l, dh_out, saved, w, g):
    h, ql, kvl, xq, xkv, q, k, v, o, lse, z, gq, gkv, cs = saved
    T = h.shape[0]
    tm = _tile(T, 512, 8)
    H = MLA_HEADS
    dz, dg, db = _ln_bwd(f"mla{l}_lnb", dh_out, z, g)
    do = _mm_simple(f"mla{l}_dO", dz, w["w_out"], "nt", BF16)
    dw_out = _mm_simple(f"mla{l}_dWout", o, dz, "tn", BF16, tm=1024, tn=1024, tk=512)
    dq, dk, dv = _flash_bwd(f"mla{l}_attnb", q, k, v, do, o, lse[0], lse[1])
    head_in = lambda i, hh: (hh, i, 0)
    row2 = lambda i, hh: (i, 0)

    def fn_q(dq_t, cs_t):
        d = dq_t[:, MLA_NOPE:]
        return [jnp.concatenate([dq_t[:, :MLA_NOPE], _rope_sum(d) * cs_t], axis=1) * MLA_SCALE]

    dq_eff = _grid_call(f"mla{l}_dqeff", (T // tm, H), [(dq, (None, tm, MLA_W), head_in), (cs, (tm, 2 * MLA_ROPE), row2)],
                        [((T, H * MLA_W), BF16, (tm, MLA_W), lambda i, hh: (i, hh))], fn_q)[0]

    def fn_kv(dk_t, dv_t, cs_t):
        return [jnp.concatenate([dk_t[:, :MLA_NOPE], dv_t], axis=1), _rope_sum(_low_half(dk_t[:, MLA_NOPE:])) * cs_t]

    dkv_eff, dkr = _grid_call(
        f"mla{l}_dkveff", (T // tm, H),
        [(dk, (None, tm, MLA_W), head_in), (dv, (None, tm, MLA_V), head_in), (cs, (tm, 2 * MLA_ROPE), row2)],
        [((T, H * MLA_W), BF16, (tm, MLA_W), lambda i, hh: (i, hh)), ((T, 2 * MLA_ROPE), F32, (tm, 2 * MLA_ROPE), row2)],
        fn_kv, acc_outs=(1,))
    dxq = _mm_simple(f"mla{l}_dxq", dq_eff, w["wq_ext"], "nt", F32, tn=MLA_Q_LORA)
    dwq_ext = _mm_simple(f"mla{l}_dWq", xq, dq_eff, "tn", BF16, tm=MLA_Q_LORA, tn=1024, tk=512)
    dxkv = _mm_simple(f"mla{l}_dxkv", dkv_eff, w["w_kv_b"], "nt", F32, tn=MLA_KV_LORA)
    dwkv = _mm_simple(f"mla{l}_dWkv", xkv, dkv_eff, "tn", BF16, tm=MLA_KV_LORA, tn=1024, tk=512)

    def fn_lat(ql_t, dxq_t, gq_t, kvl_t, dxkv_t, gkv_t, dkr_t):
        dql, dgq = _rms_bwd(ql_t, gq_t, dxq_t)
        dkvl, dgkv = _rms_bwd(kvl_t, gkv_t, dxkv_t)
        return [jnp.concatenate([dql, dkvl, dkr_t], axis=1), _bcast8(dgq), _bcast8(dgkv)]

    n_ext = MLA_Q_LORA + MLA_KV_LORA + 2 * MLA_ROPE
    dlat, pgq, pgkv = _rows(f"mla{l}_dlat", T, tm,
                            [_rt(ql, tm), _rt(dxq, tm), _full(gq), _rt(kvl, tm), _rt(dxkv, tm), _full(gkv), _rt(dkr, tm)],
                            [_rt_out(T, n_ext, BF16, tm), _ps_out(T, MLA_Q_LORA, tm), _ps_out(T, MLA_KV_LORA, tm)], fn_lat)
    dh = _mm_simple(f"mla{l}_dX", dlat, w["w_in_ext"], "nt", F32, tk=n_ext, extras=(dz,),
                    epilogue=lambda accs, dz_t: [accs[0] + ALPHA * dz_t])
    dw_in_ext = _mm_simple(f"mla{l}_dWin", h, dlat, "tn", BF16, tm=1024, tn=n_ext, tk=512)
    dw_in, dwq = _mla_ext_grads(dw_in_ext.astype(F32), dwq_ext.astype(F32))
    grads = dict(w_in=dw_in, w_q_b=dwq, w_kv_b=dwkv, w_out=dw_out, q_norm_g=_psum_rows(pgq), kv_norm_g=_psum_rows(pgkv),
                 ln_g=dg, ln_b=db)
    return dh, grads


def _loss_head(y, target):
    T = y.shape[0]
    tm = _tile(T, 512, 8)

    def fn(y_t, t_t):
        d = y_t - t_t
        part = 0.5 * jnp.sum(jnp.mean(d * d, axis=-1, keepdims=True), axis=0, keepdims=True)
        return [d * (1.0 / D_MODEL), jnp.broadcast_to(part, (8, 128))]

    dy, part = _rows("loss_head", T, tm, [_rt(y, tm), _rt(target, tm)],
                     [_rt_out(T, D_MODEL, F32, tm), ((T // tm * 8, 128), F32, (8, 128), lambda i: (i, 0))], fn)
    return jnp.sum(part.reshape(-1, 8, 128)[:, 0, 0]), dy


_S5_SMALL = ("lam_re", "lam_im", "log_step", "b_re", "b_im", "c_re", "c_im", "d", "b_glu")


def _local_step(x, positions, target, W):
    row = lambda a, i: a[i].reshape(1, -1)
    w_in_ext, wq_ext = _mla_ext_weights(W["mla_w_in"][0], W["mla_w_q_b"][0])
    cs = _rope_table(positions)
    h, saves = x, []
    for l in range(DEPTH):
        kind, slot = LAYER_MIXER[l], l // 3
        g, b = row(W["ln_mix_g"], l), row(W["ln_mix_b"], l)
        if kind == 0:
            w = {k: W["s5_" + k][slot] for k in ("w_in", "w_glu", "w_out")}
            small = {k: W["s5_" + k][slot] for k in _S5_SMALL}
            h, sv = _s5_fwd(l, h, w, small, g, b)
        elif kind == 1:
            w = dict(w_in=W["hg_w_in"][slot], w_out=W["hg_w_out"][slot])
            small = dict(lower_bound=W["hg_lower_bound"], norm_g=W["hg_norm_g"][slot])
            h, sv = _hg_fwd(l, h, w, small, g, b)
        else:
            w = dict(w_in_ext=w_in_ext, wq_ext=wq_ext, w_kv_b=W["mla_w_kv_b"][slot], w_out=W["mla_w_out"][slot])
            small = dict(q_norm_g=W["mla_q_norm_g"][slot], kv_norm_g=W["mla_kv_norm_g"][slot])
            h, sv = _mla_fwd(l, h, w, small, g, b, cs)
        h, fsv = _ffn_fwd(l, h, W["ffn_w_in"][l], W["ffn_w_out"][l], row(W["ln_ffn_g"], l), row(W["ln_ffn_b"], l))
        saves.append((w, small, sv, fsv))
    loss, dh = _loss_head(h, target)
    per_layer = [None] * DEPTH
    for l in reversed(range(DEPTH)):
        kind = LAYER_MIXER[l]
        w, small, sv, fsv = saves[l]
        dh, gf = _ffn_bwd(l, dh, fsv, W["ffn_w_in"][l], W["ffn_w_out"][l], row(W["ln_ffn_g"], l))
        g = row(W["ln_mix_g"], l)
        if kind == 0:
            dh, gm = _s5_bwd(l, dh, sv, w, small, g)
        elif kind == 1:
            dh, gm = _hg_bwd(l, dh, sv, w, small, g)
        else:
            dh, gm = _mla_bwd(l, dh, sv, w, g)
        per_layer[l] = (gm, gf)
    grads = {}
    stack = lambda xs: jnp.stack([a.astype(F32) if a.dtype != BF16 else a for a in xs])
    grads["ln_mix_g"] = stack([per_layer[l][0]["ln_g"] for l in range(DEPTH)])
    grads["ln_mix_b"] = stack([per_layer[l][0]["ln_b"] for l in range(DEPTH)])
    grads["ln_ffn_g"] = stack([per_layer[l][1]["ln_g"] for l in range(DEPTH)])
    grads["ln_ffn_b"] = stack([per_layer[l][1]["ln_b"] for l in range(DEPTH)])
    grads["ffn_w_in"] = stack([per_layer[l][1]["w_in"] for l in range(DEPTH)])
    grads["ffn_w_out"] = stack([per_layer[l][1]["w_out"] for l in range(DEPTH)])
    s5_layers = [l for l in range(DEPTH) if LAYER_MIXER[l] == 0]
    for k in ("w_in", "w_glu", "w_out") + _S5_SMALL:
        grads["s5_" + k] = stack([per_layer[l][0][k] for l in s5_layers])
    hg = per_layer[1][0]
    grads["hg_w_in"], grads["hg_w_out"] = hg["w_in"][None], hg["w_out"][None]
    grads["hg_lower_bound"], grads["hg_norm_g"] = hg["lower_bound"], hg["norm_g"][None]
    for k, v in per_layer[2][0].items():
        if not k.startswith("ln_"):
            grads["mla_" + k] = v[None]
    return loss, dh, grads


def _here():
    return lax.axis_index("x"), lax.axis_index("y"), lax.axis_index("c")


def _any_spec():
    return pl.BlockSpec(memory_space=pl.ANY)


def _chip_exchange(name, xs, scatter):
    n = len(xs)

    def body(*refs):
        ins, outs = refs[:n], refs[n:2 * n]
        send_sems, recv_sems, loc_sems = refs[2 * n:]
        x, y, c = _here()
        me = 2 * x + y
        peers = [(1 - x, y), (x, 1 - y), (1 - x, 1 - y)]
        copies = []
        for t in range(n):
            src_of = (lambda p, t=t: ins[t].at[p]) if scatter else (lambda p, t=t: ins[t])
            loc = pltpu.make_async_copy(src_of(me), outs[t].at[me], loc_sems.at[t])
            loc.start()
            copies.append(loc)
            for j, (px, py) in enumerate(peers):
                cp = pltpu.make_async_remote_copy(
                    src_ref=src_of(2 * px + py), dst_ref=outs[t].at[me], send_sem=send_sems.at[t, j],
                    recv_sem=recv_sems.at[t, j], device_id=(px, py, c), device_id_type=MESH)
                cp.start()
                copies.append(cp)
        for cp in copies:
            cp.wait()

    out_shape = [jax.ShapeDtypeStruct(a.shape if scatter else (N_CHIPS,) + a.shape, a.dtype) for a in xs]
    return pl.pallas_call(
        body, name=name, in_specs=[_any_spec()] * n, out_specs=[_any_spec()] * n, out_shape=out_shape,
        scratch_shapes=[pltpu.SemaphoreType.DMA((n, 3)), pltpu.SemaphoreType.DMA((n, 3)), pltpu.SemaphoreType.DMA((n,))],
    )(*xs)


def _core_exchange(name, a):
    def body(a_ref, o_ref, send_sem, recv_sem, loc_sem):
        x, y, c = _here()
        loc = pltpu.make_async_copy(a_ref, o_ref.at[c], loc_sem)
        loc.start()
        cp = pltpu.make_async_remote_copy(src_ref=a_ref, dst_ref=o_ref.at[c], send_sem=send_sem, recv_sem=recv_sem,
                                          device_id=(x, y, 1 - c), device_id_type=MESH)
        cp.start()
        cp.wait()
        loc.wait()

    return pl.pallas_call(
        body, name=name, in_specs=[_any_spec()], out_specs=_any_spec(),
        out_shape=jax.ShapeDtypeStruct((2,) + a.shape, a.dtype),
        scratch_shapes=[pltpu.SemaphoreType.DMA, pltpu.SemaphoreType.DMA, pltpu.SemaphoreType.DMA],
    )(a)


def _all_exchange(name, a):
    def body(a_ref, o_ref, send_sems, recv_sems, loc_sem):
        x, y, c = _here()
        me = 4 * x + 2 * y + c
        loc = pltpu.make_async_copy(a_ref, o_ref.at[me], loc_sem)
        loc.start()
        copies = [loc]
        for mask in range(1, N_DEV):
            fx, fy, fc = (mask >> 2) & 1, (mask >> 1) & 1, mask & 1
            peer = (1 - x if fx else x, 1 - y if fy else y, 1 - c if fc else c)
            cp = pltpu.make_async_remote_copy(src_ref=a_ref, dst_ref=o_ref.at[me], send_sem=send_sems.at[mask - 1],
                                              recv_sem=recv_sems.at[mask - 1], device_id=peer, device_id_type=MESH)
            cp.start()
            copies.append(cp)
        for cp in copies:
            cp.wait()

    return pl.pallas_call(
        body, name=name, in_specs=[_any_spec()], out_specs=_any_spec(),
        out_shape=jax.ShapeDtypeStruct((N_DEV,) + a.shape, a.dtype),
        scratch_shapes=[pltpu.SemaphoreType.DMA((N_DEV - 1,)), pltpu.SemaphoreType.DMA((N_DEV - 1,)), pltpu.SemaphoreType.DMA],
    )(a)


def _sum_leading(name, a, out_dtype=F32):
    n, R, C = a.shape
    tr = _tile(R, 512, 16)

    def fn(a_t):
        s = a_t[0].astype(F32)
        for i in range(1, n):
            s = s + a_t[i].astype(F32)
        return [s]

    return _grid_call(name, (R // tr,), [(a, (n, tr, C), lambda i: (0, i, 0))],
                      [((R, C), out_dtype, (tr, C), lambda i: (i, 0))], fn)[0]


def _adamw(name, g_parts, w, m, v):
    R, C = w.shape
    tr = _tile(R, 256, 8)
    ng = len(g_parts)
    c1 = 1.0 / (1.0 - ADAM_B1 ** ADAM_STEP)
    c2 = 1.0 / (1.0 - ADAM_B2 ** ADAM_STEP)

    def fn(*tiles):
        g = tiles[0]
        for t in tiles[1:ng]:
            g = g + t
        w_t, m_t, v_t = tiles[ng:]
        m_n = ADAM_B1 * m_t + (1.0 - ADAM_B1) * g
        v_n = ADAM_B2 * v_t + (1.0 - ADAM_B2) * (g * g)
        delta = -ADAM_LR * ((m_n * c1) / (jnp.sqrt(v_n * c2) + ADAM_EPS) + ADAM_WD * w_t)
        return [g, delta, m_n, v_n]

    spec = lambda a: (a, (tr, C), lambda i: (i, 0))
    return _grid_call(name, (R // tr,), [spec(a) for a in list(g_parts) + [w, m, v]],
                      [((R, C), F32, (tr, C), lambda i: (i, 0))] * 4, fn)


_WEIGHTS = ("ln_mix_g", "ln_mix_b", "ln_ffn_g", "ln_ffn_b", "ffn_w_in", "ffn_w_out", "s5_w_in", "s5_lam_re", "s5_lam_im",
            "s5_log_step", "s5_b_re", "s5_b_im", "s5_c_re", "s5_c_im", "s5_d", "s5_w_glu", "s5_b_glu", "s5_w_out", "hg_w_in",
            "hg_lower_bound", "hg_norm_g", "hg_w_out", "mla_w_in", "mla_q_norm_g", "mla_w_q_b", "mla_kv_norm_g", "mla_w_kv_b",
            "mla_w_out")
_BIG = {"ffn_w_in": 2, "ffn_w_out": 1, "s5_w_in": 1, "s5_w_glu": 1, "s5_w_out": 1, "hg_w_in": 2, "hg_w_out": 1,
        "mla_w_in": 1, "mla_w_q_b": 2, "mla_w_kv_b": 2, "mla_w_out": 1}
_SMALL_SHARDED = {"s5_d": 1, "s5_b_glu": 1, "mla_q_norm_g": 1, "mla_kv_norm_g": 1}
_REPLICATED = tuple(n for n in _WEIGHTS if n not in _BIG and n not in _SMALL_SHARDED)
LANES = 1024


def _pack(arrs, dtype, row_mult, lead=0):
    rows, segs, r = [], [], 0
    for a in arrs:
        lead_shape = a.shape[:lead]
        flat = a.astype(dtype).reshape(lead_shape + (-1,))
        n = -(-flat.shape[-1] // LANES)
        flat = jnp.pad(flat, [(0, 0)] * lead + [(0, n * LANES - flat.shape[-1])])
        rows.append(flat.reshape(lead_shape + (n, LANES)))
        segs.append((r, n))
        r += n
    pad = -r % row_mult
    if pad:
        rows.append(jnp.zeros(rows[0].shape[:lead] + (pad, LANES), dtype))
    return jnp.concatenate(rows, axis=lead), segs


def _unpack(packed, segs, shapes):
    out = []
    for (r0, n), shp in zip(segs, shapes):
        size = int(np.prod(shp))
        out.append(packed[..., r0:r0 + n, :].reshape(packed.shape[:-2] + (n * LANES,))[..., :size].reshape(packed.shape[:-2] + tuple(shp)))
    return out


def _unshard(stacked, axis):
    moved = jnp.moveaxis(stacked, 0, axis)
    shp = list(moved.shape)
    return moved.reshape(shp[:axis] + [shp[axis] * shp[axis + 1]] + shp[axis + 2:])


def _shard_split(full, axis):
    shp = list(full.shape)
    a = full.reshape(shp[:axis] + [N_CHIPS, shp[axis] // N_CHIPS] + shp[axis + 1:])
    return jnp.moveaxis(a, axis, 0)


def _train_step(x, positions, target, w, m, v):
    big, small_sh = list(_BIG), list(_SMALL_SHARDED)
    chip = 2 * lax.axis_index("x") + lax.axis_index("y")

    big_pack, big_segs = _pack([w[n] for n in big], BF16, 16)
    sm_pack, sm_segs = _pack([w[n] for n in small_sh], F32, 8)
    big_all, sm_all = _chip_exchange("gather_weights", [big_pack, sm_pack], scatter=False)
    W = {n: w[n] for n in _REPLICATED}
    for n, s in zip(big, _unpack(big_all, big_segs, [w[n].shape for n in big])):
        W[n] = _unshard(s, _BIG[n])
    for n, s in zip(small_sh, _unpack(sm_all, sm_segs, [w[n].shape for n in small_sh])):
        W[n] = _unshard(s, _SMALL_SHARDED[n])

    loss_local, grad_x, G = _local_step(x, positions, target, W)
    loss = lax.psum(loss_local, ("x", "y", "c"))
    out = {}

    g_pack, _ = _pack([_shard_split(G[n].astype(BF16), _BIG[n]) for n in big], BF16, 16, lead=1)
    recv = _chip_exchange("scatter_grads", [g_pack], scatter=True)[0]
    pair = _core_exchange("swap_core_sums", _sum_leading("sum_chips", recv))
    for n, (r0, nr) in zip(big, big_segs):
        as_rows = lambda a: a.reshape(nr, LANES)
        res = _adamw("adamw_" + n, [pair[0, r0:r0 + nr], pair[1, r0:r0 + nr]], as_rows(w[n]), as_rows(m[n]), as_rows(v[n]))
        out[n] = tuple(r.reshape(w[n].shape) for r in res)

    small = list(_REPLICATED) + small_sh
    s_pack, s_segs = _pack([G[n] for n in small], F32, 16)
    total = _sum_leading("sum_small", _all_exchange("gather_small_grads", s_pack))
    g_small = dict(zip(small, _unpack(total, s_segs, [G[n].shape for n in small])))
    for n in small_sh:
        width = w[n].shape[1]
        g_small[n] = lax.dynamic_slice_in_dim(g_small[n], chip * width, width, axis=1)
    packs = [_pack([d[n] for n in small], F32, 8)[0] for d in (g_small, w, m, v)]
    _, a_segs = _pack([w[n] for n in small], F32, 8)
    res = _adamw("adamw_small", [packs[0]], packs[1], packs[2], packs[3])
    unpacked = [_unpack(r, a_segs, [w[n].shape for n in small]) for r in res]
    for i, n in enumerate(small):
        out[n] = tuple(u[i] for u in unpacked)
    return loss, grad_x, out


def kernel(x, positions, ln_mix_g, ln_mix_b, ln_ffn_g, ln_ffn_b, ffn_w_in, ffn_w_out, s5_w_in, s5_lam_re, s5_lam_im,
           s5_log_step, s5_b_re, s5_b_im, s5_c_re, s5_c_im, s5_d, s5_w_glu, s5_b_glu, s5_w_out, hg_w_in,
           hg_lower_bound, hg_norm_g, hg_w_out, mla_w_in, mla_q_norm_g, mla_w_q_b, mla_kv_norm_g, mla_w_kv_b,
           mla_w_out, loss_target, m_ln_mix_g, m_ln_mix_b, m_ln_ffn_g, m_ln_ffn_b, m_ffn_w_in, m_ffn_w_out, m_s5_w_in,
           m_s5_lam_re, m_s5_lam_im, m_s5_log_step, m_s5_b_re, m_s5_b_im, m_s5_c_re, m_s5_c_im, m_s5_d, m_s5_w_glu,
           m_s5_b_glu, m_s5_w_out, m_hg_w_in, m_hg_lower_bound, m_hg_norm_g, m_hg_w_out, m_mla_w_in, m_mla_q_norm_g,
           m_mla_w_q_b, m_mla_kv_norm_g, m_mla_w_kv_b, m_mla_w_out, v_ln_mix_g, v_ln_mix_b, v_ln_ffn_g, v_ln_ffn_b,
           v_ffn_w_in, v_ffn_w_out, v_s5_w_in, v_s5_lam_re, v_s5_lam_im, v_s5_log_step, v_s5_b_re, v_s5_b_im,
           v_s5_c_re, v_s5_c_im, v_s5_d, v_s5_w_glu, v_s5_b_glu, v_s5_w_out, v_hg_w_in, v_hg_lower_bound, v_hg_norm_g,
           v_hg_w_out, v_mla_w_in, v_mla_q_norm_g, v_mla_w_q_b, v_mla_kv_norm_g, v_mla_w_kv_b, v_mla_w_out):
    args = (ln_mix_g, ln_mix_b, ln_ffn_g, ln_ffn_b, ffn_w_in, ffn_w_out, s5_w_in, s5_lam_re, s5_lam_im,
            s5_log_step, s5_b_re, s5_b_im, s5_c_re, s5_c_im, s5_d, s5_w_glu, s5_b_glu, s5_w_out, hg_w_in,
            hg_lower_bound, hg_norm_g, hg_w_out, mla_w_in, mla_q_norm_g, mla_w_q_b, mla_kv_norm_g, mla_w_kv_b,
            mla_w_out, m_ln_mix_g, m_ln_mix_b, m_ln_ffn_g, m_ln_ffn_b, m_ffn_w_in, m_ffn_w_out,
            m_s5_w_in, m_s5_lam_re, m_s5_lam_im, m_s5_log_step, m_s5_b_re, m_s5_b_im, m_s5_c_re, m_s5_c_im, m_s5_d,
            m_s5_w_glu, m_s5_b_glu, m_s5_w_out, m_hg_w_in, m_hg_lower_bound, m_hg_norm_g, m_hg_w_out, m_mla_w_in,
            m_mla_q_norm_g, m_mla_w_q_b, m_mla_kv_norm_g, m_mla_w_kv_b, m_mla_w_out, v_ln_mix_g, v_ln_mix_b,
            v_ln_ffn_g, v_ln_ffn_b, v_ffn_w_in, v_ffn_w_out, v_s5_w_in, v_s5_lam_re, v_s5_lam_im, v_s5_log_step,
            v_s5_b_re, v_s5_b_im, v_s5_c_re, v_s5_c_im, v_s5_d, v_s5_w_glu, v_s5_b_glu, v_s5_w_out, v_hg_w_in,
            v_hg_lower_bound, v_hg_norm_g, v_hg_w_out, v_mla_w_in, v_mla_q_norm_g, v_mla_w_q_b, v_mla_kv_norm_g,
            v_mla_w_kv_b, v_mla_w_out)
    nw = len(_WEIGHTS)
    w = dict(zip(_WEIGHTS, args[:nw]))
    m = dict(zip(_WEIGHTS, args[nw:2 * nw]))
    v = dict(zip(_WEIGHTS, args[2 * nw:]))
    loss, grad_x, out = _train_step(x[0], positions[0], loss_target[0], w, m, v)
    res = [loss, grad_x[None]]
    for i in range(4):
        res += [out[n][i] for n in _WEIGHTS]
    return tuple(res)
```

```python
import functools
import math

import numpy as np
import jax
import jax.numpy as jnp
from jax import lax
from jax.experimental import pallas as pl
from jax.experimental.pallas import tpu as pltpu

F32 = jnp.float32
BF16 = jnp.bfloat16

D_MODEL = 1024
DEPTH = 4
LAYER_MIXER = (0, 1, 2, 0)
S5_GROUP = 16
S5_GROUPS = 64
S5_STATE = 64
S5_CHUNK = 16
HG_HEADS = 8
HG_DIM = 128
HG_CHUNK = 128
MLA_HEADS = 8
MLA_NOPE = 128
MLA_ROPE = 64
MLA_V = 128
MLA_Q_LORA = 384
MLA_KV_LORA = 256
ROPE_THETA = 10000.0
FFN_HIDDEN = 2816
ALPHA = (2 * DEPTH) ** 0.25
LN_EPS = 1e-5
RMS_EPS = 1e-6
ADAM_LR, ADAM_B1, ADAM_B2, ADAM_EPS, ADAM_WD, ADAM_STEP = 0.001, 0.9, 0.999, 1e-08, 0.01, 10
VMEM_LIMIT_BYTES = 56 * 1024 * 1024
MESH = pl.DeviceIdType.MESH
N_CHIPS = 4
N_DEV = 8


def _cparams():
    return pltpu.CompilerParams(vmem_limit_bytes=VMEM_LIMIT_BYTES)


def _tile(n, want, mult):
    t = min(want, n)
    t -= t % mult
    while t >= mult:
        if n % t == 0:
            return t
        t -= mult
    return n


def _sigmoid(x):
    return 1.0 / (1.0 + jnp.exp(-x))


def _silu(x):
    return x * _sigmoid(x)


def _dsilu(x):
    s = _sigmoid(x)
    return s * (1.0 + x * (1.0 - s))


_GELU_C = math.sqrt(2.0 / math.pi)


def _gelu(x):
    return 0.5 * x * (1.0 + jnp.tanh(_GELU_C * (x + 0.044715 * x * x * x)))


def _dgelu(x):
    t = jnp.tanh(_GELU_C * (x + 0.044715 * x * x * x))
    return 0.5 * (1.0 + t) + 0.5 * x * (1.0 - t * t) * _GELU_C * (1.0 + 3 * 0.044715 * x * x)


def _layer_norm(z, g, b):
    mu = jnp.mean(z, axis=-1, keepdims=True)
    zc = z - mu
    var = jnp.mean(zc * zc, axis=-1, keepdims=True)
    return zc * lax.rsqrt(var + LN_EPS) * g + b


def _bcast8(row):
    return jnp.broadcast_to(row, (8, row.shape[-1]))


def _stack_rows8(rows):
    n = rows[0].shape[-1]
    idx = lax.broadcasted_iota(jnp.int32, (8, n), 0)
    out = jnp.zeros((8, n), F32)
    for i, r in enumerate(rows):
        out = jnp.where(idx == i, _bcast8(r), out)
    return out


def _psum_rows(a):
    return a.reshape(-1, 8, a.shape[-1])[:, 0, :].sum(axis=0)


_DN = {"nn": ((1,), (0,)), "nt": ((1,), (1,)), "tn": ((0,), (0,))}


def _mm(name, grid, a_defs, b_defs, pairs, n_acc, acc_shape, extra_defs, out_defs, epilogue):
    nk = grid[2]
    na, nb, ne, no = len(a_defs), len(b_defs), len(extra_defs), len(out_defs)

    def body(*refs):
        a_refs = refs[:na]
        b_refs = refs[na:na + nb]
        e_refs = refs[na + nb:na + nb + ne]
        o_refs = refs[na + nb + ne:na + nb + ne + no]
        acc = refs[-1]
        k = pl.program_id(2)

        @pl.when(k == 0)
        def _():
            acc[...] = jnp.zeros(acc.shape, F32)

        for (ai, bi, ci, mode) in pairs:
            a = a_refs[ai][...].astype(BF16)
            b = b_refs[bi][...].astype(BF16)
            acc[ci] += lax.dot_general(a, b, (_DN[mode], ((), ())), preferred_element_type=F32)

        @pl.when(k == nk - 1)
        def _():
            outs = epilogue([acc[c] for c in range(n_acc)], *[e[...] for e in e_refs])
            for o_ref, o in zip(o_refs, outs):
                o_ref[...] = o.astype(o_ref.dtype)

    in_specs = [pl.BlockSpec(d[1], d[2]) for d in list(a_defs) + list(b_defs) + list(extra_defs)]
    out_specs = [pl.BlockSpec(d[2], d[3]) for d in out_defs]
    out_shape = [jax.ShapeDtypeStruct(d[0], d[1]) for d in out_defs]
    res = pl.pallas_call(
        body, name=name, grid=grid, in_specs=in_specs, out_specs=out_specs, out_shape=out_shape,
        scratch_shapes=[pltpu.VMEM((n_acc,) + tuple(acc_shape), F32)], compiler_params=_cparams(),
    )(*[d[0] for d in list(a_defs) + list(b_defs) + list(extra_defs)])
    return res


def _mm_simple(name, a, b, mode, out_dtype, tm=512, tn=1024, tk=1024, extras=(), epilogue=None, n_out=1,
               out_dtypes=None, psum_outs=0, j_outer=False):
    if mode == "nn":
        (M, K), (K2, N) = a.shape, b.shape
    elif mode == "nt":
        (M, K), (N, K2) = a.shape, b.shape
    else:
        (K, M), (K2, N) = a.shape, b.shape
    assert K == K2, (name, a.shape, b.shape, mode)
    tm, tn, tk = _tile(M, tm, 8), _tile(N, tn, 128), _tile(K, tk, 128)
    grid = (M // tm, N // tn, K // tk)
    if mode == "nn":
        a_def = (a, (tm, tk), lambda i, j, k: (i, k))
        b_def = (b, (tk, tn), lambda i, j, k: (k, j))
    elif mode == "nt":
        a_def = (a, (tm, tk), lambda i, j, k: (i, k))
        b_def = (b, (tn, tk), lambda i, j, k: (j, k))
    else:
        a_def = (a, (tk, tm), lambda i, j, k: (k, i))
        b_def = (b, (tk, tn), lambda i, j, k: (k, j))
    extra_defs = []
    for e in extras:
        if e.shape[0] == 1:
            extra_defs.append((e, (1, tn), lambda i, j, k: (0, j)))
        else:
            extra_defs.append((e, (tm, tn), lambda i, j, k: (i, j)))
    out_dtypes = out_dtypes or [out_dtype] * n_out
    out_defs = [((M, N), dt, (tm, tn), lambda i, j, k: (i, j)) for dt in out_dtypes]
    out_defs += [((M // tm * 8, N), F32, (8, tn), lambda i, j, k: (i, j)) for _ in range(psum_outs)]
    if epilogue is None:
        epilogue = lambda accs: [accs[0]]
    a_defs, b_defs = [a_def], [b_def]
    if j_outer:
        swap = lambda d: d[:-1] + ((lambda f: lambda j, i, k: f(i, j, k))(d[-1]),)
        grid = (grid[1], grid[0], grid[2])
        a_defs, b_defs = [swap(a_def)], [swap(b_def)]
        extra_defs, out_defs = [swap(d) for d in extra_defs], [swap(d) for d in out_defs]
    res = _mm(name, grid, a_defs, b_defs, [(0, 0, 0, mode)], 1, (tm, tn), extra_defs, out_defs, epilogue)
    return res[0] if len(res) == 1 else res


def _rows(name, T, tm, in_defs, out_defs, fn):
    ni = len(in_defs)

    def body(*refs):
        outs = fn(*[r[...] for r in refs[:ni]])
        for o_ref, o in zip(refs[ni:], outs):
            o_ref[...] = o.astype(o_ref.dtype)

    res = pl.pallas_call(
        body, name=name, grid=(T // tm,),
        in_specs=[pl.BlockSpec(d[1], d[2]) for d in in_defs],
        out_specs=[pl.BlockSpec(d[2], d[3]) for d in out_defs],
        out_shape=[jax.ShapeDtypeStruct(d[0], d[1]) for d in out_defs],
        compiler_params=_cparams(),
    )(*[d[0] for d in in_defs])
    return res


def _grid_call(name, grid, in_defs, out_defs, fn, acc_outs=()):
    ni = len(in_defs)

    def body(*refs):
        outs = fn(*[r[...] for r in refs[:ni]])
        first = pl.program_id(len(grid) - 1) == 0
        for idx, (o_ref, o) in enumerate(zip(refs[ni:], outs)):
            if idx in acc_outs:
                @pl.when(first)
                def _(o_ref=o_ref, o=o):
                    o_ref[...] = o.astype(o_ref.dtype)

                @pl.when(jnp.logical_not(first))
                def _(o_ref=o_ref, o=o):
                    o_ref[...] += o.astype(o_ref.dtype)
            else:
                o_ref[...] = o.astype(o_ref.dtype)

    return pl.pallas_call(
        body, name=name, grid=grid,
        in_specs=[pl.BlockSpec(d[1], d[2]) for d in in_defs],
        out_specs=[pl.BlockSpec(d[2], d[3]) for d in out_defs],
        out_shape=[jax.ShapeDtypeStruct(d[0], d[1]) for d in out_defs],
        compiler_params=_cparams(),
    )(*[d[0] for d in in_defs])


def _rt(a, tm):
    return (a, (tm, a.shape[1]), lambda i: (i, 0))


def _full(a):
    return (a, a.shape, lambda i: (0,) * a.ndim)


def _rt_out(T, n, dt, tm):
    return ((T, n), dt, (tm, n), lambda i: (i, 0))


def _ps_out(T, n, tm):
    return ((T // tm * 8, n), F32, (8, n), lambda i: (i, 0))


def _out_proj_ln(name, a, w, h_in, g, b):
    def epi(accs, h_t, g_t, b_t):
        z = ALPHA * h_t + accs[0]
        return [z, _layer_norm(z, g_t, b_t)]
    return _mm_simple(name, a, w, "nn", F32, tm=512, tn=D_MODEL, tk=FFN_HIDDEN, extras=(h_in, g, b), epilogue=epi, n_out=2)


def _ln_bwd(name, dh, z, g):
    T = dh.shape[0]
    tm = _tile(T, 512, 8)

    def fn(dh_t, z_t, g_t):
        mu = jnp.mean(z_t, axis=-1, keepdims=True)
        zc = z_t - mu
        var = jnp.mean(zc * zc, axis=-1, keepdims=True)
        rstd = lax.rsqrt(var + LN_EPS)
        xhat = zc * rstd
        dxh = dh_t * g_t
        m1 = jnp.mean(dxh, axis=-1, keepdims=True)
        m2 = jnp.mean(dxh * xhat, axis=-1, keepdims=True)
        dz = rstd * (dxh - m1 - xhat * m2)
        return [dz, _bcast8(jnp.sum(dh_t * xhat, axis=0, keepdims=True)), _bcast8(jnp.sum(dh_t, axis=0, keepdims=True))]

    dz, pg, pb = _rows(name, T, tm, [_rt(dh, tm), _rt(z, tm), _full(g)],
                       [_rt_out(T, D_MODEL, F32, tm), _ps_out(T, D_MODEL, tm), _ps_out(T, D_MODEL, tm)], fn)
    return dz, _psum_rows(pg), _psum_rows(pb)


def _ffn_fwd(l, h, w_in, w_out, g, b):
    T = h.shape[0]
    tm, tn = _tile(T, 512, 8), 1408
    nj = FFN_HIDDEN // tn
    grid = (nj, T // tm, 1)

    def epi(accs):
        gg, uu = accs
        return [gg, uu, _silu(gg) * uu]

    G, U, A = _mm(
        f"ffn{l}_in", grid, [(h, (tm, D_MODEL), lambda j, i, k: (i, 0))],
        [(w_in, (D_MODEL, tn), lambda j, i, k: (0, j)), (w_in, (D_MODEL, tn), lambda j, i, k: (0, j + nj))],
        [(0, 0, 0, "nn"), (0, 1, 1, "nn")], 2, (tm, tn), [],
        [((T, FFN_HIDDEN), F32, (tm, tn), lambda j, i, k: (i, j)),
         ((T, FFN_HIDDEN), F32, (tm, tn), lambda j, i, k: (i, j)),
         ((T, FFN_HIDDEN), BF16, (tm, tn), lambda j, i, k: (i, j))], epi)
    z, h_out = _out_proj_ln(f"ffn{l}_out", A, w_out, h, g, b)
    return h_out, (h, G, U, A, z)


def _ffn_bwd(l, dh_out, saved, w_in, w_out, g):
    h, G, U, A, z = saved
    T = h.shape[0]
    dz, dg, db = _ln_bwd(f"ffn{l}_lnb", dh_out, z, g)

    def epi(accs, g_t, u_t):
        da = accs[0]
        return [da * u_t * _dsilu(g_t), da * _silu(g_t)]

    dG, dU = _mm_simple(f"ffn{l}_dA", dz, w_out, "nt", BF16, tm=512, tn=1408, tk=1024, extras=(G, U), epilogue=epi, n_out=2,
                        j_outer=True)
    dw_out = _mm_simple(f"ffn{l}_dWout", A, dz, "tn", BF16, tm=1408, tn=1024, tk=512)
    dw_g = _mm_simple(f"ffn{l}_dWg", h, dG, "tn", BF16, tm=1024, tn=1408, tk=512)
    dw_u = _mm_simple(f"ffn{l}_dWu", h, dU, "tn", BF16, tm=1024, tn=1408, tk=512)
    tm, tk = _tile(T, 256, 8), FFN_HIDDEN
    dh = _mm(
        f"ffn{l}_dX", (T // tm, 1, 1),
        [(dG, (tm, tk), lambda i, j, k: (i, 0)), (dU, (tm, tk), lambda i, j, k: (i, 0))],
        [(w_in, (D_MODEL, tk), lambda i, j, k: (0, 0)), (w_in, (D_MODEL, tk), lambda i, j, k: (0, 1))],
        [(0, 0, 0, "nt"), (1, 1, 0, "nt")], 1, (tm, D_MODEL),
        [(dz, (tm, D_MODEL), lambda i, j, k: (i, 0))],
        [((T, D_MODEL), F32, (tm, D_MODEL), lambda i, j, k: (i, 0))],
        lambda accs, dz_t: [accs[0] + ALPHA * dz_t])[0]
    return dh, dict(w_in=jnp.concatenate([dw_g, dw_u], axis=1), w_out=dw_out, ln_g=dg, ln_b=db)


def _s5_matrices(lam_re, lam_im, log_step, b_re, b_im, c_re, c_im):
    L, hp = S5_CHUNK, lax.Precision.HIGHEST
    out = {}
    mt_total = 0.0
    for d in range(2):
        lam = lax.complex(lam_re[d], lam_im[d])
        step = jnp.exp(log_step[d])[:, None]
        lam_dt = lam * step
        lam_bar = jnp.exp(lam_dt)
        b_bar = ((lam_bar - 1.0) / lam)[..., None] * lax.complex(b_re[d], b_im[d])
        c = lax.complex(c_re[d], c_im[d])
        pw = jnp.exp(lam_dt[None] * jnp.arange(L + 1, dtype=F32)[:, None, None])
        kj = jnp.einsum("ghp,jgp,gpk->gjhk", c, pw[:L], b_bar, precision=hp).real
        lag = np.arange(L)[None, :] - np.arange(L)[:, None]
        lag = lag if d == 0 else -lag
        sel = np.stack([(lag == j) for j in range(L)]).astype(np.float32)
        mt = jnp.einsum("jst,gjab->gsbta", sel, kj, precision=hp).reshape(S5_GROUPS, 16 * L, 16 * L)
        mt_total = mt_total + mt
        pw_dist = jnp.flip(pw[:L], 0) if d == 0 else pw[:L]
        pc = pw_dist.transpose(1, 0, 2)[:, :, None, :] * b_bar.transpose(0, 2, 1)[:, None, :, :]
        pm = jnp.concatenate([pc.real, pc.imag], axis=-1).reshape(S5_GROUPS, 16 * L, 2 * S5_STATE)
        pw_read = pw[1:] if d == 0 else jnp.flip(pw[1:], 0)
        qc = c[:, None, :, :] * pw_read.transpose(1, 0, 2)[:, :, None, :]
        qm = jnp.concatenate([qc.real, -qc.imag], axis=-1).reshape(S5_GROUPS, 16 * L, 2 * S5_STATE).transpose(0, 2, 1)
        a = pw[L]
        out[f"p{d}"], out[f"q{d}"] = pm, qm
        out[f"a{d}"] = jnp.concatenate([a.real, a.imag], axis=-1)
    out["mt"] = mt_total
    return out


def _s5_apow(lam_re, lam_im, log_step, n_steps, conj):
    lam_dt = lax.complex(lam_re, lam_im) * jnp.exp(log_step)[..., None]
    k = (S5_CHUNK * 2.0 ** jnp.arange(n_steps, dtype=F32))[None, None, :, None]
    a = jnp.exp(lam_dt[:, :, None, :] * k)
    re, im = a.real, (-a.imag if conj else a.imag)
    return jnp.stack([jnp.concatenate([re, re], -1), jnp.concatenate([-im, im], -1)], axis=3)


def _shift_rows(x, s, down):
    n = x.shape[0]
    if s >= n:
        return jnp.zeros_like(x)
    if s % 8 == 0:
        z = jnp.zeros((s, x.shape[1]), x.dtype)
        return jnp.concatenate([z, x[:n - s]], axis=0) if down else jnp.concatenate([x[s:], z], axis=0)
    row = lax.broadcasted_iota(jnp.int32, x.shape, 0)
    if down:
        return jnp.where(row >= s, pltpu.roll(x, s, 0), 0.0)
    return jnp.where(row < n - s, pltpu.roll(x, n - s, 0), 0.0)


def _cmul(x, a1, a2):
    return x * a1 + pltpu.roll(x, S5_STATE, 1) * a2


def _chunk_scan(s, apow_ref, down):
    n = s.shape[0]
    k, sh = 0, 1
    while sh < n:
        s = s + _cmul(_shift_rows(s, sh, down), apow_ref[k, 0:1, :], apow_ref[k, 1:2, :])
        k, sh = k + 1, sh * 2
    return s


def _s5_scan_fwd(name, ug, mats, apow):
    G, C, W = ug.shape
    n_steps = apow.shape[2]

    def body(u_ref, mt_ref, p0_ref, p1_ref, q0_ref, q1_ref, ap0_ref, ap1_ref, y_ref, h0_ref, h1_ref):
        u = u_ref[...]
        s0 = jnp.dot(u, p0_ref[...], preferred_element_type=F32)
        s1 = jnp.dot(u, p1_ref[...], preferred_element_type=F32)
        h0 = _shift_rows(_chunk_scan(s0, ap0_ref, True), 1, True)
        h1 = _shift_rows(_chunk_scan(s1, ap1_ref, False), 1, False)
        y = jnp.dot(u, mt_ref[...], preferred_element_type=F32)
        y += jnp.dot(h0.astype(BF16), q0_ref[...], preferred_element_type=F32)
        y += jnp.dot(h1.astype(BF16), q1_ref[...], preferred_element_type=F32)
        y_ref[...] = y
        h0_ref[...] = h0
        h1_ref[...] = h1

    def gspec(shape):
        return pl.BlockSpec((None,) + shape, lambda g: (g,) + (0,) * len(shape))

    ap0, ap1 = apow[0], apow[1]
    return pl.pallas_call(
        body, name=name, grid=(G,),
        in_specs=[gspec((C, W)), gspec((W, W)), gspec((W, 128)), gspec((W, 128)), gspec((128, W)), gspec((128, W)),
                  gspec((n_steps, 2, 128)), gspec((n_steps, 2, 128))],
        out_specs=[gspec((C, W)), gspec((C, 128)), gspec((C, 128))],
        out_shape=[jax.ShapeDtypeStruct((G, C, W), F32), jax.ShapeDtypeStruct((G, C, 128), F32),
                   jax.ShapeDtypeStruct((G, C, 128), F32)],
        compiler_params=_cparams(),
    )(ug, mats["mt"].astype(BF16), mats["p0"].astype(BF16), mats["p1"].astype(BF16),
      mats["q0"].astype(BF16), mats["q1"].astype(BF16), ap0, ap1)


def _s5_scan_bwd(name, dyg, ug, h0, h1, mats, apow_conj):
    G, C, W = ug.shape
    n_steps = apow_conj.shape[2]

    def body(dy_ref, u_ref, h0_ref, h1_ref, mt_ref, p0_ref, p1_ref, q0_ref, q1_ref, ap0_ref, ap1_ref,
             du_ref, dmt_ref, dp0_ref, dp1_ref, dq0_ref, dq1_ref, da_ref):
        dy, u = dy_ref[...], u_ref[...]
        nt, tn = (_DN["nt"], ((), ())), (_DN["tn"], ((), ()))
        dh0 = lax.dot_general(dy, q0_ref[...], nt, preferred_element_type=F32)
        dh1 = lax.dot_general(dy, q1_ref[...], nt, preferred_element_type=F32)
        ds0 = _chunk_scan(_shift_rows(dh0, 1, False), ap0_ref, False)
        ds1 = _chunk_scan(_shift_rows(dh1, 1, True), ap1_ref, True)
        ds0b, ds1b = ds0.astype(BF16), ds1.astype(BF16)
        du = lax.dot_general(dy, mt_ref[...], nt, preferred_element_type=F32)
        du += lax.dot_general(ds0b, p0_ref[...], nt, preferred_element_type=F32)
        du += lax.dot_general(ds1b, p1_ref[...], nt, preferred_element_type=F32)
        du_ref[...] = du
        dmt_ref[...] = lax.dot_general(u, dy, tn, preferred_element_type=F32)
        dp0_ref[...] = lax.dot_general(u, ds0b, tn, preferred_element_type=F32)
        dp1_ref[...] = lax.dot_general(u, ds1b, tn, preferred_element_type=F32)
        h0v, h1v = h0_ref[...], h1_ref[...]
        dq0_ref[...] = lax.dot_general(h0v.astype(BF16), dy, tn, preferred_element_type=F32)
        dq1_ref[...] = lax.dot_general(h1v.astype(BF16), dy, tn, preferred_element_type=F32)
        rows = [jnp.sum(ds0 * h0v, axis=0, keepdims=True), jnp.sum(ds0 * pltpu.roll(h0v, S5_STATE, 1), axis=0, keepdims=True),
                jnp.sum(ds1 * h1v, axis=0, keepdims=True), jnp.sum(ds1 * pltpu.roll(h1v, S5_STATE, 1), axis=0, keepdims=True)]
        da_ref[...] = _stack_rows8(rows)

    def gspec(shape):
        return pl.BlockSpec((None,) + shape, lambda g: (g,) + (0,) * len(shape))

    f32s = lambda *s: jax.ShapeDtypeStruct((G,) + s, F32)
    return pl.pallas_call(
        body, name=name, grid=(G,),
        in_specs=[gspec((C, W)), gspec((C, W)), gspec((C, 128)), gspec((C, 128)), gspec((W, W)), gspec((W, 128)),
                  gspec((W, 128)), gspec((128, W)), gspec((128, W)), gspec((n_steps, 2, 128)), gspec((n_steps, 2, 128))],
        out_specs=[gspec((C, W)), gspec((W, W)), gspec((W, 128)), gspec((W, 128)), gspec((128, W)), gspec((128, W)),
                   gspec((8, 128))],
        out_shape=[f32s(C, W), f32s(W, W), f32s(W, 128), f32s(W, 128), f32s(128, W), f32s(128, W), f32s(8, 128)],
        compiler_params=_cparams(),
    )(dyg, ug, h0, h1, mats["mt"].astype(BF16), mats["p0"].astype(BF16), mats["p1"].astype(BF16),
      mats["q0"].astype(BF16), mats["q1"].astype(BF16), apow_conj[0], apow_conj[1])


def _to_groups(a):
    T = a.shape[0]
    return a.reshape(T // S5_CHUNK, S5_CHUNK, S5_GROUPS, S5_GROUP).transpose(2, 0, 1, 3).reshape(
        S5_GROUPS, T // S5_CHUNK, S5_CHUNK * S5_GROUP)


def _from_groups(a):
    G, C, W = a.shape
    return a.reshape(G, C, S5_CHUNK, S5_GROUP).transpose(1, 2, 0, 3).reshape(C * S5_CHUNK, G * S5_GROUP)


def _s5_fwd(l, h, w, small, g, b):
    T = h.shape[0]
    tm = _tile(T, 512, 8)
    u, u_bf = _mm_simple(f"s5{l}_in", h, w["w_in"], "nn", None, epilogue=lambda accs: [accs[0], accs[0]], n_out=2,
                         out_dtypes=[F32, BF16])
    ug = _to_groups(u_bf)
    prep = lambda *p: _s5_matrices(*p)
    sp = (small["lam_re"], small["lam_im"], small["log_step"], small["b_re"], small["b_im"], small["c_re"], small["c_im"])
    mats, mats_vjp = jax.vjp(prep, *sp)
    n_steps = max(1, int(math.log2(T // S5_CHUNK)))
    apow = _s5_apow(small["lam_re"], small["lam_im"], small["log_step"], n_steps, False)
    yg, h0, h1 = _s5_scan_fwd(f"s5{l}_scan", ug, mats, apow)
    yssm = _from_groups(yg)
    d_row, bglu_row = small["d"].reshape(1, -1), small["b_glu"].reshape(1, -1)

    def mid(y_t, u_t, d_t):
        return [_gelu(y_t + d_t * u_t)]

    y1 = _rows(f"s5{l}_mid", T, tm, [_rt(yssm, tm), _rt(u, tm), _full(d_row)], [_rt_out(T, D_MODEL, F32, tm)], mid)[0]

    def epi(accs, y1_t, bg_t):
        gate = _sigmoid(accs[0] + bg_t)
        return [y1_t * gate, gate]

    y2, gate = _mm_simple(f"s5{l}_glu", y1, w["w_glu"], "nn", None, extras=(y1, bglu_row), epilogue=epi, n_out=2,
                          out_dtypes=[BF16, F32])
    z, h_out = _out_proj_ln(f"s5{l}_out", y2, w["w_out"], h, g, b)
    return h_out, (h, u, ug, yssm, y1, y2, gate, z, h0, h1, mats, mats_vjp, d_row)


def _s5_bwd(l, dh_out, saved, w, small, g):
    h, u, ug, yssm, y1, y2, gate, z, h0, h1, mats, mats_vjp, d_row = saved
    T = h.shape[0]
    dz, dg, db = _ln_bwd(f"s5{l}_lnb", dh_out, z, g)

    def epi1(accs, y1_t, gate_t):
        dy2 = accs[0]
        dpre = dy2 * y1_t * gate_t * (1.0 - gate_t)
        return [dpre, dy2 * gate_t, _bcast8(jnp.sum(dpre, axis=0, keepdims=True))]

    dpre, dy1a, pbg = _mm_simple(f"s5{l}_dy2", dz, w["w_out"], "nt", None, extras=(y1, gate), epilogue=epi1, n_out=2,
                                 out_dtypes=[BF16, F32], psum_outs=1)
    dw_out = _mm_simple(f"s5{l}_dWout", y2, dz, "tn", BF16, tm=1024, tn=1024, tk=512)

    def epi2(accs, dy1a_t, yssm_t, u_t, d_t):
        dy1 = accs[0] + dy1a_t
        dy = dy1 * _dgelu(yssm_t + d_t * u_t)
        return [dy, dy * d_t, _bcast8(jnp.sum(dy * u_t, axis=0, keepdims=True))]

    dy, du_skip, pdd = _mm_simple(f"s5{l}_dy1", dpre, w["w_glu"], "nt", None, extras=(dy1a, yssm, u, d_row), epilogue=epi2,
                                  n_out=2, out_dtypes=[BF16, F32], psum_outs=1)
    dw_glu = _mm_simple(f"s5{l}_dWglu", y1, dpre, "tn", BF16, tm=1024, tn=1024, tk=512)
    n_steps = max(1, int(math.log2(T // S5_CHUNK)))
    apow_c = _s5_apow(small["lam_re"], small["lam_im"], small["log_step"], n_steps, True)
    dug, dmt, dp0, dp1, dq0, dq1, da = _s5_scan_bwd(f"s5{l}_scanb", _to_groups(dy), ug, h0, h1, mats, apow_c)
    du = _from_groups(dug) + du_skip

    def a_grad(p, q):
        return jnp.concatenate([p[:, :S5_STATE] + p[:, S5_STATE:], q[:, S5_STATE:] - q[:, :S5_STATE]], axis=-1)

    dmats = dict(mt=dmt, p0=dp0, p1=dp1, q0=dq0, q1=dq1, a0=a_grad(da[:, 0], da[:, 1]), a1=a_grad(da[:, 2], da[:, 3]))
    dsp = mats_vjp(dmats)
    dh = _mm_simple(f"s5{l}_dX", du, w["w_in"], "nt", F32, extras=(dz,), epilogue=lambda accs, dz_t: [accs[0] + ALPHA * dz_t])
    dw_in = _mm_simple(f"s5{l}_dWin", h, du, "tn", BF16, tm=1024, tn=1024, tk=512)
    grads = dict(w_in=dw_in, w_glu=dw_glu, w_out=dw_out, d=_psum_rows(pdd), b_glu=_psum_rows(pbg), ln_g=dg, ln_b=db,
                 lam_re=dsp[0], lam_im=dsp[1], log_step=dsp[2], b_re=dsp[3], b_im=dsp[4], c_re=dsp[5], c_im=dsp[6])
    return dh, grads


def _gla_levels(lc):
    ms, m = [], lc // 2
    while m >= 1:
        ms.append(m)
        m //= 2
    return ms


def _gla_scan_matrix(lc, rev):
    r = np.arange(lc)[:, None]
    t = np.arange(lc)[None, :]
    blocks = []
    for m in _gla_levels(lc):
        same = (r // m) == (t // m)
        upper = ((r // m) % 2) == 1
        blocks.append(same & np.where(upper, t >= r, t < r))
    blocks.append(t >= r)
    blocks.append(t < r)
    if rev:
        blocks = [blk[::-1, ::-1] for blk in blocks]
    return np.concatenate(blocks, axis=1).astype(np.float32)


def _gla_gates(z, lb):
    sig = _sigmoid(z)
    ls = jnp.minimum(z, 0.0) - jnp.log(1.0 + jnp.exp(-jnp.abs(z)))
    a = jnp.log(lb)
    bb = jnp.log(1.0 - lb) + ls
    lf = jnp.maximum(a, bb) + jnp.log(1.0 + jnp.exp(-jnp.abs(a - bb)))
    return lf, (1.0 - lb) * (1.0 - sig), sig


def _gla_cumsum(lf, rev):
    b, sh = lf, 1
    while sh < lf.shape[0]:
        b = b + _shift_rows(b, sh, not rev)
        sh *= 2
    return b


def _gla_bref(b, m, rev):
    lc, n = b.shape
    idx = m if rev else m - 1
    if 2 * m >= 8:
        nb = lc // (2 * m)
        b3 = b.reshape(nb, 2 * m, n)
        return jnp.broadcast_to(b3[:, idx:idx + 1, :], (nb, 2 * m, n)).reshape(lc, n)
    row = lax.broadcasted_iota(jnp.int32, b.shape, 0)
    j = row & (2 * m - 1)
    out = b
    for jj in range(2 * m):
        if jj != idx:
            out = jnp.where(j == jj, pltpu.roll(b, (jj - idx) % lc, 0), out)
    return out


def _gla_chunk(q, k, lf, rev):
    lc = q.shape[0]
    nt = (_DN["nt"], ((), ()))
    b = _gla_cumsum(lf, rev)
    row = lax.broadcasted_iota(jnp.int32, (lc, HG_DIM), 0)
    ri = lax.broadcasted_iota(jnp.int32, (lc, lc), 0)
    ci = lax.broadcasted_iota(jnp.int32, (lc, lc), 1)
    qb, kb = q.astype(BF16), k.astype(BF16)
    sc = jnp.where(ri == ci, lax.dot_general(qb, kb, nt, preferred_element_type=F32), 0.0)
    levels = []
    for m in _gla_levels(lc):
        lg = int(math.log2(m))
        isq = ((row >> lg) & 1) == (0 if rev else 1)
        bref = _gla_bref(b, m, rev)
        w = jnp.exp(jnp.where(isq, b - bref, bref - b))
        xf = jnp.where(isq, q * w, 0.0)
        yf = jnp.where(isq, 0.0, k * w)
        bmask = (ri >> (lg + 1)) == (ci >> (lg + 1))
        xb, yb = xf.astype(BF16), yf.astype(BF16)
        sc = sc + jnp.where(bmask, lax.dot_general(xb, yb, nt, preferred_element_type=F32), 0.0)
        levels.append((isq, w, xf, yf, xb, yb, bmask))
    return b, sc, levels, (ri == ci)


def _hg_specs(T, lc, rev, backward):
    nc = T // lc
    cc = (lambda c: nc - 1 - c) if rev != backward else (lambda c: c)
    zcol = HG_HEADS * (2 if rev else 1)
    q_spec = pl.BlockSpec((lc, HG_DIM), lambda h, c: (cc(c), h))
    z_spec = pl.BlockSpec((lc, HG_DIM), lambda h, c: (cc(c), zcol + h))
    v_spec = pl.BlockSpec((lc, HG_DIM), lambda h, c: (cc(c), 3 * HG_HEADS + h))
    lb_spec = pl.BlockSpec((1, HG_DIM), lambda h, c: (0, h))
    st_spec = pl.BlockSpec((None, None, HG_DIM, HG_DIM), lambda h, c: (h, cc(c), 0, 0))
    return nc, q_spec, z_spec, v_spec, lb_spec, st_spec


def _gla_fwd(name, proj, lb_row, rev):
    T = proj.shape[0]
    lc = _tile(T, HG_CHUNK, 8)
    nc, q_spec, z_spec, v_spec, lb_spec, st_spec = _hg_specs(T, lc, rev, False)
    last = 0 if rev else lc - 1

    def body(q_ref, z_ref, v_ref, lb_ref, o_ref, st_ref, st_s):
        @pl.when(pl.program_id(1) == 0)
        def _():
            st_s[...] = jnp.zeros(st_s.shape, F32)

        q = _silu(q_ref[...])
        lf, k, _ = _gla_gates(z_ref[...], lb_ref[...])
        vb = v_ref[...].astype(BF16)
        b, sc, _, _ = _gla_chunk(q, k, lf, rev)
        st0 = st_s[...]
        st_ref[...] = st0
        bl = b[last:last + 1, :]
        o = jnp.dot(sc.astype(BF16), vb, preferred_element_type=F32)
        o += lax.dot_general((q * jnp.exp(b)).astype(BF16), st0.astype(BF16), (_DN["nt"], ((), ())), preferred_element_type=F32)
        o_ref[...] = o
        kd = (k * jnp.exp(bl - b)).astype(BF16)
        st_s[...] = st0 * jnp.exp(bl) + lax.dot_general(vb, kd, (_DN["tn"], ((), ())), preferred_element_type=F32)

    return pl.pallas_call(
        body, name=name, grid=(HG_HEADS, nc),
        in_specs=[q_spec, z_spec, v_spec, lb_spec], out_specs=[q_spec, st_spec],
        out_shape=[jax.ShapeDtypeStruct((T, D_MODEL), F32), jax.ShapeDtypeStruct((HG_HEADS, nc, HG_DIM, HG_DIM), F32)],
        scratch_shapes=[pltpu.VMEM((HG_DIM, HG_DIM), F32)], compiler_params=_cparams(),
    )(proj, proj, proj, lb_row)


def _gla_bwd(name, proj, lb_row, do, st, rev):
    T = proj.shape[0]
    lc = _tile(T, HG_CHUNK, 8)
    nc, q_spec, z_spec, v_spec, lb_spec, st_spec = _hg_specs(T, lc, rev, True)
    wall = jnp.asarray(_gla_scan_matrix(lc, rev), BF16)
    last = 0 if rev else lc - 1

    def body(q_ref, z_ref, v_ref, lb_ref, do_ref, st_ref, wall_ref, dq_ref, dz_ref, dv_ref, dlb_ref, dst_s):
        first = pl.program_id(1) == 0

        @pl.when(first)
        def _():
            dst_s[...] = jnp.zeros(dst_s.shape, F32)
            dlb_ref[...] = jnp.zeros(dlb_ref.shape, F32)

        nn, nt, tn = (_DN["nn"], ((), ())), (_DN["nt"], ((), ())), (_DN["tn"], ((), ()))
        dot = functools.partial(lax.dot_general, preferred_element_type=F32)
        qr, z, lb = q_ref[...], z_ref[...], lb_ref[...]
        q = _silu(qr)
        lf, k, sig = _gla_gates(z, lb)
        vb = v_ref[...].astype(BF16)
        b, sc, levels, eye = _gla_chunk(q, k, lf, rev)
        st0, dst = st_ref[...], dst_s[...]
        st0b, dstb = st0.astype(BF16), dst.astype(BF16)
        dob = do_ref[...].astype(BF16)
        bl = b[last:last + 1, :]
        eb, ebl, ekd = jnp.exp(b), jnp.exp(bl), jnp.exp(bl - b)
        qe, kd = q * eb, k * ekd
        kdb = kd.astype(BF16)
        dsc = dot(dob, vb, nt)
        dv_ref[...] = dot(sc.astype(BF16), dob, tn) + dot(kdb, dstb, nt)
        dqe = dot(dob, st0b, nn)
        dkd = dot(vb, dstb, nn)
        dq = dqe * eb
        dk = dkd * ekd
        zs = []
        dsd = jnp.where(eye, dsc, 0.0).astype(BF16)
        dq += dot(dsd, k.astype(BF16), nn)
        dk += dot(dsd, q.astype(BF16), tn)
        for (upper, w, xf, yf, xb, yb, bmask) in levels:
            dsl = jnp.where(bmask, dsc, 0.0).astype(BF16)
            dx = dot(dsl, yb, nn)
            dy = dot(dsl, xb, tn)
            dq += jnp.where(upper, dx * w, 0.0)
            dk += jnp.where(upper, 0.0, dy * w)
            zs.append(jnp.where(upper, dx * xf, dy * yf).astype(BF16))
        zs.append((dqe * qe).astype(BF16))
        zs.append((dkd * kd).astype(BF16))
        zl = jnp.sum(dst * st0, axis=0, keepdims=True) * ebl
        dlf = dot(wall_ref[...], jnp.concatenate(zs, axis=0), nn) + zl
        dst_s[...] = dst * ebl + dot(dob, qe.astype(BF16), tn)
        inv_f = jnp.exp(-lf)
        one_sig = 1.0 - sig
        dz_ref[...] = (dlf * inv_f - dk) * (1.0 - lb) * sig * one_sig
        dq_ref[...] = dq * _dsilu(qr)
        dlb_ref[...] += _bcast8(jnp.sum((dlf * inv_f - dk) * one_sig, axis=0, keepdims=True))

    big = jax.ShapeDtypeStruct((T, D_MODEL), F32)
    return pl.pallas_call(
        body, name=name, grid=(HG_HEADS, nc),
        in_specs=[q_spec, z_spec, v_spec, lb_spec, q_spec, st_spec, pl.BlockSpec(wall.shape, lambda h, c: (0, 0))],
        out_specs=[q_spec, q_spec, q_spec, pl.BlockSpec((None, 8, HG_DIM), lambda h, c: (h, 0, 0))],
        out_shape=[big, big, big, jax.ShapeDtypeStruct((HG_HEADS, 8, HG_DIM), F32)],
        scratch_shapes=[pltpu.VMEM((HG_DIM, HG_DIM), F32)], compiler_params=_cparams(),
    )(proj, proj, proj, lb_row, do, st, wall)


def _hg_lower_bounds(hg_lower_bound, layer):
    lbs = jax.nn.softmax(hg_lower_bound, axis=0)
    lbs = jnp.cumsum(lbs, axis=0) - lbs[0]
    return lbs[layer].reshape(1, -1)


def _hg_post(o_fw, o_bw, g_raw, ng):
    outs, ons, os_, rs = [], [], [], []
    for hd in range(o_fw.shape[1] // HG_DIM):
        sl = slice(hd * HG_DIM, (hd + 1) * HG_DIM)
        o = o_fw[:, sl] + o_bw[:, sl]
        r = lax.rsqrt(jnp.mean(o * o, axis=-1, keepdims=True) + RMS_EPS)
        on = o * r * ng
        outs.append(on * _silu(g_raw[:, sl]))
        ons.append(on)
        os_.append(o)
        rs.append(r)
    return outs, ons, os_, rs


def _hg_fwd(l, h, w, small, g, b):
    T = h.shape[0]
    tm = _tile(T, 512, 8)
    proj = _mm_simple(f"hg{l}_in", h, w["w_in"], "nn", F32, tn=1280, j_outer=True)
    lb_fn = lambda p: _hg_lower_bounds(p, l)
    lb_row, lb_vjp = jax.vjp(lb_fn, small["lower_bound"])
    o_fw, st_fw = _gla_fwd(f"hg{l}_gla_fw", proj, lb_row, False)
    o_bw, st_bw = _gla_fwd(f"hg{l}_gla_bw", proj, lb_row, True)
    ng = small["norm_g"].reshape(1, HG_DIM)

    def post(of_t, ob_t, g_t, ng_t):
        return [jnp.concatenate(_hg_post(of_t, ob_t, g_t, ng_t)[0], axis=1)]

    og = _rows(f"hg{l}_post", T, tm,
               [_rt(o_fw, tm), _rt(o_bw, tm), (proj, (tm, D_MODEL), lambda i: (i, 4)), _full(ng)],
               [_rt_out(T, D_MODEL, BF16, tm)], post)[0]
    z, h_out = _out_proj_ln(f"hg{l}_out", og, w["w_out"], h, g, b)
    return h_out, (h, proj, lb_row, lb_vjp, st_fw, st_bw, o_fw, o_bw, ng, og, z)


def _hg_bwd(l, dh_out, saved, w, small, g):
    h, proj, lb_row, lb_vjp, st_fw, st_bw, o_fw, o_bw, ng, og, z = saved
    T = h.shape[0]
    dz, dg, db = _ln_bwd(f"hg{l}_lnb", dh_out, z, g)
    tm = _tile(T, 512, 8)
    nn_tiles = D_MODEL // HG_DIM

    def epi(accs, of_t, ob_t, g_t, ng_t):
        dog = accs[0]
        _, ons, os_, rs = _hg_post(of_t, ob_t, g_t, ng_t)
        dos, dgs, dngs = [], [], []
        for hd in range(nn_tiles):
            sl = slice(hd * HG_DIM, (hd + 1) * HG_DIM)
            d, o, r = dog[:, sl], os_[hd], rs[hd]
            dgs.append(d * ons[hd] * _dsilu(g_t[:, sl]))
            don = d * _silu(g_t[:, sl])
            dngs.append(jnp.sum(don * o * r, axis=0, keepdims=True))
            dxn = don * ng_t
            dos.append(r * dxn - o * (r * r * r) * jnp.mean(dxn * o, axis=-1, keepdims=True))
        return [jnp.concatenate(dos, axis=1), jnp.concatenate(dgs, axis=1), _bcast8(jnp.concatenate(dngs, axis=1))]

    grid = (T // tm, 1, 1)
    row_map = lambda i, j, k: (i, 0)
    do, dg_raw, png = _mm(
        f"hg{l}_dog", grid, [(dz, (tm, D_MODEL), row_map)], [(w["w_out"], (D_MODEL, D_MODEL), lambda i, j, k: (0, 0))],
        [(0, 0, 0, "nt")], 1, (tm, D_MODEL),
        [(o_fw, (tm, D_MODEL), row_map), (o_bw, (tm, D_MODEL), row_map), (proj, (tm, D_MODEL), lambda i, j, k: (i, 4)),
         (ng, (1, HG_DIM), lambda i, j, k: (0, 0))],
        [((T, D_MODEL), F32, (tm, D_MODEL), row_map), ((T, D_MODEL), F32, (tm, D_MODEL), row_map),
         ((T // tm * 8, D_MODEL), F32, (8, D_MODEL), row_map)], epi)
    dw_out = _mm_simple(f"hg{l}_dWout", og, dz, "tn", BF16, tm=1024, tn=1024, tk=512)
    dq_f, dz_f, dv_f, dlb_f = _gla_bwd(f"hg{l}_glab_fw", proj, lb_row, do, st_fw, False)
    dq_b, dz_b, dv_b, dlb_b = _gla_bwd(f"hg{l}_glab_bw", proj, lb_row, do, st_bw, True)
    dproj = jnp.concatenate([dq_f + dq_b, dz_f, dz_b, dv_f + dv_b, dg_raw], axis=1).astype(BF16)
    dh = _mm_simple(f"hg{l}_dX", dproj, w["w_in"], "nt", F32, tm=256, tk=5 * D_MODEL, extras=(dz,),
                    epilogue=lambda accs, dz_t: [accs[0] + ALPHA * dz_t])
    dw_in = _mm_simple(f"hg{l}_dWin", h, dproj, "tn", BF16, tm=1024, tn=1280, tk=512)
    dlb_row = (dlb_f[:, 0, :] + dlb_b[:, 0, :]).reshape(1, D_MODEL)
    grads = dict(w_in=dw_in, w_out=dw_out, lower_bound=lb_vjp(dlb_row)[0],
                 norm_g=_psum_rows(png).reshape(nn_tiles, HG_DIM).sum(axis=0), ln_g=dg, ln_b=db)
    return dh, grads


MLA_W = 256
MLA_SCALE = (MLA_NOPE + MLA_ROPE) ** -0.5


def _swap_halves(a):
    n = a.shape[-1] // 2
    return jnp.concatenate([a[..., n:], a[..., :n]], axis=-1)


def _mla_ext_weights(w_in, w_q_b):
    w_in_ext = jnp.concatenate([w_in, _swap_halves(w_in[:, MLA_Q_LORA + MLA_KV_LORA:])], axis=1)
    wq = w_q_b.reshape(MLA_Q_LORA, MLA_HEADS, MLA_NOPE + MLA_ROPE)
    wq_ext = jnp.concatenate([wq, _swap_halves(wq[:, :, MLA_NOPE:])], axis=2).reshape(MLA_Q_LORA, MLA_HEADS * MLA_W)
    return w_in_ext, wq_ext


def _mla_ext_grads(dw_in_ext, dwq_ext):
    n_lat = MLA_Q_LORA + MLA_KV_LORA
    dw_in = jnp.concatenate([dw_in_ext[:, :n_lat], dw_in_ext[:, n_lat:n_lat + MLA_ROPE]
                             + _swap_halves(dw_in_ext[:, n_lat + MLA_ROPE:])], axis=1)
    dq = dwq_ext.reshape(MLA_Q_LORA, MLA_HEADS, MLA_W)
    dwq = jnp.concatenate([dq[:, :, :MLA_NOPE], dq[:, :, MLA_NOPE:MLA_NOPE + MLA_ROPE]
                           + _swap_halves(dq[:, :, MLA_NOPE + MLA_ROPE:])], axis=2)
    return dw_in, dwq.reshape(MLA_Q_LORA, MLA_HEADS * (MLA_NOPE + MLA_ROPE))


def _rope_table(positions):
    half = MLA_ROPE // 2
    inv_freq = 1.0 / (ROPE_THETA ** (jnp.arange(half, dtype=F32) * (2.0 / MLA_ROPE)))
    ang = positions.astype(F32)[:, None] * inv_freq
    cos, sin = jnp.cos(ang), jnp.sin(ang)
    return jnp.concatenate([cos, cos, -sin, sin], axis=1)


def _rope_sum(prod):
    return prod + pltpu.roll(prod, MLA_ROPE, 1)


def _low_half(a):
    lane = lax.broadcasted_iota(jnp.int32, a.shape, 1)
    return jnp.where(lane < MLA_ROPE, a, 0.0)


def _rms(x, g):
    r = lax.rsqrt(jnp.mean(x * x, axis=-1, keepdims=True) + RMS_EPS)
    return x * r * g


def _rms_bwd(x, g, dy):
    r = lax.rsqrt(jnp.mean(x * x, axis=-1, keepdims=True) + RMS_EPS)
    dxn = dy * g
    dx = r * dxn - x * (r * r * r) * jnp.mean(dxn * x, axis=-1, keepdims=True)
    return dx, jnp.sum(dy * x * r, axis=0, keepdims=True)


ATT_TILE = 512
ATT_BLOCK = 2048


def _lanes(col, n):
    return jnp.tile(col, (1, n // 128))


def _flash_fwd(name, q, k, v):
    H, T, _ = q.shape
    tq, tkb = _tile(T, ATT_TILE, 8), _tile(T, ATT_BLOCK, 128)
    ts = _tile(tkb, ATT_TILE, 128)
    nk, nsub = T // tkb, tkb // ts

    def body(q_ref, k_ref, v_ref, o_ref, lse_ref, lser_ref, m_s, l_s, acc_s):
        ki = pl.program_id(2)

        @pl.when(ki == 0)
        def _():
            m_s[...] = jnp.full(m_s.shape, -jnp.inf, F32)
            l_s[...] = jnp.zeros(l_s.shape, F32)
            acc_s[...] = jnp.zeros(acc_s.shape, F32)

        qv = q_ref[...]
        m, l, acc = m_s[...], l_s[...], acc_s[...]
        for j in range(nsub):
            kj, vj = k_ref[j * ts:(j + 1) * ts, :], v_ref[j * ts:(j + 1) * ts, :]
            s = lax.dot_general(qv, kj, (_DN["nt"], ((), ())), preferred_element_type=F32)
            m_new = jnp.maximum(m, jnp.max(s, axis=-1, keepdims=True))
            a = jnp.exp(m - m_new)
            p = jnp.exp(s - _lanes(m_new, ts))
            l = a * l + jnp.sum(p, axis=-1, keepdims=True)
            acc = a * acc + jnp.dot(p.astype(BF16), vj, preferred_element_type=F32)
            m = m_new
        m_s[...], l_s[...], acc_s[...] = m, l, acc

        @pl.when(ki == nk - 1)
        def _():
            o_ref[...] = acc / l
            lse = m + jnp.log(l)
            lse_ref[...] = lse
            lser_ref[...] = lse.T[:8, :]

    return pl.pallas_call(
        body, name=name, grid=(H, T // tq, nk),
        in_specs=[pl.BlockSpec((None, tq, MLA_W), lambda h, i, j: (h, i, 0)),
                  pl.BlockSpec((None, tkb, MLA_W), lambda h, i, j: (h, j, 0)),
                  pl.BlockSpec((None, tkb, MLA_V), lambda h, i, j: (h, j, 0))],
        out_specs=[pl.BlockSpec((tq, MLA_V), lambda h, i, j: (i, h)), pl.BlockSpec((tq, MLA_V), lambda h, i, j: (i, h)),
                   pl.BlockSpec((None, 8, tq), lambda h, i, j: (h, 0, i))],
        out_shape=[jax.ShapeDtypeStruct((T, H * MLA_V), F32), jax.ShapeDtypeStruct((T, H * MLA_V), F32),
                   jax.ShapeDtypeStruct((H, 8, T), F32)],
        scratch_shapes=[pltpu.VMEM((tq, MLA_V), F32), pltpu.VMEM((tq, MLA_V), F32), pltpu.VMEM((tq, MLA_V), F32)],
        compiler_params=_cparams(),
    )(q, k, v)


def _flash_bwd(name, q, k, v, do, o, lse, lse_rows):
    H, T, _ = q.shape
    nt = (_DN["nt"], ((), ()))
    tq, tkb = _tile(T, ATT_TILE, 8), _tile(T, ATT_BLOCK, 128)
    ts = _tile(tkb, ATT_TILE, 128)
    nkb, nsub = T // tkb, tkb // ts

    def dq_body(q_ref, k_ref, v_ref, do_ref, o_ref, lse_ref, dq_ref, dlr_ref, acc_s, dl_s):
        ki = pl.program_id(2)

        @pl.when(ki == 0)
        def _():
            acc_s[...] = jnp.zeros(acc_s.shape, F32)
            delta = jnp.sum(do_ref[...].astype(F32) * o_ref[...], axis=-1, keepdims=True)
            dl = jnp.broadcast_to(delta, dl_s.shape)
            dl_s[...] = dl
            dlr_ref[...] = dl.T[:8, :]

        qv, dob = q_ref[...], do_ref[...]
        lse_t, dl_t = _lanes(lse_ref[...], ts), _lanes(dl_s[...], ts)
        acc = acc_s[...]
        for j in range(nsub):
            kj, vj = k_ref[j * ts:(j + 1) * ts, :], v_ref[j * ts:(j + 1) * ts, :]
            s = lax.dot_general(qv, kj, nt, preferred_element_type=F32)
            dp = lax.dot_general(dob, vj, nt, preferred_element_type=F32)
            ds = jnp.exp(s - lse_t) * (dp - dl_t)
            acc = acc + jnp.dot(ds.astype(BF16), kj, preferred_element_type=F32)
        acc_s[...] = acc

        @pl.when(ki == nkb - 1)
        def _():
            dq_ref[...] = acc

    dq, delta_rows = pl.pallas_call(
        dq_body, name=name + "_dq", grid=(H, T // tq, nkb),
        in_specs=[pl.BlockSpec((None, tq, MLA_W), lambda h, i, j: (h, i, 0)),
                  pl.BlockSpec((None, tkb, MLA_W), lambda h, i, j: (h, j, 0)),
                  pl.BlockSpec((None, tkb, MLA_V), lambda h, i, j: (h, j, 0)),
                  pl.BlockSpec((tq, MLA_V), lambda h, i, j: (i, h)),
                  pl.BlockSpec((tq, MLA_V), lambda h, i, j: (i, h)),
                  pl.BlockSpec((tq, MLA_V), lambda h, i, j: (i, h))],
        out_specs=[pl.BlockSpec((None, tq, MLA_W), lambda h, i, j: (h, i, 0)),
                   pl.BlockSpec((None, 8, tq), lambda h, i, j: (h, 0, i))],
        out_shape=[jax.ShapeDtypeStruct((H, T, MLA_W), F32), jax.ShapeDtypeStruct((H, 8, T), F32)],
        scratch_shapes=[pltpu.VMEM((tq, MLA_W), F32), pltpu.VMEM((tq, MLA_V), F32)], compiler_params=_cparams(),
    )(q, k, v, do, o, lse)

    tk, tqb = _tile(T, ATT_TILE, 128), _tile(T, ATT_BLOCK, 128)
    tqs = _tile(tqb, ATT_TILE, 128)
    nqb, nqsub = T // tqb, tqb // tqs

    def dkv_body(q_ref, k_ref, v_ref, do_ref, lser_ref, dlr_ref, dk_ref, dv_ref, dk_s, dv_s):
        qi = pl.program_id(2)

        @pl.when(qi == 0)
        def _():
            dk_s[...] = jnp.zeros(dk_s.shape, F32)
            dv_s[...] = jnp.zeros(dv_s.shape, F32)

        kv, vv = k_ref[...], v_ref[...]
        dk, dv = dk_s[...], dv_s[...]
        for j in range(nqsub):
            sl = slice(j * tqs, (j + 1) * tqs)
            qj, doj = q_ref[sl, :], do_ref[sl, :]
            st = lax.dot_general(kv, qj, nt, preferred_element_type=F32)
            dpt = lax.dot_general(vv, doj, nt, preferred_element_type=F32)
            pt = jnp.exp(st - lser_ref[0:1, sl])
            dst = pt * (dpt - dlr_ref[0:1, sl])
            dv = dv + jnp.dot(pt.astype(BF16), doj, preferred_element_type=F32)
            dk = dk + jnp.dot(dst.astype(BF16), qj, preferred_element_type=F32)
        dk_s[...], dv_s[...] = dk, dv

        @pl.when(qi == nqb - 1)
        def _():
            dk_ref[...] = dk
            dv_ref[...] = dv

    dk, dv = pl.pallas_call(
        dkv_body, name=name + "_dkv", grid=(H, T // tk, nqb),
        in_specs=[pl.BlockSpec((None, tqb, MLA_W), lambda h, i, j: (h, j, 0)),
                  pl.BlockSpec((None, tk, MLA_W), lambda h, i, j: (h, i, 0)),
                  pl.BlockSpec((None, tk, MLA_V), lambda h, i, j: (h, i, 0)),
                  pl.BlockSpec((tqb, MLA_V), lambda h, i, j: (j, h)),
                  pl.BlockSpec((None, 8, tqb), lambda h, i, j: (h, 0, j)),
                  pl.BlockSpec((None, 8, tqb), lambda h, i, j: (h, 0, j))],
        out_specs=[pl.BlockSpec((None, tk, MLA_W), lambda h, i, j: (h, i, 0)),
                   pl.BlockSpec((None, tk, MLA_V), lambda h, i, j: (h, i, 0))],
        out_shape=[jax.ShapeDtypeStruct((H, T, MLA_W), F32), jax.ShapeDtypeStruct((H, T, MLA_V), F32)],
        scratch_shapes=[pltpu.VMEM((tk, MLA_W), F32), pltpu.VMEM((tk, MLA_V), F32)], compiler_params=_cparams(),
    )(q, k, v, do, lse_rows, delta_rows)
    return dq, dk, dv


def _mla_fwd(l, h, w, small, g, b, cs):
    T = h.shape[0]
    tm = _tile(T, 512, 8)
    H = MLA_HEADS
    gq, gkv = small["q_norm_g"].reshape(1, -1), small["kv_norm_g"].reshape(1, -1)
    n_ext = MLA_Q_LORA + MLA_KV_LORA + 2 * MLA_ROPE
    row = lambda i, j, k: (i, 0)
    fix = lambda i, j, k: (0, 0)

    def epi_lat(accs, gq_t, gkv_t):
        a = accs[0]
        ql, kvl = a[:, :MLA_Q_LORA], a[:, MLA_Q_LORA:MLA_Q_LORA + MLA_KV_LORA]
        return [ql, kvl, a[:, MLA_Q_LORA + MLA_KV_LORA:], _rms(ql, gq_t), _rms(kvl, gkv_t)]

    ql, kvl, kr, xq, xkv = _mm(
        f"mla{l}_in", (T // tm, 1, 1), [(h, (tm, D_MODEL), row)], [(w["w_in_ext"], (D_MODEL, n_ext), fix)],
        [(0, 0, 0, "nn")], 1, (tm, n_ext), [(gq, gq.shape, fix), (gkv, gkv.shape, fix)],
        [((T, MLA_Q_LORA), F32, (tm, MLA_Q_LORA), row), ((T, MLA_KV_LORA), F32, (tm, MLA_KV_LORA), row),
         ((T, 2 * MLA_ROPE), F32, (tm, 2 * MLA_ROPE), row), ((T, MLA_Q_LORA), BF16, (tm, MLA_Q_LORA), row),
         ((T, MLA_KV_LORA), BF16, (tm, MLA_KV_LORA), row)], epi_lat)

    def epi_q(accs, cs_t):
        a = accs[0]
        return [jnp.concatenate([a[:, :MLA_NOPE], _rope_sum(a[:, MLA_NOPE:] * cs_t)], axis=1) * MLA_SCALE]

    head_out = lambda i, j, k: (j, i, 0)
    q = _mm(f"mla{l}_q", (T // tm, H, 1), [(xq, (tm, MLA_Q_LORA), row)],
            [(w["wq_ext"], (MLA_Q_LORA, MLA_W), lambda i, j, k: (0, j))], [(0, 0, 0, "nn")], 1, (tm, MLA_W),
            [(cs, (tm, 2 * MLA_ROPE), row)], [((H, T, MLA_W), BF16, (None, tm, MLA_W), head_out)], epi_q)[0]

    def epi_kv(accs, kr_t, cs_t):
        a = accs[0]
        return [jnp.concatenate([a[:, :MLA_NOPE], _low_half(_rope_sum(kr_t * cs_t))], axis=1), a[:, MLA_NOPE:]]

    k, v = _mm(f"mla{l}_kv", (T // tm, H, 1), [(xkv, (tm, MLA_KV_LORA), row)],
               [(w["w_kv_b"], (MLA_KV_LORA, MLA_W), lambda i, j, k: (0, j))], [(0, 0, 0, "nn")], 1, (tm, MLA_W),
               [(kr, (tm, 2 * MLA_ROPE), row), (cs, (tm, 2 * MLA_ROPE), row)],
               [((H, T, MLA_W), BF16, (None, tm, MLA_W), head_out), ((H, T, MLA_V), BF16, (None, tm, MLA_V), head_out)],
               epi_kv)
    o, lse, lse_rows = _flash_fwd(f"mla{l}_attn", q, k, v)
    z, h_out = _out_proj_ln(f"mla{l}_out", o, w["w_out"], h, g, b)
    return h_out, (h, ql, kvl, xq, xkv, q, k, v, o, (lse, lse_rows), z, gq, gkv, cs)


def _mla_bwd(l, dh_out, saved, w, g):
    h, ql, kvl, xq, xkv, q, k, v, o, lse, z, gq, gkv, cs = saved
    T = h.shape[0]
    tm = _tile(T, 512, 8)
    H = MLA_HEADS
    dz, dg, db = _ln_bwd(f"mla{l}_lnb", dh_out, z, g)
    do = _mm_simple(f"mla{l}_dO", dz, w["w_out"], "nt", BF16)
    dw_out = _mm_simple(f"mla{l}_dWout", o, dz, "tn", BF16, tm=1024, tn=1024, tk=512)
    dq, dk, dv = _flash_bwd(f"mla{l}_attnb", q, k, v, do, o, lse[0], lse[1])
    head_in = lambda i, hh: (hh, i, 0)
    row2 = lambda i, hh: (i, 0)

    def fn_q(dq_t, cs_t):
        d = dq_t[:, MLA_NOPE:]
        return [jnp.concatenate([dq_t[:, :MLA_NOPE], _rope_sum(d) * cs_t], axis=1) * MLA_SCALE]

    dq_eff = _grid_call(f"mla{l}_dqeff", (T // tm, H), [(dq, (None, tm, MLA_W), head_in), (cs, (tm, 2 * MLA_ROPE), row2)],
                        [((T, H * MLA_W), BF16, (tm, MLA_W), lambda i, hh: (i, hh))], fn_q)[0]

    def fn_kv(dk_t, dv_t, cs_t):
        return [jnp.concatenate([dk_t[:, :MLA_NOPE], dv_t], axis=1), _rope_sum(_low_half(dk_t[:, MLA_NOPE:])) * cs_t]

    dkv_eff, dkr = _grid_call(
        f"mla{l}_dkveff", (T // tm, H),
        [(dk, (None, tm, MLA_W), head_in), (dv, (None, tm, MLA_V), head_in), (cs, (tm, 2 * MLA_ROPE), row2)],
        [((T, H * MLA_W), BF16, (tm, MLA_W), lambda i, hh: (i, hh)), ((T, 2 * MLA_ROPE), F32, (tm, 2 * MLA_ROPE), row2)],
        fn_kv, acc_outs=(1,))
    dxq = _mm_simple(f"mla{l}_dxq", dq_eff, w["wq_ext"], "nt", F32, tn=MLA_Q_LORA)
    dwq_ext = _mm_simple(f"mla{l}_dWq", xq, dq_eff, "tn", BF16, tm=MLA_Q_LORA, tn=1024, tk=512)
    dxkv = _mm_simple(f"mla{l}_dxkv", dkv_eff, w["w_kv_b"], "nt", F32, tn=MLA_KV_LORA)
    dwkv = _mm_simple(f"mla{l}_dWkv", xkv, dkv_eff, "tn", BF16, tm=MLA_KV_LORA, tn=1024, tk=512)

    def fn_lat(ql_t, dxq_t, gq_t, kvl_t, dxkv_t, gkv_t, dkr_t):
        dql, dgq = _rms_bwd(ql_t, gq_t, dxq_t)
        dkvl, dgkv = _rms_bwd(kvl_t, gkv_t, dxkv_t)
        return [jnp.concatenate([dql, dkvl, dkr_t], axis=1), _bcast8(dgq), _bcast8(dgkv)]

    n_ext = MLA_Q_LORA + MLA_KV_LORA + 2 * MLA_ROPE
    dlat, pgq, pgkv = _rows(f"mla{l}_dlat", T, tm,
                            [_rt(ql, tm), _rt(dxq, tm), _full(gq), _rt(kvl, tm), _rt(dxkv, tm), _full(gkv), _rt(dkr, tm)],
                            [_rt_out(T, n_ext, BF16, tm), _ps_out(T, MLA_Q_LORA, tm), _ps_out(T, MLA_KV_LORA, tm)], fn_lat)
    dh = _mm_simple(f"mla{l}_dX", dlat, w["w_in_ext"], "nt", F32, tk=n_ext, extras=(dz,),
                    epilogue=lambda accs, dz_t: [accs[0] + ALPHA * dz_t])
    dw_in_ext = _mm_simple(f"mla{l}_dWin", h, dlat, "tn", BF16, tm=1024, tn=n_ext, tk=512)
    dw_in, dwq = _mla_ext_grads(dw_in_ext.astype(F32), dwq_ext.astype(F32))
    grads = dict(w_in=dw_in, w_q_b=dwq, w_kv_b=dwkv, w_out=dw_out, q_norm_g=_psum_rows(pgq), kv_norm_g=_psum_rows(pgkv),
                 ln_g=dg, ln_b=db)
    return dh, grads


def _loss_head(y, target):
    T = y.shape[0]
    tm = _tile(T, 512, 8)

    def fn(y_t, t_t):
        d = y_t - t_t
        part = 0.5 * jnp.sum(jnp.mean(d * d, axis=-1, keepdims=True), axis=0, keepdims=True)
        return [d * (1.0 / D_MODEL), jnp.broadcast_to(part, (8, 128))]

    dy, part = _rows("loss_head", T, tm, [_rt(y, tm), _rt(target, tm)],
                     [_rt_out(T, D_MODEL, F32, tm), ((T // tm * 8, 128), F32, (8, 128), lambda i: (i, 0))], fn)
    return jnp.sum(part.reshape(-1, 8, 128)[:, 0, 0]), dy


_S5_SMALL = ("lam_re", "lam_im", "log_step", "b_re", "b_im", "c_re", "c_im", "d", "b_glu")


def _local_step(x, positions, target, W):
    row = lambda a, i: a[i].reshape(1, -1)
    w_in_ext, wq_ext = _mla_ext_weights(W["mla_w_in"][0], W["mla_w_q_b"][0])
    cs = _rope_table(positions)
    h, saves = x, []
    for l in range(DEPTH):
        kind, slot = LAYER_MIXER[l], l // 3
        g, b = row(W["ln_mix_g"], l), row(W["ln_mix_b"], l)
        if kind == 0:
            w = {k: W["s5_" + k][slot] for k in ("w_in", "w_glu", "w_out")}
            small = {k: W["s5_" + k][slot] for k in _S5_SMALL}
            h, sv = _s5_fwd(l, h, w, small, g, b)
        elif kind == 1:
            w = dict(w_in=W["hg_w_in"][slot], w_out=W["hg_w_out"][slot])
            small = dict(lower_bound=W["hg_lower_bound"], norm_g=W["hg_norm_g"][slot])
            h, sv = _hg_fwd(l, h, w, small, g, b)
        else:
            w = dict(w_in_ext=w_in_ext, wq_ext=wq_ext, w_kv_b=W["mla_w_kv_b"][slot], w_out=W["mla_w_out"][slot])
            small = dict(q_norm_g=W["mla_q_norm_g"][slot], kv_norm_g=W["mla_kv_norm_g"][slot])
            h, sv = _mla_fwd(l, h, w, small, g, b, cs)
        h, fsv = _ffn_fwd(l, h, W["ffn_w_in"][l], W["ffn_w_out"][l], row(W["ln_ffn_g"], l), row(W["ln_ffn_b"], l))
        saves.append((w, small, sv, fsv))
    loss, dh = _loss_head(h, target)
    per_layer = [None] * DEPTH
    for l in reversed(range(DEPTH)):
        kind = LAYER_MIXER[l]
        w, small, sv, fsv = saves[l]
        dh, gf = _ffn_bwd(l, dh, fsv, W["ffn_w_in"][l], W["ffn_w_out"][l], row(W["ln_ffn_g"], l))
        g = row(W["ln_mix_g"], l)
        if kind == 0:
            dh, gm = _s5_bwd(l, dh, sv, w, small, g)
        elif kind == 1:
            dh, gm = _hg_bwd(l, dh, sv, w, small, g)
        else:
            dh, gm = _mla_bwd(l, dh, sv, w, g)
        per_layer[l] = (gm, gf)
    grads = {}
    stack = lambda xs: jnp.stack([a.astype(F32) if a.dtype != BF16 else a for a in xs])
    grads["ln_mix_g"] = stack([per_layer[l][0]["ln_g"] for l in range(DEPTH)])
    grads["ln_mix_b"] = stack([per_layer[l][0]["ln_b"] for l in range(DEPTH)])
    grads["ln_ffn_g"] = stack([per_layer[l][1]["ln_g"] for l in range(DEPTH)])
    grads["ln_ffn_b"] = stack([per_layer[l][1]["ln_b"] for l in range(DEPTH)])
    grads["ffn_w_in"] = stack([per_layer[l][1]["w_in"] for l in range(DEPTH)])
    grads["ffn_w_out"] = stack([per_layer[l][1]["w_out"] for l in range(DEPTH)])
    s5_layers = [l for l in range(DEPTH) if LAYER_MIXER[l] == 0]
    for k in ("w_in", "w_glu", "w_out") + _S5_SMALL:
        grads["s5_" + k] = stack([per_layer[l][0][k] for l in s5_layers])
    hg = per_layer[1][0]
    grads["hg_w_in"], grads["hg_w_out"] = hg["w_in"][None], hg["w_out"][None]
    grads["hg_lower_bound"], grads["hg_norm_g"] = hg["lower_bound"], hg["norm_g"][None]
    for k, v in per_layer[2][0].items():
        if not k.startswith("ln_"):
            grads["mla_" + k] = v[None]
    return loss, dh, grads


def _here():
    return lax.axis_index("x"), lax.axis_index("y"), lax.axis_index("c")


def _any_spec():
    return pl.BlockSpec(memory_space=pl.ANY)


def _chip_exchange(name, xs, scatter):
    n = len(xs)

    def body(*refs):
        ins, outs = refs[:n], refs[n:2 * n]
        send_sems, recv_sems, loc_sems = refs[2 * n:]
        x, y, c = _here()
        me = 2 * x + y
        peers = [(1 - x, y), (x, 1 - y), (1 - x, 1 - y)]
        copies = []
        for t in range(n):
            src_of = (lambda p, t=t: ins[t].at[p]) if scatter else (lambda p, t=t: ins[t])
            loc = pltpu.make_async_copy(src_of(me), outs[t].at[me], loc_sems.at[t])
            loc.start()
            copies.append(loc)
            for j, (px, py) in enumerate(peers):
                cp = pltpu.make_async_remote_copy(
                    src_ref=src_of(2 * px + py), dst_ref=outs[t].at[me], send_sem=send_sems.at[t, j],
                    recv_sem=recv_sems.at[t, j], device_id=(px, py, c), device_id_type=MESH)
                cp.start()
                copies.append(cp)
        for cp in copies:
            cp.wait()

    out_shape = [jax.ShapeDtypeStruct(a.shape if scatter else (N_CHIPS,) + a.shape, a.dtype) for a in xs]
    return pl.pallas_call(
        body, name=name, in_specs=[_any_spec()] * n, out_specs=[_any_spec()] * n, out_shape=out_shape,
        scratch_shapes=[pltpu.SemaphoreType.DMA((n, 3)), pltpu.SemaphoreType.DMA((n, 3)), pltpu.SemaphoreType.DMA((n,))],
    )(*xs)


def _core_exchange(name, a, n_chunks=8):
    rows = a.shape[0] // n_chunks
    assert rows * n_chunks == a.shape[0] and rows % 8 == 0, (a.shape, n_chunks)

    def body(a_ref, o_ref, send_sems, recv_sems, loc_sem):
        x, y, c = _here()
        loc = pltpu.make_async_copy(a_ref, o_ref.at[c], loc_sem)
        loc.start()
        copies = [loc]
        for i in range(n_chunks):
            part = pl.ds(i * rows, rows)
            cp = pltpu.make_async_remote_copy(
                src_ref=a_ref.at[part], dst_ref=o_ref.at[c, part], send_sem=send_sems.at[i], recv_sem=recv_sems.at[i],
                device_id=(x, y, 1 - c), device_id_type=MESH)
            cp.start()
            copies.append(cp)
        for cp in copies:
            cp.wait()

    return pl.pallas_call(
        body, name=name, in_specs=[_any_spec()], out_specs=_any_spec(),
        out_shape=jax.ShapeDtypeStruct((2,) + a.shape, a.dtype),
        scratch_shapes=[pltpu.SemaphoreType.DMA((n_chunks,)), pltpu.SemaphoreType.DMA((n_chunks,)), pltpu.SemaphoreType.DMA],
    )(a)


def _all_exchange(name, a):
    def body(a_ref, o_ref, send_sems, recv_sems, loc_sem):
        x, y, c = _here()
        me = 4 * x + 2 * y + c
        loc = pltpu.make_async_copy(a_ref, o_ref.at[me], loc_sem)
        loc.start()
        copies = [loc]
        for mask in range(1, N_DEV):
            fx, fy, fc = (mask >> 2) & 1, (mask >> 1) & 1, mask & 1
            peer = (1 - x if fx else x, 1 - y if fy else y, 1 - c if fc else c)
            cp = pltpu.make_async_remote_copy(src_ref=a_ref, dst_ref=o_ref.at[me], send_sem=send_sems.at[mask - 1],
                                              recv_sem=recv_sems.at[mask - 1], device_id=peer, device_id_type=MESH)
            cp.start()
            copies.append(cp)
        for cp in copies:
            cp.wait()

    return pl.pallas_call(
        body, name=name, in_specs=[_any_spec()], out_specs=_any_spec(),
        out_shape=jax.ShapeDtypeStruct((N_DEV,) + a.shape, a.dtype),
        scratch_shapes=[pltpu.SemaphoreType.DMA((N_DEV - 1,)), pltpu.SemaphoreType.DMA((N_DEV - 1,)), pltpu.SemaphoreType.DMA],
    )(a)


def _sum_leading(name, a, out_dtype=F32):
    n, R, C = a.shape
    tr = _tile(R, 512, 16)

    def fn(a_t):
        s = a_t[0].astype(F32)
        for i in range(1, n):
            s = s + a_t[i].astype(F32)
        return [s]

    return _grid_call(name, (R // tr,), [(a, (n, tr, C), lambda i: (0, i, 0))],
                      [((R, C), out_dtype, (tr, C), lambda i: (i, 0))], fn)[0]


def _adamw(name, g_parts, w, m, v):
    R, C = w.shape
    tr = _tile(R, 256, 8)
    ng = len(g_parts)
    c1 = 1.0 / (1.0 - ADAM_B1 ** ADAM_STEP)
    c2 = 1.0 / (1.0 - ADAM_B2 ** ADAM_STEP)

    def fn(*tiles):
        g = tiles[0]
        for t in tiles[1:ng]:
            g = g + t
        w_t, m_t, v_t = tiles[ng:]
        m_n = ADAM_B1 * m_t + (1.0 - ADAM_B1) * g
        v_n = ADAM_B2 * v_t + (1.0 - ADAM_B2) * (g * g)
        delta = -ADAM_LR * ((m_n * c1) / (jnp.sqrt(v_n * c2) + ADAM_EPS) + ADAM_WD * w_t)
        return [g, delta, m_n, v_n]

    spec = lambda a: (a, (tr, C), lambda i: (i, 0))
    return _grid_call(name, (R // tr,), [spec(a) for a in list(g_parts) + [w, m, v]],
                      [((R, C), F32, (tr, C), lambda i: (i, 0))] * 4, fn)


_WEIGHTS = ("ln_mix_g", "ln_mix_b", "ln_ffn_g", "ln_ffn_b", "ffn_w_in", "ffn_w_out", "s5_w_in", "s5_lam_re", "s5_lam_im",
            "s5_log_step", "s5_b_re", "s5_b_im", "s5_c_re", "s5_c_im", "s5_d", "s5_w_glu", "s5_b_glu", "s5_w_out", "hg_w_in",
            "hg_lower_bound", "hg_norm_g", "hg_w_out", "mla_w_in", "mla_q_norm_g", "mla_w_q_b", "mla_kv_norm_g", "mla_w_kv_b",
            "mla_w_out")
_BIG = {"ffn_w_in": 2, "ffn_w_out": 1, "s5_w_in": 1, "s5_w_glu": 1, "s5_w_out": 1, "hg_w_in": 2, "hg_w_out": 1,
        "mla_w_in": 1, "mla_w_q_b": 2, "mla_w_kv_b": 2, "mla_w_out": 1}
_SMALL_SHARDED = {"s5_d": 1, "s5_b_glu": 1, "mla_q_norm_g": 1, "mla_kv_norm_g": 1}
_REPLICATED = tuple(n for n in _WEIGHTS if n not in _BIG and n not in _SMALL_SHARDED)
LANES = 1024


def _pack(arrs, dtype, row_mult, lead=0):
    rows, segs, r = [], [], 0
    for a in arrs:
        lead_shape = a.shape[:lead]
        flat = a.astype(dtype).reshape(lead_shape + (-1,))
        n = -(-flat.shape[-1] // LANES)
        flat = jnp.pad(flat, [(0, 0)] * lead + [(0, n * LANES - flat.shape[-1])])
        rows.append(flat.reshape(lead_shape + (n, LANES)))
        segs.append((r, n))
        r += n
    pad = -r % row_mult
    if pad:
        rows.append(jnp.zeros(rows[0].shape[:lead] + (pad, LANES), dtype))
    return jnp.concatenate(rows, axis=lead), segs


def _unpack(packed, segs, shapes):
    out = []
    for (r0, n), shp in zip(segs, shapes):
        size = int(np.prod(shp))
        out.append(packed[..., r0:r0 + n, :].reshape(packed.shape[:-2] + (n * LANES,))[..., :size].reshape(packed.shape[:-2] + tuple(shp)))
    return out


def _unshard(stacked, axis):
    moved = jnp.moveaxis(stacked, 0, axis)
    shp = list(moved.shape)
    return moved.reshape(shp[:axis] + [shp[axis] * shp[axis + 1]] + shp[axis + 2:])


def _shard_split(full, axis):
    shp = list(full.shape)
    a = full.reshape(shp[:axis] + [N_CHIPS, shp[axis] // N_CHIPS] + shp[axis + 1:])
    return jnp.moveaxis(a, axis, 0)


def _train_step(x, positions, target, w, m, v):
    big, small_sh = list(_BIG), list(_SMALL_SHARDED)
    chip = 2 * lax.axis_index("x") + lax.axis_index("y")

    big_pack, big_segs = _pack([w[n] for n in big], BF16, 16)
    sm_pack, sm_segs = _pack([w[n] for n in small_sh], F32, 8)
    big_all, sm_all = _chip_exchange("gather_weights", [big_pack, sm_pack], scatter=False)
    W = {n: w[n] for n in _REPLICATED}
    for n, s in zip(big, _unpack(big_all, big_segs, [w[n].shape for n in big])):
        W[n] = _unshard(s, _BIG[n])
    for n, s in zip(small_sh, _unpack(sm_all, sm_segs, [w[n].shape for n in small_sh])):
        W[n] = _unshard(s, _SMALL_SHARDED[n])

    loss_local, grad_x, G = _local_step(x, positions, target, W)
    loss = lax.psum(loss_local, ("x", "y", "c"))
    out = {}

    g_pack, _ = _pack([_shard_split(G[n].astype(BF16), _BIG[n]) for n in big], BF16, 16, lead=1)
    recv = _chip_exchange("scatter_grads", [g_pack], scatter=True)[0]
    pair = _core_exchange("swap_core_sums", _sum_leading("sum_chips", recv))
    for n, (r0, nr) in zip(big, big_segs):
        as_rows = lambda a: a.reshape(nr, LANES)
        res = _adamw("adamw_" + n, [pair[0, r0:r0 + nr], pair[1, r0:r0 + nr]], as_rows(w[n]), as_rows(m[n]), as_rows(v[n]))
        out[n] = tuple(r.reshape(w[n].shape) for r in res)

    small = list(_REPLICATED) + small_sh
    s_pack, s_segs = _pack([G[n] for n in small], F32, 16)
    total = _sum_leading("sum_small", _all_exchange("gather_small_grads", s_pack))
    g_small = dict(zip(small, _unpack(total, s_segs, [G[n].shape for n in small])))
    for n in small_sh:
        width = w[n].shape[1]
        g_small[n] = lax.dynamic_slice_in_dim(g_small[n], chip * width, width, axis=1)
    packs = [_pack([d[n] for n in small], F32, 8)[0] for d in (g_small, w, m, v)]
    _, a_segs = _pack([w[n] for n in small], F32, 8)
    res = _adamw("adamw_small", [packs[0]], packs[1], packs[2], packs[3])
    unpacked = [_unpack(r, a_segs, [w[n].shape for n in small]) for r in res]
    for i, n in enumerate(small):
        out[n] = tuple(u[i] for u in unpacked)
    return loss, grad_x, out


def kernel(x, positions, ln_mix_g, ln_mix_b, ln_ffn_g, ln_ffn_b, ffn_w_in, ffn_w_out, s5_w_in, s5_lam_re, s5_lam_im,
           s5_log_step, s5_b_re, s5_b_im, s5_c_re, s5_c_im, s5_d, s5_w_glu, s5_b_glu, s5_w_out, hg_w_in,
           hg_lower_bound, hg_norm_g, hg_w_out, mla_w_in, mla_q_norm_g, mla_w_q_b, mla_kv_norm_g, mla_w_kv_b,
           mla_w_out, loss_target, m_ln_mix_g, m_ln_mix_b, m_ln_ffn_g, m_ln_ffn_b, m_ffn_w_in, m_ffn_w_out, m_s5_w_in,
           m_s5_lam_re, m_s5_lam_im, m_s5_log_step, m_s5_b_re, m_s5_b_im, m_s5_c_re, m_s5_c_im, m_s5_d, m_s5_w_glu,
           m_s5_b_glu, m_s5_w_out, m_hg_w_in, m_hg_lower_bound, m_hg_norm_g, m_hg_w_out, m_mla_w_in, m_mla_q_norm_g,
           m_mla_w_q_b, m_mla_kv_norm_g, m_mla_w_kv_b, m_mla_w_out, v_ln_mix_g, v_ln_mix_b, v_ln_ffn_g, v_ln_ffn_b,
           v_ffn_w_in, v_ffn_w_out, v_s5_w_in, v_s5_lam_re, v_s5_lam_im, v_s5_log_step, v_s5_b_re, v_s5_b_im,
           v_s5_c_re, v_s5_c_im, v_s5_d, v_s5_w_glu, v_s5_b_glu, v_s5_w_out, v_hg_w_in, v_hg_lower_bound, v_hg_norm_g,
           v_hg_w_out, v_mla_w_in, v_mla_q_norm_g, v_mla_w_q_b, v_mla_kv_norm_g, v_mla_w_kv_b, v_mla_w_out):
    args = (ln_mix_g, ln_mix_b, ln_ffn_g, ln_ffn_b, ffn_w_in, ffn_w_out, s5_w_in, s5_lam_re, s5_lam_im,
            s5_log_step, s5_b_re, s5_b_im, s5_c_re, s5_c_im, s5_d, s5_w_glu, s5_b_glu, s5_w_out, hg_w_in,
            hg_lower_bound, hg_norm_g, hg_w_out, mla_w_in, mla_q_norm_g, mla_w_q_b, mla_kv_norm_g, mla_w_kv_b,
            mla_w_out, m_ln_mix_g, m_ln_mix_b, m_ln_ffn_g, m_ln_ffn_b, m_ffn_w_in, m_ffn_w_out,
            m_s5_w_in, m_s5_lam_re, m_s5_lam_im, m_s5_log_step, m_s5_b_re, m_s5_b_im, m_s5_c_re, m_s5_c_im, m_s5_d,
            m_s5_w_glu, m_s5_b_glu, m_s5_w_out, m_hg_w_in, m_hg_lower_bound, m_hg_norm_g, m_hg_w_out, m_mla_w_in,
            m_mla_q_norm_g, m_mla_w_q_b, m_mla_kv_norm_g, m_mla_w_kv_b, m_mla_w_out, v_ln_mix_g, v_ln_mix_b,
            v_ln_ffn_g, v_ln_ffn_b, v_ffn_w_in, v_ffn_w_out, v_s5_w_in, v_s5_lam_re, v_s5_lam_im, v_s5_log_step,
            v_s5_b_re, v_s5_b_im, v_s5_c_re, v_s5_c_im, v_s5_d, v_s5_w_glu, v_s5_b_glu, v_s5_w_out, v_hg_w_in,
            v_hg_lower_bound, v_hg_norm_g, v_hg_w_out, v_mla_w_in, v_mla_q_norm_g, v_mla_w_q_b, v_mla_kv_norm_g,
            v_mla_w_kv_b, v_mla_w_out)
    nw = len(_WEIGHTS)
    w = dict(zip(_WEIGHTS, args[:nw]))
    m = dict(zip(_WEIGHTS, args[nw:2 * nw]))
    v = dict(zip(_WEIGHTS, args[2 * nw:]))
    loss, grad_x, out = _train_step(x[0], positions[0], loss_target[0], w, m, v)
    res = [loss, grad_x[None]]
    for i in range(4):
        res += [out[n][i] for n in _WEIGHTS]
    return tuple(res)
```

```python
import functools
import math

import numpy as np
import jax
import jax.numpy as jnp
from jax import lax
from jax.experimental import pallas as pl
from jax.experimental.pallas import tpu as pltpu

F32 = jnp.float32
BF16 = jnp.bfloat16

D_MODEL = 1024
DEPTH = 4
LAYER_MIXER = (0, 1, 2, 0)
S5_GROUP = 16
S5_GROUPS = 64
S5_STATE = 64
S5_CHUNK = 16
HG_HEADS = 8
HG_DIM = 128
HG_CHUNK = 128
MLA_HEADS = 8
MLA_NOPE = 128
MLA_ROPE = 64
MLA_V = 128
MLA_Q_LORA = 384
MLA_KV_LORA = 256
ROPE_THETA = 10000.0
FFN_HIDDEN = 2816
ALPHA = (2 * DEPTH) ** 0.25
LN_EPS = 1e-5
RMS_EPS = 1e-6
ADAM_LR, ADAM_B1, ADAM_B2, ADAM_EPS, ADAM_WD, ADAM_STEP = 0.001, 0.9, 0.999, 1e-08, 0.01, 10
VMEM_LIMIT_BYTES = 56 * 1024 * 1024
MESH = pl.DeviceIdType.MESH
N_CHIPS = 4
N_DEV = 8


def _cparams():
    return pltpu.CompilerParams(vmem_limit_bytes=VMEM_LIMIT_BYTES)


def _tile(n, want, mult):
    t = min(want, n)
    t -= t % mult
    while t >= mult:
        if n % t == 0:
            return t
        t -= mult
    return n


def _sigmoid(x):
    return 1.0 / (1.0 + jnp.exp(-x))


def _silu(x):
    return x * _sigmoid(x)


def _dsilu(x):
    s = _sigmoid(x)
    return s * (1.0 + x * (1.0 - s))


_GELU_C = math.sqrt(2.0 / math.pi)


def _gelu(x):
    return 0.5 * x * (1.0 + jnp.tanh(_GELU_C * (x + 0.044715 * x * x * x)))


def _dgelu(x):
    t = jnp.tanh(_GELU_C * (x + 0.044715 * x * x * x))
    return 0.5 * (1.0 + t) + 0.5 * x * (1.0 - t * t) * _GELU_C * (1.0 + 3 * 0.044715 * x * x)


def _layer_norm(z, g, b):
    mu = jnp.mean(z, axis=-1, keepdims=True)
    zc = z - mu
    var = jnp.mean(zc * zc, axis=-1, keepdims=True)
    return zc * lax.rsqrt(var + LN_EPS) * g + b


def _bcast8(row):
    return jnp.broadcast_to(row, (8, row.shape[-1]))


def _stack_rows8(rows):
    n = rows[0].shape[-1]
    idx = lax.broadcasted_iota(jnp.int32, (8, n), 0)
    out = jnp.zeros((8, n), F32)
    for i, r in enumerate(rows):
        out = jnp.where(idx == i, _bcast8(r), out)
    return out


def _psum_rows(a):
    return a.reshape(-1, 8, a.shape[-1])[:, 0, :].sum(axis=0)


_DN = {"nn": ((1,), (0,)), "nt": ((1,), (1,)), "tn": ((0,), (0,))}


def _mm(name, grid, a_defs, b_defs, pairs, n_acc, acc_shape, extra_defs, out_defs, epilogue):
    nk = grid[2]
    na, nb, ne, no = len(a_defs), len(b_defs), len(extra_defs), len(out_defs)

    def body(*refs):
        a_refs = refs[:na]
        b_refs = refs[na:na + nb]
        e_refs = refs[na + nb:na + nb + ne]
        o_refs = refs[na + nb + ne:na + nb + ne + no]
        acc = refs[-1]
        k = pl.program_id(2)

        @pl.when(k == 0)
        def _():
            acc[...] = jnp.zeros(acc.shape, F32)

        for (ai, bi, ci, mode) in pairs:
            a = a_refs[ai][...].astype(BF16)
            b = b_refs[bi][...].astype(BF16)
            acc[ci] += lax.dot_general(a, b, (_DN[mode], ((), ())), preferred_element_type=F32)

        @pl.when(k == nk - 1)
        def _():
            outs = epilogue([acc[c] for c in range(n_acc)], *[e[...] for e in e_refs])
            for o_ref, o in zip(o_refs, outs):
                o_ref[...] = o.astype(o_ref.dtype)

    in_specs = [pl.BlockSpec(d[1], d[2]) for d in list(a_defs) + list(b_defs) + list(extra_defs)]
    out_specs = [pl.BlockSpec(d[2], d[3]) for d in out_defs]
    out_shape = [jax.ShapeDtypeStruct(d[0], d[1]) for d in out_defs]
    res = pl.pallas_call(
        body, name=name, grid=grid, in_specs=in_specs, out_specs=out_specs, out_shape=out_shape,
        scratch_shapes=[pltpu.VMEM((n_acc,) + tuple(acc_shape), F32)], compiler_params=_cparams(),
    )(*[d[0] for d in list(a_defs) + list(b_defs) + list(extra_defs)])
    return res


def _mm_simple(name, a, b, mode, out_dtype, tm=512, tn=1024, tk=1024, extras=(), epilogue=None, n_out=1,
               out_dtypes=None, psum_outs=0, j_outer=False):
    if mode == "nn":
        (M, K), (K2, N) = a.shape, b.shape
    elif mode == "nt":
        (M, K), (N, K2) = a.shape, b.shape
    else:
        (K, M), (K2, N) = a.shape, b.shape
    assert K == K2, (name, a.shape, b.shape, mode)
    tm, tn, tk = _tile(M, tm, 8), _tile(N, tn, 128), _tile(K, tk, 128)
    grid = (M // tm, N // tn, K // tk)
    if mode == "nn":
        a_def = (a, (tm, tk), lambda i, j, k: (i, k))
        b_def = (b, (tk, tn), lambda i, j, k: (k, j))
    elif mode == "nt":
        a_def = (a, (tm, tk), lambda i, j, k: (i, k))
        b_def = (b, (tn, tk), lambda i, j, k: (j, k))
    else:
        a_def = (a, (tk, tm), lambda i, j, k: (k, i))
        b_def = (b, (tk, tn), lambda i, j, k: (k, j))
    extra_defs = []
    for e in extras:
        if e.shape[0] == 1:
            extra_defs.append((e, (1, tn), lambda i, j, k: (0, j)))
        else:
            extra_defs.append((e, (tm, tn), lambda i, j, k: (i, j)))
    out_dtypes = out_dtypes or [out_dtype] * n_out
    out_defs = [((M, N), dt, (tm, tn), lambda i, j, k: (i, j)) for dt in out_dtypes]
    out_defs += [((M // tm * 8, N), F32, (8, tn), lambda i, j, k: (i, j)) for _ in range(psum_outs)]
    if epilogue is None:
        epilogue = lambda accs: [accs[0]]
    a_defs, b_defs = [a_def], [b_def]
    if j_outer:
        swap = lambda d: d[:-1] + ((lambda f: lambda j, i, k: f(i, j, k))(d[-1]),)
        grid = (grid[1], grid[0], grid[2])
        a_defs, b_defs = [swap(a_def)], [swap(b_def)]
        extra_defs, out_defs = [swap(d) for d in extra_defs], [swap(d) for d in out_defs]
    res = _mm(name, grid, a_defs, b_defs, [(0, 0, 0, mode)], 1, (tm, tn), extra_defs, out_defs, epilogue)
    return res[0] if len(res) == 1 else res


def _rows(name, T, tm, in_defs, out_defs, fn):
    ni = len(in_defs)

    def body(*refs):
        outs = fn(*[r[...] for r in refs[:ni]])
        for o_ref, o in zip(refs[ni:], outs):
            o_ref[...] = o.astype(o_ref.dtype)

    res = pl.pallas_call(
        body, name=name, grid=(T // tm,),
        in_specs=[pl.BlockSpec(d[1], d[2]) for d in in_defs],
        out_specs=[pl.BlockSpec(d[2], d[3]) for d in out_defs],
        out_shape=[jax.ShapeDtypeStruct(d[0], d[1]) for d in out_defs],
        compiler_params=_cparams(),
    )(*[d[0] for d in in_defs])
    return res


def _grid_call(name, grid, in_defs, out_defs, fn, acc_outs=()):
    ni = len(in_defs)

    def body(*refs):
        outs = fn(*[r[...] for r in refs[:ni]])
        first = pl.program_id(len(grid) - 1) == 0
        for idx, (o_ref, o) in enumerate(zip(refs[ni:], outs)):
            if idx in acc_outs:
                @pl.when(first)
                def _(o_ref=o_ref, o=o):
                    o_ref[...] = o.astype(o_ref.dtype)

                @pl.when(jnp.logical_not(first))
                def _(o_ref=o_ref, o=o):
                    o_ref[...] += o.astype(o_ref.dtype)
            else:
                o_ref[...] = o.astype(o_ref.dtype)

    return pl.pallas_call(
        body, name=name, grid=grid,
        in_specs=[pl.BlockSpec(d[1], d[2]) for d in in_defs],
        out_specs=[pl.BlockSpec(d[2], d[3]) for d in out_defs],
        out_shape=[jax.ShapeDtypeStruct(d[0], d[1]) for d in out_defs],
        compiler_params=_cparams(),
    )(*[d[0] for d in in_defs])


def _rt(a, tm):
    return (a, (tm, a.shape[1]), lambda i: (i, 0))


def _full(a):
    return (a, a.shape, lambda i: (0,) * a.ndim)


def _rt_out(T, n, dt, tm):
    return ((T, n), dt, (tm, n), lambda i: (i, 0))


def _ps_out(T, n, tm):
    return ((T // tm * 8, n), F32, (8, n), lambda i: (i, 0))


def _out_proj_ln(name, a, w, h_in, g, b):
    def epi(accs, h_t, g_t, b_t):
        z = ALPHA * h_t + accs[0]
        return [z, _layer_norm(z, g_t, b_t)]
    return _mm_simple(name, a, w, "nn", F32, tm=512, tn=D_MODEL, tk=FFN_HIDDEN, extras=(h_in, g, b), epilogue=epi, n_out=2)


def _ln_bwd(name, dh, z, g):
    T = dh.shape[0]
    tm = _tile(T, 512, 8)

    def fn(dh_t, z_t, g_t):
        mu = jnp.mean(z_t, axis=-1, keepdims=True)
        zc = z_t - mu
        var = jnp.mean(zc * zc, axis=-1, keepdims=True)
        rstd = lax.rsqrt(var + LN_EPS)
        xhat = zc * rstd
        dxh = dh_t * g_t
        m1 = jnp.mean(dxh, axis=-1, keepdims=True)
        m2 = jnp.mean(dxh * xhat, axis=-1, keepdims=True)
        dz = rstd * (dxh - m1 - xhat * m2)
        return [dz, _bcast8(jnp.sum(dh_t * xhat, axis=0, keepdims=True)), _bcast8(jnp.sum(dh_t, axis=0, keepdims=True))]

    dz, pg, pb = _rows(name, T, tm, [_rt(dh, tm), _rt(z, tm), _full(g)],
                       [_rt_out(T, D_MODEL, F32, tm), _ps_out(T, D_MODEL, tm), _ps_out(T, D_MODEL, tm)], fn)
    return dz, _psum_rows(pg), _psum_rows(pb)


def _ffn_fwd(l, h, w_in, w_out, g, b):
    T = h.shape[0]
    tm, tn = _tile(T, 512, 8), 1408
    nj = FFN_HIDDEN // tn
    grid = (nj, T // tm, 1)

    def epi(accs):
        gg, uu = accs
        return [gg, uu, _silu(gg) * uu]

    G, U, A = _mm(
        f"ffn{l}_in", grid, [(h, (tm, D_MODEL), lambda j, i, k: (i, 0))],
        [(w_in, (D_MODEL, tn), lambda j, i, k: (0, j)), (w_in, (D_MODEL, tn), lambda j, i, k: (0, j + nj))],
        [(0, 0, 0, "nn"), (0, 1, 1, "nn")], 2, (tm, tn), [],
        [((T, FFN_HIDDEN), F32, (tm, tn), lambda j, i, k: (i, j)),
         ((T, FFN_HIDDEN), F32, (tm, tn), lambda j, i, k: (i, j)),
         ((T, FFN_HIDDEN), BF16, (tm, tn), lambda j, i, k: (i, j))], epi)
    z, h_out = _out_proj_ln(f"ffn{l}_out", A, w_out, h, g, b)
    return h_out, (h, G, U, A, z)


def _ffn_bwd(l, dh_out, saved, w_in, w_out, g):
    h, G, U, A, z = saved
    T = h.shape[0]
    dz, dg, db = _ln_bwd(f"ffn{l}_lnb", dh_out, z, g)

    def epi(accs, g_t, u_t):
        da = accs[0]
        return [da * u_t * _dsilu(g_t), da * _silu(g_t)]

    dG, dU = _mm_simple(f"ffn{l}_dA", dz, w_out, "nt", BF16, tm=512, tn=1408, tk=1024, extras=(G, U), epilogue=epi, n_out=2,
                        j_outer=True)
    dw_out = _mm_simple(f"ffn{l}_dWout", A, dz, "tn", BF16, tm=1408, tn=1024, tk=512)
    dw_g = _mm_simple(f"ffn{l}_dWg", h, dG, "tn", BF16, tm=1024, tn=1408, tk=512)
    dw_u = _mm_simple(f"ffn{l}_dWu", h, dU, "tn", BF16, tm=1024, tn=1408, tk=512)
    tm, tk = _tile(T, 256, 8), FFN_HIDDEN
    dh = _mm(
        f"ffn{l}_dX", (T // tm, 1, 1),
        [(dG, (tm, tk), lambda i, j, k: (i, 0)), (dU, (tm, tk), lambda i, j, k: (i, 0))],
        [(w_in, (D_MODEL, tk), lambda i, j, k: (0, 0)), (w_in, (D_MODEL, tk), lambda i, j, k: (0, 1))],
        [(0, 0, 0, "nt"), (1, 1, 0, "nt")], 1, (tm, D_MODEL),
        [(dz, (tm, D_MODEL), lambda i, j, k: (i, 0))],
        [((T, D_MODEL), F32, (tm, D_MODEL), lambda i, j, k: (i, 0))],
        lambda accs, dz_t: [accs[0] + ALPHA * dz_t])[0]
    return dh, dict(w_in=jnp.concatenate([dw_g, dw_u], axis=1), w_out=dw_out, ln_g=dg, ln_b=db)


def _s5_matrices(lam_re, lam_im, log_step, b_re, b_im, c_re, c_im):
    L, hp = S5_CHUNK, lax.Precision.HIGHEST
    out = {}
    mt_total = 0.0
    for d in range(2):
        lam = lax.complex(lam_re[d], lam_im[d])
        step = jnp.exp(log_step[d])[:, None]
        lam_dt = lam * step
        lam_bar = jnp.exp(lam_dt)
        b_bar = ((lam_bar - 1.0) / lam)[..., None] * lax.complex(b_re[d], b_im[d])
        c = lax.complex(c_re[d], c_im[d])
        pw = jnp.exp(lam_dt[None] * jnp.arange(L + 1, dtype=F32)[:, None, None])
        kj = jnp.einsum("ghp,jgp,gpk->gjhk", c, pw[:L], b_bar, precision=hp).real
        lag = np.arange(L)[None, :] - np.arange(L)[:, None]
        lag = lag if d == 0 else -lag
        sel = np.stack([(lag == j) for j in range(L)]).astype(np.float32)
        mt = jnp.einsum("jst,gjab->gsbta", sel, kj, precision=hp).reshape(S5_GROUPS, 16 * L, 16 * L)
        mt_total = mt_total + mt
        pw_dist = jnp.flip(pw[:L], 0) if d == 0 else pw[:L]
        pc = pw_dist.transpose(1, 0, 2)[:, :, None, :] * b_bar.transpose(0, 2, 1)[:, None, :, :]
        pm = jnp.concatenate([pc.real, pc.imag], axis=-1).reshape(S5_GROUPS, 16 * L, 2 * S5_STATE)
        pw_read = pw[1:] if d == 0 else jnp.flip(pw[1:], 0)
        qc = c[:, None, :, :] * pw_read.transpose(1, 0, 2)[:, :, None, :]
        qm = jnp.concatenate([qc.real, -qc.imag], axis=-1).reshape(S5_GROUPS, 16 * L, 2 * S5_STATE).transpose(0, 2, 1)
        a = pw[L]
        out[f"p{d}"], out[f"q{d}"] = pm, qm
        out[f"a{d}"] = jnp.concatenate([a.real, a.imag], axis=-1)
    out["mt"] = mt_total
    return out


def _s5_apow(lam_re, lam_im, log_step, n_steps, conj):
    lam_dt = lax.complex(lam_re, lam_im) * jnp.exp(log_step)[..., None]
    k = (S5_CHUNK * 2.0 ** jnp.arange(n_steps, dtype=F32))[None, None, :, None]
    a = jnp.exp(lam_dt[:, :, None, :] * k)
    re, im = a.real, (-a.imag if conj else a.imag)
    return jnp.stack([jnp.concatenate([re, re], -1), jnp.concatenate([-im, im], -1)], axis=3)


def _shift_rows(x, s, down):
    n = x.shape[0]
    if s >= n:
        return jnp.zeros_like(x)
    if s % 8 == 0:
        z = jnp.zeros((s, x.shape[1]), x.dtype)
        return jnp.concatenate([z, x[:n - s]], axis=0) if down else jnp.concatenate([x[s:], z], axis=0)
    row = lax.broadcasted_iota(jnp.int32, x.shape, 0)
    if down:
        return jnp.where(row >= s, pltpu.roll(x, s, 0), 0.0)
    return jnp.where(row < n - s, pltpu.roll(x, n - s, 0), 0.0)


def _cmul(x, a1, a2):
    return x * a1 + pltpu.roll(x, S5_STATE, 1) * a2


def _chunk_scan(s, apow_ref, down):
    n = s.shape[0]
    k, sh = 0, 1
    while sh < n:
        s = s + _cmul(_shift_rows(s, sh, down), apow_ref[k, 0:1, :], apow_ref[k, 1:2, :])
        k, sh = k + 1, sh * 2
    return s


def _s5_scan_fwd(name, ug, mats, apow):
    G, C, W = ug.shape
    n_steps = apow.shape[2]

    def body(u_ref, mt_ref, p0_ref, p1_ref, q0_ref, q1_ref, ap0_ref, ap1_ref, y_ref, h0_ref, h1_ref):
        u = u_ref[...]
        s0 = jnp.dot(u, p0_ref[...], preferred_element_type=F32)
        s1 = jnp.dot(u, p1_ref[...], preferred_element_type=F32)
        h0 = _shift_rows(_chunk_scan(s0, ap0_ref, True), 1, True)
        h1 = _shift_rows(_chunk_scan(s1, ap1_ref, False), 1, False)
        y = jnp.dot(u, mt_ref[...], preferred_element_type=F32)
        y += jnp.dot(h0.astype(BF16), q0_ref[...], preferred_element_type=F32)
        y += jnp.dot(h1.astype(BF16), q1_ref[...], preferred_element_type=F32)
        y_ref[...] = y
        h0_ref[...] = h0
        h1_ref[...] = h1

    def gspec(shape):
        return pl.BlockSpec((None,) + shape, lambda g: (g,) + (0,) * len(shape))

    ap0, ap1 = apow[0], apow[1]
    return pl.pallas_call(
        body, name=name, grid=(G,),
        in_specs=[gspec((C, W)), gspec((W, W)), gspec((W, 128)), gspec((W, 128)), gspec((128, W)), gspec((128, W)),
                  gspec((n_steps, 2, 128)), gspec((n_steps, 2, 128))],
        out_specs=[gspec((C, W)), gspec((C, 128)), gspec((C, 128))],
        out_shape=[jax.ShapeDtypeStruct((G, C, W), F32), jax.ShapeDtypeStruct((G, C, 128), F32),
                   jax.ShapeDtypeStruct((G, C, 128), F32)],
        compiler_params=_cparams(),
    )(ug, mats["mt"].astype(BF16), mats["p0"].astype(BF16), mats["p1"].astype(BF16),
      mats["q0"].astype(BF16), mats["q1"].astype(BF16), ap0, ap1)


def _s5_scan_bwd(name, dyg, ug, h0, h1, mats, apow_conj):
    G, C, W = ug.shape
    n_steps = apow_conj.shape[2]

    def body(dy_ref, u_ref, h0_ref, h1_ref, mt_ref, p0_ref, p1_ref, q0_ref, q1_ref, ap0_ref, ap1_ref,
             du_ref, dmt_ref, dp0_ref, dp1_ref, dq0_ref, dq1_ref, da_ref):
        dy, u = dy_ref[...], u_ref[...]
        nt, tn = (_DN["nt"], ((), ())), (_DN["tn"], ((), ()))
        dh0 = lax.dot_general(dy, q0_ref[...], nt, preferred_element_type=F32)
        dh1 = lax.dot_general(dy, q1_ref[...], nt, preferred_element_type=F32)
        ds0 = _chunk_scan(_shift_rows(dh0, 1, False), ap0_ref, False)
        ds1 = _chunk_scan(_shift_rows(dh1, 1, True), ap1_ref, True)
        ds0b, ds1b = ds0.astype(BF16), ds1.astype(BF16)
        du = lax.dot_general(dy, mt_ref[...], nt, preferred_element_type=F32)
        du += lax.dot_general(ds0b, p0_ref[...], nt, preferred_element_type=F32)
        du += lax.dot_general(ds1b, p1_ref[...], nt, preferred_element_type=F32)
        du_ref[...] = du
        dmt_ref[...] = lax.dot_general(u, dy, tn, preferred_element_type=F32)
        dp0_ref[...] = lax.dot_general(u, ds0b, tn, preferred_element_type=F32)
        dp1_ref[...] = lax.dot_general(u, ds1b, tn, preferred_element_type=F32)
        h0v, h1v = h0_ref[...], h1_ref[...]
        dq0_ref[...] = lax.dot_general(h0v.astype(BF16), dy, tn, preferred_element_type=F32)
        dq1_ref[...] = lax.dot_general(h1v.astype(BF16), dy, tn, preferred_element_type=F32)
        rows = [jnp.sum(ds0 * h0v, axis=0, keepdims=True), jnp.sum(ds0 * pltpu.roll(h0v, S5_STATE, 1), axis=0, keepdims=True),
                jnp.sum(ds1 * h1v, axis=0, keepdims=True), jnp.sum(ds1 * pltpu.roll(h1v, S5_STATE, 1), axis=0, keepdims=True)]
        da_ref[...] = _stack_rows8(rows)

    def gspec(shape):
        return pl.BlockSpec((None,) + shape, lambda g: (g,) + (0,) * len(shape))

    f32s = lambda *s: jax.ShapeDtypeStruct((G,) + s, F32)
    return pl.pallas_call(
        body, name=name, grid=(G,),
        in_specs=[gspec((C, W)), gspec((C, W)), gspec((C, 128)), gspec((C, 128)), gspec((W, W)), gspec((W, 128)),
                  gspec((W, 128)), gspec((128, W)), gspec((128, W)), gspec((n_steps, 2, 128)), gspec((n_steps, 2, 128))],
        out_specs=[gspec((C, W)), gspec((W, W)), gspec((W, 128)), gspec((W, 128)), gspec((128, W)), gspec((128, W)),
                   gspec((8, 128))],
        out_shape=[f32s(C, W), f32s(W, W), f32s(W, 128), f32s(W, 128), f32s(128, W), f32s(128, W), f32s(8, 128)],
        compiler_params=_cparams(),
    )(dyg, ug, h0, h1, mats["mt"].astype(BF16), mats["p0"].astype(BF16), mats["p1"].astype(BF16),
      mats["q0"].astype(BF16), mats["q1"].astype(BF16), apow_conj[0], apow_conj[1])


def _to_groups(a):
    T = a.shape[0]
    return a.reshape(T // S5_CHUNK, S5_CHUNK, S5_GROUPS, S5_GROUP).transpose(2, 0, 1, 3).reshape(
        S5_GROUPS, T // S5_CHUNK, S5_CHUNK * S5_GROUP)


def _from_groups(a):
    G, C, W = a.shape
    return a.reshape(G, C, S5_CHUNK, S5_GROUP).transpose(1, 2, 0, 3).reshape(C * S5_CHUNK, G * S5_GROUP)


def _s5_fwd(l, h, w, small, g, b):
    T = h.shape[0]
    tm = _tile(T, 512, 8)
    u, u_bf = _mm_simple(f"s5{l}_in", h, w["w_in"], "nn", None, epilogue=lambda accs: [accs[0], accs[0]], n_out=2,
                         out_dtypes=[F32, BF16])
    ug = _to_groups(u_bf)
    prep = lambda *p: _s5_matrices(*p)
    sp = (small["lam_re"], small["lam_im"], small["log_step"], small["b_re"], small["b_im"], small["c_re"], small["c_im"])
    mats, mats_vjp = jax.vjp(prep, *sp)
    n_steps = max(1, int(math.log2(T // S5_CHUNK)))
    apow = _s5_apow(small["lam_re"], small["lam_im"], small["log_step"], n_steps, False)
    yg, h0, h1 = _s5_scan_fwd(f"s5{l}_scan", ug, mats, apow)
    yssm = _from_groups(yg)
    d_row, bglu_row = small["d"].reshape(1, -1), small["b_glu"].reshape(1, -1)

    def mid(y_t, u_t, d_t):
        return [_gelu(y_t + d_t * u_t)]

    y1 = _rows(f"s5{l}_mid", T, tm, [_rt(yssm, tm), _rt(u, tm), _full(d_row)], [_rt_out(T, D_MODEL, F32, tm)], mid)[0]

    def epi(accs, y1_t, bg_t):
        gate = _sigmoid(accs[0] + bg_t)
        return [y1_t * gate, gate]

    y2, gate = _mm_simple(f"s5{l}_glu", y1, w["w_glu"], "nn", None, extras=(y1, bglu_row), epilogue=epi, n_out=2,
                          out_dtypes=[BF16, F32])
    z, h_out = _out_proj_ln(f"s5{l}_out", y2, w["w_out"], h, g, b)
    return h_out, (h, u, ug, yssm, y1, y2, gate, z, h0, h1, mats, mats_vjp, d_row)


def _s5_bwd(l, dh_out, saved, w, small, g):
    h, u, ug, yssm, y1, y2, gate, z, h0, h1, mats, mats_vjp, d_row = saved
    T = h.shape[0]
    dz, dg, db = _ln_bwd(f"s5{l}_lnb", dh_out, z, g)

    def epi1(accs, y1_t, gate_t):
        dy2 = accs[0]
        dpre = dy2 * y1_t * gate_t * (1.0 - gate_t)
        return [dpre, dy2 * gate_t, _bcast8(jnp.sum(dpre, axis=0, keepdims=True))]

    dpre, dy1a, pbg = _mm_simple(f"s5{l}_dy2", dz, w["w_out"], "nt", None, extras=(y1, gate), epilogue=epi1, n_out=2,
                                 out_dtypes=[BF16, F32], psum_outs=1)
    dw_out = _mm_simple(f"s5{l}_dWout", y2, dz, "tn", BF16, tm=1024, tn=1024, tk=512)

    def epi2(accs, dy1a_t, yssm_t, u_t, d_t):
        dy1 = accs[0] + dy1a_t
        dy = dy1 * _dgelu(yssm_t + d_t * u_t)
        return [dy, dy * d_t, _bcast8(jnp.sum(dy * u_t, axis=0, keepdims=True))]

    dy, du_skip, pdd = _mm_simple(f"s5{l}_dy1", dpre, w["w_glu"], "nt", None, extras=(dy1a, yssm, u, d_row), epilogue=epi2,
                                  n_out=2, out_dtypes=[BF16, F32], psum_outs=1)
    dw_glu = _mm_simple(f"s5{l}_dWglu", y1, dpre, "tn", BF16, tm=1024, tn=1024, tk=512)
    n_steps = max(1, int(math.log2(T // S5_CHUNK)))
    apow_c = _s5_apow(small["lam_re"], small["lam_im"], small["log_step"], n_steps, True)
    dug, dmt, dp0, dp1, dq0, dq1, da = _s5_scan_bwd(f"s5{l}_scanb", _to_groups(dy), ug, h0, h1, mats, apow_c)
    du = _from_groups(dug) + du_skip

    def a_grad(p, q):
        return jnp.concatenate([p[:, :S5_STATE] + p[:, S5_STATE:], q[:, S5_STATE:] - q[:, :S5_STATE]], axis=-1)

    dmats = dict(mt=dmt, p0=dp0, p1=dp1, q0=dq0, q1=dq1, a0=a_grad(da[:, 0], da[:, 1]), a1=a_grad(da[:, 2], da[:, 3]))
    dsp = mats_vjp(dmats)
    dh = _mm_simple(f"s5{l}_dX", du, w["w_in"], "nt", F32, extras=(dz,), epilogue=lambda accs, dz_t: [accs[0] + ALPHA * dz_t])
    dw_in = _mm_simple(f"s5{l}_dWin", h, du, "tn", BF16, tm=1024, tn=1024, tk=512)
    grads = dict(w_in=dw_in, w_glu=dw_glu, w_out=dw_out, d=_psum_rows(pdd), b_glu=_psum_rows(pbg), ln_g=dg, ln_b=db,
                 lam_re=dsp[0], lam_im=dsp[1], log_step=dsp[2], b_re=dsp[3], b_im=dsp[4], c_re=dsp[5], c_im=dsp[6])
    return dh, grads


def _gla_levels(lc):
    ms, m = [], lc // 2
    while m >= 1:
        ms.append(m)
        m //= 2
    return ms


def _gla_scan_matrix(lc, rev):
    r = np.arange(lc)[:, None]
    t = np.arange(lc)[None, :]
    blocks = []
    for m in _gla_levels(lc):
        same = (r // m) == (t // m)
        upper = ((r // m) % 2) == 1
        blocks.append(same & np.where(upper, t >= r, t < r))
    blocks.append(t >= r)
    blocks.append(t < r)
    if rev:
        blocks = [blk[::-1, ::-1] for blk in blocks]
    return np.concatenate(blocks, axis=1).astype(np.float32)


def _gla_gates(z, lb):
    sig = _sigmoid(z)
    ls = jnp.minimum(z, 0.0) - jnp.log(1.0 + jnp.exp(-jnp.abs(z)))
    a = jnp.log(lb)
    bb = jnp.log(1.0 - lb) + ls
    lf = jnp.maximum(a, bb) + jnp.log(1.0 + jnp.exp(-jnp.abs(a - bb)))
    return lf, (1.0 - lb) * (1.0 - sig), sig


def _gla_cumsum(lf, rev):
    b, sh = lf, 1
    while sh < lf.shape[0]:
        b = b + _shift_rows(b, sh, not rev)
        sh *= 2
    return b


def _gla_bref(b, m, rev):
    lc, n = b.shape
    idx = m if rev else m - 1
    if 2 * m >= 8:
        nb = lc // (2 * m)
        b3 = b.reshape(nb, 2 * m, n)
        return jnp.broadcast_to(b3[:, idx:idx + 1, :], (nb, 2 * m, n)).reshape(lc, n)
    row = lax.broadcasted_iota(jnp.int32, b.shape, 0)
    j = row & (2 * m - 1)
    out = b
    for jj in range(2 * m):
        if jj != idx:
            out = jnp.where(j == jj, pltpu.roll(b, (jj - idx) % lc, 0), out)
    return out


def _gla_chunk(q, k, lf, rev):
    lc = q.shape[0]
    nt = (_DN["nt"], ((), ()))
    b = _gla_cumsum(lf, rev)
    row = lax.broadcasted_iota(jnp.int32, (lc, HG_DIM), 0)
    ri = lax.broadcasted_iota(jnp.int32, (lc, lc), 0)
    ci = lax.broadcasted_iota(jnp.int32, (lc, lc), 1)
    qb, kb = q.astype(BF16), k.astype(BF16)
    sc = jnp.where(ri == ci, lax.dot_general(qb, kb, nt, preferred_element_type=F32), 0.0)
    levels = []
    for m in _gla_levels(lc):
        lg = int(math.log2(m))
        isq = ((row >> lg) & 1) == (0 if rev else 1)
        bref = _gla_bref(b, m, rev)
        w = jnp.exp(jnp.where(isq, b - bref, bref - b))
        xf = jnp.where(isq, q * w, 0.0)
        yf = jnp.where(isq, 0.0, k * w)
        bmask = (ri >> (lg + 1)) == (ci >> (lg + 1))
        xb, yb = xf.astype(BF16), yf.astype(BF16)
        sc = sc + jnp.where(bmask, lax.dot_general(xb, yb, nt, preferred_element_type=F32), 0.0)
        levels.append((isq, w, xf, yf, xb, yb, bmask))
    return b, sc, levels, (ri == ci)


def _hg_specs(T, lc, rev, backward):
    nc = T // lc
    cc = (lambda c: nc - 1 - c) if rev != backward else (lambda c: c)
    zcol = HG_HEADS * (2 if rev else 1)
    q_spec = pl.BlockSpec((lc, HG_DIM), lambda h, c: (cc(c), h))
    z_spec = pl.BlockSpec((lc, HG_DIM), lambda h, c: (cc(c), zcol + h))
    v_spec = pl.BlockSpec((lc, HG_DIM), lambda h, c: (cc(c), 3 * HG_HEADS + h))
    lb_spec = pl.BlockSpec((1, HG_DIM), lambda h, c: (0, h))
    st_spec = pl.BlockSpec((None, None, HG_DIM, HG_DIM), lambda h, c: (h, cc(c), 0, 0))
    return nc, q_spec, z_spec, v_spec, lb_spec, st_spec


def _gla_fwd(name, proj, lb_row, rev):
    T = proj.shape[0]
    lc = _tile(T, HG_CHUNK, 8)
    nc, q_spec, z_spec, v_spec, lb_spec, st_spec = _hg_specs(T, lc, rev, False)
    last = 0 if rev else lc - 1

    def body(q_ref, z_ref, v_ref, lb_ref, o_ref, st_ref, st_s):
        @pl.when(pl.program_id(1) == 0)
        def _():
            st_s[...] = jnp.zeros(st_s.shape, F32)

        q = _silu(q_ref[...])
        lf, k, _ = _gla_gates(z_ref[...], lb_ref[...])
        vb = v_ref[...].astype(BF16)
        b, sc, _, _ = _gla_chunk(q, k, lf, rev)
        st0 = st_s[...]
        st_ref[...] = st0
        bl = b[last:last + 1, :]
        o = jnp.dot(sc.astype(BF16), vb, preferred_element_type=F32)
        o += lax.dot_general((q * jnp.exp(b)).astype(BF16), st0.astype(BF16), (_DN["nt"], ((), ())), preferred_element_type=F32)
        o_ref[...] = o
        kd = (k * jnp.exp(bl - b)).astype(BF16)
        st_s[...] = st0 * jnp.exp(bl) + lax.dot_general(vb, kd, (_DN["tn"], ((), ())), preferred_element_type=F32)

    return pl.pallas_call(
        body, name=name, grid=(HG_HEADS, nc),
        in_specs=[q_spec, z_spec, v_spec, lb_spec], out_specs=[q_spec, st_spec],
        out_shape=[jax.ShapeDtypeStruct((T, D_MODEL), F32), jax.ShapeDtypeStruct((HG_HEADS, nc, HG_DIM, HG_DIM), F32)],
        scratch_shapes=[pltpu.VMEM((HG_DIM, HG_DIM), F32)], compiler_params=_cparams(),
    )(proj, proj, proj, lb_row)


def _gla_bwd(name, proj, lb_row, do, st, rev):
    T = proj.shape[0]
    lc = _tile(T, HG_CHUNK, 8)
    nc, q_spec, z_spec, v_spec, lb_spec, st_spec = _hg_specs(T, lc, rev, True)
    wall = jnp.asarray(_gla_scan_matrix(lc, rev), BF16)
    last = 0 if rev else lc - 1

    def body(q_ref, z_ref, v_ref, lb_ref, do_ref, st_ref, wall_ref, dq_ref, dz_ref, dv_ref, dlb_ref, dst_s):
        first = pl.program_id(1) == 0

        @pl.when(first)
        def _():
            dst_s[...] = jnp.zeros(dst_s.shape, F32)
            dlb_ref[...] = jnp.zeros(dlb_ref.shape, F32)

        nn, nt, tn = (_DN["nn"], ((), ())), (_DN["nt"], ((), ())), (_DN["tn"], ((), ()))
        dot = functools.partial(lax.dot_general, preferred_element_type=F32)
        qr, z, lb = q_ref[...], z_ref[...], lb_ref[...]
        q = _silu(qr)
        lf, k, sig = _gla_gates(z, lb)
        vb = v_ref[...].astype(BF16)
        b, sc, levels, eye = _gla_chunk(q, k, lf, rev)
        st0, dst = st_ref[...], dst_s[...]
        st0b, dstb = st0.astype(BF16), dst.astype(BF16)
        dob = do_ref[...].astype(BF16)
        bl = b[last:last + 1, :]
        eb, ebl, ekd = jnp.exp(b), jnp.exp(bl), jnp.exp(bl - b)
        qe, kd = q * eb, k * ekd
        kdb = kd.astype(BF16)
        dsc = dot(dob, vb, nt)
        dv_ref[...] = dot(sc.astype(BF16), dob, tn) + dot(kdb, dstb, nt)
        dqe = dot(dob, st0b, nn)
        dkd = dot(vb, dstb, nn)
        dq = dqe * eb
        dk = dkd * ekd
        zs = []
        dsd = jnp.where(eye, dsc, 0.0).astype(BF16)
        dq += dot(dsd, k.astype(BF16), nn)
        dk += dot(dsd, q.astype(BF16), tn)
        for (upper, w, xf, yf, xb, yb, bmask) in levels:
            dsl = jnp.where(bmask, dsc, 0.0).astype(BF16)
            dx = dot(dsl, yb, nn)
            dy = dot(dsl, xb, tn)
            dq += jnp.where(upper, dx * w, 0.0)
            dk += jnp.where(upper, 0.0, dy * w)
            zs.append(jnp.where(upper, dx * xf, dy * yf).astype(BF16))
        zs.append((dqe * qe).astype(BF16))
        zs.append((dkd * kd).astype(BF16))
        zl = jnp.sum(dst * st0, axis=0, keepdims=True) * ebl
        dlf = dot(wall_ref[...], jnp.concatenate(zs, axis=0), nn) + zl
        dst_s[...] = dst * ebl + dot(dob, qe.astype(BF16), tn)
        inv_f = jnp.exp(-lf)
        one_sig = 1.0 - sig
        dz_ref[...] = (dlf * inv_f - dk) * (1.0 - lb) * sig * one_sig
        dq_ref[...] = dq * _dsilu(qr)
        dlb_ref[...] += _bcast8(jnp.sum((dlf * inv_f - dk) * one_sig, axis=0, keepdims=True))

    big = jax.ShapeDtypeStruct((T, D_MODEL), F32)
    return pl.pallas_call(
        body, name=name, grid=(HG_HEADS, nc),
        in_specs=[q_spec, z_spec, v_spec, lb_spec, q_spec, st_spec, pl.BlockSpec(wall.shape, lambda h, c: (0, 0))],
        out_specs=[q_spec, q_spec, q_spec, pl.BlockSpec((None, 8, HG_DIM), lambda h, c: (h, 0, 0))],
        out_shape=[big, big, big, jax.ShapeDtypeStruct((HG_HEADS, 8, HG_DIM), F32)],
        scratch_shapes=[pltpu.VMEM((HG_DIM, HG_DIM), F32)], compiler_params=_cparams(),
    )(proj, proj, proj, lb_row, do, st, wall)


def _hg_lower_bounds(hg_lower_bound, layer):
    lbs = jax.nn.softmax(hg_lower_bound, axis=0)
    lbs = jnp.cumsum(lbs, axis=0) - lbs[0]
    return lbs[layer].reshape(1, -1)


def _hg_post(o_fw, o_bw, g_raw, ng):
    outs, ons, os_, rs = [], [], [], []
    for hd in range(o_fw.shape[1] // HG_DIM):
        sl = slice(hd * HG_DIM, (hd + 1) * HG_DIM)
        o = o_fw[:, sl] + o_bw[:, sl]
        r = lax.rsqrt(jnp.mean(o * o, axis=-1, keepdims=True) + RMS_EPS)
        on = o * r * ng
        outs.append(on * _silu(g_raw[:, sl]))
        ons.append(on)
        os_.append(o)
        rs.append(r)
    return outs, ons, os_, rs


def _hg_fwd(l, h, w, small, g, b):
    T = h.shape[0]
    tm = _tile(T, 512, 8)
    proj = _mm_simple(f"hg{l}_in", h, w["w_in"], "nn", F32, tn=1280, j_outer=True)
    lb_fn = lambda p: _hg_lower_bounds(p, l)
    lb_row, lb_vjp = jax.vjp(lb_fn, small["lower_bound"])
    o_fw, st_fw = _gla_fwd(f"hg{l}_gla_fw", proj, lb_row, False)
    o_bw, st_bw = _gla_fwd(f"hg{l}_gla_bw", proj, lb_row, True)
    ng = small["norm_g"].reshape(1, HG_DIM)

    def post(of_t, ob_t, g_t, ng_t):
        return [jnp.concatenate(_hg_post(of_t, ob_t, g_t, ng_t)[0], axis=1)]

    og = _rows(f"hg{l}_post", T, tm,
               [_rt(o_fw, tm), _rt(o_bw, tm), (proj, (tm, D_MODEL), lambda i: (i, 4)), _full(ng)],
               [_rt_out(T, D_MODEL, BF16, tm)], post)[0]
    z, h_out = _out_proj_ln(f"hg{l}_out", og, w["w_out"], h, g, b)
    return h_out, (h, proj, lb_row, lb_vjp, st_fw, st_bw, o_fw, o_bw, ng, og, z)


def _hg_bwd(l, dh_out, saved, w, small, g):
    h, proj, lb_row, lb_vjp, st_fw, st_bw, o_fw, o_bw, ng, og, z = saved
    T = h.shape[0]
    dz, dg, db = _ln_bwd(f"hg{l}_lnb", dh_out, z, g)
    tm = _tile(T, 512, 8)
    nn_tiles = D_MODEL // HG_DIM

    def epi(accs, of_t, ob_t, g_t, ng_t):
        dog = accs[0]
        _, ons, os_, rs = _hg_post(of_t, ob_t, g_t, ng_t)
        dos, dgs, dngs = [], [], []
        for hd in range(nn_tiles):
            sl = slice(hd * HG_DIM, (hd + 1) * HG_DIM)
            d, o, r = dog[:, sl], os_[hd], rs[hd]
            dgs.append(d * ons[hd] * _dsilu(g_t[:, sl]))
            don = d * _silu(g_t[:, sl])
            dngs.append(jnp.sum(don * o * r, axis=0, keepdims=True))
            dxn = don * ng_t
            dos.append(r * dxn - o * (r * r * r) * jnp.mean(dxn * o, axis=-1, keepdims=True))
        return [jnp.concatenate(dos, axis=1), jnp.concatenate(dgs, axis=1), _bcast8(jnp.concatenate(dngs, axis=1))]

    grid = (T // tm, 1, 1)
    row_map = lambda i, j, k: (i, 0)
    do, dg_raw, png = _mm(
        f"hg{l}_dog", grid, [(dz, (tm, D_MODEL), row_map)], [(w["w_out"], (D_MODEL, D_MODEL), lambda i, j, k: (0, 0))],
        [(0, 0, 0, "nt")], 1, (tm, D_MODEL),
        [(o_fw, (tm, D_MODEL), row_map), (o_bw, (tm, D_MODEL), row_map), (proj, (tm, D_MODEL), lambda i, j, k: (i, 4)),
         (ng, (1, HG_DIM), lambda i, j, k: (0, 0))],
        [((T, D_MODEL), F32, (tm, D_MODEL), row_map), ((T, D_MODEL), F32, (tm, D_MODEL), row_map),
         ((T // tm * 8, D_MODEL), F32, (8, D_MODEL), row_map)], epi)
    dw_out = _mm_simple(f"hg{l}_dWout", og, dz, "tn", BF16, tm=1024, tn=1024, tk=512)
    dq_f, dz_f, dv_f, dlb_f = _gla_bwd(f"hg{l}_glab_fw", proj, lb_row, do, st_fw, False)
    dq_b, dz_b, dv_b, dlb_b = _gla_bwd(f"hg{l}_glab_bw", proj, lb_row, do, st_bw, True)
    dproj = jnp.concatenate([dq_f + dq_b, dz_f, dz_b, dv_f + dv_b, dg_raw], axis=1).astype(BF16)
    dh = _mm_simple(f"hg{l}_dX", dproj, w["w_in"], "nt", F32, tm=256, tk=5 * D_MODEL, extras=(dz,),
                    epilogue=lambda accs, dz_t: [accs[0] + ALPHA * dz_t])
    dw_in = _mm_simple(f"hg{l}_dWin", h, dproj, "tn", BF16, tm=1024, tn=1280, tk=512)
    dlb_row = (dlb_f[:, 0, :] + dlb_b[:, 0, :]).reshape(1, D_MODEL)
    grads = dict(w_in=dw_in, w_out=dw_out, lower_bound=lb_vjp(dlb_row)[0],
                 norm_g=_psum_rows(png).reshape(nn_tiles, HG_DIM).sum(axis=0), ln_g=dg, ln_b=db)
    return dh, grads


MLA_W = 256
MLA_SCALE = (MLA_NOPE + MLA_ROPE) ** -0.5


def _swap_halves(a):
    n = a.shape[-1] // 2
    return jnp.concatenate([a[..., n:], a[..., :n]], axis=-1)


def _mla_ext_weights(w_in, w_q_b):
    w_in_ext = jnp.concatenate([w_in, _swap_halves(w_in[:, MLA_Q_LORA + MLA_KV_LORA:])], axis=1)
    wq = w_q_b.reshape(MLA_Q_LORA, MLA_HEADS, MLA_NOPE + MLA_ROPE)
    wq_ext = jnp.concatenate([wq, _swap_halves(wq[:, :, MLA_NOPE:])], axis=2).reshape(MLA_Q_LORA, MLA_HEADS * MLA_W)
    return w_in_ext, wq_ext


def _mla_ext_grads(dw_in_ext, dwq_ext):
    n_lat = MLA_Q_LORA + MLA_KV_LORA
    dw_in = jnp.concatenate([dw_in_ext[:, :n_lat], dw_in_ext[:, n_lat:n_lat + MLA_ROPE]
                             + _swap_halves(dw_in_ext[:, n_lat + MLA_ROPE:])], axis=1)
    dq = dwq_ext.reshape(MLA_Q_LORA, MLA_HEADS, MLA_W)
    dwq = jnp.concatenate([dq[:, :, :MLA_NOPE], dq[:, :, MLA_NOPE:MLA_NOPE + MLA_ROPE]
                           + _swap_halves(dq[:, :, MLA_NOPE + MLA_ROPE:])], axis=2)
    return dw_in, dwq.reshape(MLA_Q_LORA, MLA_HEADS * (MLA_NOPE + MLA_ROPE))


def _rope_table(positions):
    half = MLA_ROPE // 2
    inv_freq = 1.0 / (ROPE_THETA ** (jnp.arange(half, dtype=F32) * (2.0 / MLA_ROPE)))
    ang = positions.astype(F32)[:, None] * inv_freq
    cos, sin = jnp.cos(ang), jnp.sin(ang)
    return jnp.concatenate([cos, cos, -sin, sin], axis=1)


def _rope_sum(prod):
    return prod + pltpu.roll(prod, MLA_ROPE, 1)


def _low_half(a):
    lane = lax.broadcasted_iota(jnp.int32, a.shape, 1)
    return jnp.where(lane < MLA_ROPE, a, 0.0)


def _rms(x, g):
    r = lax.rsqrt(jnp.mean(x * x, axis=-1, keepdims=True) + RMS_EPS)
    return x * r * g


def _rms_bwd(x, g, dy):
    r = lax.rsqrt(jnp.mean(x * x, axis=-1, keepdims=True) + RMS_EPS)
    dxn = dy * g
    dx = r * dxn - x * (r * r * r) * jnp.mean(dxn * x, axis=-1, keepdims=True)
    return dx, jnp.sum(dy * x * r, axis=0, keepdims=True)


ATT_TILE = 512
ATT_BLOCK = 2048


def _lanes(col, n):
    return jnp.tile(col, (1, n // 128))


def _flash_fwd(name, q, k, v):
    H, T, _ = q.shape
    tq, tkb = _tile(T, ATT_TILE, 8), _tile(T, ATT_BLOCK, 128)
    ts = _tile(tkb, ATT_TILE, 128)
    nk, nsub = T // tkb, tkb // ts

    def body(q_ref, k_ref, v_ref, o_ref, lse_ref, m_s, acc_s):
        ki = pl.program_id(2)

        @pl.when(ki == 0)
        def _():
            m_s[...] = jnp.full(m_s.shape, -jnp.inf, F32)
            acc_s[...] = jnp.zeros(acc_s.shape, F32)

        qv = q_ref[...]
        m, acc = m_s[...], acc_s[...]
        for j in range(nsub):
            kj, vj = k_ref[j * ts:(j + 1) * ts, :], v_ref[j * ts:(j + 1) * ts, :]
            s = lax.dot_general(qv, kj, (_DN["nt"], ((), ())), preferred_element_type=F32)
            m_new = jnp.maximum(m, jnp.max(s, axis=-1, keepdims=True))
            p = jnp.exp(s - _lanes(m_new, ts)).astype(BF16)
            acc = _lanes(jnp.exp(m - m_new), 2 * MLA_V) * acc + jnp.dot(p, vj, preferred_element_type=F32)
            m = m_new
        m_s[...], acc_s[...] = m, acc

        @pl.when(ki == nk - 1)
        def _():
            l = acc[:, MLA_V:]
            o_ref[...] = acc[:, :MLA_V] / l
            lse_ref[...] = m + jnp.log(l)

    return pl.pallas_call(
        body, name=name, grid=(H, T // tq, nk),
        in_specs=[pl.BlockSpec((None, tq, MLA_W), lambda h, i, j: (h, i, 0)),
                  pl.BlockSpec((None, tkb, MLA_W), lambda h, i, j: (h, j, 0)),
                  pl.BlockSpec((None, tkb, 2 * MLA_V), lambda h, i, j: (h, j, 0))],
        out_specs=[pl.BlockSpec((tq, MLA_V), lambda h, i, j: (i, h)), pl.BlockSpec((tq, MLA_V), lambda h, i, j: (i, h))],
        out_shape=[jax.ShapeDtypeStruct((T, H * MLA_V), F32), jax.ShapeDtypeStruct((T, H * MLA_V), F32)],
        scratch_shapes=[pltpu.VMEM((tq, MLA_V), F32), pltpu.VMEM((tq, 2 * MLA_V), F32)],
        compiler_params=_cparams(),
    )(q, k, v)


def _flash_bwd(name, q, k, v, do, lse, delta):
    H, T, _ = q.shape
    nt, tn = (_DN["nt"], ((), ())), (_DN["tn"], ((), ()))
    tk, tqb = _tile(T, ATT_TILE, 128), _tile(T, ATT_BLOCK, 128)
    tqs = _tile(tqb, ATT_TILE, 128)
    nk, nqb, nsub = T // tk, T // tqb, tqb // tqs

    def body(q_ref, k_ref, v_ref, do_ref, lse_ref, dl_ref, dq_ref, dk_ref, dv_ref, dk_s, dv_s):
        ki, qb = pl.program_id(1), pl.program_id(2)

        @pl.when(jnp.logical_and(ki == 0, qb == 0))
        def _():
            dq_ref[...] = jnp.zeros(dq_ref.shape, F32)

        @pl.when(qb == 0)
        def _():
            dk_s[...] = jnp.zeros(dk_s.shape, F32)
            dv_s[...] = jnp.zeros(dv_s.shape, F32)

        kv, vv = k_ref[...], v_ref[...]
        dk, dv = dk_s[...], dv_s[...]
        for j in range(nsub):
            sl = slice(j * tqs, (j + 1) * tqs)
            qj, doj = q_ref[sl, :], do_ref[sl, :]
            s = lax.dot_general(qj, kv, nt, preferred_element_type=F32)
            dp = lax.dot_general(doj, vv, nt, preferred_element_type=F32)
            p = jnp.exp(s - _lanes(lse_ref[sl, :], tk))
            ds = (p * (dp - _lanes(dl_ref[sl, :], tk))).astype(BF16)
            dv = dv + lax.dot_general(p.astype(BF16), doj, tn, preferred_element_type=F32)
            dk = dk + lax.dot_general(ds, qj, tn, preferred_element_type=F32)
            rows = pl.ds(pl.multiple_of(qb * tqb + j * tqs, tqs), tqs)
            dq_ref[rows, :] += jnp.dot(ds, kv, preferred_element_type=F32)
        dk_s[...], dv_s[...] = dk, dv

        @pl.when(qb == nqb - 1)
        def _():
            dk_ref[...] = dk
            dv_ref[...] = dv

    return pl.pallas_call(
        body, name=name, grid=(H, nk, nqb),
        in_specs=[pl.BlockSpec((None, tqb, MLA_W), lambda h, i, j: (h, j, 0)),
                  pl.BlockSpec((None, tk, MLA_W), lambda h, i, j: (h, i, 0)),
                  pl.BlockSpec((None, tk, MLA_V), lambda h, i, j: (h, i, 0)),
                  pl.BlockSpec((tqb, MLA_V), lambda h, i, j: (j, h)),
                  pl.BlockSpec((tqb, MLA_V), lambda h, i, j: (j, h)),
                  pl.BlockSpec((tqb, MLA_V), lambda h, i, j: (j, h))],
        out_specs=[pl.BlockSpec((None, T, MLA_W), lambda h, i, j: (h, 0, 0)),
                   pl.BlockSpec((None, tk, MLA_W), lambda h, i, j: (h, i, 0)),
                   pl.BlockSpec((None, tk, MLA_V), lambda h, i, j: (h, i, 0))],
        out_shape=[jax.ShapeDtypeStruct((H, T, MLA_W), F32), jax.ShapeDtypeStruct((H, T, MLA_W), F32),
                   jax.ShapeDtypeStruct((H, T, MLA_V), F32)],
        scratch_shapes=[pltpu.VMEM((tk, MLA_W), F32), pltpu.VMEM((tk, MLA_V), F32)], compiler_params=_cparams(),
    )(q, k, v, do, lse, delta)


def _mla_fwd(l, h, w, small, g, b, cs):
    T = h.shape[0]
    tm = _tile(T, 512, 8)
    H = MLA_HEADS
    gq, gkv = small["q_norm_g"].reshape(1, -1), small["kv_norm_g"].reshape(1, -1)
    n_ext = MLA_Q_LORA + MLA_KV_LORA + 2 * MLA_ROPE
    row = lambda i, j, k: (i, 0)
    fix = lambda i, j, k: (0, 0)

    def epi_lat(accs, gq_t, gkv_t):
        a = accs[0]
        ql, kvl = a[:, :MLA_Q_LORA], a[:, MLA_Q_LORA:MLA_Q_LORA + MLA_KV_LORA]
        return [ql, kvl, a[:, MLA_Q_LORA + MLA_KV_LORA:], _rms(ql, gq_t), _rms(kvl, gkv_t)]

    ql, kvl, kr, xq, xkv = _mm(
        f"mla{l}_in", (T // tm, 1, 1), [(h, (tm, D_MODEL), row)], [(w["w_in_ext"], (D_MODEL, n_ext), fix)],
        [(0, 0, 0, "nn")], 1, (tm, n_ext), [(gq, gq.shape, fix), (gkv, gkv.shape, fix)],
        [((T, MLA_Q_LORA), F32, (tm, MLA_Q_LORA), row), ((T, MLA_KV_LORA), F32, (tm, MLA_KV_LORA), row),
         ((T, 2 * MLA_ROPE), F32, (tm, 2 * MLA_ROPE), row), ((T, MLA_Q_LORA), BF16, (tm, MLA_Q_LORA), row),
         ((T, MLA_KV_LORA), BF16, (tm, MLA_KV_LORA), row)], epi_lat)

    def epi_q(accs, cs_t):
        a = accs[0]
        return [jnp.concatenate([a[:, :MLA_NOPE], _rope_sum(a[:, MLA_NOPE:] * cs_t)], axis=1) * MLA_SCALE]

    head_out = lambda i, j, k: (j, i, 0)
    q = _mm(f"mla{l}_q", (T // tm, H, 1), [(xq, (tm, MLA_Q_LORA), row)],
            [(w["wq_ext"], (MLA_Q_LORA, MLA_W), lambda i, j, k: (0, j))], [(0, 0, 0, "nn")], 1, (tm, MLA_W),
            [(cs, (tm, 2 * MLA_ROPE), row)], [((H, T, MLA_W), BF16, (None, tm, MLA_W), head_out)], epi_q)[0]

    def epi_kv(accs, kr_t, cs_t):
        a = accs[0]
        v_t = a[:, MLA_NOPE:]
        return [jnp.concatenate([a[:, :MLA_NOPE], _low_half(_rope_sum(kr_t * cs_t))], axis=1),
                jnp.concatenate([v_t, jnp.ones_like(v_t)], axis=1)]

    k, v = _mm(f"mla{l}_kv", (T // tm, H, 1), [(xkv, (tm, MLA_KV_LORA), row)],
               [(w["w_kv_b"], (MLA_KV_LORA, MLA_W), lambda i, j, k: (0, j))], [(0, 0, 0, "nn")], 1, (tm, MLA_W),
               [(kr, (tm, 2 * MLA_ROPE), row), (cs, (tm, 2 * MLA_ROPE), row)],
               [((H, T, MLA_W), BF16, (None, tm, MLA_W), head_out), ((H, T, 2 * MLA_V), BF16, (None, tm, 2 * MLA_V), head_out)],
               epi_kv)
    o, lse = _flash_fwd(f"mla{l}_attn", q, k, v)
    z, h_out = _out_proj_ln(f"mla{l}_out", o, w["w_out"], h, g, b)
    return h_out, (h, ql, kvl, xq, xkv, q, k, v, o, lse, z, gq, gkv, cs)


def _mla_bwd(l, dh_out, saved, w, g):
    h, ql, kvl, xq, xkv, q, k, v, o, lse, z, gq, gkv, cs = saved
    T = h.shape[0]
    tm = _tile(T, 512, 8)
    H = MLA_HEADS
    dz, dg, db = _ln_bwd(f"mla{l}_lnb", dh_out, z, g)
    def epi_do(accs, o_t):
        d = accs[0]
        deltas = [jnp.broadcast_to(jnp.sum(d[:, hd * MLA_V:(hd + 1) * MLA_V] * o_t[:, hd * MLA_V:(hd + 1) * MLA_V],
                                           axis=-1, keepdims=True), (d.shape[0], MLA_V)) for hd in range(H)]
        return [d, jnp.concatenate(deltas, axis=1)]

    do, delta = _mm_simple(f"mla{l}_dO", dz, w["w_out"], "nt", None, extras=(o,), epilogue=epi_do, n_out=2,
                           out_dtypes=[BF16, F32])
    dw_out = _mm_simple(f"mla{l}_dWout", o, dz, "tn", BF16, tm=1024, tn=1024, tk=512)
    dq, dk, dv = _flash_bwd(f"mla{l}_attnb", q, k, v, do, lse, delta)
    head_in = lambda i, hh: (hh, i, 0)
    row2 = lambda i, hh: (i, 0)

    def fn_q(dq_t, cs_t):
        d = dq_t[:, MLA_NOPE:]
        return [jnp.concatenate([dq_t[:, :MLA_NOPE], _rope_sum(d) * cs_t], axis=1) * MLA_SCALE]

    dq_eff = _grid_call(f"mla{l}_dqeff", (T // tm, H), [(dq, (None, tm, MLA_W), head_in), (cs, (tm, 2 * MLA_ROPE), row2)],
                        [((T, H * MLA_W), BF16, (tm, MLA_W), lambda i, hh: (i, hh))], fn_q)[0]

    def fn_kv(dk_t, dv_t, cs_t):
        return [jnp.concatenate([dk_t[:, :MLA_NOPE], dv_t], axis=1), _rope_sum(_low_half(dk_t[:, MLA_NOPE:])) * cs_t]

    dkv_eff, dkr = _grid_call(
        f"mla{l}_dkveff", (T // tm, H),
        [(dk, (None, tm, MLA_W), head_in), (dv, (None, tm, MLA_V), head_in), (cs, (tm, 2 * MLA_ROPE), row2)],
        [((T, H * MLA_W), BF16, (tm, MLA_W), lambda i, hh: (i, hh)), ((T, 2 * MLA_ROPE), F32, (tm, 2 * MLA_ROPE), row2)],
        fn_kv, acc_outs=(1,))
    dxq = _mm_simple(f"mla{l}_dxq", dq_eff, w["wq_ext"], "nt", F32, tn=MLA_Q_LORA)
    dwq_ext = _mm_simple(f"mla{l}_dWq", xq, dq_eff, "tn", BF16, tm=MLA_Q_LORA, tn=1024, tk=512)
    dxkv = _mm_simple(f"mla{l}_dxkv", dkv_eff, w["w_kv_b"], "nt", F32, tn=MLA_KV_LORA)
    dwkv = _mm_simple(f"mla{l}_dWkv", xkv, dkv_eff, "tn", BF16, tm=MLA_KV_LORA, tn=1024, tk=512)

    def fn_lat(ql_t, dxq_t, gq_t, kvl_t, dxkv_t, gkv_t, dkr_t):
        dql, dgq = _rms_bwd(ql_t, gq_t, dxq_t)
        dkvl, dgkv = _rms_bwd(kvl_t, gkv_t, dxkv_t)
        return [jnp.concatenate([dql, dkvl, dkr_t], axis=1), _bcast8(dgq), _bcast8(dgkv)]

    n_ext = MLA_Q_LORA + MLA_KV_LORA + 2 * MLA_ROPE
    dlat, pgq, pgkv = _rows(f"mla{l}_dlat", T, tm,
                            [_rt(ql, tm), _rt(dxq, tm), _full(gq), _rt(kvl, tm), _rt(dxkv, tm), _full(gkv), _rt(dkr, tm)],
                            [_rt_out(T, n_ext, BF16, tm), _ps_out(T, MLA_Q_LORA, tm), _ps_out(T, MLA_KV_LORA, tm)], fn_lat)
    dh = _mm_simple(f"mla{l}_dX", dlat, w["w_in_ext"], "nt", F32, tk=n_ext, extras=(dz,),
                    epilogue=lambda accs, dz_t: [accs[0] + ALPHA * dz_t])
    dw_in_ext = _mm_simple(f"mla{l}_dWin", h, dlat, "tn", BF16, tm=1024, tn=n_ext, tk=512)
    dw_in, dwq = _mla_ext_grads(dw_in_ext.astype(F32), dwq_ext.astype(F32))
    grads = dict(w_in=dw_in, w_q_b=dwq, w_kv_b=dwkv, w_out=dw_out, q_norm_g=_psum_rows(pgq), kv_norm_g=_psum_rows(pgkv),
                 ln_g=dg, ln_b=db)
    return dh, grads


def _loss_head(y, target):
    T = y.shape[0]
    tm = _tile(T, 512, 8)

    def fn(y_t, t_t):
        d = y_t - t_t
        part = 0.5 * jnp.sum(jnp.mean(d * d, axis=-1, keepdims=True), axis=0, keepdims=True)
        return [d * (1.0 / D_MODEL), jnp.broadcast_to(part, (8, 128))]

    dy, part = _rows("loss_head", T, tm, [_rt(y, tm), _rt(target, tm)],
                     [_rt_out(T, D_MODEL, F32, tm), ((T // tm * 8, 128), F32, (8, 128), lambda i: (i, 0))], fn)
    return jnp.sum(part.reshape(-1, 8, 128)[:, 0, 0]), dy


_S5_SMALL = ("lam_re", "lam_im", "log_step", "b_re", "b_im", "c_re", "c_im", "d", "b_glu")


def _local_step(x, positions, target, W):
    row = lambda a, i: a[i].reshape(1, -1)
    w_in_ext, wq_ext = _mla_ext_weights(W["mla_w_in"][0], W["mla_w_q_b"][0])
    cs = _rope_table(positions)
    h, saves = x, []
    for l in range(DEPTH):
        kind, slot = LAYER_MIXER[l], l // 3
        g, b = row(W["ln_mix_g"], l), row(W["ln_mix_b"], l)
        if kind == 0:
            w = {k: W["s5_" + k][slot] for k in ("w_in", "w_glu", "w_out")}
            small = {k: W["s5_" + k][slot] for k in _S5_SMALL}
            h, sv = _s5_fwd(l, h, w, small, g, b)
        elif kind == 1:
            w = dict(w_in=W["hg_w_in"][slot], w_out=W["hg_w_out"][slot])
            small = dict(lower_bound=W["hg_lower_bound"], norm_g=W["hg_norm_g"][slot])
            h, sv = _hg_fwd(l, h, w, small, g, b)
        else:
            w = dict(w_in_ext=w_in_ext, wq_ext=wq_ext, w_kv_b=W["mla_w_kv_b"][slot], w_out=W["mla_w_out"][slot])
            small = dict(q_norm_g=W["mla_q_norm_g"][slot], kv_norm_g=W["mla_kv_norm_g"][slot])
            h, sv = _mla_fwd(l, h, w, small, g, b, cs)
        h, fsv = _ffn_fwd(l, h, W["ffn_w_in"][l], W["ffn_w_out"][l], row(W["ln_ffn_g"], l), row(W["ln_ffn_b"], l))
        saves.append((w, small, sv, fsv))
    loss, dh = _loss_head(h, target)
    per_layer = [None] * DEPTH
    for l in reversed(range(DEPTH)):
        kind = LAYER_MIXER[l]
        w, small, sv, fsv = saves[l]
        dh, gf = _ffn_bwd(l, dh, fsv, W["ffn_w_in"][l], W["ffn_w_out"][l], row(W["ln_ffn_g"], l))
        g = row(W["ln_mix_g"], l)
        if kind == 0:
            dh, gm = _s5_bwd(l, dh, sv, w, small, g)
        elif kind == 1:
            dh, gm = _hg_bwd(l, dh, sv, w, small, g)
        else:
            dh, gm = _mla_bwd(l, dh, sv, w, g)
        per_layer[l] = (gm, gf)
    grads = {}
    stack = lambda xs: jnp.stack([a.astype(F32) if a.dtype != BF16 else a for a in xs])
    grads["ln_mix_g"] = stack([per_layer[l][0]["ln_g"] for l in range(DEPTH)])
    grads["ln_mix_b"] = stack([per_layer[l][0]["ln_b"] for l in range(DEPTH)])
    grads["ln_ffn_g"] = stack([per_layer[l][1]["ln_g"] for l in range(DEPTH)])
    grads["ln_ffn_b"] = stack([per_layer[l][1]["ln_b"] for l in range(DEPTH)])
    grads["ffn_w_in"] = stack([per_layer[l][1]["w_in"] for l in range(DEPTH)])
    grads["ffn_w_out"] = stack([per_layer[l][1]["w_out"] for l in range(DEPTH)])
    s5_layers = [l for l in range(DEPTH) if LAYER_MIXER[l] == 0]
    for k in ("w_in", "w_glu", "w_out") + _S5_SMALL:
        grads["s5_" + k] = stack([per_layer[l][0][k] for l in s5_layers])
    hg = per_layer[1][0]
    grads["hg_w_in"], grads["hg_w_out"] = hg["w_in"][None], hg["w_out"][None]
    grads["hg_lower_bound"], grads["hg_norm_g"] = hg["lower_bound"], hg["norm_g"][None]
    for k, v in per_layer[2][0].items():
        if not k.startswith("ln_"):
            grads["mla_" + k] = v[None]
    return loss, dh, grads


def _here():
    return lax.axis_index("x"), lax.axis_index("y"), lax.axis_index("c")


def _any_spec():
    return pl.BlockSpec(memory_space=pl.ANY)


def _chip_exchange(name, xs, scatter):
    n = len(xs)

    def body(*refs):
        ins, outs = refs[:n], refs[n:2 * n]
        send_sems, recv_sems, loc_sems = refs[2 * n:]
        x, y, c = _here()
        me = 2 * x + y
        peers = [(1 - x, y), (x, 1 - y), (1 - x, 1 - y)]
        copies = []
        for t in range(n):
            src_of = (lambda p, t=t: ins[t].at[p]) if scatter else (lambda p, t=t: ins[t])
            loc = pltpu.make_async_copy(src_of(me), outs[t].at[me], loc_sems.at[t])
            loc.start()
            copies.append(loc)
            for j, (px, py) in enumerate(peers):
                cp = pltpu.make_async_remote_copy(
                    src_ref=src_of(2 * px + py), dst_ref=outs[t].at[me], send_sem=send_sems.at[t, j],
                    recv_sem=recv_sems.at[t, j], device_id=(px, py, c), device_id_type=MESH)
                cp.start()
                copies.append(cp)
        for cp in copies:
            cp.wait()

    out_shape = [jax.ShapeDtypeStruct(a.shape if scatter else (N_CHIPS,) + a.shape, a.dtype) for a in xs]
    return pl.pallas_call(
        body, name=name, in_specs=[_any_spec()] * n, out_specs=[_any_spec()] * n, out_shape=out_shape,
        scratch_shapes=[pltpu.SemaphoreType.DMA((n, 3)), pltpu.SemaphoreType.DMA((n, 3)), pltpu.SemaphoreType.DMA((n,))],
    )(*xs)


def _core_exchange(name, a, n_chunks=8):
    rows = a.shape[0] // n_chunks
    assert rows * n_chunks == a.shape[0] and rows % 8 == 0, (a.shape, n_chunks)

    def body(a_ref, o_ref, send_sems, recv_sems, loc_sem):
        x, y, c = _here()
        loc = pltpu.make_async_copy(a_ref, o_ref.at[c], loc_sem)
        loc.start()
        copies = [loc]
        for i in range(n_chunks):
            part = pl.ds(i * rows, rows)
            cp = pltpu.make_async_remote_copy(
                src_ref=a_ref.at[part], dst_ref=o_ref.at[c, part], send_sem=send_sems.at[i], recv_sem=recv_sems.at[i],
                device_id=(x, y, 1 - c), device_id_type=MESH)
            cp.start()
            copies.append(cp)
        for cp in copies:
            cp.wait()

    return pl.pallas_call(
        body, name=name, in_specs=[_any_spec()], out_specs=_any_spec(),
        out_shape=jax.ShapeDtypeStruct((2,) + a.shape, a.dtype),
        scratch_shapes=[pltpu.SemaphoreType.DMA((n_chunks,)), pltpu.SemaphoreType.DMA((n_chunks,)), pltpu.SemaphoreType.DMA],
    )(a)


def _all_exchange(name, a):
    def body(a_ref, o_ref, send_sems, recv_sems, loc_sem):
        x, y, c = _here()
        me = 4 * x + 2 * y + c
        loc = pltpu.make_async_copy(a_ref, o_ref.at[me], loc_sem)
        loc.start()
        copies = [loc]
        for mask in range(1, N_DEV):
            fx, fy, fc = (mask >> 2) & 1, (mask >> 1) & 1, mask & 1
            peer = (1 - x if fx else x, 1 - y if fy else y, 1 - c if fc else c)
            cp = pltpu.make_async_remote_copy(src_ref=a_ref, dst_ref=o_ref.at[me], send_sem=send_sems.at[mask - 1],
                                              recv_sem=recv_sems.at[mask - 1], device_id=peer, device_id_type=MESH)
            cp.start()
            copies.append(cp)
        for cp in copies:
            cp.wait()

    return pl.pallas_call(
        body, name=name, in_specs=[_any_spec()], out_specs=_any_spec(),
        out_shape=jax.ShapeDtypeStruct((N_DEV,) + a.shape, a.dtype),
        scratch_shapes=[pltpu.SemaphoreType.DMA((N_DEV - 1,)), pltpu.SemaphoreType.DMA((N_DEV - 1,)), pltpu.SemaphoreType.DMA],
    )(a)


def _sum_leading(name, a, out_dtype=F32):
    n, R, C = a.shape
    tr = _tile(R, 512, 16)

    def fn(a_t):
        s = a_t[0].astype(F32)
        for i in range(1, n):
            s = s + a_t[i].astype(F32)
        return [s]

    return _grid_call(name, (R // tr,), [(a, (n, tr, C), lambda i: (0, i, 0))],
                      [((R, C), out_dtype, (tr, C), lambda i: (i, 0))], fn)[0]


def _adamw(name, g_parts, w, m, v):
    R, C = w.shape
    tr = _tile(R, 256, 8)
    ng = len(g_parts)
    c1 = 1.0 / (1.0 - ADAM_B1 ** ADAM_STEP)
    c2 = 1.0 / (1.0 - ADAM_B2 ** ADAM_STEP)

    def fn(*tiles):
        g = tiles[0].astype(F32)
        for t in tiles[1:ng]:
            g = g + t.astype(F32)
        w_t, m_t, v_t = tiles[ng:]
        m_n = ADAM_B1 * m_t + (1.0 - ADAM_B1) * g
        v_n = ADAM_B2 * v_t + (1.0 - ADAM_B2) * (g * g)
        delta = -ADAM_LR * ((m_n * c1) / (jnp.sqrt(v_n * c2) + ADAM_EPS) + ADAM_WD * w_t)
        return [g, delta, m_n, v_n]

    spec = lambda a: (a, (tr, C), lambda i: (i, 0))
    return _grid_call(name, (R // tr,), [spec(a) for a in list(g_parts) + [w, m, v]],
                      [((R, C), F32, (tr, C), lambda i: (i, 0))] * 4, fn)


_WEIGHTS = ("ln_mix_g", "ln_mix_b", "ln_ffn_g", "ln_ffn_b", "ffn_w_in", "ffn_w_out", "s5_w_in", "s5_lam_re", "s5_lam_im",
            "s5_log_step", "s5_b_re", "s5_b_im", "s5_c_re", "s5_c_im", "s5_d", "s5_w_glu", "s5_b_glu", "s5_w_out", "hg_w_in",
            "hg_lower_bound", "hg_norm_g", "hg_w_out", "mla_w_in", "mla_q_norm_g", "mla_w_q_b", "mla_kv_norm_g", "mla_w_kv_b",
            "mla_w_out")
_BIG = {"ffn_w_in": 2, "ffn_w_out": 1, "s5_w_in": 1, "s5_w_glu": 1, "s5_w_out": 1, "hg_w_in": 2, "hg_w_out": 1,
        "mla_w_in": 1, "mla_w_q_b": 2, "mla_w_kv_b": 2, "mla_w_out": 1}
_SMALL_SHARDED = {"s5_d": 1, "s5_b_glu": 1, "mla_q_norm_g": 1, "mla_kv_norm_g": 1}
_REPLICATED = tuple(n for n in _WEIGHTS if n not in _BIG and n not in _SMALL_SHARDED)
LANES = 1024


def _pack(arrs, dtype, row_mult, lead=0):
    rows, segs, r = [], [], 0
    for a in arrs:
        lead_shape = a.shape[:lead]
        flat = a.astype(dtype).reshape(lead_shape + (-1,))
        n = -(-flat.shape[-1] // LANES)
        flat = jnp.pad(flat, [(0, 0)] * lead + [(0, n * LANES - flat.shape[-1])])
        rows.append(flat.reshape(lead_shape + (n, LANES)))
        segs.append((r, n))
        r += n
    pad = -r % row_mult
    if pad:
        rows.append(jnp.zeros(rows[0].shape[:lead] + (pad, LANES), dtype))
    return jnp.concatenate(rows, axis=lead), segs


def _unpack(packed, segs, shapes):
    out = []
    for (r0, n), shp in zip(segs, shapes):
        size = int(np.prod(shp))
        out.append(packed[..., r0:r0 + n, :].reshape(packed.shape[:-2] + (n * LANES,))[..., :size].reshape(packed.shape[:-2] + tuple(shp)))
    return out


def _unshard(stacked, axis):
    moved = jnp.moveaxis(stacked, 0, axis)
    shp = list(moved.shape)
    return moved.reshape(shp[:axis] + [shp[axis] * shp[axis + 1]] + shp[axis + 2:])


def _shard_split(full, axis):
    shp = list(full.shape)
    a = full.reshape(shp[:axis] + [N_CHIPS, shp[axis] // N_CHIPS] + shp[axis + 1:])
    return jnp.moveaxis(a, axis, 0)


def _train_step(x, positions, target, w, m, v):
    big, small_sh = list(_BIG), list(_SMALL_SHARDED)
    chip = 2 * lax.axis_index("x") + lax.axis_index("y")

    big_pack, big_segs = _pack([w[n] for n in big], BF16, 16)
    sm_pack, sm_segs = _pack([w[n] for n in small_sh], F32, 8)
    big_all, sm_all = _chip_exchange("gather_weights", [big_pack, sm_pack], scatter=False)
    W = {n: w[n] for n in _REPLICATED}
    for n, s in zip(big, _unpack(big_all, big_segs, [w[n].shape for n in big])):
        W[n] = _unshard(s, _BIG[n])
    for n, s in zip(small_sh, _unpack(sm_all, sm_segs, [w[n].shape for n in small_sh])):
        W[n] = _unshard(s, _SMALL_SHARDED[n])

    loss_local, grad_x, G = _local_step(x, positions, target, W)
    loss = lax.psum(loss_local, ("x", "y", "c"))
    out = {}

    g_pack, _ = _pack([_shard_split(G[n].astype(BF16), _BIG[n]) for n in big], BF16, 16, lead=1)
    recv = _chip_exchange("scatter_grads", [g_pack], scatter=True)[0]
    pair = _core_exchange("swap_core_sums", _sum_leading("sum_chips", recv, BF16), n_chunks=1)
    for n, (r0, nr) in zip(big, big_segs):
        as_rows = lambda a: a.reshape(nr, LANES)
        res = _adamw("adamw_" + n, [pair[0, r0:r0 + nr], pair[1, r0:r0 + nr]], as_rows(w[n]), as_rows(m[n]), as_rows(v[n]))
        out[n] = tuple(r.reshape(w[n].shape) for r in res)

    small = list(_REPLICATED) + small_sh
    s_pack, s_segs = _pack([G[n] for n in small], F32, 16)
    total = _sum_leading("sum_small", _all_exchange("gather_small_grads", s_pack))
    g_small = dict(zip(small, _unpack(total, s_segs, [G[n].shape for n in small])))
    for n in small_sh:
        width = w[n].shape[1]
        g_small[n] = lax.dynamic_slice_in_dim(g_small[n], chip * width, width, axis=1)
    packs = [_pack([d[n] for n in small], F32, 8)[0] for d in (g_small, w, m, v)]
    _, a_segs = _pack([w[n] for n in small], F32, 8)
    res = _adamw("adamw_small", [packs[0]], packs[1], packs[2], packs[3])
    unpacked = [_unpack(r, a_segs, [w[n].shape for n in small]) for r in res]
    for i, n in enumerate(small):
        out[n] = tuple(u[i] for u in unpacked)
    return loss, grad_x, out


def kernel(x, positions, ln_mix_g, ln_mix_b, ln_ffn_g, ln_ffn_b, ffn_w_in, ffn_w_out, s5_w_in, s5_lam_re, s5_lam_im,
           s5_log_step, s5_b_re, s5_b_im, s5_c_re, s5_c_im, s5_d, s5_w_glu, s5_b_glu, s5_w_out, hg_w_in,
           hg_lower_bound, hg_norm_g, hg_w_out, mla_w_in, mla_q_norm_g, mla_w_q_b, mla_kv_norm_g, mla_w_kv_b,
           mla_w_out, loss_target, m_ln_mix_g, m_ln_mix_b, m_ln_ffn_g, m_ln_ffn_b, m_ffn_w_in, m_ffn_w_out, m_s5_w_in,
           m_s5_lam_re, m_s5_lam_im, m_s5_log_step, m_s5_b_re, m_s5_b_im, m_s5_c_re, m_s5_c_im, m_s5_d, m_s5_w_glu,
           m_s5_b_glu, m_s5_w_out, m_hg_w_in, m_hg_lower_bound, m_hg_norm_g, m_hg_w_out, m_mla_w_in, m_mla_q_norm_g,
           m_mla_w_q_b, m_mla_kv_norm_g, m_mla_w_kv_b, m_mla_w_out, v_ln_mix_g, v_ln_mix_b, v_ln_ffn_g, v_ln_ffn_b,
           v_ffn_w_in, v_ffn_w_out, v_s5_w_in, v_s5_lam_re, v_s5_lam_im, v_s5_log_step, v_s5_b_re, v_s5_b_im,
           v_s5_c_re, v_s5_c_im, v_s5_d, v_s5_w_glu, v_s5_b_glu, v_s5_w_out, v_hg_w_in, v_hg_lower_bound, v_hg_norm_g,
           v_hg_w_out, v_mla_w_in, v_mla_q_norm_g, v_mla_w_q_b, v_mla_kv_norm_g, v_mla_w_kv_b, v_mla_w_out):
    args = (ln_mix_g, ln_mix_b, ln_ffn_g, ln_ffn_b, ffn_w_in, ffn_w_out, s5_w_in, s5_lam_re, s5_lam_im,
            s5_log_step, s5_b_re, s5_b_im, s5_c_re, s5_c_im, s5_d, s5_w_glu, s5_b_glu, s5_w_out, hg_w_in,
            hg_lower_bound, hg_norm_g, hg_w_out, mla_w_in, mla_q_norm_g, mla_w_q_b, mla_kv_norm_g, mla_w_kv_b,
            mla_w_out, m_ln_mix_g, m_ln_mix_b, m_ln_ffn_g, m_ln_ffn_b, m_ffn_w_in, m_ffn_w_out,
            m_s5_w_in, m_s5_lam_re, m_s5_lam_im, m_s5_log_step, m_s5_b_re, m_s5_b_im, m_s5_c_re, m_s5_c_im, m_s5_d,
            m_s5_w_glu, m_s5_b_glu, m_s5_w_out, m_hg_w_in, m_hg_lower_bound, m_hg_norm_g, m_hg_w_out, m_mla_w_in,
            m_mla_q_norm_g, m_mla_w_q_b, m_mla_kv_norm_g, m_mla_w_kv_b, m_mla_w_out, v_ln_mix_g, v_ln_mix_b,
            v_ln_ffn_g, v_ln_ffn_b, v_ffn_w_in, v_ffn_w_out, v_s5_w_in, v_s5_lam_re, v_s5_lam_im, v_s5_log_step,
            v_s5_b_re, v_s5_b_im, v_s5_c_re, v_s5_c_im, v_s5_d, v_s5_w_glu, v_s5_b_glu, v_s5_w_out, v_hg_w_in,
            v_hg_lower_bound, v_hg_norm_g, v_hg_w_out, v_mla_w_in, v_mla_q_norm_g, v_mla_w_q_b, v_mla_kv_norm_g,
            v_mla_w_kv_b, v_mla_w_out)
    nw = len(_WEIGHTS)
    w = dict(zip(_WEIGHTS, args[:nw]))
    m = dict(zip(_WEIGHTS, args[nw:2 * nw]))
    v = dict(zip(_WEIGHTS, args[2 * nw:]))
    loss, grad_x, out = _train_step(x[0], positions[0], loss_target[0], w, m, v)
    res = [loss, grad_x[None]]
    for i in range(4):
        res += [out[n][i] for n in _WEIGHTS]
    return tuple(res)
```

```python
import functools
import math

import numpy as np
import jax
import jax.numpy as jnp
from jax import lax
from jax.experimental import pallas as pl
from jax.experimental.pallas import tpu as pltpu

F32 = jnp.float32
BF16 = jnp.bfloat16

D_MODEL = 1024
DEPTH = 4
LAYER_MIXER = (0, 1, 2, 0)
S5_GROUP = 16
S5_GROUPS = 64
S5_STATE = 64
S5_CHUNK = 16
HG_HEADS = 8
HG_DIM = 128
HG_CHUNK = 128
MLA_HEADS = 8
MLA_NOPE = 128
MLA_ROPE = 64
MLA_V = 128
MLA_Q_LORA = 384
MLA_KV_LORA = 256
ROPE_THETA = 10000.0
FFN_HIDDEN = 2816
ALPHA = (2 * DEPTH) ** 0.25
LN_EPS = 1e-5
RMS_EPS = 1e-6
ADAM_LR, ADAM_B1, ADAM_B2, ADAM_EPS, ADAM_WD, ADAM_STEP = 0.001, 0.9, 0.999, 1e-08, 0.01, 10
VMEM_LIMIT_BYTES = 56 * 1024 * 1024
MESH = pl.DeviceIdType.MESH
N_CHIPS = 4
N_DEV = 8


def _cparams():
    return pltpu.CompilerParams(vmem_limit_bytes=VMEM_LIMIT_BYTES)


def _tile(n, want, mult):
    t = min(want, n)
    t -= t % mult
    while t >= mult:
        if n % t == 0:
            return t
        t -= mult
    return n


def _sigmoid(x):
    return 1.0 / (1.0 + jnp.exp(-x))


def _silu(x):
    return x * _sigmoid(x)


def _dsilu(x):
    s = _sigmoid(x)
    return s * (1.0 + x * (1.0 - s))


_GELU_C = math.sqrt(2.0 / math.pi)


def _gelu(x):
    return 0.5 * x * (1.0 + jnp.tanh(_GELU_C * (x + 0.044715 * x * x * x)))


def _dgelu(x):
    t = jnp.tanh(_GELU_C * (x + 0.044715 * x * x * x))
    return 0.5 * (1.0 + t) + 0.5 * x * (1.0 - t * t) * _GELU_C * (1.0 + 3 * 0.044715 * x * x)


def _layer_norm(z, g, b):
    mu = jnp.mean(z, axis=-1, keepdims=True)
    zc = z - mu
    var = jnp.mean(zc * zc, axis=-1, keepdims=True)
    return zc * lax.rsqrt(var + LN_EPS) * g + b


def _bcast8(row):
    return jnp.broadcast_to(row, (8, row.shape[-1]))


def _stack_rows8(rows):
    n = rows[0].shape[-1]
    idx = lax.broadcasted_iota(jnp.int32, (8, n), 0)
    out = jnp.zeros((8, n), F32)
    for i, r in enumerate(rows):
        out = jnp.where(idx == i, _bcast8(r), out)
    return out


def _psum_rows(a):
    return a.reshape(-1, 8, a.shape[-1])[:, 0, :].sum(axis=0)


_DN = {"nn": ((1,), (0,)), "nt": ((1,), (1,)), "tn": ((0,), (0,))}


def _mm(name, grid, a_defs, b_defs, pairs, n_acc, acc_shape, extra_defs, out_defs, epilogue):
    nk = grid[2]
    na, nb, ne, no = len(a_defs), len(b_defs), len(extra_defs), len(out_defs)

    def body(*refs):
        a_refs = refs[:na]
        b_refs = refs[na:na + nb]
        e_refs = refs[na + nb:na + nb + ne]
        o_refs = refs[na + nb + ne:na + nb + ne + no]
        acc = refs[-1]
        k = pl.program_id(2)

        @pl.when(k == 0)
        def _():
            acc[...] = jnp.zeros(acc.shape, F32)

        for (ai, bi, ci, mode) in pairs:
            a = a_refs[ai][...].astype(BF16)
            b = b_refs[bi][...].astype(BF16)
            acc[ci] += lax.dot_general(a, b, (_DN[mode], ((), ())), preferred_element_type=F32)

        @pl.when(k == nk - 1)
        def _():
            outs = epilogue([acc[c] for c in range(n_acc)], *[e[...] for e in e_refs])
            for o_ref, o in zip(o_refs, outs):
                o_ref[...] = o.astype(o_ref.dtype)

    in_specs = [pl.BlockSpec(d[1], d[2]) for d in list(a_defs) + list(b_defs) + list(extra_defs)]
    out_specs = [pl.BlockSpec(d[2], d[3]) for d in out_defs]
    out_shape = [jax.ShapeDtypeStruct(d[0], d[1]) for d in out_defs]
    res = pl.pallas_call(
        body, name=name, grid=grid, in_specs=in_specs, out_specs=out_specs, out_shape=out_shape,
        scratch_shapes=[pltpu.VMEM((n_acc,) + tuple(acc_shape), F32)], compiler_params=_cparams(),
    )(*[d[0] for d in list(a_defs) + list(b_defs) + list(extra_defs)])
    return res


def _mm_simple(name, a, b, mode, out_dtype, tm=512, tn=1024, tk=1024, extras=(), epilogue=None, n_out=1,
               out_dtypes=None, psum_outs=0, j_outer=False):
    if mode == "nn":
        (M, K), (K2, N) = a.shape, b.shape
    elif mode == "nt":
        (M, K), (N, K2) = a.shape, b.shape
    else:
        (K, M), (K2, N) = a.shape, b.shape
    assert K == K2, (name, a.shape, b.shape, mode)
    tm, tn, tk = _tile(M, tm, 8), _tile(N, tn, 128), _tile(K, tk, 128)
    grid = (M // tm, N // tn, K // tk)
    if mode == "nn":
        a_def = (a, (tm, tk), lambda i, j, k: (i, k))
        b_def = (b, (tk, tn), lambda i, j, k: (k, j))
    elif mode == "nt":
        a_def = (a, (tm, tk), lambda i, j, k: (i, k))
        b_def = (b, (tn, tk), lambda i, j, k: (j, k))
    else:
        a_def = (a, (tk, tm), lambda i, j, k: (k, i))
        b_def = (b, (tk, tn), lambda i, j, k: (k, j))
    extra_defs = []
    for e in extras:
        if e.shape[0] == 1:
            extra_defs.append((e, (1, tn), lambda i, j, k: (0, j)))
        else:
            extra_defs.append((e, (tm, tn), lambda i, j, k: (i, j)))
    out_dtypes = out_dtypes or [out_dtype] * n_out
    out_defs = [((M, N), dt, (tm, tn), lambda i, j, k: (i, j)) for dt in out_dtypes]
    out_defs += [((M // tm * 8, N), F32, (8, tn), lambda i, j, k: (i, j)) for _ in range(psum_outs)]
    if epilogue is None:
        epilogue = lambda accs: [accs[0]]
    a_defs, b_defs = [a_def], [b_def]
    if j_outer:
        swap = lambda d: d[:-1] + ((lambda f: lambda j, i, k: f(i, j, k))(d[-1]),)
        grid = (grid[1], grid[0], grid[2])
        a_defs, b_defs = [swap(a_def)], [swap(b_def)]
        extra_defs, out_defs = [swap(d) for d in extra_defs], [swap(d) for d in out_defs]
    res = _mm(name, grid, a_defs, b_defs, [(0, 0, 0, mode)], 1, (tm, tn), extra_defs, out_defs, epilogue)
    return res[0] if len(res) == 1 else res


def _rows(name, T, tm, in_defs, out_defs, fn):
    ni = len(in_defs)

    def body(*refs):
        outs = fn(*[r[...] for r in refs[:ni]])
        for o_ref, o in zip(refs[ni:], outs):
            o_ref[...] = o.astype(o_ref.dtype)

    res = pl.pallas_call(
        body, name=name, grid=(T // tm,),
        in_specs=[pl.BlockSpec(d[1], d[2]) for d in in_defs],
        out_specs=[pl.BlockSpec(d[2], d[3]) for d in out_defs],
        out_shape=[jax.ShapeDtypeStruct(d[0], d[1]) for d in out_defs],
        compiler_params=_cparams(),
    )(*[d[0] for d in in_defs])
    return res


def _grid_call(name, grid, in_defs, out_defs, fn, acc_outs=()):
    ni = len(in_defs)

    def body(*refs):
        outs = fn(*[r[...] for r in refs[:ni]])
        first = pl.program_id(len(grid) - 1) == 0
        for idx, (o_ref, o) in enumerate(zip(refs[ni:], outs)):
            if idx in acc_outs:
                @pl.when(first)
                def _(o_ref=o_ref, o=o):
                    o_ref[...] = o.astype(o_ref.dtype)

                @pl.when(jnp.logical_not(first))
                def _(o_ref=o_ref, o=o):
                    o_ref[...] += o.astype(o_ref.dtype)
            else:
                o_ref[...] = o.astype(o_ref.dtype)

    return pl.pallas_call(
        body, name=name, grid=grid,
        in_specs=[pl.BlockSpec(d[1], d[2]) for d in in_defs],
        out_specs=[pl.BlockSpec(d[2], d[3]) for d in out_defs],
        out_shape=[jax.ShapeDtypeStruct(d[0], d[1]) for d in out_defs],
        compiler_params=_cparams(),
    )(*[d[0] for d in in_defs])


def _rt(a, tm):
    return (a, (tm, a.shape[1]), lambda i: (i, 0))


def _full(a):
    return (a, a.shape, lambda i: (0,) * a.ndim)


def _rt_out(T, n, dt, tm):
    return ((T, n), dt, (tm, n), lambda i: (i, 0))


def _ps_out(T, n, tm):
    return ((T // tm * 8, n), F32, (8, n), lambda i: (i, 0))


def _out_proj_ln(name, a, w, h_in, g, b):
    def epi(accs, h_t, g_t, b_t):
        z = ALPHA * h_t + accs[0]
        return [z, _layer_norm(z, g_t, b_t)]
    return _mm_simple(name, a, w, "nn", F32, tm=512, tn=D_MODEL, tk=FFN_HIDDEN, extras=(h_in, g, b), epilogue=epi, n_out=2)


def _ln_bwd(name, dh, z, g):
    T = dh.shape[0]
    tm = _tile(T, 512, 8)

    def fn(dh_t, z_t, g_t):
        mu = jnp.mean(z_t, axis=-1, keepdims=True)
        zc = z_t - mu
        var = jnp.mean(zc * zc, axis=-1, keepdims=True)
        rstd = lax.rsqrt(var + LN_EPS)
        xhat = zc * rstd
        dxh = dh_t * g_t
        m1 = jnp.mean(dxh, axis=-1, keepdims=True)
        m2 = jnp.mean(dxh * xhat, axis=-1, keepdims=True)
        dz = rstd * (dxh - m1 - xhat * m2)
        return [dz, _bcast8(jnp.sum(dh_t * xhat, axis=0, keepdims=True)), _bcast8(jnp.sum(dh_t, axis=0, keepdims=True))]

    dz, pg, pb = _rows(name, T, tm, [_rt(dh, tm), _rt(z, tm), _full(g)],
                       [_rt_out(T, D_MODEL, F32, tm), _ps_out(T, D_MODEL, tm), _ps_out(T, D_MODEL, tm)], fn)
    return dz, _psum_rows(pg), _psum_rows(pb)


def _ffn_fwd(l, h, w_in, w_out, g, b):
    T = h.shape[0]
    tm, tn = _tile(T, 512, 8), 1408
    nj = FFN_HIDDEN // tn
    grid = (nj, T // tm, 1)

    def epi(accs):
        gg, uu = accs
        return [gg, uu, _silu(gg) * uu]

    G, U, A = _mm(
        f"ffn{l}_in", grid, [(h, (tm, D_MODEL), lambda j, i, k: (i, 0))],
        [(w_in, (D_MODEL, tn), lambda j, i, k: (0, j)), (w_in, (D_MODEL, tn), lambda j, i, k: (0, j + nj))],
        [(0, 0, 0, "nn"), (0, 1, 1, "nn")], 2, (tm, tn), [],
        [((T, FFN_HIDDEN), F32, (tm, tn), lambda j, i, k: (i, j)),
         ((T, FFN_HIDDEN), F32, (tm, tn), lambda j, i, k: (i, j)),
         ((T, FFN_HIDDEN), BF16, (tm, tn), lambda j, i, k: (i, j))], epi)
    z, h_out = _out_proj_ln(f"ffn{l}_out", A, w_out, h, g, b)
    return h_out, (h, G, U, A, z)


def _ffn_bwd(l, dh_out, saved, w_in, w_out, g):
    h, G, U, A, z = saved
    T = h.shape[0]
    dz, dg, db = _ln_bwd(f"ffn{l}_lnb", dh_out, z, g)

    def epi(accs, g_t, u_t):
        da = accs[0]
        return [da * u_t * _dsilu(g_t), da * _silu(g_t)]

    dG, dU = _mm_simple(f"ffn{l}_dA", dz, w_out, "nt", BF16, tm=512, tn=1408, tk=1024, extras=(G, U), epilogue=epi, n_out=2,
                        j_outer=True)
    dw_out = _mm_simple(f"ffn{l}_dWout", A, dz, "tn", BF16, tm=1408, tn=1024, tk=512)
    dw_g = _mm_simple(f"ffn{l}_dWg", h, dG, "tn", BF16, tm=1024, tn=1408, tk=512)
    dw_u = _mm_simple(f"ffn{l}_dWu", h, dU, "tn", BF16, tm=1024, tn=1408, tk=512)
    tm, tk = _tile(T, 256, 8), FFN_HIDDEN
    dh = _mm(
        f"ffn{l}_dX", (T // tm, 1, 1),
        [(dG, (tm, tk), lambda i, j, k: (i, 0)), (dU, (tm, tk), lambda i, j, k: (i, 0))],
        [(w_in, (D_MODEL, tk), lambda i, j, k: (0, 0)), (w_in, (D_MODEL, tk), lambda i, j, k: (0, 1))],
        [(0, 0, 0, "nt"), (1, 1, 0, "nt")], 1, (tm, D_MODEL),
        [(dz, (tm, D_MODEL), lambda i, j, k: (i, 0))],
        [((T, D_MODEL), F32, (tm, D_MODEL), lambda i, j, k: (i, 0))],
        lambda accs, dz_t: [accs[0] + ALPHA * dz_t])[0]
    return dh, dict(w_in=jnp.concatenate([dw_g, dw_u], axis=1), w_out=dw_out, ln_g=dg, ln_b=db)


def _s5_matrices(lam_re, lam_im, log_step, b_re, b_im, c_re, c_im):
    L, hp = S5_CHUNK, lax.Precision.HIGHEST
    out = {}
    mt_total = 0.0
    for d in range(2):
        lam = lax.complex(lam_re[d], lam_im[d])
        step = jnp.exp(log_step[d])[:, None]
        lam_dt = lam * step
        lam_bar = jnp.exp(lam_dt)
        b_bar = ((lam_bar - 1.0) / lam)[..., None] * lax.complex(b_re[d], b_im[d])
        c = lax.complex(c_re[d], c_im[d])
        pw = jnp.exp(lam_dt[None] * jnp.arange(L + 1, dtype=F32)[:, None, None])
        kj = jnp.einsum("ghp,jgp,gpk->gjhk", c, pw[:L], b_bar, precision=hp).real
        lag = np.arange(L)[None, :] - np.arange(L)[:, None]
        lag = lag if d == 0 else -lag
        sel = np.stack([(lag == j) for j in range(L)]).astype(np.float32)
        mt = jnp.einsum("jst,gjab->gsbta", sel, kj, precision=hp).reshape(S5_GROUPS, 16 * L, 16 * L)
        mt_total = mt_total + mt
        pw_dist = jnp.flip(pw[:L], 0) if d == 0 else pw[:L]
        pc = pw_dist.transpose(1, 0, 2)[:, :, None, :] * b_bar.transpose(0, 2, 1)[:, None, :, :]
        pm = jnp.concatenate([pc.real, pc.imag], axis=-1).reshape(S5_GROUPS, 16 * L, 2 * S5_STATE)
        pw_read = pw[1:] if d == 0 else jnp.flip(pw[1:], 0)
        qc = c[:, None, :, :] * pw_read.transpose(1, 0, 2)[:, :, None, :]
        qm = jnp.concatenate([qc.real, -qc.imag], axis=-1).reshape(S5_GROUPS, 16 * L, 2 * S5_STATE).transpose(0, 2, 1)
        a = pw[L]
        out[f"p{d}"], out[f"q{d}"] = pm, qm
        out[f"a{d}"] = jnp.concatenate([a.real, a.imag], axis=-1)
    out["mt"] = mt_total
    return out


def _s5_apow(lam_re, lam_im, log_step, n_steps, conj):
    lam_dt = lax.complex(lam_re, lam_im) * jnp.exp(log_step)[..., None]
    k = (S5_CHUNK * 2.0 ** jnp.arange(n_steps, dtype=F32))[None, None, :, None]
    a = jnp.exp(lam_dt[:, :, None, :] * k)
    re, im = a.real, (-a.imag if conj else a.imag)
    return jnp.stack([jnp.concatenate([re, re], -1), jnp.concatenate([-im, im], -1)], axis=3)


def _shift_rows(x, s, down):
    n = x.shape[0]
    if s >= n:
        return jnp.zeros_like(x)
    if s % 8 == 0:
        z = jnp.zeros((s, x.shape[1]), x.dtype)
        return jnp.concatenate([z, x[:n - s]], axis=0) if down else jnp.concatenate([x[s:], z], axis=0)
    row = lax.broadcasted_iota(jnp.int32, x.shape, 0)
    if down:
        return jnp.where(row >= s, pltpu.roll(x, s, 0), 0.0)
    return jnp.where(row < n - s, pltpu.roll(x, n - s, 0), 0.0)


def _cmul(x, a1, a2):
    return x * a1 + pltpu.roll(x, S5_STATE, 1) * a2


def _chunk_scan(s, apow_ref, down):
    n = s.shape[0]
    k, sh = 0, 1
    while sh < n:
        s = s + _cmul(_shift_rows(s, sh, down), apow_ref[k, 0:1, :], apow_ref[k, 1:2, :])
        k, sh = k + 1, sh * 2
    return s


def _s5_scan_fwd(name, ug, mats, apow):
    G, C, W = ug.shape
    n_steps = apow.shape[2]

    def body(u_ref, mt_ref, p0_ref, p1_ref, q0_ref, q1_ref, ap0_ref, ap1_ref, y_ref, h0_ref, h1_ref):
        u = u_ref[...]
        s0 = jnp.dot(u, p0_ref[...], preferred_element_type=F32)
        s1 = jnp.dot(u, p1_ref[...], preferred_element_type=F32)
        h0 = _shift_rows(_chunk_scan(s0, ap0_ref, True), 1, True)
        h1 = _shift_rows(_chunk_scan(s1, ap1_ref, False), 1, False)
        y = jnp.dot(u, mt_ref[...], preferred_element_type=F32)
        y += jnp.dot(h0.astype(BF16), q0_ref[...], preferred_element_type=F32)
        y += jnp.dot(h1.astype(BF16), q1_ref[...], preferred_element_type=F32)
        y_ref[...] = y.astype(y_ref.dtype)
        h0_ref[...] = h0
        h1_ref[...] = h1

    def gspec(shape):
        return pl.BlockSpec((None,) + shape, lambda g: (g,) + (0,) * len(shape))

    ap0, ap1 = apow[0], apow[1]
    return pl.pallas_call(
        body, name=name, grid=(G,),
        in_specs=[gspec((C, W)), gspec((W, W)), gspec((W, 128)), gspec((W, 128)), gspec((128, W)), gspec((128, W)),
                  gspec((n_steps, 2, 128)), gspec((n_steps, 2, 128))],
        out_specs=[gspec((C, W)), gspec((C, 128)), gspec((C, 128))],
        out_shape=[jax.ShapeDtypeStruct((G, C, W), BF16), jax.ShapeDtypeStruct((G, C, 128), F32),
                   jax.ShapeDtypeStruct((G, C, 128), F32)],
        compiler_params=_cparams(),
    )(ug, mats["mt"].astype(BF16), mats["p0"].astype(BF16), mats["p1"].astype(BF16),
      mats["q0"].astype(BF16), mats["q1"].astype(BF16), ap0, ap1)


def _s5_scan_bwd(name, dyg, ug, h0, h1, mats, apow_conj):
    G, C, W = ug.shape
    n_steps = apow_conj.shape[2]

    def body(dy_ref, u_ref, h0_ref, h1_ref, mt_ref, p0_ref, p1_ref, q0_ref, q1_ref, ap0_ref, ap1_ref,
             du_ref, dmt_ref, dp0_ref, dp1_ref, dq0_ref, dq1_ref, da_ref):
        dy, u = dy_ref[...], u_ref[...]
        nt, tn = (_DN["nt"], ((), ())), (_DN["tn"], ((), ()))
        dh0 = lax.dot_general(dy, q0_ref[...], nt, preferred_element_type=F32)
        dh1 = lax.dot_general(dy, q1_ref[...], nt, preferred_element_type=F32)
        ds0 = _chunk_scan(_shift_rows(dh0, 1, False), ap0_ref, False)
        ds1 = _chunk_scan(_shift_rows(dh1, 1, True), ap1_ref, True)
        ds0b, ds1b = ds0.astype(BF16), ds1.astype(BF16)
        du = lax.dot_general(dy, mt_ref[...], nt, preferred_element_type=F32)
        du += lax.dot_general(ds0b, p0_ref[...], nt, preferred_element_type=F32)
        du += lax.dot_general(ds1b, p1_ref[...], nt, preferred_element_type=F32)
        du_ref[...] = du.astype(du_ref.dtype)
        dmt_ref[...] = lax.dot_general(u, dy, tn, preferred_element_type=F32)
        dp0_ref[...] = lax.dot_general(u, ds0b, tn, preferred_element_type=F32)
        dp1_ref[...] = lax.dot_general(u, ds1b, tn, preferred_element_type=F32)
        h0v, h1v = h0_ref[...], h1_ref[...]
        dq0_ref[...] = lax.dot_general(h0v.astype(BF16), dy, tn, preferred_element_type=F32)
        dq1_ref[...] = lax.dot_general(h1v.astype(BF16), dy, tn, preferred_element_type=F32)
        rows = [jnp.sum(ds0 * h0v, axis=0, keepdims=True), jnp.sum(ds0 * pltpu.roll(h0v, S5_STATE, 1), axis=0, keepdims=True),
                jnp.sum(ds1 * h1v, axis=0, keepdims=True), jnp.sum(ds1 * pltpu.roll(h1v, S5_STATE, 1), axis=0, keepdims=True)]
        da_ref[...] = _stack_rows8(rows)

    def gspec(shape):
        return pl.BlockSpec((None,) + shape, lambda g: (g,) + (0,) * len(shape))

    f32s = lambda *s: jax.ShapeDtypeStruct((G,) + s, F32)
    return pl.pallas_call(
        body, name=name, grid=(G,),
        in_specs=[gspec((C, W)), gspec((C, W)), gspec((C, 128)), gspec((C, 128)), gspec((W, W)), gspec((W, 128)),
                  gspec((W, 128)), gspec((128, W)), gspec((128, W)), gspec((n_steps, 2, 128)), gspec((n_steps, 2, 128))],
        out_specs=[gspec((C, W)), gspec((W, W)), gspec((W, 128)), gspec((W, 128)), gspec((128, W)), gspec((128, W)),
                   gspec((8, 128))],
        out_shape=[jax.ShapeDtypeStruct((G, C, W), BF16), f32s(W, W), f32s(W, 128), f32s(W, 128), f32s(128, W),
                   f32s(128, W), f32s(8, 128)],
        compiler_params=_cparams(),
    )(dyg, ug, h0, h1, mats["mt"].astype(BF16), mats["p0"].astype(BF16), mats["p1"].astype(BF16),
      mats["q0"].astype(BF16), mats["q1"].astype(BF16), apow_conj[0], apow_conj[1])


def _to_groups(a):
    T = a.shape[0]
    return a.reshape(T // S5_CHUNK, S5_CHUNK, S5_GROUPS, S5_GROUP).transpose(2, 0, 1, 3).reshape(
        S5_GROUPS, T // S5_CHUNK, S5_CHUNK * S5_GROUP)


def _from_groups(a):
    G, C, W = a.shape
    return a.reshape(G, C, S5_CHUNK, S5_GROUP).transpose(1, 2, 0, 3).reshape(C * S5_CHUNK, G * S5_GROUP)


_S5_SSM = ("lam_re", "lam_im", "log_step", "b_re", "b_im", "c_re", "c_im")


def _s5_prepare(W, T):
    sp = tuple(W["s5_" + k] for k in _S5_SSM)
    mats, mats_vjp = jax.vjp(jax.vmap(_s5_matrices), *sp)
    n_steps = max(1, int(math.log2(T // S5_CHUNK)))
    apow = jax.vmap(lambda a, b, c: _s5_apow(a, b, c, n_steps, False))(*sp[:3])
    apow_c = jax.vmap(lambda a, b, c: _s5_apow(a, b, c, n_steps, True))(*sp[:3])
    return mats, mats_vjp, apow, apow_c


def _s5_fwd(l, h, w, small, g, b, mats, apow, apow_c):
    T = h.shape[0]
    tm = _tile(T, 512, 8)
    u, u_bf = _mm_simple(f"s5{l}_in", h, w["w_in"], "nn", None, epilogue=lambda accs: [accs[0], accs[0]], n_out=2,
                         out_dtypes=[F32, BF16])
    ug = _to_groups(u_bf)
    yg, h0, h1 = _s5_scan_fwd(f"s5{l}_scan", ug, mats, apow)
    yssm = _from_groups(yg)
    d_row, bglu_row = small["d"].reshape(1, -1), small["b_glu"].reshape(1, -1)

    def mid(y_t, u_t, d_t):
        return [_gelu(y_t + d_t * u_t)]

    y1 = _rows(f"s5{l}_mid", T, tm, [_rt(yssm, tm), _rt(u, tm), _full(d_row)], [_rt_out(T, D_MODEL, F32, tm)], mid)[0]

    def epi(accs, y1_t, bg_t):
        gate = _sigmoid(accs[0] + bg_t)
        return [y1_t * gate, gate]

    y2, gate = _mm_simple(f"s5{l}_glu", y1, w["w_glu"], "nn", None, extras=(y1, bglu_row), epilogue=epi, n_out=2,
                          out_dtypes=[BF16, F32])
    z, h_out = _out_proj_ln(f"s5{l}_out", y2, w["w_out"], h, g, b)
    return h_out, (h, u, ug, yssm, y1, y2, gate, z, h0, h1, mats, apow_c, d_row)


def _s5_bwd(l, dh_out, saved, w, small, g):
    h, u, ug, yssm, y1, y2, gate, z, h0, h1, mats, apow_c, d_row = saved
    T = h.shape[0]
    dz, dg, db = _ln_bwd(f"s5{l}_lnb", dh_out, z, g)

    def epi1(accs, y1_t, gate_t):
        dy2 = accs[0]
        dpre = dy2 * y1_t * gate_t * (1.0 - gate_t)
        return [dpre, dy2 * gate_t, _bcast8(jnp.sum(dpre, axis=0, keepdims=True))]

    dpre, dy1a, pbg = _mm_simple(f"s5{l}_dy2", dz, w["w_out"], "nt", None, extras=(y1, gate), epilogue=epi1, n_out=2,
                                 out_dtypes=[BF16, F32], psum_outs=1)
    dw_out = _mm_simple(f"s5{l}_dWout", y2, dz, "tn", BF16, tm=1024, tn=1024, tk=512)

    def epi2(accs, dy1a_t, yssm_t, u_t, d_t):
        dy1 = accs[0] + dy1a_t
        dy = dy1 * _dgelu(yssm_t + d_t * u_t)
        return [dy, dy * d_t, _bcast8(jnp.sum(dy * u_t, axis=0, keepdims=True))]

    dy, du_skip, pdd = _mm_simple(f"s5{l}_dy1", dpre, w["w_glu"], "nt", None, extras=(dy1a, yssm, u, d_row), epilogue=epi2,
                                  n_out=2, out_dtypes=[BF16, F32], psum_outs=1)
    dw_glu = _mm_simple(f"s5{l}_dWglu", y1, dpre, "tn", BF16, tm=1024, tn=1024, tk=512)
    dug, dmt, dp0, dp1, dq0, dq1, da = _s5_scan_bwd(f"s5{l}_scanb", _to_groups(dy), ug, h0, h1, mats, apow_c)
    du = _from_groups(dug) + du_skip

    def a_grad(p, q):
        return jnp.concatenate([p[:, :S5_STATE] + p[:, S5_STATE:], q[:, S5_STATE:] - q[:, :S5_STATE]], axis=-1)

    dmats = dict(mt=dmt, p0=dp0, p1=dp1, q0=dq0, q1=dq1, a0=a_grad(da[:, 0], da[:, 1]), a1=a_grad(da[:, 2], da[:, 3]))
    dh = _mm_simple(f"s5{l}_dX", du, w["w_in"], "nt", F32, extras=(dz,), epilogue=lambda accs, dz_t: [accs[0] + ALPHA * dz_t])
    dw_in = _mm_simple(f"s5{l}_dWin", h, du, "tn", BF16, tm=1024, tn=1024, tk=512)
    grads = dict(w_in=dw_in, w_glu=dw_glu, w_out=dw_out, d=_psum_rows(pdd), b_glu=_psum_rows(pbg), ln_g=dg, ln_b=db)
    return dh, grads, dmats


def _gla_levels(lc):
    ms, m = [], lc // 2
    while m >= 1:
        ms.append(m)
        m //= 2
    return ms


def _gla_scan_matrix(lc, rev):
    r = np.arange(lc)[:, None]
    t = np.arange(lc)[None, :]
    blocks = []
    for m in _gla_levels(lc):
        same = (r // m) == (t // m)
        upper = ((r // m) % 2) == 1
        blocks.append(same & np.where(upper, t >= r, t < r))
    blocks.append(t >= r)
    blocks.append(t < r)
    if rev:
        blocks = [blk[::-1, ::-1] for blk in blocks]
    return np.concatenate(blocks, axis=1).astype(np.float32)


def _gla_gates(z, lb):
    sig = _sigmoid(z)
    ls = jnp.minimum(z, 0.0) - jnp.log(1.0 + jnp.exp(-jnp.abs(z)))
    a = jnp.log(lb)
    bb = jnp.log(1.0 - lb) + ls
    lf = jnp.maximum(a, bb) + jnp.log(1.0 + jnp.exp(-jnp.abs(a - bb)))
    return lf, (1.0 - lb) * (1.0 - sig), sig


def _gla_cumsum(lf, rev):
    b, sh = lf, 1
    while sh < lf.shape[0]:
        b = b + _shift_rows(b, sh, not rev)
        sh *= 2
    return b


def _gla_bref(b, m, rev):
    lc, n = b.shape
    idx = m if rev else m - 1
    if 2 * m >= 8:
        nb = lc // (2 * m)
        b3 = b.reshape(nb, 2 * m, n)
        return jnp.broadcast_to(b3[:, idx:idx + 1, :], (nb, 2 * m, n)).reshape(lc, n)
    row = lax.broadcasted_iota(jnp.int32, b.shape, 0)
    j = row & (2 * m - 1)
    out = b
    for jj in range(2 * m):
        if jj != idx:
            out = jnp.where(j == jj, pltpu.roll(b, (jj - idx) % lc, 0), out)
    return out


def _gla_chunk(q, k, lf, rev):
    lc = q.shape[0]
    nt = (_DN["nt"], ((), ()))
    b = _gla_cumsum(lf, rev)
    row = lax.broadcasted_iota(jnp.int32, (lc, HG_DIM), 0)
    ri = lax.broadcasted_iota(jnp.int32, (lc, lc), 0)
    ci = lax.broadcasted_iota(jnp.int32, (lc, lc), 1)
    qb, kb = q.astype(BF16), k.astype(BF16)
    sc = jnp.where(ri == ci, lax.dot_general(qb, kb, nt, preferred_element_type=F32), 0.0)
    levels = []
    for m in _gla_levels(lc):
        lg = int(math.log2(m))
        isq = ((row >> lg) & 1) == (0 if rev else 1)
        bref = _gla_bref(b, m, rev)
        w = jnp.exp(jnp.where(isq, b - bref, bref - b))
        xf = jnp.where(isq, q * w, 0.0)
        yf = jnp.where(isq, 0.0, k * w)
        bmask = (ri >> (lg + 1)) == (ci >> (lg + 1))
        xb, yb = xf.astype(BF16), yf.astype(BF16)
        sc = sc + jnp.where(bmask, lax.dot_general(xb, yb, nt, preferred_element_type=F32), 0.0)
        levels.append((isq, w, xf, yf, xb, yb, bmask))
    return b, sc, levels, (ri == ci)


def _hg_specs(T, lc, rev, backward):
    nc = T // lc
    cc = (lambda c: nc - 1 - c) if rev != backward else (lambda c: c)
    zcol = HG_HEADS * (2 if rev else 1)
    q_spec = pl.BlockSpec((lc, HG_DIM), lambda h, c: (cc(c), h))
    z_spec = pl.BlockSpec((lc, HG_DIM), lambda h, c: (cc(c), zcol + h))
    v_spec = pl.BlockSpec((lc, HG_DIM), lambda h, c: (cc(c), 3 * HG_HEADS + h))
    lb_spec = pl.BlockSpec((1, HG_DIM), lambda h, c: (0, h))
    st_spec = pl.BlockSpec((None, None, HG_DIM, HG_DIM), lambda h, c: (h, cc(c), 0, 0))
    return nc, q_spec, z_spec, v_spec, lb_spec, st_spec


def _gla_fwd(name, proj, lb_row, rev):
    T = proj.shape[0]
    lc = _tile(T, HG_CHUNK, 8)
    nc, q_spec, z_spec, v_spec, lb_spec, st_spec = _hg_specs(T, lc, rev, False)
    last = 0 if rev else lc - 1

    def body(q_ref, z_ref, v_ref, lb_ref, o_ref, st_ref, st_s):
        @pl.when(pl.program_id(1) == 0)
        def _():
            st_s[...] = jnp.zeros(st_s.shape, F32)

        q = _silu(q_ref[...])
        lf, k, _ = _gla_gates(z_ref[...], lb_ref[...])
        vb = v_ref[...].astype(BF16)
        b, sc, _, _ = _gla_chunk(q, k, lf, rev)
        st0 = st_s[...]
        st_ref[...] = st0
        bl = b[last:last + 1, :]
        o = jnp.dot(sc.astype(BF16), vb, preferred_element_type=F32)
        o += lax.dot_general((q * jnp.exp(b)).astype(BF16), st0.astype(BF16), (_DN["nt"], ((), ())), preferred_element_type=F32)
        o_ref[...] = o
        kd = (k * jnp.exp(bl - b)).astype(BF16)
        st_s[...] = st0 * jnp.exp(bl) + lax.dot_general(vb, kd, (_DN["tn"], ((), ())), preferred_element_type=F32)

    return pl.pallas_call(
        body, name=name, grid=(HG_HEADS, nc),
        in_specs=[q_spec, z_spec, v_spec, lb_spec], out_specs=[q_spec, st_spec],
        out_shape=[jax.ShapeDtypeStruct((T, D_MODEL), F32), jax.ShapeDtypeStruct((HG_HEADS, nc, HG_DIM, HG_DIM), F32)],
        scratch_shapes=[pltpu.VMEM((HG_DIM, HG_DIM), F32)], compiler_params=_cparams(),
    )(proj, proj, proj, lb_row)


def _gla_bwd(name, proj, lb_row, do, st, rev):
    T = proj.shape[0]
    lc = _tile(T, HG_CHUNK, 8)
    nc, q_spec, z_spec, v_spec, lb_spec, st_spec = _hg_specs(T, lc, rev, True)
    wall = jnp.asarray(_gla_scan_matrix(lc, rev), BF16)
    last = 0 if rev else lc - 1

    def body(q_ref, z_ref, v_ref, lb_ref, do_ref, st_ref, wall_ref, dq_ref, dz_ref, dv_ref, dlb_ref, dst_s):
        first = pl.program_id(1) == 0

        @pl.when(first)
        def _():
            dst_s[...] = jnp.zeros(dst_s.shape, F32)
            dlb_ref[...] = jnp.zeros(dlb_ref.shape, F32)

        nn, nt, tn = (_DN["nn"], ((), ())), (_DN["nt"], ((), ())), (_DN["tn"], ((), ()))
        dot = functools.partial(lax.dot_general, preferred_element_type=F32)
        qr, z, lb = q_ref[...], z_ref[...], lb_ref[...]
        q = _silu(qr)
        lf, k, sig = _gla_gates(z, lb)
        vb = v_ref[...].astype(BF16)
        b, sc, levels, eye = _gla_chunk(q, k, lf, rev)
        st0, dst = st_ref[...], dst_s[...]
        st0b, dstb = st0.astype(BF16), dst.astype(BF16)
        dob = do_ref[...].astype(BF16)
        bl = b[last:last + 1, :]
        eb, ebl, ekd = jnp.exp(b), jnp.exp(bl), jnp.exp(bl - b)
        qe, kd = q * eb, k * ekd
        kdb = kd.astype(BF16)
        dsc = dot(dob, vb, nt)
        dv_ref[...] = dot(sc.astype(BF16), dob, tn) + dot(kdb, dstb, nt)
        dqe = dot(dob, st0b, nn)
        dkd = dot(vb, dstb, nn)
        dq = dqe * eb
        dk = dkd * ekd
        zs = []
        dsd = jnp.where(eye, dsc, 0.0).astype(BF16)
        dq += dot(dsd, k.astype(BF16), nn)
        dk += dot(dsd, q.astype(BF16), tn)
        for (upper, w, xf, yf, xb, yb, bmask) in levels:
            dsl = jnp.where(bmask, dsc, 0.0).astype(BF16)
            dx = dot(dsl, yb, nn)
            dy = dot(dsl, xb, tn)
            dq += jnp.where(upper, dx * w, 0.0)
            dk += jnp.where(upper, 0.0, dy * w)
            zs.append(jnp.where(upper, dx * xf, dy * yf).astype(BF16))
        zs.append((dqe * qe).astype(BF16))
        zs.append((dkd * kd).astype(BF16))
        zl = jnp.sum(dst * st0, axis=0, keepdims=True) * ebl
        dlf = dot(wall_ref[...], jnp.concatenate(zs, axis=0), nn) + zl
        dst_s[...] = dst * ebl + dot(dob, qe.astype(BF16), tn)
        inv_f = jnp.exp(-lf)
        one_sig = 1.0 - sig
        dz_ref[...] = (dlf * inv_f - dk) * (1.0 - lb) * sig * one_sig
        dq_ref[...] = dq * _dsilu(qr)
        dlb_ref[...] += _bcast8(jnp.sum((dlf * inv_f - dk) * one_sig, axis=0, keepdims=True))

    big = jax.ShapeDtypeStruct((T, D_MODEL), F32)
    return pl.pallas_call(
        body, name=name, grid=(HG_HEADS, nc),
        in_specs=[q_spec, z_spec, v_spec, lb_spec, q_spec, st_spec, pl.BlockSpec(wall.shape, lambda h, c: (0, 0))],
        out_specs=[q_spec, q_spec, q_spec, pl.BlockSpec((None, 8, HG_DIM), lambda h, c: (h, 0, 0))],
        out_shape=[big, big, big, jax.ShapeDtypeStruct((HG_HEADS, 8, HG_DIM), F32)],
        scratch_shapes=[pltpu.VMEM((HG_DIM, HG_DIM), F32)], compiler_params=_cparams(),
    )(proj, proj, proj, lb_row, do, st, wall)


def _hg_lower_bounds(hg_lower_bound, layer):
    lbs = jax.nn.softmax(hg_lower_bound, axis=0)
    lbs = jnp.cumsum(lbs, axis=0) - lbs[0]
    return lbs[layer].reshape(1, -1)


def _hg_post(o_fw, o_bw, g_raw, ng):
    outs, ons, os_, rs = [], [], [], []
    for hd in range(o_fw.shape[1] // HG_DIM):
        sl = slice(hd * HG_DIM, (hd + 1) * HG_DIM)
        o = o_fw[:, sl] + o_bw[:, sl]
        r = lax.rsqrt(jnp.mean(o * o, axis=-1, keepdims=True) + RMS_EPS)
        on = o * r * ng
        outs.append(on * _silu(g_raw[:, sl]))
        ons.append(on)
        os_.append(o)
        rs.append(r)
    return outs, ons, os_, rs


def _hg_fwd(l, h, w, small, g, b):
    T = h.shape[0]
    tm = _tile(T, 512, 8)
    proj = _mm_simple(f"hg{l}_in", h, w["w_in"], "nn", F32, tn=1280, j_outer=True)
    lb_fn = lambda p: _hg_lower_bounds(p, l)
    lb_row, lb_vjp = jax.vjp(lb_fn, small["lower_bound"])
    o_fw, st_fw = _gla_fwd(f"hg{l}_gla_fw", proj, lb_row, False)
    o_bw, st_bw = _gla_fwd(f"hg{l}_gla_bw", proj, lb_row, True)
    ng = small["norm_g"].reshape(1, HG_DIM)

    def post(of_t, ob_t, g_t, ng_t):
        return [jnp.concatenate(_hg_post(of_t, ob_t, g_t, ng_t)[0], axis=1)]

    og = _rows(f"hg{l}_post", T, tm,
               [_rt(o_fw, tm), _rt(o_bw, tm), (proj, (tm, D_MODEL), lambda i: (i, 4)), _full(ng)],
               [_rt_out(T, D_MODEL, BF16, tm)], post)[0]
    z, h_out = _out_proj_ln(f"hg{l}_out", og, w["w_out"], h, g, b)
    return h_out, (h, proj, lb_row, lb_vjp, st_fw, st_bw, o_fw, o_bw, ng, og, z)


def _hg_bwd(l, dh_out, saved, w, small, g):
    h, proj, lb_row, lb_vjp, st_fw, st_bw, o_fw, o_bw, ng, og, z = saved
    T = h.shape[0]
    dz, dg, db = _ln_bwd(f"hg{l}_lnb", dh_out, z, g)
    tm = _tile(T, 512, 8)
    nn_tiles = D_MODEL // HG_DIM

    def epi(accs, of_t, ob_t, g_t, ng_t):
        dog = accs[0]
        _, ons, os_, rs = _hg_post(of_t, ob_t, g_t, ng_t)
        dos, dgs, dngs = [], [], []
        for hd in range(nn_tiles):
            sl = slice(hd * HG_DIM, (hd + 1) * HG_DIM)
            d, o, r = dog[:, sl], os_[hd], rs[hd]
            dgs.append(d * ons[hd] * _dsilu(g_t[:, sl]))
            don = d * _silu(g_t[:, sl])
            dngs.append(jnp.sum(don * o * r, axis=0, keepdims=True))
            dxn = don * ng_t
            dos.append(r * dxn - o * (r * r * r) * jnp.mean(dxn * o, axis=-1, keepdims=True))
        return [jnp.concatenate(dos, axis=1), jnp.concatenate(dgs, axis=1), _bcast8(jnp.concatenate(dngs, axis=1))]

    grid = (T // tm, 1, 1)
    row_map = lambda i, j, k: (i, 0)
    do, dg_raw, png = _mm(
        f"hg{l}_dog", grid, [(dz, (tm, D_MODEL), row_map)], [(w["w_out"], (D_MODEL, D_MODEL), lambda i, j, k: (0, 0))],
        [(0, 0, 0, "nt")], 1, (tm, D_MODEL),
        [(o_fw, (tm, D_MODEL), row_map), (o_bw, (tm, D_MODEL), row_map), (proj, (tm, D_MODEL), lambda i, j, k: (i, 4)),
         (ng, (1, HG_DIM), lambda i, j, k: (0, 0))],
        [((T, D_MODEL), F32, (tm, D_MODEL), row_map), ((T, D_MODEL), F32, (tm, D_MODEL), row_map),
         ((T // tm * 8, D_MODEL), F32, (8, D_MODEL), row_map)], epi)
    dw_out = _mm_simple(f"hg{l}_dWout", og, dz, "tn", BF16, tm=1024, tn=1024, tk=512)
    dq_f, dz_f, dv_f, dlb_f = _gla_bwd(f"hg{l}_glab_fw", proj, lb_row, do, st_fw, False)
    dq_b, dz_b, dv_b, dlb_b = _gla_bwd(f"hg{l}_glab_bw", proj, lb_row, do, st_bw, True)
    dproj = jnp.concatenate([dq_f + dq_b, dz_f, dz_b, dv_f + dv_b, dg_raw], axis=1).astype(BF16)
    dh = _mm_simple(f"hg{l}_dX", dproj, w["w_in"], "nt", F32, tm=256, tk=5 * D_MODEL, extras=(dz,),
                    epilogue=lambda accs, dz_t: [accs[0] + ALPHA * dz_t])
    dw_in = _mm_simple(f"hg{l}_dWin", h, dproj, "tn", BF16, tm=1024, tn=1280, tk=512)
    dlb_row = (dlb_f[:, 0, :] + dlb_b[:, 0, :]).reshape(1, D_MODEL)
    grads = dict(w_in=dw_in, w_out=dw_out, lower_bound=lb_vjp(dlb_row)[0],
                 norm_g=_psum_rows(png).reshape(nn_tiles, HG_DIM).sum(axis=0), ln_g=dg, ln_b=db)
    return dh, grads


MLA_W = 256
MLA_SCALE = (MLA_NOPE + MLA_ROPE) ** -0.5


def _swap_halves(a):
    n = a.shape[-1] // 2
    return jnp.concatenate([a[..., n:], a[..., :n]], axis=-1)


def _mla_ext_weights(w_in, w_q_b):
    w_in_ext = jnp.concatenate([w_in, _swap_halves(w_in[:, MLA_Q_LORA + MLA_KV_LORA:])], axis=1)
    wq = w_q_b.reshape(MLA_Q_LORA, MLA_HEADS, MLA_NOPE + MLA_ROPE)
    wq_ext = jnp.concatenate([wq, _swap_halves(wq[:, :, MLA_NOPE:])], axis=2).reshape(MLA_Q_LORA, MLA_HEADS * MLA_W)
    return w_in_ext, wq_ext


def _mla_ext_grads(dw_in_ext, dwq_ext):
    n_lat = MLA_Q_LORA + MLA_KV_LORA
    dw_in = jnp.concatenate([dw_in_ext[:, :n_lat], dw_in_ext[:, n_lat:n_lat + MLA_ROPE]
                             + _swap_halves(dw_in_ext[:, n_lat + MLA_ROPE:])], axis=1)
    dq = dwq_ext.reshape(MLA_Q_LORA, MLA_HEADS, MLA_W)
    dwq = jnp.concatenate([dq[:, :, :MLA_NOPE], dq[:, :, MLA_NOPE:MLA_NOPE + MLA_ROPE]
                           + _swap_halves(dq[:, :, MLA_NOPE + MLA_ROPE:])], axis=2)
    return dw_in, dwq.reshape(MLA_Q_LORA, MLA_HEADS * (MLA_NOPE + MLA_ROPE))


def _rope_table(positions):
    half = MLA_ROPE // 2
    inv_freq = 1.0 / (ROPE_THETA ** (jnp.arange(half, dtype=F32) * (2.0 / MLA_ROPE)))
    ang = positions.astype(F32)[:, None] * inv_freq
    cos, sin = jnp.cos(ang), jnp.sin(ang)
    return jnp.concatenate([cos, cos, -sin, sin], axis=1)


def _rope_sum(prod):
    return prod + pltpu.roll(prod, MLA_ROPE, 1)


def _low_half(a):
    lane = lax.broadcasted_iota(jnp.int32, a.shape, 1)
    return jnp.where(lane < MLA_ROPE, a, 0.0)


def _rms(x, g):
    r = lax.rsqrt(jnp.mean(x * x, axis=-1, keepdims=True) + RMS_EPS)
    return x * r * g


def _rms_bwd(x, g, dy):
    r = lax.rsqrt(jnp.mean(x * x, axis=-1, keepdims=True) + RMS_EPS)
    dxn = dy * g
    dx = r * dxn - x * (r * r * r) * jnp.mean(dxn * x, axis=-1, keepdims=True)
    return dx, jnp.sum(dy * x * r, axis=0, keepdims=True)


ATT_TILE = 512
ATT_BLOCK = 2048


def _lanes(col, n):
    return jnp.tile(col, (1, n // 128))


def _flash_fwd(name, q, k, v):
    H, T, _ = q.shape
    tq, tkb = _tile(T, ATT_TILE, 8), _tile(T, ATT_BLOCK, 128)
    ts = _tile(tkb, ATT_TILE, 128)
    nk, nsub = T // tkb, tkb // ts

    def body(q_ref, k_ref, v_ref, o_ref, lse_ref, m_s, acc_s):
        ki = pl.program_id(2)

        @pl.when(ki == 0)
        def _():
            m_s[...] = jnp.full(m_s.shape, -jnp.inf, F32)
            acc_s[...] = jnp.zeros(acc_s.shape, F32)

        qv = q_ref[...]
        m, acc = m_s[...], acc_s[...]
        for j in range(nsub):
            kj, vj = k_ref[j * ts:(j + 1) * ts, :], v_ref[j * ts:(j + 1) * ts, :]
            s = lax.dot_general(qv, kj, (_DN["nt"], ((), ())), preferred_element_type=F32)
            m_new = jnp.maximum(m, jnp.max(s, axis=-1, keepdims=True))
            p = jnp.exp(s - _lanes(m_new, ts)).astype(BF16)
            acc = _lanes(jnp.exp(m - m_new), 2 * MLA_V) * acc + jnp.dot(p, vj, preferred_element_type=F32)
            m = m_new
        m_s[...], acc_s[...] = m, acc

        @pl.when(ki == nk - 1)
        def _():
            l = acc[:, MLA_V:]
            o_ref[...] = acc[:, :MLA_V] / l
            lse_ref[...] = m + jnp.log(l)

    return pl.pallas_call(
        body, name=name, grid=(H, T // tq, nk),
        in_specs=[pl.BlockSpec((None, tq, MLA_W), lambda h, i, j: (h, i, 0)),
                  pl.BlockSpec((None, tkb, MLA_W), lambda h, i, j: (h, j, 0)),
                  pl.BlockSpec((None, tkb, 2 * MLA_V), lambda h, i, j: (h, j, 0))],
        out_specs=[pl.BlockSpec((tq, MLA_V), lambda h, i, j: (i, h)), pl.BlockSpec((tq, MLA_V), lambda h, i, j: (i, h))],
        out_shape=[jax.ShapeDtypeStruct((T, H * MLA_V), F32), jax.ShapeDtypeStruct((T, H * MLA_V), F32)],
        scratch_shapes=[pltpu.VMEM((tq, MLA_V), F32), pltpu.VMEM((tq, 2 * MLA_V), F32)],
        compiler_params=_cparams(),
    )(q, k, v)


def _flash_bwd(name, q, k, v, do, lse, delta):
    H, T, _ = q.shape
    nt, tn = (_DN["nt"], ((), ())), (_DN["tn"], ((), ()))
    tk, tqb = _tile(T, ATT_TILE, 128), _tile(T, ATT_BLOCK, 128)
    tqs = _tile(tqb, ATT_TILE, 128)
    nk, nqb, nsub = T // tk, T // tqb, tqb // tqs

    def body(q_ref, k_ref, v_ref, do_ref, lse_ref, dl_ref, dq_ref, dk_ref, dv_ref, dk_s, dv_s):
        ki, qb = pl.program_id(1), pl.program_id(2)

        @pl.when(jnp.logical_and(ki == 0, qb == 0))
        def _():
            dq_ref[...] = jnp.zeros(dq_ref.shape, F32)

        @pl.when(qb == 0)
        def _():
            dk_s[...] = jnp.zeros(dk_s.shape, F32)
            dv_s[...] = jnp.zeros(dv_s.shape, F32)

        kv, vv = k_ref[...], v_ref[...]
        dk, dv = dk_s[...], dv_s[...]
        for j in range(nsub):
            sl = slice(j * tqs, (j + 1) * tqs)
            qj, doj = q_ref[sl, :], do_ref[sl, :]
            s = lax.dot_general(qj, kv, nt, preferred_element_type=F32)
            dp = lax.dot_general(doj, vv, nt, preferred_element_type=F32)
            p = jnp.exp(s - _lanes(lse_ref[sl, :], tk))
            ds = (p * (dp - _lanes(dl_ref[sl, :], tk))).astype(BF16)
            dv = dv + lax.dot_general(p.astype(BF16), doj, tn, preferred_element_type=F32)
            dk = dk + lax.dot_general(ds, qj, tn, preferred_element_type=F32)
            rows = pl.ds(pl.multiple_of(qb * tqb + j * tqs, tqs), tqs)
            dq_ref[rows, :] += jnp.dot(ds, kv, preferred_element_type=F32)
        dk_s[...], dv_s[...] = dk, dv

        @pl.when(qb == nqb - 1)
        def _():
            dk_ref[...] = dk
            dv_ref[...] = dv

    return pl.pallas_call(
        body, name=name, grid=(H, nk, nqb),
        in_specs=[pl.BlockSpec((None, tqb, MLA_W), lambda h, i, j: (h, j, 0)),
                  pl.BlockSpec((None, tk, MLA_W), lambda h, i, j: (h, i, 0)),
                  pl.BlockSpec((None, tk, MLA_V), lambda h, i, j: (h, i, 0)),
                  pl.BlockSpec((tqb, MLA_V), lambda h, i, j: (j, h)),
                  pl.BlockSpec((tqb, MLA_V), lambda h, i, j: (j, h)),
                  pl.BlockSpec((tqb, MLA_V), lambda h, i, j: (j, h))],
        out_specs=[pl.BlockSpec((None, T, MLA_W), lambda h, i, j: (h, 0, 0)),
                   pl.BlockSpec((None, tk, MLA_W), lambda h, i, j: (h, i, 0)),
                   pl.BlockSpec((None, tk, MLA_V), lambda h, i, j: (h, i, 0))],
        out_shape=[jax.ShapeDtypeStruct((H, T, MLA_W), F32), jax.ShapeDtypeStruct((H, T, MLA_W), F32),
                   jax.ShapeDtypeStruct((H, T, MLA_V), F32)],
        scratch_shapes=[pltpu.VMEM((tk, MLA_W), F32), pltpu.VMEM((tk, MLA_V), F32)], compiler_params=_cparams(),
    )(q, k, v, do, lse, delta)


def _mla_fwd(l, h, w, small, g, b, cs):
    T = h.shape[0]
    tm = _tile(T, 512, 8)
    H = MLA_HEADS
    gq, gkv = small["q_norm_g"].reshape(1, -1), small["kv_norm_g"].reshape(1, -1)
    n_ext = MLA_Q_LORA + MLA_KV_LORA + 2 * MLA_ROPE
    row = lambda i, j, k: (i, 0)
    fix = lambda i, j, k: (0, 0)

    def epi_lat(accs, gq_t, gkv_t):
        a = accs[0]
        ql, kvl = a[:, :MLA_Q_LORA], a[:, MLA_Q_LORA:MLA_Q_LORA + MLA_KV_LORA]
        return [ql, kvl, a[:, MLA_Q_LORA + MLA_KV_LORA:], _rms(ql, gq_t), _rms(kvl, gkv_t)]

    ql, kvl, kr, xq, xkv = _mm(
        f"mla{l}_in", (T // tm, 1, 1), [(h, (tm, D_MODEL), row)], [(w["w_in_ext"], (D_MODEL, n_ext), fix)],
        [(0, 0, 0, "nn")], 1, (tm, n_ext), [(gq, gq.shape, fix), (gkv, gkv.shape, fix)],
        [((T, MLA_Q_LORA), F32, (tm, MLA_Q_LORA), row), ((T, MLA_KV_LORA), F32, (tm, MLA_KV_LORA), row),
         ((T, 2 * MLA_ROPE), F32, (tm, 2 * MLA_ROPE), row), ((T, MLA_Q_LORA), BF16, (tm, MLA_Q_LORA), row),
         ((T, MLA_KV_LORA), BF16, (tm, MLA_KV_LORA), row)], epi_lat)

    def epi_q(accs, cs_t):
        a = accs[0]
        return [jnp.concatenate([a[:, :MLA_NOPE], _rope_sum(a[:, MLA_NOPE:] * cs_t)], axis=1) * MLA_SCALE]

    head_out = lambda i, j, k: (j, i, 0)
    q = _mm(f"mla{l}_q", (T // tm, H, 1), [(xq, (tm, MLA_Q_LORA), row)],
            [(w["wq_ext"], (MLA_Q_LORA, MLA_W), lambda i, j, k: (0, j))], [(0, 0, 0, "nn")], 1, (tm, MLA_W),
            [(cs, (tm, 2 * MLA_ROPE), row)], [((H, T, MLA_W), BF16, (None, tm, MLA_W), head_out)], epi_q)[0]

    def epi_kv(accs, kr_t, cs_t):
        a = accs[0]
        v_t = a[:, MLA_NOPE:]
        return [jnp.concatenate([a[:, :MLA_NOPE], _low_half(_rope_sum(kr_t * cs_t))], axis=1),
                jnp.concatenate([v_t, jnp.ones_like(v_t)], axis=1)]

    k, v = _mm(f"mla{l}_kv", (T // tm, H, 1), [(xkv, (tm, MLA_KV_LORA), row)],
               [(w["w_kv_b"], (MLA_KV_LORA, MLA_W), lambda i, j, k: (0, j))], [(0, 0, 0, "nn")], 1, (tm, MLA_W),
               [(kr, (tm, 2 * MLA_ROPE), row), (cs, (tm, 2 * MLA_ROPE), row)],
               [((H, T, MLA_W), BF16, (None, tm, MLA_W), head_out), ((H, T, 2 * MLA_V), BF16, (None, tm, 2 * MLA_V), head_out)],
               epi_kv)
    o, lse = _flash_fwd(f"mla{l}_attn", q, k, v)
    z, h_out = _out_proj_ln(f"mla{l}_out", o, w["w_out"], h, g, b)
    return h_out, (h, ql, kvl, xq, xkv, q, k, v, o, lse, z, gq, gkv, cs)


def _mla_bwd(l, dh_out, saved, w, g):
    h, ql, kvl, xq, xkv, q, k, v, o, lse, z, gq, gkv, cs = saved
    T = h.shape[0]
    tm = _tile(T, 512, 8)
    H = MLA_HEADS
    dz, dg, db = _ln_bwd(f"mla{l}_lnb", dh_out, z, g)
    def epi_do(accs, o_t):
        d = accs[0]
        deltas = [jnp.broadcast_to(jnp.sum(d[:, hd * MLA_V:(hd + 1) * MLA_V] * o_t[:, hd * MLA_V:(hd + 1) * MLA_V],
                                           axis=-1, keepdims=True), (d.shape[0], MLA_V)) for hd in range(H)]
        return [d, jnp.concatenate(deltas, axis=1)]

    do, delta = _mm_simple(f"mla{l}_dO", dz, w["w_out"], "nt", None, extras=(o,), epilogue=epi_do, n_out=2,
                           out_dtypes=[BF16, F32])
    dw_out = _mm_simple(f"mla{l}_dWout", o, dz, "tn", BF16, tm=1024, tn=1024, tk=512)
    dq, dk, dv = _flash_bwd(f"mla{l}_attnb", q, k, v, do, lse, delta)
    head_in = lambda i, hh: (hh, i, 0)
    row2 = lambda i, hh: (i, 0)

    def fn_q(dq_t, cs_t):
        d = dq_t[:, MLA_NOPE:]
        return [jnp.concatenate([dq_t[:, :MLA_NOPE], _rope_sum(d) * cs_t], axis=1) * MLA_SCALE]

    dq_eff = _grid_call(f"mla{l}_dqeff", (T // tm, H), [(dq, (None, tm, MLA_W), head_in), (cs, (tm, 2 * MLA_ROPE), row2)],
                        [((T, H * MLA_W), BF16, (tm, MLA_W), lambda i, hh: (i, hh))], fn_q)[0]

    def fn_kv(dk_t, dv_t, cs_t):
        return [jnp.concatenate([dk_t[:, :MLA_NOPE], dv_t], axis=1), _rope_sum(_low_half(dk_t[:, MLA_NOPE:])) * cs_t]

    dkv_eff, dkr = _grid_call(
        f"mla{l}_dkveff", (T // tm, H),
        [(dk, (None, tm, MLA_W), head_in), (dv, (None, tm, MLA_V), head_in), (cs, (tm, 2 * MLA_ROPE), row2)],
        [((T, H * MLA_W), BF16, (tm, MLA_W), lambda i, hh: (i, hh)), ((T, 2 * MLA_ROPE), F32, (tm, 2 * MLA_ROPE), row2)],
        fn_kv, acc_outs=(1,))
    dxq = _mm_simple(f"mla{l}_dxq", dq_eff, w["wq_ext"], "nt", F32, tn=MLA_Q_LORA)
    dwq_ext = _mm_simple(f"mla{l}_dWq", xq, dq_eff, "tn", BF16, tm=MLA_Q_LORA, tn=1024, tk=512)
    dxkv = _mm_simple(f"mla{l}_dxkv", dkv_eff, w["w_kv_b"], "nt", F32, tn=MLA_KV_LORA)
    dwkv = _mm_simple(f"mla{l}_dWkv", xkv, dkv_eff, "tn", BF16, tm=MLA_KV_LORA, tn=1024, tk=512)

    def fn_lat(ql_t, dxq_t, gq_t, kvl_t, dxkv_t, gkv_t, dkr_t):
        dql, dgq = _rms_bwd(ql_t, gq_t, dxq_t)
        dkvl, dgkv = _rms_bwd(kvl_t, gkv_t, dxkv_t)
        return [jnp.concatenate([dql, dkvl, dkr_t], axis=1), _bcast8(dgq), _bcast8(dgkv)]

    n_ext = MLA_Q_LORA + MLA_KV_LORA + 2 * MLA_ROPE
    dlat, pgq, pgkv = _rows(f"mla{l}_dlat", T, tm,
                            [_rt(ql, tm), _rt(dxq, tm), _full(gq), _rt(kvl, tm), _rt(dxkv, tm), _full(gkv), _rt(dkr, tm)],
                            [_rt_out(T, n_ext, BF16, tm), _ps_out(T, MLA_Q_LORA, tm), _ps_out(T, MLA_KV_LORA, tm)], fn_lat)
    dh = _mm_simple(f"mla{l}_dX", dlat, w["w_in_ext"], "nt", F32, tk=n_ext, extras=(dz,),
                    epilogue=lambda accs, dz_t: [accs[0] + ALPHA * dz_t])
    dw_in_ext = _mm_simple(f"mla{l}_dWin", h, dlat, "tn", BF16, tm=1024, tn=n_ext, tk=512)
    dw_in, dwq = _mla_ext_grads(dw_in_ext.astype(F32), dwq_ext.astype(F32))
    grads = dict(w_in=dw_in, w_q_b=dwq, w_kv_b=dwkv, w_out=dw_out, q_norm_g=_psum_rows(pgq), kv_norm_g=_psum_rows(pgkv),
                 ln_g=dg, ln_b=db)
    return dh, grads


def _loss_head(y, target):
    T = y.shape[0]
    tm = _tile(T, 512, 8)

    def fn(y_t, t_t):
        d = y_t - t_t
        part = 0.5 * jnp.sum(jnp.mean(d * d, axis=-1, keepdims=True), axis=0, keepdims=True)
        return [d * (1.0 / D_MODEL), jnp.broadcast_to(part, (8, 128))]

    dy, part = _rows("loss_head", T, tm, [_rt(y, tm), _rt(target, tm)],
                     [_rt_out(T, D_MODEL, F32, tm), ((T // tm * 8, 128), F32, (8, 128), lambda i: (i, 0))], fn)
    return jnp.sum(part.reshape(-1, 8, 128)[:, 0, 0]), dy


_S5_VECTORS = ("d", "b_glu")


def _local_step(x, positions, target, W):
    row = lambda a, i: a[i].reshape(1, -1)
    w_in_ext, wq_ext = _mla_ext_weights(W["mla_w_in"][0], W["mla_w_q_b"][0])
    cs = _rope_table(positions)
    s5_mats, s5_mats_vjp, s5_apow, s5_apow_c = _s5_prepare(W, x.shape[0])
    h, saves = x, []
    for l in range(DEPTH):
        kind, slot = LAYER_MIXER[l], l // 3
        g, b = row(W["ln_mix_g"], l), row(W["ln_mix_b"], l)
        if kind == 0:
            w = {k: W["s5_" + k][slot] for k in ("w_in", "w_glu", "w_out")}
            small = {k: W["s5_" + k][slot] for k in _S5_VECTORS}
            h, sv = _s5_fwd(l, h, w, small, g, b, {k: v[slot] for k, v in s5_mats.items()}, s5_apow[slot], s5_apow_c[slot])
        elif kind == 1:
            w = dict(w_in=W["hg_w_in"][slot], w_out=W["hg_w_out"][slot])
            small = dict(lower_bound=W["hg_lower_bound"], norm_g=W["hg_norm_g"][slot])
            h, sv = _hg_fwd(l, h, w, small, g, b)
        else:
            w = dict(w_in_ext=w_in_ext, wq_ext=wq_ext, w_kv_b=W["mla_w_kv_b"][slot], w_out=W["mla_w_out"][slot])
            small = dict(q_norm_g=W["mla_q_norm_g"][slot], kv_norm_g=W["mla_kv_norm_g"][slot])
            h, sv = _mla_fwd(l, h, w, small, g, b, cs)
        h, fsv = _ffn_fwd(l, h, W["ffn_w_in"][l], W["ffn_w_out"][l], row(W["ln_ffn_g"], l), row(W["ln_ffn_b"], l))
        saves.append((w, small, sv, fsv))
    loss, dh = _loss_head(h, target)
    per_layer = [None] * DEPTH
    s5_dmats = {}
    for l in reversed(range(DEPTH)):
        kind = LAYER_MIXER[l]
        w, small, sv, fsv = saves[l]
        dh, gf = _ffn_bwd(l, dh, fsv, W["ffn_w_in"][l], W["ffn_w_out"][l], row(W["ln_ffn_g"], l))
        g = row(W["ln_mix_g"], l)
        if kind == 0:
            dh, gm, s5_dmats[l // 3] = _s5_bwd(l, dh, sv, w, small, g)
        elif kind == 1:
            dh, gm = _hg_bwd(l, dh, sv, w, small, g)
        else:
            dh, gm = _mla_bwd(l, dh, sv, w, g)
        per_layer[l] = (gm, gf)
    grads = {}
    stack = lambda xs: jnp.stack([a.astype(F32) if a.dtype != BF16 else a for a in xs])
    grads["ln_mix_g"] = stack([per_layer[l][0]["ln_g"] for l in range(DEPTH)])
    grads["ln_mix_b"] = stack([per_layer[l][0]["ln_b"] for l in range(DEPTH)])
    grads["ln_ffn_g"] = stack([per_layer[l][1]["ln_g"] for l in range(DEPTH)])
    grads["ln_ffn_b"] = stack([per_layer[l][1]["ln_b"] for l in range(DEPTH)])
    grads["ffn_w_in"] = stack([per_layer[l][1]["w_in"] for l in range(DEPTH)])
    grads["ffn_w_out"] = stack([per_layer[l][1]["w_out"] for l in range(DEPTH)])
    s5_layers = [l for l in range(DEPTH) if LAYER_MIXER[l] == 0]
    for k in ("w_in", "w_glu", "w_out") + _S5_VECTORS:
        grads["s5_" + k] = stack([per_layer[l][0][k] for l in s5_layers])
    d_ssm = s5_mats_vjp({k: jnp.stack([s5_dmats[s][k] for s in range(len(s5_layers))]) for k in s5_mats})
    for k, v in zip(_S5_SSM, d_ssm):
        grads["s5_" + k] = v
    hg = per_layer[1][0]
    grads["hg_w_in"], grads["hg_w_out"] = hg["w_in"][None], hg["w_out"][None]
    grads["hg_lower_bound"], grads["hg_norm_g"] = hg["lower_bound"], hg["norm_g"][None]
    for k, v in per_layer[2][0].items():
        if not k.startswith("ln_"):
            grads["mla_" + k] = v[None]
    return loss, dh, grads


def _here():
    return lax.axis_index("x"), lax.axis_index("y"), lax.axis_index("c")


def _any_spec():
    return pl.BlockSpec(memory_space=pl.ANY)


def _chip_exchange(name, xs, scatter):
    n = len(xs)

    def body(*refs):
        ins, outs = refs[:n], refs[n:2 * n]
        send_sems, recv_sems, loc_sems = refs[2 * n:]
        x, y, c = _here()
        me = 2 * x + y
        peers = [(1 - x, y), (x, 1 - y), (1 - x, 1 - y)]
        copies = []
        for t in range(n):
            src_of = (lambda p, t=t: ins[t].at[p]) if scatter else (lambda p, t=t: ins[t])
            loc = pltpu.make_async_copy(src_of(me), outs[t].at[me], loc_sems.at[t])
            loc.start()
            copies.append(loc)
            for j, (px, py) in enumerate(peers):
                cp = pltpu.make_async_remote_copy(
                    src_ref=src_of(2 * px + py), dst_ref=outs[t].at[me], send_sem=send_sems.at[t, j],
                    recv_sem=recv_sems.at[t, j], device_id=(px, py, c), device_id_type=MESH)
                cp.start()
                copies.append(cp)
        for cp in copies:
            cp.wait()

    out_shape = [jax.ShapeDtypeStruct(a.shape if scatter else (N_CHIPS,) + a.shape, a.dtype) for a in xs]
    return pl.pallas_call(
        body, name=name, in_specs=[_any_spec()] * n, out_specs=[_any_spec()] * n, out_shape=out_shape,
        scratch_shapes=[pltpu.SemaphoreType.DMA((n, 3)), pltpu.SemaphoreType.DMA((n, 3)), pltpu.SemaphoreType.DMA((n,))],
    )(*xs)


def _core_exchange(name, a, n_chunks=8):
    rows = a.shape[0] // n_chunks
    assert rows * n_chunks == a.shape[0] and rows % 8 == 0, (a.shape, n_chunks)

    def body(a_ref, o_ref, send_sems, recv_sems, loc_sem):
        x, y, c = _here()
        loc = pltpu.make_async_copy(a_ref, o_ref.at[c], loc_sem)
        loc.start()
        copies = [loc]
        for i in range(n_chunks):
            part = pl.ds(i * rows, rows)
            cp = pltpu.make_async_remote_copy(
                src_ref=a_ref.at[part], dst_ref=o_ref.at[c, part], send_sem=send_sems.at[i], recv_sem=recv_sems.at[i],
                device_id=(x, y, 1 - c), device_id_type=MESH)
            cp.start()
            copies.append(cp)
        for cp in copies:
            cp.wait()

    return pl.pallas_call(
        body, name=name, in_specs=[_any_spec()], out_specs=_any_spec(),
        out_shape=jax.ShapeDtypeStruct((2,) + a.shape, a.dtype),
        scratch_shapes=[pltpu.SemaphoreType.DMA((n_chunks,)), pltpu.SemaphoreType.DMA((n_chunks,)), pltpu.SemaphoreType.DMA],
    )(a)


def _all_exchange(name, a):
    def body(a_ref, o_ref, send_sems, recv_sems, loc_sem):
        x, y, c = _here()
        me = 4 * x + 2 * y + c
        loc = pltpu.make_async_copy(a_ref, o_ref.at[me], loc_sem)
        loc.start()
        copies = [loc]
        for mask in range(1, N_DEV):
            fx, fy, fc = (mask >> 2) & 1, (mask >> 1) & 1, mask & 1
            peer = (1 - x if fx else x, 1 - y if fy else y, 1 - c if fc else c)
            cp = pltpu.make_async_remote_copy(src_ref=a_ref, dst_ref=o_ref.at[me], send_sem=send_sems.at[mask - 1],
                                              recv_sem=recv_sems.at[mask - 1], device_id=peer, device_id_type=MESH)
            cp.start()
            copies.append(cp)
        for cp in copies:
            cp.wait()

    return pl.pallas_call(
        body, name=name, in_specs=[_any_spec()], out_specs=_any_spec(),
        out_shape=jax.ShapeDtypeStruct((N_DEV,) + a.shape, a.dtype),
        scratch_shapes=[pltpu.SemaphoreType.DMA((N_DEV - 1,)), pltpu.SemaphoreType.DMA((N_DEV - 1,)), pltpu.SemaphoreType.DMA],
    )(a)


def _sum_leading(name, a, out_dtype=F32):
    n, R, C = a.shape
    tr = _tile(R, 512, 16)

    def fn(a_t):
        s = a_t[0].astype(F32)
        for i in range(1, n):
            s = s + a_t[i].astype(F32)
        return [s]

    return _grid_call(name, (R // tr,), [(a, (n, tr, C), lambda i: (0, i, 0))],
                      [((R, C), out_dtype, (tr, C), lambda i: (i, 0))], fn)[0]


def _adamw(name, g_parts, w, m, v):
    R, C = w.shape
    tr = _tile(R, 256, 8)
    ng = len(g_parts)
    c1 = 1.0 / (1.0 - ADAM_B1 ** ADAM_STEP)
    c2 = 1.0 / (1.0 - ADAM_B2 ** ADAM_STEP)

    def fn(*tiles):
        g = tiles[0].astype(F32)
        for t in tiles[1:ng]:
            g = g + t.astype(F32)
        w_t, m_t, v_t = tiles[ng:]
        m_n = ADAM_B1 * m_t + (1.0 - ADAM_B1) * g
        v_n = ADAM_B2 * v_t + (1.0 - ADAM_B2) * (g * g)
        delta = -ADAM_LR * ((m_n * c1) / (jnp.sqrt(v_n * c2) + ADAM_EPS) + ADAM_WD * w_t)
        return [g, delta, m_n, v_n]

    spec = lambda a: (a, (tr, C), lambda i: (i, 0))
    return _grid_call(name, (R // tr,), [spec(a) for a in list(g_parts) + [w, m, v]],
                      [((R, C), F32, (tr, C), lambda i: (i, 0))] * 4, fn)


_WEIGHTS = ("ln_mix_g", "ln_mix_b", "ln_ffn_g", "ln_ffn_b", "ffn_w_in", "ffn_w_out", "s5_w_in", "s5_lam_re", "s5_lam_im",
            "s5_log_step", "s5_b_re", "s5_b_im", "s5_c_re", "s5_c_im", "s5_d", "s5_w_glu", "s5_b_glu", "s5_w_out", "hg_w_in",
            "hg_lower_bound", "hg_norm_g", "hg_w_out", "mla_w_in", "mla_q_norm_g", "mla_w_q_b", "mla_kv_norm_g", "mla_w_kv_b",
            "mla_w_out")
_BIG = {"ffn_w_in": 2, "ffn_w_out": 1, "s5_w_in": 1, "s5_w_glu": 1, "s5_w_out": 1, "hg_w_in": 2, "hg_w_out": 1,
        "mla_w_in": 1, "mla_w_q_b": 2, "mla_w_kv_b": 2, "mla_w_out": 1}
_SMALL_SHARDED = {"s5_d": 1, "s5_b_glu": 1, "mla_q_norm_g": 1, "mla_kv_norm_g": 1}
_REPLICATED = tuple(n for n in _WEIGHTS if n not in _BIG and n not in _SMALL_SHARDED)
LANES = 1024


def _pack(arrs, dtype, row_mult, lead=0):
    rows, segs, r = [], [], 0
    for a in arrs:
        lead_shape = a.shape[:lead]
        flat = a.astype(dtype).reshape(lead_shape + (-1,))
        n = -(-flat.shape[-1] // LANES)
        flat = jnp.pad(flat, [(0, 0)] * lead + [(0, n * LANES - flat.shape[-1])])
        rows.append(flat.reshape(lead_shape + (n, LANES)))
        segs.append((r, n))
        r += n
    pad = -r % row_mult
    if pad:
        rows.append(jnp.zeros(rows[0].shape[:lead] + (pad, LANES), dtype))
    return jnp.concatenate(rows, axis=lead), segs


def _unpack(packed, segs, shapes):
    out = []
    for (r0, n), shp in zip(segs, shapes):
        size = int(np.prod(shp))
        out.append(packed[..., r0:r0 + n, :].reshape(packed.shape[:-2] + (n * LANES,))[..., :size].reshape(packed.shape[:-2] + tuple(shp)))
    return out


def _unshard(stacked, axis):
    moved = jnp.moveaxis(stacked, 0, axis)
    shp = list(moved.shape)
    return moved.reshape(shp[:axis] + [shp[axis] * shp[axis + 1]] + shp[axis + 2:])


def _shard_split(full, axis):
    shp = list(full.shape)
    a = full.reshape(shp[:axis] + [N_CHIPS, shp[axis] // N_CHIPS] + shp[axis + 1:])
    return jnp.moveaxis(a, axis, 0)


def _train_step(x, positions, target, w, m, v):
    big, small_sh = list(_BIG), list(_SMALL_SHARDED)
    chip = 2 * lax.axis_index("x") + lax.axis_index("y")

    big_pack, big_segs = _pack([w[n] for n in big], BF16, 16)
    sm_pack, sm_segs = _pack([w[n] for n in small_sh], F32, 8)
    big_all, sm_all = _chip_exchange("gather_weights", [big_pack, sm_pack], scatter=False)
    W = {n: w[n] for n in _REPLICATED}
    for n, s in zip(big, _unpack(big_all, big_segs, [w[n].shape for n in big])):
        W[n] = _unshard(s, _BIG[n])
    for n, s in zip(small_sh, _unpack(sm_all, sm_segs, [w[n].shape for n in small_sh])):
        W[n] = _unshard(s, _SMALL_SHARDED[n])

    loss_local, grad_x, G = _local_step(x, positions, target, W)
    loss = lax.psum(loss_local, ("x", "y", "c"))
    out = {}

    g_pack, _ = _pack([_shard_split(G[n].astype(BF16), _BIG[n]) for n in big], BF16, 16, lead=1)
    recv = _chip_exchange("scatter_grads", [g_pack], scatter=True)[0]
    pair = _core_exchange("swap_core_sums", _sum_leading("sum_chips", recv, BF16), n_chunks=1)
    for n, (r0, nr) in zip(big, big_segs):
        as_rows = lambda a: a.reshape(nr, LANES)
        res = _adamw("adamw_" + n, [pair[0, r0:r0 + nr], pair[1, r0:r0 + nr]], as_rows(w[n]), as_rows(m[n]), as_rows(v[n]))
        out[n] = tuple(r.reshape(w[n].shape) for r in res)

    small = list(_REPLICATED) + small_sh
    s_pack, s_segs = _pack([G[n] for n in small], F32, 16)
    total = _sum_leading("sum_small", _all_exchange("gather_small_grads", s_pack))
    g_small = dict(zip(small, _unpack(total, s_segs, [G[n].shape for n in small])))
    for n in small_sh:
        width = w[n].shape[1]
        g_small[n] = lax.dynamic_slice_in_dim(g_small[n], chip * width, width, axis=1)
    packs = [_pack([d[n] for n in small], F32, 8)[0] for d in (g_small, w, m, v)]
    _, a_segs = _pack([w[n] for n in small], F32, 8)
    res = _adamw("adamw_small", [packs[0]], packs[1], packs[2], packs[3])
    unpacked = [_unpack(r, a_segs, [w[n].shape for n in small]) for r in res]
    for i, n in enumerate(small):
        out[n] = tuple(u[i] for u in unpacked)
    return loss, grad_x, out


def kernel(x, positions, ln_mix_g, ln_mix_b, ln_ffn_g, ln_ffn_b, ffn_w_in, ffn_w_out, s5_w_in, s5_lam_re, s5_lam_im,
           s5_log_step, s5_b_re, s5_b_im, s5_c_re, s5_c_im, s5_d, s5_w_glu, s5_b_glu, s5_w_out, hg_w_in,
           hg_lower_bound, hg_norm_g, hg_w_out, mla_w_in, mla_q_norm_g, mla_w_q_b, mla_kv_norm_g, mla_w_kv_b,
           mla_w_out, loss_target, m_ln_mix_g, m_ln_mix_b, m_ln_ffn_g, m_ln_ffn_b, m_ffn_w_in, m_ffn_w_out, m_s5_w_in,
           m_s5_lam_re, m_s5_lam_im, m_s5_log_step, m_s5_b_re, m_s5_b_im, m_s5_c_re, m_s5_c_im, m_s5_d, m_s5_w_glu,
           m_s5_b_glu, m_s5_w_out, m_hg_w_in, m_hg_lower_bound, m_hg_norm_g, m_hg_w_out, m_mla_w_in, m_mla_q_norm_g,
           m_mla_w_q_b, m_mla_kv_norm_g, m_mla_w_kv_b, m_mla_w_out, v_ln_mix_g, v_ln_mix_b, v_ln_ffn_g, v_ln_ffn_b,
           v_ffn_w_in, v_ffn_w_out, v_s5_w_in, v_s5_lam_re, v_s5_lam_im, v_s5_log_step, v_s5_b_re, v_s5_b_im,
           v_s5_c_re, v_s5_c_im, v_s5_d, v_s5_w_glu, v_s5_b_glu, v_s5_w_out, v_hg_w_in, v_hg_lower_bound, v_hg_norm_g,
           v_hg_w_out, v_mla_w_in, v_mla_q_norm_g, v_mla_w_q_b, v_mla_kv_norm_g, v_mla_w_kv_b, v_mla_w_out):
    args = (ln_mix_g, ln_mix_b, ln_ffn_g, ln_ffn_b, ffn_w_in, ffn_w_out, s5_w_in, s5_lam_re, s5_lam_im,
            s5_log_step, s5_b_re, s5_b_im, s5_c_re, s5_c_im, s5_d, s5_w_glu, s5_b_glu, s5_w_out, hg_w_in,
            hg_lower_bound, hg_norm_g, hg_w_out, mla_w_in, mla_q_norm_g, mla_w_q_b, mla_kv_norm_g, mla_w_kv_b,
            mla_w_out, m_ln_mix_g, m_ln_mix_b, m_ln_ffn_g, m_ln_ffn_b, m_ffn_w_in, m_ffn_w_out,
            m_s5_w_in, m_s5_lam_re, m_s5_lam_im, m_s5_log_step, m_s5_b_re, m_s5_b_im, m_s5_c_re, m_s5_c_im, m_s5_d,
            m_s5_w_glu, m_s5_b_glu, m_s5_w_out, m_hg_w_in, m_hg_lower_bound, m_hg_norm_g, m_hg_w_out, m_mla_w_in,
            m_mla_q_norm_g, m_mla_w_q_b, m_mla_kv_norm_g, m_mla_w_kv_b, m_mla_w_out, v_ln_mix_g, v_ln_mix_b,
            v_ln_ffn_g, v_ln_ffn_b, v_ffn_w_in, v_ffn_w_out, v_s5_w_in, v_s5_lam_re, v_s5_lam_im, v_s5_log_step,
            v_s5_b_re, v_s5_b_im, v_s5_c_re, v_s5_c_im, v_s5_d, v_s5_w_glu, v_s5_b_glu, v_s5_w_out, v_hg_w_in,
            v_hg_lower_bound, v_hg_norm_g, v_hg_w_out, v_mla_w_in, v_mla_q_norm_g, v_mla_w_q_b, v_mla_kv_norm_g,
            v_mla_w_kv_b, v_mla_w_out)
    nw = len(_WEIGHTS)
    w = dict(zip(_WEIGHTS, args[:nw]))
    m = dict(zip(_WEIGHTS, args[nw:2 * nw]))
    v = dict(zip(_WEIGHTS, args[2 * nw:]))
    loss, grad_x, out = _train_step(x[0], positions[0], loss_target[0], w, m, v)
    res = [loss, grad_x[None]]
    for i in range(4):
        res += [out[n][i] for n in _WEIGHTS]
    return tuple(res)
```

```python
import functools
import math

import numpy as np
import jax
import jax.numpy as jnp
from jax import lax
from jax.experimental import pallas as pl
from jax.experimental.pallas import tpu as pltpu

F32 = jnp.float32
BF16 = jnp.bfloat16

D_MODEL = 1024
DEPTH = 4
LAYER_MIXER = (0, 1, 2, 0)
S5_GROUP = 16
S5_GROUPS = 64
S5_STATE = 64
S5_CHUNK = 16
HG_HEADS = 8
HG_DIM = 128
HG_CHUNK = 128
MLA_HEADS = 8
MLA_NOPE = 128
MLA_ROPE = 64
MLA_V = 128
MLA_Q_LORA = 384
MLA_KV_LORA = 256
ROPE_THETA = 10000.0
FFN_HIDDEN = 2816
ALPHA = (2 * DEPTH) ** 0.25
LN_EPS = 1e-5
RMS_EPS = 1e-6
ADAM_LR, ADAM_B1, ADAM_B2, ADAM_EPS, ADAM_WD, ADAM_STEP = 0.001, 0.9, 0.999, 1e-08, 0.01, 10
VMEM_LIMIT_BYTES = 56 * 1024 * 1024
MESH = pl.DeviceIdType.MESH
N_CHIPS = 4
N_DEV = 8


def _cparams():
    return pltpu.CompilerParams(vmem_limit_bytes=VMEM_LIMIT_BYTES)


def _tile(n, want, mult):
    t = min(want, n)
    t -= t % mult
    while t >= mult:
        if n % t == 0:
            return t
        t -= mult
    return n


def _sigmoid(x):
    return 1.0 / (1.0 + jnp.exp(-x))


def _silu(x):
    return x * _sigmoid(x)


def _dsilu(x):
    s = _sigmoid(x)
    return s * (1.0 + x * (1.0 - s))


_GELU_C = math.sqrt(2.0 / math.pi)


def _gelu(x):
    return 0.5 * x * (1.0 + jnp.tanh(_GELU_C * (x + 0.044715 * x * x * x)))


def _dgelu(x):
    t = jnp.tanh(_GELU_C * (x + 0.044715 * x * x * x))
    return 0.5 * (1.0 + t) + 0.5 * x * (1.0 - t * t) * _GELU_C * (1.0 + 3 * 0.044715 * x * x)


def _layer_norm(z, g, b):
    mu = jnp.mean(z, axis=-1, keepdims=True)
    zc = z - mu
    var = jnp.mean(zc * zc, axis=-1, keepdims=True)
    return zc * lax.rsqrt(var + LN_EPS) * g + b


def _bcast8(row):
    return jnp.broadcast_to(row, (8, row.shape[-1]))


def _stack_rows8(rows):
    n = rows[0].shape[-1]
    idx = lax.broadcasted_iota(jnp.int32, (8, n), 0)
    out = jnp.zeros((8, n), F32)
    for i, r in enumerate(rows):
        out = jnp.where(idx == i, _bcast8(r), out)
    return out


def _psum_rows(a):
    return a.reshape(-1, 8, a.shape[-1])[:, 0, :].sum(axis=0)


_DN = {"nn": ((1,), (0,)), "nt": ((1,), (1,)), "tn": ((0,), (0,))}


def _mm(name, grid, a_defs, b_defs, pairs, n_acc, acc_shape, extra_defs, out_defs, epilogue):
    nk = grid[2]
    na, nb, ne, no = len(a_defs), len(b_defs), len(extra_defs), len(out_defs)

    def body(*refs):
        a_refs = refs[:na]
        b_refs = refs[na:na + nb]
        e_refs = refs[na + nb:na + nb + ne]
        o_refs = refs[na + nb + ne:na + nb + ne + no]
        acc = refs[-1]
        k = pl.program_id(2)

        @pl.when(k == 0)
        def _():
            acc[...] = jnp.zeros(acc.shape, F32)

        for (ai, bi, ci, mode) in pairs:
            a = a_refs[ai][...].astype(BF16)
            b = b_refs[bi][...].astype(BF16)
            acc[ci] += lax.dot_general(a, b, (_DN[mode], ((), ())), preferred_element_type=F32)

        @pl.when(k == nk - 1)
        def _():
            outs = epilogue([acc[c] for c in range(n_acc)], *[e[...] for e in e_refs])
            for o_ref, o in zip(o_refs, outs):
                o_ref[...] = o.astype(o_ref.dtype)

    in_specs = [pl.BlockSpec(d[1], d[2]) for d in list(a_defs) + list(b_defs) + list(extra_defs)]
    out_specs = [pl.BlockSpec(d[2], d[3]) for d in out_defs]
    out_shape = [jax.ShapeDtypeStruct(d[0], d[1]) for d in out_defs]
    res = pl.pallas_call(
        body, name=name, grid=grid, in_specs=in_specs, out_specs=out_specs, out_shape=out_shape,
        scratch_shapes=[pltpu.VMEM((n_acc,) + tuple(acc_shape), F32)], compiler_params=_cparams(),
    )(*[d[0] for d in list(a_defs) + list(b_defs) + list(extra_defs)])
    return res


def _mm_simple(name, a, b, mode, out_dtype, tm=512, tn=1024, tk=1024, extras=(), epilogue=None, n_out=1,
               out_dtypes=None, psum_outs=0, j_outer=False):
    if mode == "nn":
        (M, K), (K2, N) = a.shape, b.shape
    elif mode == "nt":
        (M, K), (N, K2) = a.shape, b.shape
    else:
        (K, M), (K2, N) = a.shape, b.shape
    assert K == K2, (name, a.shape, b.shape, mode)
    tm, tn, tk = _tile(M, tm, 8), _tile(N, tn, 128), _tile(K, tk, 128)
    grid = (M // tm, N // tn, K // tk)
    if mode == "nn":
        a_def = (a, (tm, tk), lambda i, j, k: (i, k))
        b_def = (b, (tk, tn), lambda i, j, k: (k, j))
    elif mode == "nt":
        a_def = (a, (tm, tk), lambda i, j, k: (i, k))
        b_def = (b, (tn, tk), lambda i, j, k: (j, k))
    else:
        a_def = (a, (tk, tm), lambda i, j, k: (k, i))
        b_def = (b, (tk, tn), lambda i, j, k: (k, j))
    extra_defs = []
    for e in extras:
        if e.shape[0] == 1:
            extra_defs.append((e, (1, tn), lambda i, j, k: (0, j)))
        else:
            extra_defs.append((e, (tm, tn), lambda i, j, k: (i, j)))
    out_dtypes = out_dtypes or [out_dtype] * n_out
    out_defs = [((M, N), dt, (tm, tn), lambda i, j, k: (i, j)) for dt in out_dtypes]
    out_defs += [((M // tm * 8, N), F32, (8, tn), lambda i, j, k: (i, j)) for _ in range(psum_outs)]
    if epilogue is None:
        epilogue = lambda accs: [accs[0]]
    a_defs, b_defs = [a_def], [b_def]
    if j_outer:
        swap = lambda d: d[:-1] + ((lambda f: lambda j, i, k: f(i, j, k))(d[-1]),)
        grid = (grid[1], grid[0], grid[2])
        a_defs, b_defs = [swap(a_def)], [swap(b_def)]
        extra_defs, out_defs = [swap(d) for d in extra_defs], [swap(d) for d in out_defs]
    res = _mm(name, grid, a_defs, b_defs, [(0, 0, 0, mode)], 1, (tm, tn), extra_defs, out_defs, epilogue)
    return res[0] if len(res) == 1 else res


def _rows(name, T, tm, in_defs, out_defs, fn):
    ni = len(in_defs)

    def body(*refs):
        outs = fn(*[r[...] for r in refs[:ni]])
        for o_ref, o in zip(refs[ni:], outs):
            o_ref[...] = o.astype(o_ref.dtype)

    res = pl.pallas_call(
        body, name=name, grid=(T // tm,),
        in_specs=[pl.BlockSpec(d[1], d[2]) for d in in_defs],
        out_specs=[pl.BlockSpec(d[2], d[3]) for d in out_defs],
        out_shape=[jax.ShapeDtypeStruct(d[0], d[1]) for d in out_defs],
        compiler_params=_cparams(),
    )(*[d[0] for d in in_defs])
    return res


def _grid_call(name, grid, in_defs, out_defs, fn, acc_outs=()):
    ni = len(in_defs)

    def body(*refs):
        outs = fn(*[r[...] for r in refs[:ni]])
        first = pl.program_id(len(grid) - 1) == 0
        for idx, (o_ref, o) in enumerate(zip(refs[ni:], outs)):
            if idx in acc_outs:
                @pl.when(first)
                def _(o_ref=o_ref, o=o):
                    o_ref[...] = o.astype(o_ref.dtype)

                @pl.when(jnp.logical_not(first))
                def _(o_ref=o_ref, o=o):
                    o_ref[...] += o.astype(o_ref.dtype)
            else:
                o_ref[...] = o.astype(o_ref.dtype)

    return pl.pallas_call(
        body, name=name, grid=grid,
        in_specs=[pl.BlockSpec(d[1], d[2]) for d in in_defs],
        out_specs=[pl.BlockSpec(d[2], d[3]) for d in out_defs],
        out_shape=[jax.ShapeDtypeStruct(d[0], d[1]) for d in out_defs],
        compiler_params=_cparams(),
    )(*[d[0] for d in in_defs])


def _rt(a, tm):
    return (a, (tm, a.shape[1]), lambda i: (i, 0))


def _full(a):
    return (a, a.shape, lambda i: (0,) * a.ndim)


def _rt_out(T, n, dt, tm):
    return ((T, n), dt, (tm, n), lambda i: (i, 0))


def _ps_out(T, n, tm):
    return ((T // tm * 8, n), F32, (8, n), lambda i: (i, 0))


def _out_proj_ln(name, a, w, h_in, g, b):
    def epi(accs, h_t, g_t, b_t):
        z = ALPHA * h_t + accs[0]
        return [z, _layer_norm(z, g_t, b_t)]
    return _mm_simple(name, a, w, "nn", F32, tm=512, tn=D_MODEL, tk=FFN_HIDDEN, extras=(h_in, g, b), epilogue=epi, n_out=2)


def _ln_bwd(name, dh, z, g):
    T = dh.shape[0]
    tm = _tile(T, 512, 8)

    def fn(dh_t, z_t, g_t):
        mu = jnp.mean(z_t, axis=-1, keepdims=True)
        zc = z_t - mu
        var = jnp.mean(zc * zc, axis=-1, keepdims=True)
        rstd = lax.rsqrt(var + LN_EPS)
        xhat = zc * rstd
        dxh = dh_t * g_t
        m1 = jnp.mean(dxh, axis=-1, keepdims=True)
        m2 = jnp.mean(dxh * xhat, axis=-1, keepdims=True)
        dz = rstd * (dxh - m1 - xhat * m2)
        return [dz, _bcast8(jnp.sum(dh_t * xhat, axis=0, keepdims=True)), _bcast8(jnp.sum(dh_t, axis=0, keepdims=True))]

    dz, pg, pb = _rows(name, T, tm, [_rt(dh, tm), _rt(z, tm), _full(g)],
                       [_rt_out(T, D_MODEL, F32, tm), _ps_out(T, D_MODEL, tm), _ps_out(T, D_MODEL, tm)], fn)
    return dz, _psum_rows(pg), _psum_rows(pb)


def _ffn_fwd(l, h, w_in, w_out, g, b):
    T = h.shape[0]
    tm, tn = _tile(T, 512, 8), 1408
    nj = FFN_HIDDEN // tn
    grid = (nj, T // tm, 1)

    def epi(accs):
        gg, uu = accs
        return [gg, uu, _silu(gg) * uu]

    G, U, A = _mm(
        f"ffn{l}_in", grid, [(h, (tm, D_MODEL), lambda j, i, k: (i, 0))],
        [(w_in, (D_MODEL, tn), lambda j, i, k: (0, j)), (w_in, (D_MODEL, tn), lambda j, i, k: (0, j + nj))],
        [(0, 0, 0, "nn"), (0, 1, 1, "nn")], 2, (tm, tn), [],
        [((T, FFN_HIDDEN), F32, (tm, tn), lambda j, i, k: (i, j)),
         ((T, FFN_HIDDEN), F32, (tm, tn), lambda j, i, k: (i, j)),
         ((T, FFN_HIDDEN), BF16, (tm, tn), lambda j, i, k: (i, j))], epi)
    z, h_out = _out_proj_ln(f"ffn{l}_out", A, w_out, h, g, b)
    return h_out, (h, G, U, A, z)


def _ffn_bwd(l, dh_out, saved, w_in, w_out, g):
    h, G, U, A, z = saved
    T = h.shape[0]
    dz, dg, db = _ln_bwd(f"ffn{l}_lnb", dh_out, z, g)

    def epi(accs, g_t, u_t):
        da = accs[0]
        return [da * u_t * _dsilu(g_t), da * _silu(g_t)]

    dG, dU = _mm_simple(f"ffn{l}_dA", dz, w_out, "nt", BF16, tm=512, tn=1408, tk=1024, extras=(G, U), epilogue=epi, n_out=2,
                        j_outer=True)
    dw_out = _mm_simple(f"ffn{l}_dWout", A, dz, "tn", BF16, tm=1408, tn=1024, tk=512)
    dw_g = _mm_simple(f"ffn{l}_dWg", h, dG, "tn", BF16, tm=1024, tn=1408, tk=512)
    dw_u = _mm_simple(f"ffn{l}_dWu", h, dU, "tn", BF16, tm=1024, tn=1408, tk=512)
    tm, tk = _tile(T, 256, 8), FFN_HIDDEN
    dh = _mm(
        f"ffn{l}_dX", (T // tm, 1, 1),
        [(dG, (tm, tk), lambda i, j, k: (i, 0)), (dU, (tm, tk), lambda i, j, k: (i, 0))],
        [(w_in, (D_MODEL, tk), lambda i, j, k: (0, 0)), (w_in, (D_MODEL, tk), lambda i, j, k: (0, 1))],
        [(0, 0, 0, "nt"), (1, 1, 0, "nt")], 1, (tm, D_MODEL),
        [(dz, (tm, D_MODEL), lambda i, j, k: (i, 0))],
        [((T, D_MODEL), F32, (tm, D_MODEL), lambda i, j, k: (i, 0))],
        lambda accs, dz_t: [accs[0] + ALPHA * dz_t])[0]
    return dh, dict(w_in=jnp.concatenate([dw_g, dw_u], axis=1), w_out=dw_out, ln_g=dg, ln_b=db)


def _s5_matrices(lam_re, lam_im, log_step, b_re, b_im, c_re, c_im):
    L, hp = S5_CHUNK, lax.Precision.HIGHEST
    out = {}
    mt_total = 0.0
    for d in range(2):
        lam = lax.complex(lam_re[d], lam_im[d])
        step = jnp.exp(log_step[d])[:, None]
        lam_dt = lam * step
        lam_bar = jnp.exp(lam_dt)
        b_bar = ((lam_bar - 1.0) / lam)[..., None] * lax.complex(b_re[d], b_im[d])
        c = lax.complex(c_re[d], c_im[d])
        pw = jnp.exp(lam_dt[None] * jnp.arange(L + 1, dtype=F32)[:, None, None])
        kj = jnp.einsum("ghp,jgp,gpk->gjhk", c, pw[:L], b_bar, precision=hp).real
        lag = np.arange(L)[None, :] - np.arange(L)[:, None]
        lag = lag if d == 0 else -lag
        sel = np.stack([(lag == j) for j in range(L)]).astype(np.float32)
        mt = jnp.einsum("jst,gjab->gsbta", sel, kj, precision=hp).reshape(S5_GROUPS, 16 * L, 16 * L)
        mt_total = mt_total + mt
        pw_dist = jnp.flip(pw[:L], 0) if d == 0 else pw[:L]
        pc = pw_dist.transpose(1, 0, 2)[:, :, None, :] * b_bar.transpose(0, 2, 1)[:, None, :, :]
        pm = jnp.concatenate([pc.real, pc.imag], axis=-1).reshape(S5_GROUPS, 16 * L, 2 * S5_STATE)
        pw_read = pw[1:] if d == 0 else jnp.flip(pw[1:], 0)
        qc = c[:, None, :, :] * pw_read.transpose(1, 0, 2)[:, :, None, :]
        qm = jnp.concatenate([qc.real, -qc.imag], axis=-1).reshape(S5_GROUPS, 16 * L, 2 * S5_STATE).transpose(0, 2, 1)
        a = pw[L]
        out[f"p{d}"], out[f"q{d}"] = pm, qm
        out[f"a{d}"] = jnp.concatenate([a.real, a.imag], axis=-1)
    out["mt"] = mt_total
    return out


def _s5_apow(lam_re, lam_im, log_step, n_steps, conj):
    lam_dt = lax.complex(lam_re, lam_im) * jnp.exp(log_step)[..., None]
    k = (S5_CHUNK * 2.0 ** jnp.arange(n_steps, dtype=F32))[None, None, :, None]
    a = jnp.exp(lam_dt[:, :, None, :] * k)
    re, im = a.real, (-a.imag if conj else a.imag)
    return jnp.stack([jnp.concatenate([re, re], -1), jnp.concatenate([-im, im], -1)], axis=3)


def _shift_rows(x, s, down):
    n = x.shape[0]
    if s >= n:
        return jnp.zeros_like(x)
    if s % 8 == 0:
        z = jnp.zeros((s, x.shape[1]), x.dtype)
        return jnp.concatenate([z, x[:n - s]], axis=0) if down else jnp.concatenate([x[s:], z], axis=0)
    row = lax.broadcasted_iota(jnp.int32, x.shape, 0)
    if down:
        return jnp.where(row >= s, pltpu.roll(x, s, 0), 0.0)
    return jnp.where(row < n - s, pltpu.roll(x, n - s, 0), 0.0)


def _cmul(x, a1, a2):
    return x * a1 + pltpu.roll(x, S5_STATE, 1) * a2


def _chunk_scan(s, apow_ref, down):
    n = s.shape[0]
    k, sh = 0, 1
    while sh < n:
        s = s + _cmul(_shift_rows(s, sh, down), apow_ref[k, 0:1, :], apow_ref[k, 1:2, :])
        k, sh = k + 1, sh * 2
    return s


def _s5_scan_fwd(name, ug, mats, apow):
    G, C, W = ug.shape
    n_steps = apow.shape[2]

    def body(u_ref, mt_ref, p0_ref, p1_ref, q0_ref, q1_ref, ap0_ref, ap1_ref, y_ref, h0_ref, h1_ref):
        u = u_ref[...]
        s0 = jnp.dot(u, p0_ref[...], preferred_element_type=F32)
        s1 = jnp.dot(u, p1_ref[...], preferred_element_type=F32)
        h0 = _shift_rows(_chunk_scan(s0, ap0_ref, True), 1, True)
        h1 = _shift_rows(_chunk_scan(s1, ap1_ref, False), 1, False)
        y = jnp.dot(u, mt_ref[...], preferred_element_type=F32)
        y += jnp.dot(h0.astype(BF16), q0_ref[...], preferred_element_type=F32)
        y += jnp.dot(h1.astype(BF16), q1_ref[...], preferred_element_type=F32)
        y_ref[...] = y.astype(y_ref.dtype)
        h0_ref[...] = h0
        h1_ref[...] = h1

    def gspec(shape):
        return pl.BlockSpec((None,) + shape, lambda g: (g,) + (0,) * len(shape))

    ap0, ap1 = apow[0], apow[1]
    return pl.pallas_call(
        body, name=name, grid=(G,),
        in_specs=[gspec((C, W)), gspec((W, W)), gspec((W, 128)), gspec((W, 128)), gspec((128, W)), gspec((128, W)),
                  gspec((n_steps, 2, 128)), gspec((n_steps, 2, 128))],
        out_specs=[gspec((C, W)), gspec((C, 128)), gspec((C, 128))],
        out_shape=[jax.ShapeDtypeStruct((G, C, W), BF16), jax.ShapeDtypeStruct((G, C, 128), F32),
                   jax.ShapeDtypeStruct((G, C, 128), F32)],
        compiler_params=_cparams(),
    )(ug, mats["mt"].astype(BF16), mats["p0"].astype(BF16), mats["p1"].astype(BF16),
      mats["q0"].astype(BF16), mats["q1"].astype(BF16), ap0, ap1)


def _s5_scan_bwd(name, dyg, ug, h0, h1, mats, apow_conj):
    G, C, W = ug.shape
    n_steps = apow_conj.shape[2]

    def body(dy_ref, u_ref, h0_ref, h1_ref, mt_ref, p0_ref, p1_ref, q0_ref, q1_ref, ap0_ref, ap1_ref,
             du_ref, dmt_ref, dp0_ref, dp1_ref, dq0_ref, dq1_ref, da_ref):
        dy, u = dy_ref[...], u_ref[...]
        nt, tn = (_DN["nt"], ((), ())), (_DN["tn"], ((), ()))
        dh0 = lax.dot_general(dy, q0_ref[...], nt, preferred_element_type=F32)
        dh1 = lax.dot_general(dy, q1_ref[...], nt, preferred_element_type=F32)
        ds0 = _chunk_scan(_shift_rows(dh0, 1, False), ap0_ref, False)
        ds1 = _chunk_scan(_shift_rows(dh1, 1, True), ap1_ref, True)
        ds0b, ds1b = ds0.astype(BF16), ds1.astype(BF16)
        du = lax.dot_general(dy, mt_ref[...], nt, preferred_element_type=F32)
        du += lax.dot_general(ds0b, p0_ref[...], nt, preferred_element_type=F32)
        du += lax.dot_general(ds1b, p1_ref[...], nt, preferred_element_type=F32)
        du_ref[...] = du.astype(du_ref.dtype)
        dmt_ref[...] = lax.dot_general(u, dy, tn, preferred_element_type=F32)
        dp0_ref[...] = lax.dot_general(u, ds0b, tn, preferred_element_type=F32)
        dp1_ref[...] = lax.dot_general(u, ds1b, tn, preferred_element_type=F32)
        h0v, h1v = h0_ref[...], h1_ref[...]
        dq0_ref[...] = lax.dot_general(h0v.astype(BF16), dy, tn, preferred_element_type=F32)
        dq1_ref[...] = lax.dot_general(h1v.astype(BF16), dy, tn, preferred_element_type=F32)
        rows = [jnp.sum(ds0 * h0v, axis=0, keepdims=True), jnp.sum(ds0 * pltpu.roll(h0v, S5_STATE, 1), axis=0, keepdims=True),
                jnp.sum(ds1 * h1v, axis=0, keepdims=True), jnp.sum(ds1 * pltpu.roll(h1v, S5_STATE, 1), axis=0, keepdims=True)]
        da_ref[...] = _stack_rows8(rows)

    def gspec(shape):
        return pl.BlockSpec((None,) + shape, lambda g: (g,) + (0,) * len(shape))

    f32s = lambda *s: jax.ShapeDtypeStruct((G,) + s, F32)
    return pl.pallas_call(
        body, name=name, grid=(G,),
        in_specs=[gspec((C, W)), gspec((C, W)), gspec((C, 128)), gspec((C, 128)), gspec((W, W)), gspec((W, 128)),
                  gspec((W, 128)), gspec((128, W)), gspec((128, W)), gspec((n_steps, 2, 128)), gspec((n_steps, 2, 128))],
        out_specs=[gspec((C, W)), gspec((W, W)), gspec((W, 128)), gspec((W, 128)), gspec((128, W)), gspec((128, W)),
                   gspec((8, 128))],
        out_shape=[jax.ShapeDtypeStruct((G, C, W), BF16), f32s(W, W), f32s(W, 128), f32s(W, 128), f32s(128, W),
                   f32s(128, W), f32s(8, 128)],
        compiler_params=_cparams(),
    )(dyg, ug, h0, h1, mats["mt"].astype(BF16), mats["p0"].astype(BF16), mats["p1"].astype(BF16),
      mats["q0"].astype(BF16), mats["q1"].astype(BF16), apow_conj[0], apow_conj[1])


S5_LANE_GROUPS = 128 // S5_GROUP


def _group_select():
    e = np.zeros((S5_LANE_GROUPS, S5_CHUNK, 128, S5_CHUNK, S5_GROUP), np.float32)
    for j in range(S5_LANE_GROUPS):
        for l in range(S5_CHUNK):
            for hh in range(S5_GROUP):
                e[j, l, S5_GROUP * j + hh, l, hh] = 1.0
    return jnp.asarray(e.reshape(S5_LANE_GROUPS, S5_CHUNK * 128, S5_CHUNK * S5_GROUP), BF16)


def _to_groups(name, a):
    T = a.shape[0]
    C = T // S5_CHUNK
    cb = _tile(C, 256, 8)
    sel = _group_select()

    def body(x_ref, e_ref, o_ref):
        xcat = jnp.concatenate([x_ref[pl.ds(l, cb, stride=S5_CHUNK), :].astype(BF16) for l in range(S5_CHUNK)], axis=1)
        for j in range(S5_LANE_GROUPS):
            o_ref[j] = jnp.dot(xcat, e_ref[j], preferred_element_type=F32).astype(BF16)

    return pl.pallas_call(
        body, name=name, grid=(S5_GROUPS // S5_LANE_GROUPS, C // cb),
        in_specs=[pl.BlockSpec((cb * S5_CHUNK, 128), lambda b, r: (r, b)), pl.BlockSpec(sel.shape, lambda b, r: (0, 0, 0))],
        out_specs=pl.BlockSpec((S5_LANE_GROUPS, cb, S5_CHUNK * S5_GROUP), lambda b, r: (b, r, 0)),
        out_shape=jax.ShapeDtypeStruct((S5_GROUPS, C, S5_CHUNK * S5_GROUP), BF16), compiler_params=_cparams(),
    )(a, sel)


def _from_groups(name, a, extras, fn, n_out):
    G, C, W = a.shape
    T = C * S5_CHUNK
    cb = _tile(C, 256, 8)
    sel = _group_select()
    ne = len(extras)

    def body(*refs):
        y_ref, e_ref = refs[:2]
        e_refs, o_refs = refs[2:2 + ne], refs[2 + ne:]
        ycat = jnp.zeros((cb, S5_CHUNK * 128), F32)
        for j in range(S5_LANE_GROUPS):
            ycat = ycat + lax.dot_general(y_ref[j], e_ref[j], (_DN["nt"], ((), ())), preferred_element_type=F32)
        for l in range(S5_CHUNK):
            rows = pl.ds(l, cb, stride=S5_CHUNK)
            ex = [r[...] if r.shape[0] == 1 else r[rows, :] for r in e_refs]
            for o_ref, o in zip(o_refs, fn(ycat[:, l * 128:(l + 1) * 128], *ex)):
                o_ref[rows, :] = o

    col = pl.BlockSpec((cb * S5_CHUNK, 128), lambda b, r: (r, b))
    return pl.pallas_call(
        body, name=name, grid=(G // S5_LANE_GROUPS, C // cb),
        in_specs=[pl.BlockSpec((S5_LANE_GROUPS, cb, W), lambda b, r: (b, r, 0)),
                  pl.BlockSpec(sel.shape, lambda b, r: (0, 0, 0))]
        + [pl.BlockSpec((1, 128), lambda b, r: (0, b)) if e.shape[0] == 1 else col for e in extras],
        out_specs=[col] * n_out,
        out_shape=[jax.ShapeDtypeStruct((T, G * S5_GROUP), F32)] * n_out, compiler_params=_cparams(),
    )(a, sel, *extras)


_S5_SSM = ("lam_re", "lam_im", "log_step", "b_re", "b_im", "c_re", "c_im")


def _s5_prepare(W, T):
    sp = tuple(W["s5_" + k] for k in _S5_SSM)
    mats, mats_vjp = jax.vjp(jax.vmap(_s5_matrices), *sp)
    n_steps = max(1, int(math.log2(T // S5_CHUNK)))
    apow = jax.vmap(lambda a, b, c: _s5_apow(a, b, c, n_steps, False))(*sp[:3])
    apow_c = jax.vmap(lambda a, b, c: _s5_apow(a, b, c, n_steps, True))(*sp[:3])
    return mats, mats_vjp, apow, apow_c


def _s5_fwd(l, h, w, small, g, b, mats, apow, apow_c):
    u = _mm_simple(f"s5{l}_in", h, w["w_in"], "nn", F32)
    ug = _to_groups(f"s5{l}_togroups", u)
    yg, h0, h1 = _s5_scan_fwd(f"s5{l}_scan", ug, mats, apow)
    d_row, bglu_row = small["d"].reshape(1, -1), small["b_glu"].reshape(1, -1)
    yssm, y1 = _from_groups(f"s5{l}_fromgroups", yg, [u, d_row], lambda y, u_t, d_t: [y, _gelu(y + d_t * u_t)], 2)

    def epi(accs, y1_t, bg_t):
        gate = _sigmoid(accs[0] + bg_t)
        return [y1_t * gate, gate]

    y2, gate = _mm_simple(f"s5{l}_glu", y1, w["w_glu"], "nn", None, extras=(y1, bglu_row), epilogue=epi, n_out=2,
                          out_dtypes=[BF16, F32])
    z, h_out = _out_proj_ln(f"s5{l}_out", y2, w["w_out"], h, g, b)
    return h_out, (h, u, ug, yssm, y1, y2, gate, z, h0, h1, mats, apow_c, d_row)


def _s5_bwd(l, dh_out, saved, w, small, g):
    h, u, ug, yssm, y1, y2, gate, z, h0, h1, mats, apow_c, d_row = saved
    T = h.shape[0]
    dz, dg, db = _ln_bwd(f"s5{l}_lnb", dh_out, z, g)

    def epi1(accs, y1_t, gate_t):
        dy2 = accs[0]
        dpre = dy2 * y1_t * gate_t * (1.0 - gate_t)
        return [dpre, dy2 * gate_t, _bcast8(jnp.sum(dpre, axis=0, keepdims=True))]

    dpre, dy1a, pbg = _mm_simple(f"s5{l}_dy2", dz, w["w_out"], "nt", None, extras=(y1, gate), epilogue=epi1, n_out=2,
                                 out_dtypes=[BF16, F32], psum_outs=1)
    dw_out = _mm_simple(f"s5{l}_dWout", y2, dz, "tn", BF16, tm=1024, tn=1024, tk=512)

    def epi2(accs, dy1a_t, yssm_t, u_t, d_t):
        dy1 = accs[0] + dy1a_t
        dy = dy1 * _dgelu(yssm_t + d_t * u_t)
        return [dy, dy * d_t, _bcast8(jnp.sum(dy * u_t, axis=0, keepdims=True))]

    dy, du_skip, pdd = _mm_simple(f"s5{l}_dy1", dpre, w["w_glu"], "nt", None, extras=(dy1a, yssm, u, d_row), epilogue=epi2,
                                  n_out=2, out_dtypes=[F32, F32], psum_outs=1)
    dw_glu = _mm_simple(f"s5{l}_dWglu", y1, dpre, "tn", BF16, tm=1024, tn=1024, tk=512)
    dyg = _to_groups(f"s5{l}_togroups_b", dy)
    dug, dmt, dp0, dp1, dq0, dq1, da = _s5_scan_bwd(f"s5{l}_scanb", dyg, ug, h0, h1, mats, apow_c)
    du = _from_groups(f"s5{l}_fromgroups_b", dug, [du_skip], lambda y, skip: [y + skip], 1)[0]

    def a_grad(p, q):
        return jnp.concatenate([p[:, :S5_STATE] + p[:, S5_STATE:], q[:, S5_STATE:] - q[:, :S5_STATE]], axis=-1)

    dmats = dict(mt=dmt, p0=dp0, p1=dp1, q0=dq0, q1=dq1, a0=a_grad(da[:, 0], da[:, 1]), a1=a_grad(da[:, 2], da[:, 3]))
    dh = _mm_simple(f"s5{l}_dX", du, w["w_in"], "nt", F32, extras=(dz,), epilogue=lambda accs, dz_t: [accs[0] + ALPHA * dz_t])
    dw_in = _mm_simple(f"s5{l}_dWin", h, du, "tn", BF16, tm=1024, tn=1024, tk=512)
    grads = dict(w_in=dw_in, w_glu=dw_glu, w_out=dw_out, d=_psum_rows(pdd), b_glu=_psum_rows(pbg), ln_g=dg, ln_b=db)
    return dh, grads, dmats


def _gla_levels(lc):
    ms, m = [], lc // 2
    while m >= 1:
        ms.append(m)
        m //= 2
    return ms


def _gla_scan_matrix(lc, rev):
    r = np.arange(lc)[:, None]
    t = np.arange(lc)[None, :]
    blocks = []
    for m in _gla_levels(lc):
        same = (r // m) == (t // m)
        upper = ((r // m) % 2) == 1
        blocks.append(same & np.where(upper, t >= r, t < r))
    blocks.append(t >= r)
    blocks.append(t < r)
    if rev:
        blocks = [blk[::-1, ::-1] for blk in blocks]
    return np.concatenate(blocks, axis=1).astype(np.float32)


def _gla_gates(z, lb):
    sig = _sigmoid(z)
    ls = jnp.minimum(z, 0.0) - jnp.log(1.0 + jnp.exp(-jnp.abs(z)))
    a = jnp.log(lb)
    bb = jnp.log(1.0 - lb) + ls
    lf = jnp.maximum(a, bb) + jnp.log(1.0 + jnp.exp(-jnp.abs(a - bb)))
    return lf, (1.0 - lb) * (1.0 - sig), sig


def _gla_cumsum(lf, rev):
    b, sh = lf, 1
    while sh < lf.shape[0]:
        b = b + _shift_rows(b, sh, not rev)
        sh *= 2
    return b


def _gla_bref(b, m, rev):
    lc, n = b.shape
    idx = m if rev else m - 1
    if 2 * m >= 8:
        nb = lc // (2 * m)
        b3 = b.reshape(nb, 2 * m, n)
        return jnp.broadcast_to(b3[:, idx:idx + 1, :], (nb, 2 * m, n)).reshape(lc, n)
    row = lax.broadcasted_iota(jnp.int32, b.shape, 0)
    j = row & (2 * m - 1)
    out = b
    for jj in range(2 * m):
        if jj != idx:
            out = jnp.where(j == jj, pltpu.roll(b, (jj - idx) % lc, 0), out)
    return out


def _gla_masks(lc, rev):
    r = np.arange(lc)
    mq, bm = [], [np.eye(lc)]
    for m in _gla_levels(lc):
        isq = ((r // m) % 2) == (0 if rev else 1)
        mq.append(np.broadcast_to(isq[:, None], (lc, HG_DIM)))
        bm.append((r[:, None] // (2 * m)) == (r[None, :] // (2 * m)))
    return jnp.asarray(np.stack(mq), F32), jnp.asarray(np.stack(bm), F32)


def _gla_chunk(q, k, lf, rev, mq_ref, bm_ref):
    lc = q.shape[0]
    nt = (_DN["nt"], ((), ()))
    b = _gla_cumsum(lf, rev)
    qb, kb = q.astype(BF16), k.astype(BF16)
    sc = bm_ref[0] * lax.dot_general(qb, kb, nt, preferred_element_type=F32)
    levels = []
    for i, m in enumerate(_gla_levels(lc)):
        mq = mq_ref[i]
        mk = 1.0 - mq
        w = jnp.exp((b - _gla_bref(b, m, rev)) * (mq - mk))
        wq, wk = w * mq, w * mk
        xf, yf = q * wq, k * wk
        xb, yb = xf.astype(BF16), yf.astype(BF16)
        sc = sc + bm_ref[i + 1] * lax.dot_general(xb, yb, nt, preferred_element_type=F32)
        levels.append((wq, wk, xf, yf, xb, yb))
    return b, sc, levels


def _hg_specs(T, lc, rev, backward):
    nc = T // lc
    cc = (lambda c: nc - 1 - c) if rev != backward else (lambda c: c)
    zcol = HG_HEADS * (2 if rev else 1)
    q_spec = pl.BlockSpec((lc, HG_DIM), lambda h, c: (cc(c), h))
    z_spec = pl.BlockSpec((lc, HG_DIM), lambda h, c: (cc(c), zcol + h))
    v_spec = pl.BlockSpec((lc, HG_DIM), lambda h, c: (cc(c), 3 * HG_HEADS + h))
    lb_spec = pl.BlockSpec((1, HG_DIM), lambda h, c: (0, h))
    st_spec = pl.BlockSpec((None, None, HG_DIM, HG_DIM), lambda h, c: (h, cc(c), 0, 0))
    return nc, q_spec, z_spec, v_spec, lb_spec, st_spec


def _gla_fwd(name, proj, lb_row, rev):
    T = proj.shape[0]
    lc = _tile(T, HG_CHUNK, 8)
    nc, q_spec, z_spec, v_spec, lb_spec, st_spec = _hg_specs(T, lc, rev, False)
    last = 0 if rev else lc - 1
    mq, bm = _gla_masks(lc, rev)
    const3 = lambda a: pl.BlockSpec(a.shape, lambda h, c: (0, 0, 0))

    def body(q_ref, z_ref, v_ref, lb_ref, mq_ref, bm_ref, o_ref, st_ref, st_s):
        @pl.when(pl.program_id(1) == 0)
        def _():
            st_s[...] = jnp.zeros(st_s.shape, F32)

        q = _silu(q_ref[...])
        lf, k, _ = _gla_gates(z_ref[...], lb_ref[...])
        vb = v_ref[...].astype(BF16)
        b, sc, _ = _gla_chunk(q, k, lf, rev, mq_ref, bm_ref)
        st0 = st_s[...]
        st_ref[...] = st0
        bl = b[last:last + 1, :]
        o = jnp.dot(sc.astype(BF16), vb, preferred_element_type=F32)
        o += lax.dot_general((q * jnp.exp(b)).astype(BF16), st0.astype(BF16), (_DN["nt"], ((), ())), preferred_element_type=F32)
        o_ref[...] = o
        kd = (k * jnp.exp(bl - b)).astype(BF16)
        st_s[...] = st0 * jnp.exp(bl) + lax.dot_general(vb, kd, (_DN["tn"], ((), ())), preferred_element_type=F32)

    return pl.pallas_call(
        body, name=name, grid=(HG_HEADS, nc),
        in_specs=[q_spec, z_spec, v_spec, lb_spec, const3(mq), const3(bm)], out_specs=[q_spec, st_spec],
        out_shape=[jax.ShapeDtypeStruct((T, D_MODEL), F32), jax.ShapeDtypeStruct((HG_HEADS, nc, HG_DIM, HG_DIM), F32)],
        scratch_shapes=[pltpu.VMEM((HG_DIM, HG_DIM), F32)], compiler_params=_cparams(),
    )(proj, proj, proj, lb_row, mq, bm)


def _gla_bwd(name, proj, lb_row, do, st, rev):
    T = proj.shape[0]
    lc = _tile(T, HG_CHUNK, 8)
    nc, q_spec, z_spec, v_spec, lb_spec, st_spec = _hg_specs(T, lc, rev, True)
    wall = jnp.asarray(_gla_scan_matrix(lc, rev), BF16)
    last = 0 if rev else lc - 1
    mq, bm = _gla_masks(lc, rev)
    const3 = lambda a: pl.BlockSpec(a.shape, lambda h, c: (0, 0, 0))

    def body(q_ref, z_ref, v_ref, lb_ref, do_ref, st_ref, wall_ref, mq_ref, bm_ref, dq_ref, dz_ref, dv_ref, dlb_ref, dst_s):
        first = pl.program_id(1) == 0

        @pl.when(first)
        def _():
            dst_s[...] = jnp.zeros(dst_s.shape, F32)
            dlb_ref[...] = jnp.zeros(dlb_ref.shape, F32)

        nn, nt, tn = (_DN["nn"], ((), ())), (_DN["nt"], ((), ())), (_DN["tn"], ((), ()))
        dot = functools.partial(lax.dot_general, preferred_element_type=F32)
        qr, z, lb = q_ref[...], z_ref[...], lb_ref[...]
        q = _silu(qr)
        lf, k, sig = _gla_gates(z, lb)
        vb = v_ref[...].astype(BF16)
        b, sc, levels = _gla_chunk(q, k, lf, rev, mq_ref, bm_ref)
        st0, dst = st_ref[...], dst_s[...]
        st0b, dstb = st0.astype(BF16), dst.astype(BF16)
        dob = do_ref[...].astype(BF16)
        bl = b[last:last + 1, :]
        eb, ebl, ekd = jnp.exp(b), jnp.exp(bl), jnp.exp(bl - b)
        qe, kd = q * eb, k * ekd
        kdb = kd.astype(BF16)
        dsc = dot(dob, vb, nt)
        dv_ref[...] = dot(sc.astype(BF16), dob, tn) + dot(kdb, dstb, nt)
        dqe = dot(dob, st0b, nn)
        dkd = dot(vb, dstb, nn)
        dq = dqe * eb
        dk = dkd * ekd
        zs = []
        dsd = (bm_ref[0] * dsc).astype(BF16)
        dq += dot(dsd, k.astype(BF16), nn)
        dk += dot(dsd, q.astype(BF16), tn)
        for i, (wq, wk, xf, yf, xb, yb) in enumerate(levels):
            dsl = (bm_ref[i + 1] * dsc).astype(BF16)
            dx = dot(dsl, yb, nn)
            dy = dot(dsl, xb, tn)
            dq += dx * wq
            dk += dy * wk
            zs.append((dx * xf + dy * yf).astype(BF16))
        zs.append((dqe * qe).astype(BF16))
        zs.append((dkd * kd).astype(BF16))
        zl = jnp.sum(dst * st0, axis=0, keepdims=True) * ebl
        dlf = dot(wall_ref[...], jnp.concatenate(zs, axis=0), nn) + zl
        dst_s[...] = dst * ebl + dot(dob, qe.astype(BF16), tn)
        inv_f = jnp.exp(-lf)
        one_sig = 1.0 - sig
        dz_ref[...] = (dlf * inv_f - dk) * (1.0 - lb) * sig * one_sig
        dq_ref[...] = dq * _dsilu(qr)
        dlb_ref[...] += _bcast8(jnp.sum((dlf * inv_f - dk) * one_sig, axis=0, keepdims=True))

    big = jax.ShapeDtypeStruct((T, D_MODEL), F32)
    return pl.pallas_call(
        body, name=name, grid=(HG_HEADS, nc),
        in_specs=[q_spec, z_spec, v_spec, lb_spec, q_spec, st_spec, pl.BlockSpec(wall.shape, lambda h, c: (0, 0)),
                  const3(mq), const3(bm)],
        out_specs=[q_spec, q_spec, q_spec, pl.BlockSpec((None, 8, HG_DIM), lambda h, c: (h, 0, 0))],
        out_shape=[big, big, big, jax.ShapeDtypeStruct((HG_HEADS, 8, HG_DIM), F32)],
        scratch_shapes=[pltpu.VMEM((HG_DIM, HG_DIM), F32)], compiler_params=_cparams(),
    )(proj, proj, proj, lb_row, do, st, wall, mq, bm)


def _hg_lower_bounds(hg_lower_bound, layer):
    lbs = jax.nn.softmax(hg_lower_bound, axis=0)
    lbs = jnp.cumsum(lbs, axis=0) - lbs[0]
    return lbs[layer].reshape(1, -1)


def _hg_post(o_fw, o_bw, g_raw, ng):
    outs, ons, os_, rs = [], [], [], []
    for hd in range(o_fw.shape[1] // HG_DIM):
        sl = slice(hd * HG_DIM, (hd + 1) * HG_DIM)
        o = o_fw[:, sl] + o_bw[:, sl]
        r = lax.rsqrt(jnp.mean(o * o, axis=-1, keepdims=True) + RMS_EPS)
        on = o * r * ng
        outs.append(on * _silu(g_raw[:, sl]))
        ons.append(on)
        os_.append(o)
        rs.append(r)
    return outs, ons, os_, rs


def _hg_fwd(l, h, w, small, g, b):
    T = h.shape[0]
    tm = _tile(T, 512, 8)
    proj = _mm_simple(f"hg{l}_in", h, w["w_in"], "nn", F32, tn=1280, j_outer=True)
    lb_fn = lambda p: _hg_lower_bounds(p, l)
    lb_row, lb_vjp = jax.vjp(lb_fn, small["lower_bound"])
    o_fw, st_fw = _gla_fwd(f"hg{l}_gla_fw", proj, lb_row, False)
    o_bw, st_bw = _gla_fwd(f"hg{l}_gla_bw", proj, lb_row, True)
    ng = small["norm_g"].reshape(1, HG_DIM)

    def post(of_t, ob_t, g_t, ng_t):
        return [jnp.concatenate(_hg_post(of_t, ob_t, g_t, ng_t)[0], axis=1)]

    og = _rows(f"hg{l}_post", T, tm,
               [_rt(o_fw, tm), _rt(o_bw, tm), (proj, (tm, D_MODEL), lambda i: (i, 4)), _full(ng)],
               [_rt_out(T, D_MODEL, BF16, tm)], post)[0]
    z, h_out = _out_proj_ln(f"hg{l}_out", og, w["w_out"], h, g, b)
    return h_out, (h, proj, lb_row, lb_vjp, st_fw, st_bw, o_fw, o_bw, ng, og, z)


def _hg_bwd(l, dh_out, saved, w, small, g):
    h, proj, lb_row, lb_vjp, st_fw, st_bw, o_fw, o_bw, ng, og, z = saved
    T = h.shape[0]
    dz, dg, db = _ln_bwd(f"hg{l}_lnb", dh_out, z, g)
    tm = _tile(T, 512, 8)
    nn_tiles = D_MODEL // HG_DIM

    def epi(accs, of_t, ob_t, g_t, ng_t):
        dog = accs[0]
        _, ons, os_, rs = _hg_post(of_t, ob_t, g_t, ng_t)
        dos, dgs, dngs = [], [], []
        for hd in range(nn_tiles):
            sl = slice(hd * HG_DIM, (hd + 1) * HG_DIM)
            d, o, r = dog[:, sl], os_[hd], rs[hd]
            dgs.append(d * ons[hd] * _dsilu(g_t[:, sl]))
            don = d * _silu(g_t[:, sl])
            dngs.append(jnp.sum(don * o * r, axis=0, keepdims=True))
            dxn = don * ng_t
            dos.append(r * dxn - o * (r * r * r) * jnp.mean(dxn * o, axis=-1, keepdims=True))
        return [jnp.concatenate(dos, axis=1), jnp.concatenate(dgs, axis=1), _bcast8(jnp.concatenate(dngs, axis=1))]

    grid = (T // tm, 1, 1)
    row_map = lambda i, j, k: (i, 0)
    do, dg_raw, png = _mm(
        f"hg{l}_dog", grid, [(dz, (tm, D_MODEL), row_map)], [(w["w_out"], (D_MODEL, D_MODEL), lambda i, j, k: (0, 0))],
        [(0, 0, 0, "nt")], 1, (tm, D_MODEL),
        [(o_fw, (tm, D_MODEL), row_map), (o_bw, (tm, D_MODEL), row_map), (proj, (tm, D_MODEL), lambda i, j, k: (i, 4)),
         (ng, (1, HG_DIM), lambda i, j, k: (0, 0))],
        [((T, D_MODEL), F32, (tm, D_MODEL), row_map), ((T, D_MODEL), F32, (tm, D_MODEL), row_map),
         ((T // tm * 8, D_MODEL), F32, (8, D_MODEL), row_map)], epi)
    dw_out = _mm_simple(f"hg{l}_dWout", og, dz, "tn", BF16, tm=1024, tn=1024, tk=512)
    dq_f, dz_f, dv_f, dlb_f = _gla_bwd(f"hg{l}_glab_fw", proj, lb_row, do, st_fw, False)
    dq_b, dz_b, dv_b, dlb_b = _gla_bwd(f"hg{l}_glab_bw", proj, lb_row, do, st_bw, True)
    dproj = jnp.concatenate([dq_f + dq_b, dz_f, dz_b, dv_f + dv_b, dg_raw], axis=1).astype(BF16)
    dh = _mm_simple(f"hg{l}_dX", dproj, w["w_in"], "nt", F32, tm=256, tk=5 * D_MODEL, extras=(dz,),
                    epilogue=lambda accs, dz_t: [accs[0] + ALPHA * dz_t])
    dw_in = _mm_simple(f"hg{l}_dWin", h, dproj, "tn", BF16, tm=1024, tn=1280, tk=512)
    dlb_row = (dlb_f[:, 0, :] + dlb_b[:, 0, :]).reshape(1, D_MODEL)
    grads = dict(w_in=dw_in, w_out=dw_out, lower_bound=lb_vjp(dlb_row)[0],
                 norm_g=_psum_rows(png).reshape(nn_tiles, HG_DIM).sum(axis=0), ln_g=dg, ln_b=db)
    return dh, grads


MLA_W = 256
MLA_SCALE = (MLA_NOPE + MLA_ROPE) ** -0.5


def _swap_halves(a):
    n = a.shape[-1] // 2
    return jnp.concatenate([a[..., n:], a[..., :n]], axis=-1)


def _mla_ext_weights(w_in, w_q_b):
    w_in_ext = jnp.concatenate([w_in, _swap_halves(w_in[:, MLA_Q_LORA + MLA_KV_LORA:])], axis=1)
    wq = w_q_b.reshape(MLA_Q_LORA, MLA_HEADS, MLA_NOPE + MLA_ROPE)
    wq_ext = jnp.concatenate([wq, _swap_halves(wq[:, :, MLA_NOPE:])], axis=2).reshape(MLA_Q_LORA, MLA_HEADS * MLA_W)
    return w_in_ext, wq_ext


def _mla_ext_grads(dw_in_ext, dwq_ext):
    n_lat = MLA_Q_LORA + MLA_KV_LORA
    dw_in = jnp.concatenate([dw_in_ext[:, :n_lat], dw_in_ext[:, n_lat:n_lat + MLA_ROPE]
                             + _swap_halves(dw_in_ext[:, n_lat + MLA_ROPE:])], axis=1)
    dq = dwq_ext.reshape(MLA_Q_LORA, MLA_HEADS, MLA_W)
    dwq = jnp.concatenate([dq[:, :, :MLA_NOPE], dq[:, :, MLA_NOPE:MLA_NOPE + MLA_ROPE]
                           + _swap_halves(dq[:, :, MLA_NOPE + MLA_ROPE:])], axis=2)
    return dw_in, dwq.reshape(MLA_Q_LORA, MLA_HEADS * (MLA_NOPE + MLA_ROPE))


def _rope_table(positions):
    half = MLA_ROPE // 2
    inv_freq = 1.0 / (ROPE_THETA ** (jnp.arange(half, dtype=F32) * (2.0 / MLA_ROPE)))
    ang = positions.astype(F32)[:, None] * inv_freq
    cos, sin = jnp.cos(ang), jnp.sin(ang)
    return jnp.concatenate([cos, cos, -sin, sin], axis=1)


def _rope_sum(prod):
    return prod + pltpu.roll(prod, MLA_ROPE, 1)


def _low_half(a):
    lane = lax.broadcasted_iota(jnp.int32, a.shape, 1)
    return jnp.where(lane < MLA_ROPE, a, 0.0)


def _rms(x, g):
    r = lax.rsqrt(jnp.mean(x * x, axis=-1, keepdims=True) + RMS_EPS)
    return x * r * g


def _rms_bwd(x, g, dy):
    r = lax.rsqrt(jnp.mean(x * x, axis=-1, keepdims=True) + RMS_EPS)
    dxn = dy * g
    dx = r * dxn - x * (r * r * r) * jnp.mean(dxn * x, axis=-1, keepdims=True)
    return dx, jnp.sum(dy * x * r, axis=0, keepdims=True)


ATT_TILE = 512
ATT_BLOCK = 2048


def _lanes(col, n):
    return jnp.tile(col, (1, n // 128))


def _flash_fwd(name, q, k, v):
    H, T, _ = q.shape
    tq, tkb = _tile(T, ATT_TILE, 8), _tile(T, ATT_BLOCK, 128)
    ts = _tile(tkb, ATT_TILE, 128)
    nk, nsub = T // tkb, tkb // ts

    def body(q_ref, k_ref, v_ref, o_ref, lse_ref, m_s, acc_s):
        ki = pl.program_id(2)

        @pl.when(ki == 0)
        def _():
            m_s[...] = jnp.full(m_s.shape, -jnp.inf, F32)
            acc_s[...] = jnp.zeros(acc_s.shape, F32)

        qv = q_ref[...]
        m, acc = m_s[...], acc_s[...]
        for j in range(nsub):
            kj, vj = k_ref[j * ts:(j + 1) * ts, :], v_ref[j * ts:(j + 1) * ts, :]
            s = lax.dot_general(qv, kj, (_DN["nt"], ((), ())), preferred_element_type=F32)
            m_new = jnp.maximum(m, jnp.max(s, axis=-1, keepdims=True))
            p = jnp.exp(s - _lanes(m_new, ts)).astype(BF16)
            acc = _lanes(jnp.exp(m - m_new), 2 * MLA_V) * acc + jnp.dot(p, vj, preferred_element_type=F32)
            m = m_new
        m_s[...], acc_s[...] = m, acc

        @pl.when(ki == nk - 1)
        def _():
            l = acc[:, MLA_V:]
            o_ref[...] = acc[:, :MLA_V] / l
            lse_ref[...] = m + jnp.log(l)

    return pl.pallas_call(
        body, name=name, grid=(H, T // tq, nk),
        in_specs=[pl.BlockSpec((None, tq, MLA_W), lambda h, i, j: (h, i, 0)),
                  pl.BlockSpec((None, tkb, MLA_W), lambda h, i, j: (h, j, 0)),
                  pl.BlockSpec((None, tkb, 2 * MLA_V), lambda h, i, j: (h, j, 0))],
        out_specs=[pl.BlockSpec((tq, MLA_V), lambda h, i, j: (i, h)), pl.BlockSpec((tq, MLA_V), lambda h, i, j: (i, h))],
        out_shape=[jax.ShapeDtypeStruct((T, H * MLA_V), F32), jax.ShapeDtypeStruct((T, H * MLA_V), F32)],
        scratch_shapes=[pltpu.VMEM((tq, MLA_V), F32), pltpu.VMEM((tq, 2 * MLA_V), F32)],
        compiler_params=_cparams(),
    )(q, k, v)


def _flash_bwd(name, q, k, v, do, lse, delta):
    H, T, _ = q.shape
    nt, tn = (_DN["nt"], ((), ())), (_DN["tn"], ((), ()))
    tk, tqb = _tile(T, ATT_TILE, 128), _tile(T, ATT_BLOCK, 128)
    tqs = _tile(tqb, ATT_TILE, 128)
    nk, nqb, nsub = T // tk, T // tqb, tqb // tqs

    def body(q_ref, k_ref, v_ref, do_ref, lse_ref, dl_ref, dq_ref, dk_ref, dv_ref, dk_s, dv_s):
        ki, qb = pl.program_id(1), pl.program_id(2)

        @pl.when(jnp.logical_and(ki == 0, qb == 0))
        def _():
            dq_ref[...] = jnp.zeros(dq_ref.shape, F32)

        @pl.when(qb == 0)
        def _():
            dk_s[...] = jnp.zeros(dk_s.shape, F32)
            dv_s[...] = jnp.zeros(dv_s.shape, F32)

        kv, vv = k_ref[...], v_ref[...]
        dk, dv = dk_s[...], dv_s[...]
        for j in range(nsub):
            sl = slice(j * tqs, (j + 1) * tqs)
            qj, doj = q_ref[sl, :], do_ref[sl, :]
            s = lax.dot_general(qj, kv, nt, preferred_element_type=F32)
            dp = lax.dot_general(doj, vv, nt, preferred_element_type=F32)
            p = jnp.exp(s - _lanes(lse_ref[sl, :], tk))
            ds = (p * (dp - _lanes(dl_ref[sl, :], tk))).astype(BF16)
            dv = dv + lax.dot_general(p.astype(BF16), doj, tn, preferred_element_type=F32)
            dk = dk + lax.dot_general(ds, qj, tn, preferred_element_type=F32)
            rows = pl.ds(pl.multiple_of(qb * tqb + j * tqs, tqs), tqs)
            dq_ref[rows, :] += jnp.dot(ds, kv, preferred_element_type=F32)
        dk_s[...], dv_s[...] = dk, dv

        @pl.when(qb == nqb - 1)
        def _():
            dk_ref[...] = dk
            dv_ref[...] = dv

    return pl.pallas_call(
        body, name=name, grid=(H, nk, nqb),
        in_specs=[pl.BlockSpec((None, tqb, MLA_W), lambda h, i, j: (h, j, 0)),
                  pl.BlockSpec((None, tk, MLA_W), lambda h, i, j: (h, i, 0)),
                  pl.BlockSpec((None, tk, MLA_V), lambda h, i, j: (h, i, 0)),
                  pl.BlockSpec((tqb, MLA_V), lambda h, i, j: (j, h)),
                  pl.BlockSpec((tqb, MLA_V), lambda h, i, j: (j, h)),
                  pl.BlockSpec((tqb, MLA_V), lambda h, i, j: (j, h))],
        out_specs=[pl.BlockSpec((None, T, MLA_W), lambda h, i, j: (h, 0, 0)),
                   pl.BlockSpec((None, tk, MLA_W), lambda h, i, j: (h, i, 0)),
                   pl.BlockSpec((None, tk, MLA_V), lambda h, i, j: (h, i, 0))],
        out_shape=[jax.ShapeDtypeStruct((H, T, MLA_W), F32), jax.ShapeDtypeStruct((H, T, MLA_W), F32),
                   jax.ShapeDtypeStruct((H, T, MLA_V), F32)],
        scratch_shapes=[pltpu.VMEM((tk, MLA_W), F32), pltpu.VMEM((tk, MLA_V), F32)], compiler_params=_cparams(),
    )(q, k, v, do, lse, delta)


def _mla_fwd(l, h, w, small, g, b, cs):
    T = h.shape[0]
    tm = _tile(T, 512, 8)
    H = MLA_HEADS
    gq, gkv = small["q_norm_g"].reshape(1, -1), small["kv_norm_g"].reshape(1, -1)
    n_ext = MLA_Q_LORA + MLA_KV_LORA + 2 * MLA_ROPE
    row = lambda i, j, k: (i, 0)
    fix = lambda i, j, k: (0, 0)

    def epi_lat(accs, gq_t, gkv_t):
        a = accs[0]
        ql, kvl = a[:, :MLA_Q_LORA], a[:, MLA_Q_LORA:MLA_Q_LORA + MLA_KV_LORA]
        return [ql, kvl, a[:, MLA_Q_LORA + MLA_KV_LORA:], _rms(ql, gq_t), _rms(kvl, gkv_t)]

    ql, kvl, kr, xq, xkv = _mm(
        f"mla{l}_in", (T // tm, 1, 1), [(h, (tm, D_MODEL), row)], [(w["w_in_ext"], (D_MODEL, n_ext), fix)],
        [(0, 0, 0, "nn")], 1, (tm, n_ext), [(gq, gq.shape, fix), (gkv, gkv.shape, fix)],
        [((T, MLA_Q_LORA), F32, (tm, MLA_Q_LORA), row), ((T, MLA_KV_LORA), F32, (tm, MLA_KV_LORA), row),
         ((T, 2 * MLA_ROPE), F32, (tm, 2 * MLA_ROPE), row), ((T, MLA_Q_LORA), BF16, (tm, MLA_Q_LORA), row),
         ((T, MLA_KV_LORA), BF16, (tm, MLA_KV_LORA), row)], epi_lat)

    def epi_q(accs, cs_t):
        a = accs[0]
        return [jnp.concatenate([a[:, :MLA_NOPE], _rope_sum(a[:, MLA_NOPE:] * cs_t)], axis=1) * MLA_SCALE]

    head_out = lambda i, j, k: (j, i, 0)
    q = _mm(f"mla{l}_q", (T // tm, H, 1), [(xq, (tm, MLA_Q_LORA), row)],
            [(w["wq_ext"], (MLA_Q_LORA, MLA_W), lambda i, j, k: (0, j))], [(0, 0, 0, "nn")], 1, (tm, MLA_W),
            [(cs, (tm, 2 * MLA_ROPE), row)], [((H, T, MLA_W), BF16, (None, tm, MLA_W), head_out)], epi_q)[0]

    def epi_kv(accs, kr_t, cs_t):
        a = accs[0]
        v_t = a[:, MLA_NOPE:]
        return [jnp.concatenate([a[:, :MLA_NOPE], _low_half(_rope_sum(kr_t * cs_t))], axis=1),
                jnp.concatenate([v_t, jnp.ones_like(v_t)], axis=1)]

    k, v = _mm(f"mla{l}_kv", (T // tm, H, 1), [(xkv, (tm, MLA_KV_LORA), row)],
               [(w["w_kv_b"], (MLA_KV_LORA, MLA_W), lambda i, j, k: (0, j))], [(0, 0, 0, "nn")], 1, (tm, MLA_W),
               [(kr, (tm, 2 * MLA_ROPE), row), (cs, (tm, 2 * MLA_ROPE), row)],
               [((H, T, MLA_W), BF16, (None, tm, MLA_W), head_out), ((H, T, 2 * MLA_V), BF16, (None, tm, 2 * MLA_V), head_out)],
               epi_kv)
    o, lse = _flash_fwd(f"mla{l}_attn", q, k, v)
    z, h_out = _out_proj_ln(f"mla{l}_out", o, w["w_out"], h, g, b)
    return h_out, (h, ql, kvl, xq, xkv, q, k, v, o, lse, z, gq, gkv, cs)


def _mla_bwd(l, dh_out, saved, w, g):
    h, ql, kvl, xq, xkv, q, k, v, o, lse, z, gq, gkv, cs = saved
    T = h.shape[0]
    tm = _tile(T, 512, 8)
    H = MLA_HEADS
    dz, dg, db = _ln_bwd(f"mla{l}_lnb", dh_out, z, g)
    def epi_do(accs, o_t):
        d = accs[0]
        deltas = [jnp.broadcast_to(jnp.sum(d[:, hd * MLA_V:(hd + 1) * MLA_V] * o_t[:, hd * MLA_V:(hd + 1) * MLA_V],
                                           axis=-1, keepdims=True), (d.shape[0], MLA_V)) for hd in range(H)]
        return [d, jnp.concatenate(deltas, axis=1)]

    do, delta = _mm_simple(f"mla{l}_dO", dz, w["w_out"], "nt", None, extras=(o,), epilogue=epi_do, n_out=2,
                           out_dtypes=[BF16, F32])
    dw_out = _mm_simple(f"mla{l}_dWout", o, dz, "tn", BF16, tm=1024, tn=1024, tk=512)
    dq, dk, dv = _flash_bwd(f"mla{l}_attnb", q, k, v, do, lse, delta)
    head_in = lambda i, hh: (hh, i, 0)
    row2 = lambda i, hh: (i, 0)

    def fn_q(dq_t, cs_t):
        d = dq_t[:, MLA_NOPE:]
        return [jnp.concatenate([dq_t[:, :MLA_NOPE], _rope_sum(d) * cs_t], axis=1) * MLA_SCALE]

    dq_eff = _grid_call(f"mla{l}_dqeff", (T // tm, H), [(dq, (None, tm, MLA_W), head_in), (cs, (tm, 2 * MLA_ROPE), row2)],
                        [((T, H * MLA_W), BF16, (tm, MLA_W), lambda i, hh: (i, hh))], fn_q)[0]

    def fn_kv(dk_t, dv_t, cs_t):
        return [jnp.concatenate([dk_t[:, :MLA_NOPE], dv_t], axis=1), _rope_sum(_low_half(dk_t[:, MLA_NOPE:])) * cs_t]

    dkv_eff, dkr = _grid_call(
        f"mla{l}_dkveff", (T // tm, H),
        [(dk, (None, tm, MLA_W), head_in), (dv, (None, tm, MLA_V), head_in), (cs, (tm, 2 * MLA_ROPE), row2)],
        [((T, H * MLA_W), BF16, (tm, MLA_W), lambda i, hh: (i, hh)), ((T, 2 * MLA_ROPE), F32, (tm, 2 * MLA_ROPE), row2)],
        fn_kv, acc_outs=(1,))
    dxq = _mm_simple(f"mla{l}_dxq", dq_eff, w["wq_ext"], "nt", F32, tn=MLA_Q_LORA)
    dwq_ext = _mm_simple(f"mla{l}_dWq", xq, dq_eff, "tn", BF16, tm=MLA_Q_LORA, tn=1024, tk=512)
    dxkv = _mm_simple(f"mla{l}_dxkv", dkv_eff, w["w_kv_b"], "nt", F32, tn=MLA_KV_LORA)
    dwkv = _mm_simple(f"mla{l}_dWkv", xkv, dkv_eff, "tn", BF16, tm=MLA_KV_LORA, tn=1024, tk=512)

    def fn_lat(ql_t, dxq_t, gq_t, kvl_t, dxkv_t, gkv_t, dkr_t):
        dql, dgq = _rms_bwd(ql_t, gq_t, dxq_t)
        dkvl, dgkv = _rms_bwd(kvl_t, gkv_t, dxkv_t)
        return [jnp.concatenate([dql, dkvl, dkr_t], axis=1), _bcast8(dgq), _bcast8(dgkv)]

    n_ext = MLA_Q_LORA + MLA_KV_LORA + 2 * MLA_ROPE
    dlat, pgq, pgkv = _rows(f"mla{l}_dlat", T, tm,
                            [_rt(ql, tm), _rt(dxq, tm), _full(gq), _rt(kvl, tm), _rt(dxkv, tm), _full(gkv), _rt(dkr, tm)],
                            [_rt_out(T, n_ext, BF16, tm), _ps_out(T, MLA_Q_LORA, tm), _ps_out(T, MLA_KV_LORA, tm)], fn_lat)
    dh = _mm_simple(f"mla{l}_dX", dlat, w["w_in_ext"], "nt", F32, tk=n_ext, extras=(dz,),
                    epilogue=lambda accs, dz_t: [accs[0] + ALPHA * dz_t])
    dw_in_ext = _mm_simple(f"mla{l}_dWin", h, dlat, "tn", BF16, tm=1024, tn=n_ext, tk=512)
    dw_in, dwq = _mla_ext_grads(dw_in_ext.astype(F32), dwq_ext.astype(F32))
    grads = dict(w_in=dw_in, w_q_b=dwq, w_kv_b=dwkv, w_out=dw_out, q_norm_g=_psum_rows(pgq), kv_norm_g=_psum_rows(pgkv),
                 ln_g=dg, ln_b=db)
    return dh, grads


def _loss_head(y, target):
    T = y.shape[0]
    tm = _tile(T, 512, 8)

    def fn(y_t, t_t):
        d = y_t - t_t
        part = 0.5 * jnp.sum(jnp.mean(d * d, axis=-1, keepdims=True), axis=0, keepdims=True)
        return [d * (1.0 / D_MODEL), jnp.broadcast_to(part, (8, 128))]

    dy, part = _rows("loss_head", T, tm, [_rt(y, tm), _rt(target, tm)],
                     [_rt_out(T, D_MODEL, F32, tm), ((T // tm * 8, 128), F32, (8, 128), lambda i: (i, 0))], fn)
    return jnp.sum(part.reshape(-1, 8, 128)[:, 0, 0]), dy


_S5_VECTORS = ("d", "b_glu")


def _local_step(x, positions, target, W):
    row = lambda a, i: a[i].reshape(1, -1)
    w_in_ext, wq_ext = _mla_ext_weights(W["mla_w_in"][0], W["mla_w_q_b"][0])
    cs = _rope_table(positions)
    s5_mats, s5_mats_vjp, s5_apow, s5_apow_c = _s5_prepare(W, x.shape[0])
    h, saves = x, []
    for l in range(DEPTH):
        kind, slot = LAYER_MIXER[l], l // 3
        g, b = row(W["ln_mix_g"], l), row(W["ln_mix_b"], l)
        if kind == 0:
            w = {k: W["s5_" + k][slot] for k in ("w_in", "w_glu", "w_out")}
            small = {k: W["s5_" + k][slot] for k in _S5_VECTORS}
            h, sv = _s5_fwd(l, h, w, small, g, b, {k: v[slot] for k, v in s5_mats.items()}, s5_apow[slot], s5_apow_c[slot])
        elif kind == 1:
            w = dict(w_in=W["hg_w_in"][slot], w_out=W["hg_w_out"][slot])
            small = dict(lower_bound=W["hg_lower_bound"], norm_g=W["hg_norm_g"][slot])
            h, sv = _hg_fwd(l, h, w, small, g, b)
        else:
            w = dict(w_in_ext=w_in_ext, wq_ext=wq_ext, w_kv_b=W["mla_w_kv_b"][slot], w_out=W["mla_w_out"][slot])
            small = dict(q_norm_g=W["mla_q_norm_g"][slot], kv_norm_g=W["mla_kv_norm_g"][slot])
            h, sv = _mla_fwd(l, h, w, small, g, b, cs)
        h, fsv = _ffn_fwd(l, h, W["ffn_w_in"][l], W["ffn_w_out"][l], row(W["ln_ffn_g"], l), row(W["ln_ffn_b"], l))
        saves.append((w, small, sv, fsv))
    loss, dh = _loss_head(h, target)
    per_layer = [None] * DEPTH
    s5_dmats = {}
    for l in reversed(range(DEPTH)):
        kind = LAYER_MIXER[l]
        w, small, sv, fsv = saves[l]
        dh, gf = _ffn_bwd(l, dh, fsv, W["ffn_w_in"][l], W["ffn_w_out"][l], row(W["ln_ffn_g"], l))
        g = row(W["ln_mix_g"], l)
        if kind == 0:
            dh, gm, s5_dmats[l // 3] = _s5_bwd(l, dh, sv, w, small, g)
        elif kind == 1:
            dh, gm = _hg_bwd(l, dh, sv, w, small, g)
        else:
            dh, gm = _mla_bwd(l, dh, sv, w, g)
        per_layer[l] = (gm, gf)
    grads = {}
    stack = lambda xs: jnp.stack([a.astype(F32) if a.dtype != BF16 else a for a in xs])
    grads["ln_mix_g"] = stack([per_layer[l][0]["ln_g"] for l in range(DEPTH)])
    grads["ln_mix_b"] = stack([per_layer[l][0]["ln_b"] for l in range(DEPTH)])
    grads["ln_ffn_g"] = stack([per_layer[l][1]["ln_g"] for l in range(DEPTH)])
    grads["ln_ffn_b"] = stack([per_layer[l][1]["ln_b"] for l in range(DEPTH)])
    grads["ffn_w_in"] = stack([per_layer[l][1]["w_in"] for l in range(DEPTH)])
    grads["ffn_w_out"] = stack([per_layer[l][1]["w_out"] for l in range(DEPTH)])
    s5_layers = [l for l in range(DEPTH) if LAYER_MIXER[l] == 0]
    for k in ("w_in", "w_glu", "w_out") + _S5_VECTORS:
        grads["s5_" + k] = stack([per_layer[l][0][k] for l in s5_layers])
    d_ssm = s5_mats_vjp({k: jnp.stack([s5_dmats[s][k] for s in range(len(s5_layers))]) for k in s5_mats})
    for k, v in zip(_S5_SSM, d_ssm):
        grads["s5_" + k] = v
    hg = per_layer[1][0]
    grads["hg_w_in"], grads["hg_w_out"] = hg["w_in"][None], hg["w_out"][None]
    grads["hg_lower_bound"], grads["hg_norm_g"] = hg["lower_bound"], hg["norm_g"][None]
    for k, v in per_layer[2][0].items():
        if not k.startswith("ln_"):
            grads["mla_" + k] = v[None]
    return loss, dh, grads


def _here():
    return lax.axis_index("x"), lax.axis_index("y"), lax.axis_index("c")


def _any_spec():
    return pl.BlockSpec(memory_space=pl.ANY)


def _chip_exchange(name, xs, scatter):
    n = len(xs)

    def body(*refs):
        ins, outs = refs[:n], refs[n:2 * n]
        send_sems, recv_sems, loc_sems = refs[2 * n:]
        x, y, c = _here()
        me = 2 * x + y
        peers = [(1 - x, y), (x, 1 - y), (1 - x, 1 - y)]
        copies = []
        for t in range(n):
            src_of = (lambda p, t=t: ins[t].at[p]) if scatter else (lambda p, t=t: ins[t])
            loc = pltpu.make_async_copy(src_of(me), outs[t].at[me], loc_sems.at[t])
            loc.start()
            copies.append(loc)
            for j, (px, py) in enumerate(peers):
                cp = pltpu.make_async_remote_copy(
                    src_ref=src_of(2 * px + py), dst_ref=outs[t].at[me], send_sem=send_sems.at[t, j],
                    recv_sem=recv_sems.at[t, j], device_id=(px, py, c), device_id_type=MESH)
                cp.start()
                copies.append(cp)
        for cp in copies:
            cp.wait()

    out_shape = [jax.ShapeDtypeStruct(a.shape if scatter else (N_CHIPS,) + a.shape, a.dtype) for a in xs]
    return pl.pallas_call(
        body, name=name, in_specs=[_any_spec()] * n, out_specs=[_any_spec()] * n, out_shape=out_shape,
        scratch_shapes=[pltpu.SemaphoreType.DMA((n, 3)), pltpu.SemaphoreType.DMA((n, 3)), pltpu.SemaphoreType.DMA((n,))],
    )(*xs)


def _core_exchange(name, a, n_chunks=8):
    rows = a.shape[0] // n_chunks
    assert rows * n_chunks == a.shape[0] and rows % 8 == 0, (a.shape, n_chunks)

    def body(a_ref, o_ref, send_sems, recv_sems, loc_sem):
        x, y, c = _here()
        loc = pltpu.make_async_copy(a_ref, o_ref.at[c], loc_sem)
        loc.start()
        copies = [loc]
        for i in range(n_chunks):
            part = pl.ds(i * rows, rows)
            cp = pltpu.make_async_remote_copy(
                src_ref=a_ref.at[part], dst_ref=o_ref.at[c, part], send_sem=send_sems.at[i], recv_sem=recv_sems.at[i],
                device_id=(x, y, 1 - c), device_id_type=MESH)
            cp.start()
            copies.append(cp)
        for cp in copies:
            cp.wait()

    return pl.pallas_call(
        body, name=name, in_specs=[_any_spec()], out_specs=_any_spec(),
        out_shape=jax.ShapeDtypeStruct((2,) + a.shape, a.dtype),
        scratch_shapes=[pltpu.SemaphoreType.DMA((n_chunks,)), pltpu.SemaphoreType.DMA((n_chunks,)), pltpu.SemaphoreType.DMA],
    )(a)


def _all_exchange(name, a):
    def body(a_ref, o_ref, send_sems, recv_sems, loc_sem):
        x, y, c = _here()
        me = 4 * x + 2 * y + c
        loc = pltpu.make_async_copy(a_ref, o_ref.at[me], loc_sem)
        loc.start()
        copies = [loc]
        for mask in range(1, N_DEV):
            fx, fy, fc = (mask >> 2) & 1, (mask >> 1) & 1, mask & 1
            peer = (1 - x if fx else x, 1 - y if fy else y, 1 - c if fc else c)
            cp = pltpu.make_async_remote_copy(src_ref=a_ref, dst_ref=o_ref.at[me], send_sem=send_sems.at[mask - 1],
                                              recv_sem=recv_sems.at[mask - 1], device_id=peer, device_id_type=MESH)
            cp.start()
            copies.append(cp)
        for cp in copies:
            cp.wait()

    return pl.pallas_call(
        body, name=name, in_specs=[_any_spec()], out_specs=_any_spec(),
        out_shape=jax.ShapeDtypeStruct((N_DEV,) + a.shape, a.dtype),
        scratch_shapes=[pltpu.SemaphoreType.DMA((N_DEV - 1,)), pltpu.SemaphoreType.DMA((N_DEV - 1,)), pltpu.SemaphoreType.DMA],
    )(a)


def _sum_leading(name, a, out_dtype=F32):
    n, R, C = a.shape
    tr = _tile(R, 512, 16)

    def fn(a_t):
        s = a_t[0].astype(F32)
        for i in range(1, n):
            s = s + a_t[i].astype(F32)
        return [s]

    return _grid_call(name, (R // tr,), [(a, (n, tr, C), lambda i: (0, i, 0))],
                      [((R, C), out_dtype, (tr, C), lambda i: (i, 0))], fn)[0]


def _adamw(name, g_parts, w, m, v):
    R, C = w.shape
    tr = _tile(R, 256, 8)
    ng = len(g_parts)
    c1 = 1.0 / (1.0 - ADAM_B1 ** ADAM_STEP)
    c2 = 1.0 / (1.0 - ADAM_B2 ** ADAM_STEP)

    def fn(*tiles):
        g = tiles[0].astype(F32)
        for t in tiles[1:ng]:
            g = g + t.astype(F32)
        w_t, m_t, v_t = tiles[ng:]
        m_n = ADAM_B1 * m_t + (1.0 - ADAM_B1) * g
        v_n = ADAM_B2 * v_t + (1.0 - ADAM_B2) * (g * g)
        delta = -ADAM_LR * ((m_n * c1) / (jnp.sqrt(v_n * c2) + ADAM_EPS) + ADAM_WD * w_t)
        return [g, delta, m_n, v_n]

    spec = lambda a: (a, (tr, C), lambda i: (i, 0))
    return _grid_call(name, (R // tr,), [spec(a) for a in list(g_parts) + [w, m, v]],
                      [((R, C), F32, (tr, C), lambda i: (i, 0))] * 4, fn)


_WEIGHTS = ("ln_mix_g", "ln_mix_b", "ln_ffn_g", "ln_ffn_b", "ffn_w_in", "ffn_w_out", "s5_w_in", "s5_lam_re", "s5_lam_im",
            "s5_log_step", "s5_b_re", "s5_b_im", "s5_c_re", "s5_c_im", "s5_d", "s5_w_glu", "s5_b_glu", "s5_w_out", "hg_w_in",
            "hg_lower_bound", "hg_norm_g", "hg_w_out", "mla_w_in", "mla_q_norm_g", "mla_w_q_b", "mla_kv_norm_g", "mla_w_kv_b",
            "mla_w_out")
_BIG = {"ffn_w_in": 2, "ffn_w_out": 1, "s5_w_in": 1, "s5_w_glu": 1, "s5_w_out": 1, "hg_w_in": 2, "hg_w_out": 1,
        "mla_w_in": 1, "mla_w_q_b": 2, "mla_w_kv_b": 2, "mla_w_out": 1}
_SMALL_SHARDED = {"s5_d": 1, "s5_b_glu": 1, "mla_q_norm_g": 1, "mla_kv_norm_g": 1}
_REPLICATED = tuple(n for n in _WEIGHTS if n not in _BIG and n not in _SMALL_SHARDED)
LANES = 1024


def _pack(arrs, dtype, row_mult, lead=0):
    rows, segs, r = [], [], 0
    for a in arrs:
        lead_shape = a.shape[:lead]
        flat = a.astype(dtype).reshape(lead_shape + (-1,))
        n = -(-flat.shape[-1] // LANES)
        flat = jnp.pad(flat, [(0, 0)] * lead + [(0, n * LANES - flat.shape[-1])])
        rows.append(flat.reshape(lead_shape + (n, LANES)))
        segs.append((r, n))
        r += n
    pad = -r % row_mult
    if pad:
        rows.append(jnp.zeros(rows[0].shape[:lead] + (pad, LANES), dtype))
    return jnp.concatenate(rows, axis=lead), segs


def _unpack(packed, segs, shapes):
    out = []
    for (r0, n), shp in zip(segs, shapes):
        size = int(np.prod(shp))
        out.append(packed[..., r0:r0 + n, :].reshape(packed.shape[:-2] + (n * LANES,))[..., :size].reshape(packed.shape[:-2] + tuple(shp)))
    return out


def _unshard(stacked, axis):
    moved = jnp.moveaxis(stacked, 0, axis)
    shp = list(moved.shape)
    return moved.reshape(shp[:axis] + [shp[axis] * shp[axis + 1]] + shp[axis + 2:])


def _shard_split(full, axis):
    shp = list(full.shape)
    a = full.reshape(shp[:axis] + [N_CHIPS, shp[axis] // N_CHIPS] + shp[axis + 1:])
    return jnp.moveaxis(a, axis, 0)


def _train_step(x, positions, target, w, m, v):
    big, small_sh = list(_BIG), list(_SMALL_SHARDED)
    chip = 2 * lax.axis_index("x") + lax.axis_index("y")

    big_pack, big_segs = _pack([w[n] for n in big], BF16, 16)
    sm_pack, sm_segs = _pack([w[n] for n in small_sh], F32, 8)
    big_all, sm_all = _chip_exchange("gather_weights", [big_pack, sm_pack], scatter=False)
    W = {n: w[n] for n in _REPLICATED}
    for n, s in zip(big, _unpack(big_all, big_segs, [w[n].shape for n in big])):
        W[n] = _unshard(s, _BIG[n])
    for n, s in zip(small_sh, _unpack(sm_all, sm_segs, [w[n].shape for n in small_sh])):
        W[n] = _unshard(s, _SMALL_SHARDED[n])

    loss_local, grad_x, G = _local_step(x, positions, target, W)
    loss = lax.psum(loss_local, ("x", "y", "c"))
    out = {}

    g_pack, _ = _pack([_shard_split(G[n].astype(BF16), _BIG[n]) for n in big], BF16, 16, lead=1)
    recv = _chip_exchange("scatter_grads", [g_pack], scatter=True)[0]
    pair = _core_exchange("swap_core_sums", _sum_leading("sum_chips", recv, BF16), n_chunks=1)
    for n, (r0, nr) in zip(big, big_segs):
        as_rows = lambda a: a.reshape(nr, LANES)
        res = _adamw("adamw_" + n, [pair[0, r0:r0 + nr], pair[1, r0:r0 + nr]], as_rows(w[n]), as_rows(m[n]), as_rows(v[n]))
        out[n] = tuple(r.reshape(w[n].shape) for r in res)

    small = list(_REPLICATED) + small_sh
    s_pack, s_segs = _pack([G[n] for n in small], F32, 16)
    total = _sum_leading("sum_small", _all_exchange("gather_small_grads", s_pack))
    g_small = dict(zip(small, _unpack(total, s_segs, [G[n].shape for n in small])))
    for n in small_sh:
        width = w[n].shape[1]
        g_small[n] = lax.dynamic_slice_in_dim(g_small[n], chip * width, width, axis=1)
    packs = [_pack([d[n] for n in small], F32, 8)[0] for d in (g_small, w, m, v)]
    _, a_segs = _pack([w[n] for n in small], F32, 8)
    res = _adamw("adamw_small", [packs[0]], packs[1], packs[2], packs[3])
    unpacked = [_unpack(r, a_segs, [w[n].shape for n in small]) for r in res]
    for i, n in enumerate(small):
        out[n] = tuple(u[i] for u in unpacked)
    return loss, grad_x, out


def kernel(x, positions, ln_mix_g, ln_mix_b, ln_ffn_g, ln_ffn_b, ffn_w_in, ffn_w_out, s5_w_in, s5_lam_re, s5_lam_im,
           s5_log_step, s5_b_re, s5_b_im, s5_c_re, s5_c_im, s5_d, s5_w_glu, s5_b_glu, s5_w_out, hg_w_in,
           hg_lower_bound, hg_norm_g, hg_w_out, mla_w_in, mla_q_norm_g, mla_w_q_b, mla_kv_norm_g, mla_w_kv_b,
           mla_w_out, loss_target, m_ln_mix_g, m_ln_mix_b, m_ln_ffn_g, m_ln_ffn_b, m_ffn_w_in, m_ffn_w_out, m_s5_w_in,
           m_s5_lam_re, m_s5_lam_im, m_s5_log_step, m_s5_b_re, m_s5_b_im, m_s5_c_re, m_s5_c_im, m_s5_d, m_s5_w_glu,
           m_s5_b_glu, m_s5_w_out, m_hg_w_in, m_hg_lower_bound, m_hg_norm_g, m_hg_w_out, m_mla_w_in, m_mla_q_norm_g,
           m_mla_w_q_b, m_mla_kv_norm_g, m_mla_w_kv_b, m_mla_w_out, v_ln_mix_g, v_ln_mix_b, v_ln_ffn_g, v_ln_ffn_b,
           v_ffn_w_in, v_ffn_w_out, v_s5_w_in, v_s5_lam_re, v_s5_lam_im, v_s5_log_step, v_s5_b_re, v_s5_b_im,
           v_s5_c_re, v_s5_c_im, v_s5_d, v_s5_w_glu, v_s5_b_glu, v_s5_w_out, v_hg_w_in, v_hg_lower_bound, v_hg_norm_g,
           v_hg_w_out, v_mla_w_in, v_mla_q_norm_g, v_mla_w_q_b, v_mla_kv_norm_g, v_mla_w_kv_b, v_mla_w_out):
    args = (ln_mix_g, ln_mix_b, ln_ffn_g, ln_ffn_b, ffn_w_in, ffn_w_out, s5_w_in, s5_lam_re, s5_lam_im,
            s5_log_step, s5_b_re, s5_b_im, s5_c_re, s5_c_im, s5_d, s5_w_glu, s5_b_glu, s5_w_out, hg_w_in,
            hg_lower_bound, hg_norm_g, hg_w_out, mla_w_in, mla_q_norm_g, mla_w_q_b, mla_kv_norm_g, mla_w_kv_b,
            mla_w_out, m_ln_mix_g, m_ln_mix_b, m_ln_ffn_g, m_ln_ffn_b, m_ffn_w_in, m_ffn_w_out,
            m_s5_w_in, m_s5_lam_re, m_s5_lam_im, m_s5_log_step, m_s5_b_re, m_s5_b_im, m_s5_c_re, m_s5_c_im, m_s5_d,
            m_s5_w_glu, m_s5_b_glu, m_s5_w_out, m_hg_w_in, m_hg_lower_bound, m_hg_norm_g, m_hg_w_out, m_mla_w_in,
            m_mla_q_norm_g, m_mla_w_q_b, m_mla_kv_norm_g, m_mla_w_kv_b, m_mla_w_out, v_ln_mix_g, v_ln_mix_b,
            v_ln_ffn_g, v_ln_ffn_b, v_ffn_w_in, v_ffn_w_out, v_s5_w_in, v_s5_lam_re, v_s5_lam_im, v_s5_log_step,
            v_s5_b_re, v_s5_b_im, v_s5_c_re, v_s5_c_im, v_s5_d, v_s5_w_glu, v_s5_b_glu, v_s5_w_out, v_hg_w_in,
            v_hg_lower_bound, v_hg_norm_g, v_hg_w_out, v_mla_w_in, v_mla_q_norm_g, v_mla_w_q_b, v_mla_kv_norm_g,
            v_mla_w_kv_b, v_mla_w_out)
    nw = len(_WEIGHTS)
    w = dict(zip(_WEIGHTS, args[:nw]))
    m = dict(zip(_WEIGHTS, args[nw:2 * nw]))
    v = dict(zip(_WEIGHTS, args[2 * nw:]))
    loss, grad_x, out = _train_step(x[0], positions[0], loss_target[0], w, m, v)
    res = [loss, grad_x[None]]
    for i in range(4):
        res += [out[n][i] for n in _WEIGHTS]
    return tuple(res)
```

```python
import functools
import math

import numpy as np
import jax
import jax.numpy as jnp
from jax import lax
from jax.experimental import pallas as pl
from jax.experimental.pallas import tpu as pltpu

F32 = jnp.float32
BF16 = jnp.bfloat16

D_MODEL = 1024
DEPTH = 4
LAYER_MIXER = (0, 1, 2, 0)
S5_GROUP = 16
S5_GROUPS = 64
S5_STATE = 64
S5_CHUNK = 16
HG_HEADS = 8
HG_DIM = 128
HG_CHUNK = 128
MLA_HEADS = 8
MLA_NOPE = 128
MLA_ROPE = 64
MLA_V = 128
MLA_Q_LORA = 384
MLA_KV_LORA = 256
ROPE_THETA = 10000.0
FFN_HIDDEN = 2816
ALPHA = (2 * DEPTH) ** 0.25
LN_EPS = 1e-5
RMS_EPS = 1e-6
ADAM_LR, ADAM_B1, ADAM_B2, ADAM_EPS, ADAM_WD, ADAM_STEP = 0.001, 0.9, 0.999, 1e-08, 0.01, 10
VMEM_LIMIT_BYTES = 56 * 1024 * 1024
MESH = pl.DeviceIdType.MESH
N_CHIPS = 4
N_DEV = 8


def _cparams():
    return pltpu.CompilerParams(vmem_limit_bytes=VMEM_LIMIT_BYTES)


def _tile(n, want, mult):
    t = min(want, n)
    t -= t % mult
    while t >= mult:
        if n % t == 0:
            return t
        t -= mult
    return n


def _sigmoid(x):
    return 1.0 / (1.0 + jnp.exp(-x))


def _silu(x):
    return x * _sigmoid(x)


def _dsilu(x):
    s = _sigmoid(x)
    return s * (1.0 + x * (1.0 - s))


_GELU_C = math.sqrt(2.0 / math.pi)


def _gelu(x):
    return 0.5 * x * (1.0 + jnp.tanh(_GELU_C * (x + 0.044715 * x * x * x)))


def _dgelu(x):
    t = jnp.tanh(_GELU_C * (x + 0.044715 * x * x * x))
    return 0.5 * (1.0 + t) + 0.5 * x * (1.0 - t * t) * _GELU_C * (1.0 + 3 * 0.044715 * x * x)


def _layer_norm(z, g, b):
    mu = jnp.mean(z, axis=-1, keepdims=True)
    zc = z - mu
    var = jnp.mean(zc * zc, axis=-1, keepdims=True)
    return zc * lax.rsqrt(var + LN_EPS) * g + b


def _bcast8(row):
    return jnp.broadcast_to(row, (8, row.shape[-1]))


def _stack_rows8(rows):
    n = rows[0].shape[-1]
    idx = lax.broadcasted_iota(jnp.int32, (8, n), 0)
    out = jnp.zeros((8, n), F32)
    for i, r in enumerate(rows):
        out = jnp.where(idx == i, _bcast8(r), out)
    return out


def _psum_rows(a):
    return a.reshape(-1, 8, a.shape[-1])[:, 0, :].sum(axis=0)


_DN = {"nn": ((1,), (0,)), "nt": ((1,), (1,)), "tn": ((0,), (0,))}


def _mm(name, grid, a_defs, b_defs, pairs, n_acc, acc_shape, extra_defs, out_defs, epilogue):
    nk = grid[2]
    na, nb, ne, no = len(a_defs), len(b_defs), len(extra_defs), len(out_defs)

    def body(*refs):
        a_refs = refs[:na]
        b_refs = refs[na:na + nb]
        e_refs = refs[na + nb:na + nb + ne]
        o_refs = refs[na + nb + ne:na + nb + ne + no]
        acc = refs[-1]
        k = pl.program_id(2)

        @pl.when(k == 0)
        def _():
            acc[...] = jnp.zeros(acc.shape, F32)

        for (ai, bi, ci, mode) in pairs:
            a = a_refs[ai][...].astype(BF16)
            b = b_refs[bi][...].astype(BF16)
            acc[ci] += lax.dot_general(a, b, (_DN[mode], ((), ())), preferred_element_type=F32)

        @pl.when(k == nk - 1)
        def _():
            outs = epilogue([acc[c] for c in range(n_acc)], *[e[...] for e in e_refs])
            for o_ref, o in zip(o_refs, outs):
                o_ref[...] = o.astype(o_ref.dtype)

    in_specs = [pl.BlockSpec(d[1], d[2]) for d in list(a_defs) + list(b_defs) + list(extra_defs)]
    out_specs = [pl.BlockSpec(d[2], d[3]) for d in out_defs]
    out_shape = [jax.ShapeDtypeStruct(d[0], d[1]) for d in out_defs]
    res = pl.pallas_call(
        body, name=name, grid=grid, in_specs=in_specs, out_specs=out_specs, out_shape=out_shape,
        scratch_shapes=[pltpu.VMEM((n_acc,) + tuple(acc_shape), F32)], compiler_params=_cparams(),
    )(*[d[0] for d in list(a_defs) + list(b_defs) + list(extra_defs)])
    return res


def _mm_simple(name, a, b, mode, out_dtype, tm=512, tn=1024, tk=1024, extras=(), epilogue=None, n_out=1,
               out_dtypes=None, psum_outs=0, j_outer=False):
    if mode == "nn":
        (M, K), (K2, N) = a.shape, b.shape
    elif mode == "nt":
        (M, K), (N, K2) = a.shape, b.shape
    else:
        (K, M), (K2, N) = a.shape, b.shape
    assert K == K2, (name, a.shape, b.shape, mode)
    tm, tn, tk = _tile(M, tm, 8), _tile(N, tn, 128), _tile(K, tk, 128)
    grid = (M // tm, N // tn, K // tk)
    if mode == "nn":
        a_def = (a, (tm, tk), lambda i, j, k: (i, k))
        b_def = (b, (tk, tn), lambda i, j, k: (k, j))
    elif mode == "nt":
        a_def = (a, (tm, tk), lambda i, j, k: (i, k))
        b_def = (b, (tn, tk), lambda i, j, k: (j, k))
    else:
        a_def = (a, (tk, tm), lambda i, j, k: (k, i))
        b_def = (b, (tk, tn), lambda i, j, k: (k, j))
    extra_defs = []
    for e in extras:
        if e.shape[0] == 1:
            extra_defs.append((e, (1, tn), lambda i, j, k: (0, j)))
        else:
            extra_defs.append((e, (tm, tn), lambda i, j, k: (i, j)))
    out_dtypes = out_dtypes or [out_dtype] * n_out
    out_defs = [((M, N), dt, (tm, tn), lambda i, j, k: (i, j)) for dt in out_dtypes]
    out_defs += [((M // tm * 8, N), F32, (8, tn), lambda i, j, k: (i, j)) for _ in range(psum_outs)]
    if epilogue is None:
        epilogue = lambda accs: [accs[0]]
    a_defs, b_defs = [a_def], [b_def]
    if j_outer:
        swap = lambda d: d[:-1] + ((lambda f: lambda j, i, k: f(i, j, k))(d[-1]),)
        grid = (grid[1], grid[0], grid[2])
        a_defs, b_defs = [swap(a_def)], [swap(b_def)]
        extra_defs, out_defs = [swap(d) for d in extra_defs], [swap(d) for d in out_defs]
    res = _mm(name, grid, a_defs, b_defs, [(0, 0, 0, mode)], 1, (tm, tn), extra_defs, out_defs, epilogue)
    return res[0] if len(res) == 1 else res


def _rows(name, T, tm, in_defs, out_defs, fn):
    ni = len(in_defs)

    def body(*refs):
        outs = fn(*[r[...] for r in refs[:ni]])
        for o_ref, o in zip(refs[ni:], outs):
            o_ref[...] = o.astype(o_ref.dtype)

    res = pl.pallas_call(
        body, name=name, grid=(T // tm,),
        in_specs=[pl.BlockSpec(d[1], d[2]) for d in in_defs],
        out_specs=[pl.BlockSpec(d[2], d[3]) for d in out_defs],
        out_shape=[jax.ShapeDtypeStruct(d[0], d[1]) for d in out_defs],
        compiler_params=_cparams(),
    )(*[d[0] for d in in_defs])
    return res


def _grid_call(name, grid, in_defs, out_defs, fn, acc_outs=()):
    ni = len(in_defs)

    def body(*refs):
        outs = fn(*[r[...] for r in refs[:ni]])
        first = pl.program_id(len(grid) - 1) == 0
        for idx, (o_ref, o) in enumerate(zip(refs[ni:], outs)):
            if idx in acc_outs:
                @pl.when(first)
                def _(o_ref=o_ref, o=o):
                    o_ref[...] = o.astype(o_ref.dtype)

                @pl.when(jnp.logical_not(first))
                def _(o_ref=o_ref, o=o):
                    o_ref[...] += o.astype(o_ref.dtype)
            else:
                o_ref[...] = o.astype(o_ref.dtype)

    return pl.pallas_call(
        body, name=name, grid=grid,
        in_specs=[pl.BlockSpec(d[1], d[2]) for d in in_defs],
        out_specs=[pl.BlockSpec(d[2], d[3]) for d in out_defs],
        out_shape=[jax.ShapeDtypeStruct(d[0], d[1]) for d in out_defs],
        compiler_params=_cparams(),
    )(*[d[0] for d in in_defs])


def _rt(a, tm):
    return (a, (tm, a.shape[1]), lambda i: (i, 0))


def _full(a):
    return (a, a.shape, lambda i: (0,) * a.ndim)


def _rt_out(T, n, dt, tm):
    return ((T, n), dt, (tm, n), lambda i: (i, 0))


def _ps_out(T, n, tm):
    return ((T // tm * 8, n), F32, (8, n), lambda i: (i, 0))


def _out_proj_ln(name, a, w, h_in, g, b):
    def epi(accs, h_t, g_t, b_t):
        z = ALPHA * h_t + accs[0]
        return [z, _layer_norm(z, g_t, b_t)]
    return _mm_simple(name, a, w, "nn", F32, tm=512, tn=D_MODEL, tk=FFN_HIDDEN, extras=(h_in, g, b), epilogue=epi, n_out=2)


def _ln_bwd(name, dh, z, g):
    T = dh.shape[0]
    tm = _tile(T, 512, 8)

    def fn(dh_t, z_t, g_t):
        mu = jnp.mean(z_t, axis=-1, keepdims=True)
        zc = z_t - mu
        var = jnp.mean(zc * zc, axis=-1, keepdims=True)
        rstd = lax.rsqrt(var + LN_EPS)
        xhat = zc * rstd
        dxh = dh_t * g_t
        m1 = jnp.mean(dxh, axis=-1, keepdims=True)
        m2 = jnp.mean(dxh * xhat, axis=-1, keepdims=True)
        dz = rstd * (dxh - m1 - xhat * m2)
        return [dz, _bcast8(jnp.sum(dh_t * xhat, axis=0, keepdims=True)), _bcast8(jnp.sum(dh_t, axis=0, keepdims=True))]

    dz, pg, pb = _rows(name, T, tm, [_rt(dh, tm), _rt(z, tm), _full(g)],
                       [_rt_out(T, D_MODEL, F32, tm), _ps_out(T, D_MODEL, tm), _ps_out(T, D_MODEL, tm)], fn)
    return dz, _psum_rows(pg), _psum_rows(pb)


def _ffn_fwd(l, h, w_in, w_out, g, b):
    T = h.shape[0]
    tm, tn = _tile(T, 512, 8), 1408
    nj = FFN_HIDDEN // tn
    grid = (nj, T // tm, 1)

    def epi(accs):
        gg, uu = accs
        return [gg, uu, _silu(gg) * uu]

    G, U, A = _mm(
        f"ffn{l}_in", grid, [(h, (tm, D_MODEL), lambda j, i, k: (i, 0))],
        [(w_in, (D_MODEL, tn), lambda j, i, k: (0, j)), (w_in, (D_MODEL, tn), lambda j, i, k: (0, j + nj))],
        [(0, 0, 0, "nn"), (0, 1, 1, "nn")], 2, (tm, tn), [],
        [((T, FFN_HIDDEN), F32, (tm, tn), lambda j, i, k: (i, j)),
         ((T, FFN_HIDDEN), F32, (tm, tn), lambda j, i, k: (i, j)),
         ((T, FFN_HIDDEN), BF16, (tm, tn), lambda j, i, k: (i, j))], epi)
    z, h_out = _out_proj_ln(f"ffn{l}_out", A, w_out, h, g, b)
    return h_out, (h, G, U, A, z)


def _ffn_bwd(l, dh_out, saved, w_in, w_out, g):
    h, G, U, A, z = saved
    T = h.shape[0]
    dz, dg, db = _ln_bwd(f"ffn{l}_lnb", dh_out, z, g)

    def epi(accs, g_t, u_t):
        da = accs[0]
        return [da * u_t * _dsilu(g_t), da * _silu(g_t)]

    dG, dU = _mm_simple(f"ffn{l}_dA", dz, w_out, "nt", BF16, tm=512, tn=1408, tk=1024, extras=(G, U), epilogue=epi, n_out=2,
                        j_outer=True)
    dw_out = _mm_simple(f"ffn{l}_dWout", A, dz, "tn", BF16, tm=1408, tn=1024, tk=512)
    dw_g = _mm_simple(f"ffn{l}_dWg", h, dG, "tn", BF16, tm=1024, tn=1408, tk=512)
    dw_u = _mm_simple(f"ffn{l}_dWu", h, dU, "tn", BF16, tm=1024, tn=1408, tk=512)
    tm, tk = _tile(T, 256, 8), FFN_HIDDEN
    dh = _mm(
        f"ffn{l}_dX", (T // tm, 1, 1),
        [(dG, (tm, tk), lambda i, j, k: (i, 0)), (dU, (tm, tk), lambda i, j, k: (i, 0))],
        [(w_in, (D_MODEL, tk), lambda i, j, k: (0, 0)), (w_in, (D_MODEL, tk), lambda i, j, k: (0, 1))],
        [(0, 0, 0, "nt"), (1, 1, 0, "nt")], 1, (tm, D_MODEL),
        [(dz, (tm, D_MODEL), lambda i, j, k: (i, 0))],
        [((T, D_MODEL), F32, (tm, D_MODEL), lambda i, j, k: (i, 0))],
        lambda accs, dz_t: [accs[0] + ALPHA * dz_t])[0]
    return dh, dict(w_in=jnp.concatenate([dw_g, dw_u], axis=1), w_out=dw_out, ln_g=dg, ln_b=db)


def _s5_matrices(lam_re, lam_im, log_step, b_re, b_im, c_re, c_im):
    L, hp = S5_CHUNK, lax.Precision.HIGHEST
    out = {}
    mt_total = 0.0
    for d in range(2):
        lam = lax.complex(lam_re[d], lam_im[d])
        step = jnp.exp(log_step[d])[:, None]
        lam_dt = lam * step
        lam_bar = jnp.exp(lam_dt)
        b_bar = ((lam_bar - 1.0) / lam)[..., None] * lax.complex(b_re[d], b_im[d])
        c = lax.complex(c_re[d], c_im[d])
        pw = jnp.exp(lam_dt[None] * jnp.arange(L + 1, dtype=F32)[:, None, None])
        kj = jnp.einsum("ghp,jgp,gpk->gjhk", c, pw[:L], b_bar, precision=hp).real
        lag = np.arange(L)[None, :] - np.arange(L)[:, None]
        lag = lag if d == 0 else -lag
        sel = np.stack([(lag == j) for j in range(L)]).astype(np.float32)
        mt = jnp.einsum("jst,gjab->gsbta", sel, kj, precision=hp).reshape(S5_GROUPS, 16 * L, 16 * L)
        mt_total = mt_total + mt
        pw_dist = jnp.flip(pw[:L], 0) if d == 0 else pw[:L]
        pc = pw_dist.transpose(1, 0, 2)[:, :, None, :] * b_bar.transpose(0, 2, 1)[:, None, :, :]
        pm = jnp.concatenate([pc.real, pc.imag], axis=-1).reshape(S5_GROUPS, 16 * L, 2 * S5_STATE)
        pw_read = pw[1:] if d == 0 else jnp.flip(pw[1:], 0)
        qc = c[:, None, :, :] * pw_read.transpose(1, 0, 2)[:, :, None, :]
        qm = jnp.concatenate([qc.real, -qc.imag], axis=-1).reshape(S5_GROUPS, 16 * L, 2 * S5_STATE).transpose(0, 2, 1)
        a = pw[L]
        out[f"p{d}"], out[f"q{d}"] = pm, qm
        out[f"a{d}"] = jnp.concatenate([a.real, a.imag], axis=-1)
    out["mt"] = mt_total
    return out


def _s5_apow(lam_re, lam_im, log_step, n_steps, conj):
    lam_dt = lax.complex(lam_re, lam_im) * jnp.exp(log_step)[..., None]
    k = (S5_CHUNK * 2.0 ** jnp.arange(n_steps, dtype=F32))[None, None, :, None]
    a = jnp.exp(lam_dt[:, :, None, :] * k)
    re, im = a.real, (-a.imag if conj else a.imag)
    return jnp.stack([jnp.concatenate([re, re], -1), jnp.concatenate([-im, im], -1)], axis=3)


def _shift_rows(x, s, down):
    n = x.shape[0]
    if s >= n:
        return jnp.zeros_like(x)
    if s % 8 == 0:
        z = jnp.zeros((s, x.shape[1]), x.dtype)
        return jnp.concatenate([z, x[:n - s]], axis=0) if down else jnp.concatenate([x[s:], z], axis=0)
    row = lax.broadcasted_iota(jnp.int32, x.shape, 0)
    if down:
        return jnp.where(row >= s, pltpu.roll(x, s, 0), 0.0)
    return jnp.where(row < n - s, pltpu.roll(x, n - s, 0), 0.0)


def _cmul(x, a1, a2):
    return x * a1 + pltpu.roll(x, S5_STATE, 1) * a2


def _chunk_scan(s, apow_ref, down):
    n = s.shape[0]
    k, sh = 0, 1
    while sh < n:
        s = s + _cmul(_shift_rows(s, sh, down), apow_ref[k, 0:1, :], apow_ref[k, 1:2, :])
        k, sh = k + 1, sh * 2
    return s


def _s5_scan_fwd(name, ug, mats, apow):
    G, C, W = ug.shape
    n_steps = apow.shape[2]

    def body(u_ref, mt_ref, p0_ref, p1_ref, q0_ref, q1_ref, ap0_ref, ap1_ref, y_ref, h0_ref, h1_ref):
        u = u_ref[...]
        s0 = jnp.dot(u, p0_ref[...], preferred_element_type=F32)
        s1 = jnp.dot(u, p1_ref[...], preferred_element_type=F32)
        h0 = _shift_rows(_chunk_scan(s0, ap0_ref, True), 1, True)
        h1 = _shift_rows(_chunk_scan(s1, ap1_ref, False), 1, False)
        y = jnp.dot(u, mt_ref[...], preferred_element_type=F32)
        y += jnp.dot(h0.astype(BF16), q0_ref[...], preferred_element_type=F32)
        y += jnp.dot(h1.astype(BF16), q1_ref[...], preferred_element_type=F32)
        y_ref[...] = y.astype(y_ref.dtype)
        h0_ref[...] = h0
        h1_ref[...] = h1

    def gspec(shape):
        return pl.BlockSpec((None,) + shape, lambda g: (g,) + (0,) * len(shape))

    ap0, ap1 = apow[0], apow[1]
    return pl.pallas_call(
        body, name=name, grid=(G,),
        in_specs=[gspec((C, W)), gspec((W, W)), gspec((W, 128)), gspec((W, 128)), gspec((128, W)), gspec((128, W)),
                  gspec((n_steps, 2, 128)), gspec((n_steps, 2, 128))],
        out_specs=[gspec((C, W)), gspec((C, 128)), gspec((C, 128))],
        out_shape=[jax.ShapeDtypeStruct((G, C, W), BF16), jax.ShapeDtypeStruct((G, C, 128), F32),
                   jax.ShapeDtypeStruct((G, C, 128), F32)],
        compiler_params=_cparams(),
    )(ug, mats["mt"].astype(BF16), mats["p0"].astype(BF16), mats["p1"].astype(BF16),
      mats["q0"].astype(BF16), mats["q1"].astype(BF16), ap0, ap1)


def _s5_scan_bwd(name, dyg, ug, h0, h1, mats, apow_conj):
    G, C, W = ug.shape
    n_steps = apow_conj.shape[2]

    def body(dy_ref, u_ref, h0_ref, h1_ref, mt_ref, p0_ref, p1_ref, q0_ref, q1_ref, ap0_ref, ap1_ref,
             du_ref, dmt_ref, dp0_ref, dp1_ref, dq0_ref, dq1_ref, da_ref):
        dy, u = dy_ref[...], u_ref[...]
        nt, tn = (_DN["nt"], ((), ())), (_DN["tn"], ((), ()))
        dh0 = lax.dot_general(dy, q0_ref[...], nt, preferred_element_type=F32)
        dh1 = lax.dot_general(dy, q1_ref[...], nt, preferred_element_type=F32)
        ds0 = _chunk_scan(_shift_rows(dh0, 1, False), ap0_ref, False)
        ds1 = _chunk_scan(_shift_rows(dh1, 1, True), ap1_ref, True)
        ds0b, ds1b = ds0.astype(BF16), ds1.astype(BF16)
        du = lax.dot_general(dy, mt_ref[...], nt, preferred_element_type=F32)
        du += lax.dot_general(ds0b, p0_ref[...], nt, preferred_element_type=F32)
        du += lax.dot_general(ds1b, p1_ref[...], nt, preferred_element_type=F32)
        du_ref[...] = du.astype(du_ref.dtype)
        dmt_ref[...] = lax.dot_general(u, dy, tn, preferred_element_type=F32)
        dp0_ref[...] = lax.dot_general(u, ds0b, tn, preferred_element_type=F32)
        dp1_ref[...] = lax.dot_general(u, ds1b, tn, preferred_element_type=F32)
        h0v, h1v = h0_ref[...], h1_ref[...]
        dq0_ref[...] = lax.dot_general(h0v.astype(BF16), dy, tn, preferred_element_type=F32)
        dq1_ref[...] = lax.dot_general(h1v.astype(BF16), dy, tn, preferred_element_type=F32)
        rows = [jnp.sum(ds0 * h0v, axis=0, keepdims=True), jnp.sum(ds0 * pltpu.roll(h0v, S5_STATE, 1), axis=0, keepdims=True),
                jnp.sum(ds1 * h1v, axis=0, keepdims=True), jnp.sum(ds1 * pltpu.roll(h1v, S5_STATE, 1), axis=0, keepdims=True)]
        da_ref[...] = _stack_rows8(rows)

    def gspec(shape):
        return pl.BlockSpec((None,) + shape, lambda g: (g,) + (0,) * len(shape))

    f32s = lambda *s: jax.ShapeDtypeStruct((G,) + s, F32)
    return pl.pallas_call(
        body, name=name, grid=(G,),
        in_specs=[gspec((C, W)), gspec((C, W)), gspec((C, 128)), gspec((C, 128)), gspec((W, W)), gspec((W, 128)),
                  gspec((W, 128)), gspec((128, W)), gspec((128, W)), gspec((n_steps, 2, 128)), gspec((n_steps, 2, 128))],
        out_specs=[gspec((C, W)), gspec((W, W)), gspec((W, 128)), gspec((W, 128)), gspec((128, W)), gspec((128, W)),
                   gspec((8, 128))],
        out_shape=[jax.ShapeDtypeStruct((G, C, W), BF16), f32s(W, W), f32s(W, 128), f32s(W, 128), f32s(128, W),
                   f32s(128, W), f32s(8, 128)],
        compiler_params=_cparams(),
    )(dyg, ug, h0, h1, mats["mt"].astype(BF16), mats["p0"].astype(BF16), mats["p1"].astype(BF16),
      mats["q0"].astype(BF16), mats["q1"].astype(BF16), apow_conj[0], apow_conj[1])


S5_LANE_GROUPS = 128 // S5_GROUP


def _group_select():
    e = np.zeros((S5_LANE_GROUPS, S5_CHUNK, 128, S5_CHUNK, S5_GROUP), np.float32)
    for j in range(S5_LANE_GROUPS):
        for l in range(S5_CHUNK):
            for hh in range(S5_GROUP):
                e[j, l, S5_GROUP * j + hh, l, hh] = 1.0
    return jnp.asarray(e.reshape(S5_LANE_GROUPS, S5_CHUNK * 128, S5_CHUNK * S5_GROUP), BF16)


def _to_groups(name, a):
    T = a.shape[0]
    C = T // S5_CHUNK
    cb = _tile(C, 256, 8)
    sel = _group_select()

    def body(x_ref, e_ref, o_ref):
        xcat = jnp.concatenate([x_ref[pl.ds(l, cb, stride=S5_CHUNK), :].astype(BF16) for l in range(S5_CHUNK)], axis=1)
        for j in range(S5_LANE_GROUPS):
            o_ref[j] = jnp.dot(xcat, e_ref[j], preferred_element_type=F32).astype(BF16)

    return pl.pallas_call(
        body, name=name, grid=(S5_GROUPS // S5_LANE_GROUPS, C // cb),
        in_specs=[pl.BlockSpec((cb * S5_CHUNK, 128), lambda b, r: (r, b)), pl.BlockSpec(sel.shape, lambda b, r: (0, 0, 0))],
        out_specs=pl.BlockSpec((S5_LANE_GROUPS, cb, S5_CHUNK * S5_GROUP), lambda b, r: (b, r, 0)),
        out_shape=jax.ShapeDtypeStruct((S5_GROUPS, C, S5_CHUNK * S5_GROUP), BF16), compiler_params=_cparams(),
    )(a, sel)


def _from_groups(name, a, extras, fn, n_out):
    G, C, W = a.shape
    T = C * S5_CHUNK
    cb = _tile(C, 256, 8)
    sel = _group_select()
    ne = len(extras)

    def body(*refs):
        y_ref, e_ref = refs[:2]
        e_refs, o_refs = refs[2:2 + ne], refs[2 + ne:]
        ycat = jnp.zeros((cb, S5_CHUNK * 128), F32)
        for j in range(S5_LANE_GROUPS):
            ycat = ycat + lax.dot_general(y_ref[j], e_ref[j], (_DN["nt"], ((), ())), preferred_element_type=F32)
        for l in range(S5_CHUNK):
            rows = pl.ds(l, cb, stride=S5_CHUNK)
            ex = [r[...] if r.shape[0] == 1 else r[rows, :] for r in e_refs]
            for o_ref, o in zip(o_refs, fn(ycat[:, l * 128:(l + 1) * 128], *ex)):
                o_ref[rows, :] = o

    col = pl.BlockSpec((cb * S5_CHUNK, 128), lambda b, r: (r, b))
    return pl.pallas_call(
        body, name=name, grid=(G // S5_LANE_GROUPS, C // cb),
        in_specs=[pl.BlockSpec((S5_LANE_GROUPS, cb, W), lambda b, r: (b, r, 0)),
                  pl.BlockSpec(sel.shape, lambda b, r: (0, 0, 0))]
        + [pl.BlockSpec((1, 128), lambda b, r: (0, b)) if e.shape[0] == 1 else col for e in extras],
        out_specs=[col] * n_out,
        out_shape=[jax.ShapeDtypeStruct((T, G * S5_GROUP), F32)] * n_out, compiler_params=_cparams(),
    )(a, sel, *extras)


_S5_SSM = ("lam_re", "lam_im", "log_step", "b_re", "b_im", "c_re", "c_im")


def _s5_prepare(W, T):
    sp = tuple(W["s5_" + k] for k in _S5_SSM)
    mats, mats_vjp = jax.vjp(jax.vmap(_s5_matrices), *sp)
    n_steps = max(1, int(math.log2(T // S5_CHUNK)))
    apow = jax.vmap(lambda a, b, c: _s5_apow(a, b, c, n_steps, False))(*sp[:3])
    apow_c = jax.vmap(lambda a, b, c: _s5_apow(a, b, c, n_steps, True))(*sp[:3])
    return mats, mats_vjp, apow, apow_c


def _s5_fwd(l, h, w, small, g, b, mats, apow, apow_c):
    u = _mm_simple(f"s5{l}_in", h, w["w_in"], "nn", F32)
    ug = _to_groups(f"s5{l}_togroups", u)
    yg, h0, h1 = _s5_scan_fwd(f"s5{l}_scan", ug, mats, apow)
    d_row, bglu_row = small["d"].reshape(1, -1), small["b_glu"].reshape(1, -1)
    yssm, y1 = _from_groups(f"s5{l}_fromgroups", yg, [u, d_row], lambda y, u_t, d_t: [y, _gelu(y + d_t * u_t)], 2)

    def epi(accs, y1_t, bg_t):
        gate = _sigmoid(accs[0] + bg_t)
        return [y1_t * gate, gate]

    y2, gate = _mm_simple(f"s5{l}_glu", y1, w["w_glu"], "nn", None, extras=(y1, bglu_row), epilogue=epi, n_out=2,
                          out_dtypes=[BF16, F32])
    z, h_out = _out_proj_ln(f"s5{l}_out", y2, w["w_out"], h, g, b)
    return h_out, (h, u, ug, yssm, y1, y2, gate, z, h0, h1, mats, apow_c, d_row)


def _s5_bwd(l, dh_out, saved, w, small, g):
    h, u, ug, yssm, y1, y2, gate, z, h0, h1, mats, apow_c, d_row = saved
    T = h.shape[0]
    dz, dg, db = _ln_bwd(f"s5{l}_lnb", dh_out, z, g)

    def epi1(accs, y1_t, gate_t):
        dy2 = accs[0]
        dpre = dy2 * y1_t * gate_t * (1.0 - gate_t)
        return [dpre, dy2 * gate_t, _bcast8(jnp.sum(dpre, axis=0, keepdims=True))]

    dpre, dy1a, pbg = _mm_simple(f"s5{l}_dy2", dz, w["w_out"], "nt", None, extras=(y1, gate), epilogue=epi1, n_out=2,
                                 out_dtypes=[BF16, F32], psum_outs=1)
    dw_out = _mm_simple(f"s5{l}_dWout", y2, dz, "tn", BF16, tm=1024, tn=1024, tk=512)

    def epi2(accs, dy1a_t, yssm_t, u_t, d_t):
        dy1 = accs[0] + dy1a_t
        dy = dy1 * _dgelu(yssm_t + d_t * u_t)
        return [dy, dy * d_t, _bcast8(jnp.sum(dy * u_t, axis=0, keepdims=True))]

    dy, du_skip, pdd = _mm_simple(f"s5{l}_dy1", dpre, w["w_glu"], "nt", None, extras=(dy1a, yssm, u, d_row), epilogue=epi2,
                                  n_out=2, out_dtypes=[F32, F32], psum_outs=1)
    dw_glu = _mm_simple(f"s5{l}_dWglu", y1, dpre, "tn", BF16, tm=1024, tn=1024, tk=512)
    dyg = _to_groups(f"s5{l}_togroups_b", dy)
    dug, dmt, dp0, dp1, dq0, dq1, da = _s5_scan_bwd(f"s5{l}_scanb", dyg, ug, h0, h1, mats, apow_c)
    du = _from_groups(f"s5{l}_fromgroups_b", dug, [du_skip], lambda y, skip: [y + skip], 1)[0]

    def a_grad(p, q):
        return jnp.concatenate([p[:, :S5_STATE] + p[:, S5_STATE:], q[:, S5_STATE:] - q[:, :S5_STATE]], axis=-1)

    dmats = dict(mt=dmt, p0=dp0, p1=dp1, q0=dq0, q1=dq1, a0=a_grad(da[:, 0], da[:, 1]), a1=a_grad(da[:, 2], da[:, 3]))
    dh = _mm_simple(f"s5{l}_dX", du, w["w_in"], "nt", F32, extras=(dz,), epilogue=lambda accs, dz_t: [accs[0] + ALPHA * dz_t])
    dw_in = _mm_simple(f"s5{l}_dWin", h, du, "tn", BF16, tm=1024, tn=1024, tk=512)
    grads = dict(w_in=dw_in, w_glu=dw_glu, w_out=dw_out, d=_psum_rows(pdd), b_glu=_psum_rows(pbg), ln_g=dg, ln_b=db)
    return dh, grads, dmats


def _gla_levels(lc):
    ms, m = [], lc // 2
    while m >= 1:
        ms.append(m)
        m //= 2
    return ms


def _gla_scan_matrix(lc, rev):
    r = np.arange(lc)[:, None]
    t = np.arange(lc)[None, :]
    blocks = []
    for m in _gla_levels(lc):
        same = (r // m) == (t // m)
        upper = ((r // m) % 2) == 1
        blocks.append(same & np.where(upper, t >= r, t < r))
    blocks.append(t >= r)
    blocks.append(t < r)
    if rev:
        blocks = [blk[::-1, ::-1] for blk in blocks]
    return np.concatenate(blocks, axis=1).astype(np.float32)


def _gla_gates(z, lb):
    sig = _sigmoid(z)
    ls = jnp.minimum(z, 0.0) - jnp.log(1.0 + jnp.exp(-jnp.abs(z)))
    a = jnp.log(lb)
    bb = jnp.log(1.0 - lb) + ls
    lf = jnp.maximum(a, bb) + jnp.log(1.0 + jnp.exp(-jnp.abs(a - bb)))
    return lf, (1.0 - lb) * (1.0 - sig), sig


def _gla_cumsum(lf, rev):
    b, sh = lf, 1
    while sh < lf.shape[0]:
        b = b + _shift_rows(b, sh, not rev)
        sh *= 2
    return b


def _gla_bref(b, m, rev):
    lc, n = b.shape
    idx = m if rev else m - 1
    if 2 * m >= 8:
        nb = lc // (2 * m)
        b3 = b.reshape(nb, 2 * m, n)
        return jnp.broadcast_to(b3[:, idx:idx + 1, :], (nb, 2 * m, n)).reshape(lc, n)
    row = lax.broadcasted_iota(jnp.int32, b.shape, 0)
    j = row & (2 * m - 1)
    out = b
    for jj in range(2 * m):
        if jj != idx:
            out = jnp.where(j == jj, pltpu.roll(b, (jj - idx) % lc, 0), out)
    return out


def _gla_masks(lc, rev):
    r = np.arange(lc)
    mq, bm = [], [np.eye(lc)]
    for m in _gla_levels(lc):
        isq = ((r // m) % 2) == (0 if rev else 1)
        mq.append(np.broadcast_to(isq[:, None], (lc, HG_DIM)))
        bm.append((r[:, None] // (2 * m)) == (r[None, :] // (2 * m)))
    return jnp.asarray(np.stack(mq), F32), jnp.asarray(np.stack(bm), F32)


def _gla_chunk(q, k, lf, rev, mq_ref, bm_ref):
    lc = q.shape[0]
    nt = (_DN["nt"], ((), ()))
    b = _gla_cumsum(lf, rev)
    qb, kb = q.astype(BF16), k.astype(BF16)
    sc = bm_ref[0] * lax.dot_general(qb, kb, nt, preferred_element_type=F32)
    levels = []
    for i, m in enumerate(_gla_levels(lc)):
        mq = mq_ref[i]
        mk = 1.0 - mq
        w = jnp.exp((b - _gla_bref(b, m, rev)) * (mq - mk))
        wq, wk = w * mq, w * mk
        xf, yf = q * wq, k * wk
        xb, yb = xf.astype(BF16), yf.astype(BF16)
        sc = sc + bm_ref[i + 1] * lax.dot_general(xb, yb, nt, preferred_element_type=F32)
        levels.append((wq, wk, xf, yf, xb, yb))
    return b, sc, levels


def _hg_specs(T, lc, rev, backward):
    nc = T // lc
    cc = (lambda c: nc - 1 - c) if rev != backward else (lambda c: c)
    zcol = HG_HEADS * (2 if rev else 1)
    q_spec = pl.BlockSpec((lc, HG_DIM), lambda h, c: (cc(c), h))
    z_spec = pl.BlockSpec((lc, HG_DIM), lambda h, c: (cc(c), zcol + h))
    v_spec = pl.BlockSpec((lc, HG_DIM), lambda h, c: (cc(c), 3 * HG_HEADS + h))
    lb_spec = pl.BlockSpec((1, HG_DIM), lambda h, c: (0, h))
    st_spec = pl.BlockSpec((None, None, HG_DIM, HG_DIM), lambda h, c: (h, cc(c), 0, 0))
    return nc, q_spec, z_spec, v_spec, lb_spec, st_spec


def _gla_fwd(name, proj, lb_row, rev):
    T = proj.shape[0]
    lc = _tile(T, HG_CHUNK, 8)
    nc, q_spec, z_spec, v_spec, lb_spec, st_spec = _hg_specs(T, lc, rev, False)
    last = 0 if rev else lc - 1
    mq, bm = _gla_masks(lc, rev)
    const3 = lambda a: pl.BlockSpec(a.shape, lambda h, c: (0, 0, 0))

    def body(q_ref, z_ref, v_ref, lb_ref, mq_ref, bm_ref, o_ref, st_ref, st_s):
        @pl.when(pl.program_id(1) == 0)
        def _():
            st_s[...] = jnp.zeros(st_s.shape, F32)

        q = _silu(q_ref[...])
        lf, k, _ = _gla_gates(z_ref[...], lb_ref[...])
        vb = v_ref[...].astype(BF16)
        b, sc, _ = _gla_chunk(q, k, lf, rev, mq_ref, bm_ref)
        st0 = st_s[...]
        st_ref[...] = st0
        bl = b[last:last + 1, :]
        o = jnp.dot(sc.astype(BF16), vb, preferred_element_type=F32)
        o += lax.dot_general((q * jnp.exp(b)).astype(BF16), st0.astype(BF16), (_DN["nt"], ((), ())), preferred_element_type=F32)
        o_ref[...] = o
        kd = (k * jnp.exp(bl - b)).astype(BF16)
        st_s[...] = st0 * jnp.exp(bl) + lax.dot_general(vb, kd, (_DN["tn"], ((), ())), preferred_element_type=F32)

    return pl.pallas_call(
        body, name=name, grid=(HG_HEADS, nc),
        in_specs=[q_spec, z_spec, v_spec, lb_spec, const3(mq), const3(bm)], out_specs=[q_spec, st_spec],
        out_shape=[jax.ShapeDtypeStruct((T, D_MODEL), F32), jax.ShapeDtypeStruct((HG_HEADS, nc, HG_DIM, HG_DIM), F32)],
        scratch_shapes=[pltpu.VMEM((HG_DIM, HG_DIM), F32)], compiler_params=_cparams(),
    )(proj, proj, proj, lb_row, mq, bm)


def _gla_bwd(name, proj, lb_row, do, st, rev):
    T = proj.shape[0]
    lc = _tile(T, HG_CHUNK, 8)
    nc, q_spec, z_spec, v_spec, lb_spec, st_spec = _hg_specs(T, lc, rev, True)
    wall = jnp.asarray(_gla_scan_matrix(lc, rev), BF16)
    last = 0 if rev else lc - 1
    mq, bm = _gla_masks(lc, rev)
    const3 = lambda a: pl.BlockSpec(a.shape, lambda h, c: (0, 0, 0))

    def body(q_ref, z_ref, v_ref, lb_ref, do_ref, st_ref, wall_ref, mq_ref, bm_ref, dq_ref, dz_ref, dv_ref, dlb_ref, dst_s):
        first = pl.program_id(1) == 0

        @pl.when(first)
        def _():
            dst_s[...] = jnp.zeros(dst_s.shape, F32)
            dlb_ref[...] = jnp.zeros(dlb_ref.shape, F32)

        nn, nt, tn = (_DN["nn"], ((), ())), (_DN["nt"], ((), ())), (_DN["tn"], ((), ()))
        dot = functools.partial(lax.dot_general, preferred_element_type=F32)
        qr, z, lb = q_ref[...], z_ref[...], lb_ref[...]
        q = _silu(qr)
        lf, k, sig = _gla_gates(z, lb)
        vb = v_ref[...].astype(BF16)
        b, sc, levels = _gla_chunk(q, k, lf, rev, mq_ref, bm_ref)
        st0, dst = st_ref[...], dst_s[...]
        st0b, dstb = st0.astype(BF16), dst.astype(BF16)
        dob = do_ref[...].astype(BF16)
        bl = b[last:last + 1, :]
        eb, ebl, ekd = jnp.exp(b), jnp.exp(bl), jnp.exp(bl - b)
        qe, kd = q * eb, k * ekd
        kdb = kd.astype(BF16)
        dsc = dot(dob, vb, nt)
        dv_ref[...] = dot(sc.astype(BF16), dob, tn) + dot(kdb, dstb, nt)
        dqe = dot(dob, st0b, nn)
        dkd = dot(vb, dstb, nn)
        dq = dqe * eb
        dk = dkd * ekd
        zs = []
        dsd = (bm_ref[0] * dsc).astype(BF16)
        dq += dot(dsd, k.astype(BF16), nn)
        dk += dot(dsd, q.astype(BF16), tn)
        for i, (wq, wk, xf, yf, xb, yb) in enumerate(levels):
            dsl = (bm_ref[i + 1] * dsc).astype(BF16)
            dx = dot(dsl, yb, nn)
            dy = dot(dsl, xb, tn)
            dq += dx * wq
            dk += dy * wk
            zs.append((dx * xf + dy * yf).astype(BF16))
        zs.append((dqe * qe).astype(BF16))
        zs.append((dkd * kd).astype(BF16))
        zl = jnp.sum(dst * st0, axis=0, keepdims=True) * ebl
        dlf = dot(wall_ref[...], jnp.concatenate(zs, axis=0), nn) + zl
        dst_s[...] = dst * ebl + dot(dob, qe.astype(BF16), tn)
        inv_f = jnp.exp(-lf)
        one_sig = 1.0 - sig
        dz_ref[...] = (dlf * inv_f - dk) * (1.0 - lb) * sig * one_sig
        dq_ref[...] = dq * _dsilu(qr)
        dlb_ref[...] += _bcast8(jnp.sum((dlf * inv_f - dk) * one_sig, axis=0, keepdims=True))

    big = jax.ShapeDtypeStruct((T, D_MODEL), F32)
    return pl.pallas_call(
        body, name=name, grid=(HG_HEADS, nc),
        in_specs=[q_spec, z_spec, v_spec, lb_spec, q_spec, st_spec, pl.BlockSpec(wall.shape, lambda h, c: (0, 0)),
                  const3(mq), const3(bm)],
        out_specs=[q_spec, q_spec, q_spec, pl.BlockSpec((None, 8, HG_DIM), lambda h, c: (h, 0, 0))],
        out_shape=[big, big, big, jax.ShapeDtypeStruct((HG_HEADS, 8, HG_DIM), F32)],
        scratch_shapes=[pltpu.VMEM((HG_DIM, HG_DIM), F32)], compiler_params=_cparams(),
    )(proj, proj, proj, lb_row, do, st, wall, mq, bm)


def _hg_lower_bounds(hg_lower_bound, layer):
    lbs = jax.nn.softmax(hg_lower_bound, axis=0)
    lbs = jnp.cumsum(lbs, axis=0) - lbs[0]
    return lbs[layer].reshape(1, -1)


def _hg_post(o_fw, o_bw, g_raw, ng):
    outs, ons, os_, rs = [], [], [], []
    for hd in range(o_fw.shape[1] // HG_DIM):
        sl = slice(hd * HG_DIM, (hd + 1) * HG_DIM)
        o = o_fw[:, sl] + o_bw[:, sl]
        r = lax.rsqrt(jnp.mean(o * o, axis=-1, keepdims=True) + RMS_EPS)
        on = o * r * ng
        outs.append(on * _silu(g_raw[:, sl]))
        ons.append(on)
        os_.append(o)
        rs.append(r)
    return outs, ons, os_, rs


def _hg_fwd(l, h, w, small, g, b):
    T = h.shape[0]
    tm = _tile(T, 512, 8)
    proj = _mm_simple(f"hg{l}_in", h, w["w_in"], "nn", F32, tn=1280, j_outer=True)
    lb_fn = lambda p: _hg_lower_bounds(p, l)
    lb_row, lb_vjp = jax.vjp(lb_fn, small["lower_bound"])
    o_fw, st_fw = _gla_fwd(f"hg{l}_gla_fw", proj, lb_row, False)
    o_bw, st_bw = _gla_fwd(f"hg{l}_gla_bw", proj, lb_row, True)
    ng = small["norm_g"].reshape(1, HG_DIM)

    def post(of_t, ob_t, g_t, ng_t):
        return [jnp.concatenate(_hg_post(of_t, ob_t, g_t, ng_t)[0], axis=1)]

    og = _rows(f"hg{l}_post", T, tm,
               [_rt(o_fw, tm), _rt(o_bw, tm), (proj, (tm, D_MODEL), lambda i: (i, 4)), _full(ng)],
               [_rt_out(T, D_MODEL, BF16, tm)], post)[0]
    z, h_out = _out_proj_ln(f"hg{l}_out", og, w["w_out"], h, g, b)
    return h_out, (h, proj, lb_row, lb_vjp, st_fw, st_bw, o_fw, o_bw, ng, og, z)


def _hg_bwd(l, dh_out, saved, w, small, g):
    h, proj, lb_row, lb_vjp, st_fw, st_bw, o_fw, o_bw, ng, og, z = saved
    T = h.shape[0]
    dz, dg, db = _ln_bwd(f"hg{l}_lnb", dh_out, z, g)
    tm = _tile(T, 512, 8)
    nn_tiles = D_MODEL // HG_DIM

    def epi(accs, of_t, ob_t, g_t, ng_t):
        dog = accs[0]
        _, ons, os_, rs = _hg_post(of_t, ob_t, g_t, ng_t)
        dos, dgs, dngs = [], [], []
        for hd in range(nn_tiles):
            sl = slice(hd * HG_DIM, (hd + 1) * HG_DIM)
            d, o, r = dog[:, sl], os_[hd], rs[hd]
            dgs.append(d * ons[hd] * _dsilu(g_t[:, sl]))
            don = d * _silu(g_t[:, sl])
            dngs.append(jnp.sum(don * o * r, axis=0, keepdims=True))
            dxn = don * ng_t
            dos.append(r * dxn - o * (r * r * r) * jnp.mean(dxn * o, axis=-1, keepdims=True))
        return [jnp.concatenate(dos, axis=1), jnp.concatenate(dgs, axis=1), _bcast8(jnp.concatenate(dngs, axis=1))]

    grid = (T // tm, 1, 1)
    row_map = lambda i, j, k: (i, 0)
    do, dg_raw, png = _mm(
        f"hg{l}_dog", grid, [(dz, (tm, D_MODEL), row_map)], [(w["w_out"], (D_MODEL, D_MODEL), lambda i, j, k: (0, 0))],
        [(0, 0, 0, "nt")], 1, (tm, D_MODEL),
        [(o_fw, (tm, D_MODEL), row_map), (o_bw, (tm, D_MODEL), row_map), (proj, (tm, D_MODEL), lambda i, j, k: (i, 4)),
         (ng, (1, HG_DIM), lambda i, j, k: (0, 0))],
        [((T, D_MODEL), F32, (tm, D_MODEL), row_map), ((T, D_MODEL), F32, (tm, D_MODEL), row_map),
         ((T // tm * 8, D_MODEL), F32, (8, D_MODEL), row_map)], epi)
    dw_out = _mm_simple(f"hg{l}_dWout", og, dz, "tn", BF16, tm=1024, tn=1024, tk=512)
    dq_f, dz_f, dv_f, dlb_f = _gla_bwd(f"hg{l}_glab_fw", proj, lb_row, do, st_fw, False)
    dq_b, dz_b, dv_b, dlb_b = _gla_bwd(f"hg{l}_glab_bw", proj, lb_row, do, st_bw, True)
    dproj = jnp.concatenate([dq_f + dq_b, dz_f, dz_b, dv_f + dv_b, dg_raw], axis=1).astype(BF16)
    dh = _mm_simple(f"hg{l}_dX", dproj, w["w_in"], "nt", F32, tm=256, tk=5 * D_MODEL, extras=(dz,),
                    epilogue=lambda accs, dz_t: [accs[0] + ALPHA * dz_t])
    dw_in = _mm_simple(f"hg{l}_dWin", h, dproj, "tn", BF16, tm=1024, tn=1280, tk=512)
    dlb_row = (dlb_f[:, 0, :] + dlb_b[:, 0, :]).reshape(1, D_MODEL)
    grads = dict(w_in=dw_in, w_out=dw_out, lower_bound=lb_vjp(dlb_row)[0],
                 norm_g=_psum_rows(png).reshape(nn_tiles, HG_DIM).sum(axis=0), ln_g=dg, ln_b=db)
    return dh, grads


MLA_W = 256
MLA_SCALE = (MLA_NOPE + MLA_ROPE) ** -0.5


def _swap_halves(a):
    n = a.shape[-1] // 2
    return jnp.concatenate([a[..., n:], a[..., :n]], axis=-1)


def _mla_ext_weights(w_in, w_q_b):
    w_in_ext = jnp.concatenate([w_in, _swap_halves(w_in[:, MLA_Q_LORA + MLA_KV_LORA:])], axis=1)
    wq = w_q_b.reshape(MLA_Q_LORA, MLA_HEADS, MLA_NOPE + MLA_ROPE)
    wq_ext = jnp.concatenate([wq, _swap_halves(wq[:, :, MLA_NOPE:])], axis=2).reshape(MLA_Q_LORA, MLA_HEADS * MLA_W)
    return w_in_ext, wq_ext


def _mla_ext_grads(dw_in_ext, dwq_ext):
    n_lat = MLA_Q_LORA + MLA_KV_LORA
    dw_in = jnp.concatenate([dw_in_ext[:, :n_lat], dw_in_ext[:, n_lat:n_lat + MLA_ROPE]
                             + _swap_halves(dw_in_ext[:, n_lat + MLA_ROPE:])], axis=1)
    dq = dwq_ext.reshape(MLA_Q_LORA, MLA_HEADS, MLA_W)
    dwq = jnp.concatenate([dq[:, :, :MLA_NOPE], dq[:, :, MLA_NOPE:MLA_NOPE + MLA_ROPE]
                           + _swap_halves(dq[:, :, MLA_NOPE + MLA_ROPE:])], axis=2)
    return dw_in, dwq.reshape(MLA_Q_LORA, MLA_HEADS * (MLA_NOPE + MLA_ROPE))


def _rope_table(positions):
    half = MLA_ROPE // 2
    inv_freq = 1.0 / (ROPE_THETA ** (jnp.arange(half, dtype=F32) * (2.0 / MLA_ROPE)))
    ang = positions.astype(F32)[:, None] * inv_freq
    cos, sin = jnp.cos(ang), jnp.sin(ang)
    return jnp.concatenate([cos, cos, -sin, sin], axis=1)


def _rope_sum(prod):
    return prod + pltpu.roll(prod, MLA_ROPE, 1)


def _low_half(a):
    lane = lax.broadcasted_iota(jnp.int32, a.shape, 1)
    return jnp.where(lane < MLA_ROPE, a, 0.0)


def _rms(x, g):
    r = lax.rsqrt(jnp.mean(x * x, axis=-1, keepdims=True) + RMS_EPS)
    return x * r * g


def _rms_bwd(x, g, dy):
    r = lax.rsqrt(jnp.mean(x * x, axis=-1, keepdims=True) + RMS_EPS)
    dxn = dy * g
    dx = r * dxn - x * (r * r * r) * jnp.mean(dxn * x, axis=-1, keepdims=True)
    return dx, jnp.sum(dy * x * r, axis=0, keepdims=True)


ATT_TILE = 512
ATT_BLOCK = 2048


def _lanes(col, n):
    return jnp.tile(col, (1, n // 128))


def _flash_fwd(name, q, k, v):
    H, T, _ = q.shape
    tq, tkb = _tile(T, ATT_TILE, 8), _tile(T, ATT_BLOCK, 128)
    ts = _tile(tkb, ATT_TILE, 128)
    nk, nsub = T // tkb, tkb // ts

    def body(q_ref, k_ref, v_ref, o_ref, lse_ref, m_s, acc_s):
        ki = pl.program_id(2)

        @pl.when(ki == 0)
        def _():
            m_s[...] = jnp.full(m_s.shape, -jnp.inf, F32)
            acc_s[...] = jnp.zeros(acc_s.shape, F32)

        qv = q_ref[...]
        m, acc = m_s[...], acc_s[...]
        for j in range(nsub):
            kj, vj = k_ref[j * ts:(j + 1) * ts, :], v_ref[j * ts:(j + 1) * ts, :]
            s = lax.dot_general(qv, kj, (_DN["nt"], ((), ())), preferred_element_type=F32)
            m_new = jnp.maximum(m, jnp.max(s, axis=-1, keepdims=True))
            p = jnp.exp(s - _lanes(m_new, ts)).astype(BF16)
            acc = _lanes(jnp.exp(m - m_new), 2 * MLA_V) * acc + jnp.dot(p, vj, preferred_element_type=F32)
            m = m_new
        m_s[...], acc_s[...] = m, acc

        @pl.when(ki == nk - 1)
        def _():
            l = acc[:, MLA_V:]
            o_ref[...] = acc[:, :MLA_V] / l
            lse_ref[...] = m + jnp.log(l)

    return pl.pallas_call(
        body, name=name, grid=(H, T // tq, nk),
        in_specs=[pl.BlockSpec((None, tq, MLA_W), lambda h, i, j: (h, i, 0)),
                  pl.BlockSpec((None, tkb, MLA_W), lambda h, i, j: (h, j, 0)),
                  pl.BlockSpec((None, tkb, 2 * MLA_V), lambda h, i, j: (h, j, 0))],
        out_specs=[pl.BlockSpec((tq, MLA_V), lambda h, i, j: (i, h)), pl.BlockSpec((tq, MLA_V), lambda h, i, j: (i, h))],
        out_shape=[jax.ShapeDtypeStruct((T, H * MLA_V), F32), jax.ShapeDtypeStruct((T, H * MLA_V), F32)],
        scratch_shapes=[pltpu.VMEM((tq, MLA_V), F32), pltpu.VMEM((tq, 2 * MLA_V), F32)],
        compiler_params=_cparams(),
    )(q, k, v)


def _flash_bwd(name, q, k, v, do, lse, delta):
    H, T, _ = q.shape
    nt, tn = (_DN["nt"], ((), ())), (_DN["tn"], ((), ()))
    tk, tqb = _tile(T, ATT_TILE, 128), _tile(T, ATT_BLOCK, 128)
    tqs = _tile(tqb, ATT_TILE, 128)
    nk, nqb, nsub = T // tk, T // tqb, tqb // tqs

    def body(q_ref, k_ref, v_ref, do_ref, lse_ref, dl_ref, dq_ref, dk_ref, dv_ref, dk_s, dv_s):
        ki, qb = pl.program_id(1), pl.program_id(2)

        @pl.when(jnp.logical_and(ki == 0, qb == 0))
        def _():
            dq_ref[...] = jnp.zeros(dq_ref.shape, F32)

        @pl.when(qb == 0)
        def _():
            dk_s[...] = jnp.zeros(dk_s.shape, F32)
            dv_s[...] = jnp.zeros(dv_s.shape, F32)

        kv, vv = k_ref[...], v_ref[...]
        dk, dv = dk_s[...], dv_s[...]
        for j in range(nsub):
            sl = slice(j * tqs, (j + 1) * tqs)
            qj, doj = q_ref[sl, :], do_ref[sl, :]
            s = lax.dot_general(qj, kv, nt, preferred_element_type=F32)
            dp = lax.dot_general(doj, vv, nt, preferred_element_type=F32)
            p = jnp.exp(s - _lanes(lse_ref[sl, :], tk))
            ds = (p * (dp - _lanes(dl_ref[sl, :], tk))).astype(BF16)
            dv = dv + lax.dot_general(p.astype(BF16), doj, tn, preferred_element_type=F32)
            dk = dk + lax.dot_general(ds, qj, tn, preferred_element_type=F32)
            rows = pl.ds(pl.multiple_of(qb * tqb + j * tqs, tqs), tqs)
            dq_ref[rows, :] += jnp.dot(ds, kv, preferred_element_type=F32)
        dk_s[...], dv_s[...] = dk, dv

        @pl.when(qb == nqb - 1)
        def _():
            dk_ref[...] = dk
            dv_ref[...] = dv

    return pl.pallas_call(
        body, name=name, grid=(H, nk, nqb),
        in_specs=[pl.BlockSpec((None, tqb, MLA_W), lambda h, i, j: (h, j, 0)),
                  pl.BlockSpec((None, tk, MLA_W), lambda h, i, j: (h, i, 0)),
                  pl.BlockSpec((None, tk, MLA_V), lambda h, i, j: (h, i, 0)),
                  pl.BlockSpec((tqb, MLA_V), lambda h, i, j: (j, h)),
                  pl.BlockSpec((tqb, MLA_V), lambda h, i, j: (j, h)),
                  pl.BlockSpec((tqb, MLA_V), lambda h, i, j: (j, h))],
        out_specs=[pl.BlockSpec((None, T, MLA_W), lambda h, i, j: (h, 0, 0)),
                   pl.BlockSpec((None, tk, MLA_W), lambda h, i, j: (h, i, 0)),
                   pl.BlockSpec((None, tk, MLA_V), lambda h, i, j: (h, i, 0))],
        out_shape=[jax.ShapeDtypeStruct((H, T, MLA_W), F32), jax.ShapeDtypeStruct((H, T, MLA_W), F32),
                   jax.ShapeDtypeStruct((H, T, MLA_V), F32)],
        scratch_shapes=[pltpu.VMEM((tk, MLA_W), F32), pltpu.VMEM((tk, MLA_V), F32)], compiler_params=_cparams(),
    )(q, k, v, do, lse, delta)


def _mla_fwd(l, h, w, small, g, b, cs):
    T = h.shape[0]
    tm = _tile(T, 512, 8)
    H = MLA_HEADS
    gq, gkv = small["q_norm_g"].reshape(1, -1), small["kv_norm_g"].reshape(1, -1)
    n_ext = MLA_Q_LORA + MLA_KV_LORA + 2 * MLA_ROPE
    row = lambda i, j, k: (i, 0)
    fix = lambda i, j, k: (0, 0)

    def epi_lat(accs, gq_t, gkv_t):
        a = accs[0]
        ql, kvl = a[:, :MLA_Q_LORA], a[:, MLA_Q_LORA:MLA_Q_LORA + MLA_KV_LORA]
        return [ql, kvl, a[:, MLA_Q_LORA + MLA_KV_LORA:], _rms(ql, gq_t), _rms(kvl, gkv_t)]

    ql, kvl, kr, xq, xkv = _mm(
        f"mla{l}_in", (T // tm, 1, 1), [(h, (tm, D_MODEL), row)], [(w["w_in_ext"], (D_MODEL, n_ext), fix)],
        [(0, 0, 0, "nn")], 1, (tm, n_ext), [(gq, gq.shape, fix), (gkv, gkv.shape, fix)],
        [((T, MLA_Q_LORA), F32, (tm, MLA_Q_LORA), row), ((T, MLA_KV_LORA), F32, (tm, MLA_KV_LORA), row),
         ((T, 2 * MLA_ROPE), F32, (tm, 2 * MLA_ROPE), row), ((T, MLA_Q_LORA), BF16, (tm, MLA_Q_LORA), row),
         ((T, MLA_KV_LORA), BF16, (tm, MLA_KV_LORA), row)], epi_lat)

    def epi_q(accs, cs_t):
        a = accs[0]
        return [jnp.concatenate([a[:, :MLA_NOPE], _rope_sum(a[:, MLA_NOPE:] * cs_t)], axis=1) * MLA_SCALE]

    head_out = lambda i, j, k: (j, i, 0)
    q = _mm(f"mla{l}_q", (T // tm, H, 1), [(xq, (tm, MLA_Q_LORA), row)],
            [(w["wq_ext"], (MLA_Q_LORA, MLA_W), lambda i, j, k: (0, j))], [(0, 0, 0, "nn")], 1, (tm, MLA_W),
            [(cs, (tm, 2 * MLA_ROPE), row)], [((H, T, MLA_W), BF16, (None, tm, MLA_W), head_out)], epi_q)[0]

    def epi_kv(accs, kr_t, cs_t):
        a = accs[0]
        v_t = a[:, MLA_NOPE:]
        return [jnp.concatenate([a[:, :MLA_NOPE], _low_half(_rope_sum(kr_t * cs_t))], axis=1),
                jnp.concatenate([v_t, jnp.ones_like(v_t)], axis=1)]

    k, v = _mm(f"mla{l}_kv", (T // tm, H, 1), [(xkv, (tm, MLA_KV_LORA), row)],
               [(w["w_kv_b"], (MLA_KV_LORA, MLA_W), lambda i, j, k: (0, j))], [(0, 0, 0, "nn")], 1, (tm, MLA_W),
               [(kr, (tm, 2 * MLA_ROPE), row), (cs, (tm, 2 * MLA_ROPE), row)],
               [((H, T, MLA_W), BF16, (None, tm, MLA_W), head_out), ((H, T, 2 * MLA_V), BF16, (None, tm, 2 * MLA_V), head_out)],
               epi_kv)
    o, lse = _flash_fwd(f"mla{l}_attn", q, k, v)
    z, h_out = _out_proj_ln(f"mla{l}_out", o, w["w_out"], h, g, b)
    return h_out, (h, ql, kvl, xq, xkv, q, k, v, o, lse, z, gq, gkv, cs)


def _mla_bwd(l, dh_out, saved, w, g):
    h, ql, kvl, xq, xkv, q, k, v, o, lse, z, gq, gkv, cs = saved
    T = h.shape[0]
    tm = _tile(T, 512, 8)
    H = MLA_HEADS
    dz, dg, db = _ln_bwd(f"mla{l}_lnb", dh_out, z, g)
    def epi_do(accs, o_t):
        d = accs[0]
        deltas = [jnp.broadcast_to(jnp.sum(d[:, hd * MLA_V:(hd + 1) * MLA_V] * o_t[:, hd * MLA_V:(hd + 1) * MLA_V],
                                           axis=-1, keepdims=True), (d.shape[0], MLA_V)) for hd in range(H)]
        return [d, jnp.concatenate(deltas, axis=1)]

    do, delta = _mm_simple(f"mla{l}_dO", dz, w["w_out"], "nt", None, extras=(o,), epilogue=epi_do, n_out=2,
                           out_dtypes=[BF16, F32])
    dw_out = _mm_simple(f"mla{l}_dWout", o, dz, "tn", BF16, tm=1024, tn=1024, tk=512)
    dq, dk, dv = _flash_bwd(f"mla{l}_attnb", q, k, v, do, lse, delta)
    head_in = lambda i, hh: (hh, i, 0)
    row2 = lambda i, hh: (i, 0)

    def fn_q(dq_t, cs_t):
        d = dq_t[:, MLA_NOPE:]
        return [jnp.concatenate([dq_t[:, :MLA_NOPE], _rope_sum(d) * cs_t], axis=1) * MLA_SCALE]

    dq_eff = _grid_call(f"mla{l}_dqeff", (T // tm, H), [(dq, (None, tm, MLA_W), head_in), (cs, (tm, 2 * MLA_ROPE), row2)],
                        [((T, H * MLA_W), BF16, (tm, MLA_W), lambda i, hh: (i, hh))], fn_q)[0]

    def fn_kv(dk_t, dv_t, cs_t):
        return [jnp.concatenate([dk_t[:, :MLA_NOPE], dv_t], axis=1), _rope_sum(_low_half(dk_t[:, MLA_NOPE:])) * cs_t]

    dkv_eff, dkr = _grid_call(
        f"mla{l}_dkveff", (T // tm, H),
        [(dk, (None, tm, MLA_W), head_in), (dv, (None, tm, MLA_V), head_in), (cs, (tm, 2 * MLA_ROPE), row2)],
        [((T, H * MLA_W), BF16, (tm, MLA_W), lambda i, hh: (i, hh)), ((T, 2 * MLA_ROPE), F32, (tm, 2 * MLA_ROPE), row2)],
        fn_kv, acc_outs=(1,))
    dxq = _mm_simple(f"mla{l}_dxq", dq_eff, w["wq_ext"], "nt", F32, tn=MLA_Q_LORA)
    dwq_ext = _mm_simple(f"mla{l}_dWq", xq, dq_eff, "tn", BF16, tm=MLA_Q_LORA, tn=1024, tk=512)
    dxkv = _mm_simple(f"mla{l}_dxkv", dkv_eff, w["w_kv_b"], "nt", F32, tn=MLA_KV_LORA)
    dwkv = _mm_simple(f"mla{l}_dWkv", xkv, dkv_eff, "tn", BF16, tm=MLA_KV_LORA, tn=1024, tk=512)

    def fn_lat(ql_t, dxq_t, gq_t, kvl_t, dxkv_t, gkv_t, dkr_t):
        dql, dgq = _rms_bwd(ql_t, gq_t, dxq_t)
        dkvl, dgkv = _rms_bwd(kvl_t, gkv_t, dxkv_t)
        return [jnp.concatenate([dql, dkvl, dkr_t], axis=1), _bcast8(dgq), _bcast8(dgkv)]

    n_ext = MLA_Q_LORA + MLA_KV_LORA + 2 * MLA_ROPE
    dlat, pgq, pgkv = _rows(f"mla{l}_dlat", T, tm,
                            [_rt(ql, tm), _rt(dxq, tm), _full(gq), _rt(kvl, tm), _rt(dxkv, tm), _full(gkv), _rt(dkr, tm)],
                            [_rt_out(T, n_ext, BF16, tm), _ps_out(T, MLA_Q_LORA, tm), _ps_out(T, MLA_KV_LORA, tm)], fn_lat)
    dh = _mm_simple(f"mla{l}_dX", dlat, w["w_in_ext"], "nt", F32, tk=n_ext, extras=(dz,),
                    epilogue=lambda accs, dz_t: [accs[0] + ALPHA * dz_t])
    dw_in_ext = _mm_simple(f"mla{l}_dWin", h, dlat, "tn", BF16, tm=1024, tn=n_ext, tk=512)
    dw_in, dwq = _mla_ext_grads(dw_in_ext.astype(F32), dwq_ext.astype(F32))
    grads = dict(w_in=dw_in, w_q_b=dwq, w_kv_b=dwkv, w_out=dw_out, q_norm_g=_psum_rows(pgq), kv_norm_g=_psum_rows(pgkv),
                 ln_g=dg, ln_b=db)
    return dh, grads


def _loss_head(y, target):
    T = y.shape[0]
    tm = _tile(T, 512, 8)

    def fn(y_t, t_t):
        d = y_t - t_t
        part = 0.5 * jnp.sum(jnp.mean(d * d, axis=-1, keepdims=True), axis=0, keepdims=True)
        return [d * (1.0 / D_MODEL), jnp.broadcast_to(part, (8, 128))]

    dy, part = _rows("loss_head", T, tm, [_rt(y, tm), _rt(target, tm)],
                     [_rt_out(T, D_MODEL, F32, tm), ((T // tm * 8, 128), F32, (8, 128), lambda i: (i, 0))], fn)
    return jnp.sum(part.reshape(-1, 8, 128)[:, 0, 0]), dy


_S5_VECTORS = ("d", "b_glu")


def _local_step(x, positions, target, W):
    row = lambda a, i: a[i].reshape(1, -1)
    w_in_ext, wq_ext = _mla_ext_weights(W["mla_w_in"][0], W["mla_w_q_b"][0])
    cs = _rope_table(positions)
    s5_mats, s5_mats_vjp, s5_apow, s5_apow_c = _s5_prepare(W, x.shape[0])
    h, saves = x, []
    for l in range(DEPTH):
        kind, slot = LAYER_MIXER[l], l // 3
        g, b = row(W["ln_mix_g"], l), row(W["ln_mix_b"], l)
        if kind == 0:
            w = {k: W["s5_" + k][slot] for k in ("w_in", "w_glu", "w_out")}
            small = {k: W["s5_" + k][slot] for k in _S5_VECTORS}
            h, sv = _s5_fwd(l, h, w, small, g, b, {k: v[slot] for k, v in s5_mats.items()}, s5_apow[slot], s5_apow_c[slot])
        elif kind == 1:
            w = dict(w_in=W["hg_w_in"][slot], w_out=W["hg_w_out"][slot])
            small = dict(lower_bound=W["hg_lower_bound"], norm_g=W["hg_norm_g"][slot])
            h, sv = _hg_fwd(l, h, w, small, g, b)
        else:
            w = dict(w_in_ext=w_in_ext, wq_ext=wq_ext, w_kv_b=W["mla_w_kv_b"][slot], w_out=W["mla_w_out"][slot])
            small = dict(q_norm_g=W["mla_q_norm_g"][slot], kv_norm_g=W["mla_kv_norm_g"][slot])
            h, sv = _mla_fwd(l, h, w, small, g, b, cs)
        h, fsv = _ffn_fwd(l, h, W["ffn_w_in"][l], W["ffn_w_out"][l], row(W["ln_ffn_g"], l), row(W["ln_ffn_b"], l))
        saves.append((w, small, sv, fsv))
    loss, dh = _loss_head(h, target)
    per_layer = [None] * DEPTH
    s5_dmats = {}
    for l in reversed(range(DEPTH)):
        kind = LAYER_MIXER[l]
        w, small, sv, fsv = saves[l]
        dh, gf = _ffn_bwd(l, dh, fsv, W["ffn_w_in"][l], W["ffn_w_out"][l], row(W["ln_ffn_g"], l))
        g = row(W["ln_mix_g"], l)
        if kind == 0:
            dh, gm, s5_dmats[l // 3] = _s5_bwd(l, dh, sv, w, small, g)
        elif kind == 1:
            dh, gm = _hg_bwd(l, dh, sv, w, small, g)
        else:
            dh, gm = _mla_bwd(l, dh, sv, w, g)
        per_layer[l] = (gm, gf)
    grads = {}
    stack = lambda xs: jnp.stack([a.astype(F32) if a.dtype != BF16 else a for a in xs])
    grads["ln_mix_g"] = stack([per_layer[l][0]["ln_g"] for l in range(DEPTH)])
    grads["ln_mix_b"] = stack([per_layer[l][0]["ln_b"] for l in range(DEPTH)])
    grads["ln_ffn_g"] = stack([per_layer[l][1]["ln_g"] for l in range(DEPTH)])
    grads["ln_ffn_b"] = stack([per_layer[l][1]["ln_b"] for l in range(DEPTH)])
    grads["ffn_w_in"] = stack([per_layer[l][1]["w_in"] for l in range(DEPTH)])
    grads["ffn_w_out"] = stack([per_layer[l][1]["w_out"] for l in range(DEPTH)])
    s5_layers = [l for l in range(DEPTH) if LAYER_MIXER[l] == 0]
    for k in ("w_in", "w_glu", "w_out") + _S5_VECTORS:
        grads["s5_" + k] = stack([per_layer[l][0][k] for l in s5_layers])
    d_ssm = s5_mats_vjp({k: jnp.stack([s5_dmats[s][k] for s in range(len(s5_layers))]) for k in s5_mats})
    for k, v in zip(_S5_SSM, d_ssm):
        grads["s5_" + k] = v
    hg = per_layer[1][0]
    grads["hg_w_in"], grads["hg_w_out"] = hg["w_in"][None], hg["w_out"][None]
    grads["hg_lower_bound"], grads["hg_norm_g"] = hg["lower_bound"], hg["norm_g"][None]
    for k, v in per_layer[2][0].items():
        if not k.startswith("ln_"):
            grads["mla_" + k] = v[None]
    return loss, dh, grads


def _here():
    return lax.axis_index("x"), lax.axis_index("y"), lax.axis_index("c")


def _any_spec():
    return pl.BlockSpec(memory_space=pl.ANY)


def _chip_exchange(name, xs, scatter):
    n = len(xs)

    def body(*refs):
        ins, outs = refs[:n], refs[n:2 * n]
        send_sems, recv_sems, loc_sems = refs[2 * n:]
        x, y, c = _here()
        me = 2 * x + y
        peers = [(1 - x, y), (x, 1 - y), (1 - x, 1 - y)]
        copies = []
        for t in range(n):
            src_of = (lambda p, t=t: ins[t].at[p]) if scatter else (lambda p, t=t: ins[t])
            loc = pltpu.make_async_copy(src_of(me), outs[t].at[me], loc_sems.at[t])
            loc.start()
            copies.append(loc)
            for j, (px, py) in enumerate(peers):
                cp = pltpu.make_async_remote_copy(
                    src_ref=src_of(2 * px + py), dst_ref=outs[t].at[me], send_sem=send_sems.at[t, j],
                    recv_sem=recv_sems.at[t, j], device_id=(px, py, c), device_id_type=MESH)
                cp.start()
                copies.append(cp)
        for cp in copies:
            cp.wait()

    out_shape = [jax.ShapeDtypeStruct(a.shape if scatter else (N_CHIPS,) + a.shape, a.dtype) for a in xs]
    return pl.pallas_call(
        body, name=name, in_specs=[_any_spec()] * n, out_specs=[_any_spec()] * n, out_shape=out_shape,
        scratch_shapes=[pltpu.SemaphoreType.DMA((n, 3)), pltpu.SemaphoreType.DMA((n, 3)), pltpu.SemaphoreType.DMA((n,))],
    )(*xs)


def _core_exchange(name, a, n_chunks=4):
    rows = a.shape[0] // n_chunks
    assert rows * n_chunks == a.shape[0] and rows % 16 == 0, (a.shape, n_chunks)

    def body(a_ref, o_ref, sbuf, rbuf, ld_sems, send_sems, recv_sems, st_sems, loc_sem):
        x, y, c = _here()
        loc = pltpu.make_async_copy(a_ref, o_ref.at[c], loc_sem)
        loc.start()
        parts = [pl.ds(i * rows, rows) for i in range(n_chunks)]
        loads = [pltpu.make_async_copy(a_ref.at[p], sbuf.at[p], ld_sems.at[i]) for i, p in enumerate(parts)]
        sends = [pltpu.make_async_remote_copy(src_ref=sbuf.at[p], dst_ref=rbuf.at[p], send_sem=send_sems.at[i],
                                              recv_sem=recv_sems.at[i], device_id=(x, y, 1 - c), device_id_type=MESH)
                 for i, p in enumerate(parts)]
        stores = [pltpu.make_async_copy(rbuf.at[p], o_ref.at[1 - c, p], st_sems.at[i]) for i, p in enumerate(parts)]
        for ld in loads:
            ld.start()
        for ld, sd in zip(loads, sends):
            ld.wait()
            sd.start()
        for sd, st in zip(sends, stores):
            sd.wait_recv()
            st.start()
        for sd, st in zip(sends, stores):
            sd.wait_send()
            st.wait()
        loc.wait()

    sem = lambda: pltpu.SemaphoreType.DMA((n_chunks,))
    return pl.pallas_call(
        body, name=name, in_specs=[_any_spec()], out_specs=_any_spec(),
        out_shape=jax.ShapeDtypeStruct((2,) + a.shape, a.dtype),
        scratch_shapes=[pltpu.VMEM(a.shape, a.dtype), pltpu.VMEM(a.shape, a.dtype), sem(), sem(), sem(), sem(),
                        pltpu.SemaphoreType.DMA],
        compiler_params=_cparams(),
    )(a)


def _all_exchange(name, a):
    def body(a_ref, o_ref, send_sems, recv_sems, loc_sem):
        x, y, c = _here()
        me = 4 * x + 2 * y + c
        loc = pltpu.make_async_copy(a_ref, o_ref.at[me], loc_sem)
        loc.start()
        copies = [loc]
        for mask in range(1, N_DEV):
            fx, fy, fc = (mask >> 2) & 1, (mask >> 1) & 1, mask & 1
            peer = (1 - x if fx else x, 1 - y if fy else y, 1 - c if fc else c)
            cp = pltpu.make_async_remote_copy(src_ref=a_ref, dst_ref=o_ref.at[me], send_sem=send_sems.at[mask - 1],
                                              recv_sem=recv_sems.at[mask - 1], device_id=peer, device_id_type=MESH)
            cp.start()
            copies.append(cp)
        for cp in copies:
            cp.wait()

    return pl.pallas_call(
        body, name=name, in_specs=[_any_spec()], out_specs=_any_spec(),
        out_shape=jax.ShapeDtypeStruct((N_DEV,) + a.shape, a.dtype),
        scratch_shapes=[pltpu.SemaphoreType.DMA((N_DEV - 1,)), pltpu.SemaphoreType.DMA((N_DEV - 1,)), pltpu.SemaphoreType.DMA],
    )(a)


def _sum_leading(name, a, out_dtype=F32):
    n, R, C = a.shape
    tr = _tile(R, 512, 16)

    def fn(a_t):
        s = a_t[0].astype(F32)
        for i in range(1, n):
            s = s + a_t[i].astype(F32)
        return [s]

    return _grid_call(name, (R // tr,), [(a, (n, tr, C), lambda i: (0, i, 0))],
                      [((R, C), out_dtype, (tr, C), lambda i: (i, 0))], fn)[0]


def _adamw(name, g_parts, w, m, v):
    R, C = w.shape
    tr = _tile(R, 256, 8)
    ng = len(g_parts)
    c1 = 1.0 / (1.0 - ADAM_B1 ** ADAM_STEP)
    c2 = 1.0 / (1.0 - ADAM_B2 ** ADAM_STEP)

    def fn(*tiles):
        g = tiles[0].astype(F32)
        for t in tiles[1:ng]:
            g = g + t.astype(F32)
        w_t, m_t, v_t = tiles[ng:]
        m_n = ADAM_B1 * m_t + (1.0 - ADAM_B1) * g
        v_n = ADAM_B2 * v_t + (1.0 - ADAM_B2) * (g * g)
        delta = -ADAM_LR * ((m_n * c1) / (jnp.sqrt(v_n * c2) + ADAM_EPS) + ADAM_WD * w_t)
        return [g, delta, m_n, v_n]

    spec = lambda a: (a, (tr, C), lambda i: (i, 0))
    return _grid_call(name, (R // tr,), [spec(a) for a in list(g_parts) + [w, m, v]],
                      [((R, C), F32, (tr, C), lambda i: (i, 0))] * 4, fn)


_WEIGHTS = ("ln_mix_g", "ln_mix_b", "ln_ffn_g", "ln_ffn_b", "ffn_w_in", "ffn_w_out", "s5_w_in", "s5_lam_re", "s5_lam_im",
            "s5_log_step", "s5_b_re", "s5_b_im", "s5_c_re", "s5_c_im", "s5_d", "s5_w_glu", "s5_b_glu", "s5_w_out", "hg_w_in",
            "hg_lower_bound", "hg_norm_g", "hg_w_out", "mla_w_in", "mla_q_norm_g", "mla_w_q_b", "mla_kv_norm_g", "mla_w_kv_b",
            "mla_w_out")
_BIG = {"ffn_w_in": 2, "ffn_w_out": 1, "s5_w_in": 1, "s5_w_glu": 1, "s5_w_out": 1, "hg_w_in": 2, "hg_w_out": 1,
        "mla_w_in": 1, "mla_w_q_b": 2, "mla_w_kv_b": 2, "mla_w_out": 1}
_SMALL_SHARDED = {"s5_d": 1, "s5_b_glu": 1, "mla_q_norm_g": 1, "mla_kv_norm_g": 1}
_REPLICATED = tuple(n for n in _WEIGHTS if n not in _BIG and n not in _SMALL_SHARDED)
LANES = 1024


def _pack(arrs, dtype, row_mult, lead=0):
    rows, segs, r = [], [], 0
    for a in arrs:
        lead_shape = a.shape[:lead]
        flat = a.astype(dtype).reshape(lead_shape + (-1,))
        n = -(-flat.shape[-1] // LANES)
        flat = jnp.pad(flat, [(0, 0)] * lead + [(0, n * LANES - flat.shape[-1])])
        rows.append(flat.reshape(lead_shape + (n, LANES)))
        segs.append((r, n))
        r += n
    pad = -r % row_mult
    if pad:
        rows.append(jnp.zeros(rows[0].shape[:lead] + (pad, LANES), dtype))
    return jnp.concatenate(rows, axis=lead), segs


def _unpack(packed, segs, shapes):
    out = []
    for (r0, n), shp in zip(segs, shapes):
        size = int(np.prod(shp))
        out.append(packed[..., r0:r0 + n, :].reshape(packed.shape[:-2] + (n * LANES,))[..., :size].reshape(packed.shape[:-2] + tuple(shp)))
    return out


def _unshard(stacked, axis):
    moved = jnp.moveaxis(stacked, 0, axis)
    shp = list(moved.shape)
    return moved.reshape(shp[:axis] + [shp[axis] * shp[axis + 1]] + shp[axis + 2:])


def _shard_split(full, axis):
    shp = list(full.shape)
    a = full.reshape(shp[:axis] + [N_CHIPS, shp[axis] // N_CHIPS] + shp[axis + 1:])
    return jnp.moveaxis(a, axis, 0)


def _train_step(x, positions, target, w, m, v):
    big, small_sh = list(_BIG), list(_SMALL_SHARDED)
    chip = 2 * lax.axis_index("x") + lax.axis_index("y")

    big_pack, big_segs = _pack([w[n] for n in big], BF16, 16)
    sm_pack, sm_segs = _pack([w[n] for n in small_sh], F32, 8)
    big_all, sm_all = _chip_exchange("gather_weights", [big_pack, sm_pack], scatter=False)
    W = {n: w[n] for n in _REPLICATED}
    for n, s in zip(big, _unpack(big_all, big_segs, [w[n].shape for n in big])):
        W[n] = _unshard(s, _BIG[n])
    for n, s in zip(small_sh, _unpack(sm_all, sm_segs, [w[n].shape for n in small_sh])):
        W[n] = _unshard(s, _SMALL_SHARDED[n])

    loss_local, grad_x, G = _local_step(x, positions, target, W)
    loss = lax.psum(loss_local, ("x", "y", "c"))
    out = {}

    g_pack, _ = _pack([_shard_split(G[n].astype(BF16), _BIG[n]) for n in big], BF16, 16, lead=1)
    recv = _chip_exchange("scatter_grads", [g_pack], scatter=True)[0]
    pair = _core_exchange("swap_core_sums", _sum_leading("sum_chips", recv, BF16))
    for n, (r0, nr) in zip(big, big_segs):
        as_rows = lambda a: a.reshape(nr, LANES)
        res = _adamw("adamw_" + n, [pair[0, r0:r0 + nr], pair[1, r0:r0 + nr]], as_rows(w[n]), as_rows(m[n]), as_rows(v[n]))
        out[n] = tuple(r.reshape(w[n].shape) for r in res)

    small = list(_REPLICATED) + small_sh
    s_pack, s_segs = _pack([G[n] for n in small], F32, 16)
    total = _sum_leading("sum_small", _all_exchange("gather_small_grads", s_pack))
    g_small = dict(zip(small, _unpack(total, s_segs, [G[n].shape for n in small])))
    for n in small_sh:
        width = w[n].shape[1]
        g_small[n] = lax.dynamic_slice_in_dim(g_small[n], chip * width, width, axis=1)
    packs = [_pack([d[n] for n in small], F32, 8)[0] for d in (g_small, w, m, v)]
    _, a_segs = _pack([w[n] for n in small], F32, 8)
    res = _adamw("adamw_small", [packs[0]], packs[1], packs[2], packs[3])
    unpacked = [_unpack(r, a_segs, [w[n].shape for n in small]) for r in res]
    for i, n in enumerate(small):
        out[n] = tuple(u[i] for u in unpacked)
    return loss, grad_x, out


def kernel(x, positions, ln_mix_g, ln_mix_b, ln_ffn_g, ln_ffn_b, ffn_w_in, ffn_w_out, s5_w_in, s5_lam_re, s5_lam_im,
           s5_log_step, s5_b_re, s5_b_im, s5_c_re, s5_c_im, s5_d, s5_w_glu, s5_b_glu, s5_w_out, hg_w_in,
           hg_lower_bound, hg_norm_g, hg_w_out, mla_w_in, mla_q_norm_g, mla_w_q_b, mla_kv_norm_g, mla_w_kv_b,
           mla_w_out, loss_target, m_ln_mix_g, m_ln_mix_b, m_ln_ffn_g, m_ln_ffn_b, m_ffn_w_in, m_ffn_w_out, m_s5_w_in,
           m_s5_lam_re, m_s5_lam_im, m_s5_log_step, m_s5_b_re, m_s5_b_im, m_s5_c_re, m_s5_c_im, m_s5_d, m_s5_w_glu,
           m_s5_b_glu, m_s5_w_out, m_hg_w_in, m_hg_lower_bound, m_hg_norm_g, m_hg_w_out, m_mla_w_in, m_mla_q_norm_g,
           m_mla_w_q_b, m_mla_kv_norm_g, m_mla_w_kv_b, m_mla_w_out, v_ln_mix_g, v_ln_mix_b, v_ln_ffn_g, v_ln_ffn_b,
           v_ffn_w_in, v_ffn_w_out, v_s5_w_in, v_s5_lam_re, v_s5_lam_im, v_s5_log_step, v_s5_b_re, v_s5_b_im,
           v_s5_c_re, v_s5_c_im, v_s5_d, v_s5_w_glu, v_s5_b_glu, v_s5_w_out, v_hg_w_in, v_hg_lower_bound, v_hg_norm_g,
           v_hg_w_out, v_mla_w_in, v_mla_q_norm_g, v_mla_w_q_b, v_mla_kv_norm_g, v_mla_w_kv_b, v_mla_w_out):
    args = (ln_mix_g, ln_mix_b, ln_ffn_g, ln_ffn_b, ffn_w_in, ffn_w_out, s5_w_in, s5_lam_re, s5_lam_im,
            s5_log_step, s5_b_re, s5_b_im, s5_c_re, s5_c_im, s5_d, s5_w_glu, s5_b_glu, s5_w_out, hg_w_in,
            hg_lower_bound, hg_norm_g, hg_w_out, mla_w_in, mla_q_norm_g, mla_w_q_b, mla_kv_norm_g, mla_w_kv_b,
            mla_w_out, m_ln_mix_g, m_ln_mix_b, m_ln_ffn_g, m_ln_ffn_b, m_ffn_w_in, m_ffn_w_out,
            m_s5_w_in, m_s5_lam_re, m_s5_lam_im, m_s5_log_step, m_s5_b_re, m_s5_b_im, m_s5_c_re, m_s5_c_im, m_s5_d,
            m_s5_w_glu, m_s5_b_glu, m_s5_w_out, m_hg_w_in, m_hg_lower_bound, m_hg_norm_g, m_hg_w_out, m_mla_w_in,
            m_mla_q_norm_g, m_mla_w_q_b, m_mla_kv_norm_g, m_mla_w_kv_b, m_mla_w_out, v_ln_mix_g, v_ln_mix_b,
            v_ln_ffn_g, v_ln_ffn_b, v_ffn_w_in, v_ffn_w_out, v_s5_w_in, v_s5_lam_re, v_s5_lam_im, v_s5_log_step,
            v_s5_b_re, v_s5_b_im, v_s5_c_re, v_s5_c_im, v_s5_d, v_s5_w_glu, v_s5_b_glu, v_s5_w_out, v_hg_w_in,
            v_hg_lower_bound, v_hg_norm_g, v_hg_w_out, v_mla_w_in, v_mla_q_norm_g, v_mla_w_q_b, v_mla_kv_norm_g,
            v_mla_w_kv_b, v_mla_w_out)
    nw = len(_WEIGHTS)
    w = dict(zip(_WEIGHTS, args[:nw]))
    m = dict(zip(_WEIGHTS, args[nw:2 * nw]))
    v = dict(zip(_WEIGHTS, args[2 * nw:]))
    loss, grad_x, out = _train_step(x[0], positions[0], loss_target[0], w, m, v)
    res = [loss, grad_x[None]]
    for i in range(4):
        res += [out[n][i] for n in _WEIGHTS]
    return tuple(res)
```

```python
import functools
import math

import numpy as np
import jax
import jax.numpy as jnp
from jax import lax
from jax.experimental import pallas as pl
from jax.experimental.pallas import tpu as pltpu

F32 = jnp.float32
BF16 = jnp.bfloat16

D_MODEL = 1024
DEPTH = 4
LAYER_MIXER = (0, 1, 2, 0)
S5_GROUP = 16
S5_GROUPS = 64
S5_STATE = 64
S5_CHUNK = 16
HG_HEADS = 8
HG_DIM = 128
HG_CHUNK = 128
MLA_HEADS = 8
MLA_NOPE = 128
MLA_ROPE = 64
MLA_V = 128
MLA_Q_LORA = 384
MLA_KV_LORA = 256
ROPE_THETA = 10000.0
FFN_HIDDEN = 2816
ALPHA = (2 * DEPTH) ** 0.25
LN_EPS = 1e-5
RMS_EPS = 1e-6
ADAM_LR, ADAM_B1, ADAM_B2, ADAM_EPS, ADAM_WD, ADAM_STEP = 0.001, 0.9, 0.999, 1e-08, 0.01, 10
VMEM_LIMIT_BYTES = 56 * 1024 * 1024
MESH = pl.DeviceIdType.MESH
N_CHIPS = 4
N_DEV = 8


def _cparams():
    return pltpu.CompilerParams(vmem_limit_bytes=VMEM_LIMIT_BYTES)


def _tile(n, want, mult):
    t = min(want, n)
    t -= t % mult
    while t >= mult:
        if n % t == 0:
            return t
        t -= mult
    return n


def _sigmoid(x):
    return 1.0 / (1.0 + jnp.exp(-x))


def _silu(x):
    return x * _sigmoid(x)


def _dsilu(x):
    s = _sigmoid(x)
    return s * (1.0 + x * (1.0 - s))


_GELU_C = math.sqrt(2.0 / math.pi)


def _gelu(x):
    return 0.5 * x * (1.0 + jnp.tanh(_GELU_C * (x + 0.044715 * x * x * x)))


def _dgelu(x):
    t = jnp.tanh(_GELU_C * (x + 0.044715 * x * x * x))
    return 0.5 * (1.0 + t) + 0.5 * x * (1.0 - t * t) * _GELU_C * (1.0 + 3 * 0.044715 * x * x)


def _layer_norm(z, g, b):
    mu = jnp.mean(z, axis=-1, keepdims=True)
    zc = z - mu
    var = jnp.mean(zc * zc, axis=-1, keepdims=True)
    return zc * lax.rsqrt(var + LN_EPS) * g + b


def _bcast8(row):
    return jnp.broadcast_to(row, (8, row.shape[-1]))


def _stack_rows8(rows):
    n = rows[0].shape[-1]
    idx = lax.broadcasted_iota(jnp.int32, (8, n), 0)
    out = jnp.zeros((8, n), F32)
    for i, r in enumerate(rows):
        out = jnp.where(idx == i, _bcast8(r), out)
    return out


def _psum_rows(a):
    return a.reshape(-1, 8, a.shape[-1])[:, 0, :].sum(axis=0)


_DN = {"nn": ((1,), (0,)), "nt": ((1,), (1,)), "tn": ((0,), (0,))}


def _mm(name, grid, a_defs, b_defs, pairs, n_acc, acc_shape, extra_defs, out_defs, epilogue):
    nk = grid[2]
    na, nb, ne, no = len(a_defs), len(b_defs), len(extra_defs), len(out_defs)

    def body(*refs):
        a_refs = refs[:na]
        b_refs = refs[na:na + nb]
        e_refs = refs[na + nb:na + nb + ne]
        o_refs = refs[na + nb + ne:na + nb + ne + no]
        acc = refs[-1]
        k = pl.program_id(2)

        @pl.when(k == 0)
        def _():
            acc[...] = jnp.zeros(acc.shape, F32)

        for (ai, bi, ci, mode) in pairs:
            a = a_refs[ai][...].astype(BF16)
            b = b_refs[bi][...].astype(BF16)
            acc[ci] += lax.dot_general(a, b, (_DN[mode], ((), ())), preferred_element_type=F32)

        @pl.when(k == nk - 1)
        def _():
            outs = epilogue([acc[c] for c in range(n_acc)], *[e[...] for e in e_refs])
            for o_ref, o in zip(o_refs, outs):
                o_ref[...] = o.astype(o_ref.dtype)

    in_specs = [pl.BlockSpec(d[1], d[2]) for d in list(a_defs) + list(b_defs) + list(extra_defs)]
    out_specs = [pl.BlockSpec(d[2], d[3]) for d in out_defs]
    out_shape = [jax.ShapeDtypeStruct(d[0], d[1]) for d in out_defs]
    res = pl.pallas_call(
        body, name=name, grid=grid, in_specs=in_specs, out_specs=out_specs, out_shape=out_shape,
        scratch_shapes=[pltpu.VMEM((n_acc,) + tuple(acc_shape), F32)], compiler_params=_cparams(),
    )(*[d[0] for d in list(a_defs) + list(b_defs) + list(extra_defs)])
    return res


def _mm_simple(name, a, b, mode, out_dtype, tm=512, tn=1024, tk=1024, extras=(), epilogue=None, n_out=1,
               out_dtypes=None, psum_outs=0, j_outer=False):
    if mode == "nn":
        (M, K), (K2, N) = a.shape, b.shape
    elif mode == "nt":
        (M, K), (N, K2) = a.shape, b.shape
    else:
        (K, M), (K2, N) = a.shape, b.shape
    assert K == K2, (name, a.shape, b.shape, mode)
    tm, tn, tk = _tile(M, tm, 8), _tile(N, tn, 128), _tile(K, tk, 128)
    grid = (M // tm, N // tn, K // tk)
    if mode == "nn":
        a_def = (a, (tm, tk), lambda i, j, k: (i, k))
        b_def = (b, (tk, tn), lambda i, j, k: (k, j))
    elif mode == "nt":
        a_def = (a, (tm, tk), lambda i, j, k: (i, k))
        b_def = (b, (tn, tk), lambda i, j, k: (j, k))
    else:
        a_def = (a, (tk, tm), lambda i, j, k: (k, i))
        b_def = (b, (tk, tn), lambda i, j, k: (k, j))
    extra_defs = []
    for e in extras:
        if e.shape[0] == 1:
            extra_defs.append((e, (1, tn), lambda i, j, k: (0, j)))
        else:
            extra_defs.append((e, (tm, tn), lambda i, j, k: (i, j)))
    out_dtypes = out_dtypes or [out_dtype] * n_out
    out_defs = [((M, N), dt, (tm, tn), lambda i, j, k: (i, j)) for dt in out_dtypes]
    out_defs += [((M // tm * 8, N), F32, (8, tn), lambda i, j, k: (i, j)) for _ in range(psum_outs)]
    if epilogue is None:
        epilogue = lambda accs: [accs[0]]
    a_defs, b_defs = [a_def], [b_def]
    if j_outer:
        swap = lambda d: d[:-1] + ((lambda f: lambda j, i, k: f(i, j, k))(d[-1]),)
        grid = (grid[1], grid[0], grid[2])
        a_defs, b_defs = [swap(a_def)], [swap(b_def)]
        extra_defs, out_defs = [swap(d) for d in extra_defs], [swap(d) for d in out_defs]
    res = _mm(name, grid, a_defs, b_defs, [(0, 0, 0, mode)], 1, (tm, tn), extra_defs, out_defs, epilogue)
    return res[0] if len(res) == 1 else res


def _rows(name, T, tm, in_defs, out_defs, fn):
    ni = len(in_defs)

    def body(*refs):
        outs = fn(*[r[...] for r in refs[:ni]])
        for o_ref, o in zip(refs[ni:], outs):
            o_ref[...] = o.astype(o_ref.dtype)

    res = pl.pallas_call(
        body, name=name, grid=(T // tm,),
        in_specs=[pl.BlockSpec(d[1], d[2]) for d in in_defs],
        out_specs=[pl.BlockSpec(d[2], d[3]) for d in out_defs],
        out_shape=[jax.ShapeDtypeStruct(d[0], d[1]) for d in out_defs],
        compiler_params=_cparams(),
    )(*[d[0] for d in in_defs])
    return res


def _grid_call(name, grid, in_defs, out_defs, fn, acc_outs=()):
    ni = len(in_defs)

    def body(*refs):
        outs = fn(*[r[...] for r in refs[:ni]])
        first = pl.program_id(len(grid) - 1) == 0
        for idx, (o_ref, o) in enumerate(zip(refs[ni:], outs)):
            if idx in acc_outs:
                @pl.when(first)
                def _(o_ref=o_ref, o=o):
                    o_ref[...] = o.astype(o_ref.dtype)

                @pl.when(jnp.logical_not(first))
                def _(o_ref=o_ref, o=o):
                    o_ref[...] += o.astype(o_ref.dtype)
            else:
                o_ref[...] = o.astype(o_ref.dtype)

    return pl.pallas_call(
        body, name=name, grid=grid,
        in_specs=[pl.BlockSpec(d[1], d[2]) for d in in_defs],
        out_specs=[pl.BlockSpec(d[2], d[3]) for d in out_defs],
        out_shape=[jax.ShapeDtypeStruct(d[0], d[1]) for d in out_defs],
        compiler_params=_cparams(),
    )(*[d[0] for d in in_defs])


def _rt(a, tm):
    return (a, (tm, a.shape[1]), lambda i: (i, 0))


def _full(a):
    return (a, a.shape, lambda i: (0,) * a.ndim)


def _rt_out(T, n, dt, tm):
    return ((T, n), dt, (tm, n), lambda i: (i, 0))


def _ps_out(T, n, tm):
    return ((T // tm * 8, n), F32, (8, n), lambda i: (i, 0))


def _out_proj_ln(name, a, w, h_in, g, b):
    def epi(accs, h_t, g_t, b_t):
        z = ALPHA * h_t + accs[0]
        return [z, _layer_norm(z, g_t, b_t)]
    return _mm_simple(name, a, w, "nn", F32, tm=512, tn=D_MODEL, tk=FFN_HIDDEN, extras=(h_in, g, b), epilogue=epi, n_out=2)


def _ln_bwd(name, dh, z, g):
    T = dh.shape[0]
    tm = _tile(T, 512, 8)

    def fn(dh_t, z_t, g_t):
        mu = jnp.mean(z_t, axis=-1, keepdims=True)
        zc = z_t - mu
        var = jnp.mean(zc * zc, axis=-1, keepdims=True)
        rstd = lax.rsqrt(var + LN_EPS)
        xhat = zc * rstd
        dxh = dh_t * g_t
        m1 = jnp.mean(dxh, axis=-1, keepdims=True)
        m2 = jnp.mean(dxh * xhat, axis=-1, keepdims=True)
        dz = rstd * (dxh - m1 - xhat * m2)
        return [dz, _bcast8(jnp.sum(dh_t * xhat, axis=0, keepdims=True)), _bcast8(jnp.sum(dh_t, axis=0, keepdims=True))]

    dz, pg, pb = _rows(name, T, tm, [_rt(dh, tm), _rt(z, tm), _full(g)],
                       [_rt_out(T, D_MODEL, F32, tm), _ps_out(T, D_MODEL, tm), _ps_out(T, D_MODEL, tm)], fn)
    return dz, _psum_rows(pg), _psum_rows(pb)


def _ffn_fwd(l, h, w_in, w_out, g, b):
    T = h.shape[0]
    tm, tn = _tile(T, 512, 8), 1408
    nj = FFN_HIDDEN // tn
    grid = (nj, T // tm, 1)

    def epi(accs):
        gg, uu = accs
        return [gg, uu, _silu(gg) * uu]

    G, U, A = _mm(
        f"ffn{l}_in", grid, [(h, (tm, D_MODEL), lambda j, i, k: (i, 0))],
        [(w_in, (D_MODEL, tn), lambda j, i, k: (0, j)), (w_in, (D_MODEL, tn), lambda j, i, k: (0, j + nj))],
        [(0, 0, 0, "nn"), (0, 1, 1, "nn")], 2, (tm, tn), [],
        [((T, FFN_HIDDEN), F32, (tm, tn), lambda j, i, k: (i, j)),
         ((T, FFN_HIDDEN), F32, (tm, tn), lambda j, i, k: (i, j)),
         ((T, FFN_HIDDEN), BF16, (tm, tn), lambda j, i, k: (i, j))], epi)
    z, h_out = _out_proj_ln(f"ffn{l}_out", A, w_out, h, g, b)
    return h_out, (h, G, U, A, z)


def _ffn_bwd(l, dh_out, saved, w_in, w_out, g):
    h, G, U, A, z = saved
    T = h.shape[0]
    dz, dg, db = _ln_bwd(f"ffn{l}_lnb", dh_out, z, g)

    def epi(accs, g_t, u_t):
        da = accs[0]
        return [da * u_t * _dsilu(g_t), da * _silu(g_t)]

    dG, dU = _mm_simple(f"ffn{l}_dA", dz, w_out, "nt", BF16, tm=512, tn=1408, tk=1024, extras=(G, U), epilogue=epi, n_out=2,
                        j_outer=True)
    dw_out = _mm_simple(f"ffn{l}_dWout", A, dz, "tn", BF16, tm=1408, tn=1024, tk=512)
    dw_g = _mm_simple(f"ffn{l}_dWg", h, dG, "tn", BF16, tm=1024, tn=1408, tk=512)
    dw_u = _mm_simple(f"ffn{l}_dWu", h, dU, "tn", BF16, tm=1024, tn=1408, tk=512)
    tm, tk = _tile(T, 256, 8), FFN_HIDDEN
    dh = _mm(
        f"ffn{l}_dX", (T // tm, 1, 1),
        [(dG, (tm, tk), lambda i, j, k: (i, 0)), (dU, (tm, tk), lambda i, j, k: (i, 0))],
        [(w_in, (D_MODEL, tk), lambda i, j, k: (0, 0)), (w_in, (D_MODEL, tk), lambda i, j, k: (0, 1))],
        [(0, 0, 0, "nt"), (1, 1, 0, "nt")], 1, (tm, D_MODEL),
        [(dz, (tm, D_MODEL), lambda i, j, k: (i, 0))],
        [((T, D_MODEL), F32, (tm, D_MODEL), lambda i, j, k: (i, 0))],
        lambda accs, dz_t: [accs[0] + ALPHA * dz_t])[0]
    return dh, dict(w_in=jnp.concatenate([dw_g, dw_u], axis=1), w_out=dw_out, ln_g=dg, ln_b=db)


def _s5_matrices(lam_re, lam_im, log_step, b_re, b_im, c_re, c_im):
    L, hp = S5_CHUNK, lax.Precision.HIGHEST
    out = {}
    mt_total = 0.0
    for d in range(2):
        lam = lax.complex(lam_re[d], lam_im[d])
        step = jnp.exp(log_step[d])[:, None]
        lam_dt = lam * step
        lam_bar = jnp.exp(lam_dt)
        b_bar = ((lam_bar - 1.0) / lam)[..., None] * lax.complex(b_re[d], b_im[d])
        c = lax.complex(c_re[d], c_im[d])
        pw = jnp.exp(lam_dt[None] * jnp.arange(L + 1, dtype=F32)[:, None, None])
        kj = jnp.einsum("ghp,jgp,gpk->gjhk", c, pw[:L], b_bar, precision=hp).real
        lag = np.arange(L)[None, :] - np.arange(L)[:, None]
        lag = lag if d == 0 else -lag
        sel = np.stack([(lag == j) for j in range(L)]).astype(np.float32)
        mt = jnp.einsum("jst,gjab->gsbta", sel, kj, precision=hp).reshape(S5_GROUPS, 16 * L, 16 * L)
        mt_total = mt_total + mt
        pw_dist = jnp.flip(pw[:L], 0) if d == 0 else pw[:L]
        pc = pw_dist.transpose(1, 0, 2)[:, :, None, :] * b_bar.transpose(0, 2, 1)[:, None, :, :]
        pm = jnp.concatenate([pc.real, pc.imag], axis=-1).reshape(S5_GROUPS, 16 * L, 2 * S5_STATE)
        pw_read = pw[1:] if d == 0 else jnp.flip(pw[1:], 0)
        qc = c[:, None, :, :] * pw_read.transpose(1, 0, 2)[:, :, None, :]
        qm = jnp.concatenate([qc.real, -qc.imag], axis=-1).reshape(S5_GROUPS, 16 * L, 2 * S5_STATE).transpose(0, 2, 1)
        a = pw[L]
        out[f"p{d}"], out[f"q{d}"] = pm, qm
        out[f"a{d}"] = jnp.concatenate([a.real, a.imag], axis=-1)
    out["mt"] = mt_total
    return out


def _s5_apow(lam_re, lam_im, log_step, n_steps, conj):
    lam_dt = lax.complex(lam_re, lam_im) * jnp.exp(log_step)[..., None]
    k = (S5_CHUNK * 2.0 ** jnp.arange(n_steps, dtype=F32))[None, None, :, None]
    a = jnp.exp(lam_dt[:, :, None, :] * k)
    re, im = a.real, (-a.imag if conj else a.imag)
    return jnp.stack([jnp.concatenate([re, re], -1), jnp.concatenate([-im, im], -1)], axis=3)


def _shift_rows(x, s, down):
    n = x.shape[0]
    if s >= n:
        return jnp.zeros_like(x)
    if s % 8 == 0:
        z = jnp.zeros((s, x.shape[1]), x.dtype)
        return jnp.concatenate([z, x[:n - s]], axis=0) if down else jnp.concatenate([x[s:], z], axis=0)
    row = lax.broadcasted_iota(jnp.int32, x.shape, 0)
    if down:
        return jnp.where(row >= s, pltpu.roll(x, s, 0), 0.0)
    return jnp.where(row < n - s, pltpu.roll(x, n - s, 0), 0.0)


def _cmul(x, a1, a2):
    return x * a1 + pltpu.roll(x, S5_STATE, 1) * a2


def _chunk_scan(s, apow_ref, down):
    n = s.shape[0]
    k, sh = 0, 1
    while sh < n:
        s = s + _cmul(_shift_rows(s, sh, down), apow_ref[k, 0:1, :], apow_ref[k, 1:2, :])
        k, sh = k + 1, sh * 2
    return s


def _s5_scan_fwd(name, ug, mats, apow):
    G, C, W = ug.shape
    n_steps = apow.shape[2]

    def body(u_ref, mt_ref, p0_ref, p1_ref, q0_ref, q1_ref, ap0_ref, ap1_ref, y_ref, h0_ref, h1_ref):
        u = u_ref[...]
        s0 = jnp.dot(u, p0_ref[...], preferred_element_type=F32)
        s1 = jnp.dot(u, p1_ref[...], preferred_element_type=F32)
        h0 = _shift_rows(_chunk_scan(s0, ap0_ref, True), 1, True)
        h1 = _shift_rows(_chunk_scan(s1, ap1_ref, False), 1, False)
        y = jnp.dot(u, mt_ref[...], preferred_element_type=F32)
        y += jnp.dot(h0.astype(BF16), q0_ref[...], preferred_element_type=F32)
        y += jnp.dot(h1.astype(BF16), q1_ref[...], preferred_element_type=F32)
        y_ref[...] = y.astype(y_ref.dtype)
        h0_ref[...] = h0
        h1_ref[...] = h1

    def gspec(shape):
        return pl.BlockSpec((None,) + shape, lambda g: (g,) + (0,) * len(shape))

    ap0, ap1 = apow[0], apow[1]
    return pl.pallas_call(
        body, name=name, grid=(G,),
        in_specs=[gspec((C, W)), gspec((W, W)), gspec((W, 128)), gspec((W, 128)), gspec((128, W)), gspec((128, W)),
                  gspec((n_steps, 2, 128)), gspec((n_steps, 2, 128))],
        out_specs=[gspec((C, W)), gspec((C, 128)), gspec((C, 128))],
        out_shape=[jax.ShapeDtypeStruct((G, C, W), BF16), jax.ShapeDtypeStruct((G, C, 128), F32),
                   jax.ShapeDtypeStruct((G, C, 128), F32)],
        compiler_params=_cparams(),
    )(ug, mats["mt"].astype(BF16), mats["p0"].astype(BF16), mats["p1"].astype(BF16),
      mats["q0"].astype(BF16), mats["q1"].astype(BF16), ap0, ap1)


def _s5_scan_bwd(name, dyg, ug, h0, h1, mats, apow_conj):
    G, C, W = ug.shape
    n_steps = apow_conj.shape[2]

    def body(dy_ref, u_ref, h0_ref, h1_ref, mt_ref, p0_ref, p1_ref, q0_ref, q1_ref, ap0_ref, ap1_ref,
             du_ref, dmt_ref, dp0_ref, dp1_ref, dq0_ref, dq1_ref, da_ref):
        dy, u = dy_ref[...], u_ref[...]
        nt, tn = (_DN["nt"], ((), ())), (_DN["tn"], ((), ()))
        dh0 = lax.dot_general(dy, q0_ref[...], nt, preferred_element_type=F32)
        dh1 = lax.dot_general(dy, q1_ref[...], nt, preferred_element_type=F32)
        ds0 = _chunk_scan(_shift_rows(dh0, 1, False), ap0_ref, False)
        ds1 = _chunk_scan(_shift_rows(dh1, 1, True), ap1_ref, True)
        ds0b, ds1b = ds0.astype(BF16), ds1.astype(BF16)
        du = lax.dot_general(dy, mt_ref[...], nt, preferred_element_type=F32)
        du += lax.dot_general(ds0b, p0_ref[...], nt, preferred_element_type=F32)
        du += lax.dot_general(ds1b, p1_ref[...], nt, preferred_element_type=F32)
        du_ref[...] = du.astype(du_ref.dtype)
        dmt_ref[...] = lax.dot_general(u, dy, tn, preferred_element_type=F32)
        dp0_ref[...] = lax.dot_general(u, ds0b, tn, preferred_element_type=F32)
        dp1_ref[...] = lax.dot_general(u, ds1b, tn, preferred_element_type=F32)
        h0v, h1v = h0_ref[...], h1_ref[...]
        dq0_ref[...] = lax.dot_general(h0v.astype(BF16), dy, tn, preferred_element_type=F32)
        dq1_ref[...] = lax.dot_general(h1v.astype(BF16), dy, tn, preferred_element_type=F32)
        rows = [jnp.sum(ds0 * h0v, axis=0, keepdims=True), jnp.sum(ds0 * pltpu.roll(h0v, S5_STATE, 1), axis=0, keepdims=True),
                jnp.sum(ds1 * h1v, axis=0, keepdims=True), jnp.sum(ds1 * pltpu.roll(h1v, S5_STATE, 1), axis=0, keepdims=True)]
        da_ref[...] = _stack_rows8(rows)

    def gspec(shape):
        return pl.BlockSpec((None,) + shape, lambda g: (g,) + (0,) * len(shape))

    f32s = lambda *s: jax.ShapeDtypeStruct((G,) + s, F32)
    return pl.pallas_call(
        body, name=name, grid=(G,),
        in_specs=[gspec((C, W)), gspec((C, W)), gspec((C, 128)), gspec((C, 128)), gspec((W, W)), gspec((W, 128)),
                  gspec((W, 128)), gspec((128, W)), gspec((128, W)), gspec((n_steps, 2, 128)), gspec((n_steps, 2, 128))],
        out_specs=[gspec((C, W)), gspec((W, W)), gspec((W, 128)), gspec((W, 128)), gspec((128, W)), gspec((128, W)),
                   gspec((8, 128))],
        out_shape=[jax.ShapeDtypeStruct((G, C, W), BF16), f32s(W, W), f32s(W, 128), f32s(W, 128), f32s(128, W),
                   f32s(128, W), f32s(8, 128)],
        compiler_params=_cparams(),
    )(dyg, ug, h0, h1, mats["mt"].astype(BF16), mats["p0"].astype(BF16), mats["p1"].astype(BF16),
      mats["q0"].astype(BF16), mats["q1"].astype(BF16), apow_conj[0], apow_conj[1])


S5_LANE_GROUPS = 128 // S5_GROUP


def _group_select():
    e = np.zeros((S5_LANE_GROUPS, S5_CHUNK, 128, S5_CHUNK, S5_GROUP), np.float32)
    for j in range(S5_LANE_GROUPS):
        for l in range(S5_CHUNK):
            for hh in range(S5_GROUP):
                e[j, l, S5_GROUP * j + hh, l, hh] = 1.0
    return jnp.asarray(e.reshape(S5_LANE_GROUPS, S5_CHUNK * 128, S5_CHUNK * S5_GROUP), BF16)


def _to_groups(name, a):
    T = a.shape[0]
    C = T // S5_CHUNK
    cb = _tile(C, 256, 8)
    sel = _group_select()

    def body(x_ref, e_ref, o_ref):
        xcat = jnp.concatenate([x_ref[pl.ds(l, cb, stride=S5_CHUNK), :].astype(BF16) for l in range(S5_CHUNK)], axis=1)
        for j in range(S5_LANE_GROUPS):
            o_ref[j] = jnp.dot(xcat, e_ref[j], preferred_element_type=F32).astype(BF16)

    return pl.pallas_call(
        body, name=name, grid=(S5_GROUPS // S5_LANE_GROUPS, C // cb),
        in_specs=[pl.BlockSpec((cb * S5_CHUNK, 128), lambda b, r: (r, b)), pl.BlockSpec(sel.shape, lambda b, r: (0, 0, 0))],
        out_specs=pl.BlockSpec((S5_LANE_GROUPS, cb, S5_CHUNK * S5_GROUP), lambda b, r: (b, r, 0)),
        out_shape=jax.ShapeDtypeStruct((S5_GROUPS, C, S5_CHUNK * S5_GROUP), BF16), compiler_params=_cparams(),
    )(a, sel)


def _from_groups(name, a, extras, fn, n_out):
    G, C, W = a.shape
    T = C * S5_CHUNK
    cb = _tile(C, 256, 8)
    sel = _group_select()
    ne = len(extras)

    def body(*refs):
        y_ref, e_ref = refs[:2]
        e_refs, o_refs = refs[2:2 + ne], refs[2 + ne:]
        ycat = jnp.zeros((cb, S5_CHUNK * 128), F32)
        for j in range(S5_LANE_GROUPS):
            ycat = ycat + lax.dot_general(y_ref[j], e_ref[j], (_DN["nt"], ((), ())), preferred_element_type=F32)
        for l in range(S5_CHUNK):
            rows = pl.ds(l, cb, stride=S5_CHUNK)
            ex = [r[...] if r.shape[0] == 1 else r[rows, :] for r in e_refs]
            for o_ref, o in zip(o_refs, fn(ycat[:, l * 128:(l + 1) * 128], *ex)):
                o_ref[rows, :] = o

    col = pl.BlockSpec((cb * S5_CHUNK, 128), lambda b, r: (r, b))
    return pl.pallas_call(
        body, name=name, grid=(G // S5_LANE_GROUPS, C // cb),
        in_specs=[pl.BlockSpec((S5_LANE_GROUPS, cb, W), lambda b, r: (b, r, 0)),
                  pl.BlockSpec(sel.shape, lambda b, r: (0, 0, 0))]
        + [pl.BlockSpec((1, 128), lambda b, r: (0, b)) if e.shape[0] == 1 else col for e in extras],
        out_specs=[col] * n_out,
        out_shape=[jax.ShapeDtypeStruct((T, G * S5_GROUP), F32)] * n_out, compiler_params=_cparams(),
    )(a, sel, *extras)


_S5_SSM = ("lam_re", "lam_im", "log_step", "b_re", "b_im", "c_re", "c_im")


def _s5_prepare(W, T):
    sp = tuple(W["s5_" + k] for k in _S5_SSM)
    mats, mats_vjp = jax.vjp(jax.vmap(_s5_matrices), *sp)
    n_steps = max(1, int(math.log2(T // S5_CHUNK)))
    apow = jax.vmap(lambda a, b, c: _s5_apow(a, b, c, n_steps, False))(*sp[:3])
    apow_c = jax.vmap(lambda a, b, c: _s5_apow(a, b, c, n_steps, True))(*sp[:3])
    return mats, mats_vjp, apow, apow_c


def _s5_fwd(l, h, w, small, g, b, mats, apow, apow_c):
    u = _mm_simple(f"s5{l}_in", h, w["w_in"], "nn", F32)
    ug = _to_groups(f"s5{l}_togroups", u)
    yg, h0, h1 = _s5_scan_fwd(f"s5{l}_scan", ug, mats, apow)
    d_row, bglu_row = small["d"].reshape(1, -1), small["b_glu"].reshape(1, -1)
    yssm, y1 = _from_groups(f"s5{l}_fromgroups", yg, [u, d_row], lambda y, u_t, d_t: [y, _gelu(y + d_t * u_t)], 2)

    def epi(accs, y1_t, bg_t):
        gate = _sigmoid(accs[0] + bg_t)
        return [y1_t * gate, gate]

    y2, gate = _mm_simple(f"s5{l}_glu", y1, w["w_glu"], "nn", None, extras=(y1, bglu_row), epilogue=epi, n_out=2,
                          out_dtypes=[BF16, F32])
    z, h_out = _out_proj_ln(f"s5{l}_out", y2, w["w_out"], h, g, b)
    return h_out, (h, u, ug, yssm, y1, y2, gate, z, h0, h1, mats, apow_c, d_row)


def _s5_bwd(l, dh_out, saved, w, small, g):
    h, u, ug, yssm, y1, y2, gate, z, h0, h1, mats, apow_c, d_row = saved
    T = h.shape[0]
    dz, dg, db = _ln_bwd(f"s5{l}_lnb", dh_out, z, g)

    def epi1(accs, y1_t, gate_t):
        dy2 = accs[0]
        dpre = dy2 * y1_t * gate_t * (1.0 - gate_t)
        return [dpre, dy2 * gate_t, _bcast8(jnp.sum(dpre, axis=0, keepdims=True))]

    dpre, dy1a, pbg = _mm_simple(f"s5{l}_dy2", dz, w["w_out"], "nt", None, extras=(y1, gate), epilogue=epi1, n_out=2,
                                 out_dtypes=[BF16, F32], psum_outs=1)
    dw_out = _mm_simple(f"s5{l}_dWout", y2, dz, "tn", BF16, tm=1024, tn=1024, tk=512)

    def epi2(accs, dy1a_t, yssm_t, u_t, d_t):
        dy1 = accs[0] + dy1a_t
        dy = dy1 * _dgelu(yssm_t + d_t * u_t)
        return [dy, dy * d_t, _bcast8(jnp.sum(dy * u_t, axis=0, keepdims=True))]

    dy, du_skip, pdd = _mm_simple(f"s5{l}_dy1", dpre, w["w_glu"], "nt", None, extras=(dy1a, yssm, u, d_row), epilogue=epi2,
                                  n_out=2, out_dtypes=[F32, F32], psum_outs=1)
    dw_glu = _mm_simple(f"s5{l}_dWglu", y1, dpre, "tn", BF16, tm=1024, tn=1024, tk=512)
    dyg = _to_groups(f"s5{l}_togroups_b", dy)
    dug, dmt, dp0, dp1, dq0, dq1, da = _s5_scan_bwd(f"s5{l}_scanb", dyg, ug, h0, h1, mats, apow_c)
    du = _from_groups(f"s5{l}_fromgroups_b", dug, [du_skip], lambda y, skip: [y + skip], 1)[0]

    def a_grad(p, q):
        return jnp.concatenate([p[:, :S5_STATE] + p[:, S5_STATE:], q[:, S5_STATE:] - q[:, :S5_STATE]], axis=-1)

    dmats = dict(mt=dmt, p0=dp0, p1=dp1, q0=dq0, q1=dq1, a0=a_grad(da[:, 0], da[:, 1]), a1=a_grad(da[:, 2], da[:, 3]))
    dh = _mm_simple(f"s5{l}_dX", du, w["w_in"], "nt", F32, extras=(dz,), epilogue=lambda accs, dz_t: [accs[0] + ALPHA * dz_t])
    dw_in = _mm_simple(f"s5{l}_dWin", h, du, "tn", BF16, tm=1024, tn=1024, tk=512)
    grads = dict(w_in=dw_in, w_glu=dw_glu, w_out=dw_out, d=_psum_rows(pdd), b_glu=_psum_rows(pbg), ln_g=dg, ln_b=db)
    return dh, grads, dmats


def _gla_levels(lc):
    ms, m = [], lc // 2
    while m >= 1:
        ms.append(m)
        m //= 2
    return ms


def _gla_scan_matrix(lc, rev):
    r = np.arange(lc)[:, None]
    t = np.arange(lc)[None, :]
    blocks = []
    for m in _gla_levels(lc):
        same = (r // m) == (t // m)
        upper = ((r // m) % 2) == 1
        blocks.append(same & np.where(upper, t >= r, t < r))
    blocks.append(t >= r)
    blocks.append(t < r)
    if rev:
        blocks = [blk[::-1, ::-1] for blk in blocks]
    return np.concatenate(blocks, axis=1).astype(np.float32)


def _gla_gates(z, lb):
    sig = _sigmoid(z)
    ls = jnp.minimum(z, 0.0) - jnp.log(1.0 + jnp.exp(-jnp.abs(z)))
    a = jnp.log(lb)
    bb = jnp.log(1.0 - lb) + ls
    lf = jnp.maximum(a, bb) + jnp.log(1.0 + jnp.exp(-jnp.abs(a - bb)))
    return lf, (1.0 - lb) * (1.0 - sig), sig


def _gla_cumsum(lf, rev):
    b, sh = lf, 1
    while sh < lf.shape[0]:
        b = b + _shift_rows(b, sh, not rev)
        sh *= 2
    return b


def _gla_bref(b, m, rev):
    lc, n = b.shape
    idx = m if rev else m - 1
    if 2 * m >= 8:
        nb = lc // (2 * m)
        b3 = b.reshape(nb, 2 * m, n)
        return jnp.broadcast_to(b3[:, idx:idx + 1, :], (nb, 2 * m, n)).reshape(lc, n)
    row = lax.broadcasted_iota(jnp.int32, b.shape, 0)
    j = row & (2 * m - 1)
    out = b
    for jj in range(2 * m):
        if jj != idx:
            out = jnp.where(j == jj, pltpu.roll(b, (jj - idx) % lc, 0), out)
    return out


def _gla_masks(lc, rev):
    r = np.arange(lc)
    mq, bm = [], [np.eye(lc)]
    for m in _gla_levels(lc):
        isq = ((r // m) % 2) == (0 if rev else 1)
        mq.append(np.broadcast_to(isq[:, None], (lc, HG_DIM)))
        bm.append((r[:, None] // (2 * m)) == (r[None, :] // (2 * m)))
    return jnp.asarray(np.stack(mq), F32), jnp.asarray(np.stack(bm), F32)


def _gla_chunk(q, k, lf, rev, mq_ref, bm_ref):
    lc = q.shape[0]
    nt = (_DN["nt"], ((), ()))
    b = _gla_cumsum(lf, rev)
    qb, kb = q.astype(BF16), k.astype(BF16)
    sc = bm_ref[0] * lax.dot_general(qb, kb, nt, preferred_element_type=F32)
    levels = []
    for i, m in enumerate(_gla_levels(lc)):
        mq = mq_ref[i]
        mk = 1.0 - mq
        w = jnp.exp((b - _gla_bref(b, m, rev)) * (mq - mk))
        wq, wk = w * mq, w * mk
        xf, yf = q * wq, k * wk
        xb, yb = xf.astype(BF16), yf.astype(BF16)
        sc = sc + bm_ref[i + 1] * lax.dot_general(xb, yb, nt, preferred_element_type=F32)
        levels.append((wq, wk, xf, yf, xb, yb))
    return b, sc, levels


def _hg_specs(T, lc, rev, backward):
    nc = T // lc
    cc = (lambda c: nc - 1 - c) if rev != backward else (lambda c: c)
    zcol = HG_HEADS * (2 if rev else 1)
    q_spec = pl.BlockSpec((lc, HG_DIM), lambda h, c: (cc(c), h))
    z_spec = pl.BlockSpec((lc, HG_DIM), lambda h, c: (cc(c), zcol + h))
    v_spec = pl.BlockSpec((lc, HG_DIM), lambda h, c: (cc(c), 3 * HG_HEADS + h))
    lb_spec = pl.BlockSpec((1, HG_DIM), lambda h, c: (0, h))
    st_spec = pl.BlockSpec((None, None, HG_DIM, HG_DIM), lambda h, c: (h, cc(c), 0, 0))
    return nc, q_spec, z_spec, v_spec, lb_spec, st_spec


def _gla_fwd(name, proj, lb_row, rev):
    T = proj.shape[0]
    lc = _tile(T, HG_CHUNK, 8)
    nc, q_spec, z_spec, v_spec, lb_spec, st_spec = _hg_specs(T, lc, rev, False)
    last = 0 if rev else lc - 1
    mq, bm = _gla_masks(lc, rev)
    const3 = lambda a: pl.BlockSpec(a.shape, lambda h, c: (0, 0, 0))

    def body(q_ref, z_ref, v_ref, lb_ref, mq_ref, bm_ref, o_ref, st_ref, st_s):
        @pl.when(pl.program_id(1) == 0)
        def _():
            st_s[...] = jnp.zeros(st_s.shape, F32)

        q = _silu(q_ref[...])
        lf, k, _ = _gla_gates(z_ref[...], lb_ref[...])
        vb = v_ref[...].astype(BF16)
        b, sc, _ = _gla_chunk(q, k, lf, rev, mq_ref, bm_ref)
        st0 = st_s[...]
        st_ref[...] = st0
        bl = b[last:last + 1, :]
        o = jnp.dot(sc.astype(BF16), vb, preferred_element_type=F32)
        o += lax.dot_general((q * jnp.exp(b)).astype(BF16), st0.astype(BF16), (_DN["nt"], ((), ())), preferred_element_type=F32)
        o_ref[...] = o
        kd = (k * jnp.exp(bl - b)).astype(BF16)
        st_s[...] = st0 * jnp.exp(bl) + lax.dot_general(vb, kd, (_DN["tn"], ((), ())), preferred_element_type=F32)

    return pl.pallas_call(
        body, name=name, grid=(HG_HEADS, nc),
        in_specs=[q_spec, z_spec, v_spec, lb_spec, const3(mq), const3(bm)], out_specs=[q_spec, st_spec],
        out_shape=[jax.ShapeDtypeStruct((T, D_MODEL), F32), jax.ShapeDtypeStruct((HG_HEADS, nc, HG_DIM, HG_DIM), F32)],
        scratch_shapes=[pltpu.VMEM((HG_DIM, HG_DIM), F32)], compiler_params=_cparams(),
    )(proj, proj, proj, lb_row, mq, bm)


def _gla_bwd(name, proj, lb_row, do, st, rev):
    T = proj.shape[0]
    lc = _tile(T, HG_CHUNK, 8)
    nc, q_spec, z_spec, v_spec, lb_spec, st_spec = _hg_specs(T, lc, rev, True)
    wall = jnp.asarray(_gla_scan_matrix(lc, rev), BF16)
    last = 0 if rev else lc - 1
    mq, bm = _gla_masks(lc, rev)
    const3 = lambda a: pl.BlockSpec(a.shape, lambda h, c: (0, 0, 0))

    def body(q_ref, z_ref, v_ref, lb_ref, do_ref, st_ref, wall_ref, mq_ref, bm_ref, dq_ref, dz_ref, dv_ref, dlb_ref, dst_s):
        first = pl.program_id(1) == 0

        @pl.when(first)
        def _():
            dst_s[...] = jnp.zeros(dst_s.shape, F32)
            dlb_ref[...] = jnp.zeros(dlb_ref.shape, F32)

        nn, nt, tn = (_DN["nn"], ((), ())), (_DN["nt"], ((), ())), (_DN["tn"], ((), ()))
        dot = functools.partial(lax.dot_general, preferred_element_type=F32)
        qr, z, lb = q_ref[...], z_ref[...], lb_ref[...]
        q = _silu(qr)
        lf, k, sig = _gla_gates(z, lb)
        vb = v_ref[...].astype(BF16)
        b, sc, levels = _gla_chunk(q, k, lf, rev, mq_ref, bm_ref)
        st0, dst = st_ref[...], dst_s[...]
        st0b, dstb = st0.astype(BF16), dst.astype(BF16)
        dob = do_ref[...].astype(BF16)
        bl = b[last:last + 1, :]
        eb, ebl, ekd = jnp.exp(b), jnp.exp(bl), jnp.exp(bl - b)
        qe, kd = q * eb, k * ekd
        kdb = kd.astype(BF16)
        dsc = dot(dob, vb, nt)
        dv_ref[...] = dot(sc.astype(BF16), dob, tn) + dot(kdb, dstb, nt)
        dqe = dot(dob, st0b, nn)
        dkd = dot(vb, dstb, nn)
        dq = dqe * eb
        dk = dkd * ekd
        zs = []
        dsd = (bm_ref[0] * dsc).astype(BF16)
        dq += dot(dsd, k.astype(BF16), nn)
        dk += dot(dsd, q.astype(BF16), tn)
        for i, (wq, wk, xf, yf, xb, yb) in enumerate(levels):
            dsl = (bm_ref[i + 1] * dsc).astype(BF16)
            dx = dot(dsl, yb, nn)
            dy = dot(dsl, xb, tn)
            dq += dx * wq
            dk += dy * wk
            zs.append((dx * xf + dy * yf).astype(BF16))
        zs.append((dqe * qe).astype(BF16))
        zs.append((dkd * kd).astype(BF16))
        zl = jnp.sum(dst * st0, axis=0, keepdims=True) * ebl
        dlf = dot(wall_ref[...], jnp.concatenate(zs, axis=0), nn) + zl
        dst_s[...] = dst * ebl + dot(dob, qe.astype(BF16), tn)
        inv_f = jnp.exp(-lf)
        one_sig = 1.0 - sig
        dz_ref[...] = (dlf * inv_f - dk) * (1.0 - lb) * sig * one_sig
        dq_ref[...] = dq * _dsilu(qr)
        dlb_ref[...] += _bcast8(jnp.sum((dlf * inv_f - dk) * one_sig, axis=0, keepdims=True))

    big = jax.ShapeDtypeStruct((T, D_MODEL), F32)
    return pl.pallas_call(
        body, name=name, grid=(HG_HEADS, nc),
        in_specs=[q_spec, z_spec, v_spec, lb_spec, q_spec, st_spec, pl.BlockSpec(wall.shape, lambda h, c: (0, 0)),
                  const3(mq), const3(bm)],
        out_specs=[q_spec, q_spec, q_spec, pl.BlockSpec((None, 8, HG_DIM), lambda h, c: (h, 0, 0))],
        out_shape=[big, big, big, jax.ShapeDtypeStruct((HG_HEADS, 8, HG_DIM), F32)],
        scratch_shapes=[pltpu.VMEM((HG_DIM, HG_DIM), F32)], compiler_params=_cparams(),
    )(proj, proj, proj, lb_row, do, st, wall, mq, bm)


def _hg_lower_bounds(hg_lower_bound, layer):
    lbs = jax.nn.softmax(hg_lower_bound, axis=0)
    lbs = jnp.cumsum(lbs, axis=0) - lbs[0]
    return lbs[layer].reshape(1, -1)


def _hg_post(o_fw, o_bw, g_raw, ng):
    outs, ons, os_, rs = [], [], [], []
    for hd in range(o_fw.shape[1] // HG_DIM):
        sl = slice(hd * HG_DIM, (hd + 1) * HG_DIM)
        o = o_fw[:, sl] + o_bw[:, sl]
        r = lax.rsqrt(jnp.mean(o * o, axis=-1, keepdims=True) + RMS_EPS)
        on = o * r * ng
        outs.append(on * _silu(g_raw[:, sl]))
        ons.append(on)
        os_.append(o)
        rs.append(r)
    return outs, ons, os_, rs


def _hg_fwd(l, h, w, small, g, b):
    T = h.shape[0]
    tm = _tile(T, 512, 8)
    proj = _mm_simple(f"hg{l}_in", h, w["w_in"], "nn", F32, tn=1280, j_outer=True)
    lb_fn = lambda p: _hg_lower_bounds(p, l)
    lb_row, lb_vjp = jax.vjp(lb_fn, small["lower_bound"])
    o_fw, st_fw = _gla_fwd(f"hg{l}_gla_fw", proj, lb_row, False)
    o_bw, st_bw = _gla_fwd(f"hg{l}_gla_bw", proj, lb_row, True)
    ng = small["norm_g"].reshape(1, HG_DIM)

    def post(of_t, ob_t, g_t, ng_t):
        return [jnp.concatenate(_hg_post(of_t, ob_t, g_t, ng_t)[0], axis=1)]

    og = _rows(f"hg{l}_post", T, tm,
               [_rt(o_fw, tm), _rt(o_bw, tm), (proj, (tm, D_MODEL), lambda i: (i, 4)), _full(ng)],
               [_rt_out(T, D_MODEL, BF16, tm)], post)[0]
    z, h_out = _out_proj_ln(f"hg{l}_out", og, w["w_out"], h, g, b)
    return h_out, (h, proj, lb_row, lb_vjp, st_fw, st_bw, o_fw, o_bw, ng, og, z)


def _hg_bwd(l, dh_out, saved, w, small, g):
    h, proj, lb_row, lb_vjp, st_fw, st_bw, o_fw, o_bw, ng, og, z = saved
    T = h.shape[0]
    dz, dg, db = _ln_bwd(f"hg{l}_lnb", dh_out, z, g)
    tm = _tile(T, 512, 8)
    nn_tiles = D_MODEL // HG_DIM

    def epi(accs, of_t, ob_t, g_t, ng_t):
        dog = accs[0]
        _, ons, os_, rs = _hg_post(of_t, ob_t, g_t, ng_t)
        dos, dgs, dngs = [], [], []
        for hd in range(nn_tiles):
            sl = slice(hd * HG_DIM, (hd + 1) * HG_DIM)
            d, o, r = dog[:, sl], os_[hd], rs[hd]
            dgs.append(d * ons[hd] * _dsilu(g_t[:, sl]))
            don = d * _silu(g_t[:, sl])
            dngs.append(jnp.sum(don * o * r, axis=0, keepdims=True))
            dxn = don * ng_t
            dos.append(r * dxn - o * (r * r * r) * jnp.mean(dxn * o, axis=-1, keepdims=True))
        return [jnp.concatenate(dos, axis=1), jnp.concatenate(dgs, axis=1), _bcast8(jnp.concatenate(dngs, axis=1))]

    grid = (T // tm, 1, 1)
    row_map = lambda i, j, k: (i, 0)
    do, dg_raw, png = _mm(
        f"hg{l}_dog", grid, [(dz, (tm, D_MODEL), row_map)], [(w["w_out"], (D_MODEL, D_MODEL), lambda i, j, k: (0, 0))],
        [(0, 0, 0, "nt")], 1, (tm, D_MODEL),
        [(o_fw, (tm, D_MODEL), row_map), (o_bw, (tm, D_MODEL), row_map), (proj, (tm, D_MODEL), lambda i, j, k: (i, 4)),
         (ng, (1, HG_DIM), lambda i, j, k: (0, 0))],
        [((T, D_MODEL), F32, (tm, D_MODEL), row_map), ((T, D_MODEL), F32, (tm, D_MODEL), row_map),
         ((T // tm * 8, D_MODEL), F32, (8, D_MODEL), row_map)], epi)
    dw_out = _mm_simple(f"hg{l}_dWout", og, dz, "tn", BF16, tm=1024, tn=1024, tk=512)
    dq_f, dz_f, dv_f, dlb_f = _gla_bwd(f"hg{l}_glab_fw", proj, lb_row, do, st_fw, False)
    dq_b, dz_b, dv_b, dlb_b = _gla_bwd(f"hg{l}_glab_bw", proj, lb_row, do, st_bw, True)
    dproj = jnp.concatenate([dq_f + dq_b, dz_f, dz_b, dv_f + dv_b, dg_raw], axis=1).astype(BF16)
    dh = _mm_simple(f"hg{l}_dX", dproj, w["w_in"], "nt", F32, tm=256, tk=5 * D_MODEL, extras=(dz,),
                    epilogue=lambda accs, dz_t: [accs[0] + ALPHA * dz_t])
    dw_in = _mm_simple(f"hg{l}_dWin", h, dproj, "tn", BF16, tm=1024, tn=1280, tk=512)
    dlb_row = (dlb_f[:, 0, :] + dlb_b[:, 0, :]).reshape(1, D_MODEL)
    grads = dict(w_in=dw_in, w_out=dw_out, lower_bound=lb_vjp(dlb_row)[0],
                 norm_g=_psum_rows(png).reshape(nn_tiles, HG_DIM).sum(axis=0), ln_g=dg, ln_b=db)
    return dh, grads


MLA_W = 256
MLA_SCALE = (MLA_NOPE + MLA_ROPE) ** -0.5


def _swap_halves(a):
    n = a.shape[-1] // 2
    return jnp.concatenate([a[..., n:], a[..., :n]], axis=-1)


def _mla_ext_weights(w_in, w_q_b):
    w_in_ext = jnp.concatenate([w_in, _swap_halves(w_in[:, MLA_Q_LORA + MLA_KV_LORA:])], axis=1)
    wq = w_q_b.reshape(MLA_Q_LORA, MLA_HEADS, MLA_NOPE + MLA_ROPE)
    wq_ext = jnp.concatenate([wq, _swap_halves(wq[:, :, MLA_NOPE:])], axis=2).reshape(MLA_Q_LORA, MLA_HEADS * MLA_W)
    return w_in_ext, wq_ext


def _mla_ext_grads(dw_in_ext, dwq_ext):
    n_lat = MLA_Q_LORA + MLA_KV_LORA
    dw_in = jnp.concatenate([dw_in_ext[:, :n_lat], dw_in_ext[:, n_lat:n_lat + MLA_ROPE]
                             + _swap_halves(dw_in_ext[:, n_lat + MLA_ROPE:])], axis=1)
    dq = dwq_ext.reshape(MLA_Q_LORA, MLA_HEADS, MLA_W)
    dwq = jnp.concatenate([dq[:, :, :MLA_NOPE], dq[:, :, MLA_NOPE:MLA_NOPE + MLA_ROPE]
                           + _swap_halves(dq[:, :, MLA_NOPE + MLA_ROPE:])], axis=2)
    return dw_in, dwq.reshape(MLA_Q_LORA, MLA_HEADS * (MLA_NOPE + MLA_ROPE))


def _rope_table(positions):
    half = MLA_ROPE // 2
    inv_freq = 1.0 / (ROPE_THETA ** (jnp.arange(half, dtype=F32) * (2.0 / MLA_ROPE)))
    ang = positions.astype(F32)[:, None] * inv_freq
    cos, sin = jnp.cos(ang), jnp.sin(ang)
    return jnp.concatenate([cos, cos, -sin, sin], axis=1)


def _rope_sum(prod):
    return prod + pltpu.roll(prod, MLA_ROPE, 1)


def _low_half(a):
    lane = lax.broadcasted_iota(jnp.int32, a.shape, 1)
    return jnp.where(lane < MLA_ROPE, a, 0.0)


def _rms(x, g):
    r = lax.rsqrt(jnp.mean(x * x, axis=-1, keepdims=True) + RMS_EPS)
    return x * r * g


def _rms_bwd(x, g, dy):
    r = lax.rsqrt(jnp.mean(x * x, axis=-1, keepdims=True) + RMS_EPS)
    dxn = dy * g
    dx = r * dxn - x * (r * r * r) * jnp.mean(dxn * x, axis=-1, keepdims=True)
    return dx, jnp.sum(dy * x * r, axis=0, keepdims=True)


ATT_TILE = 512
ATT_BLOCK = 2048
ATT_KEYS_BWD = 1024
ATT_QUERIES_FWD = 1024


def _lanes(col, n):
    return jnp.tile(col, (1, n // 128))


def _flash_fwd(name, q, k, v):
    H, T, _ = q.shape
    tq, tkb = _tile(T, ATT_QUERIES_FWD, 8), _tile(T, ATT_BLOCK, 128)
    ts = _tile(tkb, ATT_TILE, 128)
    nk, nsub = T // tkb, tkb // ts

    def body(q_ref, k_ref, v_ref, o_ref, lse_ref, m_s, acc_s):
        ki = pl.program_id(2)

        @pl.when(ki == 0)
        def _():
            m_s[...] = jnp.full(m_s.shape, -jnp.inf, F32)
            acc_s[...] = jnp.zeros(acc_s.shape, F32)

        qv = q_ref[...]
        m, acc = m_s[...], acc_s[...]
        for j in range(nsub):
            kj, vj = k_ref[j * ts:(j + 1) * ts, :], v_ref[j * ts:(j + 1) * ts, :]
            s = lax.dot_general(qv, kj, (_DN["nt"], ((), ())), preferred_element_type=F32)
            m_new = jnp.maximum(m, jnp.max(s, axis=-1, keepdims=True))
            p = jnp.exp(s - _lanes(m_new, ts)).astype(BF16)
            acc = _lanes(jnp.exp(m - m_new), 2 * MLA_V) * acc + jnp.dot(p, vj, preferred_element_type=F32)
            m = m_new
        m_s[...], acc_s[...] = m, acc

        @pl.when(ki == nk - 1)
        def _():
            l = acc[:, MLA_V:]
            o_ref[...] = acc[:, :MLA_V] / l
            lse_ref[...] = m + jnp.log(l)

    return pl.pallas_call(
        body, name=name, grid=(H, T // tq, nk),
        in_specs=[pl.BlockSpec((None, tq, MLA_W), lambda h, i, j: (h, i, 0)),
                  pl.BlockSpec((None, tkb, MLA_W), lambda h, i, j: (h, j, 0)),
                  pl.BlockSpec((None, tkb, 2 * MLA_V), lambda h, i, j: (h, j, 0))],
        out_specs=[pl.BlockSpec((tq, MLA_V), lambda h, i, j: (i, h)), pl.BlockSpec((tq, MLA_V), lambda h, i, j: (i, h))],
        out_shape=[jax.ShapeDtypeStruct((T, H * MLA_V), F32), jax.ShapeDtypeStruct((T, H * MLA_V), F32)],
        scratch_shapes=[pltpu.VMEM((tq, MLA_V), F32), pltpu.VMEM((tq, 2 * MLA_V), F32)],
        compiler_params=_cparams(),
    )(q, k, v)


def _flash_bwd(name, q, k, v, do, lse, delta):
    H, T, _ = q.shape
    nt, tn = (_DN["nt"], ((), ())), (_DN["tn"], ((), ()))
    tk, tqb = _tile(T, ATT_KEYS_BWD, 128), _tile(T, ATT_BLOCK, 128)
    tqs = _tile(tqb, ATT_TILE, 128)
    nk, nqb, nsub = T // tk, T // tqb, tqb // tqs

    def body(q_ref, k_ref, v_ref, do_ref, lse_ref, dl_ref, dq_ref, dk_ref, dv_ref, dk_s, dv_s):
        ki, qb = pl.program_id(1), pl.program_id(2)

        @pl.when(jnp.logical_and(ki == 0, qb == 0))
        def _():
            dq_ref[...] = jnp.zeros(dq_ref.shape, F32)

        @pl.when(qb == 0)
        def _():
            dk_s[...] = jnp.zeros(dk_s.shape, F32)
            dv_s[...] = jnp.zeros(dv_s.shape, F32)

        kv, vv = k_ref[...], v_ref[...]
        dk, dv = dk_s[...], dv_s[...]
        for j in range(nsub):
            sl = slice(j * tqs, (j + 1) * tqs)
            qj, doj = q_ref[sl, :], do_ref[sl, :]
            s = lax.dot_general(qj, kv, nt, preferred_element_type=F32)
            dp = lax.dot_general(doj, vv, nt, preferred_element_type=F32)
            p = jnp.exp(s - _lanes(lse_ref[sl, :], tk))
            ds = (p * (dp - _lanes(dl_ref[sl, :], tk))).astype(BF16)
            dv = dv + lax.dot_general(p.astype(BF16), doj, tn, preferred_element_type=F32)
            dk = dk + lax.dot_general(ds, qj, tn, preferred_element_type=F32)
            rows = pl.ds(pl.multiple_of(qb * tqb + j * tqs, tqs), tqs)
            dq_ref[rows, :] += jnp.dot(ds, kv, preferred_element_type=F32)
        dk_s[...], dv_s[...] = dk, dv

        @pl.when(qb == nqb - 1)
        def _():
            dk_ref[...] = dk
            dv_ref[...] = dv

    return pl.pallas_call(
        body, name=name, grid=(H, nk, nqb),
        in_specs=[pl.BlockSpec((None, tqb, MLA_W), lambda h, i, j: (h, j, 0)),
                  pl.BlockSpec((None, tk, MLA_W), lambda h, i, j: (h, i, 0)),
                  pl.BlockSpec((None, tk, MLA_V), lambda h, i, j: (h, i, 0)),
                  pl.BlockSpec((tqb, MLA_V), lambda h, i, j: (j, h)),
                  pl.BlockSpec((tqb, MLA_V), lambda h, i, j: (j, h)),
                  pl.BlockSpec((tqb, MLA_V), lambda h, i, j: (j, h))],
        out_specs=[pl.BlockSpec((None, T, MLA_W), lambda h, i, j: (h, 0, 0)),
                   pl.BlockSpec((None, tk, MLA_W), lambda h, i, j: (h, i, 0)),
                   pl.BlockSpec((None, tk, MLA_V), lambda h, i, j: (h, i, 0))],
        out_shape=[jax.ShapeDtypeStruct((H, T, MLA_W), F32), jax.ShapeDtypeStruct((H, T, MLA_W), F32),
                   jax.ShapeDtypeStruct((H, T, MLA_V), F32)],
        scratch_shapes=[pltpu.VMEM((tk, MLA_W), F32), pltpu.VMEM((tk, MLA_V), F32)], compiler_params=_cparams(),
    )(q, k, v, do, lse, delta)


def _mla_fwd(l, h, w, small, g, b, cs):
    T = h.shape[0]
    tm = _tile(T, 512, 8)
    H = MLA_HEADS
    gq, gkv = small["q_norm_g"].reshape(1, -1), small["kv_norm_g"].reshape(1, -1)
    n_ext = MLA_Q_LORA + MLA_KV_LORA + 2 * MLA_ROPE
    row = lambda i, j, k: (i, 0)
    fix = lambda i, j, k: (0, 0)

    def epi_lat(accs, gq_t, gkv_t):
        a = accs[0]
        ql, kvl = a[:, :MLA_Q_LORA], a[:, MLA_Q_LORA:MLA_Q_LORA + MLA_KV_LORA]
        return [ql, kvl, a[:, MLA_Q_LORA + MLA_KV_LORA:], _rms(ql, gq_t), _rms(kvl, gkv_t)]

    ql, kvl, kr, xq, xkv = _mm(
        f"mla{l}_in", (T // tm, 1, 1), [(h, (tm, D_MODEL), row)], [(w["w_in_ext"], (D_MODEL, n_ext), fix)],
        [(0, 0, 0, "nn")], 1, (tm, n_ext), [(gq, gq.shape, fix), (gkv, gkv.shape, fix)],
        [((T, MLA_Q_LORA), F32, (tm, MLA_Q_LORA), row), ((T, MLA_KV_LORA), F32, (tm, MLA_KV_LORA), row),
         ((T, 2 * MLA_ROPE), F32, (tm, 2 * MLA_ROPE), row), ((T, MLA_Q_LORA), BF16, (tm, MLA_Q_LORA), row),
         ((T, MLA_KV_LORA), BF16, (tm, MLA_KV_LORA), row)], epi_lat)

    def epi_q(accs, cs_t):
        a = accs[0]
        return [jnp.concatenate([a[:, :MLA_NOPE], _rope_sum(a[:, MLA_NOPE:] * cs_t)], axis=1) * MLA_SCALE]

    head_out = lambda i, j, k: (j, i, 0)
    q = _mm(f"mla{l}_q", (T // tm, H, 1), [(xq, (tm, MLA_Q_LORA), row)],
            [(w["wq_ext"], (MLA_Q_LORA, MLA_W), lambda i, j, k: (0, j))], [(0, 0, 0, "nn")], 1, (tm, MLA_W),
            [(cs, (tm, 2 * MLA_ROPE), row)], [((H, T, MLA_W), BF16, (None, tm, MLA_W), head_out)], epi_q)[0]

    def epi_kv(accs, kr_t, cs_t):
        a = accs[0]
        v_t = a[:, MLA_NOPE:]
        return [jnp.concatenate([a[:, :MLA_NOPE], _low_half(_rope_sum(kr_t * cs_t))], axis=1),
                jnp.concatenate([v_t, jnp.ones_like(v_t)], axis=1)]

    k, v = _mm(f"mla{l}_kv", (T // tm, H, 1), [(xkv, (tm, MLA_KV_LORA), row)],
               [(w["w_kv_b"], (MLA_KV_LORA, MLA_W), lambda i, j, k: (0, j))], [(0, 0, 0, "nn")], 1, (tm, MLA_W),
               [(kr, (tm, 2 * MLA_ROPE), row), (cs, (tm, 2 * MLA_ROPE), row)],
               [((H, T, MLA_W), BF16, (None, tm, MLA_W), head_out), ((H, T, 2 * MLA_V), BF16, (None, tm, 2 * MLA_V), head_out)],
               epi_kv)
    o, lse = _flash_fwd(f"mla{l}_attn", q, k, v)
    z, h_out = _out_proj_ln(f"mla{l}_out", o, w["w_out"], h, g, b)
    return h_out, (h, ql, kvl, xq, xkv, q, k, v, o, lse, z, gq, gkv, cs)


def _mla_bwd(l, dh_out, saved, w, g):
    h, ql, kvl, xq, xkv, q, k, v, o, lse, z, gq, gkv, cs = saved
    T = h.shape[0]
    tm = _tile(T, 512, 8)
    H = MLA_HEADS
    dz, dg, db = _ln_bwd(f"mla{l}_lnb", dh_out, z, g)
    def epi_do(accs, o_t):
        d = accs[0]
        deltas = [jnp.broadcast_to(jnp.sum(d[:, hd * MLA_V:(hd + 1) * MLA_V] * o_t[:, hd * MLA_V:(hd + 1) * MLA_V],
                                           axis=-1, keepdims=True), (d.shape[0], MLA_V)) for hd in range(H)]
        return [d, jnp.concatenate(deltas, axis=1)]

    do, delta = _mm_simple(f"mla{l}_dO", dz, w["w_out"], "nt", None, extras=(o,), epilogue=epi_do, n_out=2,
                           out_dtypes=[BF16, F32])
    dw_out = _mm_simple(f"mla{l}_dWout", o, dz, "tn", BF16, tm=1024, tn=1024, tk=512)
    dq, dk, dv = _flash_bwd(f"mla{l}_attnb", q, k, v, do, lse, delta)
    head_in = lambda i, hh: (hh, i, 0)
    row2 = lambda i, hh: (i, 0)

    def fn_q(dq_t, cs_t):
        d = dq_t[:, MLA_NOPE:]
        return [jnp.concatenate([dq_t[:, :MLA_NOPE], _rope_sum(d) * cs_t], axis=1) * MLA_SCALE]

    dq_eff = _grid_call(f"mla{l}_dqeff", (T // tm, H), [(dq, (None, tm, MLA_W), head_in), (cs, (tm, 2 * MLA_ROPE), row2)],
                        [((T, H * MLA_W), BF16, (tm, MLA_W), lambda i, hh: (i, hh))], fn_q)[0]

    def fn_kv(dk_t, dv_t, cs_t):
        return [jnp.concatenate([dk_t[:, :MLA_NOPE], dv_t], axis=1), _rope_sum(_low_half(dk_t[:, MLA_NOPE:])) * cs_t]

    dkv_eff, dkr = _grid_call(
        f"mla{l}_dkveff", (T // tm, H),
        [(dk, (None, tm, MLA_W), head_in), (dv, (None, tm, MLA_V), head_in), (cs, (tm, 2 * MLA_ROPE), row2)],
        [((T, H * MLA_W), BF16, (tm, MLA_W), lambda i, hh: (i, hh)), ((T, 2 * MLA_ROPE), F32, (tm, 2 * MLA_ROPE), row2)],
        fn_kv, acc_outs=(1,))
    dxq = _mm_simple(f"mla{l}_dxq", dq_eff, w["wq_ext"], "nt", F32, tn=MLA_Q_LORA)
    dwq_ext = _mm_simple(f"mla{l}_dWq", xq, dq_eff, "tn", BF16, tm=MLA_Q_LORA, tn=1024, tk=512)
    dxkv = _mm_simple(f"mla{l}_dxkv", dkv_eff, w["w_kv_b"], "nt", F32, tn=MLA_KV_LORA)
    dwkv = _mm_simple(f"mla{l}_dWkv", xkv, dkv_eff, "tn", BF16, tm=MLA_KV_LORA, tn=1024, tk=512)

    def fn_lat(ql_t, dxq_t, gq_t, kvl_t, dxkv_t, gkv_t, dkr_t):
        dql, dgq = _rms_bwd(ql_t, gq_t, dxq_t)
        dkvl, dgkv = _rms_bwd(kvl_t, gkv_t, dxkv_t)
        return [jnp.concatenate([dql, dkvl, dkr_t], axis=1), _bcast8(dgq), _bcast8(dgkv)]

    n_ext = MLA_Q_LORA + MLA_KV_LORA + 2 * MLA_ROPE
    dlat, pgq, pgkv = _rows(f"mla{l}_dlat", T, tm,
                            [_rt(ql, tm), _rt(dxq, tm), _full(gq), _rt(kvl, tm), _rt(dxkv, tm), _full(gkv), _rt(dkr, tm)],
                            [_rt_out(T, n_ext, BF16, tm), _ps_out(T, MLA_Q_LORA, tm), _ps_out(T, MLA_KV_LORA, tm)], fn_lat)
    dh = _mm_simple(f"mla{l}_dX", dlat, w["w_in_ext"], "nt", F32, tk=n_ext, extras=(dz,),
                    epilogue=lambda accs, dz_t: [accs[0] + ALPHA * dz_t])
    dw_in_ext = _mm_simple(f"mla{l}_dWin", h, dlat, "tn", BF16, tm=1024, tn=n_ext, tk=512)
    dw_in, dwq = _mla_ext_grads(dw_in_ext.astype(F32), dwq_ext.astype(F32))
    grads = dict(w_in=dw_in, w_q_b=dwq, w_kv_b=dwkv, w_out=dw_out, q_norm_g=_psum_rows(pgq), kv_norm_g=_psum_rows(pgkv),
                 ln_g=dg, ln_b=db)
    return dh, grads


def _loss_head(y, target):
    T = y.shape[0]
    tm = _tile(T, 512, 8)

    def fn(y_t, t_t):
        d = y_t - t_t
        part = 0.5 * jnp.sum(jnp.mean(d * d, axis=-1, keepdims=True), axis=0, keepdims=True)
        return [d * (1.0 / D_MODEL), jnp.broadcast_to(part, (8, 128))]

    dy, part = _rows("loss_head", T, tm, [_rt(y, tm), _rt(target, tm)],
                     [_rt_out(T, D_MODEL, F32, tm), ((T // tm * 8, 128), F32, (8, 128), lambda i: (i, 0))], fn)
    return jnp.sum(part.reshape(-1, 8, 128)[:, 0, 0]), dy


_S5_VECTORS = ("d", "b_glu")


def _local_step(x, positions, target, W):
    row = lambda a, i: a[i].reshape(1, -1)
    w_in_ext, wq_ext = _mla_ext_weights(W["mla_w_in"][0], W["mla_w_q_b"][0])
    cs = _rope_table(positions)
    s5_mats, s5_mats_vjp, s5_apow, s5_apow_c = _s5_prepare(W, x.shape[0])
    h, saves = x, []
    for l in range(DEPTH):
        kind, slot = LAYER_MIXER[l], l // 3
        g, b = row(W["ln_mix_g"], l), row(W["ln_mix_b"], l)
        if kind == 0:
            w = {k: W["s5_" + k][slot] for k in ("w_in", "w_glu", "w_out")}
            small = {k: W["s5_" + k][slot] for k in _S5_VECTORS}
            h, sv = _s5_fwd(l, h, w, small, g, b, {k: v[slot] for k, v in s5_mats.items()}, s5_apow[slot], s5_apow_c[slot])
        elif kind == 1:
            w = dict(w_in=W["hg_w_in"][slot], w_out=W["hg_w_out"][slot])
            small = dict(lower_bound=W["hg_lower_bound"], norm_g=W["hg_norm_g"][slot])
            h, sv = _hg_fwd(l, h, w, small, g, b)
        else:
            w = dict(w_in_ext=w_in_ext, wq_ext=wq_ext, w_kv_b=W["mla_w_kv_b"][slot], w_out=W["mla_w_out"][slot])
            small = dict(q_norm_g=W["mla_q_norm_g"][slot], kv_norm_g=W["mla_kv_norm_g"][slot])
            h, sv = _mla_fwd(l, h, w, small, g, b, cs)
        h, fsv = _ffn_fwd(l, h, W["ffn_w_in"][l], W["ffn_w_out"][l], row(W["ln_ffn_g"], l), row(W["ln_ffn_b"], l))
        saves.append((w, small, sv, fsv))
    loss, dh = _loss_head(h, target)
    per_layer = [None] * DEPTH
    s5_dmats = {}
    for l in reversed(range(DEPTH)):
        kind = LAYER_MIXER[l]
        w, small, sv, fsv = saves[l]
        dh, gf = _ffn_bwd(l, dh, fsv, W["ffn_w_in"][l], W["ffn_w_out"][l], row(W["ln_ffn_g"], l))
        g = row(W["ln_mix_g"], l)
        if kind == 0:
            dh, gm, s5_dmats[l // 3] = _s5_bwd(l, dh, sv, w, small, g)
        elif kind == 1:
            dh, gm = _hg_bwd(l, dh, sv, w, small, g)
        else:
            dh, gm = _mla_bwd(l, dh, sv, w, g)
        per_layer[l] = (gm, gf)
    grads = {}
    stack = lambda xs: jnp.stack([a.astype(F32) if a.dtype != BF16 else a for a in xs])
    grads["ln_mix_g"] = stack([per_layer[l][0]["ln_g"] for l in range(DEPTH)])
    grads["ln_mix_b"] = stack([per_layer[l][0]["ln_b"] for l in range(DEPTH)])
    grads["ln_ffn_g"] = stack([per_layer[l][1]["ln_g"] for l in range(DEPTH)])
    grads["ln_ffn_b"] = stack([per_layer[l][1]["ln_b"] for l in range(DEPTH)])
    grads["ffn_w_in"] = stack([per_layer[l][1]["w_in"] for l in range(DEPTH)])
    grads["ffn_w_out"] = stack([per_layer[l][1]["w_out"] for l in range(DEPTH)])
    s5_layers = [l for l in range(DEPTH) if LAYER_MIXER[l] == 0]
    for k in ("w_in", "w_glu", "w_out") + _S5_VECTORS:
        grads["s5_" + k] = stack([per_layer[l][0][k] for l in s5_layers])
    d_ssm = s5_mats_vjp({k: jnp.stack([s5_dmats[s][k] for s in range(len(s5_layers))]) for k in s5_mats})
    for k, v in zip(_S5_SSM, d_ssm):
        grads["s5_" + k] = v
    hg = per_layer[1][0]
    grads["hg_w_in"], grads["hg_w_out"] = hg["w_in"][None], hg["w_out"][None]
    grads["hg_lower_bound"], grads["hg_norm_g"] = hg["lower_bound"], hg["norm_g"][None]
    for k, v in per_layer[2][0].items():
        if not k.startswith("ln_"):
            grads["mla_" + k] = v[None]
    return loss, dh, grads


def _here():
    return lax.axis_index("x"), lax.axis_index("y"), lax.axis_index("c")


def _any_spec():
    return pl.BlockSpec(memory_space=pl.ANY)


def _chip_exchange(name, xs, scatter):
    n = len(xs)

    def body(*refs):
        ins, outs = refs[:n], refs[n:2 * n]
        send_sems, recv_sems, loc_sems = refs[2 * n:]
        x, y, c = _here()
        me = 2 * x + y
        peers = [(1 - x, y), (x, 1 - y), (1 - x, 1 - y)]
        copies = []
        for t in range(n):
            src_of = (lambda p, t=t: ins[t].at[p]) if scatter else (lambda p, t=t: ins[t])
            loc = pltpu.make_async_copy(src_of(me), outs[t].at[me], loc_sems.at[t])
            loc.start()
            copies.append(loc)
            for j, (px, py) in enumerate(peers):
                cp = pltpu.make_async_remote_copy(
                    src_ref=src_of(2 * px + py), dst_ref=outs[t].at[me], send_sem=send_sems.at[t, j],
                    recv_sem=recv_sems.at[t, j], device_id=(px, py, c), device_id_type=MESH)
                cp.start()
                copies.append(cp)
        for cp in copies:
            cp.wait()

    out_shape = [jax.ShapeDtypeStruct(a.shape if scatter else (N_CHIPS,) + a.shape, a.dtype) for a in xs]
    return pl.pallas_call(
        body, name=name, in_specs=[_any_spec()] * n, out_specs=[_any_spec()] * n, out_shape=out_shape,
        scratch_shapes=[pltpu.SemaphoreType.DMA((n, 3)), pltpu.SemaphoreType.DMA((n, 3)), pltpu.SemaphoreType.DMA((n,))],
    )(*xs)


def _core_exchange(name, a):
    def body(a_ref, o_ref, send_sem, recv_sem, loc_sem):
        x, y, c = _here()
        loc = pltpu.make_async_copy(a_ref, o_ref.at[c], loc_sem)
        loc.start()
        cp = pltpu.make_async_remote_copy(src_ref=a_ref, dst_ref=o_ref.at[c], send_sem=send_sem, recv_sem=recv_sem,
                                          device_id=(x, y, 1 - c), device_id_type=MESH)
        cp.start()
        cp.wait()
        loc.wait()

    return pl.pallas_call(
        body, name=name, in_specs=[_any_spec()], out_specs=_any_spec(),
        out_shape=jax.ShapeDtypeStruct((2,) + a.shape, a.dtype),
        scratch_shapes=[pltpu.SemaphoreType.DMA, pltpu.SemaphoreType.DMA, pltpu.SemaphoreType.DMA],
    )(a)


def _all_exchange(name, a):
    def body(a_ref, o_ref, send_sems, recv_sems, loc_sem):
        x, y, c = _here()
        me = 4 * x + 2 * y + c
        loc = pltpu.make_async_copy(a_ref, o_ref.at[me], loc_sem)
        loc.start()
        copies = [loc]
        for mask in range(1, N_DEV):
            fx, fy, fc = (mask >> 2) & 1, (mask >> 1) & 1, mask & 1
            peer = (1 - x if fx else x, 1 - y if fy else y, 1 - c if fc else c)
            cp = pltpu.make_async_remote_copy(src_ref=a_ref, dst_ref=o_ref.at[me], send_sem=send_sems.at[mask - 1],
                                              recv_sem=recv_sems.at[mask - 1], device_id=peer, device_id_type=MESH)
            cp.start()
            copies.append(cp)
        for cp in copies:
            cp.wait()

    return pl.pallas_call(
        body, name=name, in_specs=[_any_spec()], out_specs=_any_spec(),
        out_shape=jax.ShapeDtypeStruct((N_DEV,) + a.shape, a.dtype),
        scratch_shapes=[pltpu.SemaphoreType.DMA((N_DEV - 1,)), pltpu.SemaphoreType.DMA((N_DEV - 1,)), pltpu.SemaphoreType.DMA],
    )(a)


def _sum_leading(name, a, out_dtype=F32):
    n, R, C = a.shape
    tr = _tile(R, 512, 16)

    def fn(a_t):
        s = a_t[0].astype(F32)
        for i in range(1, n):
            s = s + a_t[i].astype(F32)
        return [s]

    return _grid_call(name, (R // tr,), [(a, (n, tr, C), lambda i: (0, i, 0))],
                      [((R, C), out_dtype, (tr, C), lambda i: (i, 0))], fn)[0]


def _adamw(name, g_parts, w, m, v):
    R, C = w.shape
    tr = _tile(R, 256, 8)
    ng = len(g_parts)
    c1 = 1.0 / (1.0 - ADAM_B1 ** ADAM_STEP)
    c2 = 1.0 / (1.0 - ADAM_B2 ** ADAM_STEP)

    def fn(*tiles):
        g = tiles[0].astype(F32)
        for t in tiles[1:ng]:
            g = g + t.astype(F32)
        w_t, m_t, v_t = tiles[ng:]
        m_n = ADAM_B1 * m_t + (1.0 - ADAM_B1) * g
        v_n = ADAM_B2 * v_t + (1.0 - ADAM_B2) * (g * g)
        delta = -ADAM_LR * ((m_n * c1) / (jnp.sqrt(v_n * c2) + ADAM_EPS) + ADAM_WD * w_t)
        return [g, delta, m_n, v_n]

    spec = lambda a: (a, (tr, C), lambda i: (i, 0))
    return _grid_call(name, (R // tr,), [spec(a) for a in list(g_parts) + [w, m, v]],
                      [((R, C), F32, (tr, C), lambda i: (i, 0))] * 4, fn)


_WEIGHTS = ("ln_mix_g", "ln_mix_b", "ln_ffn_g", "ln_ffn_b", "ffn_w_in", "ffn_w_out", "s5_w_in", "s5_lam_re", "s5_lam_im",
            "s5_log_step", "s5_b_re", "s5_b_im", "s5_c_re", "s5_c_im", "s5_d", "s5_w_glu", "s5_b_glu", "s5_w_out", "hg_w_in",
            "hg_lower_bound", "hg_norm_g", "hg_w_out", "mla_w_in", "mla_q_norm_g", "mla_w_q_b", "mla_kv_norm_g", "mla_w_kv_b",
            "mla_w_out")
_BIG = {"ffn_w_in": 2, "ffn_w_out": 1, "s5_w_in": 1, "s5_w_glu": 1, "s5_w_out": 1, "hg_w_in": 2, "hg_w_out": 1,
        "mla_w_in": 1, "mla_w_q_b": 2, "mla_w_kv_b": 2, "mla_w_out": 1}
_SMALL_SHARDED = {"s5_d": 1, "s5_b_glu": 1, "mla_q_norm_g": 1, "mla_kv_norm_g": 1}
_REPLICATED = tuple(n for n in _WEIGHTS if n not in _BIG and n not in _SMALL_SHARDED)
LANES = 1024


def _pack(arrs, dtype, row_mult, lead=0):
    rows, segs, r = [], [], 0
    for a in arrs:
        lead_shape = a.shape[:lead]
        flat = a.astype(dtype).reshape(lead_shape + (-1,))
        n = -(-flat.shape[-1] // LANES)
        flat = jnp.pad(flat, [(0, 0)] * lead + [(0, n * LANES - flat.shape[-1])])
        rows.append(flat.reshape(lead_shape + (n, LANES)))
        segs.append((r, n))
        r += n
    pad = -r % row_mult
    if pad:
        rows.append(jnp.zeros(rows[0].shape[:lead] + (pad, LANES), dtype))
    return jnp.concatenate(rows, axis=lead), segs


def _unpack(packed, segs, shapes):
    out = []
    for (r0, n), shp in zip(segs, shapes):
        size = int(np.prod(shp))
        out.append(packed[..., r0:r0 + n, :].reshape(packed.shape[:-2] + (n * LANES,))[..., :size].reshape(packed.shape[:-2] + tuple(shp)))
    return out


def _unshard(stacked, axis):
    moved = jnp.moveaxis(stacked, 0, axis)
    shp = list(moved.shape)
    return moved.reshape(shp[:axis] + [shp[axis] * shp[axis + 1]] + shp[axis + 2:])


def _shard_split(full, axis):
    shp = list(full.shape)
    a = full.reshape(shp[:axis] + [N_CHIPS, shp[axis] // N_CHIPS] + shp[axis + 1:])
    return jnp.moveaxis(a, axis, 0)


def _train_step(x, positions, target, w, m, v):
    big, small_sh = list(_BIG), list(_SMALL_SHARDED)
    chip = 2 * lax.axis_index("x") + lax.axis_index("y")

    big_pack, big_segs = _pack([w[n] for n in big], BF16, 16)
    sm_pack, sm_segs = _pack([w[n] for n in small_sh], F32, 8)
    big_all, sm_all = _chip_exchange("gather_weights", [big_pack, sm_pack], scatter=False)
    W = {n: w[n] for n in _REPLICATED}
    for n, s in zip(big, _unpack(big_all, big_segs, [w[n].shape for n in big])):
        W[n] = _unshard(s, _BIG[n])
    for n, s in zip(small_sh, _unpack(sm_all, sm_segs, [w[n].shape for n in small_sh])):
        W[n] = _unshard(s, _SMALL_SHARDED[n])

    loss_local, grad_x, G = _local_step(x, positions, target, W)
    loss = lax.psum(loss_local, ("x", "y", "c"))
    out = {}

    g_pack, _ = _pack([_shard_split(G[n].astype(BF16), _BIG[n]) for n in big], BF16, 16, lead=1)
    recv = _chip_exchange("scatter_grads", [g_pack], scatter=True)[0]
    pair = _core_exchange("swap_core_sums", _sum_leading("sum_chips", recv, BF16))
    for n, (r0, nr) in zip(big, big_segs):
        as_rows = lambda a: a.reshape(nr, LANES)
        res = _adamw("adamw_" + n, [pair[0, r0:r0 + nr], pair[1, r0:r0 + nr]], as_rows(w[n]), as_rows(m[n]), as_rows(v[n]))
        out[n] = tuple(r.reshape(w[n].shape) for r in res)

    small = list(_REPLICATED) + small_sh
    s_pack, s_segs = _pack([G[n] for n in small], F32, 16)
    total = _sum_leading("sum_small", _all_exchange("gather_small_grads", s_pack))
    g_small = dict(zip(small, _unpack(total, s_segs, [G[n].shape for n in small])))
    for n in small_sh:
        width = w[n].shape[1]
        g_small[n] = lax.dynamic_slice_in_dim(g_small[n], chip * width, width, axis=1)
    packs = [_pack([d[n] for n in small], F32, 8)[0] for d in (g_small, w, m, v)]
    _, a_segs = _pack([w[n] for n in small], F32, 8)
    res = _adamw("adamw_small", [packs[0]], packs[1], packs[2], packs[3])
    unpacked = [_unpack(r, a_segs, [w[n].shape for n in small]) for r in res]
    for i, n in enumerate(small):
        out[n] = tuple(u[i] for u in unpacked)
    return loss, grad_x, out


def kernel(x, positions, ln_mix_g, ln_mix_b, ln_ffn_g, ln_ffn_b, ffn_w_in, ffn_w_out, s5_w_in, s5_lam_re, s5_lam_im,
           s5_log_step, s5_b_re, s5_b_im, s5_c_re, s5_c_im, s5_d, s5_w_glu, s5_b_glu, s5_w_out, hg_w_in,
           hg_lower_bound, hg_norm_g, hg_w_out, mla_w_in, mla_q_norm_g, mla_w_q_b, mla_kv_norm_g, mla_w_kv_b,
           mla_w_out, loss_target, m_ln_mix_g, m_ln_mix_b, m_ln_ffn_g, m_ln_ffn_b, m_ffn_w_in, m_ffn_w_out, m_s5_w_in,
           m_s5_lam_re, m_s5_lam_im, m_s5_log_step, m_s5_b_re, m_s5_b_im, m_s5_c_re, m_s5_c_im, m_s5_d, m_s5_w_glu,
           m_s5_b_glu, m_s5_w_out, m_hg_w_in, m_hg_lower_bound, m_hg_norm_g, m_hg_w_out, m_mla_w_in, m_mla_q_norm_g,
           m_mla_w_q_b, m_mla_kv_norm_g, m_mla_w_kv_b, m_mla_w_out, v_ln_mix_g, v_ln_mix_b, v_ln_ffn_g, v_ln_ffn_b,
           v_ffn_w_in, v_ffn_w_out, v_s5_w_in, v_s5_lam_re, v_s5_lam_im, v_s5_log_step, v_s5_b_re, v_s5_b_im,
           v_s5_c_re, v_s5_c_im, v_s5_d, v_s5_w_glu, v_s5_b_glu, v_s5_w_out, v_hg_w_in, v_hg_lower_bound, v_hg_norm_g,
           v_hg_w_out, v_mla_w_in, v_mla_q_norm_g, v_mla_w_q_b, v_mla_kv_norm_g, v_mla_w_kv_b, v_mla_w_out):
    args = (ln_mix_g, ln_mix_b, ln_ffn_g, ln_ffn_b, ffn_w_in, ffn_w_out, s5_w_in, s5_lam_re, s5_lam_im,
            s5_log_step, s5_b_re, s5_b_im, s5_c_re, s5_c_im, s5_d, s5_w_glu, s5_b_glu, s5_w_out, hg_w_in,
            hg_lower_bound, hg_norm_g, hg_w_out, mla_w_in, mla_q_norm_g, mla_w_q_b, mla_kv_norm_g, mla_w_kv_b,
            mla_w_out, m_ln_mix_g, m_ln_mix_b, m_ln_ffn_g, m_ln_ffn_b, m_ffn_w_in, m_ffn_w_out,
            m_s5_w_in, m_s5_lam_re, m_s5_lam_im, m_s5_log_step, m_s5_b_re, m_s5_b_im, m_s5_c_re, m_s5_c_im, m_s5_d,
            m_s5_w_glu, m_s5_b_glu, m_s5_w_out, m_hg_w_in, m_hg_lower_bound, m_hg_norm_g, m_hg_w_out, m_mla_w_in,
            m_mla_q_norm_g, m_mla_w_q_b, m_mla_kv_norm_g, m_mla_w_kv_b, m_mla_w_out, v_ln_mix_g, v_ln_mix_b,
            v_ln_ffn_g, v_ln_ffn_b, v_ffn_w_in, v_ffn_w_out, v_s5_w_in, v_s5_lam_re, v_s5_lam_im, v_s5_log_step,
            v_s5_b_re, v_s5_b_im, v_s5_c_re, v_s5_c_im, v_s5_d, v_s5_w_glu, v_s5_b_glu, v_s5_w_out, v_hg_w_in,
            v_hg_lower_bound, v_hg_norm_g, v_hg_w_out, v_mla_w_in, v_mla_q_norm_g, v_mla_w_q_b, v_mla_kv_norm_g,
            v_mla_w_kv_b, v_mla_w_out)
    nw = len(_WEIGHTS)
    w = dict(zip(_WEIGHTS, args[:nw]))
    m = dict(zip(_WEIGHTS, args[nw:2 * nw]))
    v = dict(zip(_WEIGHTS, args[2 * nw:]))
    loss, grad_x, out = _train_step(x[0], positions[0], loss_target[0], w, m, v)
    res = [loss, grad_x[None]]
    for i in range(4):
        res += [out[n][i] for n in _WEIGHTS]
    return tuple(res)
```

```python
import functools
import math

import numpy as np
import jax
import jax.numpy as jnp
from jax import lax
from jax.experimental import pallas as pl
from jax.experimental.pallas import tpu as pltpu

F32 = jnp.float32
BF16 = jnp.bfloat16

D_MODEL = 1024
DEPTH = 4
LAYER_MIXER = (0, 1, 2, 0)
S5_GROUP = 16
S5_GROUPS = 64
S5_STATE = 64
S5_CHUNK = 16
HG_HEADS = 8
HG_DIM = 128
HG_CHUNK = 128
MLA_HEADS = 8
MLA_NOPE = 128
MLA_ROPE = 64
MLA_V = 128
MLA_Q_LORA = 384
MLA_KV_LORA = 256
ROPE_THETA = 10000.0
FFN_HIDDEN = 2816
ALPHA = (2 * DEPTH) ** 0.25
LN_EPS = 1e-5
RMS_EPS = 1e-6
ADAM_LR, ADAM_B1, ADAM_B2, ADAM_EPS, ADAM_WD, ADAM_STEP = 0.001, 0.9, 0.999, 1e-08, 0.01, 10
VMEM_LIMIT_BYTES = 56 * 1024 * 1024
MESH = pl.DeviceIdType.MESH
N_CHIPS = 4
N_DEV = 8


def _cparams():
    return pltpu.CompilerParams(vmem_limit_bytes=VMEM_LIMIT_BYTES)


def _tile(n, want, mult):
    t = min(want, n)
    t -= t % mult
    while t >= mult:
        if n % t == 0:
            return t
        t -= mult
    return n


def _sigmoid(x):
    return 1.0 / (1.0 + jnp.exp(-x))


def _silu(x):
    return x * _sigmoid(x)


def _dsilu(x):
    s = _sigmoid(x)
    return s * (1.0 + x * (1.0 - s))


_GELU_C = math.sqrt(2.0 / math.pi)


def _gelu(x):
    return 0.5 * x * (1.0 + jnp.tanh(_GELU_C * (x + 0.044715 * x * x * x)))


def _dgelu(x):
    t = jnp.tanh(_GELU_C * (x + 0.044715 * x * x * x))
    return 0.5 * (1.0 + t) + 0.5 * x * (1.0 - t * t) * _GELU_C * (1.0 + 3 * 0.044715 * x * x)


def _layer_norm(z, g, b):
    mu = jnp.mean(z, axis=-1, keepdims=True)
    zc = z - mu
    var = jnp.mean(zc * zc, axis=-1, keepdims=True)
    return zc * lax.rsqrt(var + LN_EPS) * g + b


def _bcast8(row):
    return jnp.broadcast_to(row, (8, row.shape[-1]))


def _stack_rows8(rows):
    n = rows[0].shape[-1]
    idx = lax.broadcasted_iota(jnp.int32, (8, n), 0)
    out = jnp.zeros((8, n), F32)
    for i, r in enumerate(rows):
        out = jnp.where(idx == i, _bcast8(r), out)
    return out


def _psum_rows(a):
    return a.reshape(-1, 8, a.shape[-1])[:, 0, :].sum(axis=0)


_DN = {"nn": ((1,), (0,)), "nt": ((1,), (1,)), "tn": ((0,), (0,))}


def _mm(name, grid, a_defs, b_defs, pairs, n_acc, acc_shape, extra_defs, out_defs, epilogue):
    nk = grid[2]
    na, nb, ne, no = len(a_defs), len(b_defs), len(extra_defs), len(out_defs)

    def body(*refs):
        a_refs = refs[:na]
        b_refs = refs[na:na + nb]
        e_refs = refs[na + nb:na + nb + ne]
        o_refs = refs[na + nb + ne:na + nb + ne + no]
        acc = refs[-1]
        k = pl.program_id(2)

        @pl.when(k == 0)
        def _():
            acc[...] = jnp.zeros(acc.shape, F32)

        for (ai, bi, ci, mode) in pairs:
            a = a_refs[ai][...].astype(BF16)
            b = b_refs[bi][...].astype(BF16)
            acc[ci] += lax.dot_general(a, b, (_DN[mode], ((), ())), preferred_element_type=F32)

        @pl.when(k == nk - 1)
        def _():
            outs = epilogue([acc[c] for c in range(n_acc)], *[e[...] for e in e_refs])
            for o_ref, o in zip(o_refs, outs):
                o_ref[...] = o.astype(o_ref.dtype)

    in_specs = [pl.BlockSpec(d[1], d[2]) for d in list(a_defs) + list(b_defs) + list(extra_defs)]
    out_specs = [pl.BlockSpec(d[2], d[3]) for d in out_defs]
    out_shape = [jax.ShapeDtypeStruct(d[0], d[1]) for d in out_defs]
    res = pl.pallas_call(
        body, name=name, grid=grid, in_specs=in_specs, out_specs=out_specs, out_shape=out_shape,
        scratch_shapes=[pltpu.VMEM((n_acc,) + tuple(acc_shape), F32)], compiler_params=_cparams(),
    )(*[d[0] for d in list(a_defs) + list(b_defs) + list(extra_defs)])
    return res


def _mm_simple(name, a, b, mode, out_dtype, tm=512, tn=1024, tk=1024, extras=(), epilogue=None, n_out=1,
               out_dtypes=None, psum_outs=0, j_outer=False):
    if mode == "nn":
        (M, K), (K2, N) = a.shape, b.shape
    elif mode == "nt":
        (M, K), (N, K2) = a.shape, b.shape
    else:
        (K, M), (K2, N) = a.shape, b.shape
    assert K == K2, (name, a.shape, b.shape, mode)
    tm, tn, tk = _tile(M, tm, 8), _tile(N, tn, 128), _tile(K, tk, 128)
    grid = (M // tm, N // tn, K // tk)
    if mode == "nn":
        a_def = (a, (tm, tk), lambda i, j, k: (i, k))
        b_def = (b, (tk, tn), lambda i, j, k: (k, j))
    elif mode == "nt":
        a_def = (a, (tm, tk), lambda i, j, k: (i, k))
        b_def = (b, (tn, tk), lambda i, j, k: (j, k))
    else:
        a_def = (a, (tk, tm), lambda i, j, k: (k, i))
        b_def = (b, (tk, tn), lambda i, j, k: (k, j))
    extra_defs = []
    for e in extras:
        if e.shape[0] == 1:
            extra_defs.append((e, (1, tn), lambda i, j, k: (0, j)))
        else:
            extra_defs.append((e, (tm, tn), lambda i, j, k: (i, j)))
    out_dtypes = out_dtypes or [out_dtype] * n_out
    out_defs = [((M, N), dt, (tm, tn), lambda i, j, k: (i, j)) for dt in out_dtypes]
    out_defs += [((M // tm * 8, N), F32, (8, tn), lambda i, j, k: (i, j)) for _ in range(psum_outs)]
    if epilogue is None:
        epilogue = lambda accs: [accs[0]]
    a_defs, b_defs = [a_def], [b_def]
    if j_outer:
        swap = lambda d: d[:-1] + ((lambda f: lambda j, i, k: f(i, j, k))(d[-1]),)
        grid = (grid[1], grid[0], grid[2])
        a_defs, b_defs = [swap(a_def)], [swap(b_def)]
        extra_defs, out_defs = [swap(d) for d in extra_defs], [swap(d) for d in out_defs]
    res = _mm(name, grid, a_defs, b_defs, [(0, 0, 0, mode)], 1, (tm, tn), extra_defs, out_defs, epilogue)
    return res[0] if len(res) == 1 else res


def _rows(name, T, tm, in_defs, out_defs, fn):
    ni = len(in_defs)

    def body(*refs):
        outs = fn(*[r[...] for r in refs[:ni]])
        for o_ref, o in zip(refs[ni:], outs):
            o_ref[...] = o.astype(o_ref.dtype)

    res = pl.pallas_call(
        body, name=name, grid=(T // tm,),
        in_specs=[pl.BlockSpec(d[1], d[2]) for d in in_defs],
        out_specs=[pl.BlockSpec(d[2], d[3]) for d in out_defs],
        out_shape=[jax.ShapeDtypeStruct(d[0], d[1]) for d in out_defs],
        compiler_params=_cparams(),
    )(*[d[0] for d in in_defs])
    return res


def _grid_call(name, grid, in_defs, out_defs, fn, acc_outs=()):
    ni = len(in_defs)

    def body(*refs):
        outs = fn(*[r[...] for r in refs[:ni]])
        first = pl.program_id(len(grid) - 1) == 0
        for idx, (o_ref, o) in enumerate(zip(refs[ni:], outs)):
            if idx in acc_outs:
                @pl.when(first)
                def _(o_ref=o_ref, o=o):
                    o_ref[...] = o.astype(o_ref.dtype)

                @pl.when(jnp.logical_not(first))
                def _(o_ref=o_ref, o=o):
                    o_ref[...] += o.astype(o_ref.dtype)
            else:
                o_ref[...] = o.astype(o_ref.dtype)

    return pl.pallas_call(
        body, name=name, grid=grid,
        in_specs=[pl.BlockSpec(d[1], d[2]) for d in in_defs],
        out_specs=[pl.BlockSpec(d[2], d[3]) for d in out_defs],
        out_shape=[jax.ShapeDtypeStruct(d[0], d[1]) for d in out_defs],
        compiler_params=_cparams(),
    )(*[d[0] for d in in_defs])


def _rt(a, tm):
    return (a, (tm, a.shape[1]), lambda i: (i, 0))


def _full(a):
    return (a, a.shape, lambda i: (0,) * a.ndim)


def _rt_out(T, n, dt, tm):
    return ((T, n), dt, (tm, n), lambda i: (i, 0))


def _ps_out(T, n, tm):
    return ((T // tm * 8, n), F32, (8, n), lambda i: (i, 0))


def _out_proj_ln(name, a, w, h_in, g, b):
    def epi(accs, h_t, g_t, b_t):
        z = ALPHA * h_t + accs[0]
        return [z, _layer_norm(z, g_t, b_t)]
    return _mm_simple(name, a, w, "nn", F32, tm=512, tn=D_MODEL, tk=FFN_HIDDEN, extras=(h_in, g, b), epilogue=epi, n_out=2)


def _ln_bwd(name, dh, z, g):
    T = dh.shape[0]
    tm = _tile(T, 512, 8)

    def fn(dh_t, z_t, g_t):
        mu = jnp.mean(z_t, axis=-1, keepdims=True)
        zc = z_t - mu
        var = jnp.mean(zc * zc, axis=-1, keepdims=True)
        rstd = lax.rsqrt(var + LN_EPS)
        xhat = zc * rstd
        dxh = dh_t * g_t
        m1 = jnp.mean(dxh, axis=-1, keepdims=True)
        m2 = jnp.mean(dxh * xhat, axis=-1, keepdims=True)
        dz = rstd * (dxh - m1 - xhat * m2)
        return [dz, _bcast8(jnp.sum(dh_t * xhat, axis=0, keepdims=True)), _bcast8(jnp.sum(dh_t, axis=0, keepdims=True))]

    dz, pg, pb = _rows(name, T, tm, [_rt(dh, tm), _rt(z, tm), _full(g)],
                       [_rt_out(T, D_MODEL, F32, tm), _ps_out(T, D_MODEL, tm), _ps_out(T, D_MODEL, tm)], fn)
    return dz, _psum_rows(pg), _psum_rows(pb)


def _ffn_fwd(l, h, w_in, w_out, g, b):
    T = h.shape[0]
    tm, tn = _tile(T, 512, 8), 1408
    nj = FFN_HIDDEN // tn
    grid = (nj, T // tm, 1)

    def epi(accs):
        gg, uu = accs
        return [gg, uu, _silu(gg) * uu]

    G, U, A = _mm(
        f"ffn{l}_in", grid, [(h, (tm, D_MODEL), lambda j, i, k: (i, 0))],
        [(w_in, (D_MODEL, tn), lambda j, i, k: (0, j)), (w_in, (D_MODEL, tn), lambda j, i, k: (0, j + nj))],
        [(0, 0, 0, "nn"), (0, 1, 1, "nn")], 2, (tm, tn), [],
        [((T, FFN_HIDDEN), F32, (tm, tn), lambda j, i, k: (i, j)),
         ((T, FFN_HIDDEN), F32, (tm, tn), lambda j, i, k: (i, j)),
         ((T, FFN_HIDDEN), BF16, (tm, tn), lambda j, i, k: (i, j))], epi)
    z, h_out = _out_proj_ln(f"ffn{l}_out", A, w_out, h, g, b)
    return h_out, (h, G, U, A, z)


def _ffn_bwd(l, dh_out, saved, w_in, w_out, g):
    h, G, U, A, z = saved
    T = h.shape[0]
    dz, dg, db = _ln_bwd(f"ffn{l}_lnb", dh_out, z, g)

    def epi(accs, g_t, u_t):
        da = accs[0]
        return [da * u_t * _dsilu(g_t), da * _silu(g_t)]

    dG, dU = _mm_simple(f"ffn{l}_dA", dz, w_out, "nt", BF16, tm=512, tn=1408, tk=1024, extras=(G, U), epilogue=epi, n_out=2,
                        j_outer=True)
    dw_out = _mm_simple(f"ffn{l}_dWout", A, dz, "tn", BF16, tm=1408, tn=1024, tk=512)
    dw_g = _mm_simple(f"ffn{l}_dWg", h, dG, "tn", BF16, tm=1024, tn=1408, tk=512)
    dw_u = _mm_simple(f"ffn{l}_dWu", h, dU, "tn", BF16, tm=1024, tn=1408, tk=512)
    tm, tk = _tile(T, 256, 8), FFN_HIDDEN
    dh = _mm(
        f"ffn{l}_dX", (T // tm, 1, 1),
        [(dG, (tm, tk), lambda i, j, k: (i, 0)), (dU, (tm, tk), lambda i, j, k: (i, 0))],
        [(w_in, (D_MODEL, tk), lambda i, j, k: (0, 0)), (w_in, (D_MODEL, tk), lambda i, j, k: (0, 1))],
        [(0, 0, 0, "nt"), (1, 1, 0, "nt")], 1, (tm, D_MODEL),
        [(dz, (tm, D_MODEL), lambda i, j, k: (i, 0))],
        [((T, D_MODEL), F32, (tm, D_MODEL), lambda i, j, k: (i, 0))],
        lambda accs, dz_t: [accs[0] + ALPHA * dz_t])[0]
    return dh, dict(w_in=jnp.concatenate([dw_g, dw_u], axis=1), w_out=dw_out, ln_g=dg, ln_b=db)


def _s5_matrices(lam_re, lam_im, log_step, b_re, b_im, c_re, c_im):
    L, hp = S5_CHUNK, lax.Precision.HIGHEST
    out = {}
    mt_total = 0.0
    for d in range(2):
        lam = lax.complex(lam_re[d], lam_im[d])
        step = jnp.exp(log_step[d])[:, None]
        lam_dt = lam * step
        lam_bar = jnp.exp(lam_dt)
        b_bar = ((lam_bar - 1.0) / lam)[..., None] * lax.complex(b_re[d], b_im[d])
        c = lax.complex(c_re[d], c_im[d])
        pw = jnp.exp(lam_dt[None] * jnp.arange(L + 1, dtype=F32)[:, None, None])
        kj = jnp.einsum("ghp,jgp,gpk->gjhk", c, pw[:L], b_bar, precision=hp).real
        lag = np.arange(L)[None, :] - np.arange(L)[:, None]
        lag = lag if d == 0 else -lag
        sel = np.stack([(lag == j) for j in range(L)]).astype(np.float32)
        mt = jnp.einsum("jst,gjab->gsbta", sel, kj, precision=hp).reshape(S5_GROUPS, 16 * L, 16 * L)
        mt_total = mt_total + mt
        pw_dist = jnp.flip(pw[:L], 0) if d == 0 else pw[:L]
        pc = pw_dist.transpose(1, 0, 2)[:, :, None, :] * b_bar.transpose(0, 2, 1)[:, None, :, :]
        pm = jnp.concatenate([pc.real, pc.imag], axis=-1).reshape(S5_GROUPS, 16 * L, 2 * S5_STATE)
        pw_read = pw[1:] if d == 0 else jnp.flip(pw[1:], 0)
        qc = c[:, None, :, :] * pw_read.transpose(1, 0, 2)[:, :, None, :]
        qm = jnp.concatenate([qc.real, -qc.imag], axis=-1).reshape(S5_GROUPS, 16 * L, 2 * S5_STATE).transpose(0, 2, 1)
        a = pw[L]
        out[f"p{d}"], out[f"q{d}"] = pm, qm
        out[f"a{d}"] = jnp.concatenate([a.real, a.imag], axis=-1)
    out["mt"] = mt_total
    return out


def _s5_apow(lam_re, lam_im, log_step, n_steps, conj):
    lam_dt = lax.complex(lam_re, lam_im) * jnp.exp(log_step)[..., None]
    k = (S5_CHUNK * 2.0 ** jnp.arange(n_steps, dtype=F32))[None, None, :, None]
    a = jnp.exp(lam_dt[:, :, None, :] * k)
    re, im = a.real, (-a.imag if conj else a.imag)
    return jnp.stack([jnp.concatenate([re, re], -1), jnp.concatenate([-im, im], -1)], axis=3)


def _shift_rows(x, s, down):
    n = x.shape[0]
    if s >= n:
        return jnp.zeros_like(x)
    if s % 8 == 0:
        z = jnp.zeros((s, x.shape[1]), x.dtype)
        return jnp.concatenate([z, x[:n - s]], axis=0) if down else jnp.concatenate([x[s:], z], axis=0)
    row = lax.broadcasted_iota(jnp.int32, x.shape, 0)
    if down:
        return jnp.where(row >= s, pltpu.roll(x, s, 0), 0.0)
    return jnp.where(row < n - s, pltpu.roll(x, n - s, 0), 0.0)


def _cmul(x, a1, a2):
    return x * a1 + pltpu.roll(x, S5_STATE, 1) * a2


def _chunk_scan(s, apow_ref, down):
    n = s.shape[0]
    k, sh = 0, 1
    while sh < n:
        s = s + _cmul(_shift_rows(s, sh, down), apow_ref[k, 0:1, :], apow_ref[k, 1:2, :])
        k, sh = k + 1, sh * 2
    return s


def _s5_scan_fwd(name, ug, mats, apow):
    G, C, W = ug.shape
    n_steps = apow.shape[2]

    def body(u_ref, mt_ref, p0_ref, p1_ref, q0_ref, q1_ref, ap0_ref, ap1_ref, y_ref, h0_ref, h1_ref):
        u = u_ref[...]
        s0 = jnp.dot(u, p0_ref[...], preferred_element_type=F32)
        s1 = jnp.dot(u, p1_ref[...], preferred_element_type=F32)
        h0 = _shift_rows(_chunk_scan(s0, ap0_ref, True), 1, True)
        h1 = _shift_rows(_chunk_scan(s1, ap1_ref, False), 1, False)
        y = jnp.dot(u, mt_ref[...], preferred_element_type=F32)
        y += jnp.dot(h0.astype(BF16), q0_ref[...], preferred_element_type=F32)
        y += jnp.dot(h1.astype(BF16), q1_ref[...], preferred_element_type=F32)
        y_ref[...] = y.astype(y_ref.dtype)
        h0_ref[...] = h0
        h1_ref[...] = h1

    def gspec(shape):
        return pl.BlockSpec((None,) + shape, lambda g: (g,) + (0,) * len(shape))

    ap0, ap1 = apow[0], apow[1]
    return pl.pallas_call(
        body, name=name, grid=(G,),
        in_specs=[gspec((C, W)), gspec((W, W)), gspec((W, 128)), gspec((W, 128)), gspec((128, W)), gspec((128, W)),
                  gspec((n_steps, 2, 128)), gspec((n_steps, 2, 128))],
        out_specs=[gspec((C, W)), gspec((C, 128)), gspec((C, 128))],
        out_shape=[jax.ShapeDtypeStruct((G, C, W), BF16), jax.ShapeDtypeStruct((G, C, 128), F32),
                   jax.ShapeDtypeStruct((G, C, 128), F32)],
        compiler_params=_cparams(),
    )(ug, mats["mt"].astype(BF16), mats["p0"].astype(BF16), mats["p1"].astype(BF16),
      mats["q0"].astype(BF16), mats["q1"].astype(BF16), ap0, ap1)


def _s5_scan_bwd(name, dyg, ug, h0, h1, mats, apow_conj):
    G, C, W = ug.shape
    n_steps = apow_conj.shape[2]

    def body(dy_ref, u_ref, h0_ref, h1_ref, mt_ref, p0_ref, p1_ref, q0_ref, q1_ref, ap0_ref, ap1_ref,
             du_ref, dmt_ref, dp0_ref, dp1_ref, dq0_ref, dq1_ref, da_ref):
        dy, u = dy_ref[...], u_ref[...]
        nt, tn = (_DN["nt"], ((), ())), (_DN["tn"], ((), ()))
        dh0 = lax.dot_general(dy, q0_ref[...], nt, preferred_element_type=F32)
        dh1 = lax.dot_general(dy, q1_ref[...], nt, preferred_element_type=F32)
        ds0 = _chunk_scan(_shift_rows(dh0, 1, False), ap0_ref, False)
        ds1 = _chunk_scan(_shift_rows(dh1, 1, True), ap1_ref, True)
        ds0b, ds1b = ds0.astype(BF16), ds1.astype(BF16)
        du = lax.dot_general(dy, mt_ref[...], nt, preferred_element_type=F32)
        du += lax.dot_general(ds0b, p0_ref[...], nt, preferred_element_type=F32)
        du += lax.dot_general(ds1b, p1_ref[...], nt, preferred_element_type=F32)
        du_ref[...] = du.astype(du_ref.dtype)
        dmt_ref[...] = lax.dot_general(u, dy, tn, preferred_element_type=F32)
        dp0_ref[...] = lax.dot_general(u, ds0b, tn, preferred_element_type=F32)
        dp1_ref[...] = lax.dot_general(u, ds1b, tn, preferred_element_type=F32)
        h0v, h1v = h0_ref[...], h1_ref[...]
        dq0_ref[...] = lax.dot_general(h0v.astype(BF16), dy, tn, preferred_element_type=F32)
        dq1_ref[...] = lax.dot_general(h1v.astype(BF16), dy, tn, preferred_element_type=F32)
        rows = [jnp.sum(ds0 * h0v, axis=0, keepdims=True), jnp.sum(ds0 * pltpu.roll(h0v, S5_STATE, 1), axis=0, keepdims=True),
                jnp.sum(ds1 * h1v, axis=0, keepdims=True), jnp.sum(ds1 * pltpu.roll(h1v, S5_STATE, 1), axis=0, keepdims=True)]
        da_ref[...] = _stack_rows8(rows)

    def gspec(shape):
        return pl.BlockSpec((None,) + shape, lambda g: (g,) + (0,) * len(shape))

    f32s = lambda *s: jax.ShapeDtypeStruct((G,) + s, F32)
    return pl.pallas_call(
        body, name=name, grid=(G,),
        in_specs=[gspec((C, W)), gspec((C, W)), gspec((C, 128)), gspec((C, 128)), gspec((W, W)), gspec((W, 128)),
                  gspec((W, 128)), gspec((128, W)), gspec((128, W)), gspec((n_steps, 2, 128)), gspec((n_steps, 2, 128))],
        out_specs=[gspec((C, W)), gspec((W, W)), gspec((W, 128)), gspec((W, 128)), gspec((128, W)), gspec((128, W)),
                   gspec((8, 128))],
        out_shape=[jax.ShapeDtypeStruct((G, C, W), BF16), f32s(W, W), f32s(W, 128), f32s(W, 128), f32s(128, W),
                   f32s(128, W), f32s(8, 128)],
        compiler_params=_cparams(),
    )(dyg, ug, h0, h1, mats["mt"].astype(BF16), mats["p0"].astype(BF16), mats["p1"].astype(BF16),
      mats["q0"].astype(BF16), mats["q1"].astype(BF16), apow_conj[0], apow_conj[1])


S5_LANE_GROUPS = 128 // S5_GROUP


def _group_select():
    e = np.zeros((S5_LANE_GROUPS, S5_CHUNK, 128, S5_CHUNK, S5_GROUP), np.float32)
    for j in range(S5_LANE_GROUPS):
        for l in range(S5_CHUNK):
            for hh in range(S5_GROUP):
                e[j, l, S5_GROUP * j + hh, l, hh] = 1.0
    return jnp.asarray(e.reshape(S5_LANE_GROUPS, S5_CHUNK * 128, S5_CHUNK * S5_GROUP), BF16)


def _to_groups(name, a):
    T = a.shape[0]
    C = T // S5_CHUNK
    cb = _tile(C, 256, 8)
    sel = _group_select()

    def body(x_ref, e_ref, o_ref):
        xcat = jnp.concatenate([x_ref[pl.ds(l, cb, stride=S5_CHUNK), :].astype(BF16) for l in range(S5_CHUNK)], axis=1)
        for j in range(S5_LANE_GROUPS):
            o_ref[j] = jnp.dot(xcat, e_ref[j], preferred_element_type=F32).astype(BF16)

    return pl.pallas_call(
        body, name=name, grid=(S5_GROUPS // S5_LANE_GROUPS, C // cb),
        in_specs=[pl.BlockSpec((cb * S5_CHUNK, 128), lambda b, r: (r, b)), pl.BlockSpec(sel.shape, lambda b, r: (0, 0, 0))],
        out_specs=pl.BlockSpec((S5_LANE_GROUPS, cb, S5_CHUNK * S5_GROUP), lambda b, r: (b, r, 0)),
        out_shape=jax.ShapeDtypeStruct((S5_GROUPS, C, S5_CHUNK * S5_GROUP), BF16), compiler_params=_cparams(),
    )(a, sel)


def _from_groups(name, a, extras, fn, n_out):
    G, C, W = a.shape
    T = C * S5_CHUNK
    cb = _tile(C, 256, 8)
    sel = _group_select()
    ne = len(extras)

    def body(*refs):
        y_ref, e_ref = refs[:2]
        e_refs, o_refs = refs[2:2 + ne], refs[2 + ne:]
        ycat = jnp.zeros((cb, S5_CHUNK * 128), F32)
        for j in range(S5_LANE_GROUPS):
            ycat = ycat + lax.dot_general(y_ref[j], e_ref[j], (_DN["nt"], ((), ())), preferred_element_type=F32)
        for l in range(S5_CHUNK):
            rows = pl.ds(l, cb, stride=S5_CHUNK)
            ex = [r[...] if r.shape[0] == 1 else r[rows, :] for r in e_refs]
            for o_ref, o in zip(o_refs, fn(ycat[:, l * 128:(l + 1) * 128], *ex)):
                o_ref[rows, :] = o

    col = pl.BlockSpec((cb * S5_CHUNK, 128), lambda b, r: (r, b))
    return pl.pallas_call(
        body, name=name, grid=(G // S5_LANE_GROUPS, C // cb),
        in_specs=[pl.BlockSpec((S5_LANE_GROUPS, cb, W), lambda b, r: (b, r, 0)),
                  pl.BlockSpec(sel.shape, lambda b, r: (0, 0, 0))]
        + [pl.BlockSpec((1, 128), lambda b, r: (0, b)) if e.shape[0] == 1 else col for e in extras],
        out_specs=[col] * n_out,
        out_shape=[jax.ShapeDtypeStruct((T, G * S5_GROUP), F32)] * n_out, compiler_params=_cparams(),
    )(a, sel, *extras)


_S5_SSM = ("lam_re", "lam_im", "log_step", "b_re", "b_im", "c_re", "c_im")


def _s5_prepare(W, T):
    sp = tuple(W["s5_" + k] for k in _S5_SSM)
    mats, mats_vjp = jax.vjp(jax.vmap(_s5_matrices), *sp)
    n_steps = max(1, int(math.log2(T // S5_CHUNK)))
    apow = jax.vmap(lambda a, b, c: _s5_apow(a, b, c, n_steps, False))(*sp[:3])
    apow_c = jax.vmap(lambda a, b, c: _s5_apow(a, b, c, n_steps, True))(*sp[:3])
    return mats, mats_vjp, apow, apow_c


def _s5_fwd(l, h, w, small, g, b, mats, apow, apow_c):
    u = _mm_simple(f"s5{l}_in", h, w["w_in"], "nn", F32)
    ug = _to_groups(f"s5{l}_togroups", u)
    yg, h0, h1 = _s5_scan_fwd(f"s5{l}_scan", ug, mats, apow)
    d_row, bglu_row = small["d"].reshape(1, -1), small["b_glu"].reshape(1, -1)
    yssm, y1 = _from_groups(f"s5{l}_fromgroups", yg, [u, d_row], lambda y, u_t, d_t: [y, _gelu(y + d_t * u_t)], 2)

    def epi(accs, y1_t, bg_t):
        gate = _sigmoid(accs[0] + bg_t)
        return [y1_t * gate, gate]

    y2, gate = _mm_simple(f"s5{l}_glu", y1, w["w_glu"], "nn", None, extras=(y1, bglu_row), epilogue=epi, n_out=2,
                          out_dtypes=[BF16, F32])
    z, h_out = _out_proj_ln(f"s5{l}_out", y2, w["w_out"], h, g, b)
    return h_out, (h, u, ug, yssm, y1, y2, gate, z, h0, h1, mats, apow_c, d_row)


def _s5_bwd(l, dh_out, saved, w, small, g):
    h, u, ug, yssm, y1, y2, gate, z, h0, h1, mats, apow_c, d_row = saved
    T = h.shape[0]
    dz, dg, db = _ln_bwd(f"s5{l}_lnb", dh_out, z, g)

    def epi1(accs, y1_t, gate_t):
        dy2 = accs[0]
        dpre = dy2 * y1_t * gate_t * (1.0 - gate_t)
        return [dpre, dy2 * gate_t, _bcast8(jnp.sum(dpre, axis=0, keepdims=True))]

    dpre, dy1a, pbg = _mm_simple(f"s5{l}_dy2", dz, w["w_out"], "nt", None, extras=(y1, gate), epilogue=epi1, n_out=2,
                                 out_dtypes=[BF16, F32], psum_outs=1)
    dw_out = _mm_simple(f"s5{l}_dWout", y2, dz, "tn", BF16, tm=1024, tn=1024, tk=512)

    def epi2(accs, dy1a_t, yssm_t, u_t, d_t):
        dy1 = accs[0] + dy1a_t
        dy = dy1 * _dgelu(yssm_t + d_t * u_t)
        return [dy, dy * d_t, _bcast8(jnp.sum(dy * u_t, axis=0, keepdims=True))]

    dy, du_skip, pdd = _mm_simple(f"s5{l}_dy1", dpre, w["w_glu"], "nt", None, extras=(dy1a, yssm, u, d_row), epilogue=epi2,
                                  n_out=2, out_dtypes=[F32, F32], psum_outs=1)
    dw_glu = _mm_simple(f"s5{l}_dWglu", y1, dpre, "tn", BF16, tm=1024, tn=1024, tk=512)
    dyg = _to_groups(f"s5{l}_togroups_b", dy)
    dug, dmt, dp0, dp1, dq0, dq1, da = _s5_scan_bwd(f"s5{l}_scanb", dyg, ug, h0, h1, mats, apow_c)
    du = _from_groups(f"s5{l}_fromgroups_b", dug, [du_skip], lambda y, skip: [y + skip], 1)[0]

    def a_grad(p, q):
        return jnp.concatenate([p[:, :S5_STATE] + p[:, S5_STATE:], q[:, S5_STATE:] - q[:, :S5_STATE]], axis=-1)

    dmats = dict(mt=dmt, p0=dp0, p1=dp1, q0=dq0, q1=dq1, a0=a_grad(da[:, 0], da[:, 1]), a1=a_grad(da[:, 2], da[:, 3]))
    dh = _mm_simple(f"s5{l}_dX", du, w["w_in"], "nt", F32, extras=(dz,), epilogue=lambda accs, dz_t: [accs[0] + ALPHA * dz_t])
    dw_in = _mm_simple(f"s5{l}_dWin", h, du, "tn", BF16, tm=1024, tn=1024, tk=512)
    grads = dict(w_in=dw_in, w_glu=dw_glu, w_out=dw_out, d=_psum_rows(pdd), b_glu=_psum_rows(pbg), ln_g=dg, ln_b=db)
    return dh, grads, dmats


def _gla_levels(lc):
    ms, m = [], lc // 2
    while m >= 1:
        ms.append(m)
        m //= 2
    return ms


def _gla_scan_matrix(lc, rev):
    r = np.arange(lc)[:, None]
    t = np.arange(lc)[None, :]
    blocks = []
    for m in _gla_levels(lc):
        same = (r // m) == (t // m)
        upper = ((r // m) % 2) == 1
        blocks.append(same & np.where(upper, t >= r, t < r))
    blocks.append(t >= r)
    blocks.append(t < r)
    if rev:
        blocks = [blk[::-1, ::-1] for blk in blocks]
    return np.concatenate(blocks, axis=1).astype(np.float32)


def _gla_gates(z, lb):
    sig = _sigmoid(z)
    ls = jnp.minimum(z, 0.0) - jnp.log(1.0 + jnp.exp(-jnp.abs(z)))
    a = jnp.log(lb)
    bb = jnp.log(1.0 - lb) + ls
    lf = jnp.maximum(a, bb) + jnp.log(1.0 + jnp.exp(-jnp.abs(a - bb)))
    return lf, (1.0 - lb) * (1.0 - sig), sig


def _gla_cumsum(lf, rev):
    b, sh = lf, 1
    while sh < lf.shape[0]:
        b = b + _shift_rows(b, sh, not rev)
        sh *= 2
    return b


def _gla_bref(b, m, rev):
    lc, n = b.shape
    idx = m if rev else m - 1
    if 2 * m >= 8:
        nb = lc // (2 * m)
        b3 = b.reshape(nb, 2 * m, n)
        return jnp.broadcast_to(b3[:, idx:idx + 1, :], (nb, 2 * m, n)).reshape(lc, n)
    row = lax.broadcasted_iota(jnp.int32, b.shape, 0)
    j = row & (2 * m - 1)
    out = b
    for jj in range(2 * m):
        if jj != idx:
            out = jnp.where(j == jj, pltpu.roll(b, (jj - idx) % lc, 0), out)
    return out


def _gla_masks(lc, rev):
    r = np.arange(lc)
    mq, bm = [], [np.eye(lc)]
    for m in _gla_levels(lc):
        isq = ((r // m) % 2) == (0 if rev else 1)
        mq.append(np.broadcast_to(isq[:, None], (lc, HG_DIM)))
        bm.append((r[:, None] // (2 * m)) == (r[None, :] // (2 * m)))
    return jnp.asarray(np.stack(mq), F32), jnp.asarray(np.stack(bm), F32)


def _gla_chunk(q, k, lf, rev, mq_ref, bm_ref):
    lc = q.shape[0]
    nt = (_DN["nt"], ((), ()))
    b = _gla_cumsum(lf, rev)
    qb, kb = q.astype(BF16), k.astype(BF16)
    sc = bm_ref[0] * lax.dot_general(qb, kb, nt, preferred_element_type=F32)
    levels = []
    for i, m in enumerate(_gla_levels(lc)):
        mq = mq_ref[i]
        mk = 1.0 - mq
        w = jnp.exp((b - _gla_bref(b, m, rev)) * (mq - mk))
        wq, wk = w * mq, w * mk
        xf, yf = q * wq, k * wk
        xb, yb = xf.astype(BF16), yf.astype(BF16)
        sc = sc + bm_ref[i + 1] * lax.dot_general(xb, yb, nt, preferred_element_type=F32)
        levels.append((wq, wk, xf, yf, xb, yb))
    return b, sc, levels


def _hg_specs(T, lc, rev, backward):
    nc = T // lc
    cc = (lambda c: nc - 1 - c) if rev != backward else (lambda c: c)
    zcol = HG_HEADS * (2 if rev else 1)
    q_spec = pl.BlockSpec((lc, HG_DIM), lambda h, c: (cc(c), h))
    z_spec = pl.BlockSpec((lc, HG_DIM), lambda h, c: (cc(c), zcol + h))
    v_spec = pl.BlockSpec((lc, HG_DIM), lambda h, c: (cc(c), 3 * HG_HEADS + h))
    lb_spec = pl.BlockSpec((1, HG_DIM), lambda h, c: (0, h))
    st_spec = pl.BlockSpec((None, None, HG_DIM, HG_DIM), lambda h, c: (h, cc(c), 0, 0))
    return nc, q_spec, z_spec, v_spec, lb_spec, st_spec


def _gla_fwd(name, proj, lb_row, rev):
    T = proj.shape[0]
    lc = _tile(T, HG_CHUNK, 8)
    nc, q_spec, z_spec, v_spec, lb_spec, st_spec = _hg_specs(T, lc, rev, False)
    last = 0 if rev else lc - 1
    mq, bm = _gla_masks(lc, rev)
    const3 = lambda a: pl.BlockSpec(a.shape, lambda h, c: (0, 0, 0))

    def body(q_ref, z_ref, v_ref, lb_ref, mq_ref, bm_ref, o_ref, st_ref, st_s):
        @pl.when(pl.program_id(1) == 0)
        def _():
            st_s[...] = jnp.zeros(st_s.shape, F32)

        q = _silu(q_ref[...])
        lf, k, _ = _gla_gates(z_ref[...], lb_ref[...])
        vb = v_ref[...].astype(BF16)
        b, sc, _ = _gla_chunk(q, k, lf, rev, mq_ref, bm_ref)
        st0 = st_s[...]
        st_ref[...] = st0
        bl = b[last:last + 1, :]
        o = jnp.dot(sc.astype(BF16), vb, preferred_element_type=F32)
        o += lax.dot_general((q * jnp.exp(b)).astype(BF16), st0.astype(BF16), (_DN["nt"], ((), ())), preferred_element_type=F32)
        o_ref[...] = o
        kd = (k * jnp.exp(bl - b)).astype(BF16)
        st_s[...] = st0 * jnp.exp(bl) + lax.dot_general(vb, kd, (_DN["tn"], ((), ())), preferred_element_type=F32)

    return pl.pallas_call(
        body, name=name, grid=(HG_HEADS, nc),
        in_specs=[q_spec, z_spec, v_spec, lb_spec, const3(mq), const3(bm)], out_specs=[q_spec, st_spec],
        out_shape=[jax.ShapeDtypeStruct((T, D_MODEL), F32), jax.ShapeDtypeStruct((HG_HEADS, nc, HG_DIM, HG_DIM), F32)],
        scratch_shapes=[pltpu.VMEM((HG_DIM, HG_DIM), F32)], compiler_params=_cparams(),
    )(proj, proj, proj, lb_row, mq, bm)


def _gla_bwd(name, proj, lb_row, do, st, rev):
    T = proj.shape[0]
    lc = _tile(T, HG_CHUNK, 8)
    nc, q_spec, z_spec, v_spec, lb_spec, st_spec = _hg_specs(T, lc, rev, True)
    wall = jnp.asarray(_gla_scan_matrix(lc, rev), BF16)
    last = 0 if rev else lc - 1
    mq, bm = _gla_masks(lc, rev)
    const3 = lambda a: pl.BlockSpec(a.shape, lambda h, c: (0, 0, 0))

    def body(q_ref, z_ref, v_ref, lb_ref, do_ref, st_ref, wall_ref, mq_ref, bm_ref, dq_ref, dz_ref, dv_ref, dlb_ref, dst_s):
        first = pl.program_id(1) == 0

        @pl.when(first)
        def _():
            dst_s[...] = jnp.zeros(dst_s.shape, F32)
            dlb_ref[...] = jnp.zeros(dlb_ref.shape, F32)

        nn, nt, tn = (_DN["nn"], ((), ())), (_DN["nt"], ((), ())), (_DN["tn"], ((), ()))
        dot = functools.partial(lax.dot_general, preferred_element_type=F32)
        qr, z, lb = q_ref[...], z_ref[...], lb_ref[...]
        q = _silu(qr)
        lf, k, sig = _gla_gates(z, lb)
        vb = v_ref[...].astype(BF16)
        b, sc, levels = _gla_chunk(q, k, lf, rev, mq_ref, bm_ref)
        st0, dst = st_ref[...], dst_s[...]
        st0b, dstb = st0.astype(BF16), dst.astype(BF16)
        dob = do_ref[...].astype(BF16)
        bl = b[last:last + 1, :]
        eb, ebl, ekd = jnp.exp(b), jnp.exp(bl), jnp.exp(bl - b)
        qe, kd = q * eb, k * ekd
        kdb = kd.astype(BF16)
        dsc = dot(dob, vb, nt)
        dv_ref[...] = dot(sc.astype(BF16), dob, tn) + dot(kdb, dstb, nt)
        dqe = dot(dob, st0b, nn)
        dkd = dot(vb, dstb, nn)
        dq = dqe * eb
        dk = dkd * ekd
        zs = []
        dsd = (bm_ref[0] * dsc).astype(BF16)
        dq += dot(dsd, k.astype(BF16), nn)
        dk += dot(dsd, q.astype(BF16), tn)
        for i, (wq, wk, xf, yf, xb, yb) in enumerate(levels):
            dsl = (bm_ref[i + 1] * dsc).astype(BF16)
            dx = dot(dsl, yb, nn)
            dy = dot(dsl, xb, tn)
            dq += dx * wq
            dk += dy * wk
            zs.append((dx * xf + dy * yf).astype(BF16))
        zs.append((dqe * qe).astype(BF16))
        zs.append((dkd * kd).astype(BF16))
        zl = jnp.sum(dst * st0, axis=0, keepdims=True) * ebl
        dlf = dot(wall_ref[...], jnp.concatenate(zs, axis=0), nn) + zl
        dst_s[...] = dst * ebl + dot(dob, qe.astype(BF16), tn)
        inv_f = jnp.exp(-lf)
        one_sig = 1.0 - sig
        dz_ref[...] = (dlf * inv_f - dk) * (1.0 - lb) * sig * one_sig
        dq_ref[...] = dq * _dsilu(qr)
        dlb_ref[...] += _bcast8(jnp.sum((dlf * inv_f - dk) * one_sig, axis=0, keepdims=True))

    big = jax.ShapeDtypeStruct((T, D_MODEL), F32)
    return pl.pallas_call(
        body, name=name, grid=(HG_HEADS, nc),
        in_specs=[q_spec, z_spec, v_spec, lb_spec, q_spec, st_spec, pl.BlockSpec(wall.shape, lambda h, c: (0, 0)),
                  const3(mq), const3(bm)],
        out_specs=[q_spec, q_spec, q_spec, pl.BlockSpec((None, 8, HG_DIM), lambda h, c: (h, 0, 0))],
        out_shape=[big, big, big, jax.ShapeDtypeStruct((HG_HEADS, 8, HG_DIM), F32)],
        scratch_shapes=[pltpu.VMEM((HG_DIM, HG_DIM), F32)], compiler_params=_cparams(),
    )(proj, proj, proj, lb_row, do, st, wall, mq, bm)


def _hg_lower_bounds(hg_lower_bound, layer):
    lbs = jax.nn.softmax(hg_lower_bound, axis=0)
    lbs = jnp.cumsum(lbs, axis=0) - lbs[0]
    return lbs[layer].reshape(1, -1)


def _hg_post(o_fw, o_bw, g_raw, ng):
    outs, ons, os_, rs = [], [], [], []
    for hd in range(o_fw.shape[1] // HG_DIM):
        sl = slice(hd * HG_DIM, (hd + 1) * HG_DIM)
        o = o_fw[:, sl] + o_bw[:, sl]
        r = lax.rsqrt(jnp.mean(o * o, axis=-1, keepdims=True) + RMS_EPS)
        on = o * r * ng
        outs.append(on * _silu(g_raw[:, sl]))
        ons.append(on)
        os_.append(o)
        rs.append(r)
    return outs, ons, os_, rs


def _hg_fwd(l, h, w, small, g, b):
    T = h.shape[0]
    tm = _tile(T, 512, 8)
    proj = _mm_simple(f"hg{l}_in", h, w["w_in"], "nn", F32, tn=1280, j_outer=True)
    lb_fn = lambda p: _hg_lower_bounds(p, l)
    lb_row, lb_vjp = jax.vjp(lb_fn, small["lower_bound"])
    o_fw, st_fw = _gla_fwd(f"hg{l}_gla_fw", proj, lb_row, False)
    o_bw, st_bw = _gla_fwd(f"hg{l}_gla_bw", proj, lb_row, True)
    ng = small["norm_g"].reshape(1, HG_DIM)

    def post(of_t, ob_t, g_t, ng_t):
        return [jnp.concatenate(_hg_post(of_t, ob_t, g_t, ng_t)[0], axis=1)]

    og = _rows(f"hg{l}_post", T, tm,
               [_rt(o_fw, tm), _rt(o_bw, tm), (proj, (tm, D_MODEL), lambda i: (i, 4)), _full(ng)],
               [_rt_out(T, D_MODEL, BF16, tm)], post)[0]
    z, h_out = _out_proj_ln(f"hg{l}_out", og, w["w_out"], h, g, b)
    return h_out, (h, proj, lb_row, lb_vjp, st_fw, st_bw, o_fw, o_bw, ng, og, z)


def _hg_bwd(l, dh_out, saved, w, small, g):
    h, proj, lb_row, lb_vjp, st_fw, st_bw, o_fw, o_bw, ng, og, z = saved
    T = h.shape[0]
    dz, dg, db = _ln_bwd(f"hg{l}_lnb", dh_out, z, g)
    tm = _tile(T, 512, 8)
    nn_tiles = D_MODEL // HG_DIM

    def epi(accs, of_t, ob_t, g_t, ng_t):
        dog = accs[0]
        _, ons, os_, rs = _hg_post(of_t, ob_t, g_t, ng_t)
        dos, dgs, dngs = [], [], []
        for hd in range(nn_tiles):
            sl = slice(hd * HG_DIM, (hd + 1) * HG_DIM)
            d, o, r = dog[:, sl], os_[hd], rs[hd]
            dgs.append(d * ons[hd] * _dsilu(g_t[:, sl]))
            don = d * _silu(g_t[:, sl])
            dngs.append(jnp.sum(don * o * r, axis=0, keepdims=True))
            dxn = don * ng_t
            dos.append(r * dxn - o * (r * r * r) * jnp.mean(dxn * o, axis=-1, keepdims=True))
        return [jnp.concatenate(dos, axis=1), jnp.concatenate(dgs, axis=1), _bcast8(jnp.concatenate(dngs, axis=1))]

    grid = (T // tm, 1, 1)
    row_map = lambda i, j, k: (i, 0)
    do, dg_raw, png = _mm(
        f"hg{l}_dog", grid, [(dz, (tm, D_MODEL), row_map)], [(w["w_out"], (D_MODEL, D_MODEL), lambda i, j, k: (0, 0))],
        [(0, 0, 0, "nt")], 1, (tm, D_MODEL),
        [(o_fw, (tm, D_MODEL), row_map), (o_bw, (tm, D_MODEL), row_map), (proj, (tm, D_MODEL), lambda i, j, k: (i, 4)),
         (ng, (1, HG_DIM), lambda i, j, k: (0, 0))],
        [((T, D_MODEL), F32, (tm, D_MODEL), row_map), ((T, D_MODEL), F32, (tm, D_MODEL), row_map),
         ((T // tm * 8, D_MODEL), F32, (8, D_MODEL), row_map)], epi)
    dw_out = _mm_simple(f"hg{l}_dWout", og, dz, "tn", BF16, tm=1024, tn=1024, tk=512)
    dq_f, dz_f, dv_f, dlb_f = _gla_bwd(f"hg{l}_glab_fw", proj, lb_row, do, st_fw, False)
    dq_b, dz_b, dv_b, dlb_b = _gla_bwd(f"hg{l}_glab_bw", proj, lb_row, do, st_bw, True)
    dproj = jnp.concatenate([dq_f + dq_b, dz_f, dz_b, dv_f + dv_b, dg_raw], axis=1).astype(BF16)
    dh = _mm_simple(f"hg{l}_dX", dproj, w["w_in"], "nt", F32, tm=256, tk=5 * D_MODEL, extras=(dz,),
                    epilogue=lambda accs, dz_t: [accs[0] + ALPHA * dz_t])
    dw_in = _mm_simple(f"hg{l}_dWin", h, dproj, "tn", BF16, tm=1024, tn=1280, tk=512)
    dlb_row = (dlb_f[:, 0, :] + dlb_b[:, 0, :]).reshape(1, D_MODEL)
    grads = dict(w_in=dw_in, w_out=dw_out, lower_bound=lb_vjp(dlb_row)[0],
                 norm_g=_psum_rows(png).reshape(nn_tiles, HG_DIM).sum(axis=0), ln_g=dg, ln_b=db)
    return dh, grads


MLA_W = 256
MLA_SCALE = (MLA_NOPE + MLA_ROPE) ** -0.5


def _swap_halves(a):
    n = a.shape[-1] // 2
    return jnp.concatenate([a[..., n:], a[..., :n]], axis=-1)


def _mla_ext_weights(w_in, w_q_b):
    w_in_ext = jnp.concatenate([w_in, _swap_halves(w_in[:, MLA_Q_LORA + MLA_KV_LORA:])], axis=1)
    wq = w_q_b.reshape(MLA_Q_LORA, MLA_HEADS, MLA_NOPE + MLA_ROPE)
    wq_ext = jnp.concatenate([wq, _swap_halves(wq[:, :, MLA_NOPE:])], axis=2).reshape(MLA_Q_LORA, MLA_HEADS * MLA_W)
    return w_in_ext, wq_ext


def _mla_ext_grads(dw_in_ext, dwq_ext):
    n_lat = MLA_Q_LORA + MLA_KV_LORA
    dw_in = jnp.concatenate([dw_in_ext[:, :n_lat], dw_in_ext[:, n_lat:n_lat + MLA_ROPE]
                             + _swap_halves(dw_in_ext[:, n_lat + MLA_ROPE:])], axis=1)
    dq = dwq_ext.reshape(MLA_Q_LORA, MLA_HEADS, MLA_W)
    dwq = jnp.concatenate([dq[:, :, :MLA_NOPE], dq[:, :, MLA_NOPE:MLA_NOPE + MLA_ROPE]
                           + _swap_halves(dq[:, :, MLA_NOPE + MLA_ROPE:])], axis=2)
    return dw_in, dwq.reshape(MLA_Q_LORA, MLA_HEADS * (MLA_NOPE + MLA_ROPE))


def _rope_table(positions):
    half = MLA_ROPE // 2
    inv_freq = 1.0 / (ROPE_THETA ** (jnp.arange(half, dtype=F32) * (2.0 / MLA_ROPE)))
    ang = positions.astype(F32)[:, None] * inv_freq
    cos, sin = jnp.cos(ang), jnp.sin(ang)
    return jnp.concatenate([cos, cos, -sin, sin], axis=1)


def _rope_sum(prod):
    return prod + pltpu.roll(prod, MLA_ROPE, 1)


def _low_half(a):
    lane = lax.broadcasted_iota(jnp.int32, a.shape, 1)
    return jnp.where(lane < MLA_ROPE, a, 0.0)


def _rms(x, g):
    r = lax.rsqrt(jnp.mean(x * x, axis=-1, keepdims=True) + RMS_EPS)
    return x * r * g


def _rms_bwd(x, g, dy):
    r = lax.rsqrt(jnp.mean(x * x, axis=-1, keepdims=True) + RMS_EPS)
    dxn = dy * g
    dx = r * dxn - x * (r * r * r) * jnp.mean(dxn * x, axis=-1, keepdims=True)
    return dx, jnp.sum(dy * x * r, axis=0, keepdims=True)


ATT_TILE = 512
ATT_BLOCK = 2048
ATT_KEYS_BWD = 1024
ATT_QUERIES_FWD = 1024


def _lanes(col, n):
    return jnp.tile(col, (1, n // 128))


def _flash_fwd(name, q, k, v):
    H, T, _ = q.shape
    tq, tkb = _tile(T, ATT_QUERIES_FWD, 8), _tile(T, ATT_BLOCK, 128)
    ts = _tile(tkb, ATT_TILE, 128)
    nk, nsub = T // tkb, tkb // ts

    def body(q_ref, k_ref, v_ref, o_ref, lse_ref, m_s, acc_s):
        ki = pl.program_id(2)

        @pl.when(ki == 0)
        def _():
            m_s[...] = jnp.full(m_s.shape, -jnp.inf, F32)
            acc_s[...] = jnp.zeros(acc_s.shape, F32)

        qv = q_ref[...]
        m, acc = m_s[...], acc_s[...]
        for j in range(nsub):
            kj, vj = k_ref[j * ts:(j + 1) * ts, :], v_ref[j * ts:(j + 1) * ts, :]
            s = lax.dot_general(qv, kj, (_DN["nt"], ((), ())), preferred_element_type=F32)
            m_new = jnp.maximum(m, jnp.max(s, axis=-1, keepdims=True))
            p = jnp.exp(s - _lanes(m_new, ts)).astype(BF16)
            acc = _lanes(jnp.exp(m - m_new), 2 * MLA_V) * acc + jnp.dot(p, vj, preferred_element_type=F32)
            m = m_new
        m_s[...], acc_s[...] = m, acc

        @pl.when(ki == nk - 1)
        def _():
            l = acc[:, MLA_V:]
            o_ref[...] = acc[:, :MLA_V] / l
            lse_ref[...] = m + jnp.log(l)

    return pl.pallas_call(
        body, name=name, grid=(H, T // tq, nk),
        in_specs=[pl.BlockSpec((None, tq, MLA_W), lambda h, i, j: (h, i, 0)),
                  pl.BlockSpec((None, tkb, MLA_W), lambda h, i, j: (h, j, 0)),
                  pl.BlockSpec((None, tkb, 2 * MLA_V), lambda h, i, j: (h, j, 0))],
        out_specs=[pl.BlockSpec((tq, MLA_V), lambda h, i, j: (i, h)), pl.BlockSpec((tq, MLA_V), lambda h, i, j: (i, h))],
        out_shape=[jax.ShapeDtypeStruct((T, H * MLA_V), F32), jax.ShapeDtypeStruct((T, H * MLA_V), F32)],
        scratch_shapes=[pltpu.VMEM((tq, MLA_V), F32), pltpu.VMEM((tq, 2 * MLA_V), F32)],
        compiler_params=_cparams(),
    )(q, k, v)


def _flash_bwd(name, q, k, v, do, lse, delta):
    H, T, _ = q.shape
    nt, tn = (_DN["nt"], ((), ())), (_DN["tn"], ((), ()))
    tk, tqb = _tile(T, ATT_KEYS_BWD, 128), _tile(T, ATT_BLOCK, 128)
    tqs = _tile(tqb, ATT_TILE, 128)
    nk, nqb, nsub = T // tk, T // tqb, tqb // tqs

    def body(q_ref, k_ref, v_ref, do_ref, lse_ref, dl_ref, dq_ref, dk_ref, dv_ref, dk_s, dv_s):
        ki, qb = pl.program_id(1), pl.program_id(2)

        @pl.when(jnp.logical_and(ki == 0, qb == 0))
        def _():
            dq_ref[...] = jnp.zeros(dq_ref.shape, F32)

        @pl.when(qb == 0)
        def _():
            dk_s[...] = jnp.zeros(dk_s.shape, F32)
            dv_s[...] = jnp.zeros(dv_s.shape, F32)

        kv, vv = k_ref[...], v_ref[...]
        dk, dv = dk_s[...], dv_s[...]
        for j in range(nsub):
            sl = slice(j * tqs, (j + 1) * tqs)
            qj, doj = q_ref[sl, :], do_ref[sl, :]
            s = lax.dot_general(qj, kv, nt, preferred_element_type=F32)
            dp = lax.dot_general(doj, vv, nt, preferred_element_type=F32)
            p = jnp.exp(s - _lanes(lse_ref[sl, :], tk))
            ds = (p * (dp - _lanes(dl_ref[sl, :], tk))).astype(BF16)
            dv = dv + lax.dot_general(p.astype(BF16), doj, tn, preferred_element_type=F32)
            dk = dk + lax.dot_general(ds, qj, tn, preferred_element_type=F32)
            rows = pl.ds(pl.multiple_of(qb * tqb + j * tqs, tqs), tqs)
            dq_ref[rows, :] += jnp.dot(ds, kv, preferred_element_type=F32)
        dk_s[...], dv_s[...] = dk, dv

        @pl.when(qb == nqb - 1)
        def _():
            dk_ref[...] = dk
            dv_ref[...] = dv

    return pl.pallas_call(
        body, name=name, grid=(H, nk, nqb),
        in_specs=[pl.BlockSpec((None, tqb, MLA_W), lambda h, i, j: (h, j, 0)),
                  pl.BlockSpec((None, tk, MLA_W), lambda h, i, j: (h, i, 0)),
                  pl.BlockSpec((None, tk, MLA_V), lambda h, i, j: (h, i, 0)),
                  pl.BlockSpec((tqb, MLA_V), lambda h, i, j: (j, h)),
                  pl.BlockSpec((tqb, MLA_V), lambda h, i, j: (j, h)),
                  pl.BlockSpec((tqb, MLA_V), lambda h, i, j: (j, h))],
        out_specs=[pl.BlockSpec((None, T, MLA_W), lambda h, i, j: (h, 0, 0)),
                   pl.BlockSpec((None, tk, MLA_W), lambda h, i, j: (h, i, 0)),
                   pl.BlockSpec((None, tk, MLA_V), lambda h, i, j: (h, i, 0))],
        out_shape=[jax.ShapeDtypeStruct((H, T, MLA_W), F32), jax.ShapeDtypeStruct((H, T, MLA_W), F32),
                   jax.ShapeDtypeStruct((H, T, MLA_V), F32)],
        scratch_shapes=[pltpu.VMEM((tk, MLA_W), F32), pltpu.VMEM((tk, MLA_V), F32)], compiler_params=_cparams(),
    )(q, k, v, do, lse, delta)


def _mla_fwd(l, h, w, small, g, b, cs):
    T = h.shape[0]
    tm = _tile(T, 512, 8)
    H = MLA_HEADS
    gq, gkv = small["q_norm_g"].reshape(1, -1), small["kv_norm_g"].reshape(1, -1)
    n_ext = MLA_Q_LORA + MLA_KV_LORA + 2 * MLA_ROPE
    row = lambda i, j, k: (i, 0)
    fix = lambda i, j, k: (0, 0)

    def epi_lat(accs, gq_t, gkv_t):
        a = accs[0]
        ql, kvl = a[:, :MLA_Q_LORA], a[:, MLA_Q_LORA:MLA_Q_LORA + MLA_KV_LORA]
        return [ql, kvl, a[:, MLA_Q_LORA + MLA_KV_LORA:], _rms(ql, gq_t), _rms(kvl, gkv_t)]

    ql, kvl, kr, xq, xkv = _mm(
        f"mla{l}_in", (T // tm, 1, 1), [(h, (tm, D_MODEL), row)], [(w["w_in_ext"], (D_MODEL, n_ext), fix)],
        [(0, 0, 0, "nn")], 1, (tm, n_ext), [(gq, gq.shape, fix), (gkv, gkv.shape, fix)],
        [((T, MLA_Q_LORA), F32, (tm, MLA_Q_LORA), row), ((T, MLA_KV_LORA), F32, (tm, MLA_KV_LORA), row),
         ((T, 2 * MLA_ROPE), F32, (tm, 2 * MLA_ROPE), row), ((T, MLA_Q_LORA), BF16, (tm, MLA_Q_LORA), row),
         ((T, MLA_KV_LORA), BF16, (tm, MLA_KV_LORA), row)], epi_lat)

    def epi_q(accs, cs_t):
        a = accs[0]
        return [jnp.concatenate([a[:, :MLA_NOPE], _rope_sum(a[:, MLA_NOPE:] * cs_t)], axis=1) * MLA_SCALE]

    head_out = lambda i, j, k: (j, i, 0)
    q = _mm(f"mla{l}_q", (T // tm, H, 1), [(xq, (tm, MLA_Q_LORA), row)],
            [(w["wq_ext"], (MLA_Q_LORA, MLA_W), lambda i, j, k: (0, j))], [(0, 0, 0, "nn")], 1, (tm, MLA_W),
            [(cs, (tm, 2 * MLA_ROPE), row)], [((H, T, MLA_W), BF16, (None, tm, MLA_W), head_out)], epi_q)[0]

    def epi_kv(accs, kr_t, cs_t):
        a = accs[0]
        v_t = a[:, MLA_NOPE:]
        return [jnp.concatenate([a[:, :MLA_NOPE], _low_half(_rope_sum(kr_t * cs_t))], axis=1),
                jnp.concatenate([v_t, jnp.ones_like(v_t)], axis=1)]

    k, v = _mm(f"mla{l}_kv", (T // tm, H, 1), [(xkv, (tm, MLA_KV_LORA), row)],
               [(w["w_kv_b"], (MLA_KV_LORA, MLA_W), lambda i, j, k: (0, j))], [(0, 0, 0, "nn")], 1, (tm, MLA_W),
               [(kr, (tm, 2 * MLA_ROPE), row), (cs, (tm, 2 * MLA_ROPE), row)],
               [((H, T, MLA_W), BF16, (None, tm, MLA_W), head_out), ((H, T, 2 * MLA_V), BF16, (None, tm, 2 * MLA_V), head_out)],
               epi_kv)
    o, lse = _flash_fwd(f"mla{l}_attn", q, k, v)
    z, h_out = _out_proj_ln(f"mla{l}_out", o, w["w_out"], h, g, b)
    return h_out, (h, ql, kvl, xq, xkv, q, k, v, o, lse, z, gq, gkv, cs)


def _mla_bwd(l, dh_out, saved, w, g):
    h, ql, kvl, xq, xkv, q, k, v, o, lse, z, gq, gkv, cs = saved
    T = h.shape[0]
    tm = _tile(T, 512, 8)
    H = MLA_HEADS
    dz, dg, db = _ln_bwd(f"mla{l}_lnb", dh_out, z, g)
    def epi_do(accs, o_t):
        d = accs[0]
        deltas = [jnp.broadcast_to(jnp.sum(d[:, hd * MLA_V:(hd + 1) * MLA_V] * o_t[:, hd * MLA_V:(hd + 1) * MLA_V],
                                           axis=-1, keepdims=True), (d.shape[0], MLA_V)) for hd in range(H)]
        return [d, jnp.concatenate(deltas, axis=1)]

    do, delta = _mm_simple(f"mla{l}_dO", dz, w["w_out"], "nt", None, extras=(o,), epilogue=epi_do, n_out=2,
                           out_dtypes=[BF16, F32])
    dw_out = _mm_simple(f"mla{l}_dWout", o, dz, "tn", BF16, tm=1024, tn=1024, tk=512)
    dq, dk, dv = _flash_bwd(f"mla{l}_attnb", q, k, v, do, lse, delta)
    head_in = lambda i, hh: (hh, i, 0)
    row2 = lambda i, hh: (i, 0)

    def fn_q(dq_t, cs_t):
        d = dq_t[:, MLA_NOPE:]
        return [jnp.concatenate([dq_t[:, :MLA_NOPE], _rope_sum(d) * cs_t], axis=1) * MLA_SCALE]

    dq_eff = _grid_call(f"mla{l}_dqeff", (T // tm, H), [(dq, (None, tm, MLA_W), head_in), (cs, (tm, 2 * MLA_ROPE), row2)],
                        [((T, H * MLA_W), BF16, (tm, MLA_W), lambda i, hh: (i, hh))], fn_q)[0]

    def fn_kv(dk_t, dv_t, cs_t):
        return [jnp.concatenate([dk_t[:, :MLA_NOPE], dv_t], axis=1), _rope_sum(_low_half(dk_t[:, MLA_NOPE:])) * cs_t]

    dkv_eff, dkr = _grid_call(
        f"mla{l}_dkveff", (T // tm, H),
        [(dk, (None, tm, MLA_W), head_in), (dv, (None, tm, MLA_V), head_in), (cs, (tm, 2 * MLA_ROPE), row2)],
        [((T, H * MLA_W), BF16, (tm, MLA_W), lambda i, hh: (i, hh)), ((T, 2 * MLA_ROPE), F32, (tm, 2 * MLA_ROPE), row2)],
        fn_kv, acc_outs=(1,))
    dxq = _mm_simple(f"mla{l}_dxq", dq_eff, w["wq_ext"], "nt", F32, tn=MLA_Q_LORA)
    dwq_ext = _mm_simple(f"mla{l}_dWq", xq, dq_eff, "tn", BF16, tm=MLA_Q_LORA, tn=1024, tk=512)
    dxkv = _mm_simple(f"mla{l}_dxkv", dkv_eff, w["w_kv_b"], "nt", F32, tn=MLA_KV_LORA)
    dwkv = _mm_simple(f"mla{l}_dWkv", xkv, dkv_eff, "tn", BF16, tm=MLA_KV_LORA, tn=1024, tk=512)

    def fn_lat(ql_t, dxq_t, gq_t, kvl_t, dxkv_t, gkv_t, dkr_t):
        dql, dgq = _rms_bwd(ql_t, gq_t, dxq_t)
        dkvl, dgkv = _rms_bwd(kvl_t, gkv_t, dxkv_t)
        return [jnp.concatenate([dql, dkvl, dkr_t], axis=1), _bcast8(dgq), _bcast8(dgkv)]

    n_ext = MLA_Q_LORA + MLA_KV_LORA + 2 * MLA_ROPE
    dlat, pgq, pgkv = _rows(f"mla{l}_dlat", T, tm,
                            [_rt(ql, tm), _rt(dxq, tm), _full(gq), _rt(kvl, tm), _rt(dxkv, tm), _full(gkv), _rt(dkr, tm)],
                            [_rt_out(T, n_ext, BF16, tm), _ps_out(T, MLA_Q_LORA, tm), _ps_out(T, MLA_KV_LORA, tm)], fn_lat)
    dh = _mm_simple(f"mla{l}_dX", dlat, w["w_in_ext"], "nt", F32, tk=n_ext, extras=(dz,),
                    epilogue=lambda accs, dz_t: [accs[0] + ALPHA * dz_t])
    dw_in_ext = _mm_simple(f"mla{l}_dWin", h, dlat, "tn", BF16, tm=1024, tn=n_ext, tk=512)
    dw_in, dwq = _mla_ext_grads(dw_in_ext.astype(F32), dwq_ext.astype(F32))
    grads = dict(w_in=dw_in, w_q_b=dwq, w_kv_b=dwkv, w_out=dw_out, q_norm_g=_psum_rows(pgq), kv_norm_g=_psum_rows(pgkv),
                 ln_g=dg, ln_b=db)
    return dh, grads


def _loss_head(y, target):
    T = y.shape[0]
    tm = _tile(T, 512, 8)

    def fn(y_t, t_t):
        d = y_t - t_t
        part = 0.5 * jnp.sum(jnp.mean(d * d, axis=-1, keepdims=True), axis=0, keepdims=True)
        return [d * (1.0 / D_MODEL), jnp.broadcast_to(part, (8, 128))]

    dy, part = _rows("loss_head", T, tm, [_rt(y, tm), _rt(target, tm)],
                     [_rt_out(T, D_MODEL, F32, tm), ((T // tm * 8, 128), F32, (8, 128), lambda i: (i, 0))], fn)
    return jnp.sum(part.reshape(-1, 8, 128)[:, 0, 0]), dy


_S5_VECTORS = ("d", "b_glu")


def _local_step(x, positions, target, W):
    row = lambda a, i: a[i].reshape(1, -1)
    w_in_ext, wq_ext = _mla_ext_weights(W["mla_w_in"][0], W["mla_w_q_b"][0])
    cs = _rope_table(positions)
    s5_mats, s5_mats_vjp, s5_apow, s5_apow_c = _s5_prepare(W, x.shape[0])
    h, saves = x, []
    for l in range(DEPTH):
        kind, slot = LAYER_MIXER[l], l // 3
        g, b = row(W["ln_mix_g"], l), row(W["ln_mix_b"], l)
        if kind == 0:
            w = {k: W["s5_" + k][slot] for k in ("w_in", "w_glu", "w_out")}
            small = {k: W["s5_" + k][slot] for k in _S5_VECTORS}
            h, sv = _s5_fwd(l, h, w, small, g, b, {k: v[slot] for k, v in s5_mats.items()}, s5_apow[slot], s5_apow_c[slot])
        elif kind == 1:
            w = dict(w_in=W["hg_w_in"][slot], w_out=W["hg_w_out"][slot])
            small = dict(lower_bound=W["hg_lower_bound"], norm_g=W["hg_norm_g"][slot])
            h, sv = _hg_fwd(l, h, w, small, g, b)
        else:
            w = dict(w_in_ext=w_in_ext, wq_ext=wq_ext, w_kv_b=W["mla_w_kv_b"][slot], w_out=W["mla_w_out"][slot])
            small = dict(q_norm_g=W["mla_q_norm_g"][slot], kv_norm_g=W["mla_kv_norm_g"][slot])
            h, sv = _mla_fwd(l, h, w, small, g, b, cs)
        h, fsv = _ffn_fwd(l, h, W["ffn_w_in"][l], W["ffn_w_out"][l], row(W["ln_ffn_g"], l), row(W["ln_ffn_b"], l))
        saves.append((w, small, sv, fsv))
    loss, dh = _loss_head(h, target)
    per_layer = [None] * DEPTH
    s5_dmats = {}
    for l in reversed(range(DEPTH)):
        kind = LAYER_MIXER[l]
        w, small, sv, fsv = saves[l]
        dh, gf = _ffn_bwd(l, dh, fsv, W["ffn_w_in"][l], W["ffn_w_out"][l], row(W["ln_ffn_g"], l))
        g = row(W["ln_mix_g"], l)
        if kind == 0:
            dh, gm, s5_dmats[l // 3] = _s5_bwd(l, dh, sv, w, small, g)
        elif kind == 1:
            dh, gm = _hg_bwd(l, dh, sv, w, small, g)
        else:
            dh, gm = _mla_bwd(l, dh, sv, w, g)
        per_layer[l] = (gm, gf)
    grads = {}
    stack = lambda xs: jnp.stack([a.astype(F32) if a.dtype != BF16 else a for a in xs])
    grads["ln_mix_g"] = stack([per_layer[l][0]["ln_g"] for l in range(DEPTH)])
    grads["ln_mix_b"] = stack([per_layer[l][0]["ln_b"] for l in range(DEPTH)])
    grads["ln_ffn_g"] = stack([per_layer[l][1]["ln_g"] for l in range(DEPTH)])
    grads["ln_ffn_b"] = stack([per_layer[l][1]["ln_b"] for l in range(DEPTH)])
    grads["ffn_w_in"] = stack([per_layer[l][1]["w_in"] for l in range(DEPTH)])
    grads["ffn_w_out"] = stack([per_layer[l][1]["w_out"] for l in range(DEPTH)])
    s5_layers = [l for l in range(DEPTH) if LAYER_MIXER[l] == 0]
    for k in ("w_in", "w_glu", "w_out") + _S5_VECTORS:
        grads["s5_" + k] = stack([per_layer[l][0][k] for l in s5_layers])
    d_ssm = s5_mats_vjp({k: jnp.stack([s5_dmats[s][k] for s in range(len(s5_layers))]) for k in s5_mats})
    for k, v in zip(_S5_SSM, d_ssm):
        grads["s5_" + k] = v
    hg = per_layer[1][0]
    grads["hg_w_in"], grads["hg_w_out"] = hg["w_in"][None], hg["w_out"][None]
    grads["hg_lower_bound"], grads["hg_norm_g"] = hg["lower_bound"], hg["norm_g"][None]
    for k, v in per_layer[2][0].items():
        if not k.startswith("ln_"):
            grads["mla_" + k] = v[None]
    return loss, dh, grads


def _here():
    return lax.axis_index("x"), lax.axis_index("y"), lax.axis_index("c")


def _any_spec():
    return pl.BlockSpec(memory_space=pl.ANY)


def _chip_exchange(name, xs, scatter):
    n = len(xs)

    def body(*refs):
        ins, outs = refs[:n], refs[n:2 * n]
        send_sems, recv_sems, loc_sems = refs[2 * n:]
        x, y, c = _here()
        me = 2 * x + y
        peers = [(1 - x, y), (x, 1 - y), (1 - x, 1 - y)]
        copies = []
        for t in range(n):
            src_of = (lambda p, t=t: ins[t].at[p]) if scatter else (lambda p, t=t: ins[t])
            loc = pltpu.make_async_copy(src_of(me), outs[t].at[me], loc_sems.at[t])
            loc.start()
            copies.append(loc)
            for j, (px, py) in enumerate(peers):
                cp = pltpu.make_async_remote_copy(
                    src_ref=src_of(2 * px + py), dst_ref=outs[t].at[me], send_sem=send_sems.at[t, j],
                    recv_sem=recv_sems.at[t, j], device_id=(px, py, c), device_id_type=MESH)
                cp.start()
                copies.append(cp)
        for cp in copies:
            cp.wait()

    out_shape = [jax.ShapeDtypeStruct(a.shape if scatter else (N_CHIPS,) + a.shape, a.dtype) for a in xs]
    return pl.pallas_call(
        body, name=name, in_specs=[_any_spec()] * n, out_specs=[_any_spec()] * n, out_shape=out_shape,
        scratch_shapes=[pltpu.SemaphoreType.DMA((n, 3)), pltpu.SemaphoreType.DMA((n, 3)), pltpu.SemaphoreType.DMA((n,))],
    )(*xs)


def _all_and_core_exchange(name, a, b):
    def body(a_ref, b_ref, oa_ref, ob_ref, send_sems, recv_sems, loc_sems):
        x, y, c = _here()
        me = 4 * x + 2 * y + c
        copies = [pltpu.make_async_copy(a_ref, oa_ref.at[me], loc_sems.at[0]),
                  pltpu.make_async_copy(b_ref, ob_ref.at[c], loc_sems.at[1])]
        for mask in range(1, N_DEV):
            fx, fy, fc = (mask >> 2) & 1, (mask >> 1) & 1, mask & 1
            peer = (1 - x if fx else x, 1 - y if fy else y, 1 - c if fc else c)
            copies.append(pltpu.make_async_remote_copy(
                src_ref=a_ref, dst_ref=oa_ref.at[me], send_sem=send_sems.at[mask - 1], recv_sem=recv_sems.at[mask - 1],
                device_id=peer, device_id_type=MESH))
        copies.append(pltpu.make_async_remote_copy(
            src_ref=b_ref, dst_ref=ob_ref.at[c], send_sem=send_sems.at[N_DEV - 1], recv_sem=recv_sems.at[N_DEV - 1],
            device_id=(x, y, 1 - c), device_id_type=MESH))
        for cp in copies:
            cp.start()
        for cp in copies:
            cp.wait()

    return pl.pallas_call(
        body, name=name, in_specs=[_any_spec(), _any_spec()], out_specs=[_any_spec(), _any_spec()],
        out_shape=[jax.ShapeDtypeStruct((N_DEV,) + a.shape, a.dtype), jax.ShapeDtypeStruct((2,) + b.shape, b.dtype)],
        scratch_shapes=[pltpu.SemaphoreType.DMA((N_DEV,)), pltpu.SemaphoreType.DMA((N_DEV,)), pltpu.SemaphoreType.DMA((2,))],
    )(a, b)


def _sum_leading(name, a, out_dtype=F32):
    n, R, C = a.shape
    tr = _tile(R, 512, 16)

    def fn(a_t):
        s = a_t[0].astype(F32)
        for i in range(1, n):
            s = s + a_t[i].astype(F32)
        return [s]

    return _grid_call(name, (R // tr,), [(a, (n, tr, C), lambda i: (0, i, 0))],
                      [((R, C), out_dtype, (tr, C), lambda i: (i, 0))], fn)[0]


def _adamw(name, g_parts, w, m, v):
    R, C = w.shape
    tr = _tile(R, 256, 8)
    ng = len(g_parts)
    c1 = 1.0 / (1.0 - ADAM_B1 ** ADAM_STEP)
    c2 = 1.0 / (1.0 - ADAM_B2 ** ADAM_STEP)

    def fn(*tiles):
        g = tiles[0].astype(F32)
        for t in tiles[1:ng]:
            g = g + t.astype(F32)
        w_t, m_t, v_t = tiles[ng:]
        m_n = ADAM_B1 * m_t + (1.0 - ADAM_B1) * g
        v_n = ADAM_B2 * v_t + (1.0 - ADAM_B2) * (g * g)
        delta = -ADAM_LR * ((m_n * c1) / (jnp.sqrt(v_n * c2) + ADAM_EPS) + ADAM_WD * w_t)
        return [g, delta, m_n, v_n]

    spec = lambda a: (a, (tr, C), lambda i: (i, 0))
    return _grid_call(name, (R // tr,), [spec(a) for a in list(g_parts) + [w, m, v]],
                      [((R, C), F32, (tr, C), lambda i: (i, 0))] * 4, fn)


_WEIGHTS = ("ln_mix_g", "ln_mix_b", "ln_ffn_g", "ln_ffn_b", "ffn_w_in", "ffn_w_out", "s5_w_in", "s5_lam_re", "s5_lam_im",
            "s5_log_step", "s5_b_re", "s5_b_im", "s5_c_re", "s5_c_im", "s5_d", "s5_w_glu", "s5_b_glu", "s5_w_out", "hg_w_in",
            "hg_lower_bound", "hg_norm_g", "hg_w_out", "mla_w_in", "mla_q_norm_g", "mla_w_q_b", "mla_kv_norm_g", "mla_w_kv_b",
            "mla_w_out")
_BIG = {"ffn_w_in": 2, "ffn_w_out": 1, "s5_w_in": 1, "s5_w_glu": 1, "s5_w_out": 1, "hg_w_in": 2, "hg_w_out": 1,
        "mla_w_in": 1, "mla_w_q_b": 2, "mla_w_kv_b": 2, "mla_w_out": 1}
_SMALL_SHARDED = {"s5_d": 1, "s5_b_glu": 1, "mla_q_norm_g": 1, "mla_kv_norm_g": 1}
_REPLICATED = tuple(n for n in _WEIGHTS if n not in _BIG and n not in _SMALL_SHARDED)
LANES = 1024


def _pack(arrs, dtype, row_mult, lead=0):
    rows, segs, r = [], [], 0
    for a in arrs:
        lead_shape = a.shape[:lead]
        flat = a.astype(dtype).reshape(lead_shape + (-1,))
        n = -(-flat.shape[-1] // LANES)
        flat = jnp.pad(flat, [(0, 0)] * lead + [(0, n * LANES - flat.shape[-1])])
        rows.append(flat.reshape(lead_shape + (n, LANES)))
        segs.append((r, n))
        r += n
    pad = -r % row_mult
    if pad:
        rows.append(jnp.zeros(rows[0].shape[:lead] + (pad, LANES), dtype))
    return jnp.concatenate(rows, axis=lead), segs


def _unpack(packed, segs, shapes):
    out = []
    for (r0, n), shp in zip(segs, shapes):
        size = int(np.prod(shp))
        out.append(packed[..., r0:r0 + n, :].reshape(packed.shape[:-2] + (n * LANES,))[..., :size].reshape(packed.shape[:-2] + tuple(shp)))
    return out


def _unshard(stacked, axis):
    moved = jnp.moveaxis(stacked, 0, axis)
    shp = list(moved.shape)
    return moved.reshape(shp[:axis] + [shp[axis] * shp[axis + 1]] + shp[axis + 2:])


def _shard_split(full, axis):
    shp = list(full.shape)
    a = full.reshape(shp[:axis] + [N_CHIPS, shp[axis] // N_CHIPS] + shp[axis + 1:])
    return jnp.moveaxis(a, axis, 0)


def _train_step(x, positions, target, w, m, v):
    big, small_sh = list(_BIG), list(_SMALL_SHARDED)
    chip = 2 * lax.axis_index("x") + lax.axis_index("y")

    big_pack, big_segs = _pack([w[n] for n in big], BF16, 16)
    sm_pack, sm_segs = _pack([w[n] for n in small_sh], F32, 8)
    big_all, sm_all = _chip_exchange("gather_weights", [big_pack, sm_pack], scatter=False)
    W = {n: w[n] for n in _REPLICATED}
    for n, s in zip(big, _unpack(big_all, big_segs, [w[n].shape for n in big])):
        W[n] = _unshard(s, _BIG[n])
    for n, s in zip(small_sh, _unpack(sm_all, sm_segs, [w[n].shape for n in small_sh])):
        W[n] = _unshard(s, _SMALL_SHARDED[n])

    loss_local, grad_x, G = _local_step(x, positions, target, W)
    loss = lax.psum(loss_local, ("x", "y", "c"))
    out = {}

    g_pack, _ = _pack([_shard_split(G[n].astype(BF16), _BIG[n]) for n in big], BF16, 16, lead=1)
    recv = _chip_exchange("scatter_grads", [g_pack], scatter=True)[0]
    small = list(_REPLICATED) + small_sh
    s_pack, s_segs = _pack([G[n] for n in small], F32, 16)
    s_all, pair = _all_and_core_exchange("exchange_sums", s_pack, _sum_leading("sum_chips", recv, BF16))
    for n, (r0, nr) in zip(big, big_segs):
        as_rows = lambda a: a.reshape(nr, LANES)
        res = _adamw("adamw_" + n, [pair[0, r0:r0 + nr], pair[1, r0:r0 + nr]], as_rows(w[n]), as_rows(m[n]), as_rows(v[n]))
        out[n] = tuple(r.reshape(w[n].shape) for r in res)

    total = _sum_leading("sum_small", s_all)
    g_small = dict(zip(small, _unpack(total, s_segs, [G[n].shape for n in small])))
    for n in small_sh:
        width = w[n].shape[1]
        g_small[n] = lax.dynamic_slice_in_dim(g_small[n], chip * width, width, axis=1)
    packs = [_pack([d[n] for n in small], F32, 8)[0] for d in (g_small, w, m, v)]
    _, a_segs = _pack([w[n] for n in small], F32, 8)
    res = _adamw("adamw_small", [packs[0]], packs[1], packs[2], packs[3])
    unpacked = [_unpack(r, a_segs, [w[n].shape for n in small]) for r in res]
    for i, n in enumerate(small):
        out[n] = tuple(u[i] for u in unpacked)
    return loss, grad_x, out


def kernel(x, positions, ln_mix_g, ln_mix_b, ln_ffn_g, ln_ffn_b, ffn_w_in, ffn_w_out, s5_w_in, s5_lam_re, s5_lam_im,
           s5_log_step, s5_b_re, s5_b_im, s5_c_re, s5_c_im, s5_d, s5_w_glu, s5_b_glu, s5_w_out, hg_w_in,
           hg_lower_bound, hg_norm_g, hg_w_out, mla_w_in, mla_q_norm_g, mla_w_q_b, mla_kv_norm_g, mla_w_kv_b,
           mla_w_out, loss_target, m_ln_mix_g, m_ln_mix_b, m_ln_ffn_g, m_ln_ffn_b, m_ffn_w_in, m_ffn_w_out, m_s5_w_in,
           m_s5_lam_re, m_s5_lam_im, m_s5_log_step, m_s5_b_re, m_s5_b_im, m_s5_c_re, m_s5_c_im, m_s5_d, m_s5_w_glu,
           m_s5_b_glu, m_s5_w_out, m_hg_w_in, m_hg_lower_bound, m_hg_norm_g, m_hg_w_out, m_mla_w_in, m_mla_q_norm_g,
           m_mla_w_q_b, m_mla_kv_norm_g, m_mla_w_kv_b, m_mla_w_out, v_ln_mix_g, v_ln_mix_b, v_ln_ffn_g, v_ln_ffn_b,
           v_ffn_w_in, v_ffn_w_out, v_s5_w_in, v_s5_lam_re, v_s5_lam_im, v_s5_log_step, v_s5_b_re, v_s5_b_im,
           v_s5_c_re, v_s5_c_im, v_s5_d, v_s5_w_glu, v_s5_b_glu, v_s5_w_out, v_hg_w_in, v_hg_lower_bound, v_hg_norm_g,
           v_hg_w_out, v_mla_w_in, v_mla_q_norm_g, v_mla_w_q_b, v_mla_kv_norm_g, v_mla_w_kv_b, v_mla_w_out):
    args = (ln_mix_g, ln_mix_b, ln_ffn_g, ln_ffn_b, ffn_w_in, ffn_w_out, s5_w_in, s5_lam_re, s5_lam_im,
            s5_log_step, s5_b_re, s5_b_im, s5_c_re, s5_c_im, s5_d, s5_w_glu, s5_b_glu, s5_w_out, hg_w_in,
            hg_lower_bound, hg_norm_g, hg_w_out, mla_w_in, mla_q_norm_g, mla_w_q_b, mla_kv_norm_g, mla_w_kv_b,
            mla_w_out, m_ln_mix_g, m_ln_mix_b, m_ln_ffn_g, m_ln_ffn_b, m_ffn_w_in, m_ffn_w_out,
            m_s5_w_in, m_s5_lam_re, m_s5_lam_im, m_s5_log_step, m_s5_b_re, m_s5_b_im, m_s5_c_re, m_s5_c_im, m_s5_d,
            m_s5_w_glu, m_s5_b_glu, m_s5_w_out, m_hg_w_in, m_hg_lower_bound, m_hg_norm_g, m_hg_w_out, m_mla_w_in,
            m_mla_q_norm_g, m_mla_w_q_b, m_mla_kv_norm_g, m_mla_w_kv_b, m_mla_w_out, v_ln_mix_g, v_ln_mix_b,
            v_ln_ffn_g, v_ln_ffn_b, v_ffn_w_in, v_ffn_w_out, v_s5_w_in, v_s5_lam_re, v_s5_lam_im, v_s5_log_step,
            v_s5_b_re, v_s5_b_im, v_s5_c_re, v_s5_c_im, v_s5_d, v_s5_w_glu, v_s5_b_glu, v_s5_w_out, v_hg_w_in,
            v_hg_lower_bound, v_hg_norm_g, v_hg_w_out, v_mla_w_in, v_mla_q_norm_g, v_mla_w_q_b, v_mla_kv_norm_g,
            v_mla_w_kv_b, v_mla_w_out)
    nw = len(_WEIGHTS)
    w = dict(zip(_WEIGHTS, args[:nw]))
    m = dict(zip(_WEIGHTS, args[nw:2 * nw]))
    v = dict(zip(_WEIGHTS, args[2 * nw:]))
    loss, grad_x, out = _train_step(x[0], positions[0], loss_target[0], w, m, v)
    res = [loss, grad_x[None]]
    for i in range(4):
        res += [out[n][i] for n in _WEIGHTS]
    return tuple(res)
```

```python
import functools
import math

import numpy as np
import jax
import jax.numpy as jnp
from jax import lax
from jax.experimental import pallas as pl
from jax.experimental.pallas import tpu as pltpu

F32 = jnp.float32
BF16 = jnp.bfloat16

D_MODEL = 1024
DEPTH = 4
LAYER_MIXER = (0, 1, 2, 0)
S5_GROUP = 16
S5_GROUPS = 64
S5_STATE = 64
S5_CHUNK = 16
HG_HEADS = 8
HG_DIM = 128
HG_CHUNK = 128
MLA_HEADS = 8
MLA_NOPE = 128
MLA_ROPE = 64
MLA_V = 128
MLA_Q_LORA = 384
MLA_KV_LORA = 256
ROPE_THETA = 10000.0
FFN_HIDDEN = 2816
ALPHA = (2 * DEPTH) ** 0.25
LN_EPS = 1e-5
RMS_EPS = 1e-6
ADAM_LR, ADAM_B1, ADAM_B2, ADAM_EPS, ADAM_WD, ADAM_STEP = 0.001, 0.9, 0.999, 1e-08, 0.01, 10
VMEM_LIMIT_BYTES = 56 * 1024 * 1024
MESH = pl.DeviceIdType.MESH
N_CHIPS = 4
N_DEV = 8


def _cparams():
    return pltpu.CompilerParams(vmem_limit_bytes=VMEM_LIMIT_BYTES)


def _tile(n, want, mult):
    t = min(want, n)
    t -= t % mult
    while t >= mult:
        if n % t == 0:
            return t
        t -= mult
    return n


def _sigmoid(x):
    return 1.0 / (1.0 + jnp.exp(-x))


def _silu(x):
    return x * _sigmoid(x)


def _dsilu(x):
    s = _sigmoid(x)
    return s * (1.0 + x * (1.0 - s))


_GELU_C = math.sqrt(2.0 / math.pi)


def _gelu(x):
    return 0.5 * x * (1.0 + jnp.tanh(_GELU_C * (x + 0.044715 * x * x * x)))


def _dgelu(x):
    t = jnp.tanh(_GELU_C * (x + 0.044715 * x * x * x))
    return 0.5 * (1.0 + t) + 0.5 * x * (1.0 - t * t) * _GELU_C * (1.0 + 3 * 0.044715 * x * x)


def _layer_norm(z, g, b):
    mu = jnp.mean(z, axis=-1, keepdims=True)
    zc = z - mu
    var = jnp.mean(zc * zc, axis=-1, keepdims=True)
    return zc * lax.rsqrt(var + LN_EPS) * g + b


def _bcast8(row):
    return jnp.broadcast_to(row, (8, row.shape[-1]))


def _stack_rows8(rows):
    n = rows[0].shape[-1]
    idx = lax.broadcasted_iota(jnp.int32, (8, n), 0)
    out = jnp.zeros((8, n), F32)
    for i, r in enumerate(rows):
        out = jnp.where(idx == i, _bcast8(r), out)
    return out


def _psum_rows(a):
    return a.reshape(-1, 8, a.shape[-1])[:, 0, :].sum(axis=0)


_DN = {"nn": ((1,), (0,)), "nt": ((1,), (1,)), "tn": ((0,), (0,))}


def _mm(name, grid, a_defs, b_defs, pairs, n_acc, acc_shape, extra_defs, out_defs, epilogue):
    nk = grid[2]
    na, nb, ne, no = len(a_defs), len(b_defs), len(extra_defs), len(out_defs)

    def body(*refs):
        a_refs = refs[:na]
        b_refs = refs[na:na + nb]
        e_refs = refs[na + nb:na + nb + ne]
        o_refs = refs[na + nb + ne:na + nb + ne + no]
        acc = refs[-1]
        k = pl.program_id(2)

        @pl.when(k == 0)
        def _():
            acc[...] = jnp.zeros(acc.shape, F32)

        for (ai, bi, ci, mode) in pairs:
            a = a_refs[ai][...].astype(BF16)
            b = b_refs[bi][...].astype(BF16)
            acc[ci] += lax.dot_general(a, b, (_DN[mode], ((), ())), preferred_element_type=F32)

        @pl.when(k == nk - 1)
        def _():
            outs = epilogue([acc[c] for c in range(n_acc)], *[e[...] for e in e_refs])
            for o_ref, o in zip(o_refs, outs):
                o_ref[...] = o.astype(o_ref.dtype)

    in_specs = [pl.BlockSpec(d[1], d[2]) for d in list(a_defs) + list(b_defs) + list(extra_defs)]
    out_specs = [pl.BlockSpec(d[2], d[3]) for d in out_defs]
    out_shape = [jax.ShapeDtypeStruct(d[0], d[1]) for d in out_defs]
    res = pl.pallas_call(
        body, name=name, grid=grid, in_specs=in_specs, out_specs=out_specs, out_shape=out_shape,
        scratch_shapes=[pltpu.VMEM((n_acc,) + tuple(acc_shape), F32)], compiler_params=_cparams(),
    )(*[d[0] for d in list(a_defs) + list(b_defs) + list(extra_defs)])
    return res


def _mm_simple(name, a, b, mode, out_dtype, tm=512, tn=1024, tk=1024, extras=(), epilogue=None, n_out=1,
               out_dtypes=None, psum_outs=0, j_outer=False):
    if mode == "nn":
        (M, K), (K2, N) = a.shape, b.shape
    elif mode == "nt":
        (M, K), (N, K2) = a.shape, b.shape
    else:
        (K, M), (K2, N) = a.shape, b.shape
    assert K == K2, (name, a.shape, b.shape, mode)
    tm, tn, tk = _tile(M, tm, 8), _tile(N, tn, 128), _tile(K, tk, 128)
    grid = (M // tm, N // tn, K // tk)
    if mode == "nn":
        a_def = (a, (tm, tk), lambda i, j, k: (i, k))
        b_def = (b, (tk, tn), lambda i, j, k: (k, j))
    elif mode == "nt":
        a_def = (a, (tm, tk), lambda i, j, k: (i, k))
        b_def = (b, (tn, tk), lambda i, j, k: (j, k))
    else:
        a_def = (a, (tk, tm), lambda i, j, k: (k, i))
        b_def = (b, (tk, tn), lambda i, j, k: (k, j))
    extra_defs = []
    for e in extras:
        if e.shape[0] == 1:
            extra_defs.append((e, (1, tn), lambda i, j, k: (0, j)))
        else:
            extra_defs.append((e, (tm, tn), lambda i, j, k: (i, j)))
    out_dtypes = out_dtypes or [out_dtype] * n_out
    out_defs = [((M, N), dt, (tm, tn), lambda i, j, k: (i, j)) for dt in out_dtypes]
    out_defs += [((M // tm * 8, N), F32, (8, tn), lambda i, j, k: (i, j)) for _ in range(psum_outs)]
    if epilogue is None:
        epilogue = lambda accs: [accs[0]]
    a_defs, b_defs = [a_def], [b_def]
    if j_outer:
        swap = lambda d: d[:-1] + ((lambda f: lambda j, i, k: f(i, j, k))(d[-1]),)
        grid = (grid[1], grid[0], grid[2])
        a_defs, b_defs = [swap(a_def)], [swap(b_def)]
        extra_defs, out_defs = [swap(d) for d in extra_defs], [swap(d) for d in out_defs]
    res = _mm(name, grid, a_defs, b_defs, [(0, 0, 0, mode)], 1, (tm, tn), extra_defs, out_defs, epilogue)
    return res[0] if len(res) == 1 else res


def _rows(name, T, tm, in_defs, out_defs, fn):
    ni = len(in_defs)

    def body(*refs):
        outs = fn(*[r[...] for r in refs[:ni]])
        for o_ref, o in zip(refs[ni:], outs):
            o_ref[...] = o.astype(o_ref.dtype)

    res = pl.pallas_call(
        body, name=name, grid=(T // tm,),
        in_specs=[pl.BlockSpec(d[1], d[2]) for d in in_defs],
        out_specs=[pl.BlockSpec(d[2], d[3]) for d in out_defs],
        out_shape=[jax.ShapeDtypeStruct(d[0], d[1]) for d in out_defs],
        compiler_params=_cparams(),
    )(*[d[0] for d in in_defs])
    return res


def _grid_call(name, grid, in_defs, out_defs, fn, acc_outs=()):
    ni = len(in_defs)

    def body(*refs):
        outs = fn(*[r[...] for r in refs[:ni]])
        first = pl.program_id(len(grid) - 1) == 0
        for idx, (o_ref, o) in enumerate(zip(refs[ni:], outs)):
            if idx in acc_outs:
                @pl.when(first)
                def _(o_ref=o_ref, o=o):
                    o_ref[...] = o.astype(o_ref.dtype)

                @pl.when(jnp.logical_not(first))
                def _(o_ref=o_ref, o=o):
                    o_ref[...] += o.astype(o_ref.dtype)
            else:
                o_ref[...] = o.astype(o_ref.dtype)

    return pl.pallas_call(
        body, name=name, grid=grid,
        in_specs=[pl.BlockSpec(d[1], d[2]) for d in in_defs],
        out_specs=[pl.BlockSpec(d[2], d[3]) for d in out_defs],
        out_shape=[jax.ShapeDtypeStruct(d[0], d[1]) for d in out_defs],
        compiler_params=_cparams(),
    )(*[d[0] for d in in_defs])


def _rt(a, tm):
    return (a, (tm, a.shape[1]), lambda i: (i, 0))


def _full(a):
    return (a, a.shape, lambda i: (0,) * a.ndim)


def _rt_out(T, n, dt, tm):
    return ((T, n), dt, (tm, n), lambda i: (i, 0))


def _ps_out(T, n, tm):
    return ((T // tm * 8, n), F32, (8, n), lambda i: (i, 0))


def _out_proj_ln(name, a, w, h_in, g, b):
    def epi(accs, h_t, g_t, b_t):
        z = ALPHA * h_t + accs[0]
        return [z, _layer_norm(z, g_t, b_t)]
    return _mm_simple(name, a, w, "nn", F32, tm=512, tn=D_MODEL, tk=FFN_HIDDEN, extras=(h_in, g, b), epilogue=epi, n_out=2)


def _ln_bwd(name, dh, z, g):
    T = dh.shape[0]
    tm = _tile(T, 512, 8)

    def fn(dh_t, z_t, g_t):
        mu = jnp.mean(z_t, axis=-1, keepdims=True)
        zc = z_t - mu
        var = jnp.mean(zc * zc, axis=-1, keepdims=True)
        rstd = lax.rsqrt(var + LN_EPS)
        xhat = zc * rstd
        dxh = dh_t * g_t
        m1 = jnp.mean(dxh, axis=-1, keepdims=True)
        m2 = jnp.mean(dxh * xhat, axis=-1, keepdims=True)
        dz = rstd * (dxh - m1 - xhat * m2)
        return [dz, _bcast8(jnp.sum(dh_t * xhat, axis=0, keepdims=True)), _bcast8(jnp.sum(dh_t, axis=0, keepdims=True))]

    dz, pg, pb = _rows(name, T, tm, [_rt(dh, tm), _rt(z, tm), _full(g)],
                       [_rt_out(T, D_MODEL, F32, tm), _ps_out(T, D_MODEL, tm), _ps_out(T, D_MODEL, tm)], fn)
    return dz, _psum_rows(pg), _psum_rows(pb)


def _ffn_fwd(l, h, w_in, w_out, g, b):
    T = h.shape[0]
    tm, tn = _tile(T, 512, 8), 1408
    nj = FFN_HIDDEN // tn
    grid = (nj, T // tm, 1)

    def epi(accs):
        gg, uu = accs
        return [gg, uu, _silu(gg) * uu]

    G, U, A = _mm(
        f"ffn{l}_in", grid, [(h, (tm, D_MODEL), lambda j, i, k: (i, 0))],
        [(w_in, (D_MODEL, tn), lambda j, i, k: (0, j)), (w_in, (D_MODEL, tn), lambda j, i, k: (0, j + nj))],
        [(0, 0, 0, "nn"), (0, 1, 1, "nn")], 2, (tm, tn), [],
        [((T, FFN_HIDDEN), F32, (tm, tn), lambda j, i, k: (i, j)),
         ((T, FFN_HIDDEN), F32, (tm, tn), lambda j, i, k: (i, j)),
         ((T, FFN_HIDDEN), BF16, (tm, tn), lambda j, i, k: (i, j))], epi)
    z, h_out = _out_proj_ln(f"ffn{l}_out", A, w_out, h, g, b)
    return h_out, (h, G, U, A, z)


def _ffn_bwd(l, dh_out, saved, w_in, w_out, g):
    h, G, U, A, z = saved
    T = h.shape[0]
    dz, dg, db = _ln_bwd(f"ffn{l}_lnb", dh_out, z, g)

    def epi(accs, g_t, u_t):
        da = accs[0]
        return [da * u_t * _dsilu(g_t), da * _silu(g_t)]

    dG, dU = _mm_simple(f"ffn{l}_dA", dz, w_out, "nt", BF16, tm=512, tn=1408, tk=1024, extras=(G, U), epilogue=epi, n_out=2,
                        j_outer=True)
    dw_out = _mm_simple(f"ffn{l}_dWout", A, dz, "tn", BF16, tm=1408, tn=1024, tk=512)
    dw_g = _mm_simple(f"ffn{l}_dWg", h, dG, "tn", BF16, tm=1024, tn=1408, tk=512)
    dw_u = _mm_simple(f"ffn{l}_dWu", h, dU, "tn", BF16, tm=1024, tn=1408, tk=512)
    tm, tk = _tile(T, 256, 8), FFN_HIDDEN
    dh = _mm(
        f"ffn{l}_dX", (T // tm, 1, 1),
        [(dG, (tm, tk), lambda i, j, k: (i, 0)), (dU, (tm, tk), lambda i, j, k: (i, 0))],
        [(w_in, (D_MODEL, tk), lambda i, j, k: (0, 0)), (w_in, (D_MODEL, tk), lambda i, j, k: (0, 1))],
        [(0, 0, 0, "nt"), (1, 1, 0, "nt")], 1, (tm, D_MODEL),
        [(dz, (tm, D_MODEL), lambda i, j, k: (i, 0))],
        [((T, D_MODEL), F32, (tm, D_MODEL), lambda i, j, k: (i, 0))],
        lambda accs, dz_t: [accs[0] + ALPHA * dz_t])[0]
    return dh, dict(w_in=jnp.concatenate([dw_g, dw_u], axis=1), w_out=dw_out, ln_g=dg, ln_b=db)


def _s5_matrices(lam_re, lam_im, log_step, b_re, b_im, c_re, c_im):
    L, hp = S5_CHUNK, lax.Precision.HIGHEST
    out = {}
    mt_total = 0.0
    for d in range(2):
        lam = lax.complex(lam_re[d], lam_im[d])
        step = jnp.exp(log_step[d])[:, None]
        lam_dt = lam * step
        lam_bar = jnp.exp(lam_dt)
        b_bar = ((lam_bar - 1.0) / lam)[..., None] * lax.complex(b_re[d], b_im[d])
        c = lax.complex(c_re[d], c_im[d])
        pw = jnp.exp(lam_dt[None] * jnp.arange(L + 1, dtype=F32)[:, None, None])
        kj = jnp.einsum("ghp,jgp,gpk->gjhk", c, pw[:L], b_bar, precision=hp).real
        lag = np.arange(L)[None, :] - np.arange(L)[:, None]
        lag = lag if d == 0 else -lag
        sel = np.stack([(lag == j) for j in range(L)]).astype(np.float32)
        mt = jnp.einsum("jst,gjab->gsbta", sel, kj, precision=hp).reshape(S5_GROUPS, 16 * L, 16 * L)
        mt_total = mt_total + mt
        pw_dist = jnp.flip(pw[:L], 0) if d == 0 else pw[:L]
        pc = pw_dist.transpose(1, 0, 2)[:, :, None, :] * b_bar.transpose(0, 2, 1)[:, None, :, :]
        pm = jnp.concatenate([pc.real, pc.imag], axis=-1).reshape(S5_GROUPS, 16 * L, 2 * S5_STATE)
        pw_read = pw[1:] if d == 0 else jnp.flip(pw[1:], 0)
        qc = c[:, None, :, :] * pw_read.transpose(1, 0, 2)[:, :, None, :]
        qm = jnp.concatenate([qc.real, -qc.imag], axis=-1).reshape(S5_GROUPS, 16 * L, 2 * S5_STATE).transpose(0, 2, 1)
        a = pw[L]
        out[f"p{d}"], out[f"q{d}"] = pm, qm
        out[f"a{d}"] = jnp.concatenate([a.real, a.imag], axis=-1)
    out["mt"] = mt_total
    return out


def _s5_apow(lam_re, lam_im, log_step, n_steps, conj):
    lam_dt = lax.complex(lam_re, lam_im) * jnp.exp(log_step)[..., None]
    k = (S5_CHUNK * 2.0 ** jnp.arange(n_steps, dtype=F32))[None, None, :, None]
    a = jnp.exp(lam_dt[:, :, None, :] * k)
    re, im = a.real, (-a.imag if conj else a.imag)
    return jnp.stack([jnp.concatenate([re, re], -1), jnp.concatenate([-im, im], -1)], axis=3)


def _shift_rows(x, s, down):
    n = x.shape[0]
    if s >= n:
        return jnp.zeros_like(x)
    if s % 8 == 0:
        z = jnp.zeros((s, x.shape[1]), x.dtype)
        return jnp.concatenate([z, x[:n - s]], axis=0) if down else jnp.concatenate([x[s:], z], axis=0)
    row = lax.broadcasted_iota(jnp.int32, x.shape, 0)
    if down:
        return jnp.where(row >= s, pltpu.roll(x, s, 0), 0.0)
    return jnp.where(row < n - s, pltpu.roll(x, n - s, 0), 0.0)


def _cmul(x, a1, a2):
    return x * a1 + pltpu.roll(x, S5_STATE, 1) * a2


def _chunk_scan(s, apow_ref, down):
    n = s.shape[0]
    k, sh = 0, 1
    while sh < n:
        s = s + _cmul(_shift_rows(s, sh, down), apow_ref[k, 0:1, :], apow_ref[k, 1:2, :])
        k, sh = k + 1, sh * 2
    return s


def _s5_scan_fwd(name, ug, mats, apow):
    G, C, W = ug.shape
    n_steps = apow.shape[2]

    def body(u_ref, mt_ref, p0_ref, p1_ref, q0_ref, q1_ref, ap0_ref, ap1_ref, y_ref, h0_ref, h1_ref):
        u = u_ref[...]
        s0 = jnp.dot(u, p0_ref[...], preferred_element_type=F32)
        s1 = jnp.dot(u, p1_ref[...], preferred_element_type=F32)
        h0 = _shift_rows(_chunk_scan(s0, ap0_ref, True), 1, True)
        h1 = _shift_rows(_chunk_scan(s1, ap1_ref, False), 1, False)
        y = jnp.dot(u, mt_ref[...], preferred_element_type=F32)
        y += jnp.dot(h0.astype(BF16), q0_ref[...], preferred_element_type=F32)
        y += jnp.dot(h1.astype(BF16), q1_ref[...], preferred_element_type=F32)
        y_ref[...] = y.astype(y_ref.dtype)
        h0_ref[...] = h0
        h1_ref[...] = h1

    def gspec(shape):
        return pl.BlockSpec((None,) + shape, lambda g: (g,) + (0,) * len(shape))

    ap0, ap1 = apow[0], apow[1]
    return pl.pallas_call(
        body, name=name, grid=(G,),
        in_specs=[gspec((C, W)), gspec((W, W)), gspec((W, 128)), gspec((W, 128)), gspec((128, W)), gspec((128, W)),
                  gspec((n_steps, 2, 128)), gspec((n_steps, 2, 128))],
        out_specs=[gspec((C, W)), gspec((C, 128)), gspec((C, 128))],
        out_shape=[jax.ShapeDtypeStruct((G, C, W), BF16), jax.ShapeDtypeStruct((G, C, 128), F32),
                   jax.ShapeDtypeStruct((G, C, 128), F32)],
        compiler_params=_cparams(),
    )(ug, mats["mt"].astype(BF16), mats["p0"].astype(BF16), mats["p1"].astype(BF16),
      mats["q0"].astype(BF16), mats["q1"].astype(BF16), ap0, ap1)


def _s5_scan_bwd(name, dyg, ug, h0, h1, mats, apow_conj):
    G, C, W = ug.shape
    n_steps = apow_conj.shape[2]

    def body(dy_ref, u_ref, h0_ref, h1_ref, mt_ref, p0_ref, p1_ref, q0_ref, q1_ref, ap0_ref, ap1_ref,
             du_ref, dmt_ref, dp0_ref, dp1_ref, dq0_ref, dq1_ref, da_ref):
        dy, u = dy_ref[...], u_ref[...]
        nt, tn = (_DN["nt"], ((), ())), (_DN["tn"], ((), ()))
        dh0 = lax.dot_general(dy, q0_ref[...], nt, preferred_element_type=F32)
        dh1 = lax.dot_general(dy, q1_ref[...], nt, preferred_element_type=F32)
        ds0 = _chunk_scan(_shift_rows(dh0, 1, False), ap0_ref, False)
        ds1 = _chunk_scan(_shift_rows(dh1, 1, True), ap1_ref, True)
        ds0b, ds1b = ds0.astype(BF16), ds1.astype(BF16)
        du = lax.dot_general(dy, mt_ref[...], nt, preferred_element_type=F32)
        du += lax.dot_general(ds0b, p0_ref[...], nt, preferred_element_type=F32)
        du += lax.dot_general(ds1b, p1_ref[...], nt, preferred_element_type=F32)
        du_ref[...] = du.astype(du_ref.dtype)
        dmt_ref[...] = lax.dot_general(u, dy, tn, preferred_element_type=F32)
        dp0_ref[...] = lax.dot_general(u, ds0b, tn, preferred_element_type=F32)
        dp1_ref[...] = lax.dot_general(u, ds1b, tn, preferred_element_type=F32)
        h0v, h1v = h0_ref[...], h1_ref[...]
        dq0_ref[...] = lax.dot_general(h0v.astype(BF16), dy, tn, preferred_element_type=F32)
        dq1_ref[...] = lax.dot_general(h1v.astype(BF16), dy, tn, preferred_element_type=F32)
        rows = [jnp.sum(ds0 * h0v, axis=0, keepdims=True), jnp.sum(ds0 * pltpu.roll(h0v, S5_STATE, 1), axis=0, keepdims=True),
                jnp.sum(ds1 * h1v, axis=0, keepdims=True), jnp.sum(ds1 * pltpu.roll(h1v, S5_STATE, 1), axis=0, keepdims=True)]
        da_ref[...] = _stack_rows8(rows)

    def gspec(shape):
        return pl.BlockSpec((None,) + shape, lambda g: (g,) + (0,) * len(shape))

    f32s = lambda *s: jax.ShapeDtypeStruct((G,) + s, F32)
    return pl.pallas_call(
        body, name=name, grid=(G,),
        in_specs=[gspec((C, W)), gspec((C, W)), gspec((C, 128)), gspec((C, 128)), gspec((W, W)), gspec((W, 128)),
                  gspec((W, 128)), gspec((128, W)), gspec((128, W)), gspec((n_steps, 2, 128)), gspec((n_steps, 2, 128))],
        out_specs=[gspec((C, W)), gspec((W, W)), gspec((W, 128)), gspec((W, 128)), gspec((128, W)), gspec((128, W)),
                   gspec((8, 128))],
        out_shape=[jax.ShapeDtypeStruct((G, C, W), BF16), f32s(W, W), f32s(W, 128), f32s(W, 128), f32s(128, W),
                   f32s(128, W), f32s(8, 128)],
        compiler_params=_cparams(),
    )(dyg, ug, h0, h1, mats["mt"].astype(BF16), mats["p0"].astype(BF16), mats["p1"].astype(BF16),
      mats["q0"].astype(BF16), mats["q1"].astype(BF16), apow_conj[0], apow_conj[1])


S5_LANE_GROUPS = 128 // S5_GROUP


def _group_select():
    e = np.zeros((S5_LANE_GROUPS, S5_CHUNK, 128, S5_CHUNK, S5_GROUP), np.float32)
    for j in range(S5_LANE_GROUPS):
        for l in range(S5_CHUNK):
            for hh in range(S5_GROUP):
                e[j, l, S5_GROUP * j + hh, l, hh] = 1.0
    return jnp.asarray(e.reshape(S5_LANE_GROUPS, S5_CHUNK * 128, S5_CHUNK * S5_GROUP), BF16)


def _to_groups(name, a):
    T = a.shape[0]
    C = T // S5_CHUNK
    cb = _tile(C, 256, 8)
    sel = _group_select()

    def body(x_ref, e_ref, o_ref):
        xcat = jnp.concatenate([x_ref[pl.ds(l, cb, stride=S5_CHUNK), :].astype(BF16) for l in range(S5_CHUNK)], axis=1)
        for j in range(S5_LANE_GROUPS):
            o_ref[j] = jnp.dot(xcat, e_ref[j], preferred_element_type=F32).astype(BF16)

    return pl.pallas_call(
        body, name=name, grid=(S5_GROUPS // S5_LANE_GROUPS, C // cb),
        in_specs=[pl.BlockSpec((cb * S5_CHUNK, 128), lambda b, r: (r, b)), pl.BlockSpec(sel.shape, lambda b, r: (0, 0, 0))],
        out_specs=pl.BlockSpec((S5_LANE_GROUPS, cb, S5_CHUNK * S5_GROUP), lambda b, r: (b, r, 0)),
        out_shape=jax.ShapeDtypeStruct((S5_GROUPS, C, S5_CHUNK * S5_GROUP), BF16), compiler_params=_cparams(),
    )(a, sel)


def _from_groups(name, a, extras, fn, n_out):
    G, C, W = a.shape
    T = C * S5_CHUNK
    cb = _tile(C, 256, 8)
    sel = _group_select()
    ne = len(extras)

    def body(*refs):
        y_ref, e_ref = refs[:2]
        e_refs, o_refs = refs[2:2 + ne], refs[2 + ne:]
        ycat = jnp.zeros((cb, S5_CHUNK * 128), F32)
        for j in range(S5_LANE_GROUPS):
            ycat = ycat + lax.dot_general(y_ref[j], e_ref[j], (_DN["nt"], ((), ())), preferred_element_type=F32)
        for l in range(S5_CHUNK):
            rows = pl.ds(l, cb, stride=S5_CHUNK)
            ex = [r[...] if r.shape[0] == 1 else r[rows, :] for r in e_refs]
            for o_ref, o in zip(o_refs, fn(ycat[:, l * 128:(l + 1) * 128], *ex)):
                o_ref[rows, :] = o

    col = pl.BlockSpec((cb * S5_CHUNK, 128), lambda b, r: (r, b))
    return pl.pallas_call(
        body, name=name, grid=(G // S5_LANE_GROUPS, C // cb),
        in_specs=[pl.BlockSpec((S5_LANE_GROUPS, cb, W), lambda b, r: (b, r, 0)),
                  pl.BlockSpec(sel.shape, lambda b, r: (0, 0, 0))]
        + [pl.BlockSpec((1, 128), lambda b, r: (0, b)) if e.shape[0] == 1 else col for e in extras],
        out_specs=[col] * n_out,
        out_shape=[jax.ShapeDtypeStruct((T, G * S5_GROUP), F32)] * n_out, compiler_params=_cparams(),
    )(a, sel, *extras)


_S5_SSM = ("lam_re", "lam_im", "log_step", "b_re", "b_im", "c_re", "c_im")


def _s5_prepare(W, T):
    sp = tuple(W["s5_" + k] for k in _S5_SSM)
    mats, mats_vjp = jax.vjp(jax.vmap(_s5_matrices), *sp)
    n_steps = max(1, int(math.log2(T // S5_CHUNK)))
    apow = jax.vmap(lambda a, b, c: _s5_apow(a, b, c, n_steps, False))(*sp[:3])
    apow_c = jax.vmap(lambda a, b, c: _s5_apow(a, b, c, n_steps, True))(*sp[:3])
    return mats, mats_vjp, apow, apow_c


def _s5_fwd(l, h, w, small, g, b, mats, apow, apow_c):
    u = _mm_simple(f"s5{l}_in", h, w["w_in"], "nn", F32)
    ug = _to_groups(f"s5{l}_togroups", u)
    yg, h0, h1 = _s5_scan_fwd(f"s5{l}_scan", ug, mats, apow)
    d_row, bglu_row = small["d"].reshape(1, -1), small["b_glu"].reshape(1, -1)
    yssm, y1 = _from_groups(f"s5{l}_fromgroups", yg, [u, d_row], lambda y, u_t, d_t: [y, _gelu(y + d_t * u_t)], 2)

    def epi(accs, y1_t, bg_t):
        gate = _sigmoid(accs[0] + bg_t)
        return [y1_t * gate, gate]

    y2, gate = _mm_simple(f"s5{l}_glu", y1, w["w_glu"], "nn", None, extras=(y1, bglu_row), epilogue=epi, n_out=2,
                          out_dtypes=[BF16, F32])
    z, h_out = _out_proj_ln(f"s5{l}_out", y2, w["w_out"], h, g, b)
    return h_out, (h, u, ug, yssm, y1, y2, gate, z, h0, h1, mats, apow_c, d_row)


def _s5_bwd(l, dh_out, saved, w, small, g):
    h, u, ug, yssm, y1, y2, gate, z, h0, h1, mats, apow_c, d_row = saved
    T = h.shape[0]
    dz, dg, db = _ln_bwd(f"s5{l}_lnb", dh_out, z, g)

    def epi1(accs, y1_t, gate_t):
        dy2 = accs[0]
        dpre = dy2 * y1_t * gate_t * (1.0 - gate_t)
        return [dpre, dy2 * gate_t, _bcast8(jnp.sum(dpre, axis=0, keepdims=True))]

    dpre, dy1a, pbg = _mm_simple(f"s5{l}_dy2", dz, w["w_out"], "nt", None, extras=(y1, gate), epilogue=epi1, n_out=2,
                                 out_dtypes=[BF16, F32], psum_outs=1)
    dw_out = _mm_simple(f"s5{l}_dWout", y2, dz, "tn", BF16, tm=1024, tn=1024, tk=512)

    def epi2(accs, dy1a_t, yssm_t, u_t, d_t):
        dy1 = accs[0] + dy1a_t
        dy = dy1 * _dgelu(yssm_t + d_t * u_t)
        return [dy, dy * d_t, _bcast8(jnp.sum(dy * u_t, axis=0, keepdims=True))]

    dy, du_skip, pdd = _mm_simple(f"s5{l}_dy1", dpre, w["w_glu"], "nt", None, extras=(dy1a, yssm, u, d_row), epilogue=epi2,
                                  n_out=2, out_dtypes=[F32, F32], psum_outs=1)
    dw_glu = _mm_simple(f"s5{l}_dWglu", y1, dpre, "tn", BF16, tm=1024, tn=1024, tk=512)
    dyg = _to_groups(f"s5{l}_togroups_b", dy)
    dug, dmt, dp0, dp1, dq0, dq1, da = _s5_scan_bwd(f"s5{l}_scanb", dyg, ug, h0, h1, mats, apow_c)
    du = _from_groups(f"s5{l}_fromgroups_b", dug, [du_skip], lambda y, skip: [y + skip], 1)[0]

    def a_grad(p, q):
        return jnp.concatenate([p[:, :S5_STATE] + p[:, S5_STATE:], q[:, S5_STATE:] - q[:, :S5_STATE]], axis=-1)

    dmats = dict(mt=dmt, p0=dp0, p1=dp1, q0=dq0, q1=dq1, a0=a_grad(da[:, 0], da[:, 1]), a1=a_grad(da[:, 2], da[:, 3]))
    dh = _mm_simple(f"s5{l}_dX", du, w["w_in"], "nt", F32, extras=(dz,), epilogue=lambda accs, dz_t: [accs[0] + ALPHA * dz_t])
    dw_in = _mm_simple(f"s5{l}_dWin", h, du, "tn", BF16, tm=1024, tn=1024, tk=512)
    grads = dict(w_in=dw_in, w_glu=dw_glu, w_out=dw_out, d=_psum_rows(pdd), b_glu=_psum_rows(pbg), ln_g=dg, ln_b=db)
    return dh, grads, dmats


def _gla_levels(lc):
    ms, m = [], lc // 2
    while m >= 1:
        ms.append(m)
        m //= 2
    return ms


def _gla_scan_matrix(lc, rev):
    r = np.arange(lc)[:, None]
    t = np.arange(lc)[None, :]
    blocks = []
    for m in _gla_levels(lc):
        same = (r // m) == (t // m)
        upper = ((r // m) % 2) == 1
        blocks.append(same & np.where(upper, t >= r, t < r))
    blocks.append(t >= r)
    blocks.append(t < r)
    if rev:
        blocks = [blk[::-1, ::-1] for blk in blocks]
    return np.concatenate(blocks, axis=1).astype(np.float32)


def _gla_gates(z, lb):
    sig = _sigmoid(z)
    ls = jnp.minimum(z, 0.0) - jnp.log(1.0 + jnp.exp(-jnp.abs(z)))
    a = jnp.log(lb)
    bb = jnp.log(1.0 - lb) + ls
    lf = jnp.maximum(a, bb) + jnp.log(1.0 + jnp.exp(-jnp.abs(a - bb)))
    return lf, (1.0 - lb) * (1.0 - sig), sig


def _gla_cumsum(lf, rev):
    b, sh = lf, 1
    while sh < lf.shape[0]:
        b = b + _shift_rows(b, sh, not rev)
        sh *= 2
    return b


def _gla_bref(b, m, rev):
    lc, n = b.shape
    idx = m if rev else m - 1
    if 2 * m >= 8:
        nb = lc // (2 * m)
        b3 = b.reshape(nb, 2 * m, n)
        return jnp.broadcast_to(b3[:, idx:idx + 1, :], (nb, 2 * m, n)).reshape(lc, n)
    row = lax.broadcasted_iota(jnp.int32, b.shape, 0)
    j = row & (2 * m - 1)
    out = b
    for jj in range(2 * m):
        if jj != idx:
            out = jnp.where(j == jj, pltpu.roll(b, (jj - idx) % lc, 0), out)
    return out


def _gla_masks(lc, rev):
    r = np.arange(lc)
    mq, bm = [], [np.eye(lc)]
    for m in _gla_levels(lc):
        isq = ((r // m) % 2) == (0 if rev else 1)
        mq.append(np.broadcast_to(isq[:, None], (lc, HG_DIM)))
        bm.append((r[:, None] // (2 * m)) == (r[None, :] // (2 * m)))
    return jnp.asarray(np.stack(mq), F32), jnp.asarray(np.stack(bm), F32)


def _gla_chunk(q, k, lf, rev, mq_ref, bm_ref):
    lc = q.shape[0]
    nt = (_DN["nt"], ((), ()))
    b = _gla_cumsum(lf, rev)
    qb, kb = q.astype(BF16), k.astype(BF16)
    sc = bm_ref[0] * lax.dot_general(qb, kb, nt, preferred_element_type=F32)
    levels = []
    for i, m in enumerate(_gla_levels(lc)):
        mq = mq_ref[i]
        mk = 1.0 - mq
        w = jnp.exp((b - _gla_bref(b, m, rev)) * (mq - mk))
        wq, wk = w * mq, w * mk
        xf, yf = q * wq, k * wk
        xb, yb = xf.astype(BF16), yf.astype(BF16)
        sc = sc + bm_ref[i + 1] * lax.dot_general(xb, yb, nt, preferred_element_type=F32)
        levels.append((wq, wk, xf, yf, xb, yb))
    return b, sc, levels


def _hg_specs(T, lc, rev, backward):
    nc = T // lc
    cc = (lambda c: nc - 1 - c) if rev != backward else (lambda c: c)
    zcol = HG_HEADS * (2 if rev else 1)
    q_spec = pl.BlockSpec((lc, HG_DIM), lambda h, c: (cc(c), h))
    z_spec = pl.BlockSpec((lc, HG_DIM), lambda h, c: (cc(c), zcol + h))
    v_spec = pl.BlockSpec((lc, HG_DIM), lambda h, c: (cc(c), 3 * HG_HEADS + h))
    lb_spec = pl.BlockSpec((1, HG_DIM), lambda h, c: (0, h))
    st_spec = pl.BlockSpec((None, None, HG_DIM, HG_DIM), lambda h, c: (h, cc(c), 0, 0))
    return nc, q_spec, z_spec, v_spec, lb_spec, st_spec


def _gla_fwd(name, proj, lb_row, rev):
    T = proj.shape[0]
    lc = _tile(T, HG_CHUNK, 8)
    nc, q_spec, z_spec, v_spec, lb_spec, st_spec = _hg_specs(T, lc, rev, False)
    last = 0 if rev else lc - 1
    mq, bm = _gla_masks(lc, rev)
    const3 = lambda a: pl.BlockSpec(a.shape, lambda h, c: (0, 0, 0))

    def body(q_ref, z_ref, v_ref, lb_ref, mq_ref, bm_ref, o_ref, st_ref, st_s):
        @pl.when(pl.program_id(1) == 0)
        def _():
            st_s[...] = jnp.zeros(st_s.shape, F32)

        q = _silu(q_ref[...])
        lf, k, _ = _gla_gates(z_ref[...], lb_ref[...])
        vb = v_ref[...].astype(BF16)
        b, sc, _ = _gla_chunk(q, k, lf, rev, mq_ref, bm_ref)
        st0 = st_s[...]
        st_ref[...] = st0
        bl = b[last:last + 1, :]
        o = jnp.dot(sc.astype(BF16), vb, preferred_element_type=F32)
        o += lax.dot_general((q * jnp.exp(b)).astype(BF16), st0.astype(BF16), (_DN["nt"], ((), ())), preferred_element_type=F32)
        o_ref[...] = o
        kd = (k * jnp.exp(bl - b)).astype(BF16)
        st_s[...] = st0 * jnp.exp(bl) + lax.dot_general(vb, kd, (_DN["tn"], ((), ())), preferred_element_type=F32)

    return pl.pallas_call(
        body, name=name, grid=(HG_HEADS, nc),
        in_specs=[q_spec, z_spec, v_spec, lb_spec, const3(mq), const3(bm)], out_specs=[q_spec, st_spec],
        out_shape=[jax.ShapeDtypeStruct((T, D_MODEL), F32), jax.ShapeDtypeStruct((HG_HEADS, nc, HG_DIM, HG_DIM), F32)],
        scratch_shapes=[pltpu.VMEM((HG_DIM, HG_DIM), F32)], compiler_params=_cparams(),
    )(proj, proj, proj, lb_row, mq, bm)


def _gla_bwd(name, proj, lb_row, do, st, rev):
    T = proj.shape[0]
    lc = _tile(T, HG_CHUNK, 8)
    nc, q_spec, z_spec, v_spec, lb_spec, st_spec = _hg_specs(T, lc, rev, True)
    wall = jnp.asarray(_gla_scan_matrix(lc, rev), BF16)
    last = 0 if rev else lc - 1
    mq, bm = _gla_masks(lc, rev)
    const3 = lambda a: pl.BlockSpec(a.shape, lambda h, c: (0, 0, 0))

    def body(q_ref, z_ref, v_ref, lb_ref, do_ref, st_ref, wall_ref, mq_ref, bm_ref, dq_ref, dz_ref, dv_ref, dlb_ref, dst_s):
        first = pl.program_id(1) == 0

        @pl.when(first)
        def _():
            dst_s[...] = jnp.zeros(dst_s.shape, F32)
            dlb_ref[...] = jnp.zeros(dlb_ref.shape, F32)

        nn, nt, tn = (_DN["nn"], ((), ())), (_DN["nt"], ((), ())), (_DN["tn"], ((), ()))
        dot = functools.partial(lax.dot_general, preferred_element_type=F32)
        qr, z, lb = q_ref[...], z_ref[...], lb_ref[...]
        q = _silu(qr)
        lf, k, sig = _gla_gates(z, lb)
        vb = v_ref[...].astype(BF16)
        b, sc, levels = _gla_chunk(q, k, lf, rev, mq_ref, bm_ref)
        st0, dst = st_ref[...], dst_s[...]
        st0b, dstb = st0.astype(BF16), dst.astype(BF16)
        dob = do_ref[...].astype(BF16)
        bl = b[last:last + 1, :]
        eb, ebl, ekd = jnp.exp(b), jnp.exp(bl), jnp.exp(bl - b)
        qe, kd = q * eb, k * ekd
        kdb = kd.astype(BF16)
        dsc = dot(dob, vb, nt)
        dv_ref[...] = dot(sc.astype(BF16), dob, tn) + dot(kdb, dstb, nt)
        dqe = dot(dob, st0b, nn)
        dkd = dot(vb, dstb, nn)
        dq = dqe * eb
        dk = dkd * ekd
        zs = []
        dsd = (bm_ref[0] * dsc).astype(BF16)
        dq += dot(dsd, k.astype(BF16), nn)
        dk += dot(dsd, q.astype(BF16), tn)
        for i, (wq, wk, xf, yf, xb, yb) in enumerate(levels):
            dsl = (bm_ref[i + 1] * dsc).astype(BF16)
            dx = dot(dsl, yb, nn)
            dy = dot(dsl, xb, tn)
            dq += dx * wq
            dk += dy * wk
            zs.append((dx * xf + dy * yf).astype(BF16))
        zs.append((dqe * qe).astype(BF16))
        zs.append((dkd * kd).astype(BF16))
        zl = jnp.sum(dst * st0, axis=0, keepdims=True) * ebl
        dlf = dot(wall_ref[...], jnp.concatenate(zs, axis=0), nn) + zl
        dst_s[...] = dst * ebl + dot(dob, qe.astype(BF16), tn)
        inv_f = jnp.exp(-lf)
        one_sig = 1.0 - sig
        dz_ref[...] = (dlf * inv_f - dk) * (1.0 - lb) * sig * one_sig
        dq_ref[...] = dq * _dsilu(qr)
        dlb_ref[...] += _bcast8(jnp.sum((dlf * inv_f - dk) * one_sig, axis=0, keepdims=True))

    big = jax.ShapeDtypeStruct((T, D_MODEL), F32)
    return pl.pallas_call(
        body, name=name, grid=(HG_HEADS, nc),
        in_specs=[q_spec, z_spec, v_spec, lb_spec, q_spec, st_spec, pl.BlockSpec(wall.shape, lambda h, c: (0, 0)),
                  const3(mq), const3(bm)],
        out_specs=[q_spec, q_spec, q_spec, pl.BlockSpec((None, 8, HG_DIM), lambda h, c: (h, 0, 0))],
        out_shape=[big, big, big, jax.ShapeDtypeStruct((HG_HEADS, 8, HG_DIM), F32)],
        scratch_shapes=[pltpu.VMEM((HG_DIM, HG_DIM), F32)], compiler_params=_cparams(),
    )(proj, proj, proj, lb_row, do, st, wall, mq, bm)


def _hg_lower_bounds(hg_lower_bound, layer):
    lbs = jax.nn.softmax(hg_lower_bound, axis=0)
    lbs = jnp.cumsum(lbs, axis=0) - lbs[0]
    return lbs[layer].reshape(1, -1)


def _hg_post(o_fw, o_bw, g_raw, ng):
    outs, ons, os_, rs = [], [], [], []
    for hd in range(o_fw.shape[1] // HG_DIM):
        sl = slice(hd * HG_DIM, (hd + 1) * HG_DIM)
        o = o_fw[:, sl] + o_bw[:, sl]
        r = lax.rsqrt(jnp.mean(o * o, axis=-1, keepdims=True) + RMS_EPS)
        on = o * r * ng
        outs.append(on * _silu(g_raw[:, sl]))
        ons.append(on)
        os_.append(o)
        rs.append(r)
    return outs, ons, os_, rs


def _hg_fwd(l, h, w, small, g, b):
    T = h.shape[0]
    tm = _tile(T, 512, 8)
    proj = _mm_simple(f"hg{l}_in", h, w["w_in"], "nn", F32, tn=1280, j_outer=True)
    lb_fn = lambda p: _hg_lower_bounds(p, l)
    lb_row, lb_vjp = jax.vjp(lb_fn, small["lower_bound"])
    o_fw, st_fw = _gla_fwd(f"hg{l}_gla_fw", proj, lb_row, False)
    o_bw, st_bw = _gla_fwd(f"hg{l}_gla_bw", proj, lb_row, True)
    ng = small["norm_g"].reshape(1, HG_DIM)

    def post(of_t, ob_t, g_t, ng_t):
        return [jnp.concatenate(_hg_post(of_t, ob_t, g_t, ng_t)[0], axis=1)]

    og = _rows(f"hg{l}_post", T, tm,
               [_rt(o_fw, tm), _rt(o_bw, tm), (proj, (tm, D_MODEL), lambda i: (i, 4)), _full(ng)],
               [_rt_out(T, D_MODEL, BF16, tm)], post)[0]
    z, h_out = _out_proj_ln(f"hg{l}_out", og, w["w_out"], h, g, b)
    return h_out, (h, proj, lb_row, lb_vjp, st_fw, st_bw, o_fw, o_bw, ng, og, z)


def _hg_bwd(l, dh_out, saved, w, small, g):
    h, proj, lb_row, lb_vjp, st_fw, st_bw, o_fw, o_bw, ng, og, z = saved
    T = h.shape[0]
    dz, dg, db = _ln_bwd(f"hg{l}_lnb", dh_out, z, g)
    tm = _tile(T, 512, 8)
    nn_tiles = D_MODEL // HG_DIM

    def epi(accs, of_t, ob_t, g_t, ng_t):
        dog = accs[0]
        _, ons, os_, rs = _hg_post(of_t, ob_t, g_t, ng_t)
        dos, dgs, dngs = [], [], []
        for hd in range(nn_tiles):
            sl = slice(hd * HG_DIM, (hd + 1) * HG_DIM)
            d, o, r = dog[:, sl], os_[hd], rs[hd]
            dgs.append(d * ons[hd] * _dsilu(g_t[:, sl]))
            don = d * _silu(g_t[:, sl])
            dngs.append(jnp.sum(don * o * r, axis=0, keepdims=True))
            dxn = don * ng_t
            dos.append(r * dxn - o * (r * r * r) * jnp.mean(dxn * o, axis=-1, keepdims=True))
        return [jnp.concatenate(dos, axis=1), jnp.concatenate(dgs, axis=1), _bcast8(jnp.concatenate(dngs, axis=1))]

    grid = (T // tm, 1, 1)
    row_map = lambda i, j, k: (i, 0)
    do, dg_raw, png = _mm(
        f"hg{l}_dog", grid, [(dz, (tm, D_MODEL), row_map)], [(w["w_out"], (D_MODEL, D_MODEL), lambda i, j, k: (0, 0))],
        [(0, 0, 0, "nt")], 1, (tm, D_MODEL),
        [(o_fw, (tm, D_MODEL), row_map), (o_bw, (tm, D_MODEL), row_map), (proj, (tm, D_MODEL), lambda i, j, k: (i, 4)),
         (ng, (1, HG_DIM), lambda i, j, k: (0, 0))],
        [((T, D_MODEL), F32, (tm, D_MODEL), row_map), ((T, D_MODEL), F32, (tm, D_MODEL), row_map),
         ((T // tm * 8, D_MODEL), F32, (8, D_MODEL), row_map)], epi)
    dw_out = _mm_simple(f"hg{l}_dWout", og, dz, "tn", BF16, tm=1024, tn=1024, tk=512)
    dq_f, dz_f, dv_f, dlb_f = _gla_bwd(f"hg{l}_glab_fw", proj, lb_row, do, st_fw, False)
    dq_b, dz_b, dv_b, dlb_b = _gla_bwd(f"hg{l}_glab_bw", proj, lb_row, do, st_bw, True)
    dproj = jnp.concatenate([dq_f + dq_b, dz_f, dz_b, dv_f + dv_b, dg_raw], axis=1).astype(BF16)
    dh = _mm_simple(f"hg{l}_dX", dproj, w["w_in"], "nt", F32, tm=256, tk=5 * D_MODEL, extras=(dz,),
                    epilogue=lambda accs, dz_t: [accs[0] + ALPHA * dz_t])
    dw_in = _mm_simple(f"hg{l}_dWin", h, dproj, "tn", BF16, tm=1024, tn=1280, tk=512)
    dlb_row = (dlb_f[:, 0, :] + dlb_b[:, 0, :]).reshape(1, D_MODEL)
    grads = dict(w_in=dw_in, w_out=dw_out, lower_bound=lb_vjp(dlb_row)[0],
                 norm_g=_psum_rows(png).reshape(nn_tiles, HG_DIM).sum(axis=0), ln_g=dg, ln_b=db)
    return dh, grads


MLA_W = 256
MLA_SCALE = (MLA_NOPE + MLA_ROPE) ** -0.5


def _swap_halves(a):
    n = a.shape[-1] // 2
    return jnp.concatenate([a[..., n:], a[..., :n]], axis=-1)


def _mla_ext_weights(w_in, w_q_b):
    w_in_ext = jnp.concatenate([w_in, _swap_halves(w_in[:, MLA_Q_LORA + MLA_KV_LORA:])], axis=1)
    wq = w_q_b.reshape(MLA_Q_LORA, MLA_HEADS, MLA_NOPE + MLA_ROPE)
    wq_ext = jnp.concatenate([wq, _swap_halves(wq[:, :, MLA_NOPE:])], axis=2).reshape(MLA_Q_LORA, MLA_HEADS * MLA_W)
    return w_in_ext, wq_ext


def _mla_ext_grads(dw_in_ext, dwq_ext):
    n_lat = MLA_Q_LORA + MLA_KV_LORA
    dw_in = jnp.concatenate([dw_in_ext[:, :n_lat], dw_in_ext[:, n_lat:n_lat + MLA_ROPE]
                             + _swap_halves(dw_in_ext[:, n_lat + MLA_ROPE:])], axis=1)
    dq = dwq_ext.reshape(MLA_Q_LORA, MLA_HEADS, MLA_W)
    dwq = jnp.concatenate([dq[:, :, :MLA_NOPE], dq[:, :, MLA_NOPE:MLA_NOPE + MLA_ROPE]
                           + _swap_halves(dq[:, :, MLA_NOPE + MLA_ROPE:])], axis=2)
    return dw_in, dwq.reshape(MLA_Q_LORA, MLA_HEADS * (MLA_NOPE + MLA_ROPE))


def _rope_table(positions):
    half = MLA_ROPE // 2
    inv_freq = 1.0 / (ROPE_THETA ** (jnp.arange(half, dtype=F32) * (2.0 / MLA_ROPE)))
    ang = positions.astype(F32)[:, None] * inv_freq
    cos, sin = jnp.cos(ang), jnp.sin(ang)
    return jnp.concatenate([cos, cos, -sin, sin], axis=1)


def _rope_sum(prod):
    return prod + pltpu.roll(prod, MLA_ROPE, 1)


def _low_half(a):
    lane = lax.broadcasted_iota(jnp.int32, a.shape, 1)
    return jnp.where(lane < MLA_ROPE, a, 0.0)


def _rms(x, g):
    r = lax.rsqrt(jnp.mean(x * x, axis=-1, keepdims=True) + RMS_EPS)
    return x * r * g


def _rms_bwd(x, g, dy):
    r = lax.rsqrt(jnp.mean(x * x, axis=-1, keepdims=True) + RMS_EPS)
    dxn = dy * g
    dx = r * dxn - x * (r * r * r) * jnp.mean(dxn * x, axis=-1, keepdims=True)
    return dx, jnp.sum(dy * x * r, axis=0, keepdims=True)


ATT_TILE = 512
ATT_BLOCK = 2048
ATT_KEYS_BWD = 1024
ATT_QUERIES_FWD = 2048


def _lanes(col, n):
    return jnp.tile(col, (1, n // 128))


def _flash_fwd(name, q, k, v):
    H, T, _ = q.shape
    tq, tkb = _tile(T, ATT_QUERIES_FWD, 8), _tile(T, ATT_BLOCK, 128)
    ts = _tile(tkb, ATT_TILE, 128)
    nk, nsub = T // tkb, tkb // ts

    def body(q_ref, k_ref, v_ref, o_ref, lse_ref, m_s, acc_s):
        ki = pl.program_id(2)

        @pl.when(ki == 0)
        def _():
            m_s[...] = jnp.full(m_s.shape, -jnp.inf, F32)
            acc_s[...] = jnp.zeros(acc_s.shape, F32)

        qv = q_ref[...]
        m, acc = m_s[...], acc_s[...]
        for j in range(nsub):
            kj, vj = k_ref[j * ts:(j + 1) * ts, :], v_ref[j * ts:(j + 1) * ts, :]
            s = lax.dot_general(qv, kj, (_DN["nt"], ((), ())), preferred_element_type=F32)
            m_new = jnp.maximum(m, jnp.max(s, axis=-1, keepdims=True))
            p = jnp.exp(s - _lanes(m_new, ts)).astype(BF16)
            acc = _lanes(jnp.exp(m - m_new), 2 * MLA_V) * acc + jnp.dot(p, vj, preferred_element_type=F32)
            m = m_new
        m_s[...], acc_s[...] = m, acc

        @pl.when(ki == nk - 1)
        def _():
            l = acc[:, MLA_V:]
            o_ref[...] = acc[:, :MLA_V] / l
            lse_ref[...] = m + jnp.log(l)

    return pl.pallas_call(
        body, name=name, grid=(H, T // tq, nk),
        in_specs=[pl.BlockSpec((None, tq, MLA_W), lambda h, i, j: (h, i, 0)),
                  pl.BlockSpec((None, tkb, MLA_W), lambda h, i, j: (h, j, 0)),
                  pl.BlockSpec((None, tkb, 2 * MLA_V), lambda h, i, j: (h, j, 0))],
        out_specs=[pl.BlockSpec((tq, MLA_V), lambda h, i, j: (i, h)), pl.BlockSpec((tq, MLA_V), lambda h, i, j: (i, h))],
        out_shape=[jax.ShapeDtypeStruct((T, H * MLA_V), F32), jax.ShapeDtypeStruct((T, H * MLA_V), F32)],
        scratch_shapes=[pltpu.VMEM((tq, MLA_V), F32), pltpu.VMEM((tq, 2 * MLA_V), F32)],
        compiler_params=_cparams(),
    )(q, k, v)


def _flash_bwd(name, q, k, v, do, lse, delta):
    H, T, _ = q.shape
    nt, tn = (_DN["nt"], ((), ())), (_DN["tn"], ((), ()))
    tk, tqb = _tile(T, ATT_KEYS_BWD, 128), _tile(T, ATT_BLOCK, 128)
    tqs = _tile(tqb, ATT_TILE, 128)
    nk, nqb, nsub = T // tk, T // tqb, tqb // tqs

    def body(q_ref, k_ref, v_ref, do_ref, lse_ref, dl_ref, dq_ref, dk_ref, dv_ref, dk_s, dv_s):
        ki, qb = pl.program_id(1), pl.program_id(2)

        @pl.when(jnp.logical_and(ki == 0, qb == 0))
        def _():
            dq_ref[...] = jnp.zeros(dq_ref.shape, F32)

        @pl.when(qb == 0)
        def _():
            dk_s[...] = jnp.zeros(dk_s.shape, F32)
            dv_s[...] = jnp.zeros(dv_s.shape, F32)

        kv, vv = k_ref[...], v_ref[...]
        dk, dv = dk_s[...], dv_s[...]
        for j in range(nsub):
            sl = slice(j * tqs, (j + 1) * tqs)
            qj, doj = q_ref[sl, :], do_ref[sl, :]
            s = lax.dot_general(qj, kv, nt, preferred_element_type=F32)
            dp = lax.dot_general(doj, vv, nt, preferred_element_type=F32)
            p = jnp.exp(s - _lanes(lse_ref[sl, :], tk))
            ds = (p * (dp - _lanes(dl_ref[sl, :], tk))).astype(BF16)
            dv = dv + lax.dot_general(p.astype(BF16), doj, tn, preferred_element_type=F32)
            dk = dk + lax.dot_general(ds, qj, tn, preferred_element_type=F32)
            rows = pl.ds(pl.multiple_of(qb * tqb + j * tqs, tqs), tqs)
            dq_ref[rows, :] += jnp.dot(ds, kv, preferred_element_type=F32)
        dk_s[...], dv_s[...] = dk, dv

        @pl.when(qb == nqb - 1)
        def _():
            dk_ref[...] = dk
            dv_ref[...] = dv

    return pl.pallas_call(
        body, name=name, grid=(H, nk, nqb),
        in_specs=[pl.BlockSpec((None, tqb, MLA_W), lambda h, i, j: (h, j, 0)),
                  pl.BlockSpec((None, tk, MLA_W), lambda h, i, j: (h, i, 0)),
                  pl.BlockSpec((None, tk, MLA_V), lambda h, i, j: (h, i, 0)),
                  pl.BlockSpec((tqb, MLA_V), lambda h, i, j: (j, h)),
                  pl.BlockSpec((tqb, MLA_V), lambda h, i, j: (j, h)),
                  pl.BlockSpec((tqb, MLA_V), lambda h, i, j: (j, h))],
        out_specs=[pl.BlockSpec((None, T, MLA_W), lambda h, i, j: (h, 0, 0)),
                   pl.BlockSpec((None, tk, MLA_W), lambda h, i, j: (h, i, 0)),
                   pl.BlockSpec((None, tk, MLA_V), lambda h, i, j: (h, i, 0))],
        out_shape=[jax.ShapeDtypeStruct((H, T, MLA_W), F32), jax.ShapeDtypeStruct((H, T, MLA_W), F32),
                   jax.ShapeDtypeStruct((H, T, MLA_V), F32)],
        scratch_shapes=[pltpu.VMEM((tk, MLA_W), F32), pltpu.VMEM((tk, MLA_V), F32)], compiler_params=_cparams(),
    )(q, k, v, do, lse, delta)


def _mla_fwd(l, h, w, small, g, b, cs):
    T = h.shape[0]
    tm = _tile(T, 512, 8)
    H = MLA_HEADS
    gq, gkv = small["q_norm_g"].reshape(1, -1), small["kv_norm_g"].reshape(1, -1)
    n_ext = MLA_Q_LORA + MLA_KV_LORA + 2 * MLA_ROPE
    row = lambda i, j, k: (i, 0)
    fix = lambda i, j, k: (0, 0)

    def epi_lat(accs, gq_t, gkv_t):
        a = accs[0]
        ql, kvl = a[:, :MLA_Q_LORA], a[:, MLA_Q_LORA:MLA_Q_LORA + MLA_KV_LORA]
        return [ql, kvl, a[:, MLA_Q_LORA + MLA_KV_LORA:], _rms(ql, gq_t), _rms(kvl, gkv_t)]

    ql, kvl, kr, xq, xkv = _mm(
        f"mla{l}_in", (T // tm, 1, 1), [(h, (tm, D_MODEL), row)], [(w["w_in_ext"], (D_MODEL, n_ext), fix)],
        [(0, 0, 0, "nn")], 1, (tm, n_ext), [(gq, gq.shape, fix), (gkv, gkv.shape, fix)],
        [((T, MLA_Q_LORA), F32, (tm, MLA_Q_LORA), row), ((T, MLA_KV_LORA), F32, (tm, MLA_KV_LORA), row),
         ((T, 2 * MLA_ROPE), F32, (tm, 2 * MLA_ROPE), row), ((T, MLA_Q_LORA), BF16, (tm, MLA_Q_LORA), row),
         ((T, MLA_KV_LORA), BF16, (tm, MLA_KV_LORA), row)], epi_lat)

    def epi_q(accs, cs_t):
        a = accs[0]
        return [jnp.concatenate([a[:, :MLA_NOPE], _rope_sum(a[:, MLA_NOPE:] * cs_t)], axis=1) * MLA_SCALE]

    head_out = lambda i, j, k: (j, i, 0)
    q = _mm(f"mla{l}_q", (T // tm, H, 1), [(xq, (tm, MLA_Q_LORA), row)],
            [(w["wq_ext"], (MLA_Q_LORA, MLA_W), lambda i, j, k: (0, j))], [(0, 0, 0, "nn")], 1, (tm, MLA_W),
            [(cs, (tm, 2 * MLA_ROPE), row)], [((H, T, MLA_W), BF16, (None, tm, MLA_W), head_out)], epi_q)[0]

    def epi_kv(accs, kr_t, cs_t):
        a = accs[0]
        v_t = a[:, MLA_NOPE:]
        return [jnp.concatenate([a[:, :MLA_NOPE], _low_half(_rope_sum(kr_t * cs_t))], axis=1),
                jnp.concatenate([v_t, jnp.ones_like(v_t)], axis=1)]

    k, v = _mm(f"mla{l}_kv", (T // tm, H, 1), [(xkv, (tm, MLA_KV_LORA), row)],
               [(w["w_kv_b"], (MLA_KV_LORA, MLA_W), lambda i, j, k: (0, j))], [(0, 0, 0, "nn")], 1, (tm, MLA_W),
               [(kr, (tm, 2 * MLA_ROPE), row), (cs, (tm, 2 * MLA_ROPE), row)],
               [((H, T, MLA_W), BF16, (None, tm, MLA_W), head_out), ((H, T, 2 * MLA_V), BF16, (None, tm, 2 * MLA_V), head_out)],
               epi_kv)
    o, lse = _flash_fwd(f"mla{l}_attn", q, k, v)
    z, h_out = _out_proj_ln(f"mla{l}_out", o, w["w_out"], h, g, b)
    return h_out, (h, ql, kvl, xq, xkv, q, k, v, o, lse, z, gq, gkv, cs)


def _mla_bwd(l, dh_out, saved, w, g):
    h, ql, kvl, xq, xkv, q, k, v, o, lse, z, gq, gkv, cs = saved
    T = h.shape[0]
    tm = _tile(T, 512, 8)
    H = MLA_HEADS
    dz, dg, db = _ln_bwd(f"mla{l}_lnb", dh_out, z, g)
    def epi_do(accs, o_t):
        d = accs[0]
        deltas = [jnp.broadcast_to(jnp.sum(d[:, hd * MLA_V:(hd + 1) * MLA_V] * o_t[:, hd * MLA_V:(hd + 1) * MLA_V],
                                           axis=-1, keepdims=True), (d.shape[0], MLA_V)) for hd in range(H)]
        return [d, jnp.concatenate(deltas, axis=1)]

    do, delta = _mm_simple(f"mla{l}_dO", dz, w["w_out"], "nt", None, extras=(o,), epilogue=epi_do, n_out=2,
                           out_dtypes=[BF16, F32])
    dw_out = _mm_simple(f"mla{l}_dWout", o, dz, "tn", BF16, tm=1024, tn=1024, tk=512)
    dq, dk, dv = _flash_bwd(f"mla{l}_attnb", q, k, v, do, lse, delta)
    head_in = lambda i, hh: (hh, i, 0)
    row2 = lambda i, hh: (i, 0)

    def fn_q(dq_t, cs_t):
        d = dq_t[:, MLA_NOPE:]
        return [jnp.concatenate([dq_t[:, :MLA_NOPE], _rope_sum(d) * cs_t], axis=1) * MLA_SCALE]

    dq_eff = _grid_call(f"mla{l}_dqeff", (T // tm, H), [(dq, (None, tm, MLA_W), head_in), (cs, (tm, 2 * MLA_ROPE), row2)],
                        [((T, H * MLA_W), BF16, (tm, MLA_W), lambda i, hh: (i, hh))], fn_q)[0]

    def fn_kv(dk_t, dv_t, cs_t):
        return [jnp.concatenate([dk_t[:, :MLA_NOPE], dv_t], axis=1), _rope_sum(_low_half(dk_t[:, MLA_NOPE:])) * cs_t]

    dkv_eff, dkr = _grid_call(
        f"mla{l}_dkveff", (T // tm, H),
        [(dk, (None, tm, MLA_W), head_in), (dv, (None, tm, MLA_V), head_in), (cs, (tm, 2 * MLA_ROPE), row2)],
        [((T, H * MLA_W), BF16, (tm, MLA_W), lambda i, hh: (i, hh)), ((T, 2 * MLA_ROPE), F32, (tm, 2 * MLA_ROPE), row2)],
        fn_kv, acc_outs=(1,))
    dxq = _mm_simple(f"mla{l}_dxq", dq_eff, w["wq_ext"], "nt", F32, tn=MLA_Q_LORA)
    dwq_ext = _mm_simple(f"mla{l}_dWq", xq, dq_eff, "tn", BF16, tm=MLA_Q_LORA, tn=1024, tk=512)
    dxkv = _mm_simple(f"mla{l}_dxkv", dkv_eff, w["w_kv_b"], "nt", F32, tn=MLA_KV_LORA)
    dwkv = _mm_simple(f"mla{l}_dWkv", xkv, dkv_eff, "tn", BF16, tm=MLA_KV_LORA, tn=1024, tk=512)

    def fn_lat(ql_t, dxq_t, gq_t, kvl_t, dxkv_t, gkv_t, dkr_t):
        dql, dgq = _rms_bwd(ql_t, gq_t, dxq_t)
        dkvl, dgkv = _rms_bwd(kvl_t, gkv_t, dxkv_t)
        return [jnp.concatenate([dql, dkvl, dkr_t], axis=1), _bcast8(dgq), _bcast8(dgkv)]

    n_ext = MLA_Q_LORA + MLA_KV_LORA + 2 * MLA_ROPE
    dlat, pgq, pgkv = _rows(f"mla{l}_dlat", T, tm,
                            [_rt(ql, tm), _rt(dxq, tm), _full(gq), _rt(kvl, tm), _rt(dxkv, tm), _full(gkv), _rt(dkr, tm)],
                            [_rt_out(T, n_ext, BF16, tm), _ps_out(T, MLA_Q_LORA, tm), _ps_out(T, MLA_KV_LORA, tm)], fn_lat)
    dh = _mm_simple(f"mla{l}_dX", dlat, w["w_in_ext"], "nt", F32, tk=n_ext, extras=(dz,),
                    epilogue=lambda accs, dz_t: [accs[0] + ALPHA * dz_t])
    dw_in_ext = _mm_simple(f"mla{l}_dWin", h, dlat, "tn", BF16, tm=1024, tn=n_ext, tk=512)
    dw_in, dwq = _mla_ext_grads(dw_in_ext.astype(F32), dwq_ext.astype(F32))
    grads = dict(w_in=dw_in, w_q_b=dwq, w_kv_b=dwkv, w_out=dw_out, q_norm_g=_psum_rows(pgq), kv_norm_g=_psum_rows(pgkv),
                 ln_g=dg, ln_b=db)
    return dh, grads


def _loss_head(y, target):
    T = y.shape[0]
    tm = _tile(T, 512, 8)

    def fn(y_t, t_t):
        d = y_t - t_t
        part = 0.5 * jnp.sum(jnp.mean(d * d, axis=-1, keepdims=True), axis=0, keepdims=True)
        return [d * (1.0 / D_MODEL), jnp.broadcast_to(part, (8, 128))]

    dy, part = _rows("loss_head", T, tm, [_rt(y, tm), _rt(target, tm)],
                     [_rt_out(T, D_MODEL, F32, tm), ((T // tm * 8, 128), F32, (8, 128), lambda i: (i, 0))], fn)
    return jnp.sum(part.reshape(-1, 8, 128)[:, 0, 0]), dy


_S5_VECTORS = ("d", "b_glu")


def _local_step(x, positions, target, W):
    row = lambda a, i: a[i].reshape(1, -1)
    w_in_ext, wq_ext = _mla_ext_weights(W["mla_w_in"][0], W["mla_w_q_b"][0])
    cs = _rope_table(positions)
    s5_mats, s5_mats_vjp, s5_apow, s5_apow_c = _s5_prepare(W, x.shape[0])
    h, saves = x, []
    for l in range(DEPTH):
        kind, slot = LAYER_MIXER[l], l // 3
        g, b = row(W["ln_mix_g"], l), row(W["ln_mix_b"], l)
        if kind == 0:
            w = {k: W["s5_" + k][slot] for k in ("w_in", "w_glu", "w_out")}
            small = {k: W["s5_" + k][slot] for k in _S5_VECTORS}
            h, sv = _s5_fwd(l, h, w, small, g, b, {k: v[slot] for k, v in s5_mats.items()}, s5_apow[slot], s5_apow_c[slot])
        elif kind == 1:
            w = dict(w_in=W["hg_w_in"][slot], w_out=W["hg_w_out"][slot])
            small = dict(lower_bound=W["hg_lower_bound"], norm_g=W["hg_norm_g"][slot])
            h, sv = _hg_fwd(l, h, w, small, g, b)
        else:
            w = dict(w_in_ext=w_in_ext, wq_ext=wq_ext, w_kv_b=W["mla_w_kv_b"][slot], w_out=W["mla_w_out"][slot])
            small = dict(q_norm_g=W["mla_q_norm_g"][slot], kv_norm_g=W["mla_kv_norm_g"][slot])
            h, sv = _mla_fwd(l, h, w, small, g, b, cs)
        h, fsv = _ffn_fwd(l, h, W["ffn_w_in"][l], W["ffn_w_out"][l], row(W["ln_ffn_g"], l), row(W["ln_ffn_b"], l))
        saves.append((w, small, sv, fsv))
    loss, dh = _loss_head(h, target)
    per_layer = [None] * DEPTH
    s5_dmats = {}
    for l in reversed(range(DEPTH)):
        kind = LAYER_MIXER[l]
        w, small, sv, fsv = saves[l]
        dh, gf = _ffn_bwd(l, dh, fsv, W["ffn_w_in"][l], W["ffn_w_out"][l], row(W["ln_ffn_g"], l))
        g = row(W["ln_mix_g"], l)
        if kind == 0:
            dh, gm, s5_dmats[l // 3] = _s5_bwd(l, dh, sv, w, small, g)
        elif kind == 1:
            dh, gm = _hg_bwd(l, dh, sv, w, small, g)
        else:
            dh, gm = _mla_bwd(l, dh, sv, w, g)
        per_layer[l] = (gm, gf)
    grads = {}
    stack = lambda xs: jnp.stack([a.astype(F32) if a.dtype != BF16 else a for a in xs])
    grads["ln_mix_g"] = stack([per_layer[l][0]["ln_g"] for l in range(DEPTH)])
    grads["ln_mix_b"] = stack([per_layer[l][0]["ln_b"] for l in range(DEPTH)])
    grads["ln_ffn_g"] = stack([per_layer[l][1]["ln_g"] for l in range(DEPTH)])
    grads["ln_ffn_b"] = stack([per_layer[l][1]["ln_b"] for l in range(DEPTH)])
    grads["ffn_w_in"] = stack([per_layer[l][1]["w_in"] for l in range(DEPTH)])
    grads["ffn_w_out"] = stack([per_layer[l][1]["w_out"] for l in range(DEPTH)])
    s5_layers = [l for l in range(DEPTH) if LAYER_MIXER[l] == 0]
    for k in ("w_in", "w_glu", "w_out") + _S5_VECTORS:
        grads["s5_" + k] = stack([per_layer[l][0][k] for l in s5_layers])
    d_ssm = s5_mats_vjp({k: jnp.stack([s5_dmats[s][k] for s in range(len(s5_layers))]) for k in s5_mats})
    for k, v in zip(_S5_SSM, d_ssm):
        grads["s5_" + k] = v
    hg = per_layer[1][0]
    grads["hg_w_in"], grads["hg_w_out"] = hg["w_in"][None], hg["w_out"][None]
    grads["hg_lower_bound"], grads["hg_norm_g"] = hg["lower_bound"], hg["norm_g"][None]
    for k, v in per_layer[2][0].items():
        if not k.startswith("ln_"):
            grads["mla_" + k] = v[None]
    return loss, dh, grads


def _here():
    return lax.axis_index("x"), lax.axis_index("y"), lax.axis_index("c")


def _any_spec():
    return pl.BlockSpec(memory_space=pl.ANY)


def _chip_exchange(name, xs, scatter):
    n = len(xs)

    def body(*refs):
        ins, outs = refs[:n], refs[n:2 * n]
        send_sems, recv_sems, loc_sems = refs[2 * n:]
        x, y, c = _here()
        me = 2 * x + y
        peers = [(1 - x, y), (x, 1 - y), (1 - x, 1 - y)]
        copies = []
        for t in range(n):
            src_of = (lambda p, t=t: ins[t].at[p]) if scatter else (lambda p, t=t: ins[t])
            loc = pltpu.make_async_copy(src_of(me), outs[t].at[me], loc_sems.at[t])
            loc.start()
            copies.append(loc)
            for j, (px, py) in enumerate(peers):
                cp = pltpu.make_async_remote_copy(
                    src_ref=src_of(2 * px + py), dst_ref=outs[t].at[me], send_sem=send_sems.at[t, j],
                    recv_sem=recv_sems.at[t, j], device_id=(px, py, c), device_id_type=MESH)
                cp.start()
                copies.append(cp)
        for cp in copies:
            cp.wait()

    out_shape = [jax.ShapeDtypeStruct(a.shape if scatter else (N_CHIPS,) + a.shape, a.dtype) for a in xs]
    return pl.pallas_call(
        body, name=name, in_specs=[_any_spec()] * n, out_specs=[_any_spec()] * n, out_shape=out_shape,
        scratch_shapes=[pltpu.SemaphoreType.DMA((n, 3)), pltpu.SemaphoreType.DMA((n, 3)), pltpu.SemaphoreType.DMA((n,))],
    )(*xs)


def _all_and_core_exchange(name, a, b):
    def body(a_ref, b_ref, oa_ref, ob_ref, send_sems, recv_sems, loc_sems):
        x, y, c = _here()
        me = 4 * x + 2 * y + c
        copies = [pltpu.make_async_copy(a_ref, oa_ref.at[me], loc_sems.at[0]),
                  pltpu.make_async_copy(b_ref, ob_ref.at[c], loc_sems.at[1])]
        for mask in range(1, N_DEV):
            fx, fy, fc = (mask >> 2) & 1, (mask >> 1) & 1, mask & 1
            peer = (1 - x if fx else x, 1 - y if fy else y, 1 - c if fc else c)
            copies.append(pltpu.make_async_remote_copy(
                src_ref=a_ref, dst_ref=oa_ref.at[me], send_sem=send_sems.at[mask - 1], recv_sem=recv_sems.at[mask - 1],
                device_id=peer, device_id_type=MESH))
        copies.append(pltpu.make_async_remote_copy(
            src_ref=b_ref, dst_ref=ob_ref.at[c], send_sem=send_sems.at[N_DEV - 1], recv_sem=recv_sems.at[N_DEV - 1],
            device_id=(x, y, 1 - c), device_id_type=MESH))
        for cp in copies:
            cp.start()
        for cp in copies:
            cp.wait()

    return pl.pallas_call(
        body, name=name, in_specs=[_any_spec(), _any_spec()], out_specs=[_any_spec(), _any_spec()],
        out_shape=[jax.ShapeDtypeStruct((N_DEV,) + a.shape, a.dtype), jax.ShapeDtypeStruct((2,) + b.shape, b.dtype)],
        scratch_shapes=[pltpu.SemaphoreType.DMA((N_DEV,)), pltpu.SemaphoreType.DMA((N_DEV,)), pltpu.SemaphoreType.DMA((2,))],
    )(a, b)


def _sum_leading(name, a, out_dtype=F32):
    n, R, C = a.shape
    tr = _tile(R, 512, 16)

    def fn(a_t):
        s = a_t[0].astype(F32)
        for i in range(1, n):
            s = s + a_t[i].astype(F32)
        return [s]

    return _grid_call(name, (R // tr,), [(a, (n, tr, C), lambda i: (0, i, 0))],
                      [((R, C), out_dtype, (tr, C), lambda i: (i, 0))], fn)[0]


def _adamw(name, g_parts, w, m, v):
    R, C = w.shape
    tr = _tile(R, 256, 8)
    ng = len(g_parts)
    c1 = 1.0 / (1.0 - ADAM_B1 ** ADAM_STEP)
    c2 = 1.0 / (1.0 - ADAM_B2 ** ADAM_STEP)

    def fn(*tiles):
        g = tiles[0].astype(F32)
        for t in tiles[1:ng]:
            g = g + t.astype(F32)
        w_t, m_t, v_t = tiles[ng:]
        m_n = ADAM_B1 * m_t + (1.0 - ADAM_B1) * g
        v_n = ADAM_B2 * v_t + (1.0 - ADAM_B2) * (g * g)
        delta = -ADAM_LR * ((m_n * c1) / (jnp.sqrt(v_n * c2) + ADAM_EPS) + ADAM_WD * w_t)
        return [g, delta, m_n, v_n]

    spec = lambda a: (a, (tr, C), lambda i: (i, 0))
    return _grid_call(name, (R // tr,), [spec(a) for a in list(g_parts) + [w, m, v]],
                      [((R, C), F32, (tr, C), lambda i: (i, 0))] * 4, fn)


_WEIGHTS = ("ln_mix_g", "ln_mix_b", "ln_ffn_g", "ln_ffn_b", "ffn_w_in", "ffn_w_out", "s5_w_in", "s5_lam_re", "s5_lam_im",
            "s5_log_step", "s5_b_re", "s5_b_im", "s5_c_re", "s5_c_im", "s5_d", "s5_w_glu", "s5_b_glu", "s5_w_out", "hg_w_in",
            "hg_lower_bound", "hg_norm_g", "hg_w_out", "mla_w_in", "mla_q_norm_g", "mla_w_q_b", "mla_kv_norm_g", "mla_w_kv_b",
            "mla_w_out")
_BIG = {"ffn_w_in": 2, "ffn_w_out": 1, "s5_w_in": 1, "s5_w_glu": 1, "s5_w_out": 1, "hg_w_in": 2, "hg_w_out": 1,
        "mla_w_in": 1, "mla_w_q_b": 2, "mla_w_kv_b": 2, "mla_w_out": 1}
_SMALL_SHARDED = {"s5_d": 1, "s5_b_glu": 1, "mla_q_norm_g": 1, "mla_kv_norm_g": 1}
_REPLICATED = tuple(n for n in _WEIGHTS if n not in _BIG and n not in _SMALL_SHARDED)
LANES = 1024


def _pack(arrs, dtype, row_mult, lead=0):
    rows, segs, r = [], [], 0
    for a in arrs:
        lead_shape = a.shape[:lead]
        flat = a.astype(dtype).reshape(lead_shape + (-1,))
        n = -(-flat.shape[-1] // LANES)
        flat = jnp.pad(flat, [(0, 0)] * lead + [(0, n * LANES - flat.shape[-1])])
        rows.append(flat.reshape(lead_shape + (n, LANES)))
        segs.append((r, n))
        r += n
    pad = -r % row_mult
    if pad:
        rows.append(jnp.zeros(rows[0].shape[:lead] + (pad, LANES), dtype))
    return jnp.concatenate(rows, axis=lead), segs


def _unpack(packed, segs, shapes):
    out = []
    for (r0, n), shp in zip(segs, shapes):
        size = int(np.prod(shp))
        out.append(packed[..., r0:r0 + n, :].reshape(packed.shape[:-2] + (n * LANES,))[..., :size].reshape(packed.shape[:-2] + tuple(shp)))
    return out


def _unshard(stacked, axis):
    moved = jnp.moveaxis(stacked, 0, axis)
    shp = list(moved.shape)
    return moved.reshape(shp[:axis] + [shp[axis] * shp[axis + 1]] + shp[axis + 2:])


def _shard_split(full, axis):
    shp = list(full.shape)
    a = full.reshape(shp[:axis] + [N_CHIPS, shp[axis] // N_CHIPS] + shp[axis + 1:])
    return jnp.moveaxis(a, axis, 0)


def _train_step(x, positions, target, w, m, v):
    big, small_sh = list(_BIG), list(_SMALL_SHARDED)
    chip = 2 * lax.axis_index("x") + lax.axis_index("y")

    big_pack, big_segs = _pack([w[n] for n in big], BF16, 16)
    sm_pack, sm_segs = _pack([w[n] for n in small_sh], F32, 8)
    big_all, sm_all = _chip_exchange("gather_weights", [big_pack, sm_pack], scatter=False)
    W = {n: w[n] for n in _REPLICATED}
    for n, s in zip(big, _unpack(big_all, big_segs, [w[n].shape for n in big])):
        W[n] = _unshard(s, _BIG[n])
    for n, s in zip(small_sh, _unpack(sm_all, sm_segs, [w[n].shape for n in small_sh])):
        W[n] = _unshard(s, _SMALL_SHARDED[n])

    loss_local, grad_x, G = _local_step(x, positions, target, W)
    loss = lax.psum(loss_local, ("x", "y", "c"))
    out = {}

    g_pack, _ = _pack([_shard_split(G[n].astype(BF16), _BIG[n]) for n in big], BF16, 16, lead=1)
    recv = _chip_exchange("scatter_grads", [g_pack], scatter=True)[0]
    small = list(_REPLICATED) + small_sh
    s_pack, s_segs = _pack([G[n] for n in small], F32, 16)
    s_all, pair = _all_and_core_exchange("exchange_sums", s_pack, _sum_leading("sum_chips", recv, BF16))
    for n, (r0, nr) in zip(big, big_segs):
        as_rows = lambda a: a.reshape(nr, LANES)
        res = _adamw("adamw_" + n, [pair[0, r0:r0 + nr], pair[1, r0:r0 + nr]], as_rows(w[n]), as_rows(m[n]), as_rows(v[n]))
        out[n] = tuple(r.reshape(w[n].shape) for r in res)

    total = _sum_leading("sum_small", s_all)
    g_small = dict(zip(small, _unpack(total, s_segs, [G[n].shape for n in small])))
    for n in small_sh:
        width = w[n].shape[1]
        g_small[n] = lax.dynamic_slice_in_dim(g_small[n], chip * width, width, axis=1)
    packs = [_pack([d[n] for n in small], F32, 8)[0] for d in (g_small, w, m, v)]
    _, a_segs = _pack([w[n] for n in small], F32, 8)
    res = _adamw("adamw_small", [packs[0]], packs[1], packs[2], packs[3])
    unpacked = [_unpack(r, a_segs, [w[n].shape for n in small]) for r in res]
    for i, n in enumerate(small):
        out[n] = tuple(u[i] for u in unpacked)
    return loss, grad_x, out


def kernel(x, positions, ln_mix_g, ln_mix_b, ln_ffn_g, ln_ffn_b, ffn_w_in, ffn_w_out, s5_w_in, s5_lam_re, s5_lam_im,
           s5_log_step, s5_b_re, s5_b_im, s5_c_re, s5_c_im, s5_d, s5_w_glu, s5_b_glu, s5_w_out, hg_w_in,
           hg_lower_bound, hg_norm_g, hg_w_out, mla_w_in, mla_q_norm_g, mla_w_q_b, mla_kv_norm_g, mla_w_kv_b,
           mla_w_out, loss_target, m_ln_mix_g, m_ln_mix_b, m_ln_ffn_g, m_ln_ffn_b, m_ffn_w_in, m_ffn_w_out, m_s5_w_in,
           m_s5_lam_re, m_s5_lam_im, m_s5_log_step, m_s5_b_re, m_s5_b_im, m_s5_c_re, m_s5_c_im, m_s5_d, m_s5_w_glu,
           m_s5_b_glu, m_s5_w_out, m_hg_w_in, m_hg_lower_bound, m_hg_norm_g, m_hg_w_out, m_mla_w_in, m_mla_q_norm_g,
           m_mla_w_q_b, m_mla_kv_norm_g, m_mla_w_kv_b, m_mla_w_out, v_ln_mix_g, v_ln_mix_b, v_ln_ffn_g, v_ln_ffn_b,
           v_ffn_w_in, v_ffn_w_out, v_s5_w_in, v_s5_lam_re, v_s5_lam_im, v_s5_log_step, v_s5_b_re, v_s5_b_im,
           v_s5_c_re, v_s5_c_im, v_s5_d, v_s5_w_glu, v_s5_b_glu, v_s5_w_out, v_hg_w_in, v_hg_lower_bound, v_hg_norm_g,
           v_hg_w_out, v_mla_w_in, v_mla_q_norm_g, v_mla_w_q_b, v_mla_kv_norm_g, v_mla_w_kv_b, v_mla_w_out):
    args = (ln_mix_g, ln_mix_b, ln_ffn_g, ln_ffn_b, ffn_w_in, ffn_w_out, s5_w_in, s5_lam_re, s5_lam_im,
            s5_log_step, s5_b_re, s5_b_im, s5_c_re, s5_c_im, s5_d, s5_w_glu, s5_b_glu, s5_w_out, hg_w_in,
            hg_lower_bound, hg_norm_g, hg_w_out, mla_w_in, mla_q_norm_g, mla_w_q_b, mla_kv_norm_g, mla_w_kv_b,
            mla_w_out, m_ln_mix_g, m_ln_mix_b, m_ln_ffn_g, m_ln_ffn_b, m_ffn_w_in, m_ffn_w_out,
            m_s5_w_in, m_s5_lam_re, m_s5_lam_im, m_s5_log_step, m_s5_b_re, m_s5_b_im, m_s5_c_re, m_s5_c_im, m_s5_d,
            m_s5_w_glu, m_s5_b_glu, m_s5_w_out, m_hg_w_in, m_hg_lower_bound, m_hg_norm_g, m_hg_w_out, m_mla_w_in,
            m_mla_q_norm_g, m_mla_w_q_b, m_mla_kv_norm_g, m_mla_w_kv_b, m_mla_w_out, v_ln_mix_g, v_ln_mix_b,
            v_ln_ffn_g, v_ln_ffn_b, v_ffn_w_in, v_ffn_w_out, v_s5_w_in, v_s5_lam_re, v_s5_lam_im, v_s5_log_step,
            v_s5_b_re, v_s5_b_im, v_s5_c_re, v_s5_c_im, v_s5_d, v_s5_w_glu, v_s5_b_glu, v_s5_w_out, v_hg_w_in,
            v_hg_lower_bound, v_hg_norm_g, v_hg_w_out, v_mla_w_in, v_mla_q_norm_g, v_mla_w_q_b, v_mla_kv_norm_g,
            v_mla_w_kv_b, v_mla_w_out)
    nw = len(_WEIGHTS)
    w = dict(zip(_WEIGHTS, args[:nw]))
    m = dict(zip(_WEIGHTS, args[nw:2 * nw]))
    v = dict(zip(_WEIGHTS, args[2 * nw:]))
    loss, grad_x, out = _train_step(x[0], positions[0], loss_target[0], w, m, v)
    res = [loss, grad_x[None]]
    for i in range(4):
        res += [out[n][i] for n in _WEIGHTS]
    return tuple(res)
```
